```python
import math
import jax, jax.numpy as jnp
from jax import lax
import numpy as np

D_MODEL = 1024
BATCH = 8
SEQ = 4096
DEPTH = 1

GRID_W = 64
CTX_LEN = 256
N_MOD = 6
EPS = 1e-6

F_GROUPS = 8
F_GROUP_DIM = 128
F_WIDTH = F_GROUPS * F_GROUP_DIM

D_INNER = 2 * D_MODEL
HEAD_DIM = 64
N_HEADS = D_INNER // HEAD_DIM
N_GROUPS = 4
HEADS_PER_GROUP = N_HEADS // N_GROUPS
D_STATE = 128
BC_WIDTH = N_GROUPS * D_STATE
CONV_K = 5
CHUNK = 128

N_EXPERTS = 32
TOP_K = 4
D_FF = D_MODEL
SWIGLU_LIMIT = 7.0
SWIGLU_ALPHA = 1.702
MOE_BLOCK = 128

OFF_X = 0
OFF_BF = OFF_X + D_INNER
OFF_BB = OFF_BF + BC_WIDTH
OFF_DT = OFF_BB + BC_WIDTH
OFF_CF = OFF_DT + 2 * N_HEADS
OFF_CB = OFF_CF + BC_WIDTH
OFF_Z = OFF_CB + BC_WIDTH
OFF_F = OFF_Z + D_INNER
OFF_GF = OFF_F + F_WIDTH
OFF_GS = OFF_GF + D_MODEL
IN_COLS = OFF_GS + D_MODEL
XB_CH = OFF_DT
CONV_CH = XB_CH + 2 * BC_WIDTH

kernel_name = 'hybrid_fnet_ssd_moe_prefix_dit_block'


def rmsnorm(v, w):
    v32 = v.astype(jnp.float32)
    inv = lax.rsqrt(jnp.mean(v32 * v32, axis=-1, keepdims=True) + EPS)
    return (v32 * inv * w.astype(jnp.float32)).astype(v.dtype)


def dwconv_centred(u, w, bias):
    ch = u.shape[-1]
    y = lax.conv_general_dilated(u, w[:, None, :].astype(u.dtype), window_strides=(1,),
                                 padding=[(CONV_K // 2, CONV_K // 2)],
                                 dimension_numbers=('NWC', 'WIO', 'NWC'),
                                 feature_group_count=ch)
    return y + bias


def _decay_matrix(a_cs):
    q = a_cs.shape[-1]
    mask = jnp.tril(jnp.ones((q, q), dtype=bool))
    diff = a_cs[..., :, None] - a_cs[..., None, :]
    return jnp.exp(jnp.where(mask, diff, -jnp.inf))


def _ssd_chunks(x, dt, a, b_in):
    bsz, l = x.shape[:2]
    nc = l // CHUNK
    a_cs = jnp.cumsum((dt * a).reshape(bsz, nc, CHUNK, N_GROUPS, HEADS_PER_GROUP), axis=2)
    xdt = (x.astype(jnp.float32) * dt[..., None]).reshape(bsz, nc, CHUNK, N_GROUPS, HEADS_PER_GROUP, HEAD_DIM)
    bc = b_in.astype(jnp.float32).reshape(bsz, nc, CHUNK, N_GROUPS, D_STATE)
    decay_to_end = jnp.exp(a_cs[:, :, -1:] - a_cs)
    states = jnp.einsum('bcsgn,bcsgh,bcsghp->bcghpn', bc, decay_to_end, xdt)
    chunk_decay = jnp.exp(a_cs[:, :, -1])
    return a_cs, xdt, bc, states, chunk_decay


def _carry_states(states, chunk_decay, h0):
    def step(h, inp):
        s, dec = inp
        return h * dec[..., None, None] + s, h
    h_final, h_prev = lax.scan(step, h0, (jnp.moveaxis(states, 1, 0), jnp.moveaxis(chunk_decay, 1, 0)))
    return jnp.moveaxis(h_prev, 0, 1), h_final


def ssd_final_state(x, dt, a, b_in, h0):
    _, _, _, states, chunk_decay = _ssd_chunks(x, dt, a, b_in)
    return _carry_states(states, chunk_decay, h0)[1]


def ssd_scan(x, dt, a, b_in, c_in, h0):
    bsz, l = x.shape[:2]
    nc = l // CHUNK
    a_cs, xdt, bc, states, chunk_decay = _ssd_chunks(x, dt, a, b_in)
    h_prev, _ = _carry_states(states, chunk_decay, h0)
    cc = c_in.astype(jnp.float32).reshape(bsz, nc, CHUNK, N_GROUPS, D_STATE)
    decay_in = _decay_matrix(jnp.moveaxis(a_cs, 2, -1))
    cb = jnp.einsum('bclgn,bcsgn->bcgls', cc, bc)
    y_diag = jnp.einsum('bcgls,bcghls,bcsghp->bclghp', cb, decay_in, xdt)
    y_off = jnp.einsum('bclgn,bcghpn,bclgh->bclghp', cc, h_prev, jnp.exp(a_cs))
    return (y_diag + y_off).reshape(bsz, l, N_HEADS, HEAD_DIM)


def moe_ffn(h, w_router, b_router, w_gate_up, b_gate_up, w_down, b_down):
    bsz, n, d = h.shape
    tok = h.reshape(-1, d)
    n_tok = tok.shape[0]
    logits = (tok @ w_router + b_router).astype(jnp.float32)
    top_logit, top_idx = lax.top_k(logits, TOP_K)
    top_w = jax.nn.softmax(top_logit, axis=-1)
    flat_e = top_idx.reshape(-1)
    flat_w = top_w.reshape(-1)
    n_assign = flat_e.shape[0]
    order = jnp.argsort(flat_e)
    e_sorted = flat_e[order]
    tok_sorted = (order // TOP_K).astype(jnp.int32)
    counts = jnp.bincount(flat_e, length=N_EXPERTS)
    padded = (counts + MOE_BLOCK - 1) // MOE_BLOCK * MOE_BLOCK
    pad_end = jnp.cumsum(padded)
    pad_start = pad_end - padded
    start = jnp.cumsum(counts) - counts
    dest = pad_start[e_sorted] + jnp.arange(n_assign) - start[e_sorted]
    n_blocks = (n_assign + MOE_BLOCK - 1) // MOE_BLOCK + N_EXPERTS
    rows = n_blocks * MOE_BLOCK
    row_tok = jnp.full((rows,), n_tok, jnp.int32).at[dest].set(tok_sorted)
    row_w = jnp.zeros((rows,), jnp.float32).at[dest].set(flat_w[order])
    blk_e = jnp.minimum(jnp.searchsorted(pad_end, jnp.arange(n_blocks) * MOE_BLOCK, side='right'),
                        N_EXPERTS - 1)
    tok_ext = jnp.concatenate([tok, jnp.zeros((1, d), tok.dtype)], axis=0)
    xb = tok_ext[row_tok].reshape(n_blocks, MOE_BLOCK, d)

    def expert_block(args):
        xe, e = args
        gu = xe @ w_gate_up[e] + b_gate_up[e]
        gate = jnp.minimum(gu[:, :D_FF], SWIGLU_LIMIT)
        up = jnp.clip(gu[:, D_FF:], -SWIGLU_LIMIT, SWIGLU_LIMIT)
        act = (up + 1.0) * gate * jax.nn.sigmoid(SWIGLU_ALPHA * gate)
        return act @ w_down[e] + b_down[e]

    yb = lax.map(expert_block, (xb, blk_e)).reshape(rows, d)
    yb = yb * row_w[:, None].astype(yb.dtype)
    out = jnp.zeros_like(tok_ext).at[row_tok].add(yb)[:n_tok]
    return out.reshape(bsz, n, d)


def setup_inputs(seed: int = 0) -> dict:
    key = jax.random.key(seed)
    ks = jax.random.split(key, 25)
    f32 = jnp.float32
    D = D_MODEL

    def nrm(k, shape, scale):
        return jax.random.normal(k, shape, f32) * scale

    dt0 = jnp.exp(jax.random.uniform(ks[11], (DEPTH, 2, N_HEADS), f32, math.log(1e-3), math.log(1e-1)))
    return {
        'x': nrm(ks[0], (BATCH, SEQ, D), 1.0),
        'c': nrm(ks[1], (BATCH, D), 1.0),
        'ctx': nrm(ks[2], (BATCH, CTX_LEN, D), 1.0),
        'c_ctx': nrm(ks[3], (D,), 1.0),
        'w_mod': nrm(ks[4], (DEPTH, D, N_MOD * D), 0.5 * D ** -0.5),
        'b_mod': nrm(ks[5], (DEPTH, N_MOD * D), 0.02),
        'norm1_w': 1.0 + nrm(ks[6], (DEPTH, D), 0.05),
        'norm2_w': 1.0 + nrm(ks[7], (DEPTH, D), 0.05),
        'w_in': nrm(ks[8], (DEPTH, D, IN_COLS), D ** -0.5),
        'conv_w': nrm(ks[9], (DEPTH, CONV_K, CONV_CH), CONV_K ** -0.5),
        'conv_b': nrm(ks[10], (DEPTH, CONV_CH), 0.02),
        'dt_bias': dt0 + jnp.log(-jnp.expm1(-dt0)),
        'a_log': jnp.log(jax.random.uniform(ks[12], (DEPTH, 2, N_HEADS), f32, 1.0, 16.0)),
        'd_skip': 1.0 + nrm(ks[13], (DEPTH, N_HEADS), 0.1),
        'ssd_norm_w': 1.0 + nrm(ks[14], (DEPTH, D_INNER), 0.05),
        'w_ssd_out': nrm(ks[15], (DEPTH, D_INNER, D), D_INNER ** -0.5),
        'w_four_out': nrm(ks[16], (DEPTH, F_WIDTH, D), F_WIDTH ** -0.5),
        'w_o': nrm(ks[17], (DEPTH, D, D), D ** -0.5),
        'w_router': nrm(ks[18], (DEPTH, D, N_EXPERTS), D ** -0.5),
        'b_router': nrm(ks[19], (DEPTH, N_EXPERTS), 0.01),
        'w_gate_up': nrm(ks[20], (DEPTH, N_EXPERTS, D, 2 * D_FF), D ** -0.5),
        'b_gate_up': nrm(ks[21], (DEPTH, N_EXPERTS, 2 * D_FF), 0.02),
        'w_down': nrm(ks[22], (DEPTH, N_EXPERTS, D_FF, D), D_FF ** -0.5),
        'b_down': nrm(ks[23], (DEPTH, N_EXPERTS, D), 0.02),
        'final_norm_w': 1.0 + nrm(ks[24], (D,), 0.05),
    }


def reference(x, c, ctx, c_ctx, w_mod, b_mod, norm1_w, norm2_w, w_in, conv_w, conv_b, dt_bias,
              a_log, d_skip, ssd_norm_w, w_ssd_out, w_four_out, w_o, w_router, b_router,
              w_gate_up, b_gate_up, w_down, b_down, final_norm_w):
    bsz, n, D = x.shape
    n_ctx = ctx.shape[1]
    h0 = jnp.zeros((bsz, N_GROUPS, HEADS_PER_GROUP, HEAD_DIM, D_STATE), jnp.float32)
    for layer in range(DEPTH):
        mod = jax.nn.silu(c) @ w_mod[layer] + b_mod[layer]
        sh1, sc1, g1, sh2, sc2, g2 = jnp.split(mod[:, None, :], N_MOD, axis=-1)
        mod_c = jax.nn.silu(c_ctx) @ w_mod[layer][:, :2 * D] + b_mod[layer][:2 * D]
        sh1_c, sc1_c = mod_c[:D], mod_c[D:]
        a = -jnp.exp(a_log[layer].astype(jnp.float32))

        hc = rmsnorm(ctx, norm1_w[layer]) * (1.0 + sc1_c) + sh1_c
        pc = hc @ w_in[layer][:, :OFF_CF]
        xb_c = jax.nn.silu(dwconv_centred(pc[..., :XB_CH], conv_w[layer][:, :XB_CH], conv_b[layer][:XB_CH]))
        xc = xb_c[..., :D_INNER].reshape(bsz, n_ctx, N_HEADS, HEAD_DIM)
        bfc = xb_c[..., OFF_BF:OFF_BB].reshape(bsz, n_ctx, N_GROUPS, D_STATE)
        bbc = xb_c[..., OFF_BB:OFF_DT].reshape(bsz, n_ctx, N_GROUPS, D_STATE)
        dtc = jax.nn.softplus(pc[..., OFF_DT:OFF_CF].astype(jnp.float32).reshape(bsz, n_ctx, 2, N_HEADS)
                              + dt_bias[layer].astype(jnp.float32))
        hf_ctx = ssd_final_state(xc, dtc[:, :, 0], a[0], bfc, h0)
        hb_ctx = ssd_final_state(xc[:, ::-1], dtc[:, ::-1, 1], a[1], bbc[:, ::-1], h0)

        h = rmsnorm(x, norm1_w[layer]) * (1.0 + sc1) + sh1
        p = h @ w_in[layer]

        xl = jax.nn.silu(dwconv_centred(p[..., :XB_CH], conv_w[layer][:, :XB_CH], conv_b[layer][:XB_CH]))
        cl = jax.nn.silu(dwconv_centred(p[..., OFF_CF:OFF_Z], conv_w[layer][:, XB_CH:], conv_b[layer][XB_CH:]))
        xs = xl[..., :D_INNER].reshape(bsz, n, N_HEADS, HEAD_DIM)
        bf = xl[..., OFF_BF:OFF_BB].reshape(bsz, n, N_GROUPS, D_STATE)
        bb = xl[..., OFF_BB:OFF_DT].reshape(bsz, n, N_GROUPS, D_STATE)
        cf = cl[..., :BC_WIDTH].reshape(bsz, n, N_GROUPS, D_STATE)
        cb = cl[..., BC_WIDTH:].reshape(bsz, n, N_GROUPS, D_STATE)
        dtl = jax.nn.softplus(p[..., OFF_DT:OFF_CF].astype(jnp.float32).reshape(bsz, n, 2, N_HEADS)
                              + dt_bias[layer].astype(jnp.float32))
        y_f = ssd_scan(xs, dtl[:, :, 0], a[0], bf, cf, hf_ctx)
        y_b = ssd_scan(xs[:, ::-1], dtl[:, ::-1, 1], a[1], bb[:, ::-1], cb[:, ::-1], hb_ctx)[:, ::-1]
        y = y_f + y_b + d_skip[layer].astype(jnp.float32)[:, None] * xs.astype(jnp.float32)
        y = y.reshape(bsz, n, D_INNER).astype(x.dtype)
        z = p[..., OFF_Z:OFF_F]
        y_ssd = rmsnorm(y * jax.nn.silu(z), ssd_norm_w[layer]) @ w_ssd_out[layer]

        u = p[..., OFF_F:OFF_GF].astype(jnp.float32).reshape(bsz, n, F_GROUPS, F_GROUP_DIM)
        u = jnp.fft.fft2(u, axes=(1, 3), norm='ortho').real.reshape(bsz, n, F_WIDTH).astype(x.dtype)
        y_four = u @ w_four_out[layer]

        gate_f = jax.nn.sigmoid(p[..., OFF_GF:OFF_GS])
        gate_s = jax.nn.sigmoid(p[..., OFF_GS:IN_COLS])
        mix = (gate_f * y_four + gate_s * y_ssd) @ w_o[layer]
        x = x + g1 * mix

        h2 = rmsnorm(x, norm2_w[layer]) * (1.0 + sc2) + sh2
        x = x + g2 * moe_ffn(h2, w_router[layer], b_router[layer], w_gate_up[layer], b_gate_up[layer],
                             w_down[layer], b_down[layer])
    return rmsnorm(x, final_norm_w)
```

```python
import functools
import math

import numpy as np
import jax
import jax.numpy as jnp
from jax import lax
from jax.experimental import pallas as pl
from jax.experimental.pallas import tpu as pltpu

F32 = jnp.float32
BF16 = jnp.bfloat16

EPS = 1e-6
GRID_W = 64
N_MOD = 6
F_GROUPS = 8
F_GROUP_DIM = 128
HEAD_DIM = 64
N_HEADS = 32
N_GROUPS = 4
HEADS_PER_GROUP = N_HEADS // N_GROUPS
D_STATE = 128
BC_WIDTH = N_GROUPS * D_STATE
CONV_K = 5
CHUNK = 128
N_EXPERTS = 32
TOP_K = 4
SWIGLU_LIMIT = 7.0
SWIGLU_ALPHA = 1.702

LANES = 128
VMEM_LIMIT_BYTES = 48 * 1024 * 1024
NEG_BIG = -1e30
MOE_TILE = 512
GATHER_ROWS = 1024


def _cparams(*sem):
    return pltpu.CompilerParams(dimension_semantics=sem, vmem_limit_bytes=VMEM_LIMIT_BYTES)


def _sigmoid(v):
    return 1.0 / (1.0 + jnp.exp(-v))


def _silu(v):
    return v * _sigmoid(v)


def _softplus(v):
    return jnp.maximum(v, 0.0) + jnp.log(1.0 + jnp.exp(-jnp.abs(v)))


def _mod_kernel(c_ref, w_ref, b_ref, o_ref):
    s = _silu(c_ref[...]).astype(BF16)
    o_ref[...] = jnp.dot(s, w_ref[...].astype(BF16), preferred_element_type=F32) + b_ref[...]


def _modulation(c_rows, w_mod, b_mod):
    rows, d = c_rows.shape
    n_out = w_mod.shape[1]
    tn = 1024
    return pl.pallas_call(
        _mod_kernel,
        grid=(n_out // tn,),
        in_specs=[pl.BlockSpec((rows, d), lambda j: (0, 0)),
                  pl.BlockSpec((d, tn), lambda j: (0, j)),
                  pl.BlockSpec((1, tn), lambda j: (0, j))],
        out_specs=pl.BlockSpec((rows, tn), lambda j: (0, j)),
        out_shape=jax.ShapeDtypeStruct((rows, n_out), F32),
        compiler_params=_cparams("arbitrary"),
        name="modulation",
    )(c_rows, w_mod, b_mod.reshape(1, n_out))


def _inproj_kernel(x_ref, nw_ref, sc_ref, sh_ref, w_ref, wdt_ref, p_ref, dt_ref, h_scr):
    @pl.when(pl.program_id(1) == 0)
    def _():
        x = x_ref[...]
        inv = lax.rsqrt(jnp.mean(x * x, axis=-1, keepdims=True) + EPS)
        h = (x * inv * nw_ref[...]) * (1.0 + sc_ref[...]) + sh_ref[...]
        hb = h.astype(BF16)
        h_scr[...] = hb
        dt_ref[...] = jnp.dot(hb, wdt_ref[...], preferred_element_type=F32)

    p_ref[...] = jnp.dot(h_scr[...], w_ref[...], preferred_element_type=F32).astype(p_ref.dtype)


def _in_proj(x2d, norm_w, sc, sh, w_main, w_dt, rows_per_mod, tm):
    m, d = x2d.shape
    n = w_main.shape[1]
    tn = 1024
    ndt = w_dt.shape[1]
    return pl.pallas_call(
        _inproj_kernel,
        grid=(m // tm, n // tn),
        in_specs=[pl.BlockSpec((tm, d), lambda i, j: (i, 0)),
                  pl.BlockSpec((1, d), lambda i, j: (0, 0)),
                  pl.BlockSpec((None, 1, d), lambda i, j: ((i * tm) // rows_per_mod, 0, 0)),
                  pl.BlockSpec((None, 1, d), lambda i, j: ((i * tm) // rows_per_mod, 0, 0)),
                  pl.BlockSpec((d, tn), lambda i, j: (0, j)),
                  pl.BlockSpec((d, ndt), lambda i, j: (0, 0))],
        out_specs=[pl.BlockSpec((tm, tn), lambda i, j: (i, j)),
                   pl.BlockSpec((tm, ndt), lambda i, j: (i, 0))],
        out_shape=[jax.ShapeDtypeStruct((m, n), BF16),
                   jax.ShapeDtypeStruct((m, ndt), F32)],
        scratch_shapes=[pltpu.VMEM((tm, d), BF16)],
        compiler_params=_cparams("arbitrary", "arbitrary"),
        name="in_proj",
    )(x2d, norm_w.reshape(1, d), sc, sh, w_main, w_dt)


_CONV_HALO = 16


def _conv_kernel(p_ref, w_ref, b_ref, o_ref, *, n, rc):
    tc = o_ref.shape[-1]
    w = w_ref[...]
    bias = b_ref[...]
    zeros = jnp.zeros((_CONV_HALO, tc), F32)
    tot = rc + 2 * _CONV_HALO
    for r0 in range(0, n, rc):
        top = zeros if r0 == 0 else p_ref[r0 - _CONV_HALO:r0, :].astype(F32)
        bot = zeros if r0 + rc >= n else p_ref[r0 + rc:r0 + rc + _CONV_HALO, :].astype(F32)
        blk = jnp.concatenate([top, p_ref[r0:r0 + rc, :].astype(F32), bot], axis=0)
        acc = jnp.broadcast_to(bias, (rc, tc))
        for k in range(CONV_K):
            shift = (CONV_K // 2 - k) % tot
            rolled = blk if shift == 0 else pltpu.roll(blk, shift, axis=0)
            acc = acc + w[k:k + 1, :] * rolled[_CONV_HALO:_CONV_HALO + rc, :]
        o_ref[r0:r0 + rc, :] = _silu(acc).astype(o_ref.dtype)


def _conv_silu(p3d, conv_w, conv_b, n_ch):
    bsz, n, _ = p3d.shape
    tc = 512
    rc = min(256, n)
    return pl.pallas_call(
        functools.partial(_conv_kernel, n=n, rc=rc),
        grid=(bsz, n_ch // tc),
        in_specs=[pl.BlockSpec((None, n, tc), lambda b, j: (b, 0, j)),
                  pl.BlockSpec((CONV_K, tc), lambda b, j: (0, j)),
                  pl.BlockSpec((1, tc), lambda b, j: (0, j))],
        out_specs=pl.BlockSpec((None, n, tc), lambda b, j: (b, 0, j)),
        out_shape=jax.ShapeDtypeStruct((bsz, n, n_ch), BF16),
        compiler_params=_cparams("arbitrary", "arbitrary"),
        name="conv_silu",
    )(p3d, conv_w, conv_b.reshape(1, -1))


def _ssd_kernel(*refs, with_y, nc):
    if with_y:
        x_ref, b_ref, c_ref, dt_ref, dtb_ref, alog_ref, h0_ref, y_ref, s_scr = refs
    else:
        x_ref, b_ref, dt_ref, dtb_ref, alog_ref, h0_ref, hout_ref, s_scr = refs
    q = CHUNK
    d = pl.program_id(1)
    c = pl.program_id(2)

    @pl.when(c == 0)
    def _():
        s_scr[...] = h0_ref[...]

    row = lax.broadcasted_iota(jnp.int32, (q, q), 0)
    col = lax.broadcasted_iota(jnp.int32, (q, q), 1)
    fwd = d == 0
    tri = jnp.where(fwd, row - col, col - row) >= 0
    first_half = col < HEAD_DIM

    dt = _softplus(dt_ref[...] + dtb_ref[...])
    a = -jnp.exp(alog_ref[...])
    da = dt * a
    hi = da.astype(BF16)
    r1 = da - hi.astype(F32)
    mid = r1.astype(BF16)
    lo = (r1 - mid.astype(F32)).astype(BF16)
    ones_tri = jnp.where(tri, 1.0, 0.0).astype(BF16)
    cs3 = jnp.dot(ones_tri, jnp.concatenate([hi, mid, lo], axis=1), preferred_element_type=F32)
    cs = cs3[:, :LANES] + cs3[:, LANES:2 * LANES] + cs3[:, 2 * LANES:]
    tot = jnp.where(fwd, cs[q - 1:q, :], cs[0:1, :])
    dte = dt * jnp.exp(tot - cs)
    dec = jnp.exp(tot)
    cs_t = cs.T
    dt_t = dt.T
    dte_t = dte.T

    for g in range(N_GROUPS):
        bg = b_ref[:, g * D_STATE:(g + 1) * D_STATE]
        bt = bg.astype(F32).T
        if with_y:
            cg = c_ref[:, g * D_STATE:(g + 1) * D_STATE]
            cg32 = cg.astype(F32)
            cb = lax.dot_general(cg, bg, (((1,), (1,)), ((), ())), preferred_element_type=F32)
        for k in range(HEADS_PER_GROUP // 2):
            pair = g * (HEADS_PER_GROUP // 2) + k
            lanes = slice(pair * LANES, (pair + 1) * LANES)
            xp = x_ref[:, lanes]
            xz = jnp.zeros_like(xp)
            xa = jnp.where(first_half, xp, xz)
            xb = jnp.where(first_half, xz, xp)
            sp = s_scr[g, :, k * LANES:(k + 1) * LANES]
            lhs_y = []
            lhs_s = []
            for hh in (2 * pair, 2 * pair + 1):
                row_dte = dte_t[hh:hh + 1, :]
                lhs_s.append((bt * row_dte).astype(BF16))
                if with_y:
                    colb = jnp.broadcast_to(cs[:, hh:hh + 1], (q, q))
                    rowb = cs_t[hh:hh + 1, :]
                    decay = jnp.exp(jnp.where(tri, colb - rowb, -jnp.inf))
                    lhs_y.append((cb * decay * dt_t[hh:hh + 1, :]).astype(BF16))
                    lhs_y.append((cg32 * jnp.exp(colb)).astype(BF16))
            if with_y:
                sz = jnp.zeros_like(sp)
                sa = jnp.where(first_half, sp, sz).astype(BF16)
                sb = jnp.where(first_half, sz, sp).astype(BF16)
                y = jnp.dot(jnp.concatenate(lhs_y, axis=1), jnp.concatenate([xa, sa, xb, sb], axis=0),
                            preferred_element_type=F32)
                y_ref[:, lanes] = y.astype(y_ref.dtype)
            upd = jnp.dot(jnp.concatenate(lhs_s, axis=1), jnp.concatenate([xa, xb], axis=0),
                          preferred_element_type=F32)
            h_a = 2 * pair
            dec_pair = jnp.where(first_half[0:1, :],
                                 jnp.broadcast_to(dec[:, h_a:h_a + 1], (1, LANES)),
                                 jnp.broadcast_to(dec[:, h_a + 1:h_a + 2], (1, LANES)))
            s_scr[g, :, k * LANES:(k + 1) * LANES] = sp * dec_pair + upd

    if not with_y:
        @pl.when(c == nc - 1)
        def _():
            hout_ref[...] = s_scr[...]


def _ssd(xbc, dt_raw, dt_bias2, a_log2, h0, with_y):
    bsz, n, _ = xbc.shape
    nc = n // CHUNK
    d_inner = N_HEADS * HEAD_DIM
    gw = HEADS_PER_GROUP * HEAD_DIM
    x_blk = d_inner // BC_WIDTH

    def ceff(dd, cc):
        return cc + dd * (nc - 1 - 2 * cc)

    in_specs = [pl.BlockSpec((None, CHUNK, d_inner), lambda b, dd, cc: (b, ceff(dd, cc), 0)),
                pl.BlockSpec((None, CHUNK, BC_WIDTH), lambda b, dd, cc: (b, ceff(dd, cc), x_blk + dd))]
    args = [xbc, xbc]
    if with_y:
        in_specs.append(pl.BlockSpec((None, CHUNK, BC_WIDTH), lambda b, dd, cc: (b, ceff(dd, cc), x_blk + 2 + dd)))
        args.append(xbc)
    in_specs += [pl.BlockSpec((None, CHUNK, LANES), lambda b, dd, cc: (b, ceff(dd, cc), dd)),
                 pl.BlockSpec((None, 1, LANES), lambda b, dd, cc: (dd, 0, 0)),
                 pl.BlockSpec((None, 1, LANES), lambda b, dd, cc: (dd, 0, 0)),
                 pl.BlockSpec((None, None, N_GROUPS, D_STATE, gw), lambda b, dd, cc: (b, dd, 0, 0, 0))]
    args += [dt_raw, dt_bias2, a_log2, h0]
    if with_y:
        out_specs = pl.BlockSpec((None, None, CHUNK, d_inner), lambda b, dd, cc: (dd, b, ceff(dd, cc), 0))
        out_shape = jax.ShapeDtypeStruct((2, bsz, n, d_inner), BF16)
    else:
        out_specs = pl.BlockSpec((None, None, N_GROUPS, D_STATE, gw), lambda b, dd, cc: (b, dd, 0, 0, 0))
        out_shape = jax.ShapeDtypeStruct((bsz, 2, N_GROUPS, D_STATE, gw), F32)
    return pl.pallas_call(
        functools.partial(_ssd_kernel, with_y=with_y, nc=nc),
        grid=(bsz, 2, nc),
        in_specs=in_specs,
        out_specs=out_specs,
        out_shape=out_shape,
        scratch_shapes=[pltpu.VMEM((N_GROUPS, D_STATE, gw), F32)],
        compiler_params=_cparams("arbitrary", "arbitrary", "arbitrary"),
        name="ssd_scan" if with_y else "ssd_ctx_state",
    )(*args)


def _dft_mats(n):
    k = np.arange(n)
    ang = 2.0 * np.pi * ((k[:, None] * k[None, :]) % n) / n
    return np.cos(ang), np.sin(ang)


def _fnet_chan_kernel(u_ref, m_ref, o_ref):
    m = m_ref[...]
    for g in range(F_GROUPS):
        ug = u_ref[:, g * F_GROUP_DIM:(g + 1) * F_GROUP_DIM]
        pq = jnp.dot(ug, m, preferred_element_type=F32)
        o_ref[0, :, g * F_GROUP_DIM:(g + 1) * F_GROUP_DIM] = pq[:, :F_GROUP_DIM].astype(o_ref.dtype)
        o_ref[1, :, g * F_GROUP_DIM:(g + 1) * F_GROUP_DIM] = pq[:, F_GROUP_DIM:].astype(o_ref.dtype)


def _fnet_stage1_kernel(z_ref, m1_ref, twc_ref, tws_ref, o_ref, *, tcs, width):
    r = z_ref.shape[1]
    zz = jnp.concatenate([z_ref[0], z_ref[1]], axis=0)
    u = jnp.dot(m1_ref[...], zz, preferred_element_type=F32)
    for j in range(tcs):
        ur = u[:r, j * width:(j + 1) * width]
        ui = u[r:, j * width:(j + 1) * width]
        tc = twc_ref[j]
        ts = tws_ref[j]
        o_ref[0, j] = (ur * tc + ui * ts).astype(o_ref.dtype)
        o_ref[1, j] = (ui * tc - ur * ts).astype(o_ref.dtype)


def _fnet_stage2_kernel(u_ref, m2_ref, o_ref):
    uu = jnp.concatenate([u_ref[0], u_ref[1]], axis=0)
    o_ref[...] = jnp.dot(m2_ref[...], uu, preferred_element_type=F32).astype(o_ref.dtype)


def _fnet_mix(p2d, col_blk, bsz, n):
    m = p2d.shape[0]
    width = F_GROUPS * F_GROUP_DIM
    cgrid = GRID_W
    rgrid = n // cgrid
    scale = 1.0 / math.sqrt(n * F_GROUP_DIM)

    cc, sc = _dft_mats(F_GROUP_DIM)
    chan = jnp.asarray(np.concatenate([cc, sc], axis=1) * scale, BF16)
    tm = 512
    pq = pl.pallas_call(
        _fnet_chan_kernel,
        grid=(m // tm,),
        in_specs=[pl.BlockSpec((tm, width), lambda i: (i, col_blk)),
                  pl.BlockSpec((F_GROUP_DIM, 2 * F_GROUP_DIM), lambda i: (0, 0))],
        out_specs=pl.BlockSpec((2, tm, width), lambda i: (0, i, 0)),
        out_shape=jax.ShapeDtypeStruct((2, m, width), BF16),
        compiler_params=_cparams("arbitrary"),
        name="fnet_chan",
    )(p2d, chan)

    cr, sr = _dft_mats(rgrid)
    m1 = jnp.asarray(np.block([[cr, -sr], [-sr, -cr]]), BF16)
    bb = np.arange(rgrid)[None, :]
    ci = np.arange(cgrid)[:, None]
    ang = 2.0 * np.pi * ((ci * bb) % n) / n
    twc = jnp.asarray(np.cos(ang)[:, :, None], F32)
    tws = jnp.asarray(np.sin(ang)[:, :, None], F32)
    tcs = 4
    z4 = pq.reshape(2, bsz, rgrid, cgrid * width)
    u5 = pl.pallas_call(
        functools.partial(_fnet_stage1_kernel, tcs=tcs, width=width),
        grid=(bsz, cgrid // tcs),
        in_specs=[pl.BlockSpec((2, None, rgrid, tcs * width), lambda b, j: (0, b, 0, j)),
                  pl.BlockSpec((2 * rgrid, 2 * rgrid), lambda b, j: (0, 0)),
                  pl.BlockSpec((tcs, rgrid, 1), lambda b, j: (j, 0, 0)),
                  pl.BlockSpec((tcs, rgrid, 1), lambda b, j: (j, 0, 0))],
        out_specs=pl.BlockSpec((2, None, tcs, rgrid, width), lambda b, j: (0, b, j, 0, 0)),
        out_shape=jax.ShapeDtypeStruct((2, bsz, cgrid, rgrid, width), BF16),
        compiler_params=_cparams("arbitrary", "arbitrary"),
        name="fnet_stage1",
    )(z4, m1, twc, tws)

    cc2, sc2 = _dft_mats(cgrid)
    m2 = jnp.asarray(np.concatenate([cc2, sc2], axis=1), BF16)
    tn = min(4096, rgrid * width)
    u4 = u5.reshape(2, bsz, cgrid, rgrid * width)
    out = pl.pallas_call(
        _fnet_stage2_kernel,
        grid=(bsz, (rgrid * width) // tn),
        in_specs=[pl.BlockSpec((2, None, cgrid, tn), lambda b, j: (0, b, 0, j)),
                  pl.BlockSpec((cgrid, 2 * cgrid), lambda b, j: (0, 0))],
        out_specs=pl.BlockSpec((None, cgrid, tn), lambda b, j: (b, 0, j)),
        out_shape=jax.ShapeDtypeStruct((bsz, cgrid, rgrid * width), BF16),
        compiler_params=_cparams("arbitrary", "arbitrary"),
        name="fnet_stage2",
    )(u4, m2)
    return out.reshape(m, width)


def _tail_kernel(yf_ref, yb_ref, xs_ref, z_ref, ff_ref, gf_ref, gs_ref, x_ref, g1_ref, sc2_ref, sh2_ref,
                 dsk_ref, snw_ref, wso_ref, wfo_ref, wo_ref, n2w_ref, wr_ref, br_ref,
                 x1_ref, h2_ref, ti_ref, tw_ref):
    y = yf_ref[0].astype(F32) + yb_ref[0].astype(F32) + dsk_ref[...] * xs_ref[...].astype(F32)
    g = y * _silu(z_ref[...].astype(F32))
    inv = lax.rsqrt(jnp.mean(g * g, axis=-1, keepdims=True) + EPS)
    gn = (g * inv * snw_ref[...]).astype(BF16)
    y_ssd = jnp.dot(gn, wso_ref[...], preferred_element_type=F32)
    y_four = jnp.dot(ff_ref[...], wfo_ref[...], preferred_element_type=F32)
    t = _sigmoid(gf_ref[...].astype(F32)) * y_four + _sigmoid(gs_ref[...].astype(F32)) * y_ssd
    mix = jnp.dot(t.astype(BF16), wo_ref[...], preferred_element_type=F32)
    x1 = x_ref[...] + g1_ref[...] * mix
    x1_ref[...] = x1
    inv2 = lax.rsqrt(jnp.mean(x1 * x1, axis=-1, keepdims=True) + EPS)
    h2 = (x1 * inv2 * n2w_ref[...]) * (1.0 + sc2_ref[...]) + sh2_ref[...]
    h2_ref[...] = h2
    h_hi = h2.astype(BF16)
    h_lo = (h2 - h_hi.astype(F32)).astype(BF16)
    wr = wr_ref[...]
    w_hi = wr.astype(BF16)
    w_lo = (wr - w_hi.astype(F32)).astype(BF16)
    logits = (jnp.dot(h_hi, w_hi, preferred_element_type=F32)
              + jnp.dot(h_hi, w_lo, preferred_element_type=F32)
              + jnp.dot(h_lo, w_hi, preferred_element_type=F32)) + br_ref[...]
    tm = logits.shape[0]
    lane = lax.broadcasted_iota(jnp.int32, (tm, LANES), 1)
    vals, idxs = [], []
    cur = logits
    for _ in range(TOP_K):
        mx = jnp.max(cur, axis=-1, keepdims=True)
        ix = jnp.min(jnp.where(cur == mx, lane, LANES), axis=-1, keepdims=True)
        vals.append(mx)
        idxs.append(ix)
        cur = jnp.where(lane == ix, NEG_BIG * 2.0, cur)
    es = [jnp.exp(v - vals[0]) for v in vals]
    den = es[0] + es[1] + es[2] + es[3]
    ti = jnp.zeros((tm, LANES), jnp.int32)
    tw = jnp.zeros((tm, LANES), F32)
    for k in range(TOP_K):
        ti = jnp.where(lane == k, idxs[k], ti)
        tw = jnp.where(lane == k, es[k] / den, tw)
    ti_ref[...] = ti
    tw_ref[...] = tw


def _tail(y2, xbc2d, p2d, ff, x2d, g1, sc2, sh2, dsk, snw, wso, wfo, wo, n2w, wr, br, rows_per_mod, tm):
    m, d = x2d.shape
    di = y2.shape[-1]
    zb = (2 * di) // di
    fb = (3 * di) // d
    row = lambda i: (i, 0)
    modrow = lambda i: ((i * tm) // rows_per_mod, 0, 0)
    const = lambda i: (0, 0)
    in_specs = [pl.BlockSpec((1, tm, di), lambda i: (0, i, 0)),
                pl.BlockSpec((1, tm, di), lambda i: (1, i, 0)),
                pl.BlockSpec((tm, di), row),
                pl.BlockSpec((tm, di), lambda i: (i, zb)),
                pl.BlockSpec((tm, d), row),
                pl.BlockSpec((tm, d), lambda i: (i, fb + 1)),
                pl.BlockSpec((tm, d), lambda i: (i, fb + 2)),
                pl.BlockSpec((tm, d), row),
                pl.BlockSpec((None, 1, d), modrow),
                pl.BlockSpec((None, 1, d), modrow),
                pl.BlockSpec((None, 1, d), modrow),
                pl.BlockSpec((1, di), const),
                pl.BlockSpec((1, di), const),
                pl.BlockSpec((di, d), const),
                pl.BlockSpec((d, d), const),
                pl.BlockSpec((d, d), const),
                pl.BlockSpec((1, d), const),
                pl.BlockSpec((d, LANES), const),
                pl.BlockSpec((1, LANES), const)]
    out_specs = [pl.BlockSpec((tm, d), row), pl.BlockSpec((tm, d), row),
                 pl.BlockSpec((tm, LANES), row), pl.BlockSpec((tm, LANES), row)]
    out_shape = [jax.ShapeDtypeStruct((m, d), F32), jax.ShapeDtypeStruct((m, d), F32),
                 jax.ShapeDtypeStruct((m, LANES), jnp.int32), jax.ShapeDtypeStruct((m, LANES), F32)]
    return pl.pallas_call(
        _tail_kernel,
        grid=(m // tm,),
        in_specs=in_specs,
        out_specs=out_specs,
        out_shape=out_shape,
        compiler_params=_cparams("arbitrary"),
        name="tail",
    )(y2, y2, xbc2d, p2d, ff, p2d, p2d, x2d, g1, sc2, sh2, dsk, snw, wso, wfo, wo, n2w, wr, br)


def _gather_kernel(idx_ref, src_ref, out_ref, sem, *, rows):
    base = pl.program_id(0) * rows

    def issue(r, carry):
        pltpu.make_async_copy(src_ref.at[pl.ds(idx_ref[r], 1)], out_ref.at[pl.ds(base + r, 1)], sem).start()
        return carry

    lax.fori_loop(0, rows, issue, 0)
    pltpu.make_async_copy(src_ref.at[pl.ds(0, rows)], out_ref.at[pl.ds(base, rows)], sem).wait()


def _gather_rows(src, idx):
    n_out = idx.shape[0]
    rows = GATHER_ROWS
    return pl.pallas_call(
        functools.partial(_gather_kernel, rows=rows),
        grid=(n_out // rows,),
        in_specs=[pl.BlockSpec((rows,), lambda i: (i,), memory_space=pltpu.SMEM),
                  pl.BlockSpec(memory_space=pl.ANY)],
        out_specs=pl.BlockSpec(memory_space=pl.ANY),
        out_shape=jax.ShapeDtypeStruct((n_out, src.shape[1]), src.dtype),
        scratch_shapes=[pltpu.SemaphoreType.DMA(())],
        compiler_params=_cparams("arbitrary"),
        name="gather_rows",
    )(idx, src)


def _expert_kernel(te_ref, nu_ref, x_ref, wgu_ref, bgu_ref, wd_ref, bd_ref, o_ref):
    @pl.when(pl.program_id(0) < nu_ref[0])
    def _():
        dff = wd_ref.shape[0]
        gu = jnp.dot(x_ref[...].astype(BF16), wgu_ref[...], preferred_element_type=F32) + bgu_ref[...]
        gate = jnp.minimum(gu[:, :dff], SWIGLU_LIMIT)
        up = jnp.clip(gu[:, dff:], -SWIGLU_LIMIT, SWIGLU_LIMIT)
        act = (up + 1.0) * gate * _sigmoid(SWIGLU_ALPHA * gate)
        o_ref[...] = jnp.dot(act.astype(BF16), wd_ref[...], preferred_element_type=F32) + bd_ref[...]

    @pl.when(pl.program_id(0) >= nu_ref[0])
    def _():
        o_ref[...] = jnp.zeros(o_ref.shape, o_ref.dtype)


def _experts(xs, tile_expert, n_used, wgu, bgu, wd, bd):
    rows, d = xs.shape
    tm = MOE_TILE
    n_tiles = rows // tm
    dff2 = wgu.shape[-1]

    def tile(i, te, nu):
        return (jnp.minimum(i, nu[0] - 1), 0)

    def wsel(i, te, nu):
        return (te[jnp.minimum(i, nu[0] - 1)], 0, 0)

    grid_spec = pltpu.PrefetchScalarGridSpec(
        num_scalar_prefetch=2,
        grid=(n_tiles,),
        in_specs=[pl.BlockSpec((tm, d), tile),
                  pl.BlockSpec((None, d, dff2), wsel),
                  pl.BlockSpec((None, 1, dff2), wsel),
                  pl.BlockSpec((None, dff2 // 2, d), wsel),
                  pl.BlockSpec((None, 1, d), wsel)],
        out_specs=pl.BlockSpec((tm, d), lambda i, te, nu: (i, 0)),
    )
    return pl.pallas_call(
        _expert_kernel,
        grid_spec=grid_spec,
        out_shape=jax.ShapeDtypeStruct((rows, d), F32),
        compiler_params=_cparams("arbitrary"),
        name="experts",
    )(tile_expert, n_used, xs, wgu, bgu, wd, bd)


def _final_kernel(x1_ref, y4_ref, tw_ref, g2_ref, fw_ref, o_ref):
    d = x1_ref.shape[-1]
    tw = tw_ref[...]
    acc = jnp.zeros(x1_ref.shape, F32)
    for k in range(TOP_K):
        acc = acc + tw[:, k:k + 1] * y4_ref[:, k * d:(k + 1) * d]
    x2 = x1_ref[...] + g2_ref[...] * acc
    inv = lax.rsqrt(jnp.mean(x2 * x2, axis=-1, keepdims=True) + EPS)
    o_ref[...] = x2 * inv * fw_ref[...]


def _final(x1, y4, tw, g2, fw, rows_per_mod, tm):
    m, d = x1.shape
    return pl.pallas_call(
        _final_kernel,
        grid=(m // tm,),
        in_specs=[pl.BlockSpec((tm, d), lambda i: (i, 0)),
                  pl.BlockSpec((tm, TOP_K * d), lambda i: (i, 0)),
                  pl.BlockSpec((tm, LANES), lambda i: (i, 0)),
                  pl.BlockSpec((None, 1, d), lambda i: ((i * tm) // rows_per_mod, 0, 0)),
                  pl.BlockSpec((1, d), lambda i: (0, 0))],
        out_specs=pl.BlockSpec((tm, d), lambda i: (i, 0)),
        out_shape=jax.ShapeDtypeStruct((m, d), F32),
        compiler_params=_cparams("arbitrary"),
        name="final",
    )(x1, y4, tw, g2, fw.reshape(1, d))


def _dispatch_plan(top_idx, tm):
    n_tok = top_idx.shape[0]
    flat_e = top_idx.reshape(-1)
    n_assign = flat_e.shape[0]
    order = jnp.argsort(flat_e)
    e_sorted = flat_e[order]
    tok_sorted = (order // TOP_K).astype(jnp.int32)
    counts = jnp.bincount(flat_e, length=N_EXPERTS)
    padded = (counts + tm - 1) // tm * tm
    pad_end = jnp.cumsum(padded)
    pad_start = pad_end - padded
    start = jnp.cumsum(counts) - counts
    dest = (pad_start[e_sorted] + jnp.arange(n_assign) - start[e_sorted]).astype(jnp.int32)
    n_tiles = n_assign // tm + N_EXPERTS
    rows = n_tiles * tm
    row_tok = jnp.zeros((rows,), jnp.int32).at[dest].set(tok_sorted)
    tile_expert = jnp.minimum(jnp.searchsorted(pad_end, jnp.arange(n_tiles) * tm, side='right'),
                              N_EXPERTS - 1).astype(jnp.int32)
    n_used = (pad_end[-1] // tm).astype(jnp.int32).reshape(1)
    pos = jnp.zeros((n_assign,), jnp.int32).at[order].set(dest)
    del n_tok
    return row_tok, tile_expert, n_used, pos


def kernel(x, c, ctx, c_ctx, w_mod, b_mod, norm1_w, norm2_w, w_in, conv_w, conv_b, dt_bias, a_log, d_skip,
           ssd_norm_w, w_ssd_out, w_four_out, w_o, w_router, b_router, w_gate_up, b_gate_up, w_down, b_down,
           final_norm_w):
    bsz, n, d = x.shape
    n_ctx = ctx.shape[1]
    depth = w_mod.shape[0]
    assert depth == 1, "a stacked model would also need the context stream's residual update"
    d_inner = N_HEADS * HEAD_DIM
    off_dt = d_inner + 2 * BC_WIDTH
    m = bsz * n
    x2d = x.reshape(m, d)
    tm = min(512, n)

    for layer in range(depth):
        rows = -(-(bsz + 1) // 8) * 8
        c_rows = jnp.zeros((rows, d), F32).at[:bsz].set(c).at[bsz].set(c_ctx)
        mod = _modulation(c_rows, w_mod[layer], b_mod[layer])
        mods = [mod[:bsz, i * d:(i + 1) * d].reshape(bsz, 1, d) for i in range(N_MOD)]
        sh1, sc1, g1, sh2, sc2, g2 = mods
        sh1_c = mod[bsz:bsz + 1, :d].reshape(1, 1, d)
        sc1_c = mod[bsz:bsz + 1, d:2 * d].reshape(1, 1, d)

        wl = w_in[layer]
        w_main = jnp.concatenate([wl[:, :off_dt], wl[:, off_dt + 2 * N_HEADS:]], axis=1).astype(BF16)
        w_dt = jnp.zeros((d, 2 * LANES), F32)
        w_dt = w_dt.at[:, :N_HEADS].set(wl[:, off_dt:off_dt + N_HEADS])
        w_dt = w_dt.at[:, LANES:LANES + N_HEADS].set(wl[:, off_dt + N_HEADS:off_dt + 2 * N_HEADS]).astype(BF16)
        pad = jnp.zeros((2, 1, LANES - N_HEADS), F32)
        dt_bias2 = jnp.concatenate([dt_bias[layer].reshape(2, 1, N_HEADS), pad], axis=-1)
        a_log2 = jnp.concatenate([a_log[layer].reshape(2, 1, N_HEADS), pad], axis=-1)

        w_ctx = w_main[:, :off_dt]
        pc, dtc = _in_proj(ctx.reshape(bsz * n_ctx, d), norm1_w[layer], sc1_c, sh1_c, w_ctx, w_dt,
                           bsz * n_ctx, min(512, n_ctx))
        xbc_c = _conv_silu(pc.reshape(bsz, n_ctx, off_dt), conv_w[layer][:, :off_dt], conv_b[layer][:off_dt], off_dt)
        h0 = jnp.zeros((bsz, 2, N_GROUPS, D_STATE, HEADS_PER_GROUP * HEAD_DIM), F32)
        h_ctx = _ssd(xbc_c, dtc.reshape(bsz, n_ctx, 2 * LANES), dt_bias2, a_log2, h0, with_y=False)

        p, dtl = _in_proj(x2d, norm1_w[layer], sc1, sh1, w_main, w_dt, n, tm)
        n_conv = off_dt + 2 * BC_WIDTH
        xbc = _conv_silu(p.reshape(bsz, n, -1), conv_w[layer], conv_b[layer], n_conv)
        y2 = _ssd(xbc, dtl.reshape(bsz, n, 2 * LANES), dt_bias2, a_log2, h_ctx, with_y=True)
        ff = _fnet_mix(p, (n_conv + d_inner) // d, bsz, n)

        dsk = jnp.repeat(d_skip[layer].astype(F32), HEAD_DIM).reshape(1, d_inner)
        wr = jnp.zeros((d, LANES), F32).at[:, :N_EXPERTS].set(w_router[layer])
        br = jnp.full((1, LANES), NEG_BIG, F32).at[0, :N_EXPERTS].set(b_router[layer])
        x1, h2, ti, tw = _tail(
            y2.reshape(2, m, d_inner), xbc.reshape(m, n_conv), p, ff, x2d, g1, sc2, sh2, dsk,
            ssd_norm_w[layer].reshape(1, d_inner), w_ssd_out[layer].astype(BF16), w_four_out[layer].astype(BF16),
            w_o[layer].astype(BF16), norm2_w[layer].reshape(1, d), wr, br, n, tm)

        row_tok, tile_expert, n_used, pos = _dispatch_plan(ti[:, :TOP_K], MOE_TILE)
        xs = _gather_rows(h2, row_tok)
        ys = _experts(xs, tile_expert, n_used, w_gate_up[layer].astype(BF16),
                      b_gate_up[layer].reshape(N_EXPERTS, 1, -1), w_down[layer].astype(BF16),
                      b_down[layer].reshape(N_EXPERTS, 1, -1))
        y4 = _gather_rows(ys, pos).reshape(m, TOP_K * d)
        x2d = _final(x1, y4, tw, g2, final_norm_w, n, tm)
    return x2d.reshape(bsz, n, d)
```

```python
import functools
import math

import numpy as np
import jax
import jax.numpy as jnp
from jax import lax
from jax.experimental import pallas as pl
from jax.experimental.pallas import tpu as pltpu
from jax.experimental.pallas import tpu_sc as plsc

F32 = jnp.float32
BF16 = jnp.bfloat16

EPS = 1e-6
GRID_W = 64
N_MOD = 6
F_GROUPS = 8
F_GROUP_DIM = 128
HEAD_DIM = 64
N_HEADS = 32
N_GROUPS = 4
HEADS_PER_GROUP = N_HEADS // N_GROUPS
D_STATE = 128
BC_WIDTH = N_GROUPS * D_STATE
CONV_K = 5
CHUNK = 128
N_EXPERTS = 32
TOP_K = 4
SWIGLU_LIMIT = 7.0
SWIGLU_ALPHA = 1.702

LANES = 128
VMEM_LIMIT_BYTES = 48 * 1024 * 1024
NEG_BIG = -1e30
MOE_TILE = 512
SC_GATHER_WINDOW = 32


def _cparams(*sem):
    return pltpu.CompilerParams(dimension_semantics=sem, vmem_limit_bytes=VMEM_LIMIT_BYTES)


def _sigmoid(v):
    return 1.0 / (1.0 + jnp.exp(-v))


def _silu(v):
    return v * _sigmoid(v)


def _softplus(v):
    return jnp.maximum(v, 0.0) + jnp.log(1.0 + jnp.exp(-jnp.abs(v)))


def _mod_kernel(c_ref, w_ref, b_ref, o_ref):
    s = _silu(c_ref[...]).astype(BF16)
    o_ref[...] = jnp.dot(s, w_ref[...].astype(BF16), preferred_element_type=F32) + b_ref[...]


def _modulation(c_rows, w_mod, b_mod):
    rows, d = c_rows.shape
    n_out = w_mod.shape[1]
    tn = 1024
    return pl.pallas_call(
        _mod_kernel,
        grid=(n_out // tn,),
        in_specs=[pl.BlockSpec((rows, d), lambda j: (0, 0)),
                  pl.BlockSpec((d, tn), lambda j: (0, j)),
                  pl.BlockSpec((1, tn), lambda j: (0, j))],
        out_specs=pl.BlockSpec((rows, tn), lambda j: (0, j)),
        out_shape=jax.ShapeDtypeStruct((rows, n_out), F32),
        compiler_params=_cparams("arbitrary"),
        name="modulation",
    )(c_rows, w_mod, b_mod.reshape(1, n_out))


def _inproj_kernel(x_ref, nw_ref, sc_ref, sh_ref, w_ref, wdt_ref, p_ref, dt_ref, h_scr):
    @pl.when(pl.program_id(1) == 0)
    def _():
        x = x_ref[...]
        inv = lax.rsqrt(jnp.mean(x * x, axis=-1, keepdims=True) + EPS)
        h = (x * inv * nw_ref[...]) * (1.0 + sc_ref[...]) + sh_ref[...]
        hb = h.astype(BF16)
        h_scr[...] = hb
        dt_ref[...] = jnp.dot(hb, wdt_ref[...], preferred_element_type=F32)

    p_ref[...] = jnp.dot(h_scr[...], w_ref[...], preferred_element_type=F32).astype(p_ref.dtype)


def _in_proj(x2d, norm_w, sc, sh, w_main, w_dt, rows_per_mod, tm):
    m, d = x2d.shape
    n = w_main.shape[1]
    tn = 1024
    ndt = w_dt.shape[1]
    return pl.pallas_call(
        _inproj_kernel,
        grid=(m // tm, n // tn),
        in_specs=[pl.BlockSpec((tm, d), lambda i, j: (i, 0)),
                  pl.BlockSpec((1, d), lambda i, j: (0, 0)),
                  pl.BlockSpec((None, 1, d), lambda i, j: ((i * tm) // rows_per_mod, 0, 0)),
                  pl.BlockSpec((None, 1, d), lambda i, j: ((i * tm) // rows_per_mod, 0, 0)),
                  pl.BlockSpec((d, tn), lambda i, j: (0, j)),
                  pl.BlockSpec((d, ndt), lambda i, j: (0, 0))],
        out_specs=[pl.BlockSpec((tm, tn), lambda i, j: (i, j)),
                   pl.BlockSpec((tm, ndt), lambda i, j: (i, 0))],
        out_shape=[jax.ShapeDtypeStruct((m, n), BF16),
                   jax.ShapeDtypeStruct((m, ndt), F32)],
        scratch_shapes=[pltpu.VMEM((tm, d), BF16)],
        compiler_params=_cparams("arbitrary", "arbitrary"),
        name="in_proj",
    )(x2d, norm_w.reshape(1, d), sc, sh, w_main, w_dt)


_CONV_HALO = 16


def _conv_kernel(p_ref, w_ref, b_ref, o_ref, *, n, rc):
    tc = o_ref.shape[-1]
    w = w_ref[...]
    bias = b_ref[...]
    zeros = jnp.zeros((_CONV_HALO, tc), F32)
    tot = rc + 2 * _CONV_HALO
    for r0 in range(0, n, rc):
        top = zeros if r0 == 0 else p_ref[r0 - _CONV_HALO:r0, :].astype(F32)
        bot = zeros if r0 + rc >= n else p_ref[r0 + rc:r0 + rc + _CONV_HALO, :].astype(F32)
        blk = jnp.concatenate([top, p_ref[r0:r0 + rc, :].astype(F32), bot], axis=0)
        acc = jnp.broadcast_to(bias, (rc, tc))
        for k in range(CONV_K):
            shift = (CONV_K // 2 - k) % tot
            rolled = blk if shift == 0 else pltpu.roll(blk, shift, axis=0)
            acc = acc + w[k:k + 1, :] * rolled[_CONV_HALO:_CONV_HALO + rc, :]
        o_ref[r0:r0 + rc, :] = _silu(acc).astype(o_ref.dtype)


def _conv_silu(p3d, conv_w, conv_b, n_ch):
    bsz, n, _ = p3d.shape
    tc = 512
    rc = min(256, n)
    return pl.pallas_call(
        functools.partial(_conv_kernel, n=n, rc=rc),
        grid=(bsz, n_ch // tc),
        in_specs=[pl.BlockSpec((None, n, tc), lambda b, j: (b, 0, j)),
                  pl.BlockSpec((CONV_K, tc), lambda b, j: (0, j)),
                  pl.BlockSpec((1, tc), lambda b, j: (0, j))],
        out_specs=pl.BlockSpec((None, n, tc), lambda b, j: (b, 0, j)),
        out_shape=jax.ShapeDtypeStruct((bsz, n, n_ch), BF16),
        compiler_params=_cparams("arbitrary", "arbitrary"),
        name="conv_silu",
    )(p3d, conv_w, conv_b.reshape(1, -1))


def _ssd_kernel(*refs, with_y, nc):
    if with_y:
        x_ref, b_ref, c_ref, dt_ref, dtb_ref, alog_ref, h0_ref, y_ref, s_scr = refs
    else:
        x_ref, b_ref, dt_ref, dtb_ref, alog_ref, h0_ref, hout_ref, s_scr = refs
    q = CHUNK
    d = pl.program_id(1)
    c = pl.program_id(2)

    @pl.when(c == 0)
    def _():
        s_scr[...] = h0_ref[...]

    row = lax.broadcasted_iota(jnp.int32, (q, q), 0)
    col = lax.broadcasted_iota(jnp.int32, (q, q), 1)
    fwd = d == 0
    tri = jnp.where(fwd, row - col, col - row) >= 0
    first_half = col < HEAD_DIM

    dt = _softplus(dt_ref[...] + dtb_ref[...])
    a = -jnp.exp(alog_ref[...])
    da = dt * a
    hi = da.astype(BF16)
    r1 = da - hi.astype(F32)
    mid = r1.astype(BF16)
    lo = (r1 - mid.astype(F32)).astype(BF16)
    ones_tri = jnp.where(tri, 1.0, 0.0).astype(BF16)
    cs3 = jnp.dot(ones_tri, jnp.concatenate([hi, mid, lo], axis=1), preferred_element_type=F32)
    cs = cs3[:, :LANES] + cs3[:, LANES:2 * LANES] + cs3[:, 2 * LANES:]
    tot = jnp.where(fwd, cs[q - 1:q, :], cs[0:1, :])
    dte = dt * jnp.exp(tot - cs)
    dec = jnp.exp(tot)
    cs_t = cs.T
    dt_t = dt.T
    dte_t = dte.T

    for g in range(N_GROUPS):
        bg = b_ref[:, g * D_STATE:(g + 1) * D_STATE]
        bt = bg.astype(F32).T
        if with_y:
            cg = c_ref[:, g * D_STATE:(g + 1) * D_STATE]
            cg32 = cg.astype(F32)
            cb = lax.dot_general(cg, bg, (((1,), (1,)), ((), ())), preferred_element_type=F32)
        for k in range(HEADS_PER_GROUP // 2):
            pair = g * (HEADS_PER_GROUP // 2) + k
            lanes = slice(pair * LANES, (pair + 1) * LANES)
            xp = x_ref[:, lanes]
            xz = jnp.zeros_like(xp)
            xa = jnp.where(first_half, xp, xz)
            xb = jnp.where(first_half, xz, xp)
            sp = s_scr[g, :, k * LANES:(k + 1) * LANES]
            lhs_y = []
            lhs_s = []
            for hh in (2 * pair, 2 * pair + 1):
                row_dte = dte_t[hh:hh + 1, :]
                lhs_s.append((bt * row_dte).astype(BF16))
                if with_y:
                    colb = jnp.broadcast_to(cs[:, hh:hh + 1], (q, q))
                    rowb = cs_t[hh:hh + 1, :]
                    decay = jnp.exp(jnp.where(tri, colb - rowb, -jnp.inf))
                    lhs_y.append((cb * decay * dt_t[hh:hh + 1, :]).astype(BF16))
                    lhs_y.append((cg32 * jnp.exp(colb)).astype(BF16))
            if with_y:
                sz = jnp.zeros_like(sp)
                sa = jnp.where(first_half, sp, sz).astype(BF16)
                sb = jnp.where(first_half, sz, sp).astype(BF16)
                y = jnp.dot(jnp.concatenate(lhs_y, axis=1), jnp.concatenate([xa, sa, xb, sb], axis=0),
                            preferred_element_type=F32)
                y_ref[:, lanes] = y.astype(y_ref.dtype)
            upd = jnp.dot(jnp.concatenate(lhs_s, axis=1), jnp.concatenate([xa, xb], axis=0),
                          preferred_element_type=F32)
            h_a = 2 * pair
            dec_pair = jnp.where(first_half[0:1, :],
                                 jnp.broadcast_to(dec[:, h_a:h_a + 1], (1, LANES)),
                                 jnp.broadcast_to(dec[:, h_a + 1:h_a + 2], (1, LANES)))
            s_scr[g, :, k * LANES:(k + 1) * LANES] = sp * dec_pair + upd

    if not with_y:
        @pl.when(c == nc - 1)
        def _():
            hout_ref[...] = s_scr[...]


def _ssd(xbc, dt_raw, dt_bias2, a_log2, h0, with_y):
    bsz, n, _ = xbc.shape
    nc = n // CHUNK
    d_inner = N_HEADS * HEAD_DIM
    gw = HEADS_PER_GROUP * HEAD_DIM
    x_blk = d_inner // BC_WIDTH

    def ceff(dd, cc):
        return cc + dd * (nc - 1 - 2 * cc)

    in_specs = [pl.BlockSpec((None, CHUNK, d_inner), lambda b, dd, cc: (b, ceff(dd, cc), 0)),
                pl.BlockSpec((None, CHUNK, BC_WIDTH), lambda b, dd, cc: (b, ceff(dd, cc), x_blk + dd))]
    args = [xbc, xbc]
    if with_y:
        in_specs.append(pl.BlockSpec((None, CHUNK, BC_WIDTH), lambda b, dd, cc: (b, ceff(dd, cc), x_blk + 2 + dd)))
        args.append(xbc)
    in_specs += [pl.BlockSpec((None, CHUNK, LANES), lambda b, dd, cc: (b, ceff(dd, cc), dd)),
                 pl.BlockSpec((None, 1, LANES), lambda b, dd, cc: (dd, 0, 0)),
                 pl.BlockSpec((None, 1, LANES), lambda b, dd, cc: (dd, 0, 0)),
                 pl.BlockSpec((None, None, N_GROUPS, D_STATE, gw), lambda b, dd, cc: (b, dd, 0, 0, 0))]
    args += [dt_raw, dt_bias2, a_log2, h0]
    if with_y:
        out_specs = pl.BlockSpec((None, None, CHUNK, d_inner), lambda b, dd, cc: (dd, b, ceff(dd, cc), 0))
        out_shape = jax.ShapeDtypeStruct((2, bsz, n, d_inner), BF16)
    else:
        out_specs = pl.BlockSpec((None, None, N_GROUPS, D_STATE, gw), lambda b, dd, cc: (b, dd, 0, 0, 0))
        out_shape = jax.ShapeDtypeStruct((bsz, 2, N_GROUPS, D_STATE, gw), F32)
    return pl.pallas_call(
        functools.partial(_ssd_kernel, with_y=with_y, nc=nc),
        grid=(bsz, 2, nc),
        in_specs=in_specs,
        out_specs=out_specs,
        out_shape=out_shape,
        scratch_shapes=[pltpu.VMEM((N_GROUPS, D_STATE, gw), F32)],
        compiler_params=_cparams("arbitrary", "arbitrary", "arbitrary"),
        name="ssd_scan" if with_y else "ssd_ctx_state",
    )(*args)


def _dft_mats(n):
    k = np.arange(n)
    ang = 2.0 * np.pi * ((k[:, None] * k[None, :]) % n) / n
    return np.cos(ang), np.sin(ang)


def _fnet_chan_kernel(u_ref, m_ref, o_ref):
    m = m_ref[...]
    for g in range(F_GROUPS):
        ug = u_ref[:, g * F_GROUP_DIM:(g + 1) * F_GROUP_DIM]
        pq = jnp.dot(ug, m, preferred_element_type=F32)
        o_ref[0, :, g * F_GROUP_DIM:(g + 1) * F_GROUP_DIM] = pq[:, :F_GROUP_DIM].astype(o_ref.dtype)
        o_ref[1, :, g * F_GROUP_DIM:(g + 1) * F_GROUP_DIM] = pq[:, F_GROUP_DIM:].astype(o_ref.dtype)


def _fnet_stage1_kernel(z_ref, m1_ref, twc_ref, tws_ref, o_ref, *, tcs, width):
    r = z_ref.shape[1]
    zz = jnp.concatenate([z_ref[0], z_ref[1]], axis=0)
    u = jnp.dot(m1_ref[...], zz, preferred_element_type=F32)
    for j in range(tcs):
        ur = u[:r, j * width:(j + 1) * width]
        ui = u[r:, j * width:(j + 1) * width]
        tc = twc_ref[j]
        ts = tws_ref[j]
        o_ref[0, j] = (ur * tc + ui * ts).astype(o_ref.dtype)
        o_ref[1, j] = (ui * tc - ur * ts).astype(o_ref.dtype)


def _fnet_stage2_kernel(u_ref, m2_ref, o_ref):
    uu = jnp.concatenate([u_ref[0], u_ref[1]], axis=0)
    o_ref[...] = jnp.dot(m2_ref[...], uu, preferred_element_type=F32).astype(o_ref.dtype)


def _fnet_mix(p2d, col_blk, bsz, n):
    m = p2d.shape[0]
    width = F_GROUPS * F_GROUP_DIM
    cgrid = GRID_W
    rgrid = n // cgrid
    scale = 1.0 / math.sqrt(n * F_GROUP_DIM)

    cc, sc = _dft_mats(F_GROUP_DIM)
    chan = jnp.asarray(np.concatenate([cc, sc], axis=1) * scale, BF16)
    tm = 512
    pq = pl.pallas_call(
        _fnet_chan_kernel,
        grid=(m // tm,),
        in_specs=[pl.BlockSpec((tm, width), lambda i: (i, col_blk)),
                  pl.BlockSpec((F_GROUP_DIM, 2 * F_GROUP_DIM), lambda i: (0, 0))],
        out_specs=pl.BlockSpec((2, tm, width), lambda i: (0, i, 0)),
        out_shape=jax.ShapeDtypeStruct((2, m, width), BF16),
        compiler_params=_cparams("arbitrary"),
        name="fnet_chan",
    )(p2d, chan)

    cr, sr = _dft_mats(rgrid)
    m1 = jnp.asarray(np.block([[cr, -sr], [-sr, -cr]]), BF16)
    bb = np.arange(rgrid)[None, :]
    ci = np.arange(cgrid)[:, None]
    ang = 2.0 * np.pi * ((ci * bb) % n) / n
    twc = jnp.asarray(np.cos(ang)[:, :, None], F32)
    tws = jnp.asarray(np.sin(ang)[:, :, None], F32)
    tcs = 4
    z4 = pq.reshape(2, bsz, rgrid, cgrid * width)
    u5 = pl.pallas_call(
        functools.partial(_fnet_stage1_kernel, tcs=tcs, width=width),
        grid=(bsz, cgrid // tcs),
        in_specs=[pl.BlockSpec((2, None, rgrid, tcs * width), lambda b, j: (0, b, 0, j)),
                  pl.BlockSpec((2 * rgrid, 2 * rgrid), lambda b, j: (0, 0)),
                  pl.BlockSpec((tcs, rgrid, 1), lambda b, j: (j, 0, 0)),
                  pl.BlockSpec((tcs, rgrid, 1), lambda b, j: (j, 0, 0))],
        out_specs=pl.BlockSpec((2, None, tcs, rgrid, width), lambda b, j: (0, b, j, 0, 0)),
        out_shape=jax.ShapeDtypeStruct((2, bsz, cgrid, rgrid, width), BF16),
        compiler_params=_cparams("arbitrary", "arbitrary"),
        name="fnet_stage1",
    )(z4, m1, twc, tws)

    cc2, sc2 = _dft_mats(cgrid)
    m2 = jnp.asarray(np.concatenate([cc2, sc2], axis=1), BF16)
    tn = min(4096, rgrid * width)
    u4 = u5.reshape(2, bsz, cgrid, rgrid * width)
    out = pl.pallas_call(
        _fnet_stage2_kernel,
        grid=(bsz, (rgrid * width) // tn),
        in_specs=[pl.BlockSpec((2, None, cgrid, tn), lambda b, j: (0, b, 0, j)),
                  pl.BlockSpec((cgrid, 2 * cgrid), lambda b, j: (0, 0))],
        out_specs=pl.BlockSpec((None, cgrid, tn), lambda b, j: (b, 0, j)),
        out_shape=jax.ShapeDtypeStruct((bsz, cgrid, rgrid * width), BF16),
        compiler_params=_cparams("arbitrary", "arbitrary"),
        name="fnet_stage2",
    )(u4, m2)
    return out.reshape(m, width)


def _tail_kernel(yf_ref, yb_ref, xs_ref, z_ref, ff_ref, gf_ref, gs_ref, x_ref, g1_ref, sc2_ref, sh2_ref,
                 dsk_ref, snw_ref, wso_ref, wfo_ref, wo_ref, n2w_ref, wr_ref, br_ref,
                 x1_ref, h2_ref, ti_ref, tw_ref, rk_ref, cnt_ref, carry_scr):
    y = yf_ref[0].astype(F32) + yb_ref[0].astype(F32) + dsk_ref[...] * xs_ref[...].astype(F32)
    g = y * _silu(z_ref[...].astype(F32))
    inv = lax.rsqrt(jnp.mean(g * g, axis=-1, keepdims=True) + EPS)
    gn = (g * inv * snw_ref[...]).astype(BF16)
    y_ssd = jnp.dot(gn, wso_ref[...], preferred_element_type=F32)
    y_four = jnp.dot(ff_ref[...], wfo_ref[...], preferred_element_type=F32)
    t = _sigmoid(gf_ref[...].astype(F32)) * y_four + _sigmoid(gs_ref[...].astype(F32)) * y_ssd
    mix = jnp.dot(t.astype(BF16), wo_ref[...], preferred_element_type=F32)
    x1 = x_ref[...] + g1_ref[...] * mix
    x1_ref[...] = x1
    inv2 = lax.rsqrt(jnp.mean(x1 * x1, axis=-1, keepdims=True) + EPS)
    h2 = (x1 * inv2 * n2w_ref[...]) * (1.0 + sc2_ref[...]) + sh2_ref[...]
    h2_ref[...] = h2
    h_hi = h2.astype(BF16)
    h_lo = (h2 - h_hi.astype(F32)).astype(BF16)
    wr = wr_ref[...]
    w_hi = wr.astype(BF16)
    w_lo = (wr - w_hi.astype(F32)).astype(BF16)
    logits = (jnp.dot(h_hi, w_hi, preferred_element_type=F32)
              + jnp.dot(h_hi, w_lo, preferred_element_type=F32)
              + jnp.dot(h_lo, w_hi, preferred_element_type=F32)) + br_ref[...]
    tm = logits.shape[0]
    lane = lax.broadcasted_iota(jnp.int32, (tm, LANES), 1)
    vals, idxs = [], []
    cur = logits
    for _ in range(TOP_K):
        mx = jnp.max(cur, axis=-1, keepdims=True)
        ix = jnp.min(jnp.where(cur == mx, lane, LANES), axis=-1, keepdims=True)
        vals.append(mx)
        idxs.append(ix)
        cur = jnp.where(lane == ix, NEG_BIG * 2.0, cur)
    es = [jnp.exp(v - vals[0]) for v in vals]
    den = es[0] + es[1] + es[2] + es[3]
    ti = jnp.zeros((tm, LANES), jnp.int32)
    tw = jnp.zeros((tm, LANES), F32)
    for k in range(TOP_K):
        ti = jnp.where(lane == k, idxs[k], ti)
        tw = jnp.where(lane == k, es[k] / den, tw)
    ti_ref[...] = ti
    tw_ref[...] = tw

    @pl.when(pl.program_id(0) == 0)
    def _():
        carry_scr[...] = jnp.zeros(carry_scr.shape, F32)

    onehots = [jnp.where(lane == ix, 1.0, 0.0) for ix in idxs]
    cnt = onehots[0] + onehots[1] + onehots[2] + onehots[3]
    r_i = lax.broadcasted_iota(jnp.int32, (tm, tm), 0)
    c_i = lax.broadcasted_iota(jnp.int32, (tm, tm), 1)
    earlier = jnp.where(r_i > c_i, 1.0, 0.0).astype(BF16)
    before = jnp.dot(earlier, cnt.astype(BF16), preferred_element_type=F32) + carry_scr[0:1, :]
    rk = jnp.zeros((tm, LANES), jnp.int32)
    for k in range(TOP_K):
        rank_k = jnp.sum(onehots[k] * before, axis=-1, keepdims=True)
        rk = jnp.where(lane == k, rank_k.astype(jnp.int32), rk)
    rk_ref[...] = rk
    total = carry_scr[...] + jnp.sum(cnt, axis=0, keepdims=True)
    carry_scr[...] = total
    cnt_ref[...] = total


def _tail(y2, xbc2d, p2d, ff, x2d, g1, sc2, sh2, dsk, snw, wso, wfo, wo, n2w, wr, br, rows_per_mod, tm):
    m, d = x2d.shape
    di = y2.shape[-1]
    zb = (2 * di) // di
    fb = (3 * di) // d
    row = lambda i: (i, 0)
    modrow = lambda i: ((i * tm) // rows_per_mod, 0, 0)
    const = lambda i: (0, 0)
    in_specs = [pl.BlockSpec((1, tm, di), lambda i: (0, i, 0)),
                pl.BlockSpec((1, tm, di), lambda i: (1, i, 0)),
                pl.BlockSpec((tm, di), row),
                pl.BlockSpec((tm, di), lambda i: (i, zb)),
                pl.BlockSpec((tm, d), row),
                pl.BlockSpec((tm, d), lambda i: (i, fb + 1)),
                pl.BlockSpec((tm, d), lambda i: (i, fb + 2)),
                pl.BlockSpec((tm, d), row),
                pl.BlockSpec((None, 1, d), modrow),
                pl.BlockSpec((None, 1, d), modrow),
                pl.BlockSpec((None, 1, d), modrow),
                pl.BlockSpec((1, di), const),
                pl.BlockSpec((1, di), const),
                pl.BlockSpec((di, d), const),
                pl.BlockSpec((d, d), const),
                pl.BlockSpec((d, d), const),
                pl.BlockSpec((1, d), const),
                pl.BlockSpec((d, LANES), const),
                pl.BlockSpec((1, LANES), const)]
    out_specs = [pl.BlockSpec((tm, d), row), pl.BlockSpec((tm, d), row),
                 pl.BlockSpec((tm, LANES), row), pl.BlockSpec((tm, LANES), row), pl.BlockSpec((tm, LANES), row),
                 pl.BlockSpec((8, LANES), const)]
    out_shape = [jax.ShapeDtypeStruct((m, d), F32), jax.ShapeDtypeStruct((m, d), F32),
                 jax.ShapeDtypeStruct((m, LANES), jnp.int32), jax.ShapeDtypeStruct((m, LANES), F32),
                 jax.ShapeDtypeStruct((m, LANES), jnp.int32), jax.ShapeDtypeStruct((8, LANES), F32)]
    return pl.pallas_call(
        _tail_kernel,
        grid=(m // tm,),
        in_specs=in_specs,
        out_specs=out_specs,
        out_shape=out_shape,
        scratch_shapes=[pltpu.VMEM((8, LANES), F32)],
        compiler_params=_cparams("arbitrary"),
        name="tail",
    )(y2, y2, xbc2d, p2d, ff, p2d, p2d, x2d, g1, sc2, sh2, dsk, snw, wso, wfo, wo, n2w, wr, br)


def _gather_rows(src, idx):
    n_out = idx.shape[0]
    width = src.shape[1]
    win = SC_GATHER_WINDOW
    info = plsc.get_sparse_core_info()
    n_cores, n_workers = info.num_cores, info.num_cores * info.num_subcores
    per_worker = n_out // n_workers
    assert per_worker * n_workers == n_out and per_worker % win == 0
    mesh = plsc.VectorSubcoreMesh(core_axis_name="core", subcore_axis_name="subcore")

    @functools.partial(
        pl.kernel, out_type=jax.ShapeDtypeStruct((n_out, width), src.dtype), mesh=mesh,
        scratch_types=[pltpu.VMEM((win,), jnp.int32), pltpu.VMEM((win, width), src.dtype), pltpu.SemaphoreType.DMA],
        name="gather_rows")
    def gather(src_hbm, idx_hbm, out_hbm, idx_v, rows_v, sem):
        worker = lax.axis_index("subcore") * n_cores + lax.axis_index("core")
        base = worker * per_worker

        @pl.loop(0, per_worker, step=win)
        def _(off):
            pltpu.sync_copy(idx_hbm.at[pl.ds(base + off, win)], idx_v)
            pltpu.async_copy(src_hbm.at[idx_v], rows_v, sem).wait()
            pltpu.sync_copy(rows_v, out_hbm.at[pl.ds(base + off, win)])

    return gather(src, idx)


def _expert_kernel(te_ref, nu_ref, x_ref, wgu_ref, bgu_ref, wd_ref, bd_ref, o_ref):
    @pl.when(pl.program_id(0) < nu_ref[0])
    def _():
        dff = wd_ref.shape[0]
        gu = jnp.dot(x_ref[...].astype(BF16), wgu_ref[...], preferred_element_type=F32) + bgu_ref[...]
        gate = jnp.minimum(gu[:, :dff], SWIGLU_LIMIT)
        up = jnp.clip(gu[:, dff:], -SWIGLU_LIMIT, SWIGLU_LIMIT)
        act = (up + 1.0) * gate * _sigmoid(SWIGLU_ALPHA * gate)
        o_ref[...] = jnp.dot(act.astype(BF16), wd_ref[...], preferred_element_type=F32) + bd_ref[...]

    @pl.when(pl.program_id(0) >= nu_ref[0])
    def _():
        o_ref[...] = jnp.zeros(o_ref.shape, o_ref.dtype)


def _experts(xs, tile_expert, n_used, wgu, bgu, wd, bd):
    rows, d = xs.shape
    tm = MOE_TILE
    n_tiles = rows // tm
    dff2 = wgu.shape[-1]

    def tile(i, te, nu):
        return (jnp.minimum(i, nu[0] - 1), 0)

    def wsel(i, te, nu):
        return (te[jnp.minimum(i, nu[0] - 1)], 0, 0)

    grid_spec = pltpu.PrefetchScalarGridSpec(
        num_scalar_prefetch=2,
        grid=(n_tiles,),
        in_specs=[pl.BlockSpec((tm, d), tile),
                  pl.BlockSpec((None, d, dff2), wsel),
                  pl.BlockSpec((None, 1, dff2), wsel),
                  pl.BlockSpec((None, dff2 // 2, d), wsel),
                  pl.BlockSpec((None, 1, d), wsel)],
        out_specs=pl.BlockSpec((tm, d), lambda i, te, nu: (i, 0)),
    )
    return pl.pallas_call(
        _expert_kernel,
        grid_spec=grid_spec,
        out_shape=jax.ShapeDtypeStruct((rows, d), F32),
        compiler_params=_cparams("arbitrary"),
        name="experts",
    )(tile_expert, n_used, xs, wgu, bgu, wd, bd)


def _final_kernel(x1_ref, ya_ref, yb_ref, yc_ref, yd_ref, tw_ref, g2_ref, fw_ref, o_ref):
    tw = tw_ref[...]
    acc = jnp.zeros(x1_ref.shape, F32)
    for k, y_ref in enumerate((ya_ref, yb_ref, yc_ref, yd_ref)):
        acc = acc + tw[:, k:k + 1] * y_ref[...]
    x2 = x1_ref[...] + g2_ref[...] * acc
    inv = lax.rsqrt(jnp.mean(x2 * x2, axis=-1, keepdims=True) + EPS)
    o_ref[...] = x2 * inv * fw_ref[...]


def _final(x1, y4, tw, g2, fw, rows_per_mod, tm):
    m, d = x1.shape
    return pl.pallas_call(
        _final_kernel,
        grid=(m // tm,),
        in_specs=[pl.BlockSpec((tm, d), lambda i: (i, 0))]
        + [pl.BlockSpec((None, tm, d), functools.partial(lambda k, i: (k, i, 0), k)) for k in range(TOP_K)]
        + [pl.BlockSpec((tm, LANES), lambda i: (i, 0)),
                  pl.BlockSpec((None, 1, d), lambda i: ((i * tm) // rows_per_mod, 0, 0)),
                  pl.BlockSpec((1, d), lambda i: (0, 0))],
        out_specs=pl.BlockSpec((tm, d), lambda i: (i, 0)),
        out_shape=jax.ShapeDtypeStruct((m, d), F32),
        compiler_params=_cparams("arbitrary"),
        name="final",
    )(x1, y4, y4, y4, y4, tw, g2, fw.reshape(1, d))


def _dispatch_plan(e, rank, counts, tm):
    n_assign = e.size
    padded = (counts + tm - 1) // tm * tm
    pad_end = jnp.cumsum(padded)
    pad_start = pad_end - padded
    start = jnp.cumsum(counts) - counts
    dest = (pad_start[e] + rank).astype(jnp.int32)
    n_tiles = n_assign // tm + N_EXPERTS
    tile_start = jnp.arange(n_tiles, dtype=jnp.int32) * tm
    tile_expert = jnp.minimum(jnp.sum(pad_end[None, :] <= tile_start[:, None], axis=1), N_EXPERTS - 1).astype(jnp.int32)
    n_used = (pad_end[-1] // tm).astype(jnp.int32).reshape(1)
    order = jnp.argsort(e.reshape(-1)).astype(jnp.int32)
    er = jnp.repeat(tile_expert, tm)
    j = jnp.arange(n_tiles * tm, dtype=jnp.int32) - pad_start[er].astype(jnp.int32)
    src = order[jnp.clip(start[er].astype(jnp.int32) + j, 0, n_assign - 1)]
    row_tok = jnp.where(j < counts[er], src // TOP_K, 0).astype(jnp.int32)
    return row_tok, tile_expert, n_used, dest


def kernel(x, c, ctx, c_ctx, w_mod, b_mod, norm1_w, norm2_w, w_in, conv_w, conv_b, dt_bias, a_log, d_skip,
           ssd_norm_w, w_ssd_out, w_four_out, w_o, w_router, b_router, w_gate_up, b_gate_up, w_down, b_down,
           final_norm_w):
    bsz, n, d = x.shape
    n_ctx = ctx.shape[1]
    depth = w_mod.shape[0]
    assert depth == 1, "a stacked model would also need the context stream's residual update"
    d_inner = N_HEADS * HEAD_DIM
    off_dt = d_inner + 2 * BC_WIDTH
    m = bsz * n
    x2d = x.reshape(m, d)
    tm = min(512, n)

    for layer in range(depth):
        rows = -(-(bsz + 1) // 8) * 8
        c_rows = jnp.zeros((rows, d), F32).at[:bsz].set(c).at[bsz].set(c_ctx)
        mod = _modulation(c_rows, w_mod[layer], b_mod[layer])
        mods = [mod[:bsz, i * d:(i + 1) * d].reshape(bsz, 1, d) for i in range(N_MOD)]
        sh1, sc1, g1, sh2, sc2, g2 = mods
        sh1_c = mod[bsz:bsz + 1, :d].reshape(1, 1, d)
        sc1_c = mod[bsz:bsz + 1, d:2 * d].reshape(1, 1, d)

        wl = w_in[layer]
        w_main = jnp.concatenate([wl[:, :off_dt], wl[:, off_dt + 2 * N_HEADS:]], axis=1).astype(BF16)
        w_dt = jnp.zeros((d, 2 * LANES), F32)
        w_dt = w_dt.at[:, :N_HEADS].set(wl[:, off_dt:off_dt + N_HEADS])
        w_dt = w_dt.at[:, LANES:LANES + N_HEADS].set(wl[:, off_dt + N_HEADS:off_dt + 2 * N_HEADS]).astype(BF16)
        pad = jnp.zeros((2, 1, LANES - N_HEADS), F32)
        dt_bias2 = jnp.concatenate([dt_bias[layer].reshape(2, 1, N_HEADS), pad], axis=-1)
        a_log2 = jnp.concatenate([a_log[layer].reshape(2, 1, N_HEADS), pad], axis=-1)

        w_ctx = w_main[:, :off_dt]
        pc, dtc = _in_proj(ctx.reshape(bsz * n_ctx, d), norm1_w[layer], sc1_c, sh1_c, w_ctx, w_dt,
                           bsz * n_ctx, min(512, n_ctx))
        xbc_c = _conv_silu(pc.reshape(bsz, n_ctx, off_dt), conv_w[layer][:, :off_dt], conv_b[layer][:off_dt], off_dt)
        h0 = jnp.zeros((bsz, 2, N_GROUPS, D_STATE, HEADS_PER_GROUP * HEAD_DIM), F32)
        h_ctx = _ssd(xbc_c, dtc.reshape(bsz, n_ctx, 2 * LANES), dt_bias2, a_log2, h0, with_y=False)

        p, dtl = _in_proj(x2d, norm1_w[layer], sc1, sh1, w_main, w_dt, n, tm)
        n_conv = off_dt + 2 * BC_WIDTH
        xbc = _conv_silu(p.reshape(bsz, n, -1), conv_w[layer], conv_b[layer], n_conv)
        y2 = _ssd(xbc, dtl.reshape(bsz, n, 2 * LANES), dt_bias2, a_log2, h_ctx, with_y=True)
        ff = _fnet_mix(p, (n_conv + d_inner) // d, bsz, n)

        dsk = jnp.repeat(d_skip[layer].astype(F32), HEAD_DIM).reshape(1, d_inner)
        wr = jnp.zeros((d, LANES), F32).at[:, :N_EXPERTS].set(w_router[layer])
        br = jnp.full((1, LANES), NEG_BIG, F32).at[0, :N_EXPERTS].set(b_router[layer])
        x1, h2, ti, tw, rk, cnt = _tail(
            y2.reshape(2, m, d_inner), xbc.reshape(m, n_conv), p, ff, x2d, g1, sc2, sh2, dsk,
            ssd_norm_w[layer].reshape(1, d_inner), w_ssd_out[layer].astype(BF16), w_four_out[layer].astype(BF16),
            w_o[layer].astype(BF16), norm2_w[layer].reshape(1, d), wr, br, n, tm)

        counts = cnt[0, :N_EXPERTS].astype(jnp.int32)
        row_tok, tile_expert, n_used, dest = _dispatch_plan(ti[:, :TOP_K], rk[:, :TOP_K], counts, MOE_TILE)
        xs = _gather_rows(h2, row_tok)
        ys = _experts(xs, tile_expert, n_used, w_gate_up[layer].astype(BF16),
                      b_gate_up[layer].reshape(N_EXPERTS, 1, -1), w_down[layer].astype(BF16),
                      b_down[layer].reshape(N_EXPERTS, 1, -1))
        y4 = _gather_rows(ys, dest.T.reshape(-1)).reshape(TOP_K, m, d)
        x2d = _final(x1, y4, tw, g2, final_norm_w, n, tm)
    return x2d.reshape(bsz, n, d)
```

```python
import functools
import math

import numpy as np
import jax
import jax.numpy as jnp
from jax import lax
from jax.experimental import pallas as pl
from jax.experimental.pallas import tpu as pltpu
from jax.experimental.pallas import tpu_sc as plsc

F32 = jnp.float32
BF16 = jnp.bfloat16

EPS = 1e-6
GRID_W = 64
N_MOD = 6
F_GROUPS = 8
F_GROUP_DIM = 128
HEAD_DIM = 64
N_HEADS = 32
N_GROUPS = 4
HEADS_PER_GROUP = N_HEADS // N_GROUPS
D_STATE = 128
BC_WIDTH = N_GROUPS * D_STATE
CONV_K = 5
CHUNK = 128
N_EXPERTS = 32
TOP_K = 4
SWIGLU_LIMIT = 7.0
SWIGLU_ALPHA = 1.702

LANES = 128
VMEM_LIMIT_BYTES = 56 * 1024 * 1024
NEG_BIG = -1e30
MOE_TILE = 512
SC_GATHER_WINDOW = 64


def _cparams(*sem):
    return pltpu.CompilerParams(dimension_semantics=sem, vmem_limit_bytes=VMEM_LIMIT_BYTES)


def _sigmoid(v):
    return 1.0 / (1.0 + jnp.exp(-v))


def _silu(v):
    return v * _sigmoid(v)


def _softplus(v):
    return jnp.maximum(v, 0.0) + jnp.log(1.0 + jnp.exp(-jnp.abs(v)))


def _pack_bf16_pair(v):
    w = v.shape[1] // 2
    bits = lax.bitcast_convert_type(v.astype(BF16).astype(F32), jnp.uint32)
    return bits[:, :w] | (bits[:, w:] >> 16)


def _unpack_bf16_pair(p):
    hi = lax.bitcast_convert_type(p & jnp.uint32(0xFFFF0000), F32)
    lo = lax.bitcast_convert_type(p << 16, F32)
    return hi, lo


def _mod_kernel(c_ref, w_ref, b_ref, o_ref):
    s = _silu(c_ref[...]).astype(BF16)
    o_ref[...] = jnp.dot(s, w_ref[...].astype(BF16), preferred_element_type=F32) + b_ref[...]


def _modulation(c_rows, w_mod, b_mod):
    rows, d = c_rows.shape
    n_out = w_mod.shape[1]
    tn = 1024
    return pl.pallas_call(
        _mod_kernel,
        grid=(n_out // tn,),
        in_specs=[pl.BlockSpec((rows, d), lambda j: (0, 0)),
                  pl.BlockSpec((d, tn), lambda j: (0, j)),
                  pl.BlockSpec((1, tn), lambda j: (0, j))],
        out_specs=pl.BlockSpec((rows, tn), lambda j: (0, j)),
        out_shape=jax.ShapeDtypeStruct((rows, n_out), F32),
        compiler_params=_cparams("arbitrary"),
        name="modulation",
    )(c_rows, w_mod, b_mod.reshape(1, n_out))


_INPROJ_TN = 1024


def _inproj_kernel(x_ref, nw_ref, sc_ref, sh_ref, w_ref, wdt_ref, p_ref, dt_ref):
    x = x_ref[...]
    inv = lax.rsqrt(jnp.mean(x * x, axis=-1, keepdims=True) + EPS)
    h = (x * inv * nw_ref[...]) * (1.0 + sc_ref[...]) + sh_ref[...]
    hb = h.astype(BF16)
    dt_ref[...] = jnp.dot(hb, wdt_ref[...], preferred_element_type=F32)
    for j in range(w_ref.shape[1] // _INPROJ_TN):
        cols = slice(j * _INPROJ_TN, (j + 1) * _INPROJ_TN)
        p_ref[:, cols] = jnp.dot(hb, w_ref[:, cols], preferred_element_type=F32).astype(p_ref.dtype)


def _in_proj(x2d, norm_w, sc, sh, w_main, w_dt, rows_per_mod, tm):
    m, d = x2d.shape
    n = w_main.shape[1]
    ndt = w_dt.shape[1]
    resident = dict(pipeline_mode=pl.Buffered(1))
    return pl.pallas_call(
        _inproj_kernel,
        grid=(m // tm,),
        in_specs=[pl.BlockSpec((tm, d), lambda i: (i, 0)),
                  pl.BlockSpec((1, d), lambda i: (0, 0), **resident),
                  pl.BlockSpec((None, 1, d), lambda i: ((i * tm) // rows_per_mod, 0, 0)),
                  pl.BlockSpec((None, 1, d), lambda i: ((i * tm) // rows_per_mod, 0, 0)),
                  pl.BlockSpec((d, n), lambda i: (0, 0), **resident),
                  pl.BlockSpec((d, ndt), lambda i: (0, 0), **resident)],
        out_specs=[pl.BlockSpec((tm, n), lambda i: (i, 0)),
                   pl.BlockSpec((tm, ndt), lambda i: (i, 0))],
        out_shape=[jax.ShapeDtypeStruct((m, n), BF16),
                   jax.ShapeDtypeStruct((m, ndt), F32)],
        compiler_params=_cparams("arbitrary"),
        name="in_proj",
    )(x2d, norm_w.reshape(1, d), sc, sh, w_main, w_dt)


_CONV_PAD = 8


def _conv_kernel(p_ref, w_ref, b_ref, o_ref, stage_scr, *, n, rc):
    tc = o_ref.shape[-1]
    w = w_ref[...]
    bias = b_ref[...]
    zeros = jnp.zeros((_CONV_PAD, tc), F32)
    stage_scr[0:_CONV_PAD, :] = zeros
    stage_scr[_CONV_PAD + n:2 * _CONV_PAD + n, :] = zeros
    for r0 in range(0, n, rc):
        stage_scr[_CONV_PAD + r0:_CONV_PAD + r0 + rc, :] = p_ref[r0:r0 + rc, :].astype(F32)
    for r0 in range(0, n, rc):
        acc = jnp.broadcast_to(bias, (rc, tc))
        for k in range(CONV_K):
            start = _CONV_PAD + r0 + k - CONV_K // 2
            acc = acc + w[k:k + 1, :] * stage_scr[start:start + rc, :]
        o_ref[r0:r0 + rc, :] = _silu(acc).astype(o_ref.dtype)


def _conv_silu(p3d, conv_w, conv_b, n_ch):
    bsz, n, _ = p3d.shape
    tc = 512
    rc = min(256, n)
    return pl.pallas_call(
        functools.partial(_conv_kernel, n=n, rc=rc),
        grid=(bsz, n_ch // tc),
        in_specs=[pl.BlockSpec((None, n, tc), lambda b, j: (b, 0, j)),
                  pl.BlockSpec((CONV_K, tc), lambda b, j: (0, j)),
                  pl.BlockSpec((1, tc), lambda b, j: (0, j))],
        out_specs=pl.BlockSpec((None, n, tc), lambda b, j: (b, 0, j)),
        out_shape=jax.ShapeDtypeStruct((bsz, n, n_ch), BF16),
        scratch_shapes=[pltpu.VMEM((n + 2 * _CONV_PAD, tc), F32)],
        compiler_params=_cparams("arbitrary", "arbitrary"),
        name="conv_silu",
    )(p3d, conv_w, conv_b.reshape(1, -1))


_SSD_TERMS = 5


def _ssd_chunk_terms(dt_raw, dt_bias, a_log, tri, fwd):
    q = CHUNK
    dt = _softplus(dt_raw + dt_bias)
    da = dt * (-jnp.exp(a_log))
    hi = da.astype(BF16)
    r1 = da - hi.astype(F32)
    mid = r1.astype(BF16)
    lo = (r1 - mid.astype(F32)).astype(BF16)
    ones_tri = jnp.where(tri, 1.0, 0.0).astype(BF16)
    cs3 = jnp.dot(ones_tri, jnp.concatenate([hi, mid, lo], axis=1), preferred_element_type=F32)
    cs = cs3[:, :LANES] + cs3[:, LANES:2 * LANES] + cs3[:, 2 * LANES:]
    tot = jnp.where(fwd, cs[q - 1:q, :], cs[0:1, :])
    dte = dt * jnp.exp(tot - cs)
    dec = jnp.broadcast_to(jnp.exp(tot), (q, LANES))
    return cs, cs.T, dt.T, dte.T, dec


def _ssd_kernel(*refs, with_y, nc):
    if with_y:
        x_ref, b_ref, c_ref, dt_ref, dtn_ref, dtb_ref, alog_ref, h0_ref, y_ref, s_scr, pre_scr = refs
    else:
        x_ref, b_ref, dt_ref, dtn_ref, dtb_ref, alog_ref, h0_ref, hout_ref, s_scr, pre_scr = refs
    q = CHUNK
    d = pl.program_id(1)
    c = pl.program_id(2)
    row = lax.broadcasted_iota(jnp.int32, (q, q), 0)
    col = lax.broadcasted_iota(jnp.int32, (q, q), 1)
    fwd = d == 0
    tri = jnp.where(fwd, row - col, col - row) >= 0
    first_half = col < HEAD_DIM

    @pl.when(c == 0)
    def _():
        s_scr[...] = h0_ref[...]
        for t, v in enumerate(_ssd_chunk_terms(dt_ref[...], dtb_ref[...], alog_ref[...], tri, fwd)):
            pre_scr[0, t] = v

    slot = c % 2
    cs = pre_scr[slot, 0]
    cs_t = pre_scr[slot, 1]
    dt_t = pre_scr[slot, 2]
    dte_t = pre_scr[slot, 3]
    dec = pre_scr[slot, 4, 0:1, :]
    for t, v in enumerate(_ssd_chunk_terms(dtn_ref[...], dtb_ref[...], alog_ref[...], tri, fwd)):
        pre_scr[1 - slot, t] = v

    for g in range(N_GROUPS):
        bg = b_ref[:, g * D_STATE:(g + 1) * D_STATE]
        bt = bg.astype(F32).T
        if with_y:
            cg = c_ref[:, g * D_STATE:(g + 1) * D_STATE]
            cg32 = cg.astype(F32)
            cb = lax.dot_general(cg, bg, (((1,), (1,)), ((), ())), preferred_element_type=F32)
        for k in range(HEADS_PER_GROUP // 2):
            pair = g * (HEADS_PER_GROUP // 2) + k
            lanes = slice(pair * LANES, (pair + 1) * LANES)
            xp = x_ref[:, lanes]
            xz = jnp.zeros_like(xp)
            xa = jnp.where(first_half, xp, xz)
            xb = jnp.where(first_half, xz, xp)
            sp = s_scr[g, :, k * LANES:(k + 1) * LANES]
            lhs_y = []
            lhs_s = []
            for hh in (2 * pair, 2 * pair + 1):
                row_dte = dte_t[hh:hh + 1, :]
                lhs_s.append((bt * row_dte).astype(BF16))
                if with_y:
                    colb = jnp.broadcast_to(cs[:, hh:hh + 1], (q, q))
                    rowb = cs_t[hh:hh + 1, :]
                    decay = jnp.exp(jnp.where(tri, colb - rowb, -jnp.inf))
                    lhs_y.append((cb * decay * dt_t[hh:hh + 1, :]).astype(BF16))
                    lhs_y.append((cg32 * jnp.exp(colb)).astype(BF16))
            if with_y:
                sz = jnp.zeros_like(sp)
                sa = jnp.where(first_half, sp, sz).astype(BF16)
                sb = jnp.where(first_half, sz, sp).astype(BF16)
                y = jnp.dot(jnp.concatenate(lhs_y, axis=1), jnp.concatenate([xa, sa, xb, sb], axis=0),
                            preferred_element_type=F32)
                y_ref[:, lanes] = y.astype(y_ref.dtype)
            upd = jnp.dot(jnp.concatenate(lhs_s, axis=1), jnp.concatenate([xa, xb], axis=0),
                          preferred_element_type=F32)
            h_a = 2 * pair
            dec_pair = jnp.where(first_half[0:1, :],
                                 jnp.broadcast_to(dec[:, h_a:h_a + 1], (1, LANES)),
                                 jnp.broadcast_to(dec[:, h_a + 1:h_a + 2], (1, LANES)))
            s_scr[g, :, k * LANES:(k + 1) * LANES] = sp * dec_pair + upd

    if not with_y:
        @pl.when(c == nc - 1)
        def _():
            hout_ref[...] = s_scr[...]


def _ssd(xbc, dt_raw, dt_bias2, a_log2, h0, with_y):
    bsz, n, _ = xbc.shape
    nc = n // CHUNK
    d_inner = N_HEADS * HEAD_DIM
    gw = HEADS_PER_GROUP * HEAD_DIM
    x_blk = d_inner // BC_WIDTH

    def ceff(dd, cc):
        return cc + dd * (nc - 1 - 2 * cc)

    def cnext(dd, cc):
        return ceff(dd, jnp.minimum(cc + 1, nc - 1))

    in_specs = [pl.BlockSpec((None, CHUNK, d_inner), lambda b, dd, cc: (b, ceff(dd, cc), 0)),
                pl.BlockSpec((None, CHUNK, BC_WIDTH), lambda b, dd, cc: (b, ceff(dd, cc), x_blk + dd))]
    args = [xbc, xbc]
    if with_y:
        in_specs.append(pl.BlockSpec((None, CHUNK, BC_WIDTH), lambda b, dd, cc: (b, ceff(dd, cc), x_blk + 2 + dd)))
        args.append(xbc)
    in_specs += [pl.BlockSpec((None, CHUNK, LANES), lambda b, dd, cc: (b, ceff(dd, cc), dd)),
                 pl.BlockSpec((None, CHUNK, LANES), lambda b, dd, cc: (b, cnext(dd, cc), dd)),
                 pl.BlockSpec((None, 1, LANES), lambda b, dd, cc: (dd, 0, 0)),
                 pl.BlockSpec((None, 1, LANES), lambda b, dd, cc: (dd, 0, 0)),
                 pl.BlockSpec((None, None, N_GROUPS, D_STATE, gw), lambda b, dd, cc: (b, dd, 0, 0, 0))]
    args += [dt_raw, dt_raw, dt_bias2, a_log2, h0]
    if with_y:
        out_specs = pl.BlockSpec((None, None, CHUNK, d_inner), lambda b, dd, cc: (dd, b, ceff(dd, cc), 0))
        out_shape = jax.ShapeDtypeStruct((2, bsz, n, d_inner), BF16)
    else:
        out_specs = pl.BlockSpec((None, None, N_GROUPS, D_STATE, gw), lambda b, dd, cc: (b, dd, 0, 0, 0))
        out_shape = jax.ShapeDtypeStruct((bsz, 2, N_GROUPS, D_STATE, gw), F32)
    return pl.pallas_call(
        functools.partial(_ssd_kernel, with_y=with_y, nc=nc),
        grid=(bsz, 2, nc),
        in_specs=in_specs,
        out_specs=out_specs,
        out_shape=out_shape,
        scratch_shapes=[pltpu.VMEM((N_GROUPS, D_STATE, gw), F32),
                        pltpu.VMEM((2, _SSD_TERMS, CHUNK, LANES), F32)],
        compiler_params=_cparams("arbitrary", "arbitrary", "arbitrary"),
        name="ssd_scan" if with_y else "ssd_ctx_state",
    )(*args)


def _dft_mats(n):
    k = np.arange(n)
    ang = 2.0 * np.pi * ((k[:, None] * k[None, :]) % n) / n
    return np.cos(ang), np.sin(ang)


def _fnet_chan_kernel(u_ref, m_ref, o_ref):
    m = m_ref[...]
    for g in range(F_GROUPS):
        ug = u_ref[:, g * F_GROUP_DIM:(g + 1) * F_GROUP_DIM]
        pq = jnp.dot(ug, m, preferred_element_type=F32)
        o_ref[0, :, g * F_GROUP_DIM:(g + 1) * F_GROUP_DIM] = pq[:, :F_GROUP_DIM].astype(o_ref.dtype)
        o_ref[1, :, g * F_GROUP_DIM:(g + 1) * F_GROUP_DIM] = pq[:, F_GROUP_DIM:].astype(o_ref.dtype)


def _fnet_stage1_kernel(z_ref, m1_ref, twc_ref, tws_ref, o_ref, *, tcs, width):
    r = z_ref.shape[1]
    zz = jnp.concatenate([z_ref[0], z_ref[1]], axis=0)
    u = jnp.dot(m1_ref[...], zz, preferred_element_type=F32)
    for j in range(tcs):
        ur = u[:r, j * width:(j + 1) * width]
        ui = u[r:, j * width:(j + 1) * width]
        tc = twc_ref[j]
        ts = tws_ref[j]
        o_ref[0, j] = (ur * tc + ui * ts).astype(o_ref.dtype)
        o_ref[1, j] = (ui * tc - ur * ts).astype(o_ref.dtype)


def _fnet_stage2_kernel(u_ref, m2_ref, o_ref):
    uu = jnp.concatenate([u_ref[0], u_ref[1]], axis=0)
    o_ref[...] = jnp.dot(m2_ref[...], uu, preferred_element_type=F32).astype(o_ref.dtype)


def _fnet_mix(p2d, col_blk, bsz, n):
    m = p2d.shape[0]
    width = F_GROUPS * F_GROUP_DIM
    cgrid = GRID_W
    rgrid = n // cgrid
    scale = 1.0 / math.sqrt(n * F_GROUP_DIM)

    cc, sc = _dft_mats(F_GROUP_DIM)
    chan = jnp.asarray(np.concatenate([cc, sc], axis=1) * scale, BF16)
    tm = 512
    pq = pl.pallas_call(
        _fnet_chan_kernel,
        grid=(m // tm,),
        in_specs=[pl.BlockSpec((tm, width), lambda i: (i, col_blk)),
                  pl.BlockSpec((F_GROUP_DIM, 2 * F_GROUP_DIM), lambda i: (0, 0))],
        out_specs=pl.BlockSpec((2, tm, width), lambda i: (0, i, 0)),
        out_shape=jax.ShapeDtypeStruct((2, m, width), BF16),
        compiler_params=_cparams("arbitrary"),
        name="fnet_chan",
    )(p2d, chan)

    cr, sr = _dft_mats(rgrid)
    m1 = jnp.asarray(np.block([[cr, -sr], [-sr, -cr]]), BF16)
    bb = np.arange(rgrid)[None, :]
    ci = np.arange(cgrid)[:, None]
    ang = 2.0 * np.pi * ((ci * bb) % n) / n
    twc = jnp.asarray(np.cos(ang)[:, :, None], F32)
    tws = jnp.asarray(np.sin(ang)[:, :, None], F32)
    tcs = 4
    z4 = pq.reshape(2, bsz, rgrid, cgrid * width)
    u5 = pl.pallas_call(
        functools.partial(_fnet_stage1_kernel, tcs=tcs, width=width),
        grid=(bsz, cgrid // tcs),
        in_specs=[pl.BlockSpec((2, None, rgrid, tcs * width), lambda b, j: (0, b, 0, j)),
                  pl.BlockSpec((2 * rgrid, 2 * rgrid), lambda b, j: (0, 0)),
                  pl.BlockSpec((tcs, rgrid, 1), lambda b, j: (j, 0, 0)),
                  pl.BlockSpec((tcs, rgrid, 1), lambda b, j: (j, 0, 0))],
        out_specs=pl.BlockSpec((2, None, tcs, rgrid, width), lambda b, j: (0, b, j, 0, 0)),
        out_shape=jax.ShapeDtypeStruct((2, bsz, cgrid, rgrid, width), BF16),
        compiler_params=_cparams("arbitrary", "arbitrary"),
        name="fnet_stage1",
    )(z4, m1, twc, tws)

    cc2, sc2 = _dft_mats(cgrid)
    m2 = jnp.asarray(np.concatenate([cc2, sc2], axis=1), BF16)
    tn = min(4096, rgrid * width)
    u4 = u5.reshape(2, bsz, cgrid, rgrid * width)
    out = pl.pallas_call(
        _fnet_stage2_kernel,
        grid=(bsz, (rgrid * width) // tn),
        in_specs=[pl.BlockSpec((2, None, cgrid, tn), lambda b, j: (0, b, 0, j)),
                  pl.BlockSpec((cgrid, 2 * cgrid), lambda b, j: (0, 0))],
        out_specs=pl.BlockSpec((None, cgrid, tn), lambda b, j: (b, 0, j)),
        out_shape=jax.ShapeDtypeStruct((bsz, cgrid, rgrid * width), BF16),
        compiler_params=_cparams("arbitrary", "arbitrary"),
        name="fnet_stage2",
    )(u4, m2)
    return out.reshape(m, width)


def _tail_kernel(yf_ref, yb_ref, xs_ref, z_ref, ff_ref, gf_ref, gs_ref, x_ref, g1_ref, sc2_ref, sh2_ref,
                 dsk_ref, snw_ref, wso_ref, wfo_ref, wo_ref, n2w_ref, wr_ref, br_ref,
                 x1_ref, h2_ref, ti_ref, tw_ref, rk_ref, cnt_ref, carry_scr):
    y = yf_ref[0].astype(F32) + yb_ref[0].astype(F32) + dsk_ref[...] * xs_ref[...].astype(F32)
    g = y * _silu(z_ref[...].astype(F32))
    inv = lax.rsqrt(jnp.mean(g * g, axis=-1, keepdims=True) + EPS)
    gn = (g * inv * snw_ref[...]).astype(BF16)
    y_ssd = jnp.dot(gn, wso_ref[...], preferred_element_type=F32)
    y_four = jnp.dot(ff_ref[...], wfo_ref[...], preferred_element_type=F32)
    t = _sigmoid(gf_ref[...].astype(F32)) * y_four + _sigmoid(gs_ref[...].astype(F32)) * y_ssd
    mix = jnp.dot(t.astype(BF16), wo_ref[...], preferred_element_type=F32)
    x1 = x_ref[...] + g1_ref[...] * mix
    x1_ref[...] = x1
    inv2 = lax.rsqrt(jnp.mean(x1 * x1, axis=-1, keepdims=True) + EPS)
    h2 = (x1 * inv2 * n2w_ref[...]) * (1.0 + sc2_ref[...]) + sh2_ref[...]
    h2_ref[...] = _pack_bf16_pair(h2)
    h_hi = h2.astype(BF16)
    h_lo = (h2 - h_hi.astype(F32)).astype(BF16)
    wr = wr_ref[...]
    w_hi = wr.astype(BF16)
    w_lo = (wr - w_hi.astype(F32)).astype(BF16)
    logits = (jnp.dot(h_hi, w_hi, preferred_element_type=F32)
              + jnp.dot(h_hi, w_lo, preferred_element_type=F32)
              + jnp.dot(h_lo, w_hi, preferred_element_type=F32)) + br_ref[...]
    tm = logits.shape[0]
    lane = lax.broadcasted_iota(jnp.int32, (tm, LANES), 1)
    vals, idxs = [], []
    cur = logits
    for _ in range(TOP_K):
        mx = jnp.max(cur, axis=-1, keepdims=True)
        ix = jnp.min(jnp.where(cur == mx, lane, LANES), axis=-1, keepdims=True)
        vals.append(mx)
        idxs.append(ix)
        cur = jnp.where(lane == ix, NEG_BIG * 2.0, cur)
    es = [jnp.exp(v - vals[0]) for v in vals]
    den = es[0] + es[1] + es[2] + es[3]
    ti = jnp.zeros((tm, LANES), jnp.int32)
    tw = jnp.zeros((tm, LANES), F32)
    for k in range(TOP_K):
        ti = jnp.where(lane == k, idxs[k], ti)
        tw = jnp.where(lane == k, es[k] / den, tw)
    ti_ref[...] = ti
    tw_ref[...] = tw

    @pl.when(pl.program_id(0) == 0)
    def _():
        carry_scr[...] = jnp.zeros(carry_scr.shape, F32)

    onehots = [jnp.where(lane == ix, 1.0, 0.0) for ix in idxs]
    cnt = onehots[0] + onehots[1] + onehots[2] + onehots[3]
    r_i = lax.broadcasted_iota(jnp.int32, (tm, tm), 0)
    c_i = lax.broadcasted_iota(jnp.int32, (tm, tm), 1)
    earlier = jnp.where(r_i > c_i, 1.0, 0.0).astype(BF16)
    before = jnp.dot(earlier, cnt.astype(BF16), preferred_element_type=F32) + carry_scr[0:1, :]
    rk = jnp.zeros((tm, LANES), jnp.int32)
    for k in range(TOP_K):
        rank_k = jnp.sum(onehots[k] * before, axis=-1, keepdims=True)
        rk = jnp.where(lane == k, rank_k.astype(jnp.int32), rk)
    rk_ref[...] = rk
    total = carry_scr[...] + jnp.sum(cnt, axis=0, keepdims=True)
    carry_scr[...] = total
    cnt_ref[...] = total


def _tail(y2, xbc2d, p2d, ff, x2d, g1, sc2, sh2, dsk, snw, wso, wfo, wo, n2w, wr, br, rows_per_mod, tm):
    m, d = x2d.shape
    di = y2.shape[-1]
    zb = (2 * di) // di
    fb = (3 * di) // d
    row = lambda i: (i, 0)
    modrow = lambda i: ((i * tm) // rows_per_mod, 0, 0)
    const = lambda i: (0, 0)
    in_specs = [pl.BlockSpec((1, tm, di), lambda i: (0, i, 0)),
                pl.BlockSpec((1, tm, di), lambda i: (1, i, 0)),
                pl.BlockSpec((tm, di), row),
                pl.BlockSpec((tm, di), lambda i: (i, zb)),
                pl.BlockSpec((tm, d), row),
                pl.BlockSpec((tm, d), lambda i: (i, fb + 1)),
                pl.BlockSpec((tm, d), lambda i: (i, fb + 2)),
                pl.BlockSpec((tm, d), row),
                pl.BlockSpec((None, 1, d), modrow),
                pl.BlockSpec((None, 1, d), modrow),
                pl.BlockSpec((None, 1, d), modrow),
                pl.BlockSpec((1, di), const),
                pl.BlockSpec((1, di), const),
                pl.BlockSpec((di, d), const),
                pl.BlockSpec((d, d), const),
                pl.BlockSpec((d, d), const),
                pl.BlockSpec((1, d), const),
                pl.BlockSpec((d, LANES), const),
                pl.BlockSpec((1, LANES), const)]
    out_specs = [pl.BlockSpec((tm, d), row), pl.BlockSpec((tm, d // 2), row),
                 pl.BlockSpec((tm, LANES), row), pl.BlockSpec((tm, LANES), row), pl.BlockSpec((tm, LANES), row),
                 pl.BlockSpec((8, LANES), const)]
    out_shape = [jax.ShapeDtypeStruct((m, d), F32), jax.ShapeDtypeStruct((m, d // 2), jnp.uint32),
                 jax.ShapeDtypeStruct((m, LANES), jnp.int32), jax.ShapeDtypeStruct((m, LANES), F32),
                 jax.ShapeDtypeStruct((m, LANES), jnp.int32), jax.ShapeDtypeStruct((8, LANES), F32)]
    return pl.pallas_call(
        _tail_kernel,
        grid=(m // tm,),
        in_specs=in_specs,
        out_specs=out_specs,
        out_shape=out_shape,
        scratch_shapes=[pltpu.VMEM((8, LANES), F32)],
        compiler_params=_cparams("arbitrary"),
        name="tail",
    )(y2, y2, xbc2d, p2d, ff, p2d, p2d, x2d, g1, sc2, sh2, dsk, snw, wso, wfo, wo, n2w, wr, br)


def _gather_rows(src, idx):
    n_out = idx.shape[0]
    width = src.shape[1]
    win = SC_GATHER_WINDOW
    info = plsc.get_sparse_core_info()
    n_cores, n_workers = info.num_cores, info.num_cores * info.num_subcores
    per_worker = n_out // n_workers
    assert per_worker * n_workers == n_out and per_worker % (2 * win) == 0
    mesh = plsc.VectorSubcoreMesh(core_axis_name="core", subcore_axis_name="subcore")

    @functools.partial(
        pl.kernel, out_type=jax.ShapeDtypeStruct((n_out, width), src.dtype), mesh=mesh,
        scratch_types=[pltpu.VMEM((win,), jnp.int32), pltpu.VMEM((win,), jnp.int32),
                       pltpu.VMEM((win, width), src.dtype), pltpu.VMEM((win, width), src.dtype),
                       pltpu.SemaphoreType.DMA, pltpu.SemaphoreType.DMA,
                       pltpu.SemaphoreType.DMA, pltpu.SemaphoreType.DMA],
        name="gather_rows")
    def gather(src_hbm, idx_hbm, out_hbm, idx_a, idx_b, rows_a, rows_b, gsem_a, gsem_b, ssem_a, ssem_b):
        worker = lax.axis_index("subcore") * n_cores + lax.axis_index("core")
        base = worker * per_worker

        @pl.loop(0, per_worker, step=2 * win)
        def _(off):
            pltpu.sync_copy(idx_hbm.at[pl.ds(base + off, win)], idx_a)
            ga = pltpu.async_copy(src_hbm.at[idx_a], rows_a, gsem_a)
            pltpu.sync_copy(idx_hbm.at[pl.ds(base + off + win, win)], idx_b)
            gb = pltpu.async_copy(src_hbm.at[idx_b], rows_b, gsem_b)
            ga.wait()
            sa = pltpu.async_copy(rows_a, out_hbm.at[pl.ds(base + off, win)], ssem_a)
            gb.wait()
            sb = pltpu.async_copy(rows_b, out_hbm.at[pl.ds(base + off + win, win)], ssem_b)
            sa.wait()
            sb.wait()

    return gather(src, idx)


def _expert_kernel(te_ref, nu_ref, x_ref, wgu_ref, bgu_ref, wd_ref, bd_ref, o_ref, wgu_scr, wd_scr):
    i = pl.program_id(0)
    used = i < nu_ref[0]
    new_expert = (i == 0) | (te_ref[i] != te_ref[jnp.maximum(i - 1, 0)])

    @pl.when(used & new_expert)
    def _():
        wgu_scr[...] = wgu_ref[...].astype(BF16)
        wd_scr[...] = wd_ref[...].astype(BF16)

    @pl.when(used)
    def _():
        dff = wd_ref.shape[0]
        half = wgu_ref.shape[0] // 2
        xa, xb = _unpack_bf16_pair(x_ref[...])
        gu = (jnp.dot(xa.astype(BF16), wgu_scr[:half, :], preferred_element_type=F32)
              + jnp.dot(xb.astype(BF16), wgu_scr[half:, :], preferred_element_type=F32)) + bgu_ref[...]
        gate = jnp.minimum(gu[:, :dff], SWIGLU_LIMIT)
        up = jnp.clip(gu[:, dff:], -SWIGLU_LIMIT, SWIGLU_LIMIT)
        act = (up + 1.0) * gate * _sigmoid(SWIGLU_ALPHA * gate)
        y = jnp.dot(act.astype(BF16), wd_scr[...], preferred_element_type=F32) + bd_ref[...]
        o_ref[...] = _pack_bf16_pair(y)

    @pl.when(jnp.logical_not(used))
    def _():
        o_ref[...] = jnp.zeros(o_ref.shape, o_ref.dtype)


def _experts(xs, tile_expert, n_used, wgu, bgu, wd, bd):
    rows, dh = xs.shape
    d = 2 * dh
    tm = MOE_TILE
    n_tiles = rows // tm
    dff2 = wgu.shape[-1]

    def tile(i, te, nu):
        return (jnp.minimum(i, nu[0] - 1), 0)

    def wsel(i, te, nu):
        return (te[jnp.minimum(i, nu[0] - 1)], 0, 0)

    grid_spec = pltpu.PrefetchScalarGridSpec(
        num_scalar_prefetch=2,
        grid=(n_tiles,),
        in_specs=[pl.BlockSpec((tm, dh), tile),
                  pl.BlockSpec((None, d, dff2), wsel),
                  pl.BlockSpec((None, 1, dff2), wsel),
                  pl.BlockSpec((None, dff2 // 2, d), wsel),
                  pl.BlockSpec((None, 1, d), wsel)],
        out_specs=pl.BlockSpec((tm, dh), lambda i, te, nu: (i, 0)),
        scratch_shapes=[pltpu.VMEM((d, dff2), BF16), pltpu.VMEM((dff2 // 2, d), BF16)],
    )
    return pl.pallas_call(
        _expert_kernel,
        grid_spec=grid_spec,
        out_shape=jax.ShapeDtypeStruct((rows, dh), jnp.uint32),
        compiler_params=_cparams("arbitrary"),
        name="experts",
    )(tile_expert, n_used, xs, wgu, bgu, wd, bd)


def _final_kernel(x1_ref, ya_ref, yb_ref, yc_ref, yd_ref, tw_ref, g2_ref, fw_ref, o_ref):
    d = x1_ref.shape[-1]
    half = d // 2
    tw = tw_ref[...]
    acc_hi = jnp.zeros((x1_ref.shape[0], half), F32)
    acc_lo = jnp.zeros((x1_ref.shape[0], half), F32)
    for k, y_ref in enumerate((ya_ref, yb_ref, yc_ref, yd_ref)):
        y_hi, y_lo = _unpack_bf16_pair(y_ref[...])
        acc_hi = acc_hi + tw[:, k:k + 1] * y_hi
        acc_lo = acc_lo + tw[:, k:k + 1] * y_lo
    x_hi = x1_ref[:, :half] + g2_ref[:, :half] * acc_hi
    x_lo = x1_ref[:, half:] + g2_ref[:, half:] * acc_lo
    ms = (jnp.sum(x_hi * x_hi, axis=-1, keepdims=True) + jnp.sum(x_lo * x_lo, axis=-1, keepdims=True)) / d
    inv = lax.rsqrt(ms + EPS)
    o_ref[:, :half] = x_hi * inv * fw_ref[:, :half]
    o_ref[:, half:] = x_lo * inv * fw_ref[:, half:]


def _final(x1, y4, tw, g2, fw, rows_per_mod, tm):
    m, d = x1.shape
    return pl.pallas_call(
        _final_kernel,
        grid=(m // tm,),
        in_specs=[pl.BlockSpec((tm, d), lambda i: (i, 0))]
        + [pl.BlockSpec((None, tm, d // 2), functools.partial(lambda k, i: (k, i, 0), k)) for k in range(TOP_K)]
        + [pl.BlockSpec((tm, LANES), lambda i: (i, 0)),
           pl.BlockSpec((None, 1, d), lambda i: ((i * tm) // rows_per_mod, 0, 0)),
           pl.BlockSpec((1, d), lambda i: (0, 0))],
        out_specs=pl.BlockSpec((tm, d), lambda i: (i, 0)),
        out_shape=jax.ShapeDtypeStruct((m, d), F32),
        compiler_params=_cparams("arbitrary"),
        name="final",
    )(x1, y4, y4, y4, y4, tw, g2, fw.reshape(1, d))


def _dispatch_plan(e, rank, counts, tm):
    n_assign = e.size
    padded = (counts + tm - 1) // tm * tm
    pad_end = jnp.cumsum(padded)
    pad_start = pad_end - padded
    start = jnp.cumsum(counts) - counts
    dest = (pad_start[e] + rank).astype(jnp.int32)
    n_tiles = n_assign // tm + N_EXPERTS
    tile_start = jnp.arange(n_tiles, dtype=jnp.int32) * tm
    tile_expert = jnp.minimum(jnp.sum(pad_end[None, :] <= tile_start[:, None], axis=1), N_EXPERTS - 1).astype(jnp.int32)
    n_used = (pad_end[-1] // tm).astype(jnp.int32).reshape(1)
    order = jnp.argsort(e.reshape(-1)).astype(jnp.int32)
    er = jnp.repeat(tile_expert, tm)
    j = jnp.arange(n_tiles * tm, dtype=jnp.int32) - pad_start[er].astype(jnp.int32)
    src = order[jnp.clip(start[er].astype(jnp.int32) + j, 0, n_assign - 1)]
    row_tok = jnp.where(j < counts[er], src // TOP_K, 0).astype(jnp.int32)
    return row_tok, tile_expert, n_used, dest


def kernel(x, c, ctx, c_ctx, w_mod, b_mod, norm1_w, norm2_w, w_in, conv_w, conv_b, dt_bias, a_log, d_skip,
           ssd_norm_w, w_ssd_out, w_four_out, w_o, w_router, b_router, w_gate_up, b_gate_up, w_down, b_down,
           final_norm_w):
    bsz, n, d = x.shape
    n_ctx = ctx.shape[1]
    depth = w_mod.shape[0]
    assert depth == 1, "a stacked model would also need the context stream's residual update"
    d_inner = N_HEADS * HEAD_DIM
    off_dt = d_inner + 2 * BC_WIDTH
    m = bsz * n
    x2d = x.reshape(m, d)
    tm = min(512, n)

    for layer in range(depth):
        rows = -(-(bsz + 1) // 8) * 8
        c_rows = jnp.zeros((rows, d), F32).at[:bsz].set(c).at[bsz].set(c_ctx)
        mod = _modulation(c_rows, w_mod[layer], b_mod[layer])
        mods = [mod[:bsz, i * d:(i + 1) * d].reshape(bsz, 1, d) for i in range(N_MOD)]
        sh1, sc1, g1, sh2, sc2, g2 = mods
        sh1_c = mod[bsz:bsz + 1, :d].reshape(1, 1, d)
        sc1_c = mod[bsz:bsz + 1, d:2 * d].reshape(1, 1, d)

        wl = w_in[layer]
        w_main = jnp.concatenate([wl[:, :off_dt], wl[:, off_dt + 2 * N_HEADS:]], axis=1).astype(BF16)
        w_dt = jnp.zeros((d, 2 * LANES), F32)
        w_dt = w_dt.at[:, :N_HEADS].set(wl[:, off_dt:off_dt + N_HEADS])
        w_dt = w_dt.at[:, LANES:LANES + N_HEADS].set(wl[:, off_dt + N_HEADS:off_dt + 2 * N_HEADS]).astype(BF16)
        pad = jnp.zeros((2, 1, LANES - N_HEADS), F32)
        dt_bias2 = jnp.concatenate([dt_bias[layer].reshape(2, 1, N_HEADS), pad], axis=-1)
        a_log2 = jnp.concatenate([a_log[layer].reshape(2, 1, N_HEADS), pad], axis=-1)

        w_ctx = w_main[:, :off_dt]
        pc, dtc = _in_proj(ctx.reshape(bsz * n_ctx, d), norm1_w[layer], sc1_c, sh1_c, w_ctx, w_dt,
                           bsz * n_ctx, min(512, n_ctx))
        xbc_c = _conv_silu(pc.reshape(bsz, n_ctx, off_dt), conv_w[layer][:, :off_dt], conv_b[layer][:off_dt], off_dt)
        h0 = jnp.zeros((bsz, 2, N_GROUPS, D_STATE, HEADS_PER_GROUP * HEAD_DIM), F32)
        h_ctx = _ssd(xbc_c, dtc.reshape(bsz, n_ctx, 2 * LANES), dt_bias2, a_log2, h0, with_y=False)

        p, dtl = _in_proj(x2d, norm1_w[layer], sc1, sh1, w_main, w_dt, n, tm)
        n_conv = off_dt + 2 * BC_WIDTH
        xbc = _conv_silu(p.reshape(bsz, n, -1), conv_w[layer], conv_b[layer], n_conv)
        y2 = _ssd(xbc, dtl.reshape(bsz, n, 2 * LANES), dt_bias2, a_log2, h_ctx, with_y=True)
        ff = _fnet_mix(p, (n_conv + d_inner) // d, bsz, n)

        dsk = jnp.repeat(d_skip[layer].astype(F32), HEAD_DIM).reshape(1, d_inner)
        wr = jnp.zeros((d, LANES), F32).at[:, :N_EXPERTS].set(w_router[layer])
        br = jnp.full((1, LANES), NEG_BIG, F32).at[0, :N_EXPERTS].set(b_router[layer])
        x1, h2, ti, tw, rk, cnt = _tail(
            y2.reshape(2, m, d_inner), xbc.reshape(m, n_conv), p, ff, x2d, g1, sc2, sh2, dsk,
            ssd_norm_w[layer].reshape(1, d_inner), w_ssd_out[layer].astype(BF16), w_four_out[layer].astype(BF16),
            w_o[layer].astype(BF16), norm2_w[layer].reshape(1, d), wr, br, n, tm)

        counts = cnt[0, :N_EXPERTS].astype(jnp.int32)
        row_tok, tile_expert, n_used, dest = _dispatch_plan(ti[:, :TOP_K], rk[:, :TOP_K], counts, MOE_TILE)
        xs = _gather_rows(h2, row_tok)
        ys = _experts(xs, tile_expert, n_used, w_gate_up[layer], b_gate_up[layer].reshape(N_EXPERTS, 1, -1),
                      w_down[layer], b_down[layer].reshape(N_EXPERTS, 1, -1))
        y4 = _gather_rows(ys, dest.T.reshape(-1)).reshape(TOP_K, m, d // 2)
        x2d = _final(x1, y4, tw, g2, final_norm_w, n, tm)
    return x2d.reshape(bsz, n, d)
```

```python
import functools
import math

import numpy as np
import jax
import jax.numpy as jnp
from jax import lax
from jax.experimental import pallas as pl
from jax.experimental.pallas import tpu as pltpu
from jax.experimental.pallas import tpu_sc as plsc

F32 = jnp.float32
BF16 = jnp.bfloat16

EPS = 1e-6
GRID_W = 64
N_MOD = 6
F_GROUPS = 8
F_GROUP_DIM = 128
HEAD_DIM = 64
N_HEADS = 32
N_GROUPS = 4
HEADS_PER_GROUP = N_HEADS // N_GROUPS
D_STATE = 128
BC_WIDTH = N_GROUPS * D_STATE
CONV_K = 5
CHUNK = 128
N_EXPERTS = 32
TOP_K = 4
SWIGLU_LIMIT = 7.0
SWIGLU_ALPHA = 1.702

LANES = 128
VMEM_LIMIT_BYTES = 56 * 1024 * 1024
NEG_BIG = -1e30
MOE_TILE = 512
SC_GATHER_WINDOW = 64


def _cparams(*sem):
    return pltpu.CompilerParams(dimension_semantics=sem, vmem_limit_bytes=VMEM_LIMIT_BYTES)


def _sigmoid(v):
    return 1.0 / (1.0 + jnp.exp(-v))


def _silu(v):
    return v * _sigmoid(v)


def _softplus(v):
    return jnp.maximum(v, 0.0) + jnp.log(1.0 + jnp.exp(-jnp.abs(v)))


def _pack_bf16_pair(v):
    w = v.shape[1] // 2
    bits = lax.bitcast_convert_type(v.astype(BF16).astype(F32), jnp.uint32)
    return bits[:, :w] | (bits[:, w:] >> 16)


def _unpack_bf16_pair(p):
    hi = lax.bitcast_convert_type(p & jnp.uint32(0xFFFF0000), F32)
    lo = lax.bitcast_convert_type(p << 16, F32)
    return hi, lo


def _mod_kernel(c_ref, w_ref, b_ref, o_ref):
    s = _silu(c_ref[...]).astype(BF16)
    o_ref[...] = jnp.dot(s, w_ref[...].astype(BF16), preferred_element_type=F32) + b_ref[...]


def _modulation(c_rows, w_mod, b_mod):
    rows, d = c_rows.shape
    n_out = w_mod.shape[1]
    tn = 1024
    return pl.pallas_call(
        _mod_kernel,
        grid=(n_out // tn,),
        in_specs=[pl.BlockSpec((rows, d), lambda j: (0, 0)),
                  pl.BlockSpec((d, tn), lambda j: (0, j)),
                  pl.BlockSpec((1, tn), lambda j: (0, j))],
        out_specs=pl.BlockSpec((rows, tn), lambda j: (0, j)),
        out_shape=jax.ShapeDtypeStruct((rows, n_out), F32),
        compiler_params=_cparams("arbitrary"),
        name="modulation",
    )(c_rows, w_mod, b_mod.reshape(1, n_out))


_INPROJ_TN = 1024


def _inproj_kernel(x_ref, nw_ref, sc_ref, sh_ref, w_ref, wdt_ref, p_ref, dt_ref):
    x = x_ref[...]
    inv = lax.rsqrt(jnp.mean(x * x, axis=-1, keepdims=True) + EPS)
    h = (x * inv * nw_ref[...]) * (1.0 + sc_ref[...]) + sh_ref[...]
    hb = h.astype(BF16)
    dt_ref[...] = jnp.dot(hb, wdt_ref[...], preferred_element_type=F32)
    for j in range(w_ref.shape[1] // _INPROJ_TN):
        cols = slice(j * _INPROJ_TN, (j + 1) * _INPROJ_TN)
        p_ref[:, cols] = jnp.dot(hb, w_ref[:, cols], preferred_element_type=F32).astype(p_ref.dtype)


def _in_proj(x2d, norm_w, sc, sh, w_main, w_dt, rows_per_mod, tm):
    m, d = x2d.shape
    n = w_main.shape[1]
    ndt = w_dt.shape[1]
    resident = dict(pipeline_mode=pl.Buffered(1))
    return pl.pallas_call(
        _inproj_kernel,
        grid=(m // tm,),
        in_specs=[pl.BlockSpec((tm, d), lambda i: (i, 0)),
                  pl.BlockSpec((1, d), lambda i: (0, 0), **resident),
                  pl.BlockSpec((None, 1, d), lambda i: ((i * tm) // rows_per_mod, 0, 0)),
                  pl.BlockSpec((None, 1, d), lambda i: ((i * tm) // rows_per_mod, 0, 0)),
                  pl.BlockSpec((d, n), lambda i: (0, 0), **resident),
                  pl.BlockSpec((d, ndt), lambda i: (0, 0), **resident)],
        out_specs=[pl.BlockSpec((tm, n), lambda i: (i, 0)),
                   pl.BlockSpec((tm, ndt), lambda i: (i, 0))],
        out_shape=[jax.ShapeDtypeStruct((m, n), BF16),
                   jax.ShapeDtypeStruct((m, ndt), F32)],
        compiler_params=_cparams("arbitrary"),
        name="in_proj",
    )(x2d, norm_w.reshape(1, d), sc, sh, w_main, w_dt)


_CONV_HALO = 16


def _conv_kernel(p_ref, w_ref, b_ref, o_ref, *, n, rc):
    tc = o_ref.shape[-1]
    w = w_ref[...]
    bias = b_ref[...]
    zeros = jnp.zeros((_CONV_HALO, tc), F32)
    tot = rc + 2 * _CONV_HALO
    for r0 in range(0, n, rc):
        top = zeros if r0 == 0 else p_ref[r0 - _CONV_HALO:r0, :].astype(F32)
        bot = zeros if r0 + rc >= n else p_ref[r0 + rc:r0 + rc + _CONV_HALO, :].astype(F32)
        blk = jnp.concatenate([top, p_ref[r0:r0 + rc, :].astype(F32), bot], axis=0)
        acc = jnp.broadcast_to(bias, (rc, tc))
        for k in range(CONV_K):
            shift = (CONV_K // 2 - k) % tot
            rolled = blk if shift == 0 else pltpu.roll(blk, shift, axis=0)
            acc = acc + w[k:k + 1, :] * rolled[_CONV_HALO:_CONV_HALO + rc, :]
        o_ref[r0:r0 + rc, :] = _silu(acc).astype(o_ref.dtype)


def _conv_silu(p3d, conv_w, conv_b, n_ch):
    bsz, n, _ = p3d.shape
    tc = 512
    rc = min(256, n)
    return pl.pallas_call(
        functools.partial(_conv_kernel, n=n, rc=rc),
        grid=(bsz, n_ch // tc),
        in_specs=[pl.BlockSpec((None, n, tc), lambda b, j: (b, 0, j)),
                  pl.BlockSpec((CONV_K, tc), lambda b, j: (0, j)),
                  pl.BlockSpec((1, tc), lambda b, j: (0, j))],
        out_specs=pl.BlockSpec((None, n, tc), lambda b, j: (b, 0, j)),
        out_shape=jax.ShapeDtypeStruct((bsz, n, n_ch), BF16),
        compiler_params=_cparams("arbitrary", "arbitrary"),
        name="conv_silu",
    )(p3d, conv_w, conv_b.reshape(1, -1))


_SSD_TERMS = 5


def _ssd_chunk_terms(dt_raw, dt_bias, a_log, tri, fwd):
    q = CHUNK
    dt = _softplus(dt_raw + dt_bias)
    da = dt * (-jnp.exp(a_log))
    hi = da.astype(BF16)
    r1 = da - hi.astype(F32)
    mid = r1.astype(BF16)
    lo = (r1 - mid.astype(F32)).astype(BF16)
    ones_tri = jnp.where(tri, 1.0, 0.0).astype(BF16)
    cs3 = jnp.dot(ones_tri, jnp.concatenate([hi, mid, lo], axis=1), preferred_element_type=F32)
    cs = cs3[:, :LANES] + cs3[:, LANES:2 * LANES] + cs3[:, 2 * LANES:]
    tot = jnp.where(fwd, cs[q - 1:q, :], cs[0:1, :])
    dte = dt * jnp.exp(tot - cs)
    dec = jnp.broadcast_to(jnp.exp(tot), (q, LANES))
    return cs, cs.T, dt.T, dte.T, dec


def _ssd_kernel(*refs, with_y, nc):
    if with_y:
        x_ref, b_ref, c_ref, dt_ref, dtn_ref, dtb_ref, alog_ref, h0_ref, y_ref, s_scr, pre_scr = refs
    else:
        x_ref, b_ref, dt_ref, dtn_ref, dtb_ref, alog_ref, h0_ref, hout_ref, s_scr, pre_scr = refs
    q = CHUNK
    d = pl.program_id(1)
    c = pl.program_id(2)
    row = lax.broadcasted_iota(jnp.int32, (q, q), 0)
    col = lax.broadcasted_iota(jnp.int32, (q, q), 1)
    fwd = d == 0
    tri = jnp.where(fwd, row - col, col - row) >= 0
    first_half = col < HEAD_DIM

    @pl.when(c == 0)
    def _():
        s_scr[...] = h0_ref[...]
        for t, v in enumerate(_ssd_chunk_terms(dt_ref[...], dtb_ref[...], alog_ref[...], tri, fwd)):
            pre_scr[0, t] = v

    slot = c % 2
    cs = pre_scr[slot, 0]
    cs_t = pre_scr[slot, 1]
    dt_t = pre_scr[slot, 2]
    dte_t = pre_scr[slot, 3]
    dec = pre_scr[slot, 4, 0:1, :]
    for t, v in enumerate(_ssd_chunk_terms(dtn_ref[...], dtb_ref[...], alog_ref[...], tri, fwd)):
        pre_scr[1 - slot, t] = v

    for g in range(N_GROUPS):
        bg = b_ref[:, g * D_STATE:(g + 1) * D_STATE]
        bt = bg.astype(F32).T
        if with_y:
            cg = c_ref[:, g * D_STATE:(g + 1) * D_STATE]
            cg32 = cg.astype(F32)
            cb = lax.dot_general(cg, bg, (((1,), (1,)), ((), ())), preferred_element_type=F32)
        for k in range(HEADS_PER_GROUP // 2):
            pair = g * (HEADS_PER_GROUP // 2) + k
            lanes = slice(pair * LANES, (pair + 1) * LANES)
            xp = x_ref[:, lanes]
            xz = jnp.zeros_like(xp)
            xa = jnp.where(first_half, xp, xz)
            xb = jnp.where(first_half, xz, xp)
            sp = s_scr[g, :, k * LANES:(k + 1) * LANES]
            lhs_y = []
            lhs_s = []
            for hh in (2 * pair, 2 * pair + 1):
                row_dte = dte_t[hh:hh + 1, :]
                lhs_s.append((bt * row_dte).astype(BF16))
                if with_y:
                    colb = jnp.broadcast_to(cs[:, hh:hh + 1], (q, q))
                    rowb = cs_t[hh:hh + 1, :]
                    decay = jnp.exp(jnp.where(tri, colb - rowb, -jnp.inf))
                    lhs_y.append((cb * decay * dt_t[hh:hh + 1, :]).astype(BF16))
                    lhs_y.append((cg32 * jnp.exp(colb)).astype(BF16))
            if with_y:
                sz = jnp.zeros_like(sp)
                sa = jnp.where(first_half, sp, sz).astype(BF16)
                sb = jnp.where(first_half, sz, sp).astype(BF16)
                y = jnp.dot(jnp.concatenate(lhs_y, axis=1), jnp.concatenate([xa, sa, xb, sb], axis=0),
                            preferred_element_type=F32)
                y_ref[:, lanes] = y.astype(y_ref.dtype)
            upd = jnp.dot(jnp.concatenate(lhs_s, axis=1), jnp.concatenate([xa, xb], axis=0),
                          preferred_element_type=F32)
            h_a = 2 * pair
            dec_pair = jnp.where(first_half[0:1, :],
                                 jnp.broadcast_to(dec[:, h_a:h_a + 1], (1, LANES)),
                                 jnp.broadcast_to(dec[:, h_a + 1:h_a + 2], (1, LANES)))
            s_scr[g, :, k * LANES:(k + 1) * LANES] = sp * dec_pair + upd

    if not with_y:
        @pl.when(c == nc - 1)
        def _():
            hout_ref[...] = s_scr[...]


def _ssd(xbc, dt_raw, dt_bias2, a_log2, h0, with_y):
    bsz, n, _ = xbc.shape
    nc = n // CHUNK
    d_inner = N_HEADS * HEAD_DIM
    gw = HEADS_PER_GROUP * HEAD_DIM
    x_blk = d_inner // BC_WIDTH

    def ceff(dd, cc):
        return cc + dd * (nc - 1 - 2 * cc)

    def cnext(dd, cc):
        return ceff(dd, jnp.minimum(cc + 1, nc - 1))

    in_specs = [pl.BlockSpec((None, CHUNK, d_inner), lambda b, dd, cc: (b, ceff(dd, cc), 0)),
                pl.BlockSpec((None, CHUNK, BC_WIDTH), lambda b, dd, cc: (b, ceff(dd, cc), x_blk + dd))]
    args = [xbc, xbc]
    if with_y:
        in_specs.append(pl.BlockSpec((None, CHUNK, BC_WIDTH), lambda b, dd, cc: (b, ceff(dd, cc), x_blk + 2 + dd)))
        args.append(xbc)
    in_specs += [pl.BlockSpec((None, CHUNK, LANES), lambda b, dd, cc: (b, ceff(dd, cc), dd)),
                 pl.BlockSpec((None, CHUNK, LANES), lambda b, dd, cc: (b, cnext(dd, cc), dd)),
                 pl.BlockSpec((None, 1, LANES), lambda b, dd, cc: (dd, 0, 0)),
                 pl.BlockSpec((None, 1, LANES), lambda b, dd, cc: (dd, 0, 0)),
                 pl.BlockSpec((None, None, N_GROUPS, D_STATE, gw), lambda b, dd, cc: (b, dd, 0, 0, 0))]
    args += [dt_raw, dt_raw, dt_bias2, a_log2, h0]
    if with_y:
        out_specs = pl.BlockSpec((None, None, CHUNK, d_inner), lambda b, dd, cc: (dd, b, ceff(dd, cc), 0))
        out_shape = jax.ShapeDtypeStruct((2, bsz, n, d_inner), BF16)
    else:
        out_specs = pl.BlockSpec((None, None, N_GROUPS, D_STATE, gw), lambda b, dd, cc: (b, dd, 0, 0, 0))
        out_shape = jax.ShapeDtypeStruct((bsz, 2, N_GROUPS, D_STATE, gw), F32)
    return pl.pallas_call(
        functools.partial(_ssd_kernel, with_y=with_y, nc=nc),
        grid=(bsz, 2, nc),
        in_specs=in_specs,
        out_specs=out_specs,
        out_shape=out_shape,
        scratch_shapes=[pltpu.VMEM((N_GROUPS, D_STATE, gw), F32),
                        pltpu.VMEM((2, _SSD_TERMS, CHUNK, LANES), F32)],
        compiler_params=_cparams("arbitrary", "arbitrary", "arbitrary"),
        name="ssd_scan" if with_y else "ssd_ctx_state",
    )(*args)


def _dft_mats(n):
    k = np.arange(n)
    ang = 2.0 * np.pi * ((k[:, None] * k[None, :]) % n) / n
    return np.cos(ang), np.sin(ang)


def _fnet_chan_kernel(u_ref, m_ref, o_ref):
    m = m_ref[...]
    for g in range(F_GROUPS):
        ug = u_ref[:, g * F_GROUP_DIM:(g + 1) * F_GROUP_DIM]
        pq = jnp.dot(ug, m, preferred_element_type=F32)
        o_ref[0, :, g * F_GROUP_DIM:(g + 1) * F_GROUP_DIM] = pq[:, :F_GROUP_DIM].astype(o_ref.dtype)
        o_ref[1, :, g * F_GROUP_DIM:(g + 1) * F_GROUP_DIM] = pq[:, F_GROUP_DIM:].astype(o_ref.dtype)


def _stage_lane_blocks(src_ref, scr, rows):
    width = src_ref.shape[-1]
    for t in range(2):
        v = src_ref[t].reshape(rows, width).astype(F32)
        for lb in range(width // LANES):
            scr[t, lb] = v[:, lb * LANES:(lb + 1) * LANES]


def _strided_rows(scr, t, start, count, stride, width):
    return jnp.concatenate([scr[t, lb, pl.ds(start, count, stride=stride), :] for lb in range(width // LANES)],
                           axis=1)


def _fnet_stage1_kernel(z_ref, m1_ref, twc_ref, tws_ref, o_ref, zs_scr, *, tcs):
    r = z_ref.shape[1]
    width = z_ref.shape[-1]
    _stage_lane_blocks(z_ref, zs_scr, r * tcs)
    m1 = m1_ref[...]
    for j in range(tcs):
        zz = jnp.concatenate([_strided_rows(zs_scr, 0, j, r, tcs, width),
                              _strided_rows(zs_scr, 1, j, r, tcs, width)], axis=0).astype(BF16)
        u = jnp.dot(m1, zz, preferred_element_type=F32)
        ur = u[:r]
        ui = u[r:]
        tc = twc_ref[j]
        ts = tws_ref[j]
        o_ref[0, j] = (ur * tc + ui * ts).astype(o_ref.dtype)
        o_ref[1, j] = (ui * tc - ur * ts).astype(o_ref.dtype)


def _fnet_stage2_kernel(u_ref, m2_ref, o_ref, us_scr, os_scr, *, tbs):
    c = u_ref.shape[1]
    width = u_ref.shape[-1]
    _stage_lane_blocks(u_ref, us_scr, c * tbs)
    m2 = m2_ref[...]
    for j in range(tbs):
        uu = jnp.concatenate([_strided_rows(us_scr, 0, j, c, tbs, width),
                              _strided_rows(us_scr, 1, j, c, tbs, width)], axis=0).astype(BF16)
        x = jnp.dot(m2, uu, preferred_element_type=F32)
        for lb in range(width // LANES):
            os_scr[lb, pl.ds(j, c, stride=tbs), :] = x[:, lb * LANES:(lb + 1) * LANES]
    out = jnp.concatenate([os_scr[lb] for lb in range(width // LANES)], axis=1)
    o_ref[...] = out.reshape(c, tbs, width).astype(o_ref.dtype)


def _fnet_mix(p2d, col_blk, bsz, n):
    m = p2d.shape[0]
    width = F_GROUPS * F_GROUP_DIM
    cgrid = GRID_W
    rgrid = n // cgrid
    scale = 1.0 / math.sqrt(n * F_GROUP_DIM)

    cc, sc = _dft_mats(F_GROUP_DIM)
    chan = jnp.asarray(np.concatenate([cc, sc], axis=1) * scale, BF16)
    tm = 512
    pq = pl.pallas_call(
        _fnet_chan_kernel,
        grid=(m // tm,),
        in_specs=[pl.BlockSpec((tm, width), lambda i: (i, col_blk)),
                  pl.BlockSpec((F_GROUP_DIM, 2 * F_GROUP_DIM), lambda i: (0, 0))],
        out_specs=pl.BlockSpec((2, tm, width), lambda i: (0, i, 0)),
        out_shape=jax.ShapeDtypeStruct((2, m, width), BF16),
        compiler_params=_cparams("arbitrary"),
        name="fnet_chan",
    )(p2d, chan)

    cr, sr = _dft_mats(rgrid)
    m1 = jnp.asarray(np.block([[cr, -sr], [-sr, -cr]]), BF16)
    bb = np.arange(rgrid)[None, :]
    ci = np.arange(cgrid)[:, None]
    ang = 2.0 * np.pi * ((ci * bb) % n) / n
    twc = jnp.asarray(np.cos(ang)[:, :, None], F32)
    tws = jnp.asarray(np.sin(ang)[:, :, None], F32)
    tcs = 16
    z5 = pq.reshape(2, bsz, rgrid, cgrid, width)
    u5 = pl.pallas_call(
        functools.partial(_fnet_stage1_kernel, tcs=tcs),
        grid=(bsz, cgrid // tcs),
        in_specs=[pl.BlockSpec((2, None, rgrid, tcs, width), lambda b, j: (0, b, 0, j, 0)),
                  pl.BlockSpec((2 * rgrid, 2 * rgrid), lambda b, j: (0, 0)),
                  pl.BlockSpec((tcs, rgrid, 1), lambda b, j: (j, 0, 0)),
                  pl.BlockSpec((tcs, rgrid, 1), lambda b, j: (j, 0, 0))],
        out_specs=pl.BlockSpec((2, None, tcs, rgrid, width), lambda b, j: (0, b, j, 0, 0)),
        out_shape=jax.ShapeDtypeStruct((2, bsz, cgrid, rgrid, width), BF16),
        scratch_shapes=[pltpu.VMEM((2, width // LANES, rgrid * tcs, LANES), F32)],
        compiler_params=_cparams("arbitrary", "arbitrary"),
        name="fnet_stage1",
    )(z5, m1, twc, tws)

    cc2, sc2 = _dft_mats(cgrid)
    m2 = jnp.asarray(np.concatenate([cc2, sc2], axis=1), BF16)
    tbs = min(16, rgrid)
    out = pl.pallas_call(
        functools.partial(_fnet_stage2_kernel, tbs=tbs),
        grid=(bsz, rgrid // tbs),
        in_specs=[pl.BlockSpec((2, None, cgrid, tbs, width), lambda b, j: (0, b, 0, j, 0)),
                  pl.BlockSpec((cgrid, 2 * cgrid), lambda b, j: (0, 0))],
        out_specs=pl.BlockSpec((None, cgrid, tbs, width), lambda b, j: (b, 0, j, 0)),
        out_shape=jax.ShapeDtypeStruct((bsz, cgrid, rgrid, width), BF16),
        scratch_shapes=[pltpu.VMEM((2, width // LANES, cgrid * tbs, LANES), F32),
                        pltpu.VMEM((width // LANES, cgrid * tbs, LANES), F32)],
        compiler_params=_cparams("arbitrary", "arbitrary"),
        name="fnet_stage2",
    )(u5, m2)
    return out.reshape(m, width)


def _tail_kernel(yf_ref, yb_ref, xs_ref, z_ref, ff_ref, gf_ref, gs_ref, x_ref, g1_ref, sc2_ref, sh2_ref,
                 dsk_ref, snw_ref, wso_ref, wfo_ref, wo_ref, n2w_ref, wr_ref, br_ref,
                 x1_ref, h2_ref, ti_ref, tw_ref, rk_ref, cnt_ref, carry_scr):
    y = yf_ref[0].astype(F32) + yb_ref[0].astype(F32) + dsk_ref[...] * xs_ref[...].astype(F32)
    g = y * _silu(z_ref[...].astype(F32))
    inv = lax.rsqrt(jnp.mean(g * g, axis=-1, keepdims=True) + EPS)
    gn = (g * inv * snw_ref[...]).astype(BF16)
    y_ssd = jnp.dot(gn, wso_ref[...], preferred_element_type=F32)
    y_four = jnp.dot(ff_ref[...], wfo_ref[...], preferred_element_type=F32)
    t = _sigmoid(gf_ref[...].astype(F32)) * y_four + _sigmoid(gs_ref[...].astype(F32)) * y_ssd
    mix = jnp.dot(t.astype(BF16), wo_ref[...], preferred_element_type=F32)
    x1 = x_ref[...] + g1_ref[...] * mix
    x1_ref[...] = x1
    inv2 = lax.rsqrt(jnp.mean(x1 * x1, axis=-1, keepdims=True) + EPS)
    h2 = (x1 * inv2 * n2w_ref[...]) * (1.0 + sc2_ref[...]) + sh2_ref[...]
    h2_ref[...] = _pack_bf16_pair(h2)
    h_hi = h2.astype(BF16)
    h_lo = (h2 - h_hi.astype(F32)).astype(BF16)
    wr = wr_ref[...]
    w_hi = wr.astype(BF16)
    w_lo = (wr - w_hi.astype(F32)).astype(BF16)
    logits = (jnp.dot(h_hi, w_hi, preferred_element_type=F32)
              + jnp.dot(h_hi, w_lo, preferred_element_type=F32)
              + jnp.dot(h_lo, w_hi, preferred_element_type=F32)) + br_ref[...]
    tm = logits.shape[0]
    lane = lax.broadcasted_iota(jnp.int32, (tm, LANES), 1)
    vals, idxs = [], []
    cur = logits
    for _ in range(TOP_K):
        mx = jnp.max(cur, axis=-1, keepdims=True)
        ix = jnp.min(jnp.where(cur == mx, lane, LANES), axis=-1, keepdims=True)
        vals.append(mx)
        idxs.append(ix)
        cur = jnp.where(lane == ix, NEG_BIG * 2.0, cur)
    es = [jnp.exp(v - vals[0]) for v in vals]
    den = es[0] + es[1] + es[2] + es[3]
    ti = jnp.zeros((tm, LANES), jnp.int32)
    tw = jnp.zeros((tm, LANES), F32)
    for k in range(TOP_K):
        ti = jnp.where(lane == k, idxs[k], ti)
        tw = jnp.where(lane == k, es[k] / den, tw)
    ti_ref[...] = ti
    tw_ref[...] = tw

    @pl.when(pl.program_id(0) == 0)
    def _():
        carry_scr[...] = jnp.zeros(carry_scr.shape, F32)

    onehots = [jnp.where(lane == ix, 1.0, 0.0) for ix in idxs]
    cnt = onehots[0] + onehots[1] + onehots[2] + onehots[3]
    r_i = lax.broadcasted_iota(jnp.int32, (tm, tm), 0)
    c_i = lax.broadcasted_iota(jnp.int32, (tm, tm), 1)
    earlier = jnp.where(r_i > c_i, 1.0, 0.0).astype(BF16)
    before = jnp.dot(earlier, cnt.astype(BF16), preferred_element_type=F32) + carry_scr[0:1, :]
    rk = jnp.zeros((tm, LANES), jnp.int32)
    for k in range(TOP_K):
        rank_k = jnp.sum(onehots[k] * before, axis=-1, keepdims=True)
        rk = jnp.where(lane == k, rank_k.astype(jnp.int32), rk)
    rk_ref[...] = rk
    total = carry_scr[...] + jnp.sum(cnt, axis=0, keepdims=True)
    carry_scr[...] = total
    cnt_ref[...] = total


def _tail(y2, xbc2d, p2d, ff, x2d, g1, sc2, sh2, dsk, snw, wso, wfo, wo, n2w, wr, br, rows_per_mod, tm):
    m, d = x2d.shape
    di = y2.shape[-1]
    zb = (2 * di) // di
    fb = (3 * di) // d
    row = lambda i: (i, 0)
    modrow = lambda i: ((i * tm) // rows_per_mod, 0, 0)
    const = lambda i: (0, 0)
    in_specs = [pl.BlockSpec((1, tm, di), lambda i: (0, i, 0)),
                pl.BlockSpec((1, tm, di), lambda i: (1, i, 0)),
                pl.BlockSpec((tm, di), row),
                pl.BlockSpec((tm, di), lambda i: (i, zb)),
                pl.BlockSpec((tm, d), row),
                pl.BlockSpec((tm, d), lambda i: (i, fb + 1)),
                pl.BlockSpec((tm, d), lambda i: (i, fb + 2)),
                pl.BlockSpec((tm, d), row),
                pl.BlockSpec((None, 1, d), modrow),
                pl.BlockSpec((None, 1, d), modrow),
                pl.BlockSpec((None, 1, d), modrow),
                pl.BlockSpec((1, di), const),
                pl.BlockSpec((1, di), const),
                pl.BlockSpec((di, d), const),
                pl.BlockSpec((d, d), const),
                pl.BlockSpec((d, d), const),
                pl.BlockSpec((1, d), const),
                pl.BlockSpec((d, LANES), const),
                pl.BlockSpec((1, LANES), const)]
    out_specs = [pl.BlockSpec((tm, d), row), pl.BlockSpec((tm, d // 2), row),
                 pl.BlockSpec((tm, LANES), row), pl.BlockSpec((tm, LANES), row), pl.BlockSpec((tm, LANES), row),
                 pl.BlockSpec((8, LANES), const)]
    out_shape = [jax.ShapeDtypeStruct((m, d), F32), jax.ShapeDtypeStruct((m, d // 2), jnp.uint32),
                 jax.ShapeDtypeStruct((m, LANES), jnp.int32), jax.ShapeDtypeStruct((m, LANES), F32),
                 jax.ShapeDtypeStruct((m, LANES), jnp.int32), jax.ShapeDtypeStruct((8, LANES), F32)]
    return pl.pallas_call(
        _tail_kernel,
        grid=(m // tm,),
        in_specs=in_specs,
        out_specs=out_specs,
        out_shape=out_shape,
        scratch_shapes=[pltpu.VMEM((8, LANES), F32)],
        compiler_params=_cparams("arbitrary"),
        name="tail",
    )(y2, y2, xbc2d, p2d, ff, p2d, p2d, x2d, g1, sc2, sh2, dsk, snw, wso, wfo, wo, n2w, wr, br)


def _gather_rows(src, idx):
    n_out = idx.shape[0]
    width = src.shape[1]
    win = SC_GATHER_WINDOW
    info = plsc.get_sparse_core_info()
    n_cores, n_workers = info.num_cores, info.num_cores * info.num_subcores
    per_worker = n_out // n_workers
    assert per_worker * n_workers == n_out and per_worker % (2 * win) == 0
    mesh = plsc.VectorSubcoreMesh(core_axis_name="core", subcore_axis_name="subcore")

    @functools.partial(
        pl.kernel, out_type=jax.ShapeDtypeStruct((n_out, width), src.dtype), mesh=mesh,
        scratch_types=[pltpu.VMEM((win,), jnp.int32), pltpu.VMEM((win,), jnp.int32),
                       pltpu.VMEM((win, width), src.dtype), pltpu.VMEM((win, width), src.dtype),
                       pltpu.SemaphoreType.DMA, pltpu.SemaphoreType.DMA,
                       pltpu.SemaphoreType.DMA, pltpu.SemaphoreType.DMA],
        name="gather_rows")
    def gather(src_hbm, idx_hbm, out_hbm, idx_a, idx_b, rows_a, rows_b, gsem_a, gsem_b, ssem_a, ssem_b):
        worker = lax.axis_index("subcore") * n_cores + lax.axis_index("core")
        base = worker * per_worker

        @pl.loop(0, per_worker, step=2 * win)
        def _(off):
            pltpu.sync_copy(idx_hbm.at[pl.ds(base + off, win)], idx_a)
            ga = pltpu.async_copy(src_hbm.at[idx_a], rows_a, gsem_a)
            pltpu.sync_copy(idx_hbm.at[pl.ds(base + off + win, win)], idx_b)
            gb = pltpu.async_copy(src_hbm.at[idx_b], rows_b, gsem_b)
            ga.wait()
            sa = pltpu.async_copy(rows_a, out_hbm.at[pl.ds(base + off, win)], ssem_a)
            gb.wait()
            sb = pltpu.async_copy(rows_b, out_hbm.at[pl.ds(base + off + win, win)], ssem_b)
            sa.wait()
            sb.wait()

    return gather(src, idx)


def _expert_kernel(te_ref, nu_ref, x_ref, wgu_ref, bgu_ref, wd_ref, bd_ref, o_ref, wgu_scr, wd_scr):
    i = pl.program_id(0)
    used = i < nu_ref[0]
    new_expert = (i == 0) | (te_ref[i] != te_ref[jnp.maximum(i - 1, 0)])

    @pl.when(used & new_expert)
    def _():
        wgu_scr[...] = wgu_ref[...].astype(BF16)
        wd_scr[...] = wd_ref[...].astype(BF16)

    @pl.when(used)
    def _():
        dff = wd_ref.shape[0]
        half = wgu_ref.shape[0] // 2
        xa, xb = _unpack_bf16_pair(x_ref[...])
        gu = (jnp.dot(xa.astype(BF16), wgu_scr[:half, :], preferred_element_type=F32)
              + jnp.dot(xb.astype(BF16), wgu_scr[half:, :], preferred_element_type=F32)) + bgu_ref[...]
        gate = jnp.minimum(gu[:, :dff], SWIGLU_LIMIT)
        up = jnp.clip(gu[:, dff:], -SWIGLU_LIMIT, SWIGLU_LIMIT)
        act = (up + 1.0) * gate * _sigmoid(SWIGLU_ALPHA * gate)
        y = jnp.dot(act.astype(BF16), wd_scr[...], preferred_element_type=F32) + bd_ref[...]
        o_ref[...] = _pack_bf16_pair(y)

    @pl.when(jnp.logical_not(used))
    def _():
        o_ref[...] = jnp.zeros(o_ref.shape, o_ref.dtype)


def _experts(xs, tile_expert, n_used, wgu, bgu, wd, bd):
    rows, dh = xs.shape
    d = 2 * dh
    tm = MOE_TILE
    n_tiles = rows // tm
    dff2 = wgu.shape[-1]

    def tile(i, te, nu):
        return (jnp.minimum(i, nu[0] - 1), 0)

    def wsel(i, te, nu):
        return (te[jnp.minimum(i, nu[0] - 1)], 0, 0)

    grid_spec = pltpu.PrefetchScalarGridSpec(
        num_scalar_prefetch=2,
        grid=(n_tiles,),
        in_specs=[pl.BlockSpec((tm, dh), tile),
                  pl.BlockSpec((None, d, dff2), wsel),
                  pl.BlockSpec((None, 1, dff2), wsel),
                  pl.BlockSpec((None, dff2 // 2, d), wsel),
                  pl.BlockSpec((None, 1, d), wsel)],
        out_specs=pl.BlockSpec((tm, dh), lambda i, te, nu: (i, 0)),
        scratch_shapes=[pltpu.VMEM((d, dff2), BF16), pltpu.VMEM((dff2 // 2, d), BF16)],
    )
    return pl.pallas_call(
        _expert_kernel,
        grid_spec=grid_spec,
        out_shape=jax.ShapeDtypeStruct((rows, dh), jnp.uint32),
        compiler_params=_cparams("arbitrary"),
        name="experts",
    )(tile_expert, n_used, xs, wgu, bgu, wd, bd)


def _final_kernel(x1_ref, ya_ref, yb_ref, yc_ref, yd_ref, tw_ref, g2_ref, fw_ref, o_ref):
    d = x1_ref.shape[-1]
    half = d // 2
    tw = tw_ref[...]
    acc_hi = jnp.zeros((x1_ref.shape[0], half), F32)
    acc_lo = jnp.zeros((x1_ref.shape[0], half), F32)
    for k, y_ref in enumerate((ya_ref, yb_ref, yc_ref, yd_ref)):
        y_hi, y_lo = _unpack_bf16_pair(y_ref[...])
        acc_hi = acc_hi + tw[:, k:k + 1] * y_hi
        acc_lo = acc_lo + tw[:, k:k + 1] * y_lo
    x_hi = x1_ref[:, :half] + g2_ref[:, :half] * acc_hi
    x_lo = x1_ref[:, half:] + g2_ref[:, half:] * acc_lo
    ms = (jnp.sum(x_hi * x_hi, axis=-1, keepdims=True) + jnp.sum(x_lo * x_lo, axis=-1, keepdims=True)) / d
    inv = lax.rsqrt(ms + EPS)
    o_ref[:, :half] = x_hi * inv * fw_ref[:, :half]
    o_ref[:, half:] = x_lo * inv * fw_ref[:, half:]


def _final(x1, y4, tw, g2, fw, rows_per_mod, tm):
    m, d = x1.shape
    return pl.pallas_call(
        _final_kernel,
        grid=(m // tm,),
        in_specs=[pl.BlockSpec((tm, d), lambda i: (i, 0))]
        + [pl.BlockSpec((None, tm, d // 2), functools.partial(lambda k, i: (k, i, 0), k)) for k in range(TOP_K)]
        + [pl.BlockSpec((tm, LANES), lambda i: (i, 0)),
           pl.BlockSpec((None, 1, d), lambda i: ((i * tm) // rows_per_mod, 0, 0)),
           pl.BlockSpec((1, d), lambda i: (0, 0))],
        out_specs=pl.BlockSpec((tm, d), lambda i: (i, 0)),
        out_shape=jax.ShapeDtypeStruct((m, d), F32),
        compiler_params=_cparams("arbitrary"),
        name="final",
    )(x1, y4, y4, y4, y4, tw, g2, fw.reshape(1, d))


def _dispatch_plan(e, rank, counts, tm):
    n_assign = e.size
    padded = (counts + tm - 1) // tm * tm
    pad_end = jnp.cumsum(padded)
    pad_start = pad_end - padded
    start = jnp.cumsum(counts) - counts
    dest = (pad_start[e] + rank).astype(jnp.int32)
    n_tiles = n_assign // tm + N_EXPERTS
    tile_start = jnp.arange(n_tiles, dtype=jnp.int32) * tm
    tile_expert = jnp.minimum(jnp.sum(pad_end[None, :] <= tile_start[:, None], axis=1), N_EXPERTS - 1).astype(jnp.int32)
    n_used = (pad_end[-1] // tm).astype(jnp.int32).reshape(1)
    order = jnp.argsort(e.reshape(-1)).astype(jnp.int32)
    er = jnp.repeat(tile_expert, tm)
    j = jnp.arange(n_tiles * tm, dtype=jnp.int32) - pad_start[er].astype(jnp.int32)
    src = order[jnp.clip(start[er].astype(jnp.int32) + j, 0, n_assign - 1)]
    rho = jnp.arange(n_tiles * tm, dtype=jnp.int32)
    row_tok = jnp.where(j < counts[er], src // TOP_K, rho % (n_assign // TOP_K)).astype(jnp.int32)
    return row_tok, tile_expert, n_used, dest


def kernel(x, c, ctx, c_ctx, w_mod, b_mod, norm1_w, norm2_w, w_in, conv_w, conv_b, dt_bias, a_log, d_skip,
           ssd_norm_w, w_ssd_out, w_four_out, w_o, w_router, b_router, w_gate_up, b_gate_up, w_down, b_down,
           final_norm_w):
    bsz, n, d = x.shape
    n_ctx = ctx.shape[1]
    depth = w_mod.shape[0]
    assert depth == 1, "a stacked model would also need the context stream's residual update"
    d_inner = N_HEADS * HEAD_DIM
    off_dt = d_inner + 2 * BC_WIDTH
    m = bsz * n
    x2d = x.reshape(m, d)
    tm = min(512, n)

    for layer in range(depth):
        rows = -(-(bsz + 1) // 8) * 8
        c_rows = jnp.zeros((rows, d), F32).at[:bsz].set(c).at[bsz].set(c_ctx)
        mod = _modulation(c_rows, w_mod[layer], b_mod[layer])
        mods = [mod[:bsz, i * d:(i + 1) * d].reshape(bsz, 1, d) for i in range(N_MOD)]
        sh1, sc1, g1, sh2, sc2, g2 = mods
        sh1_c = mod[bsz:bsz + 1, :d].reshape(1, 1, d)
        sc1_c = mod[bsz:bsz + 1, d:2 * d].reshape(1, 1, d)

        wl = w_in[layer]
        w_main = jnp.concatenate([wl[:, :off_dt], wl[:, off_dt + 2 * N_HEADS:]], axis=1).astype(BF16)
        w_dt = jnp.zeros((d, 2 * LANES), F32)
        w_dt = w_dt.at[:, :N_HEADS].set(wl[:, off_dt:off_dt + N_HEADS])
        w_dt = w_dt.at[:, LANES:LANES + N_HEADS].set(wl[:, off_dt + N_HEADS:off_dt + 2 * N_HEADS]).astype(BF16)
        pad = jnp.zeros((2, 1, LANES - N_HEADS), F32)
        dt_bias2 = jnp.concatenate([dt_bias[layer].reshape(2, 1, N_HEADS), pad], axis=-1)
        a_log2 = jnp.concatenate([a_log[layer].reshape(2, 1, N_HEADS), pad], axis=-1)

        w_ctx = w_main[:, :off_dt]
        pc, dtc = _in_proj(ctx.reshape(bsz * n_ctx, d), norm1_w[layer], sc1_c, sh1_c, w_ctx, w_dt,
                           bsz * n_ctx, min(512, n_ctx))
        xbc_c = _conv_silu(pc.reshape(bsz, n_ctx, off_dt), conv_w[layer][:, :off_dt], conv_b[layer][:off_dt], off_dt)
        h0 = jnp.zeros((bsz, 2, N_GROUPS, D_STATE, HEADS_PER_GROUP * HEAD_DIM), F32)
        h_ctx = _ssd(xbc_c, dtc.reshape(bsz, n_ctx, 2 * LANES), dt_bias2, a_log2, h0, with_y=False)

        p, dtl = _in_proj(x2d, norm1_w[layer], sc1, sh1, w_main, w_dt, n, tm)
        n_conv = off_dt + 2 * BC_WIDTH
        xbc = _conv_silu(p.reshape(bsz, n, -1), conv_w[layer], conv_b[layer], n_conv)
        y2 = _ssd(xbc, dtl.reshape(bsz, n, 2 * LANES), dt_bias2, a_log2, h_ctx, with_y=True)
        ff = _fnet_mix(p, (n_conv + d_inner) // d, bsz, n)

        dsk = jnp.repeat(d_skip[layer].astype(F32), HEAD_DIM).reshape(1, d_inner)
        wr = jnp.zeros((d, LANES), F32).at[:, :N_EXPERTS].set(w_router[layer])
        br = jnp.full((1, LANES), NEG_BIG, F32).at[0, :N_EXPERTS].set(b_router[layer])
        x1, h2, ti, tw, rk, cnt = _tail(
            y2.reshape(2, m, d_inner), xbc.reshape(m, n_conv), p, ff, x2d, g1, sc2, sh2, dsk,
            ssd_norm_w[layer].reshape(1, d_inner), w_ssd_out[layer].astype(BF16), w_four_out[layer].astype(BF16),
            w_o[layer].astype(BF16), norm2_w[layer].reshape(1, d), wr, br, n, tm)

        counts = cnt[0, :N_EXPERTS].astype(jnp.int32)
        row_tok, tile_expert, n_used, dest = _dispatch_plan(ti[:, :TOP_K], rk[:, :TOP_K], counts, MOE_TILE)
        xs = _gather_rows(h2, row_tok)
        ys = _experts(xs, tile_expert, n_used, w_gate_up[layer], b_gate_up[layer].reshape(N_EXPERTS, 1, -1),
                      w_down[layer], b_down[layer].reshape(N_EXPERTS, 1, -1))
        y4 = _gather_rows(ys, dest.T.reshape(-1)).reshape(TOP_K, m, d // 2)
        x2d = _final(x1, y4, tw, g2, final_norm_w, n, tm)
    return x2d.reshape(bsz, n, d)
```

```python
import functools
import math

import numpy as np
import jax
import jax.numpy as jnp
from jax import lax
from jax.experimental import pallas as pl
from jax.experimental.pallas import tpu as pltpu
from jax.experimental.pallas import tpu_sc as plsc

F32 = jnp.float32
BF16 = jnp.bfloat16

EPS = 1e-6
GRID_W = 64
N_MOD = 6
F_GROUPS = 8
F_GROUP_DIM = 128
HEAD_DIM = 64
N_HEADS = 32
N_GROUPS = 4
HEADS_PER_GROUP = N_HEADS // N_GROUPS
D_STATE = 128
BC_WIDTH = N_GROUPS * D_STATE
CONV_K = 5
CHUNK = 128
N_EXPERTS = 32
TOP_K = 4
SWIGLU_LIMIT = 7.0
SWIGLU_ALPHA = 1.702

LANES = 128
VMEM_LIMIT_BYTES = 56 * 1024 * 1024
NEG_BIG = -1e30
MOE_TILE = 512
SC_GATHER_WINDOW = 64


def _cparams(*sem):
    return pltpu.CompilerParams(dimension_semantics=sem, vmem_limit_bytes=VMEM_LIMIT_BYTES)


def _sigmoid(v):
    return 1.0 / (1.0 + jnp.exp(-v))


def _silu(v):
    return v * _sigmoid(v)


def _softplus(v):
    return jnp.maximum(v, 0.0) + jnp.log(1.0 + jnp.exp(-jnp.abs(v)))


def _pack_bf16_pair(v):
    w = v.shape[1] // 2
    bits = lax.bitcast_convert_type(v.astype(BF16).astype(F32), jnp.uint32)
    return bits[:, :w] | (bits[:, w:] >> 16)


def _unpack_bf16_pair(p):
    hi = lax.bitcast_convert_type(p & jnp.uint32(0xFFFF0000), F32)
    lo = lax.bitcast_convert_type(p << 16, F32)
    return hi, lo


def _mod_kernel(c_ref, w_ref, b_ref, o_ref):
    s = _silu(c_ref[...]).astype(BF16)
    o_ref[...] = jnp.dot(s, w_ref[...].astype(BF16), preferred_element_type=F32) + b_ref[...]


def _modulation(c_rows, w_mod, b_mod):
    rows, d = c_rows.shape
    n_out = w_mod.shape[1]
    tn = 1024
    return pl.pallas_call(
        _mod_kernel,
        grid=(n_out // tn,),
        in_specs=[pl.BlockSpec((rows, d), lambda j: (0, 0)),
                  pl.BlockSpec((d, tn), lambda j: (0, j)),
                  pl.BlockSpec((1, tn), lambda j: (0, j))],
        out_specs=pl.BlockSpec((rows, tn), lambda j: (0, j)),
        out_shape=jax.ShapeDtypeStruct((rows, n_out), F32),
        compiler_params=_cparams("arbitrary"),
        name="modulation",
    )(c_rows, w_mod, b_mod.reshape(1, n_out))


_INPROJ_TN = 1024


def _inproj_kernel(x_ref, nw_ref, sc_ref, sh_ref, w_ref, wdt_ref, p_ref, dt_ref):
    x = x_ref[...]
    inv = lax.rsqrt(jnp.mean(x * x, axis=-1, keepdims=True) + EPS)
    h = (x * inv * nw_ref[...]) * (1.0 + sc_ref[...]) + sh_ref[...]
    hb = h.astype(BF16)
    dt_ref[...] = jnp.dot(hb, wdt_ref[...], preferred_element_type=F32)
    for j in range(w_ref.shape[1] // _INPROJ_TN):
        cols = slice(j * _INPROJ_TN, (j + 1) * _INPROJ_TN)
        p_ref[:, cols] = jnp.dot(hb, w_ref[:, cols], preferred_element_type=F32).astype(p_ref.dtype)


def _in_proj(x2d, norm_w, sc, sh, w_main, w_dt, rows_per_mod, tm):
    m, d = x2d.shape
    n = w_main.shape[1]
    ndt = w_dt.shape[1]
    resident = dict(pipeline_mode=pl.Buffered(1))
    return pl.pallas_call(
        _inproj_kernel,
        grid=(m // tm,),
        in_specs=[pl.BlockSpec((tm, d), lambda i: (i, 0)),
                  pl.BlockSpec((1, d), lambda i: (0, 0), **resident),
                  pl.BlockSpec((None, 1, d), lambda i: ((i * tm) // rows_per_mod, 0, 0)),
                  pl.BlockSpec((None, 1, d), lambda i: ((i * tm) // rows_per_mod, 0, 0)),
                  pl.BlockSpec((d, n), lambda i: (0, 0), **resident),
                  pl.BlockSpec((d, ndt), lambda i: (0, 0), **resident)],
        out_specs=[pl.BlockSpec((tm, n), lambda i: (i, 0)),
                   pl.BlockSpec((tm, ndt), lambda i: (i, 0))],
        out_shape=[jax.ShapeDtypeStruct((m, n), BF16),
                   jax.ShapeDtypeStruct((m, ndt), F32)],
        compiler_params=_cparams("arbitrary"),
        name="in_proj",
    )(x2d, norm_w.reshape(1, d), sc, sh, w_main, w_dt)


_CONV_HALO = 16


def _conv_kernel(p_ref, w_ref, b_ref, o_ref, *, n, rc):
    tc = o_ref.shape[-1]
    w = w_ref[...]
    bias = b_ref[...]
    zeros = jnp.zeros((_CONV_HALO, tc), F32)
    tot = rc + 2 * _CONV_HALO
    for r0 in range(0, n, rc):
        top = zeros if r0 == 0 else p_ref[r0 - _CONV_HALO:r0, :].astype(F32)
        bot = zeros if r0 + rc >= n else p_ref[r0 + rc:r0 + rc + _CONV_HALO, :].astype(F32)
        blk = jnp.concatenate([top, p_ref[r0:r0 + rc, :].astype(F32), bot], axis=0)
        acc = jnp.broadcast_to(bias, (rc, tc))
        for k in range(CONV_K):
            shift = (CONV_K // 2 - k) % tot
            rolled = blk if shift == 0 else pltpu.roll(blk, shift, axis=0)
            acc = acc + w[k:k + 1, :] * rolled[_CONV_HALO:_CONV_HALO + rc, :]
        o_ref[r0:r0 + rc, :] = _silu(acc).astype(o_ref.dtype)


def _conv_silu(p3d, conv_w, conv_b, n_ch):
    bsz, n, _ = p3d.shape
    tc = 512
    rc = min(256, n)
    return pl.pallas_call(
        functools.partial(_conv_kernel, n=n, rc=rc),
        grid=(bsz, n_ch // tc),
        in_specs=[pl.BlockSpec((None, n, tc), lambda b, j: (b, 0, j)),
                  pl.BlockSpec((CONV_K, tc), lambda b, j: (0, j)),
                  pl.BlockSpec((1, tc), lambda b, j: (0, j))],
        out_specs=pl.BlockSpec((None, n, tc), lambda b, j: (b, 0, j)),
        out_shape=jax.ShapeDtypeStruct((bsz, n, n_ch), BF16),
        compiler_params=_cparams("arbitrary", "arbitrary"),
        name="conv_silu",
    )(p3d, conv_w, conv_b.reshape(1, -1))


LOG2E = 1.4426950408889634


def _head_expand_matrix():
    e = np.zeros((LANES, N_HEADS * HEAD_DIM), np.float32)
    for h in range(N_HEADS):
        e[h, h * HEAD_DIM:(h + 1) * HEAD_DIM] = 1.0
    return jnp.asarray(e, BF16)


def _ssd_chunk_terms(dt_raw, dt_bias, a_log, expand, tri, fwd):
    q = CHUNK
    dt = _softplus(dt_raw + dt_bias)
    da = dt * (-jnp.exp(a_log))
    hi = da.astype(BF16)
    r1 = da - hi.astype(F32)
    mid = r1.astype(BF16)
    lo = (r1 - mid.astype(F32)).astype(BF16)
    ones_tri = jnp.where(tri, 1.0, 0.0).astype(BF16)
    cs3 = jnp.dot(ones_tri, jnp.concatenate([hi, mid, lo], axis=1), preferred_element_type=F32)
    cs = cs3[:, :LANES] + cs3[:, LANES:2 * LANES] + cs3[:, 2 * LANES:]
    tot = jnp.where(fwd, cs[q - 1:q, :], cs[0:1, :])
    dte = dt * jnp.exp(tot - cs)
    dec = jnp.exp(tot)
    dec_hi = dec.astype(BF16)
    dec_lo = (dec - dec_hi.astype(F32)).astype(BF16)
    dec2 = jnp.concatenate([dec_hi, dec_lo, jnp.zeros((6, LANES), BF16)], axis=0)
    stacked = jnp.concatenate([dt.astype(BF16), jnp.exp(cs).astype(BF16), dte.astype(BF16), dec2], axis=0)
    spread = jnp.dot(stacked, expand, preferred_element_type=F32)
    dt_e = spread[:q].astype(BF16)
    ecs_e = spread[q:2 * q]
    dte_e = spread[2 * q:3 * q].astype(BF16)
    dec_e = jnp.broadcast_to(spread[3 * q:3 * q + 1] + spread[3 * q + 1:3 * q + 2], (8, spread.shape[1]))
    cs2 = cs * LOG2E
    return cs2, cs2.T, dt_e, ecs_e, dte_e, dec_e


def _ssd_kernel(*refs, with_y, nc):
    if with_y:
        (x_ref, b_ref, c_ref, dt_ref, dtn_ref, dtb_ref, alog_ref, exp_ref, h0_ref, y_ref,
         s_scr, cs_scr, dee_scr, dce_scr, dtx_scr, ecs_scr) = refs
    else:
        (x_ref, b_ref, dt_ref, dtn_ref, dtb_ref, alog_ref, exp_ref, h0_ref, hout_ref,
         s_scr, cs_scr, dee_scr, dce_scr) = refs
    q = CHUNK
    gw = HEADS_PER_GROUP * HEAD_DIM
    d = pl.program_id(1)
    c = pl.program_id(2)
    row = lax.broadcasted_iota(jnp.int32, (q, q), 0)
    col = lax.broadcasted_iota(jnp.int32, (q, q), 1)
    fwd = d == 0
    tri = jnp.where(fwd, row - col, col - row) >= 0
    first_half = col < HEAD_DIM

    def store_terms(slot, terms):
        cs2, cs2_t, dt_e, ecs_e, dte_e, dec_e = terms
        cs_scr[slot, 0] = cs2
        cs_scr[slot, 1] = cs2_t
        dee_scr[slot] = dte_e
        dce_scr[slot] = dec_e
        if with_y:
            dtx_scr[slot] = dt_e
            ecs_scr[slot] = ecs_e

    @pl.when(c == 0)
    def _():
        s_scr[...] = h0_ref[...]
        store_terms(0, _ssd_chunk_terms(dt_ref[...], dtb_ref[...], alog_ref[...], exp_ref[...], tri, fwd))

    slot = c % 2
    cs2 = cs_scr[slot, 0]
    cs2_t = cs_scr[slot, 1]
    store_terms(1 - slot, _ssd_chunk_terms(dtn_ref[...], dtb_ref[...], alog_ref[...], exp_ref[...], tri, fwd))

    for g in range(N_GROUPS):
        cols = slice(g * gw, (g + 1) * gw)
        bg = b_ref[:, g * D_STATE:(g + 1) * D_STATE]
        xg = x_ref[:, cols]
        s_old = s_scr[g]
        if with_y:
            cg = c_ref[:, g * D_STATE:(g + 1) * D_STATE]
            cb = lax.dot_general(cg, bg, (((1,), (1,)), ((), ())), preferred_element_type=F32)
            y_off = jnp.dot(cg, s_old.astype(BF16), preferred_element_type=F32)
            xdt = xg * dtx_scr[slot, :, cols]
            for k in range(HEADS_PER_GROUP // 2):
                pair = g * (HEADS_PER_GROUP // 2) + k
                lanes = slice(pair * LANES, (pair + 1) * LANES)
                xp = xdt[:, k * LANES:(k + 1) * LANES]
                xz = jnp.zeros_like(xp)
                lhs = []
                for hh in (2 * pair, 2 * pair + 1):
                    colb = jnp.broadcast_to(cs2[:, hh:hh + 1], (q, q))
                    rowb = cs2_t[hh:hh + 1, :]
                    lhs.append((cb * jnp.exp2(jnp.where(tri, colb - rowb, -jnp.inf))).astype(BF16))
                y_diag = jnp.dot(jnp.concatenate(lhs, axis=1),
                                 jnp.concatenate([jnp.where(first_half, xp, xz), jnp.where(first_half, xz, xp)], axis=0),
                                 preferred_element_type=F32)
                y = y_diag + ecs_scr[slot, :, lanes] * y_off[:, k * LANES:(k + 1) * LANES]
                y_ref[:, lanes] = y.astype(y_ref.dtype)
        xdte = xg * dee_scr[slot, :, cols]
        upd = lax.dot_general(bg, xdte, (((0,), (0,)), ((), ())), preferred_element_type=F32)
        s_scr[g] = s_old * dce_scr[slot, 0:1, cols] + upd

    if not with_y:
        @pl.when(c == nc - 1)
        def _():
            hout_ref[...] = s_scr[...]


def _ssd(xbc, dt_raw, dt_bias2, a_log2, h0, with_y):
    bsz, n, _ = xbc.shape
    nc = n // CHUNK
    d_inner = N_HEADS * HEAD_DIM
    gw = HEADS_PER_GROUP * HEAD_DIM
    x_blk = d_inner // BC_WIDTH

    def ceff(dd, cc):
        return cc + dd * (nc - 1 - 2 * cc)

    def cnext(dd, cc):
        return ceff(dd, jnp.minimum(cc + 1, nc - 1))

    in_specs = [pl.BlockSpec((None, CHUNK, d_inner), lambda b, dd, cc: (b, ceff(dd, cc), 0)),
                pl.BlockSpec((None, CHUNK, BC_WIDTH), lambda b, dd, cc: (b, ceff(dd, cc), x_blk + dd))]
    args = [xbc, xbc]
    if with_y:
        in_specs.append(pl.BlockSpec((None, CHUNK, BC_WIDTH), lambda b, dd, cc: (b, ceff(dd, cc), x_blk + 2 + dd)))
        args.append(xbc)
    in_specs += [pl.BlockSpec((None, CHUNK, LANES), lambda b, dd, cc: (b, ceff(dd, cc), dd)),
                 pl.BlockSpec((None, CHUNK, LANES), lambda b, dd, cc: (b, cnext(dd, cc), dd)),
                 pl.BlockSpec((None, 1, LANES), lambda b, dd, cc: (dd, 0, 0)),
                 pl.BlockSpec((None, 1, LANES), lambda b, dd, cc: (dd, 0, 0)),
                 pl.BlockSpec((LANES, d_inner), lambda b, dd, cc: (0, 0)),
                 pl.BlockSpec((None, None, N_GROUPS, D_STATE, gw), lambda b, dd, cc: (b, dd, 0, 0, 0))]
    args += [dt_raw, dt_raw, dt_bias2, a_log2, _head_expand_matrix(), h0]
    scratch = [pltpu.VMEM((N_GROUPS, D_STATE, gw), F32),
               pltpu.VMEM((2, 2, CHUNK, LANES), F32),
               pltpu.VMEM((2, CHUNK, d_inner), BF16),
               pltpu.VMEM((2, 8, d_inner), F32)]
    if with_y:
        out_specs = pl.BlockSpec((None, None, CHUNK, d_inner), lambda b, dd, cc: (dd, b, ceff(dd, cc), 0))
        out_shape = jax.ShapeDtypeStruct((2, bsz, n, d_inner), BF16)
        scratch += [pltpu.VMEM((2, CHUNK, d_inner), BF16),
                    pltpu.VMEM((2, CHUNK, d_inner), F32)]
    else:
        out_specs = pl.BlockSpec((None, None, N_GROUPS, D_STATE, gw), lambda b, dd, cc: (b, dd, 0, 0, 0))
        out_shape = jax.ShapeDtypeStruct((bsz, 2, N_GROUPS, D_STATE, gw), F32)
    return pl.pallas_call(
        functools.partial(_ssd_kernel, with_y=with_y, nc=nc),
        grid=(bsz, 2, nc),
        in_specs=in_specs,
        out_specs=out_specs,
        out_shape=out_shape,
        scratch_shapes=scratch,
        compiler_params=_cparams("arbitrary", "arbitrary", "arbitrary"),
        name="ssd_scan" if with_y else "ssd_ctx_state",
    )(*args)


def _dft_mats(n):
    k = np.arange(n)
    ang = 2.0 * np.pi * ((k[:, None] * k[None, :]) % n) / n
    return np.cos(ang), np.sin(ang)


def _fnet_chan_kernel(u_ref, m_ref, o_ref):
    m = m_ref[...]
    for g in range(F_GROUPS):
        ug = u_ref[:, g * F_GROUP_DIM:(g + 1) * F_GROUP_DIM]
        pq = jnp.dot(ug, m, preferred_element_type=F32)
        o_ref[0, :, g * F_GROUP_DIM:(g + 1) * F_GROUP_DIM] = pq[:, :F_GROUP_DIM].astype(o_ref.dtype)
        o_ref[1, :, g * F_GROUP_DIM:(g + 1) * F_GROUP_DIM] = pq[:, F_GROUP_DIM:].astype(o_ref.dtype)


def _stage_lane_blocks(src_ref, scr, rows):
    width = src_ref.shape[-1]
    for t in range(2):
        v = src_ref[t].reshape(rows, width).astype(F32)
        for lb in range(width // LANES):
            scr[t, lb] = v[:, lb * LANES:(lb + 1) * LANES]


def _strided_rows(scr, t, start, count, stride, width):
    return jnp.concatenate([scr[t, lb, pl.ds(start, count, stride=stride), :] for lb in range(width // LANES)],
                           axis=1)


def _fnet_stage1_kernel(z_ref, m1_ref, twc_ref, tws_ref, o_ref, zs_scr, *, tcs):
    r = z_ref.shape[1]
    width = z_ref.shape[-1]
    _stage_lane_blocks(z_ref, zs_scr, r * tcs)
    m1 = m1_ref[...]
    for j in range(tcs):
        zz = jnp.concatenate([_strided_rows(zs_scr, 0, j, r, tcs, width),
                              _strided_rows(zs_scr, 1, j, r, tcs, width)], axis=0).astype(BF16)
        u = jnp.dot(m1, zz, preferred_element_type=F32)
        ur = u[:r]
        ui = u[r:]
        tc = twc_ref[j]
        ts = tws_ref[j]
        o_ref[0, j] = (ur * tc + ui * ts).astype(o_ref.dtype)
        o_ref[1, j] = (ui * tc - ur * ts).astype(o_ref.dtype)


def _fnet_stage2_kernel(u_ref, m2_ref, o_ref, us_scr, os_scr, *, tbs):
    c = u_ref.shape[1]
    width = u_ref.shape[-1]
    _stage_lane_blocks(u_ref, us_scr, c * tbs)
    m2 = m2_ref[...]
    for j in range(tbs):
        uu = jnp.concatenate([_strided_rows(us_scr, 0, j, c, tbs, width),
                              _strided_rows(us_scr, 1, j, c, tbs, width)], axis=0).astype(BF16)
        x = jnp.dot(m2, uu, preferred_element_type=F32)
        for lb in range(width // LANES):
            os_scr[lb, pl.ds(j, c, stride=tbs), :] = x[:, lb * LANES:(lb + 1) * LANES]
    out = jnp.concatenate([os_scr[lb] for lb in range(width // LANES)], axis=1)
    o_ref[...] = out.reshape(c, tbs, width).astype(o_ref.dtype)


def _fnet_mix(p2d, col_blk, bsz, n):
    m = p2d.shape[0]
    width = F_GROUPS * F_GROUP_DIM
    cgrid = GRID_W
    rgrid = n // cgrid
    scale = 1.0 / math.sqrt(n * F_GROUP_DIM)

    cc, sc = _dft_mats(F_GROUP_DIM)
    chan = jnp.asarray(np.concatenate([cc, sc], axis=1) * scale, BF16)
    tm = 512
    pq = pl.pallas_call(
        _fnet_chan_kernel,
        grid=(m // tm,),
        in_specs=[pl.BlockSpec((tm, width), lambda i: (i, col_blk)),
                  pl.BlockSpec((F_GROUP_DIM, 2 * F_GROUP_DIM), lambda i: (0, 0))],
        out_specs=pl.BlockSpec((2, tm, width), lambda i: (0, i, 0)),
        out_shape=jax.ShapeDtypeStruct((2, m, width), BF16),
        compiler_params=_cparams("arbitrary"),
        name="fnet_chan",
    )(p2d, chan)

    cr, sr = _dft_mats(rgrid)
    m1 = jnp.asarray(np.block([[cr, -sr], [-sr, -cr]]), BF16)
    bb = np.arange(rgrid)[None, :]
    ci = np.arange(cgrid)[:, None]
    ang = 2.0 * np.pi * ((ci * bb) % n) / n
    twc = jnp.asarray(np.cos(ang)[:, :, None], F32)
    tws = jnp.asarray(np.sin(ang)[:, :, None], F32)
    tcs = 16
    z5 = pq.reshape(2, bsz, rgrid, cgrid, width)
    u5 = pl.pallas_call(
        functools.partial(_fnet_stage1_kernel, tcs=tcs),
        grid=(bsz, cgrid // tcs),
        in_specs=[pl.BlockSpec((2, None, rgrid, tcs, width), lambda b, j: (0, b, 0, j, 0)),
                  pl.BlockSpec((2 * rgrid, 2 * rgrid), lambda b, j: (0, 0)),
                  pl.BlockSpec((tcs, rgrid, 1), lambda b, j: (j, 0, 0)),
                  pl.BlockSpec((tcs, rgrid, 1), lambda b, j: (j, 0, 0))],
        out_specs=pl.BlockSpec((2, None, tcs, rgrid, width), lambda b, j: (0, b, j, 0, 0)),
        out_shape=jax.ShapeDtypeStruct((2, bsz, cgrid, rgrid, width), BF16),
        scratch_shapes=[pltpu.VMEM((2, width // LANES, rgrid * tcs, LANES), F32)],
        compiler_params=_cparams("arbitrary", "arbitrary"),
        name="fnet_stage1",
    )(z5, m1, twc, tws)

    cc2, sc2 = _dft_mats(cgrid)
    m2 = jnp.asarray(np.concatenate([cc2, sc2], axis=1), BF16)
    tbs = min(16, rgrid)
    out = pl.pallas_call(
        functools.partial(_fnet_stage2_kernel, tbs=tbs),
        grid=(bsz, rgrid // tbs),
        in_specs=[pl.BlockSpec((2, None, cgrid, tbs, width), lambda b, j: (0, b, 0, j, 0)),
                  pl.BlockSpec((cgrid, 2 * cgrid), lambda b, j: (0, 0))],
        out_specs=pl.BlockSpec((None, cgrid, tbs, width), lambda b, j: (b, 0, j, 0)),
        out_shape=jax.ShapeDtypeStruct((bsz, cgrid, rgrid, width), BF16),
        scratch_shapes=[pltpu.VMEM((2, width // LANES, cgrid * tbs, LANES), F32),
                        pltpu.VMEM((width // LANES, cgrid * tbs, LANES), F32)],
        compiler_params=_cparams("arbitrary", "arbitrary"),
        name="fnet_stage2",
    )(u5, m2)
    return out.reshape(m, width)


def _tail_kernel(yf_ref, yb_ref, xs_ref, z_ref, ff_ref, gf_ref, gs_ref, x_ref, g1_ref, sc2_ref, sh2_ref,
                 dsk_ref, snw_ref, wso_ref, wfo_ref, wo_ref, n2w_ref, wr_ref, br_ref,
                 x1_ref, h2_ref, ti_ref, tw_ref, rk_ref, cnt_ref, carry_scr):
    y = yf_ref[0].astype(F32) + yb_ref[0].astype(F32) + dsk_ref[...] * xs_ref[...].astype(F32)
    g = y * _silu(z_ref[...].astype(F32))
    inv = lax.rsqrt(jnp.mean(g * g, axis=-1, keepdims=True) + EPS)
    gn = (g * inv * snw_ref[...]).astype(BF16)
    y_ssd = jnp.dot(gn, wso_ref[...], preferred_element_type=F32)
    y_four = jnp.dot(ff_ref[...], wfo_ref[...], preferred_element_type=F32)
    t = _sigmoid(gf_ref[...].astype(F32)) * y_four + _sigmoid(gs_ref[...].astype(F32)) * y_ssd
    mix = jnp.dot(t.astype(BF16), wo_ref[...], preferred_element_type=F32)
    x1 = x_ref[...] + g1_ref[...] * mix
    x1_ref[...] = x1
    inv2 = lax.rsqrt(jnp.mean(x1 * x1, axis=-1, keepdims=True) + EPS)
    h2 = (x1 * inv2 * n2w_ref[...]) * (1.0 + sc2_ref[...]) + sh2_ref[...]
    h2_ref[...] = _pack_bf16_pair(h2)
    h_hi = h2.astype(BF16)
    h_lo = (h2 - h_hi.astype(F32)).astype(BF16)
    wr = wr_ref[...]
    w_hi = wr.astype(BF16)
    w_lo = (wr - w_hi.astype(F32)).astype(BF16)
    logits = (jnp.dot(h_hi, w_hi, preferred_element_type=F32)
              + jnp.dot(h_hi, w_lo, preferred_element_type=F32)
              + jnp.dot(h_lo, w_hi, preferred_element_type=F32)) + br_ref[...]
    tm = logits.shape[0]
    lane = lax.broadcasted_iota(jnp.int32, (tm, LANES), 1)
    vals, idxs = [], []
    cur = logits
    for _ in range(TOP_K):
        mx = jnp.max(cur, axis=-1, keepdims=True)
        ix = jnp.min(jnp.where(cur == mx, lane, LANES), axis=-1, keepdims=True)
        vals.append(mx)
        idxs.append(ix)
        cur = jnp.where(lane == ix, NEG_BIG * 2.0, cur)
    es = [jnp.exp(v - vals[0]) for v in vals]
    den = es[0] + es[1] + es[2] + es[3]
    ti = jnp.zeros((tm, LANES), jnp.int32)
    tw = jnp.zeros((tm, LANES), F32)
    for k in range(TOP_K):
        ti = jnp.where(lane == k, idxs[k], ti)
        tw = jnp.where(lane == k, es[k] / den, tw)
    ti_ref[...] = ti
    tw_ref[...] = tw

    @pl.when(pl.program_id(0) == 0)
    def _():
        carry_scr[...] = jnp.zeros(carry_scr.shape, F32)

    onehots = [jnp.where(lane == ix, 1.0, 0.0) for ix in idxs]
    cnt = onehots[0] + onehots[1] + onehots[2] + onehots[3]
    r_i = lax.broadcasted_iota(jnp.int32, (tm, tm), 0)
    c_i = lax.broadcasted_iota(jnp.int32, (tm, tm), 1)
    earlier = jnp.where(r_i > c_i, 1.0, 0.0).astype(BF16)
    before = jnp.dot(earlier, cnt.astype(BF16), preferred_element_type=F32) + carry_scr[0:1, :]
    rk = jnp.zeros((tm, LANES), jnp.int32)
    for k in range(TOP_K):
        rank_k = jnp.sum(onehots[k] * before, axis=-1, keepdims=True)
        rk = jnp.where(lane == k, rank_k.astype(jnp.int32), rk)
    rk_ref[...] = rk
    total = carry_scr[...] + jnp.sum(cnt, axis=0, keepdims=True)
    carry_scr[...] = total
    cnt_ref[...] = total


def _tail(y2, xbc2d, p2d, ff, x2d, g1, sc2, sh2, dsk, snw, wso, wfo, wo, n2w, wr, br, rows_per_mod, tm):
    m, d = x2d.shape
    di = y2.shape[-1]
    zb = (2 * di) // di
    fb = (3 * di) // d
    row = lambda i: (i, 0)
    modrow = lambda i: ((i * tm) // rows_per_mod, 0, 0)
    const = lambda i: (0, 0)
    in_specs = [pl.BlockSpec((1, tm, di), lambda i: (0, i, 0)),
                pl.BlockSpec((1, tm, di), lambda i: (1, i, 0)),
                pl.BlockSpec((tm, di), row),
                pl.BlockSpec((tm, di), lambda i: (i, zb)),
                pl.BlockSpec((tm, d), row),
                pl.BlockSpec((tm, d), lambda i: (i, fb + 1)),
                pl.BlockSpec((tm, d), lambda i: (i, fb + 2)),
                pl.BlockSpec((tm, d), row),
                pl.BlockSpec((None, 1, d), modrow),
                pl.BlockSpec((None, 1, d), modrow),
                pl.BlockSpec((None, 1, d), modrow),
                pl.BlockSpec((1, di), const),
                pl.BlockSpec((1, di), const),
                pl.BlockSpec((di, d), const),
                pl.BlockSpec((d, d), const),
                pl.BlockSpec((d, d), const),
                pl.BlockSpec((1, d), const),
                pl.BlockSpec((d, LANES), const),
                pl.BlockSpec((1, LANES), const)]
    out_specs = [pl.BlockSpec((tm, d), row), pl.BlockSpec((tm, d // 2), row),
                 pl.BlockSpec((tm, LANES), row), pl.BlockSpec((tm, LANES), row), pl.BlockSpec((tm, LANES), row),
                 pl.BlockSpec((8, LANES), const)]
    out_shape = [jax.ShapeDtypeStruct((m, d), F32), jax.ShapeDtypeStruct((m, d // 2), jnp.uint32),
                 jax.ShapeDtypeStruct((m, LANES), jnp.int32), jax.ShapeDtypeStruct((m, LANES), F32),
                 jax.ShapeDtypeStruct((m, LANES), jnp.int32), jax.ShapeDtypeStruct((8, LANES), F32)]
    return pl.pallas_call(
        _tail_kernel,
        grid=(m // tm,),
        in_specs=in_specs,
        out_specs=out_specs,
        out_shape=out_shape,
        scratch_shapes=[pltpu.VMEM((8, LANES), F32)],
        compiler_params=_cparams("arbitrary"),
        name="tail",
    )(y2, y2, xbc2d, p2d, ff, p2d, p2d, x2d, g1, sc2, sh2, dsk, snw, wso, wfo, wo, n2w, wr, br)


def _gather_rows(src, idx):
    n_out = idx.shape[0]
    width = src.shape[1]
    win = SC_GATHER_WINDOW
    info = plsc.get_sparse_core_info()
    n_cores, n_workers = info.num_cores, info.num_cores * info.num_subcores
    per_worker = n_out // n_workers
    assert per_worker * n_workers == n_out and per_worker % (2 * win) == 0
    mesh = plsc.VectorSubcoreMesh(core_axis_name="core", subcore_axis_name="subcore")

    @functools.partial(
        pl.kernel, out_type=jax.ShapeDtypeStruct((n_out, width), src.dtype), mesh=mesh,
        scratch_types=[pltpu.VMEM((win,), jnp.int32), pltpu.VMEM((win,), jnp.int32),
                       pltpu.VMEM((win, width), src.dtype), pltpu.VMEM((win, width), src.dtype),
                       pltpu.SemaphoreType.DMA, pltpu.SemaphoreType.DMA,
                       pltpu.SemaphoreType.DMA, pltpu.SemaphoreType.DMA],
        name="gather_rows")
    def gather(src_hbm, idx_hbm, out_hbm, idx_a, idx_b, rows_a, rows_b, gsem_a, gsem_b, ssem_a, ssem_b):
        worker = lax.axis_index("subcore") * n_cores + lax.axis_index("core")
        base = worker * per_worker

        @pl.loop(0, per_worker, step=2 * win)
        def _(off):
            pltpu.sync_copy(idx_hbm.at[pl.ds(base + off, win)], idx_a)
            ga = pltpu.async_copy(src_hbm.at[idx_a], rows_a, gsem_a)
            pltpu.sync_copy(idx_hbm.at[pl.ds(base + off + win, win)], idx_b)
            gb = pltpu.async_copy(src_hbm.at[idx_b], rows_b, gsem_b)
            ga.wait()
            sa = pltpu.async_copy(rows_a, out_hbm.at[pl.ds(base + off, win)], ssem_a)
            gb.wait()
            sb = pltpu.async_copy(rows_b, out_hbm.at[pl.ds(base + off + win, win)], ssem_b)
            sa.wait()
            sb.wait()

    return gather(src, idx)


def _expert_kernel(te_ref, nu_ref, x_ref, wgu_ref, bgu_ref, wd_ref, bd_ref, o_ref, wgu_scr, wd_scr):
    i = pl.program_id(0)
    used = i < nu_ref[0]
    new_expert = (i == 0) | (te_ref[i] != te_ref[jnp.maximum(i - 1, 0)])

    @pl.when(used & new_expert)
    def _():
        wgu_scr[...] = wgu_ref[...].astype(BF16)
        wd_scr[...] = wd_ref[...].astype(BF16)

    @pl.when(used)
    def _():
        dff = wd_ref.shape[0]
        half = wgu_ref.shape[0] // 2
        xa, xb = _unpack_bf16_pair(x_ref[...])
        gu = (jnp.dot(xa.astype(BF16), wgu_scr[:half, :], preferred_element_type=F32)
              + jnp.dot(xb.astype(BF16), wgu_scr[half:, :], preferred_element_type=F32)) + bgu_ref[...]
        gate = jnp.minimum(gu[:, :dff], SWIGLU_LIMIT)
        up = jnp.clip(gu[:, dff:], -SWIGLU_LIMIT, SWIGLU_LIMIT)
        act = (up + 1.0) * gate * _sigmoid(SWIGLU_ALPHA * gate)
        y = jnp.dot(act.astype(BF16), wd_scr[...], preferred_element_type=F32) + bd_ref[...]
        o_ref[...] = _pack_bf16_pair(y)

    @pl.when(jnp.logical_not(used))
    def _():
        o_ref[...] = jnp.zeros(o_ref.shape, o_ref.dtype)


def _experts(xs, tile_expert, n_used, wgu, bgu, wd, bd):
    rows, dh = xs.shape
    d = 2 * dh
    tm = MOE_TILE
    n_tiles = rows // tm
    dff2 = wgu.shape[-1]

    def tile(i, te, nu):
        return (jnp.minimum(i, nu[0] - 1), 0)

    def wsel(i, te, nu):
        return (te[jnp.minimum(i, nu[0] - 1)], 0, 0)

    grid_spec = pltpu.PrefetchScalarGridSpec(
        num_scalar_prefetch=2,
        grid=(n_tiles,),
        in_specs=[pl.BlockSpec((tm, dh), tile),
                  pl.BlockSpec((None, d, dff2), wsel),
                  pl.BlockSpec((None, 1, dff2), wsel),
                  pl.BlockSpec((None, dff2 // 2, d), wsel),
                  pl.BlockSpec((None, 1, d), wsel)],
        out_specs=pl.BlockSpec((tm, dh), lambda i, te, nu: (i, 0)),
        scratch_shapes=[pltpu.VMEM((d, dff2), BF16), pltpu.VMEM((dff2 // 2, d), BF16)],
    )
    return pl.pallas_call(
        _expert_kernel,
        grid_spec=grid_spec,
        out_shape=jax.ShapeDtypeStruct((rows, dh), jnp.uint32),
        compiler_params=_cparams("arbitrary"),
        name="experts",
    )(tile_expert, n_used, xs, wgu, bgu, wd, bd)


def _final_kernel(x1_ref, ya_ref, yb_ref, yc_ref, yd_ref, tw_ref, g2_ref, fw_ref, o_ref):
    d = x1_ref.shape[-1]
    half = d // 2
    tw = tw_ref[...]
    acc_hi = jnp.zeros((x1_ref.shape[0], half), F32)
    acc_lo = jnp.zeros((x1_ref.shape[0], half), F32)
    for k, y_ref in enumerate((ya_ref, yb_ref, yc_ref, yd_ref)):
        y_hi, y_lo = _unpack_bf16_pair(y_ref[...])
        acc_hi = acc_hi + tw[:, k:k + 1] * y_hi
        acc_lo = acc_lo + tw[:, k:k + 1] * y_lo
    x_hi = x1_ref[:, :half] + g2_ref[:, :half] * acc_hi
    x_lo = x1_ref[:, half:] + g2_ref[:, half:] * acc_lo
    ms = (jnp.sum(x_hi * x_hi, axis=-1, keepdims=True) + jnp.sum(x_lo * x_lo, axis=-1, keepdims=True)) / d
    inv = lax.rsqrt(ms + EPS)
    o_ref[:, :half] = x_hi * inv * fw_ref[:, :half]
    o_ref[:, half:] = x_lo * inv * fw_ref[:, half:]


def _final(x1, y4, tw, g2, fw, rows_per_mod, tm):
    m, d = x1.shape
    return pl.pallas_call(
        _final_kernel,
        grid=(m // tm,),
        in_specs=[pl.BlockSpec((tm, d), lambda i: (i, 0))]
        + [pl.BlockSpec((None, tm, d // 2), functools.partial(lambda k, i: (k, i, 0), k)) for k in range(TOP_K)]
        + [pl.BlockSpec((tm, LANES), lambda i: (i, 0)),
           pl.BlockSpec((None, 1, d), lambda i: ((i * tm) // rows_per_mod, 0, 0)),
           pl.BlockSpec((1, d), lambda i: (0, 0))],
        out_specs=pl.BlockSpec((tm, d), lambda i: (i, 0)),
        out_shape=jax.ShapeDtypeStruct((m, d), F32),
        compiler_params=_cparams("arbitrary"),
        name="final",
    )(x1, y4, y4, y4, y4, tw, g2, fw.reshape(1, d))


def _dispatch_plan(e, rank, counts, tm):
    n_assign = e.size
    padded = (counts + tm - 1) // tm * tm
    pad_end = jnp.cumsum(padded)
    pad_start = pad_end - padded
    start = jnp.cumsum(counts) - counts
    dest = (pad_start[e] + rank).astype(jnp.int32)
    n_tiles = n_assign // tm + N_EXPERTS
    tile_start = jnp.arange(n_tiles, dtype=jnp.int32) * tm
    tile_expert = jnp.minimum(jnp.sum(pad_end[None, :] <= tile_start[:, None], axis=1), N_EXPERTS - 1).astype(jnp.int32)
    n_used = (pad_end[-1] // tm).astype(jnp.int32).reshape(1)
    order = jnp.argsort(e.reshape(-1)).astype(jnp.int32)
    er = jnp.repeat(tile_expert, tm)
    j = jnp.arange(n_tiles * tm, dtype=jnp.int32) - pad_start[er].astype(jnp.int32)
    src = order[jnp.clip(start[er].astype(jnp.int32) + j, 0, n_assign - 1)]
    rho = jnp.arange(n_tiles * tm, dtype=jnp.int32)
    row_tok = jnp.where(j < counts[er], src // TOP_K, rho % (n_assign // TOP_K)).astype(jnp.int32)
    return row_tok, tile_expert, n_used, dest


def kernel(x, c, ctx, c_ctx, w_mod, b_mod, norm1_w, norm2_w, w_in, conv_w, conv_b, dt_bias, a_log, d_skip,
           ssd_norm_w, w_ssd_out, w_four_out, w_o, w_router, b_router, w_gate_up, b_gate_up, w_down, b_down,
           final_norm_w):
    bsz, n, d = x.shape
    n_ctx = ctx.shape[1]
    depth = w_mod.shape[0]
    assert depth == 1, "a stacked model would also need the context stream's residual update"
    d_inner = N_HEADS * HEAD_DIM
    off_dt = d_inner + 2 * BC_WIDTH
    m = bsz * n
    x2d = x.reshape(m, d)
    tm = min(512, n)

    for layer in range(depth):
        rows = -(-(bsz + 1) // 8) * 8
        c_rows = jnp.zeros((rows, d), F32).at[:bsz].set(c).at[bsz].set(c_ctx)
        mod = _modulation(c_rows, w_mod[layer], b_mod[layer])
        mods = [mod[:bsz, i * d:(i + 1) * d].reshape(bsz, 1, d) for i in range(N_MOD)]
        sh1, sc1, g1, sh2, sc2, g2 = mods
        sh1_c = mod[bsz:bsz + 1, :d].reshape(1, 1, d)
        sc1_c = mod[bsz:bsz + 1, d:2 * d].reshape(1, 1, d)

        wl = w_in[layer]
        w_main = jnp.concatenate([wl[:, :off_dt], wl[:, off_dt + 2 * N_HEADS:]], axis=1).astype(BF16)
        w_dt = jnp.zeros((d, 2 * LANES), F32)
        w_dt = w_dt.at[:, :N_HEADS].set(wl[:, off_dt:off_dt + N_HEADS])
        w_dt = w_dt.at[:, LANES:LANES + N_HEADS].set(wl[:, off_dt + N_HEADS:off_dt + 2 * N_HEADS]).astype(BF16)
        pad = jnp.zeros((2, 1, LANES - N_HEADS), F32)
        dt_bias2 = jnp.concatenate([dt_bias[layer].reshape(2, 1, N_HEADS), pad], axis=-1)
        a_log2 = jnp.concatenate([a_log[layer].reshape(2, 1, N_HEADS), pad], axis=-1)

        w_ctx = w_main[:, :off_dt]
        pc, dtc = _in_proj(ctx.reshape(bsz * n_ctx, d), norm1_w[layer], sc1_c, sh1_c, w_ctx, w_dt,
                           bsz * n_ctx, min(512, n_ctx))
        xbc_c = _conv_silu(pc.reshape(bsz, n_ctx, off_dt), conv_w[layer][:, :off_dt], conv_b[layer][:off_dt], off_dt)
        h0 = jnp.zeros((bsz, 2, N_GROUPS, D_STATE, HEADS_PER_GROUP * HEAD_DIM), F32)
        h_ctx = _ssd(xbc_c, dtc.reshape(bsz, n_ctx, 2 * LANES), dt_bias2, a_log2, h0, with_y=False)

        p, dtl = _in_proj(x2d, norm1_w[layer], sc1, sh1, w_main, w_dt, n, tm)
        n_conv = off_dt + 2 * BC_WIDTH
        xbc = _conv_silu(p.reshape(bsz, n, -1), conv_w[layer], conv_b[layer], n_conv)
        y2 = _ssd(xbc, dtl.reshape(bsz, n, 2 * LANES), dt_bias2, a_log2, h_ctx, with_y=True)
        ff = _fnet_mix(p, (n_conv + d_inner) // d, bsz, n)

        dsk = jnp.repeat(d_skip[layer].astype(F32), HEAD_DIM).reshape(1, d_inner)
        wr = jnp.zeros((d, LANES), F32).at[:, :N_EXPERTS].set(w_router[layer])
        br = jnp.full((1, LANES), NEG_BIG, F32).at[0, :N_EXPERTS].set(b_router[layer])
        x1, h2, ti, tw, rk, cnt = _tail(
            y2.reshape(2, m, d_inner), xbc.reshape(m, n_conv), p, ff, x2d, g1, sc2, sh2, dsk,
            ssd_norm_w[layer].reshape(1, d_inner), w_ssd_out[layer].astype(BF16), w_four_out[layer].astype(BF16),
            w_o[layer].astype(BF16), norm2_w[layer].reshape(1, d), wr, br, n, tm)

        counts = cnt[0, :N_EXPERTS].astype(jnp.int32)
        row_tok, tile_expert, n_used, dest = _dispatch_plan(ti[:, :TOP_K], rk[:, :TOP_K], counts, MOE_TILE)
        xs = _gather_rows(h2, row_tok)
        ys = _experts(xs, tile_expert, n_used, w_gate_up[layer], b_gate_up[layer].reshape(N_EXPERTS, 1, -1),
                      w_down[layer], b_down[layer].reshape(N_EXPERTS, 1, -1))
        y4 = _gather_rows(ys, dest.T.reshape(-1)).reshape(TOP_K, m, d // 2)
        x2d = _final(x1, y4, tw, g2, final_norm_w, n, tm)
    return x2d.reshape(bsz, n, d)
```

```python
import functools
import math

import numpy as np
import jax
import jax.numpy as jnp
from jax import lax
from jax.experimental import pallas as pl
from jax.experimental.pallas import tpu as pltpu
from jax.experimental.pallas import tpu_sc as plsc

F32 = jnp.float32
BF16 = jnp.bfloat16

EPS = 1e-6
GRID_W = 64
N_MOD = 6
F_GROUPS = 8
F_GROUP_DIM = 128
HEAD_DIM = 64
N_HEADS = 32
N_GROUPS = 4
HEADS_PER_GROUP = N_HEADS // N_GROUPS
D_STATE = 128
BC_WIDTH = N_GROUPS * D_STATE
CONV_K = 5
CHUNK = 128
N_EXPERTS = 32
TOP_K = 4
SWIGLU_LIMIT = 7.0
SWIGLU_ALPHA = 1.702

LANES = 128
VMEM_LIMIT_BYTES = 56 * 1024 * 1024
NEG_BIG = -1e30
MOE_TILE = 512
SC_GATHER_WINDOW = 64


def _cparams(*sem):
    return pltpu.CompilerParams(dimension_semantics=sem, vmem_limit_bytes=VMEM_LIMIT_BYTES)


def _sigmoid(v):
    return 1.0 / (1.0 + jnp.exp(-v))


def _silu(v):
    return v * _sigmoid(v)


def _softplus(v):
    return jnp.maximum(v, 0.0) + jnp.log(1.0 + jnp.exp(-jnp.abs(v)))


def _pack_bf16_pair(v):
    w = v.shape[1] // 2
    bits = lax.bitcast_convert_type(v.astype(BF16).astype(F32), jnp.uint32)
    return bits[:, :w] | (bits[:, w:] >> 16)


def _unpack_bf16_pair(p):
    hi = lax.bitcast_convert_type(p & jnp.uint32(0xFFFF0000), F32)
    lo = lax.bitcast_convert_type(p << 16, F32)
    return hi, lo


def _mod_kernel(c_ref, w_ref, b_ref, o_ref):
    s = _silu(c_ref[...]).astype(BF16)
    o_ref[...] = jnp.dot(s, w_ref[...].astype(BF16), preferred_element_type=F32) + b_ref[...]


def _modulation(c_rows, w_mod, b_mod):
    rows, d = c_rows.shape
    n_out = w_mod.shape[1]
    tn = 1024
    return pl.pallas_call(
        _mod_kernel,
        grid=(n_out // tn,),
        in_specs=[pl.BlockSpec((rows, d), lambda j: (0, 0)),
                  pl.BlockSpec((d, tn), lambda j: (0, j)),
                  pl.BlockSpec((1, tn), lambda j: (0, j))],
        out_specs=pl.BlockSpec((rows, tn), lambda j: (0, j)),
        out_shape=jax.ShapeDtypeStruct((rows, n_out), F32),
        compiler_params=_cparams("arbitrary"),
        name="modulation",
    )(c_rows, w_mod, b_mod.reshape(1, n_out))


_INPROJ_TN = 1024


def _inproj_kernel(x_ref, nw_ref, sc_ref, sh_ref, w_ref, wdt_ref, p_ref, dt_ref):
    x = x_ref[...]
    inv = lax.rsqrt(jnp.mean(x * x, axis=-1, keepdims=True) + EPS)
    h = (x * inv * nw_ref[...]) * (1.0 + sc_ref[...]) + sh_ref[...]
    hb = h.astype(BF16)
    dt_ref[...] = jnp.dot(hb, wdt_ref[...], preferred_element_type=F32)
    for j in range(w_ref.shape[1] // _INPROJ_TN):
        cols = slice(j * _INPROJ_TN, (j + 1) * _INPROJ_TN)
        p_ref[:, cols] = jnp.dot(hb, w_ref[:, cols], preferred_element_type=F32).astype(p_ref.dtype)


def _in_proj(x2d, norm_w, sc, sh, w_main, w_dt, rows_per_mod, tm):
    m, d = x2d.shape
    n = w_main.shape[1]
    ndt = w_dt.shape[1]
    resident = dict(pipeline_mode=pl.Buffered(1))
    return pl.pallas_call(
        _inproj_kernel,
        grid=(m // tm,),
        in_specs=[pl.BlockSpec((tm, d), lambda i: (i, 0)),
                  pl.BlockSpec((1, d), lambda i: (0, 0), **resident),
                  pl.BlockSpec((None, 1, d), lambda i: ((i * tm) // rows_per_mod, 0, 0)),
                  pl.BlockSpec((None, 1, d), lambda i: ((i * tm) // rows_per_mod, 0, 0)),
                  pl.BlockSpec((d, n), lambda i: (0, 0), **resident),
                  pl.BlockSpec((d, ndt), lambda i: (0, 0), **resident)],
        out_specs=[pl.BlockSpec((tm, n), lambda i: (i, 0)),
                   pl.BlockSpec((tm, ndt), lambda i: (i, 0))],
        out_shape=[jax.ShapeDtypeStruct((m, n), BF16),
                   jax.ShapeDtypeStruct((m, ndt), F32)],
        compiler_params=_cparams("arbitrary"),
        name="in_proj",
    )(x2d, norm_w.reshape(1, d), sc, sh, w_main, w_dt)


_CONV_HALO = 16


def _conv_kernel(p_ref, w_ref, b_ref, o_ref, *, n, rc):
    tc = o_ref.shape[-1]
    w = w_ref[...]
    bias = b_ref[...]
    zeros = jnp.zeros((_CONV_HALO, tc), F32)
    tot = rc + 2 * _CONV_HALO
    for r0 in range(0, n, rc):
        top = zeros if r0 == 0 else p_ref[r0 - _CONV_HALO:r0, :].astype(F32)
        bot = zeros if r0 + rc >= n else p_ref[r0 + rc:r0 + rc + _CONV_HALO, :].astype(F32)
        blk = jnp.concatenate([top, p_ref[r0:r0 + rc, :].astype(F32), bot], axis=0)
        acc = jnp.broadcast_to(bias, (rc, tc))
        for k in range(CONV_K):
            shift = (CONV_K // 2 - k) % tot
            rolled = blk if shift == 0 else pltpu.roll(blk, shift, axis=0)
            acc = acc + w[k:k + 1, :] * rolled[_CONV_HALO:_CONV_HALO + rc, :]
        o_ref[r0:r0 + rc, :] = _silu(acc).astype(o_ref.dtype)


def _conv_silu(p3d, conv_w, conv_b, n_ch):
    bsz, n, _ = p3d.shape
    tc = 512
    rc = min(256, n)
    return pl.pallas_call(
        functools.partial(_conv_kernel, n=n, rc=rc),
        grid=(bsz, n_ch // tc),
        in_specs=[pl.BlockSpec((None, n, tc), lambda b, j: (b, 0, j)),
                  pl.BlockSpec((CONV_K, tc), lambda b, j: (0, j)),
                  pl.BlockSpec((1, tc), lambda b, j: (0, j))],
        out_specs=pl.BlockSpec((None, n, tc), lambda b, j: (b, 0, j)),
        out_shape=jax.ShapeDtypeStruct((bsz, n, n_ch), BF16),
        compiler_params=_cparams("arbitrary", "arbitrary"),
        name="conv_silu",
    )(p3d, conv_w, conv_b.reshape(1, -1))


LOG2E = 1.4426950408889634


def _head_expand_matrix():
    e = np.zeros((LANES, N_HEADS * HEAD_DIM), np.float32)
    for h in range(N_HEADS):
        e[h, h * HEAD_DIM:(h + 1) * HEAD_DIM] = 1.0
    return jnp.asarray(e, BF16)


def _ssd_chunk_terms(dt_raw, dt_bias, a_log, expand, tri, fwd):
    q = CHUNK
    dt = _softplus(dt_raw + dt_bias)
    da = dt * (-jnp.exp(a_log))
    hi = da.astype(BF16)
    r1 = da - hi.astype(F32)
    mid = r1.astype(BF16)
    lo = (r1 - mid.astype(F32)).astype(BF16)
    ones_tri = jnp.where(tri, 1.0, 0.0).astype(BF16)
    cs3 = jnp.dot(ones_tri, jnp.concatenate([hi, mid, lo], axis=1), preferred_element_type=F32)
    cs = cs3[:, :LANES] + cs3[:, LANES:2 * LANES] + cs3[:, 2 * LANES:]
    tot = jnp.where(fwd, cs[q - 1:q, :], cs[0:1, :])
    dte = dt * jnp.exp(tot - cs)
    dec = jnp.exp(tot)
    dec_hi = dec.astype(BF16)
    dec_lo = (dec - dec_hi.astype(F32)).astype(BF16)
    dec2 = jnp.concatenate([dec_hi, dec_lo, jnp.zeros((6, LANES), BF16)], axis=0)
    stacked = jnp.concatenate([dt.astype(BF16), jnp.exp(cs).astype(BF16), dte.astype(BF16), dec2], axis=0)
    spread = jnp.dot(stacked, expand, preferred_element_type=F32)
    dt_e = spread[:q].astype(BF16)
    ecs_e = spread[q:2 * q]
    dte_e = spread[2 * q:3 * q].astype(BF16)
    dec_e = jnp.broadcast_to(spread[3 * q:3 * q + 1] + spread[3 * q + 1:3 * q + 2], (8, spread.shape[1]))
    cs2 = cs * LOG2E
    return cs2, cs2.T, dt_e, ecs_e, dte_e, dec_e


def _ssd_kernel(*refs, with_y, nc):
    if with_y:
        (x_ref, b_ref, c_ref, dt_ref, dtn_ref, dtb_ref, alog_ref, exp_ref, h0_ref, y_ref,
         s_scr, cs_scr, dee_scr, dce_scr, dtx_scr, ecs_scr) = refs
    else:
        (x_ref, b_ref, dt_ref, dtn_ref, dtb_ref, alog_ref, exp_ref, h0_ref, hout_ref,
         s_scr, cs_scr, dee_scr, dce_scr) = refs
    q = CHUNK
    gw = HEADS_PER_GROUP * HEAD_DIM
    d = pl.program_id(1)
    c = pl.program_id(2)
    row = lax.broadcasted_iota(jnp.int32, (q, q), 0)
    col = lax.broadcasted_iota(jnp.int32, (q, q), 1)
    fwd = d == 0
    tri = jnp.where(fwd, row - col, col - row) >= 0
    first_half = col < HEAD_DIM

    def store_terms(slot, terms):
        cs2, cs2_t, dt_e, ecs_e, dte_e, dec_e = terms
        cs_scr[slot, 0] = cs2
        cs_scr[slot, 1] = cs2_t
        dee_scr[slot] = dte_e
        dce_scr[slot] = dec_e
        if with_y:
            dtx_scr[slot] = dt_e
            ecs_scr[slot] = ecs_e

    @pl.when(c == 0)
    def _():
        s_scr[...] = h0_ref[...]
        store_terms(0, _ssd_chunk_terms(dt_ref[...], dtb_ref[...], alog_ref[...], exp_ref[...], tri, fwd))

    slot = c % 2
    cs2 = cs_scr[slot, 0]
    cs2_t = cs_scr[slot, 1]
    store_terms(1 - slot, _ssd_chunk_terms(dtn_ref[...], dtb_ref[...], alog_ref[...], exp_ref[...], tri, fwd))

    for g in range(N_GROUPS):
        cols = slice(g * gw, (g + 1) * gw)
        bg = b_ref[:, g * D_STATE:(g + 1) * D_STATE]
        xg = x_ref[:, cols]
        s_old = s_scr[g]
        if with_y:
            cg = c_ref[:, g * D_STATE:(g + 1) * D_STATE]
            cb = lax.dot_general(cg, bg, (((1,), (1,)), ((), ())), preferred_element_type=F32)
            y_off = jnp.dot(cg, s_old.astype(BF16), preferred_element_type=F32)
            xdt = xg * dtx_scr[slot, :, cols]
            for k in range(HEADS_PER_GROUP // 2):
                pair = g * (HEADS_PER_GROUP // 2) + k
                lanes = slice(pair * LANES, (pair + 1) * LANES)
                xp = xdt[:, k * LANES:(k + 1) * LANES]
                xz = jnp.zeros_like(xp)
                lhs = []
                for hh in (2 * pair, 2 * pair + 1):
                    colb = jnp.broadcast_to(cs2[:, hh:hh + 1], (q, q))
                    rowb = cs2_t[hh:hh + 1, :]
                    lhs.append((cb * jnp.exp2(jnp.where(tri, colb - rowb, -jnp.inf))).astype(BF16))
                y_diag = jnp.dot(jnp.concatenate(lhs, axis=1),
                                 jnp.concatenate([jnp.where(first_half, xp, xz), jnp.where(first_half, xz, xp)], axis=0),
                                 preferred_element_type=F32)
                y = y_diag + ecs_scr[slot, :, lanes] * y_off[:, k * LANES:(k + 1) * LANES]
                y_ref[:, lanes] = y.astype(y_ref.dtype)
        xdte = xg * dee_scr[slot, :, cols]
        upd = lax.dot_general(bg, xdte, (((0,), (0,)), ((), ())), preferred_element_type=F32)
        s_scr[g] = s_old * dce_scr[slot, 0:1, cols] + upd

    if not with_y:
        @pl.when(c == nc - 1)
        def _():
            hout_ref[...] = s_scr[...]


def _ssd(xbc, dt_raw, dt_bias2, a_log2, h0, with_y):
    bsz, n, _ = xbc.shape
    nc = n // CHUNK
    d_inner = N_HEADS * HEAD_DIM
    gw = HEADS_PER_GROUP * HEAD_DIM
    x_blk = d_inner // BC_WIDTH

    def ceff(dd, cc):
        return cc + dd * (nc - 1 - 2 * cc)

    def cnext(dd, cc):
        return ceff(dd, jnp.minimum(cc + 1, nc - 1))

    in_specs = [pl.BlockSpec((None, CHUNK, d_inner), lambda b, dd, cc: (b, ceff(dd, cc), 0)),
                pl.BlockSpec((None, CHUNK, BC_WIDTH), lambda b, dd, cc: (b, ceff(dd, cc), x_blk + dd))]
    args = [xbc, xbc]
    if with_y:
        in_specs.append(pl.BlockSpec((None, CHUNK, BC_WIDTH), lambda b, dd, cc: (b, ceff(dd, cc), x_blk + 2 + dd)))
        args.append(xbc)
    in_specs += [pl.BlockSpec((None, CHUNK, LANES), lambda b, dd, cc: (b, ceff(dd, cc), dd)),
                 pl.BlockSpec((None, CHUNK, LANES), lambda b, dd, cc: (b, cnext(dd, cc), dd)),
                 pl.BlockSpec((None, 1, LANES), lambda b, dd, cc: (dd, 0, 0)),
                 pl.BlockSpec((None, 1, LANES), lambda b, dd, cc: (dd, 0, 0)),
                 pl.BlockSpec((LANES, d_inner), lambda b, dd, cc: (0, 0)),
                 pl.BlockSpec((None, None, N_GROUPS, D_STATE, gw), lambda b, dd, cc: (b, dd, 0, 0, 0))]
    args += [dt_raw, dt_raw, dt_bias2, a_log2, _head_expand_matrix(), h0]
    scratch = [pltpu.VMEM((N_GROUPS, D_STATE, gw), F32),
               pltpu.VMEM((2, 2, CHUNK, LANES), F32),
               pltpu.VMEM((2, CHUNK, d_inner), BF16),
               pltpu.VMEM((2, 8, d_inner), F32)]
    if with_y:
        out_specs = pl.BlockSpec((None, None, CHUNK, d_inner), lambda b, dd, cc: (dd, b, ceff(dd, cc), 0))
        out_shape = jax.ShapeDtypeStruct((2, bsz, n, d_inner), BF16)
        scratch += [pltpu.VMEM((2, CHUNK, d_inner), BF16),
                    pltpu.VMEM((2, CHUNK, d_inner), F32)]
    else:
        out_specs = pl.BlockSpec((None, None, N_GROUPS, D_STATE, gw), lambda b, dd, cc: (b, dd, 0, 0, 0))
        out_shape = jax.ShapeDtypeStruct((bsz, 2, N_GROUPS, D_STATE, gw), F32)
    return pl.pallas_call(
        functools.partial(_ssd_kernel, with_y=with_y, nc=nc),
        grid=(bsz, 2, nc),
        in_specs=in_specs,
        out_specs=out_specs,
        out_shape=out_shape,
        scratch_shapes=scratch,
        compiler_params=_cparams("arbitrary", "arbitrary", "arbitrary"),
        name="ssd_scan" if with_y else "ssd_ctx_state",
    )(*args)


def _dft_mats(n):
    k = np.arange(n)
    ang = 2.0 * np.pi * ((k[:, None] * k[None, :]) % n) / n
    return np.cos(ang), np.sin(ang)


def _fnet_chan_kernel(u_ref, m_ref, o_ref):
    m = m_ref[...]
    for g in range(F_GROUPS):
        ug = u_ref[:, g * F_GROUP_DIM:(g + 1) * F_GROUP_DIM]
        pq = jnp.dot(ug, m, preferred_element_type=F32)
        o_ref[0, :, g * F_GROUP_DIM:(g + 1) * F_GROUP_DIM] = pq[:, :F_GROUP_DIM].astype(o_ref.dtype)
        o_ref[1, :, g * F_GROUP_DIM:(g + 1) * F_GROUP_DIM] = pq[:, F_GROUP_DIM:].astype(o_ref.dtype)


def _stage_lane_blocks(src_ref, scr, rows):
    width = src_ref.shape[-1]
    for t in range(2):
        v = src_ref[t].reshape(rows, width).astype(F32)
        for lb in range(width // LANES):
            scr[t, lb] = v[:, lb * LANES:(lb + 1) * LANES]


def _strided_rows(scr, t, start, count, stride, width):
    return jnp.concatenate([scr[t, lb, pl.ds(start, count, stride=stride), :] for lb in range(width // LANES)],
                           axis=1)


def _fnet_stage1_kernel(z_ref, m1_ref, twc_ref, tws_ref, o_ref, zs_scr, *, tcs):
    r = z_ref.shape[1]
    width = z_ref.shape[-1]
    _stage_lane_blocks(z_ref, zs_scr, r * tcs)
    m1 = m1_ref[...]
    for j in range(tcs):
        zz = jnp.concatenate([_strided_rows(zs_scr, 0, j, r, tcs, width),
                              _strided_rows(zs_scr, 1, j, r, tcs, width)], axis=0).astype(BF16)
        u = jnp.dot(m1, zz, preferred_element_type=F32)
        ur = u[:r]
        ui = u[r:]
        tc = twc_ref[j]
        ts = tws_ref[j]
        o_ref[0, j] = (ur * tc + ui * ts).astype(o_ref.dtype)
        o_ref[1, j] = (ui * tc - ur * ts).astype(o_ref.dtype)


def _fnet_stage2_kernel(u_ref, m2_ref, o_ref, us_scr, os_scr, *, tbs):
    c = u_ref.shape[1]
    width = u_ref.shape[-1]
    _stage_lane_blocks(u_ref, us_scr, c * tbs)
    m2 = m2_ref[...]
    for j in range(tbs):
        uu = jnp.concatenate([_strided_rows(us_scr, 0, j, c, tbs, width),
                              _strided_rows(us_scr, 1, j, c, tbs, width)], axis=0).astype(BF16)
        x = jnp.dot(m2, uu, preferred_element_type=F32)
        for lb in range(width // LANES):
            os_scr[lb, pl.ds(j, c, stride=tbs), :] = x[:, lb * LANES:(lb + 1) * LANES]
    out = jnp.concatenate([os_scr[lb] for lb in range(width // LANES)], axis=1)
    o_ref[...] = out.reshape(c, tbs, width).astype(o_ref.dtype)


def _fnet_mix(p2d, col_blk, bsz, n):
    m = p2d.shape[0]
    width = F_GROUPS * F_GROUP_DIM
    cgrid = GRID_W
    rgrid = n // cgrid
    scale = 1.0 / math.sqrt(n * F_GROUP_DIM)

    cc, sc = _dft_mats(F_GROUP_DIM)
    chan = jnp.asarray(np.concatenate([cc, sc], axis=1) * scale, BF16)
    tm = 512
    pq = pl.pallas_call(
        _fnet_chan_kernel,
        grid=(m // tm,),
        in_specs=[pl.BlockSpec((tm, width), lambda i: (i, col_blk)),
                  pl.BlockSpec((F_GROUP_DIM, 2 * F_GROUP_DIM), lambda i: (0, 0))],
        out_specs=pl.BlockSpec((2, tm, width), lambda i: (0, i, 0)),
        out_shape=jax.ShapeDtypeStruct((2, m, width), BF16),
        compiler_params=_cparams("arbitrary"),
        name="fnet_chan",
    )(p2d, chan)

    cr, sr = _dft_mats(rgrid)
    m1 = jnp.asarray(np.block([[cr, -sr], [-sr, -cr]]), BF16)
    bb = np.arange(rgrid)[None, :]
    ci = np.arange(cgrid)[:, None]
    ang = 2.0 * np.pi * ((ci * bb) % n) / n
    twc = jnp.asarray(np.cos(ang)[:, :, None], F32)
    tws = jnp.asarray(np.sin(ang)[:, :, None], F32)
    tcs = 16
    z5 = pq.reshape(2, bsz, rgrid, cgrid, width)
    u5 = pl.pallas_call(
        functools.partial(_fnet_stage1_kernel, tcs=tcs),
        grid=(bsz, cgrid // tcs),
        in_specs=[pl.BlockSpec((2, None, rgrid, tcs, width), lambda b, j: (0, b, 0, j, 0)),
                  pl.BlockSpec((2 * rgrid, 2 * rgrid), lambda b, j: (0, 0)),
                  pl.BlockSpec((tcs, rgrid, 1), lambda b, j: (j, 0, 0)),
                  pl.BlockSpec((tcs, rgrid, 1), lambda b, j: (j, 0, 0))],
        out_specs=pl.BlockSpec((2, None, tcs, rgrid, width), lambda b, j: (0, b, j, 0, 0)),
        out_shape=jax.ShapeDtypeStruct((2, bsz, cgrid, rgrid, width), BF16),
        scratch_shapes=[pltpu.VMEM((2, width // LANES, rgrid * tcs, LANES), F32)],
        compiler_params=_cparams("arbitrary", "arbitrary"),
        name="fnet_stage1",
    )(z5, m1, twc, tws)

    cc2, sc2 = _dft_mats(cgrid)
    m2 = jnp.asarray(np.concatenate([cc2, sc2], axis=1), BF16)
    tbs = min(16, rgrid)
    out = pl.pallas_call(
        functools.partial(_fnet_stage2_kernel, tbs=tbs),
        grid=(bsz, rgrid // tbs),
        in_specs=[pl.BlockSpec((2, None, cgrid, tbs, width), lambda b, j: (0, b, 0, j, 0)),
                  pl.BlockSpec((cgrid, 2 * cgrid), lambda b, j: (0, 0))],
        out_specs=pl.BlockSpec((None, cgrid, tbs, width), lambda b, j: (b, 0, j, 0)),
        out_shape=jax.ShapeDtypeStruct((bsz, cgrid, rgrid, width), BF16),
        scratch_shapes=[pltpu.VMEM((2, width // LANES, cgrid * tbs, LANES), F32),
                        pltpu.VMEM((width // LANES, cgrid * tbs, LANES), F32)],
        compiler_params=_cparams("arbitrary", "arbitrary"),
        name="fnet_stage2",
    )(u5, m2)
    return out.reshape(m, width)


def _tail_kernel(yf_ref, yb_ref, xs_ref, z_ref, ff_ref, gf_ref, gs_ref, x_ref, g1_ref, sc2_ref, sh2_ref,
                 dsk_ref, snw_ref, wso_ref, wfo_ref, wo_ref, n2w_ref, wr_ref, br_ref,
                 x1_ref, h2_ref, ti_ref, tw_ref, rk_ref, cnt_ref, carry_scr):
    y = yf_ref[0].astype(F32) + yb_ref[0].astype(F32) + dsk_ref[...] * xs_ref[...].astype(F32)
    g = y * _silu(z_ref[...].astype(F32))
    inv = lax.rsqrt(jnp.mean(g * g, axis=-1, keepdims=True) + EPS)
    gn = (g * inv * snw_ref[...]).astype(BF16)
    y_ssd = jnp.dot(gn, wso_ref[...], preferred_element_type=F32)
    y_four = jnp.dot(ff_ref[...], wfo_ref[...], preferred_element_type=F32)
    t = _sigmoid(gf_ref[...].astype(F32)) * y_four + _sigmoid(gs_ref[...].astype(F32)) * y_ssd
    mix = jnp.dot(t.astype(BF16), wo_ref[...], preferred_element_type=F32)
    x1 = x_ref[...] + g1_ref[...] * mix
    x1_ref[...] = x1
    inv2 = lax.rsqrt(jnp.mean(x1 * x1, axis=-1, keepdims=True) + EPS)
    h2 = (x1 * inv2 * n2w_ref[...]) * (1.0 + sc2_ref[...]) + sh2_ref[...]
    h2_ref[...] = _pack_bf16_pair(h2)
    h_hi = h2.astype(BF16)
    h_lo = (h2 - h_hi.astype(F32)).astype(BF16)
    wr = wr_ref[...]
    w_hi = wr.astype(BF16)
    w_lo = (wr - w_hi.astype(F32)).astype(BF16)
    logits = (jnp.dot(h_hi, w_hi, preferred_element_type=F32)
              + jnp.dot(h_hi, w_lo, preferred_element_type=F32)
              + jnp.dot(h_lo, w_hi, preferred_element_type=F32)) + br_ref[...]
    tm = logits.shape[0]
    lane = lax.broadcasted_iota(jnp.int32, (tm, LANES), 1)
    vals, idxs = [], []
    cur = logits
    for _ in range(TOP_K):
        mx = jnp.max(cur, axis=-1, keepdims=True)
        ix = jnp.min(jnp.where(cur == mx, lane, LANES), axis=-1, keepdims=True)
        vals.append(mx)
        idxs.append(ix)
        cur = jnp.where(lane == ix, NEG_BIG * 2.0, cur)
    es = [jnp.exp(v - vals[0]) for v in vals]
    den = es[0] + es[1] + es[2] + es[3]
    ti = jnp.zeros((tm, LANES), jnp.int32)
    tw = jnp.zeros((tm, LANES), F32)
    for k in range(TOP_K):
        ti = jnp.where(lane == k, idxs[k], ti)
        tw = jnp.where(lane == k, es[k] / den, tw)
    ti_ref[...] = ti
    tw_ref[...] = tw

    @pl.when(pl.program_id(0) == 0)
    def _():
        carry_scr[...] = jnp.zeros(carry_scr.shape, F32)

    onehots = [jnp.where(lane == ix, 1.0, 0.0) for ix in idxs]
    cnt = onehots[0] + onehots[1] + onehots[2] + onehots[3]
    r_i = lax.broadcasted_iota(jnp.int32, (tm, tm), 0)
    c_i = lax.broadcasted_iota(jnp.int32, (tm, tm), 1)
    earlier = jnp.where(r_i > c_i, 1.0, 0.0).astype(BF16)
    before = jnp.dot(earlier, cnt.astype(BF16), preferred_element_type=F32) + carry_scr[0:1, :]
    rk = jnp.zeros((tm, LANES), jnp.int32)
    for k in range(TOP_K):
        rank_k = jnp.sum(onehots[k] * before, axis=-1, keepdims=True)
        rk = jnp.where(lane == k, rank_k.astype(jnp.int32), rk)
    rk_ref[...] = rk
    total = carry_scr[...] + jnp.sum(cnt, axis=0, keepdims=True)
    carry_scr[...] = total
    cnt_ref[...] = total


def _tail(y2, xbc2d, p2d, ff, x2d, g1, sc2, sh2, dsk, snw, wso, wfo, wo, n2w, wr, br, rows_per_mod, tm):
    m, d = x2d.shape
    di = y2.shape[-1]
    zb = (2 * di) // di
    fb = (3 * di) // d
    row = lambda i: (i, 0)
    modrow = lambda i: ((i * tm) // rows_per_mod, 0, 0)
    const = lambda i: (0, 0)
    in_specs = [pl.BlockSpec((1, tm, di), lambda i: (0, i, 0)),
                pl.BlockSpec((1, tm, di), lambda i: (1, i, 0)),
                pl.BlockSpec((tm, di), row),
                pl.BlockSpec((tm, di), lambda i: (i, zb)),
                pl.BlockSpec((tm, d), row),
                pl.BlockSpec((tm, d), lambda i: (i, fb + 1)),
                pl.BlockSpec((tm, d), lambda i: (i, fb + 2)),
                pl.BlockSpec((tm, d), row),
                pl.BlockSpec((None, 1, d), modrow),
                pl.BlockSpec((None, 1, d), modrow),
                pl.BlockSpec((None, 1, d), modrow),
                pl.BlockSpec((1, di), const),
                pl.BlockSpec((1, di), const),
                pl.BlockSpec((di, d), const),
                pl.BlockSpec((d, d), const),
                pl.BlockSpec((d, d), const),
                pl.BlockSpec((1, d), const),
                pl.BlockSpec((d, LANES), const),
                pl.BlockSpec((1, LANES), const)]
    out_specs = [pl.BlockSpec((tm, d), row), pl.BlockSpec((tm, d // 2), row),
                 pl.BlockSpec((tm, LANES), row), pl.BlockSpec((tm, LANES), row), pl.BlockSpec((tm, LANES), row),
                 pl.BlockSpec((8, LANES), const)]
    out_shape = [jax.ShapeDtypeStruct((m, d), F32), jax.ShapeDtypeStruct((m, d // 2), jnp.uint32),
                 jax.ShapeDtypeStruct((m, LANES), jnp.int32), jax.ShapeDtypeStruct((m, LANES), F32),
                 jax.ShapeDtypeStruct((m, LANES), jnp.int32), jax.ShapeDtypeStruct((8, LANES), F32)]
    return pl.pallas_call(
        _tail_kernel,
        grid=(m // tm,),
        in_specs=in_specs,
        out_specs=out_specs,
        out_shape=out_shape,
        scratch_shapes=[pltpu.VMEM((8, LANES), F32)],
        compiler_params=_cparams("arbitrary"),
        name="tail",
    )(y2, y2, xbc2d, p2d, ff, p2d, p2d, x2d, g1, sc2, sh2, dsk, snw, wso, wfo, wo, n2w, wr, br)


def _gather_rows(src, idx):
    n_out = idx.shape[0]
    width = src.shape[1]
    win = SC_GATHER_WINDOW
    info = plsc.get_sparse_core_info()
    n_cores, n_workers = info.num_cores, info.num_cores * info.num_subcores
    per_worker = n_out // n_workers
    assert per_worker * n_workers == n_out and per_worker % (2 * win) == 0
    mesh = plsc.VectorSubcoreMesh(core_axis_name="core", subcore_axis_name="subcore")

    @functools.partial(
        pl.kernel, out_type=jax.ShapeDtypeStruct((n_out, width), src.dtype), mesh=mesh,
        scratch_types=[pltpu.VMEM((win,), jnp.int32), pltpu.VMEM((win,), jnp.int32),
                       pltpu.VMEM((win, width), src.dtype), pltpu.VMEM((win, width), src.dtype),
                       pltpu.SemaphoreType.DMA, pltpu.SemaphoreType.DMA,
                       pltpu.SemaphoreType.DMA, pltpu.SemaphoreType.DMA],
        name="gather_rows")
    def gather(src_hbm, idx_hbm, out_hbm, idx_a, idx_b, rows_a, rows_b, gsem_a, gsem_b, ssem_a, ssem_b):
        worker = lax.axis_index("subcore") * n_cores + lax.axis_index("core")
        base = worker * per_worker

        @pl.loop(0, per_worker, step=2 * win)
        def _(off):
            pltpu.sync_copy(idx_hbm.at[pl.ds(base + off, win)], idx_a)
            ga = pltpu.async_copy(src_hbm.at[idx_a], rows_a, gsem_a)
            pltpu.sync_copy(idx_hbm.at[pl.ds(base + off + win, win)], idx_b)
            gb = pltpu.async_copy(src_hbm.at[idx_b], rows_b, gsem_b)
            ga.wait()
            sa = pltpu.async_copy(rows_a, out_hbm.at[pl.ds(base + off, win)], ssem_a)
            gb.wait()
            sb = pltpu.async_copy(rows_b, out_hbm.at[pl.ds(base + off + win, win)], ssem_b)
            sa.wait()
            sb.wait()

    return gather(src, idx)


def _scatter_rows(src, dest, n_rows):
    m, width = src.shape
    win = SC_GATHER_WINDOW
    info = plsc.get_sparse_core_info()
    n_cores, n_workers = info.num_cores, info.num_cores * info.num_subcores
    per_worker = m // n_workers
    assert per_worker * n_workers == m and per_worker % win == 0
    mesh = plsc.VectorSubcoreMesh(core_axis_name="core", subcore_axis_name="subcore")

    @functools.partial(
        pl.kernel, out_type=jax.ShapeDtypeStruct((n_rows, width), src.dtype), mesh=mesh,
        scratch_types=[pltpu.VMEM((win, width), src.dtype)]
        + [pltpu.VMEM((win,), jnp.int32)] * TOP_K + [pltpu.SemaphoreType.DMA] * TOP_K,
        name="scatter_rows")
    def scatter(src_hbm, dest_hbm, out_hbm, rows_v, *rest):
        idx_v, sems = rest[:TOP_K], rest[TOP_K:]
        worker = lax.axis_index("subcore") * n_cores + lax.axis_index("core")
        base = worker * per_worker

        @pl.loop(0, per_worker, step=win)
        def _(off):
            t0 = base + off
            pltpu.sync_copy(src_hbm.at[pl.ds(t0, win)], rows_v)
            for k in range(TOP_K):
                pltpu.sync_copy(dest_hbm.at[pl.ds(k * m + t0, win)], idx_v[k])
            copies = [pltpu.async_copy(rows_v, out_hbm.at[idx_v[k]], sems[k]) for k in range(TOP_K)]
            for cp in copies:
                cp.wait()

    return scatter(src, dest)


def _expert_kernel(te_ref, nu_ref, tv_ref, x_ref, wgu_ref, bgu_ref, wd_ref, bd_ref, o_ref, wgu_scr, wd_scr):
    i = pl.program_id(0)
    used = i < nu_ref[0]
    new_expert = (i == 0) | (te_ref[i] != te_ref[jnp.maximum(i - 1, 0)])

    @pl.when(used & new_expert)
    def _():
        wgu_scr[...] = wgu_ref[...].astype(BF16)
        wd_scr[...] = wd_ref[...].astype(BF16)

    @pl.when(used)
    def _():
        dff = wd_ref.shape[0]
        half = wgu_ref.shape[0] // 2
        rows = lax.broadcasted_iota(jnp.int32, x_ref.shape, 0)
        xa, xb = _unpack_bf16_pair(jnp.where(rows < tv_ref[i], x_ref[...], jnp.uint32(0)))
        gu = (jnp.dot(xa.astype(BF16), wgu_scr[:half, :], preferred_element_type=F32)
              + jnp.dot(xb.astype(BF16), wgu_scr[half:, :], preferred_element_type=F32)) + bgu_ref[...]
        gate = jnp.minimum(gu[:, :dff], SWIGLU_LIMIT)
        up = jnp.clip(gu[:, dff:], -SWIGLU_LIMIT, SWIGLU_LIMIT)
        act = (up + 1.0) * gate * _sigmoid(SWIGLU_ALPHA * gate)
        y = jnp.dot(act.astype(BF16), wd_scr[...], preferred_element_type=F32) + bd_ref[...]
        o_ref[...] = _pack_bf16_pair(y)

    @pl.when(jnp.logical_not(used))
    def _():
        o_ref[...] = jnp.zeros(o_ref.shape, o_ref.dtype)


def _experts(xs, tile_expert, n_used, tile_valid, wgu, bgu, wd, bd):
    rows, dh = xs.shape
    d = 2 * dh
    tm = MOE_TILE
    n_tiles = rows // tm
    dff2 = wgu.shape[-1]

    def tile(i, te, nu, tv):
        return (jnp.minimum(i, nu[0] - 1), 0)

    def wsel(i, te, nu, tv):
        return (te[jnp.minimum(i, nu[0] - 1)], 0, 0)

    grid_spec = pltpu.PrefetchScalarGridSpec(
        num_scalar_prefetch=3,
        grid=(n_tiles,),
        in_specs=[pl.BlockSpec((tm, dh), tile),
                  pl.BlockSpec((None, d, dff2), wsel),
                  pl.BlockSpec((None, 1, dff2), wsel),
                  pl.BlockSpec((None, dff2 // 2, d), wsel),
                  pl.BlockSpec((None, 1, d), wsel)],
        out_specs=pl.BlockSpec((tm, dh), lambda i, te, nu, tv: (i, 0)),
        scratch_shapes=[pltpu.VMEM((d, dff2), BF16), pltpu.VMEM((dff2 // 2, d), BF16)],
    )
    return pl.pallas_call(
        _expert_kernel,
        grid_spec=grid_spec,
        out_shape=jax.ShapeDtypeStruct((rows, dh), jnp.uint32),
        compiler_params=_cparams("arbitrary"),
        name="experts",
    )(tile_expert, n_used, tile_valid, xs, wgu, bgu, wd, bd)


def _final_kernel(x1_ref, ya_ref, yb_ref, yc_ref, yd_ref, tw_ref, g2_ref, fw_ref, o_ref):
    d = x1_ref.shape[-1]
    half = d // 2
    tw = tw_ref[...]
    acc_hi = jnp.zeros((x1_ref.shape[0], half), F32)
    acc_lo = jnp.zeros((x1_ref.shape[0], half), F32)
    for k, y_ref in enumerate((ya_ref, yb_ref, yc_ref, yd_ref)):
        y_hi, y_lo = _unpack_bf16_pair(y_ref[...])
        acc_hi = acc_hi + tw[:, k:k + 1] * y_hi
        acc_lo = acc_lo + tw[:, k:k + 1] * y_lo
    x_hi = x1_ref[:, :half] + g2_ref[:, :half] * acc_hi
    x_lo = x1_ref[:, half:] + g2_ref[:, half:] * acc_lo
    ms = (jnp.sum(x_hi * x_hi, axis=-1, keepdims=True) + jnp.sum(x_lo * x_lo, axis=-1, keepdims=True)) / d
    inv = lax.rsqrt(ms + EPS)
    o_ref[:, :half] = x_hi * inv * fw_ref[:, :half]
    o_ref[:, half:] = x_lo * inv * fw_ref[:, half:]


def _final(x1, y4, tw, g2, fw, rows_per_mod, tm):
    m, d = x1.shape
    return pl.pallas_call(
        _final_kernel,
        grid=(m // tm,),
        in_specs=[pl.BlockSpec((tm, d), lambda i: (i, 0))]
        + [pl.BlockSpec((None, tm, d // 2), functools.partial(lambda k, i: (k, i, 0), k)) for k in range(TOP_K)]
        + [pl.BlockSpec((tm, LANES), lambda i: (i, 0)),
           pl.BlockSpec((None, 1, d), lambda i: ((i * tm) // rows_per_mod, 0, 0)),
           pl.BlockSpec((1, d), lambda i: (0, 0))],
        out_specs=pl.BlockSpec((tm, d), lambda i: (i, 0)),
        out_shape=jax.ShapeDtypeStruct((m, d), F32),
        compiler_params=_cparams("arbitrary"),
        name="final",
    )(x1, y4, y4, y4, y4, tw, g2, fw.reshape(1, d))


def _dispatch_plan(e, rank, counts, tm):
    n_assign = e.size
    padded = (counts + tm - 1) // tm * tm
    pad_end = jnp.cumsum(padded)
    pad_start = pad_end - padded
    dest = (pad_start[e] + rank).astype(jnp.int32)
    n_tiles = n_assign // tm + N_EXPERTS
    tile_start = jnp.arange(n_tiles, dtype=jnp.int32) * tm
    tile_expert = jnp.minimum(jnp.sum(pad_end[None, :] <= tile_start[:, None], axis=1), N_EXPERTS - 1).astype(jnp.int32)
    n_used = (pad_end[-1] // tm).astype(jnp.int32).reshape(1)
    tile_valid = jnp.clip(pad_start[tile_expert] + counts[tile_expert] - tile_start, 0, tm).astype(jnp.int32)
    return dest, tile_expert, n_used, tile_valid


def kernel(x, c, ctx, c_ctx, w_mod, b_mod, norm1_w, norm2_w, w_in, conv_w, conv_b, dt_bias, a_log, d_skip,
           ssd_norm_w, w_ssd_out, w_four_out, w_o, w_router, b_router, w_gate_up, b_gate_up, w_down, b_down,
           final_norm_w):
    bsz, n, d = x.shape
    n_ctx = ctx.shape[1]
    depth = w_mod.shape[0]
    assert depth == 1, "a stacked model would also need the context stream's residual update"
    d_inner = N_HEADS * HEAD_DIM
    off_dt = d_inner + 2 * BC_WIDTH
    m = bsz * n
    x2d = x.reshape(m, d)
    tm = min(512, n)

    for layer in range(depth):
        rows = -(-(bsz + 1) // 8) * 8
        c_rows = jnp.zeros((rows, d), F32).at[:bsz].set(c).at[bsz].set(c_ctx)
        mod = _modulation(c_rows, w_mod[layer], b_mod[layer])
        mods = [mod[:bsz, i * d:(i + 1) * d].reshape(bsz, 1, d) for i in range(N_MOD)]
        sh1, sc1, g1, sh2, sc2, g2 = mods
        sh1_c = mod[bsz:bsz + 1, :d].reshape(1, 1, d)
        sc1_c = mod[bsz:bsz + 1, d:2 * d].reshape(1, 1, d)

        wl = w_in[layer]
        w_main = jnp.concatenate([wl[:, :off_dt], wl[:, off_dt + 2 * N_HEADS:]], axis=1).astype(BF16)
        w_dt = jnp.zeros((d, 2 * LANES), F32)
        w_dt = w_dt.at[:, :N_HEADS].set(wl[:, off_dt:off_dt + N_HEADS])
        w_dt = w_dt.at[:, LANES:LANES + N_HEADS].set(wl[:, off_dt + N_HEADS:off_dt + 2 * N_HEADS]).astype(BF16)
        pad = jnp.zeros((2, 1, LANES - N_HEADS), F32)
        dt_bias2 = jnp.concatenate([dt_bias[layer].reshape(2, 1, N_HEADS), pad], axis=-1)
        a_log2 = jnp.concatenate([a_log[layer].reshape(2, 1, N_HEADS), pad], axis=-1)

        w_ctx = w_main[:, :off_dt]
        pc, dtc = _in_proj(ctx.reshape(bsz * n_ctx, d), norm1_w[layer], sc1_c, sh1_c, w_ctx, w_dt,
                           bsz * n_ctx, min(512, n_ctx))
        xbc_c = _conv_silu(pc.reshape(bsz, n_ctx, off_dt), conv_w[layer][:, :off_dt], conv_b[layer][:off_dt], off_dt)
        h0 = jnp.zeros((bsz, 2, N_GROUPS, D_STATE, HEADS_PER_GROUP * HEAD_DIM), F32)
        h_ctx = _ssd(xbc_c, dtc.reshape(bsz, n_ctx, 2 * LANES), dt_bias2, a_log2, h0, with_y=False)

        p, dtl = _in_proj(x2d, norm1_w[layer], sc1, sh1, w_main, w_dt, n, tm)
        n_conv = off_dt + 2 * BC_WIDTH
        xbc = _conv_silu(p.reshape(bsz, n, -1), conv_w[layer], conv_b[layer], n_conv)
        y2 = _ssd(xbc, dtl.reshape(bsz, n, 2 * LANES), dt_bias2, a_log2, h_ctx, with_y=True)
        ff = _fnet_mix(p, (n_conv + d_inner) // d, bsz, n)

        dsk = jnp.repeat(d_skip[layer].astype(F32), HEAD_DIM).reshape(1, d_inner)
        wr = jnp.zeros((d, LANES), F32).at[:, :N_EXPERTS].set(w_router[layer])
        br = jnp.full((1, LANES), NEG_BIG, F32).at[0, :N_EXPERTS].set(b_router[layer])
        x1, h2, ti, tw, rk, cnt = _tail(
            y2.reshape(2, m, d_inner), xbc.reshape(m, n_conv), p, ff, x2d, g1, sc2, sh2, dsk,
            ssd_norm_w[layer].reshape(1, d_inner), w_ssd_out[layer].astype(BF16), w_four_out[layer].astype(BF16),
            w_o[layer].astype(BF16), norm2_w[layer].reshape(1, d), wr, br, n, tm)

        counts = cnt[0, :N_EXPERTS].astype(jnp.int32)
        dest, tile_expert, n_used, tile_valid = _dispatch_plan(ti[:, :TOP_K], rk[:, :TOP_K], counts, MOE_TILE)
        dest_km = dest.T.reshape(-1)
        xs = _scatter_rows(h2, dest_km, tile_expert.shape[0] * MOE_TILE)
        ys = _experts(xs, tile_expert, n_used, tile_valid, w_gate_up[layer],
                      b_gate_up[layer].reshape(N_EXPERTS, 1, -1), w_down[layer],
                      b_down[layer].reshape(N_EXPERTS, 1, -1))
        y4 = _gather_rows(ys, dest_km).reshape(TOP_K, m, d // 2)
        x2d = _final(x1, y4, tw, g2, final_norm_w, n, tm)
    return x2d.reshape(bsz, n, d)
```

```python
import functools
import math

import numpy as np
import jax
import jax.numpy as jnp
from jax import lax
from jax.experimental import pallas as pl
from jax.experimental.pallas import tpu as pltpu
from jax.experimental.pallas import tpu_sc as plsc

F32 = jnp.float32
BF16 = jnp.bfloat16

EPS = 1e-6
GRID_W = 64
N_MOD = 6
F_GROUPS = 8
F_GROUP_DIM = 128
HEAD_DIM = 64
N_HEADS = 32
N_GROUPS = 4
HEADS_PER_GROUP = N_HEADS // N_GROUPS
D_STATE = 128
BC_WIDTH = N_GROUPS * D_STATE
CONV_K = 5
CHUNK = 128
N_EXPERTS = 32
TOP_K = 4
SWIGLU_LIMIT = 7.0
SWIGLU_ALPHA = 1.702

LANES = 128
VMEM_LIMIT_BYTES = 56 * 1024 * 1024
NEG_BIG = -1e30
MOE_TILE = 512
SC_GATHER_WINDOW = 64


def _cparams(*sem):
    return pltpu.CompilerParams(dimension_semantics=sem, vmem_limit_bytes=VMEM_LIMIT_BYTES)


def _sigmoid(v):
    return 1.0 / (1.0 + jnp.exp(-v))


def _silu(v):
    return v * _sigmoid(v)


def _softplus(v):
    return jnp.maximum(v, 0.0) + jnp.log(1.0 + jnp.exp(-jnp.abs(v)))


def _pack_bf16_pair(v):
    w = v.shape[1] // 2
    bits = lax.bitcast_convert_type(v.astype(BF16).astype(F32), jnp.uint32)
    return bits[:, :w] | (bits[:, w:] >> 16)


def _unpack_bf16_pair(p):
    hi = lax.bitcast_convert_type(p & jnp.uint32(0xFFFF0000), F32)
    lo = lax.bitcast_convert_type(p << 16, F32)
    return hi, lo


def _mod_kernel(c_ref, w_ref, b_ref, o_ref):
    s = _silu(c_ref[...]).astype(BF16)
    o_ref[...] = jnp.dot(s, w_ref[...].astype(BF16), preferred_element_type=F32) + b_ref[...]


def _modulation(c_rows, w_mod, b_mod):
    rows, d = c_rows.shape
    n_out = w_mod.shape[1]
    tn = 1024
    return pl.pallas_call(
        _mod_kernel,
        grid=(n_out // tn,),
        in_specs=[pl.BlockSpec((rows, d), lambda j: (0, 0)),
                  pl.BlockSpec((d, tn), lambda j: (0, j)),
                  pl.BlockSpec((1, tn), lambda j: (0, j))],
        out_specs=pl.BlockSpec((rows, tn), lambda j: (0, j)),
        out_shape=jax.ShapeDtypeStruct((rows, n_out), F32),
        compiler_params=_cparams("arbitrary"),
        name="modulation",
    )(c_rows, w_mod, b_mod.reshape(1, n_out))


_INPROJ_TN = 1024


def _inproj_kernel(x_ref, nw_ref, sc_ref, sh_ref, w_ref, wdt_ref, p_ref, dt_ref):
    x = x_ref[...]
    inv = lax.rsqrt(jnp.mean(x * x, axis=-1, keepdims=True) + EPS)
    h = (x * inv * nw_ref[...]) * (1.0 + sc_ref[...]) + sh_ref[...]
    hb = h.astype(BF16)
    dt_ref[...] = jnp.dot(hb, wdt_ref[...], preferred_element_type=F32)
    for j in range(w_ref.shape[1] // _INPROJ_TN):
        cols = slice(j * _INPROJ_TN, (j + 1) * _INPROJ_TN)
        p_ref[:, cols] = jnp.dot(hb, w_ref[:, cols], preferred_element_type=F32).astype(p_ref.dtype)


def _in_proj(x2d, norm_w, sc, sh, w_main, w_dt, rows_per_mod, tm):
    m, d = x2d.shape
    n = w_main.shape[1]
    ndt = w_dt.shape[1]
    resident = dict(pipeline_mode=pl.Buffered(1))
    return pl.pallas_call(
        _inproj_kernel,
        grid=(m // tm,),
        in_specs=[pl.BlockSpec((tm, d), lambda i: (i, 0)),
                  pl.BlockSpec((1, d), lambda i: (0, 0), **resident),
                  pl.BlockSpec((None, 1, d), lambda i: ((i * tm) // rows_per_mod, 0, 0)),
                  pl.BlockSpec((None, 1, d), lambda i: ((i * tm) // rows_per_mod, 0, 0)),
                  pl.BlockSpec((d, n), lambda i: (0, 0), **resident),
                  pl.BlockSpec((d, ndt), lambda i: (0, 0), **resident)],
        out_specs=[pl.BlockSpec((tm, n), lambda i: (i, 0)),
                   pl.BlockSpec((tm, ndt), lambda i: (i, 0))],
        out_shape=[jax.ShapeDtypeStruct((m, n), BF16),
                   jax.ShapeDtypeStruct((m, ndt), F32)],
        compiler_params=_cparams("arbitrary"),
        name="in_proj",
    )(x2d, norm_w.reshape(1, d), sc, sh, w_main, w_dt)


_CONV_HALO = 16


def _conv_kernel(p_ref, w_ref, b_ref, o_ref, *, n, rc):
    tc = o_ref.shape[-1]
    w = w_ref[...]
    bias = b_ref[...]
    zeros = jnp.zeros((_CONV_HALO, tc), F32)
    tot = rc + 2 * _CONV_HALO
    for r0 in range(0, n, rc):
        top = zeros if r0 == 0 else p_ref[r0 - _CONV_HALO:r0, :].astype(F32)
        bot = zeros if r0 + rc >= n else p_ref[r0 + rc:r0 + rc + _CONV_HALO, :].astype(F32)
        blk = jnp.concatenate([top, p_ref[r0:r0 + rc, :].astype(F32), bot], axis=0)
        acc = jnp.broadcast_to(bias, (rc, tc))
        for k in range(CONV_K):
            shift = (CONV_K // 2 - k) % tot
            rolled = blk if shift == 0 else pltpu.roll(blk, shift, axis=0)
            acc = acc + w[k:k + 1, :] * rolled[_CONV_HALO:_CONV_HALO + rc, :]
        o_ref[r0:r0 + rc, :] = _silu(acc).astype(o_ref.dtype)


def _conv_silu(p3d, conv_w, conv_b, n_ch):
    bsz, n, _ = p3d.shape
    tc = 512
    rc = min(256, n)
    return pl.pallas_call(
        functools.partial(_conv_kernel, n=n, rc=rc),
        grid=(bsz, n_ch // tc),
        in_specs=[pl.BlockSpec((None, n, tc), lambda b, j: (b, 0, j)),
                  pl.BlockSpec((CONV_K, tc), lambda b, j: (0, j)),
                  pl.BlockSpec((1, tc), lambda b, j: (0, j))],
        out_specs=pl.BlockSpec((None, n, tc), lambda b, j: (b, 0, j)),
        out_shape=jax.ShapeDtypeStruct((bsz, n, n_ch), BF16),
        compiler_params=_cparams("arbitrary", "arbitrary"),
        name="conv_silu",
    )(p3d, conv_w, conv_b.reshape(1, -1))


LOG2E = 1.4426950408889634


def _head_expand_matrix():
    e = np.zeros((LANES, N_HEADS * HEAD_DIM), np.float32)
    for h in range(N_HEADS):
        e[h, h * HEAD_DIM:(h + 1) * HEAD_DIM] = 1.0
    return jnp.asarray(e, BF16)


def _ssd_chunk_terms(dt_raw, dt_bias, a_log, expand, tri, fwd):
    q = CHUNK
    dt = _softplus(dt_raw + dt_bias)
    da = dt * (-jnp.exp(a_log))
    hi = da.astype(BF16)
    r1 = da - hi.astype(F32)
    mid = r1.astype(BF16)
    lo = (r1 - mid.astype(F32)).astype(BF16)
    ones_tri = jnp.where(tri, 1.0, 0.0).astype(BF16)
    cs3 = jnp.dot(ones_tri, jnp.concatenate([hi, mid, lo], axis=1), preferred_element_type=F32)
    cs = cs3[:, :LANES] + cs3[:, LANES:2 * LANES] + cs3[:, 2 * LANES:]
    tot = jnp.where(fwd, cs[q - 1:q, :], cs[0:1, :])
    dte = dt * jnp.exp(tot - cs)
    dec = jnp.exp(tot)
    dec_hi = dec.astype(BF16)
    dec_lo = (dec - dec_hi.astype(F32)).astype(BF16)
    dec2 = jnp.concatenate([dec_hi, dec_lo, jnp.zeros((6, LANES), BF16)], axis=0)
    stacked = jnp.concatenate([jnp.exp(cs).astype(BF16), dte.astype(BF16), dec2], axis=0)
    spread = jnp.dot(stacked, expand, preferred_element_type=F32)
    ecs_e = spread[:q]
    dte_e = spread[q:2 * q].astype(BF16)
    dec_e = jnp.broadcast_to(spread[2 * q:2 * q + 1] + spread[2 * q + 1:2 * q + 2], (8, spread.shape[1]))
    cs2 = cs * LOG2E
    return cs2, cs2.T, dt.T, ecs_e, dte_e, dec_e


def _ssd_kernel(*refs, with_y, nc):
    if with_y:
        (x_ref, b_ref, c_ref, dt_ref, dtn_ref, dtb_ref, alog_ref, exp_ref, h0_ref, y_ref,
         s_scr, cs_scr, dee_scr, dce_scr, ecs_scr) = refs
    else:
        (x_ref, b_ref, dt_ref, dtn_ref, dtb_ref, alog_ref, exp_ref, h0_ref, hout_ref,
         s_scr, cs_scr, dee_scr, dce_scr) = refs
    q = CHUNK
    gw = HEADS_PER_GROUP * HEAD_DIM
    d = pl.program_id(1)
    c = pl.program_id(2)
    row = lax.broadcasted_iota(jnp.int32, (q, q), 0)
    col = lax.broadcasted_iota(jnp.int32, (q, q), 1)
    fwd = d == 0
    tri = jnp.where(fwd, row - col, col - row) >= 0
    first_half = col < HEAD_DIM

    def store_terms(slot, terms):
        cs2, cs2_t, dt_t, ecs_e, dte_e, dec_e = terms
        cs_scr[slot, 0] = cs2
        cs_scr[slot, 1] = cs2_t
        cs_scr[slot, 2] = dt_t
        dee_scr[slot] = dte_e
        dce_scr[slot] = dec_e
        if with_y:
            ecs_scr[slot] = ecs_e

    @pl.when(c == 0)
    def _():
        s_scr[...] = h0_ref[...]
        store_terms(0, _ssd_chunk_terms(dt_ref[...], dtb_ref[...], alog_ref[...], exp_ref[...], tri, fwd))

    slot = c % 2
    cs2 = cs_scr[slot, 0]
    cs2_t = cs_scr[slot, 1]
    dt_t = cs_scr[slot, 2]
    store_terms(1 - slot, _ssd_chunk_terms(dtn_ref[...], dtb_ref[...], alog_ref[...], exp_ref[...], tri, fwd))

    for g in range(N_GROUPS):
        cols = slice(g * gw, (g + 1) * gw)
        bg = b_ref[:, g * D_STATE:(g + 1) * D_STATE]
        xg = x_ref[:, cols]
        s_old = s_scr[g]
        if with_y:
            cg = c_ref[:, g * D_STATE:(g + 1) * D_STATE]
            cb = lax.dot_general(cg, bg, (((1,), (1,)), ((), ())), preferred_element_type=F32)
            y_off = jnp.dot(cg, s_old.astype(BF16), preferred_element_type=F32)
            for k in range(HEADS_PER_GROUP // 2):
                pair = g * (HEADS_PER_GROUP // 2) + k
                lanes = slice(pair * LANES, (pair + 1) * LANES)
                xp = xg[:, k * LANES:(k + 1) * LANES]
                xz = jnp.zeros_like(xp)
                lhs = []
                for hh in (2 * pair, 2 * pair + 1):
                    colb = jnp.broadcast_to(cs2[:, hh:hh + 1], (q, q))
                    rowb = cs2_t[hh:hh + 1, :]
                    decay = jnp.exp2(jnp.where(tri, colb - rowb, -jnp.inf))
                    lhs.append((cb * decay * dt_t[hh:hh + 1, :]).astype(BF16))
                y_diag = jnp.dot(jnp.concatenate(lhs, axis=1),
                                 jnp.concatenate([jnp.where(first_half, xp, xz), jnp.where(first_half, xz, xp)], axis=0),
                                 preferred_element_type=F32)
                y = y_diag + ecs_scr[slot, :, lanes] * y_off[:, k * LANES:(k + 1) * LANES]
                y_ref[:, lanes] = y.astype(y_ref.dtype)
        xdte = xg * dee_scr[slot, :, cols]
        upd = lax.dot_general(bg, xdte, (((0,), (0,)), ((), ())), preferred_element_type=F32)
        s_scr[g] = s_old * dce_scr[slot, 0:1, cols] + upd

    if not with_y:
        @pl.when(c == nc - 1)
        def _():
            hout_ref[...] = s_scr[...]


def _ssd(xbc, dt_raw, dt_bias2, a_log2, h0, with_y):
    bsz, n, _ = xbc.shape
    nc = n // CHUNK
    d_inner = N_HEADS * HEAD_DIM
    gw = HEADS_PER_GROUP * HEAD_DIM
    x_blk = d_inner // BC_WIDTH

    def ceff(dd, cc):
        return cc + dd * (nc - 1 - 2 * cc)

    def cnext(dd, cc):
        return ceff(dd, jnp.minimum(cc + 1, nc - 1))

    in_specs = [pl.BlockSpec((None, CHUNK, d_inner), lambda b, dd, cc: (b, ceff(dd, cc), 0)),
                pl.BlockSpec((None, CHUNK, BC_WIDTH), lambda b, dd, cc: (b, ceff(dd, cc), x_blk + dd))]
    args = [xbc, xbc]
    if with_y:
        in_specs.append(pl.BlockSpec((None, CHUNK, BC_WIDTH), lambda b, dd, cc: (b, ceff(dd, cc), x_blk + 2 + dd)))
        args.append(xbc)
    in_specs += [pl.BlockSpec((None, CHUNK, LANES), lambda b, dd, cc: (b, ceff(dd, cc), dd)),
                 pl.BlockSpec((None, CHUNK, LANES), lambda b, dd, cc: (b, cnext(dd, cc), dd)),
                 pl.BlockSpec((None, 1, LANES), lambda b, dd, cc: (dd, 0, 0)),
                 pl.BlockSpec((None, 1, LANES), lambda b, dd, cc: (dd, 0, 0)),
                 pl.BlockSpec((LANES, d_inner), lambda b, dd, cc: (0, 0)),
                 pl.BlockSpec((None, None, N_GROUPS, D_STATE, gw), lambda b, dd, cc: (b, dd, 0, 0, 0))]
    args += [dt_raw, dt_raw, dt_bias2, a_log2, _head_expand_matrix(), h0]
    scratch = [pltpu.VMEM((N_GROUPS, D_STATE, gw), F32),
               pltpu.VMEM((2, 3, CHUNK, LANES), F32),
               pltpu.VMEM((2, CHUNK, d_inner), BF16),
               pltpu.VMEM((2, 8, d_inner), F32)]
    if with_y:
        out_specs = pl.BlockSpec((None, None, CHUNK, d_inner), lambda b, dd, cc: (dd, b, ceff(dd, cc), 0))
        out_shape = jax.ShapeDtypeStruct((2, bsz, n, d_inner), BF16)
        scratch += [pltpu.VMEM((2, CHUNK, d_inner), F32)]
    else:
        out_specs = pl.BlockSpec((None, None, N_GROUPS, D_STATE, gw), lambda b, dd, cc: (b, dd, 0, 0, 0))
        out_shape = jax.ShapeDtypeStruct((bsz, 2, N_GROUPS, D_STATE, gw), F32)
    return pl.pallas_call(
        functools.partial(_ssd_kernel, with_y=with_y, nc=nc),
        grid=(bsz, 2, nc),
        in_specs=in_specs,
        out_specs=out_specs,
        out_shape=out_shape,
        scratch_shapes=scratch,
        compiler_params=_cparams("arbitrary", "arbitrary", "arbitrary"),
        name="ssd_scan" if with_y else "ssd_ctx_state",
    )(*args)


def _dft_mats(n):
    k = np.arange(n)
    ang = 2.0 * np.pi * ((k[:, None] * k[None, :]) % n) / n
    return np.cos(ang), np.sin(ang)


def _fnet_chan_kernel(u_ref, m_ref, o_ref):
    m = m_ref[...]
    for g in range(F_GROUPS):
        ug = u_ref[:, g * F_GROUP_DIM:(g + 1) * F_GROUP_DIM]
        pq = jnp.dot(ug, m, preferred_element_type=F32)
        o_ref[0, :, g * F_GROUP_DIM:(g + 1) * F_GROUP_DIM] = pq[:, :F_GROUP_DIM].astype(o_ref.dtype)
        o_ref[1, :, g * F_GROUP_DIM:(g + 1) * F_GROUP_DIM] = pq[:, F_GROUP_DIM:].astype(o_ref.dtype)


def _stage_lane_blocks(src_ref, scr, rows):
    width = src_ref.shape[-1]
    for t in range(2):
        v = src_ref[t].reshape(rows, width).astype(F32)
        for lb in range(width // LANES):
            scr[t, lb] = v[:, lb * LANES:(lb + 1) * LANES]


def _strided_rows(scr, t, start, count, stride, width):
    return jnp.concatenate([scr[t, lb, pl.ds(start, count, stride=stride), :] for lb in range(width // LANES)],
                           axis=1)


def _fnet_stage1_kernel(z_ref, m1_ref, twc_ref, tws_ref, o_ref, zs_scr, *, tcs):
    r = z_ref.shape[1]
    width = z_ref.shape[-1]
    _stage_lane_blocks(z_ref, zs_scr, r * tcs)
    m1 = m1_ref[...]
    for j in range(tcs):
        zz = jnp.concatenate([_strided_rows(zs_scr, 0, j, r, tcs, width),
                              _strided_rows(zs_scr, 1, j, r, tcs, width)], axis=0).astype(BF16)
        u = jnp.dot(m1, zz, preferred_element_type=F32)
        ur = u[:r]
        ui = u[r:]
        tc = twc_ref[j]
        ts = tws_ref[j]
        o_ref[0, j] = (ur * tc + ui * ts).astype(o_ref.dtype)
        o_ref[1, j] = (ui * tc - ur * ts).astype(o_ref.dtype)


def _fnet_stage2_kernel(u_ref, m2_ref, o_ref, us_scr, os_scr, *, tbs):
    c = u_ref.shape[1]
    width = u_ref.shape[-1]
    _stage_lane_blocks(u_ref, us_scr, c * tbs)
    m2 = m2_ref[...]
    for j in range(tbs):
        uu = jnp.concatenate([_strided_rows(us_scr, 0, j, c, tbs, width),
                              _strided_rows(us_scr, 1, j, c, tbs, width)], axis=0).astype(BF16)
        x = jnp.dot(m2, uu, preferred_element_type=F32)
        for lb in range(width // LANES):
            os_scr[lb, pl.ds(j, c, stride=tbs), :] = x[:, lb * LANES:(lb + 1) * LANES]
    out = jnp.concatenate([os_scr[lb] for lb in range(width // LANES)], axis=1)
    o_ref[...] = out.reshape(c, tbs, width).astype(o_ref.dtype)


def _fnet_mix(p2d, col_blk, bsz, n):
    m = p2d.shape[0]
    width = F_GROUPS * F_GROUP_DIM
    cgrid = GRID_W
    rgrid = n // cgrid
    scale = 1.0 / math.sqrt(n * F_GROUP_DIM)

    cc, sc = _dft_mats(F_GROUP_DIM)
    chan = jnp.asarray(np.concatenate([cc, sc], axis=1) * scale, BF16)
    tm = 512
    pq = pl.pallas_call(
        _fnet_chan_kernel,
        grid=(m // tm,),
        in_specs=[pl.BlockSpec((tm, width), lambda i: (i, col_blk)),
                  pl.BlockSpec((F_GROUP_DIM, 2 * F_GROUP_DIM), lambda i: (0, 0))],
        out_specs=pl.BlockSpec((2, tm, width), lambda i: (0, i, 0)),
        out_shape=jax.ShapeDtypeStruct((2, m, width), BF16),
        compiler_params=_cparams("arbitrary"),
        name="fnet_chan",
    )(p2d, chan)

    cr, sr = _dft_mats(rgrid)
    m1 = jnp.asarray(np.block([[cr, -sr], [-sr, -cr]]), BF16)
    bb = np.arange(rgrid)[None, :]
    ci = np.arange(cgrid)[:, None]
    ang = 2.0 * np.pi * ((ci * bb) % n) / n
    twc = jnp.asarray(np.cos(ang)[:, :, None], F32)
    tws = jnp.asarray(np.sin(ang)[:, :, None], F32)
    tcs = 16
    z5 = pq.reshape(2, bsz, rgrid, cgrid, width)
    u5 = pl.pallas_call(
        functools.partial(_fnet_stage1_kernel, tcs=tcs),
        grid=(bsz, cgrid // tcs),
        in_specs=[pl.BlockSpec((2, None, rgrid, tcs, width), lambda b, j: (0, b, 0, j, 0)),
                  pl.BlockSpec((2 * rgrid, 2 * rgrid), lambda b, j: (0, 0)),
                  pl.BlockSpec((tcs, rgrid, 1), lambda b, j: (j, 0, 0)),
                  pl.BlockSpec((tcs, rgrid, 1), lambda b, j: (j, 0, 0))],
        out_specs=pl.BlockSpec((2, None, tcs, rgrid, width), lambda b, j: (0, b, j, 0, 0)),
        out_shape=jax.ShapeDtypeStruct((2, bsz, cgrid, rgrid, width), BF16),
        scratch_shapes=[pltpu.VMEM((2, width // LANES, rgrid * tcs, LANES), F32)],
        compiler_params=_cparams("arbitrary", "arbitrary"),
        name="fnet_stage1",
    )(z5, m1, twc, tws)

    cc2, sc2 = _dft_mats(cgrid)
    m2 = jnp.asarray(np.concatenate([cc2, sc2], axis=1), BF16)
    tbs = min(16, rgrid)
    out = pl.pallas_call(
        functools.partial(_fnet_stage2_kernel, tbs=tbs),
        grid=(bsz, rgrid // tbs),
        in_specs=[pl.BlockSpec((2, None, cgrid, tbs, width), lambda b, j: (0, b, 0, j, 0)),
                  pl.BlockSpec((cgrid, 2 * cgrid), lambda b, j: (0, 0))],
        out_specs=pl.BlockSpec((None, cgrid, tbs, width), lambda b, j: (b, 0, j, 0)),
        out_shape=jax.ShapeDtypeStruct((bsz, cgrid, rgrid, width), BF16),
        scratch_shapes=[pltpu.VMEM((2, width // LANES, cgrid * tbs, LANES), F32),
                        pltpu.VMEM((width // LANES, cgrid * tbs, LANES), F32)],
        compiler_params=_cparams("arbitrary", "arbitrary"),
        name="fnet_stage2",
    )(u5, m2)
    return out.reshape(m, width)


def _tail_kernel(yf_ref, yb_ref, xs_ref, z_ref, ff_ref, gf_ref, gs_ref, x_ref, g1_ref, sc2_ref, sh2_ref,
                 dsk_ref, snw_ref, wso_ref, wfo_ref, wo_ref, n2w_ref, wr_ref, br_ref,
                 x1_ref, h2_ref, ti_ref, tw_ref, rk_ref, cnt_ref, carry_scr):
    y = (yf_ref[0] + yb_ref[0]).astype(F32) + dsk_ref[...] * xs_ref[...].astype(F32)
    g = y * _silu(z_ref[...]).astype(F32)
    inv = lax.rsqrt(jnp.mean(g * g, axis=-1, keepdims=True) + EPS)
    gn = (g * inv * snw_ref[...]).astype(BF16)
    y_ssd = jnp.dot(gn, wso_ref[...], preferred_element_type=F32)
    y_four = jnp.dot(ff_ref[...], wfo_ref[...], preferred_element_type=F32)
    t = _sigmoid(gf_ref[...]) * y_four.astype(BF16) + _sigmoid(gs_ref[...]) * y_ssd.astype(BF16)
    mix = jnp.dot(t, wo_ref[...], preferred_element_type=F32)
    x1 = x_ref[...] + g1_ref[...] * mix
    x1_ref[...] = x1
    inv2 = lax.rsqrt(jnp.mean(x1 * x1, axis=-1, keepdims=True) + EPS)
    h2 = (x1 * inv2 * n2w_ref[...]) * (1.0 + sc2_ref[...]) + sh2_ref[...]
    h2_ref[...] = _pack_bf16_pair(h2)
    h_hi = h2.astype(BF16)
    h_lo = (h2 - h_hi.astype(F32)).astype(BF16)
    wr = wr_ref[...]
    w_hi = wr.astype(BF16)
    w_lo = (wr - w_hi.astype(F32)).astype(BF16)
    logits = (jnp.dot(h_hi, w_hi, preferred_element_type=F32)
              + jnp.dot(h_hi, w_lo, preferred_element_type=F32)
              + jnp.dot(h_lo, w_hi, preferred_element_type=F32)) + br_ref[...]
    tm = logits.shape[0]
    lane = lax.broadcasted_iota(jnp.int32, (tm, LANES), 1)
    vals, idxs = [], []
    cur = logits
    for _ in range(TOP_K):
        mx = jnp.max(cur, axis=-1, keepdims=True)
        ix = jnp.argmax(cur, axis=-1, keepdims=True).astype(jnp.int32)
        vals.append(mx)
        idxs.append(ix)
        cur = jnp.where(lane == ix, NEG_BIG * 2.0, cur)
    es = [jnp.exp(v - vals[0]) for v in vals]
    den = es[0] + es[1] + es[2] + es[3]
    ti = jnp.zeros((tm, LANES), jnp.int32)
    tw = jnp.zeros((tm, LANES), F32)
    for k in range(TOP_K):
        ti = jnp.where(lane == k, idxs[k], ti)
        tw = jnp.where(lane == k, es[k] / den, tw)
    ti_ref[...] = ti
    tw_ref[...] = tw

    @pl.when(pl.program_id(0) == 0)
    def _():
        carry_scr[...] = jnp.zeros(carry_scr.shape, F32)

    onehots = [jnp.where(lane == ix, 1.0, 0.0) for ix in idxs]
    cnt = onehots[0] + onehots[1] + onehots[2] + onehots[3]
    r_i = lax.broadcasted_iota(jnp.int32, (tm, tm), 0)
    c_i = lax.broadcasted_iota(jnp.int32, (tm, tm), 1)
    earlier = jnp.where(r_i > c_i, 1.0, 0.0).astype(BF16)
    before = jnp.dot(earlier, cnt.astype(BF16), preferred_element_type=F32) + carry_scr[0:1, :]
    rk = jnp.zeros((tm, LANES), jnp.int32)
    for k in range(TOP_K):
        rank_k = jnp.sum(onehots[k] * before, axis=-1, keepdims=True)
        rk = jnp.where(lane == k, rank_k.astype(jnp.int32), rk)
    rk_ref[...] = rk
    total = carry_scr[...] + jnp.sum(cnt, axis=0, keepdims=True)
    carry_scr[...] = total
    cnt_ref[...] = total


def _tail(y2, xbc2d, p2d, ff, x2d, g1, sc2, sh2, dsk, snw, wso, wfo, wo, n2w, wr, br, rows_per_mod, tm):
    m, d = x2d.shape
    di = y2.shape[-1]
    zb = (2 * di) // di
    fb = (3 * di) // d
    row = lambda i: (i, 0)
    modrow = lambda i: ((i * tm) // rows_per_mod, 0, 0)
    const = lambda i: (0, 0)
    in_specs = [pl.BlockSpec((1, tm, di), lambda i: (0, i, 0)),
                pl.BlockSpec((1, tm, di), lambda i: (1, i, 0)),
                pl.BlockSpec((tm, di), row),
                pl.BlockSpec((tm, di), lambda i: (i, zb)),
                pl.BlockSpec((tm, d), row),
                pl.BlockSpec((tm, d), lambda i: (i, fb + 1)),
                pl.BlockSpec((tm, d), lambda i: (i, fb + 2)),
                pl.BlockSpec((tm, d), row),
                pl.BlockSpec((None, 1, d), modrow),
                pl.BlockSpec((None, 1, d), modrow),
                pl.BlockSpec((None, 1, d), modrow),
                pl.BlockSpec((1, di), const),
                pl.BlockSpec((1, di), const),
                pl.BlockSpec((di, d), const),
                pl.BlockSpec((d, d), const),
                pl.BlockSpec((d, d), const),
                pl.BlockSpec((1, d), const),
                pl.BlockSpec((d, LANES), const),
                pl.BlockSpec((1, LANES), const)]
    out_specs = [pl.BlockSpec((tm, d), row), pl.BlockSpec((tm, d // 2), row),
                 pl.BlockSpec((tm, LANES), row), pl.BlockSpec((tm, LANES), row), pl.BlockSpec((tm, LANES), row),
                 pl.BlockSpec((8, LANES), const)]
    out_shape = [jax.ShapeDtypeStruct((m, d), F32), jax.ShapeDtypeStruct((m, d // 2), jnp.uint32),
                 jax.ShapeDtypeStruct((m, LANES), jnp.int32), jax.ShapeDtypeStruct((m, LANES), F32),
                 jax.ShapeDtypeStruct((m, LANES), jnp.int32), jax.ShapeDtypeStruct((8, LANES), F32)]
    return pl.pallas_call(
        _tail_kernel,
        grid=(m // tm,),
        in_specs=in_specs,
        out_specs=out_specs,
        out_shape=out_shape,
        scratch_shapes=[pltpu.VMEM((8, LANES), F32)],
        compiler_params=_cparams("arbitrary"),
        name="tail",
    )(y2, y2, xbc2d, p2d, ff, p2d, p2d, x2d, g1, sc2, sh2, dsk, snw, wso, wfo, wo, n2w, wr, br)


def _gather_rows(src, idx):
    n_out = idx.shape[0]
    width = src.shape[1]
    win = SC_GATHER_WINDOW
    info = plsc.get_sparse_core_info()
    n_cores, n_workers = info.num_cores, info.num_cores * info.num_subcores
    per_worker = n_out // n_workers
    assert per_worker * n_workers == n_out and per_worker % (2 * win) == 0
    mesh = plsc.VectorSubcoreMesh(core_axis_name="core", subcore_axis_name="subcore")

    @functools.partial(
        pl.kernel, out_type=jax.ShapeDtypeStruct((n_out, width), src.dtype), mesh=mesh,
        scratch_types=[pltpu.VMEM((win,), jnp.int32), pltpu.VMEM((win,), jnp.int32),
                       pltpu.VMEM((win, width), src.dtype), pltpu.VMEM((win, width), src.dtype),
                       pltpu.SemaphoreType.DMA, pltpu.SemaphoreType.DMA,
                       pltpu.SemaphoreType.DMA, pltpu.SemaphoreType.DMA],
        name="gather_rows")
    def gather(src_hbm, idx_hbm, out_hbm, idx_a, idx_b, rows_a, rows_b, gsem_a, gsem_b, ssem_a, ssem_b):
        worker = lax.axis_index("subcore") * n_cores + lax.axis_index("core")
        base = worker * per_worker

        @pl.loop(0, per_worker, step=2 * win)
        def _(off):
            pltpu.sync_copy(idx_hbm.at[pl.ds(base + off, win)], idx_a)
            ga = pltpu.async_copy(src_hbm.at[idx_a], rows_a, gsem_a)
            pltpu.sync_copy(idx_hbm.at[pl.ds(base + off + win, win)], idx_b)
            gb = pltpu.async_copy(src_hbm.at[idx_b], rows_b, gsem_b)
            ga.wait()
            sa = pltpu.async_copy(rows_a, out_hbm.at[pl.ds(base + off, win)], ssem_a)
            gb.wait()
            sb = pltpu.async_copy(rows_b, out_hbm.at[pl.ds(base + off + win, win)], ssem_b)
            sa.wait()
            sb.wait()

    return gather(src, idx)


def _scatter_rows(src, dest, n_rows):
    m, width = src.shape
    win = SC_GATHER_WINDOW
    info = plsc.get_sparse_core_info()
    n_cores, n_workers = info.num_cores, info.num_cores * info.num_subcores
    per_worker = m // n_workers
    assert per_worker * n_workers == m and per_worker % win == 0
    mesh = plsc.VectorSubcoreMesh(core_axis_name="core", subcore_axis_name="subcore")

    @functools.partial(
        pl.kernel, out_type=jax.ShapeDtypeStruct((n_rows, width), src.dtype), mesh=mesh,
        scratch_types=[pltpu.VMEM((win, width), src.dtype)]
        + [pltpu.VMEM((win,), jnp.int32)] * TOP_K + [pltpu.SemaphoreType.DMA] * TOP_K,
        name="scatter_rows")
    def scatter(src_hbm, dest_hbm, out_hbm, rows_v, *rest):
        idx_v, sems = rest[:TOP_K], rest[TOP_K:]
        worker = lax.axis_index("subcore") * n_cores + lax.axis_index("core")
        base = worker * per_worker

        @pl.loop(0, per_worker, step=win)
        def _(off):
            t0 = base + off
            pltpu.sync_copy(src_hbm.at[pl.ds(t0, win)], rows_v)
            for k in range(TOP_K):
                pltpu.sync_copy(dest_hbm.at[pl.ds(k * m + t0, win)], idx_v[k])
            copies = [pltpu.async_copy(rows_v, out_hbm.at[idx_v[k]], sems[k]) for k in range(TOP_K)]
            for cp in copies:
                cp.wait()

    return scatter(src, dest)


def _expert_kernel(te_ref, nu_ref, tv_ref, x_ref, wgu_ref, bgu_ref, wd_ref, bd_ref, o_ref, wgu_scr, wd_scr):
    i = pl.program_id(0)
    used = i < nu_ref[0]
    new_expert = (i == 0) | (te_ref[i] != te_ref[jnp.maximum(i - 1, 0)])

    @pl.when(used & new_expert)
    def _():
        wgu_scr[...] = wgu_ref[...].astype(BF16)
        wd_scr[...] = wd_ref[...].astype(BF16)

    @pl.when(used)
    def _():
        dff = wd_ref.shape[0]
        half = wgu_ref.shape[0] // 2
        rows = lax.broadcasted_iota(jnp.int32, x_ref.shape, 0)
        xa, xb = _unpack_bf16_pair(jnp.where(rows < tv_ref[i], x_ref[...], jnp.uint32(0)))
        gu = (jnp.dot(xa.astype(BF16), wgu_scr[:half, :], preferred_element_type=F32)
              + jnp.dot(xb.astype(BF16), wgu_scr[half:, :], preferred_element_type=F32)) + bgu_ref[...]
        gate = jnp.minimum(gu[:, :dff], SWIGLU_LIMIT)
        up = jnp.clip(gu[:, dff:], -SWIGLU_LIMIT, SWIGLU_LIMIT)
        act = (up + 1.0) * gate * _sigmoid(SWIGLU_ALPHA * gate)
        y = jnp.dot(act.astype(BF16), wd_scr[...], preferred_element_type=F32) + bd_ref[...]
        o_ref[...] = _pack_bf16_pair(y)

    @pl.when(jnp.logical_not(used))
    def _():
        o_ref[...] = jnp.zeros(o_ref.shape, o_ref.dtype)


def _experts(xs, tile_expert, n_used, tile_valid, wgu, bgu, wd, bd):
    rows, dh = xs.shape
    d = 2 * dh
    tm = MOE_TILE
    n_tiles = rows // tm
    dff2 = wgu.shape[-1]

    def tile(i, te, nu, tv):
        return (jnp.minimum(i, nu[0] - 1), 0)

    def wsel(i, te, nu, tv):
        return (te[jnp.minimum(i, nu[0] - 1)], 0, 0)

    grid_spec = pltpu.PrefetchScalarGridSpec(
        num_scalar_prefetch=3,
        grid=(n_tiles,),
        in_specs=[pl.BlockSpec((tm, dh), tile),
                  pl.BlockSpec((None, d, dff2), wsel),
                  pl.BlockSpec((None, 1, dff2), wsel),
                  pl.BlockSpec((None, dff2 // 2, d), wsel),
                  pl.BlockSpec((None, 1, d), wsel)],
        out_specs=pl.BlockSpec((tm, dh), lambda i, te, nu, tv: (i, 0)),
        scratch_shapes=[pltpu.VMEM((d, dff2), BF16), pltpu.VMEM((dff2 // 2, d), BF16)],
    )
    return pl.pallas_call(
        _expert_kernel,
        grid_spec=grid_spec,
        out_shape=jax.ShapeDtypeStruct((rows, dh), jnp.uint32),
        compiler_params=_cparams("arbitrary"),
        name="experts",
    )(tile_expert, n_used, tile_valid, xs, wgu, bgu, wd, bd)


def _final_kernel(x1_ref, ya_ref, yb_ref, yc_ref, yd_ref, tw_ref, g2_ref, fw_ref, o_ref):
    d = x1_ref.shape[-1]
    half = d // 2
    tw = tw_ref[...]
    acc_hi = jnp.zeros((x1_ref.shape[0], half), F32)
    acc_lo = jnp.zeros((x1_ref.shape[0], half), F32)
    for k, y_ref in enumerate((ya_ref, yb_ref, yc_ref, yd_ref)):
        y_hi, y_lo = _unpack_bf16_pair(y_ref[...])
        acc_hi = acc_hi + tw[:, k:k + 1] * y_hi
        acc_lo = acc_lo + tw[:, k:k + 1] * y_lo
    x_hi = x1_ref[:, :half] + g2_ref[:, :half] * acc_hi
    x_lo = x1_ref[:, half:] + g2_ref[:, half:] * acc_lo
    ms = (jnp.sum(x_hi * x_hi, axis=-1, keepdims=True) + jnp.sum(x_lo * x_lo, axis=-1, keepdims=True)) / d
    inv = lax.rsqrt(ms + EPS)
    o_ref[:, :half] = x_hi * inv * fw_ref[:, :half]
    o_ref[:, half:] = x_lo * inv * fw_ref[:, half:]


def _final(x1, y4, tw, g2, fw, rows_per_mod, tm):
    m, d = x1.shape
    return pl.pallas_call(
        _final_kernel,
        grid=(m // tm,),
        in_specs=[pl.BlockSpec((tm, d), lambda i: (i, 0))]
        + [pl.BlockSpec((None, tm, d // 2), functools.partial(lambda k, i: (k, i, 0), k)) for k in range(TOP_K)]
        + [pl.BlockSpec((tm, LANES), lambda i: (i, 0)),
           pl.BlockSpec((None, 1, d), lambda i: ((i * tm) // rows_per_mod, 0, 0)),
           pl.BlockSpec((1, d), lambda i: (0, 0))],
        out_specs=pl.BlockSpec((tm, d), lambda i: (i, 0)),
        out_shape=jax.ShapeDtypeStruct((m, d), F32),
        compiler_params=_cparams("arbitrary"),
        name="final",
    )(x1, y4, y4, y4, y4, tw, g2, fw.reshape(1, d))


def _dispatch_plan(e, rank, counts, tm):
    n_assign = e.size
    padded = (counts + tm - 1) // tm * tm
    pad_end = jnp.cumsum(padded)
    pad_start = pad_end - padded
    dest = (pad_start[e] + rank).astype(jnp.int32)
    n_tiles = n_assign // tm + N_EXPERTS
    tile_start = jnp.arange(n_tiles, dtype=jnp.int32) * tm
    tile_expert = jnp.minimum(jnp.sum(pad_end[None, :] <= tile_start[:, None], axis=1), N_EXPERTS - 1).astype(jnp.int32)
    n_used = (pad_end[-1] // tm).astype(jnp.int32).reshape(1)
    tile_valid = jnp.clip(pad_start[tile_expert] + counts[tile_expert] - tile_start, 0, tm).astype(jnp.int32)
    return dest, tile_expert, n_used, tile_valid


def kernel(x, c, ctx, c_ctx, w_mod, b_mod, norm1_w, norm2_w, w_in, conv_w, conv_b, dt_bias, a_log, d_skip,
           ssd_norm_w, w_ssd_out, w_four_out, w_o, w_router, b_router, w_gate_up, b_gate_up, w_down, b_down,
           final_norm_w):
    bsz, n, d = x.shape
    n_ctx = ctx.shape[1]
    depth = w_mod.shape[0]
    assert depth == 1, "a stacked model would also need the context stream's residual update"
    d_inner = N_HEADS * HEAD_DIM
    off_dt = d_inner + 2 * BC_WIDTH
    m = bsz * n
    x2d = x.reshape(m, d)
    tm = min(512, n)

    for layer in range(depth):
        rows = -(-(bsz + 1) // 8) * 8
        c_rows = jnp.zeros((rows, d), F32).at[:bsz].set(c).at[bsz].set(c_ctx)
        mod = _modulation(c_rows, w_mod[layer], b_mod[layer])
        mods = [mod[:bsz, i * d:(i + 1) * d].reshape(bsz, 1, d) for i in range(N_MOD)]
        sh1, sc1, g1, sh2, sc2, g2 = mods
        sh1_c = mod[bsz:bsz + 1, :d].reshape(1, 1, d)
        sc1_c = mod[bsz:bsz + 1, d:2 * d].reshape(1, 1, d)

        wl = w_in[layer]
        w_main = jnp.concatenate([wl[:, :off_dt], wl[:, off_dt + 2 * N_HEADS:]], axis=1).astype(BF16)
        w_dt = jnp.zeros((d, 2 * LANES), F32)
        w_dt = w_dt.at[:, :N_HEADS].set(wl[:, off_dt:off_dt + N_HEADS])
        w_dt = w_dt.at[:, LANES:LANES + N_HEADS].set(wl[:, off_dt + N_HEADS:off_dt + 2 * N_HEADS]).astype(BF16)
        pad = jnp.zeros((2, 1, LANES - N_HEADS), F32)
        dt_bias2 = jnp.concatenate([dt_bias[layer].reshape(2, 1, N_HEADS), pad], axis=-1)
        a_log2 = jnp.concatenate([a_log[layer].reshape(2, 1, N_HEADS), pad], axis=-1)

        w_ctx = w_main[:, :off_dt]
        pc, dtc = _in_proj(ctx.reshape(bsz * n_ctx, d), norm1_w[layer], sc1_c, sh1_c, w_ctx, w_dt,
                           bsz * n_ctx, min(512, n_ctx))
        xbc_c = _conv_silu(pc.reshape(bsz, n_ctx, off_dt), conv_w[layer][:, :off_dt], conv_b[layer][:off_dt], off_dt)
        h0 = jnp.zeros((bsz, 2, N_GROUPS, D_STATE, HEADS_PER_GROUP * HEAD_DIM), F32)
        h_ctx = _ssd(xbc_c, dtc.reshape(bsz, n_ctx, 2 * LANES), dt_bias2, a_log2, h0, with_y=False)

        p, dtl = _in_proj(x2d, norm1_w[layer], sc1, sh1, w_main, w_dt, n, tm)
        n_conv = off_dt + 2 * BC_WIDTH
        xbc = _conv_silu(p.reshape(bsz, n, -1), conv_w[layer], conv_b[layer], n_conv)
        y2 = _ssd(xbc, dtl.reshape(bsz, n, 2 * LANES), dt_bias2, a_log2, h_ctx, with_y=True)
        ff = _fnet_mix(p, (n_conv + d_inner) // d, bsz, n)

        dsk = jnp.repeat(d_skip[layer].astype(F32), HEAD_DIM).reshape(1, d_inner)
        wr = jnp.zeros((d, LANES), F32).at[:, :N_EXPERTS].set(w_router[layer])
        br = jnp.full((1, LANES), NEG_BIG, F32).at[0, :N_EXPERTS].set(b_router[layer])
        x1, h2, ti, tw, rk, cnt = _tail(
            y2.reshape(2, m, d_inner), xbc.reshape(m, n_conv), p, ff, x2d, g1, sc2, sh2, dsk,
            ssd_norm_w[layer].reshape(1, d_inner), w_ssd_out[layer].astype(BF16), w_four_out[layer].astype(BF16),
            w_o[layer].astype(BF16), norm2_w[layer].reshape(1, d), wr, br, n, tm)

        counts = cnt[0, :N_EXPERTS].astype(jnp.int32)
        dest, tile_expert, n_used, tile_valid = _dispatch_plan(ti[:, :TOP_K], rk[:, :TOP_K], counts, MOE_TILE)
        dest_km = dest.T.reshape(-1)
        xs = _scatter_rows(h2, dest_km, tile_expert.shape[0] * MOE_TILE)
        ys = _experts(xs, tile_expert, n_used, tile_valid, w_gate_up[layer],
                      b_gate_up[layer].reshape(N_EXPERTS, 1, -1), w_down[layer],
                      b_down[layer].reshape(N_EXPERTS, 1, -1))
        y4 = _gather_rows(ys, dest_km).reshape(TOP_K, m, d // 2)
        x2d = _final(x1, y4, tw, g2, final_norm_w, n, tm)
    return x2d.reshape(bsz, n, d)
```

```python
import functools
import math

import numpy as np
import jax
import jax.numpy as jnp
from jax import lax
from jax.experimental import pallas as pl
from jax.experimental.pallas import tpu as pltpu
from jax.experimental.pallas import tpu_sc as plsc

F32 = jnp.float32
BF16 = jnp.bfloat16

EPS = 1e-6
GRID_W = 64
N_MOD = 6
F_GROUPS = 8
F_GROUP_DIM = 128
HEAD_DIM = 64
N_HEADS = 32
N_GROUPS = 4
HEADS_PER_GROUP = N_HEADS // N_GROUPS
D_STATE = 128
BC_WIDTH = N_GROUPS * D_STATE
CONV_K = 5
CHUNK = 128
N_EXPERTS = 32
TOP_K = 4
SWIGLU_LIMIT = 7.0
SWIGLU_ALPHA = 1.702

LANES = 128
VMEM_LIMIT_BYTES = 56 * 1024 * 1024
NEG_BIG = -1e30
MOE_TILE = 512
SC_GATHER_WINDOW = 64


def _cparams(*sem):
    return pltpu.CompilerParams(dimension_semantics=sem, vmem_limit_bytes=VMEM_LIMIT_BYTES)


def _sigmoid(v):
    return 1.0 / (1.0 + jnp.exp(-v))


def _silu(v):
    return v * _sigmoid(v)


def _softplus(v):
    return jnp.maximum(v, 0.0) + jnp.log(1.0 + jnp.exp(-jnp.abs(v)))


def _pack_bf16_pair(v):
    w = v.shape[1] // 2
    bits = lax.bitcast_convert_type(v.astype(BF16).astype(F32), jnp.uint32)
    return bits[:, :w] | (bits[:, w:] >> 16)


def _unpack_bf16_pair(p):
    hi = lax.bitcast_convert_type(p & jnp.uint32(0xFFFF0000), F32)
    lo = lax.bitcast_convert_type(p << 16, F32)
    return hi, lo


def _mod_kernel(c_ref, w_ref, b_ref, o_ref):
    s = _silu(c_ref[...]).astype(BF16)
    o_ref[...] = jnp.dot(s, w_ref[...].astype(BF16), preferred_element_type=F32) + b_ref[...]


def _modulation(c_rows, w_mod, b_mod):
    rows, d = c_rows.shape
    n_out = w_mod.shape[1]
    tn = 1024
    return pl.pallas_call(
        _mod_kernel,
        grid=(n_out // tn,),
        in_specs=[pl.BlockSpec((rows, d), lambda j: (0, 0)),
                  pl.BlockSpec((d, tn), lambda j: (0, j)),
                  pl.BlockSpec((1, tn), lambda j: (0, j))],
        out_specs=pl.BlockSpec((rows, tn), lambda j: (0, j)),
        out_shape=jax.ShapeDtypeStruct((rows, n_out), F32),
        compiler_params=_cparams("arbitrary"),
        name="modulation",
    )(c_rows, w_mod, b_mod.reshape(1, n_out))


_INPROJ_TN = 1024


def _inproj_kernel(x_ref, nw_ref, sc_ref, sh_ref, w_ref, wdt_ref, p_ref, dt_ref):
    x = x_ref[...]
    inv = lax.rsqrt(jnp.mean(x * x, axis=-1, keepdims=True) + EPS)
    h = (x * inv * nw_ref[...]) * (1.0 + sc_ref[...]) + sh_ref[...]
    hb = h.astype(BF16)
    dt_ref[...] = jnp.dot(hb, wdt_ref[...], preferred_element_type=F32)
    for j in range(w_ref.shape[1] // _INPROJ_TN):
        cols = slice(j * _INPROJ_TN, (j + 1) * _INPROJ_TN)
        p_ref[:, cols] = jnp.dot(hb, w_ref[:, cols], preferred_element_type=F32).astype(p_ref.dtype)


def _in_proj(x2d, norm_w, sc, sh, w_main, w_dt, rows_per_mod, tm):
    m, d = x2d.shape
    n = w_main.shape[1]
    ndt = w_dt.shape[1]
    resident = dict(pipeline_mode=pl.Buffered(1))
    return pl.pallas_call(
        _inproj_kernel,
        grid=(m // tm,),
        in_specs=[pl.BlockSpec((tm, d), lambda i: (i, 0)),
                  pl.BlockSpec((1, d), lambda i: (0, 0), **resident),
                  pl.BlockSpec((None, 1, d), lambda i: ((i * tm) // rows_per_mod, 0, 0)),
                  pl.BlockSpec((None, 1, d), lambda i: ((i * tm) // rows_per_mod, 0, 0)),
                  pl.BlockSpec((d, n), lambda i: (0, 0), **resident),
                  pl.BlockSpec((d, ndt), lambda i: (0, 0), **resident)],
        out_specs=[pl.BlockSpec((tm, n), lambda i: (i, 0)),
                   pl.BlockSpec((tm, ndt), lambda i: (i, 0))],
        out_shape=[jax.ShapeDtypeStruct((m, n), BF16),
                   jax.ShapeDtypeStruct((m, ndt), F32)],
        compiler_params=_cparams("arbitrary"),
        name="in_proj",
    )(x2d, norm_w.reshape(1, d), sc, sh, w_main, w_dt)


_CONV_HALO = 16


_CONV_ROWS = 64


def _conv_shift_matrix(rc):
    win = rc + 2 * _CONV_HALO
    s = np.zeros((rc, CONV_K * win), np.float32)
    for k in range(CONV_K):
        for l in range(rc):
            s[l, k * win + _CONV_HALO + l + k - CONV_K // 2] = 1.0
    return jnp.asarray(s, BF16)


def _conv_kernel(p_ref, w_ref, b_ref, s_ref, o_ref, *, n, rc):
    tc = o_ref.shape[-1]
    w = w_ref[...].astype(BF16)
    bias = b_ref[...]
    smat = s_ref[...]
    zeros = jnp.zeros((_CONV_HALO, tc), BF16)
    for r0 in range(0, n, rc):
        top = zeros if r0 == 0 else p_ref[r0 - _CONV_HALO:r0, :]
        bot = zeros if r0 + rc >= n else p_ref[r0 + rc:r0 + rc + _CONV_HALO, :]
        window = jnp.concatenate([top, p_ref[r0:r0 + rc, :], bot], axis=0)
        taps = jnp.concatenate([window * w[k:k + 1, :] for k in range(CONV_K)], axis=0)
        acc = jnp.dot(smat, taps, preferred_element_type=F32) + bias
        o_ref[r0:r0 + rc, :] = _silu(acc).astype(o_ref.dtype)


def _conv_silu(p3d, conv_w, conv_b, n_ch):
    bsz, n, _ = p3d.shape
    tc = 512
    rc = min(_CONV_ROWS, n)
    smat = _conv_shift_matrix(rc)
    return pl.pallas_call(
        functools.partial(_conv_kernel, n=n, rc=rc),
        grid=(bsz, n_ch // tc),
        in_specs=[pl.BlockSpec((None, n, tc), lambda b, j: (b, 0, j)),
                  pl.BlockSpec((CONV_K, tc), lambda b, j: (0, j)),
                  pl.BlockSpec((1, tc), lambda b, j: (0, j)),
                  pl.BlockSpec(smat.shape, lambda b, j: (0, 0))],
        out_specs=pl.BlockSpec((None, n, tc), lambda b, j: (b, 0, j)),
        out_shape=jax.ShapeDtypeStruct((bsz, n, n_ch), BF16),
        compiler_params=_cparams("arbitrary", "arbitrary"),
        name="conv_silu",
    )(p3d, conv_w, conv_b.reshape(1, -1), smat)


LOG2E = 1.4426950408889634


def _head_expand_matrix():
    e = np.zeros((LANES, N_HEADS * HEAD_DIM), np.float32)
    for h in range(N_HEADS):
        e[h, h * HEAD_DIM:(h + 1) * HEAD_DIM] = 1.0
    return jnp.asarray(e, BF16)


def _ssd_chunk_terms(dt_raw, dt_bias, a_log, expand, tri, fwd):
    q = CHUNK
    dt = _softplus(dt_raw + dt_bias)
    da = dt * (-jnp.exp(a_log))
    hi = da.astype(BF16)
    r1 = da - hi.astype(F32)
    mid = r1.astype(BF16)
    lo = (r1 - mid.astype(F32)).astype(BF16)
    ones_tri = jnp.where(tri, 1.0, 0.0).astype(BF16)
    cs3 = jnp.dot(ones_tri, jnp.concatenate([hi, mid, lo], axis=1), preferred_element_type=F32)
    cs = cs3[:, :LANES] + cs3[:, LANES:2 * LANES] + cs3[:, 2 * LANES:]
    tot = jnp.where(fwd, cs[q - 1:q, :], cs[0:1, :])
    dte = dt * jnp.exp(tot - cs)
    dec = jnp.exp(tot)
    dec_hi = dec.astype(BF16)
    dec_lo = (dec - dec_hi.astype(F32)).astype(BF16)
    dec2 = jnp.concatenate([dec_hi, dec_lo, jnp.zeros((6, LANES), BF16)], axis=0)
    stacked = jnp.concatenate([jnp.exp(cs).astype(BF16), dte.astype(BF16), dec2], axis=0)
    spread = jnp.dot(stacked, expand, preferred_element_type=F32)
    ecs_e = spread[:q]
    dte_e = spread[q:2 * q].astype(BF16)
    dec_e = jnp.broadcast_to(spread[2 * q:2 * q + 1] + spread[2 * q + 1:2 * q + 2], (8, spread.shape[1]))
    cs2 = cs * LOG2E
    return cs2, cs2.T, dt.T, ecs_e, dte_e, dec_e


def _ssd_kernel(*refs, with_y, nc):
    if with_y:
        (x_ref, b_ref, c_ref, dt_ref, dtn_ref, dtb_ref, alog_ref, exp_ref, h0_ref, y_ref,
         s_scr, cs_scr, dee_scr, dce_scr, ecs_scr) = refs
    else:
        (x_ref, b_ref, dt_ref, dtn_ref, dtb_ref, alog_ref, exp_ref, h0_ref, hout_ref,
         s_scr, cs_scr, dee_scr, dce_scr) = refs
    q = CHUNK
    gw = HEADS_PER_GROUP * HEAD_DIM
    d = pl.program_id(1)
    c = pl.program_id(2)
    row = lax.broadcasted_iota(jnp.int32, (q, q), 0)
    col = lax.broadcasted_iota(jnp.int32, (q, q), 1)
    fwd = d == 0
    tri = jnp.where(fwd, row - col, col - row) >= 0
    first_half = col < HEAD_DIM

    def store_terms(slot, terms):
        cs2, cs2_t, dt_t, ecs_e, dte_e, dec_e = terms
        cs_scr[slot, 0] = cs2
        cs_scr[slot, 1] = cs2_t
        cs_scr[slot, 2] = dt_t
        dee_scr[slot] = dte_e
        dce_scr[slot] = dec_e
        if with_y:
            ecs_scr[slot] = ecs_e

    @pl.when(c == 0)
    def _():
        s_scr[...] = h0_ref[...]
        store_terms(0, _ssd_chunk_terms(dt_ref[...], dtb_ref[...], alog_ref[...], exp_ref[...], tri, fwd))

    slot = c % 2
    cs2 = cs_scr[slot, 0]
    cs2_t = cs_scr[slot, 1]
    dt_t = cs_scr[slot, 2]
    store_terms(1 - slot, _ssd_chunk_terms(dtn_ref[...], dtb_ref[...], alog_ref[...], exp_ref[...], tri, fwd))

    for g in range(N_GROUPS):
        cols = slice(g * gw, (g + 1) * gw)
        bg = b_ref[:, g * D_STATE:(g + 1) * D_STATE]
        xg = x_ref[:, cols]
        s_old = s_scr[g]
        if with_y:
            cg = c_ref[:, g * D_STATE:(g + 1) * D_STATE]
            cb = lax.dot_general(cg, bg, (((1,), (1,)), ((), ())), preferred_element_type=F32)
            y_off = jnp.dot(cg, s_old.astype(BF16), preferred_element_type=F32)
            for k in range(HEADS_PER_GROUP // 2):
                pair = g * (HEADS_PER_GROUP // 2) + k
                lanes = slice(pair * LANES, (pair + 1) * LANES)
                xp = xg[:, k * LANES:(k + 1) * LANES]
                xz = jnp.zeros_like(xp)
                lhs = []
                for hh in (2 * pair, 2 * pair + 1):
                    colb = jnp.broadcast_to(cs2[:, hh:hh + 1], (q, q))
                    rowb = cs2_t[hh:hh + 1, :]
                    decay = jnp.exp2(jnp.where(tri, colb - rowb, -jnp.inf))
                    lhs.append((cb * decay * dt_t[hh:hh + 1, :]).astype(BF16))
                y_diag = jnp.dot(jnp.concatenate(lhs, axis=1),
                                 jnp.concatenate([jnp.where(first_half, xp, xz), jnp.where(first_half, xz, xp)], axis=0),
                                 preferred_element_type=F32)
                y = y_diag + ecs_scr[slot, :, lanes] * y_off[:, k * LANES:(k + 1) * LANES]
                y_ref[:, lanes] = y.astype(y_ref.dtype)
        xdte = xg * dee_scr[slot, :, cols]
        upd = lax.dot_general(bg, xdte, (((0,), (0,)), ((), ())), preferred_element_type=F32)
        s_scr[g] = s_old * dce_scr[slot, 0:1, cols] + upd

    if not with_y:
        @pl.when(c == nc - 1)
        def _():
            hout_ref[...] = s_scr[...]


def _ssd(xbc, dt_raw, dt_bias2, a_log2, h0, with_y):
    bsz, n, _ = xbc.shape
    nc = n // CHUNK
    d_inner = N_HEADS * HEAD_DIM
    gw = HEADS_PER_GROUP * HEAD_DIM
    x_blk = d_inner // BC_WIDTH

    def ceff(dd, cc):
        return cc + dd * (nc - 1 - 2 * cc)

    def cnext(dd, cc):
        return ceff(dd, jnp.minimum(cc + 1, nc - 1))

    in_specs = [pl.BlockSpec((None, CHUNK, d_inner), lambda b, dd, cc: (b, ceff(dd, cc), 0)),
                pl.BlockSpec((None, CHUNK, BC_WIDTH), lambda b, dd, cc: (b, ceff(dd, cc), x_blk + dd))]
    args = [xbc, xbc]
    if with_y:
        in_specs.append(pl.BlockSpec((None, CHUNK, BC_WIDTH), lambda b, dd, cc: (b, ceff(dd, cc), x_blk + 2 + dd)))
        args.append(xbc)
    in_specs += [pl.BlockSpec((None, CHUNK, LANES), lambda b, dd, cc: (b, ceff(dd, cc), dd)),
                 pl.BlockSpec((None, CHUNK, LANES), lambda b, dd, cc: (b, cnext(dd, cc), dd)),
                 pl.BlockSpec((None, 1, LANES), lambda b, dd, cc: (dd, 0, 0)),
                 pl.BlockSpec((None, 1, LANES), lambda b, dd, cc: (dd, 0, 0)),
                 pl.BlockSpec((LANES, d_inner), lambda b, dd, cc: (0, 0)),
                 pl.BlockSpec((None, None, N_GROUPS, D_STATE, gw), lambda b, dd, cc: (b, dd, 0, 0, 0))]
    args += [dt_raw, dt_raw, dt_bias2, a_log2, _head_expand_matrix(), h0]
    scratch = [pltpu.VMEM((N_GROUPS, D_STATE, gw), F32),
               pltpu.VMEM((2, 3, CHUNK, LANES), F32),
               pltpu.VMEM((2, CHUNK, d_inner), BF16),
               pltpu.VMEM((2, 8, d_inner), F32)]
    if with_y:
        out_specs = pl.BlockSpec((None, None, CHUNK, d_inner), lambda b, dd, cc: (dd, b, ceff(dd, cc), 0))
        out_shape = jax.ShapeDtypeStruct((2, bsz, n, d_inner), BF16)
        scratch += [pltpu.VMEM((2, CHUNK, d_inner), F32)]
    else:
        out_specs = pl.BlockSpec((None, None, N_GROUPS, D_STATE, gw), lambda b, dd, cc: (b, dd, 0, 0, 0))
        out_shape = jax.ShapeDtypeStruct((bsz, 2, N_GROUPS, D_STATE, gw), F32)
    return pl.pallas_call(
        functools.partial(_ssd_kernel, with_y=with_y, nc=nc),
        grid=(bsz, 2, nc),
        in_specs=in_specs,
        out_specs=out_specs,
        out_shape=out_shape,
        scratch_shapes=scratch,
        compiler_params=_cparams("arbitrary", "arbitrary", "arbitrary"),
        name="ssd_scan" if with_y else "ssd_ctx_state",
    )(*args)


def _dft_mats(n):
    k = np.arange(n)
    ang = 2.0 * np.pi * ((k[:, None] * k[None, :]) % n) / n
    return np.cos(ang), np.sin(ang)


def _fnet_chan_kernel(u_ref, m_ref, o_ref):
    m = m_ref[...]
    for g in range(F_GROUPS):
        ug = u_ref[:, g * F_GROUP_DIM:(g + 1) * F_GROUP_DIM]
        pq = jnp.dot(ug, m, preferred_element_type=F32)
        o_ref[0, :, g * F_GROUP_DIM:(g + 1) * F_GROUP_DIM] = pq[:, :F_GROUP_DIM].astype(o_ref.dtype)
        o_ref[1, :, g * F_GROUP_DIM:(g + 1) * F_GROUP_DIM] = pq[:, F_GROUP_DIM:].astype(o_ref.dtype)


def _stage_lane_blocks(src_ref, scr, rows):
    width = src_ref.shape[-1]
    for t in range(2):
        v = src_ref[t].reshape(rows, width).astype(F32)
        for lb in range(width // LANES):
            scr[t, lb] = v[:, lb * LANES:(lb + 1) * LANES]


def _strided_rows(scr, t, start, count, stride, width):
    return jnp.concatenate([scr[t, lb, pl.ds(start, count, stride=stride), :] for lb in range(width // LANES)],
                           axis=1)


def _fnet_stage1_kernel(z_ref, m1_ref, twc_ref, tws_ref, o_ref, zs_scr, *, tcs):
    r = z_ref.shape[1]
    width = z_ref.shape[-1]
    _stage_lane_blocks(z_ref, zs_scr, r * tcs)
    m1 = m1_ref[...]
    for j in range(tcs):
        zz = jnp.concatenate([_strided_rows(zs_scr, 0, j, r, tcs, width),
                              _strided_rows(zs_scr, 1, j, r, tcs, width)], axis=0).astype(BF16)
        u = jnp.dot(m1, zz, preferred_element_type=F32)
        ur = u[:r]
        ui = u[r:]
        tc = twc_ref[j]
        ts = tws_ref[j]
        o_ref[0, j] = (ur * tc + ui * ts).astype(o_ref.dtype)
        o_ref[1, j] = (ui * tc - ur * ts).astype(o_ref.dtype)


def _fnet_stage2_kernel(u_ref, m2_ref, o_ref, us_scr, os_scr, *, tbs):
    c = u_ref.shape[1]
    width = u_ref.shape[-1]
    _stage_lane_blocks(u_ref, us_scr, c * tbs)
    m2 = m2_ref[...]
    for j in range(tbs):
        uu = jnp.concatenate([_strided_rows(us_scr, 0, j, c, tbs, width),
                              _strided_rows(us_scr, 1, j, c, tbs, width)], axis=0).astype(BF16)
        x = jnp.dot(m2, uu, preferred_element_type=F32)
        for lb in range(width // LANES):
            os_scr[lb, pl.ds(j, c, stride=tbs), :] = x[:, lb * LANES:(lb + 1) * LANES]
    out = jnp.concatenate([os_scr[lb] for lb in range(width // LANES)], axis=1)
    o_ref[...] = out.reshape(c, tbs, width).astype(o_ref.dtype)


def _fnet_mix(p2d, col_blk, bsz, n):
    m = p2d.shape[0]
    width = F_GROUPS * F_GROUP_DIM
    cgrid = GRID_W
    rgrid = n // cgrid
    scale = 1.0 / math.sqrt(n * F_GROUP_DIM)

    cc, sc = _dft_mats(F_GROUP_DIM)
    chan = jnp.asarray(np.concatenate([cc, sc], axis=1) * scale, BF16)
    tm = 512
    pq = pl.pallas_call(
        _fnet_chan_kernel,
        grid=(m // tm,),
        in_specs=[pl.BlockSpec((tm, width), lambda i: (i, col_blk)),
                  pl.BlockSpec((F_GROUP_DIM, 2 * F_GROUP_DIM), lambda i: (0, 0))],
        out_specs=pl.BlockSpec((2, tm, width), lambda i: (0, i, 0)),
        out_shape=jax.ShapeDtypeStruct((2, m, width), BF16),
        compiler_params=_cparams("arbitrary"),
        name="fnet_chan",
    )(p2d, chan)

    cr, sr = _dft_mats(rgrid)
    m1 = jnp.asarray(np.block([[cr, -sr], [-sr, -cr]]), BF16)
    bb = np.arange(rgrid)[None, :]
    ci = np.arange(cgrid)[:, None]
    ang = 2.0 * np.pi * ((ci * bb) % n) / n
    twc = jnp.asarray(np.cos(ang)[:, :, None], F32)
    tws = jnp.asarray(np.sin(ang)[:, :, None], F32)
    tcs = 16
    z5 = pq.reshape(2, bsz, rgrid, cgrid, width)
    u5 = pl.pallas_call(
        functools.partial(_fnet_stage1_kernel, tcs=tcs),
        grid=(bsz, cgrid // tcs),
        in_specs=[pl.BlockSpec((2, None, rgrid, tcs, width), lambda b, j: (0, b, 0, j, 0)),
                  pl.BlockSpec((2 * rgrid, 2 * rgrid), lambda b, j: (0, 0)),
                  pl.BlockSpec((tcs, rgrid, 1), lambda b, j: (j, 0, 0)),
                  pl.BlockSpec((tcs, rgrid, 1), lambda b, j: (j, 0, 0))],
        out_specs=pl.BlockSpec((2, None, tcs, rgrid, width), lambda b, j: (0, b, j, 0, 0)),
        out_shape=jax.ShapeDtypeStruct((2, bsz, cgrid, rgrid, width), BF16),
        scratch_shapes=[pltpu.VMEM((2, width // LANES, rgrid * tcs, LANES), F32)],
        compiler_params=_cparams("arbitrary", "arbitrary"),
        name="fnet_stage1",
    )(z5, m1, twc, tws)

    cc2, sc2 = _dft_mats(cgrid)
    m2 = jnp.asarray(np.concatenate([cc2, sc2], axis=1), BF16)
    tbs = min(16, rgrid)
    out = pl.pallas_call(
        functools.partial(_fnet_stage2_kernel, tbs=tbs),
        grid=(bsz, rgrid // tbs),
        in_specs=[pl.BlockSpec((2, None, cgrid, tbs, width), lambda b, j: (0, b, 0, j, 0)),
                  pl.BlockSpec((cgrid, 2 * cgrid), lambda b, j: (0, 0))],
        out_specs=pl.BlockSpec((None, cgrid, tbs, width), lambda b, j: (b, 0, j, 0)),
        out_shape=jax.ShapeDtypeStruct((bsz, cgrid, rgrid, width), BF16),
        scratch_shapes=[pltpu.VMEM((2, width // LANES, cgrid * tbs, LANES), F32),
                        pltpu.VMEM((width // LANES, cgrid * tbs, LANES), F32)],
        compiler_params=_cparams("arbitrary", "arbitrary"),
        name="fnet_stage2",
    )(u5, m2)
    return out.reshape(m, width)


def _tail_kernel(yf_ref, yb_ref, xs_ref, z_ref, ff_ref, gf_ref, gs_ref, x_ref, g1_ref, sc2_ref, sh2_ref,
                 dsk_ref, snw_ref, wso_ref, wfo_ref, wo_ref, n2w_ref, wr_ref, br_ref,
                 x1_ref, h2_ref, ti_ref, tw_ref, rk_ref, cnt_ref, carry_scr):
    y = (yf_ref[0] + yb_ref[0]).astype(F32) + dsk_ref[...] * xs_ref[...].astype(F32)
    g = y * _silu(z_ref[...]).astype(F32)
    inv = lax.rsqrt(jnp.mean(g * g, axis=-1, keepdims=True) + EPS)
    gn = (g * inv * snw_ref[...]).astype(BF16)
    y_ssd = jnp.dot(gn, wso_ref[...], preferred_element_type=F32)
    y_four = jnp.dot(ff_ref[...], wfo_ref[...], preferred_element_type=F32)
    t = _sigmoid(gf_ref[...]) * y_four.astype(BF16) + _sigmoid(gs_ref[...]) * y_ssd.astype(BF16)
    mix = jnp.dot(t, wo_ref[...], preferred_element_type=F32)
    x1 = x_ref[...] + g1_ref[...] * mix
    x1_ref[...] = x1
    inv2 = lax.rsqrt(jnp.mean(x1 * x1, axis=-1, keepdims=True) + EPS)
    h2 = (x1 * inv2 * n2w_ref[...]) * (1.0 + sc2_ref[...]) + sh2_ref[...]
    h2_ref[...] = _pack_bf16_pair(h2)
    h_hi = h2.astype(BF16)
    h_lo = (h2 - h_hi.astype(F32)).astype(BF16)
    wr = wr_ref[...]
    w_hi = wr.astype(BF16)
    w_lo = (wr - w_hi.astype(F32)).astype(BF16)
    logits = (jnp.dot(h_hi, w_hi, preferred_element_type=F32)
              + jnp.dot(h_hi, w_lo, preferred_element_type=F32)
              + jnp.dot(h_lo, w_hi, preferred_element_type=F32)) + br_ref[...]
    tm = logits.shape[0]
    lane = lax.broadcasted_iota(jnp.int32, (tm, LANES), 1)
    vals, idxs = [], []
    cur = logits
    for _ in range(TOP_K):
        mx = jnp.max(cur, axis=-1, keepdims=True)
        ix = jnp.min(jnp.where(cur == mx, lane, LANES), axis=-1, keepdims=True)
        vals.append(mx)
        idxs.append(ix)
        cur = jnp.where(lane == ix, NEG_BIG * 2.0, cur)
    es = [jnp.exp(v - vals[0]) for v in vals]
    den = es[0] + es[1] + es[2] + es[3]
    ti = jnp.zeros((tm, LANES), jnp.int32)
    tw = jnp.zeros((tm, LANES), F32)
    for k in range(TOP_K):
        ti = jnp.where(lane == k, idxs[k], ti)
        tw = jnp.where(lane == k, es[k] / den, tw)
    ti_ref[...] = ti
    tw_ref[...] = tw

    @pl.when(pl.program_id(0) == 0)
    def _():
        carry_scr[...] = jnp.zeros(carry_scr.shape, F32)

    onehots = [jnp.where(lane == ix, 1.0, 0.0) for ix in idxs]
    cnt = onehots[0] + onehots[1] + onehots[2] + onehots[3]
    r_i = lax.broadcasted_iota(jnp.int32, (tm, tm), 0)
    c_i = lax.broadcasted_iota(jnp.int32, (tm, tm), 1)
    earlier = jnp.where(r_i > c_i, 1.0, 0.0).astype(BF16)
    before = jnp.dot(earlier, cnt.astype(BF16), preferred_element_type=F32) + carry_scr[0:1, :]
    rk = jnp.zeros((tm, LANES), jnp.int32)
    for k in range(TOP_K):
        rank_k = jnp.sum(onehots[k] * before, axis=-1, keepdims=True)
        rk = jnp.where(lane == k, rank_k.astype(jnp.int32), rk)
    rk_ref[...] = rk
    total = carry_scr[...] + jnp.sum(cnt, axis=0, keepdims=True)
    carry_scr[...] = total
    cnt_ref[...] = total


def _tail(y2, xbc2d, p2d, ff, x2d, g1, sc2, sh2, dsk, snw, wso, wfo, wo, n2w, wr, br, rows_per_mod, tm):
    m, d = x2d.shape
    di = y2.shape[-1]
    zb = (2 * di) // di
    fb = (3 * di) // d
    row = lambda i: (i, 0)
    modrow = lambda i: ((i * tm) // rows_per_mod, 0, 0)
    const = lambda i: (0, 0)
    in_specs = [pl.BlockSpec((1, tm, di), lambda i: (0, i, 0)),
                pl.BlockSpec((1, tm, di), lambda i: (1, i, 0)),
                pl.BlockSpec((tm, di), row),
                pl.BlockSpec((tm, di), lambda i: (i, zb)),
                pl.BlockSpec((tm, d), row),
                pl.BlockSpec((tm, d), lambda i: (i, fb + 1)),
                pl.BlockSpec((tm, d), lambda i: (i, fb + 2)),
                pl.BlockSpec((tm, d), row),
                pl.BlockSpec((None, 1, d), modrow),
                pl.BlockSpec((None, 1, d), modrow),
                pl.BlockSpec((None, 1, d), modrow),
                pl.BlockSpec((1, di), const),
                pl.BlockSpec((1, di), const),
                pl.BlockSpec((di, d), const),
                pl.BlockSpec((d, d), const),
                pl.BlockSpec((d, d), const),
                pl.BlockSpec((1, d), const),
                pl.BlockSpec((d, LANES), const),
                pl.BlockSpec((1, LANES), const)]
    out_specs = [pl.BlockSpec((tm, d), row), pl.BlockSpec((tm, d // 2), row),
                 pl.BlockSpec((tm, LANES), row), pl.BlockSpec((tm, LANES), row), pl.BlockSpec((tm, LANES), row),
                 pl.BlockSpec((8, LANES), const)]
    out_shape = [jax.ShapeDtypeStruct((m, d), F32), jax.ShapeDtypeStruct((m, d // 2), jnp.uint32),
                 jax.ShapeDtypeStruct((m, LANES), jnp.int32), jax.ShapeDtypeStruct((m, LANES), F32),
                 jax.ShapeDtypeStruct((m, LANES), jnp.int32), jax.ShapeDtypeStruct((8, LANES), F32)]
    return pl.pallas_call(
        _tail_kernel,
        grid=(m // tm,),
        in_specs=in_specs,
        out_specs=out_specs,
        out_shape=out_shape,
        scratch_shapes=[pltpu.VMEM((8, LANES), F32)],
        compiler_params=_cparams("arbitrary"),
        name="tail",
    )(y2, y2, xbc2d, p2d, ff, p2d, p2d, x2d, g1, sc2, sh2, dsk, snw, wso, wfo, wo, n2w, wr, br)


def _gather_rows(src, idx):
    n_out = idx.shape[0]
    width = src.shape[1]
    win = SC_GATHER_WINDOW
    info = plsc.get_sparse_core_info()
    n_cores, n_workers = info.num_cores, info.num_cores * info.num_subcores
    per_worker = n_out // n_workers
    assert per_worker * n_workers == n_out and per_worker % (2 * win) == 0
    mesh = plsc.VectorSubcoreMesh(core_axis_name="core", subcore_axis_name="subcore")

    @functools.partial(
        pl.kernel, out_type=jax.ShapeDtypeStruct((n_out, width), src.dtype), mesh=mesh,
        scratch_types=[pltpu.VMEM((win,), jnp.int32), pltpu.VMEM((win,), jnp.int32),
                       pltpu.VMEM((win, width), src.dtype), pltpu.VMEM((win, width), src.dtype),
                       pltpu.SemaphoreType.DMA, pltpu.SemaphoreType.DMA,
                       pltpu.SemaphoreType.DMA, pltpu.SemaphoreType.DMA],
        name="gather_rows")
    def gather(src_hbm, idx_hbm, out_hbm, idx_a, idx_b, rows_a, rows_b, gsem_a, gsem_b, ssem_a, ssem_b):
        worker = lax.axis_index("subcore") * n_cores + lax.axis_index("core")
        base = worker * per_worker

        @pl.loop(0, per_worker, step=2 * win)
        def _(off):
            pltpu.sync_copy(idx_hbm.at[pl.ds(base + off, win)], idx_a)
            ga = pltpu.async_copy(src_hbm.at[idx_a], rows_a, gsem_a)
            pltpu.sync_copy(idx_hbm.at[pl.ds(base + off + win, win)], idx_b)
            gb = pltpu.async_copy(src_hbm.at[idx_b], rows_b, gsem_b)
            ga.wait()
            sa = pltpu.async_copy(rows_a, out_hbm.at[pl.ds(base + off, win)], ssem_a)
            gb.wait()
            sb = pltpu.async_copy(rows_b, out_hbm.at[pl.ds(base + off + win, win)], ssem_b)
            sa.wait()
            sb.wait()

    return gather(src, idx)


def _scatter_rows(src, dest, n_rows):
    m, width = src.shape
    win = SC_GATHER_WINDOW
    info = plsc.get_sparse_core_info()
    n_cores, n_workers = info.num_cores, info.num_cores * info.num_subcores
    per_worker = m // n_workers
    assert per_worker * n_workers == m and per_worker % win == 0
    mesh = plsc.VectorSubcoreMesh(core_axis_name="core", subcore_axis_name="subcore")

    @functools.partial(
        pl.kernel, out_type=jax.ShapeDtypeStruct((n_rows, width), src.dtype), mesh=mesh,
        scratch_types=[pltpu.VMEM((win, width), src.dtype)]
        + [pltpu.VMEM((win,), jnp.int32)] * TOP_K + [pltpu.SemaphoreType.DMA] * TOP_K,
        name="scatter_rows")
    def scatter(src_hbm, dest_hbm, out_hbm, rows_v, *rest):
        idx_v, sems = rest[:TOP_K], rest[TOP_K:]
        worker = lax.axis_index("subcore") * n_cores + lax.axis_index("core")
        base = worker * per_worker

        @pl.loop(0, per_worker, step=win)
        def _(off):
            t0 = base + off
            pltpu.sync_copy(src_hbm.at[pl.ds(t0, win)], rows_v)
            for k in range(TOP_K):
                pltpu.sync_copy(dest_hbm.at[pl.ds(k * m + t0, win)], idx_v[k])
            copies = [pltpu.async_copy(rows_v, out_hbm.at[idx_v[k]], sems[k]) for k in range(TOP_K)]
            for cp in copies:
                cp.wait()

    return scatter(src, dest)


def _expert_kernel(te_ref, nu_ref, tv_ref, x_ref, wgu_ref, bgu_ref, wd_ref, bd_ref, o_ref, wgu_scr, wd_scr):
    i = pl.program_id(0)
    used = i < nu_ref[0]
    new_expert = (i == 0) | (te_ref[i] != te_ref[jnp.maximum(i - 1, 0)])

    @pl.when(used & new_expert)
    def _():
        wgu_scr[...] = wgu_ref[...].astype(BF16)
        wd_scr[...] = wd_ref[...].astype(BF16)

    @pl.when(used)
    def _():
        dff = wd_ref.shape[0]
        half = wgu_ref.shape[0] // 2
        rows = lax.broadcasted_iota(jnp.int32, x_ref.shape, 0)
        xa, xb = _unpack_bf16_pair(jnp.where(rows < tv_ref[i], x_ref[...], jnp.uint32(0)))
        gu = (jnp.dot(xa.astype(BF16), wgu_scr[:half, :], preferred_element_type=F32)
              + jnp.dot(xb.astype(BF16), wgu_scr[half:, :], preferred_element_type=F32)) + bgu_ref[...]
        gate = jnp.minimum(gu[:, :dff], SWIGLU_LIMIT)
        up = jnp.clip(gu[:, dff:], -SWIGLU_LIMIT, SWIGLU_LIMIT)
        act = (up + 1.0) * gate * _sigmoid(SWIGLU_ALPHA * gate)
        y = jnp.dot(act.astype(BF16), wd_scr[...], preferred_element_type=F32) + bd_ref[...]
        o_ref[...] = _pack_bf16_pair(y)

    @pl.when(jnp.logical_not(used))
    def _():
        o_ref[...] = jnp.zeros(o_ref.shape, o_ref.dtype)


def _experts(xs, tile_expert, n_used, tile_valid, wgu, bgu, wd, bd):
    rows, dh = xs.shape
    d = 2 * dh
    tm = MOE_TILE
    n_tiles = rows // tm
    dff2 = wgu.shape[-1]

    def tile(i, te, nu, tv):
        return (jnp.minimum(i, nu[0] - 1), 0)

    def wsel(i, te, nu, tv):
        return (te[jnp.minimum(i, nu[0] - 1)], 0, 0)

    grid_spec = pltpu.PrefetchScalarGridSpec(
        num_scalar_prefetch=3,
        grid=(n_tiles,),
        in_specs=[pl.BlockSpec((tm, dh), tile),
                  pl.BlockSpec((None, d, dff2), wsel),
                  pl.BlockSpec((None, 1, dff2), wsel),
                  pl.BlockSpec((None, dff2 // 2, d), wsel),
                  pl.BlockSpec((None, 1, d), wsel)],
        out_specs=pl.BlockSpec((tm, dh), lambda i, te, nu, tv: (i, 0)),
        scratch_shapes=[pltpu.VMEM((d, dff2), BF16), pltpu.VMEM((dff2 // 2, d), BF16)],
    )
    return pl.pallas_call(
        _expert_kernel,
        grid_spec=grid_spec,
        out_shape=jax.ShapeDtypeStruct((rows, dh), jnp.uint32),
        compiler_params=_cparams("arbitrary"),
        name="experts",
    )(tile_expert, n_used, tile_valid, xs, wgu, bgu, wd, bd)


def _final_kernel(x1_ref, ya_ref, yb_ref, yc_ref, yd_ref, tw_ref, g2_ref, fw_ref, o_ref):
    d = x1_ref.shape[-1]
    half = d // 2
    tw = tw_ref[...]
    acc_hi = jnp.zeros((x1_ref.shape[0], half), F32)
    acc_lo = jnp.zeros((x1_ref.shape[0], half), F32)
    for k, y_ref in enumerate((ya_ref, yb_ref, yc_ref, yd_ref)):
        y_hi, y_lo = _unpack_bf16_pair(y_ref[...])
        acc_hi = acc_hi + tw[:, k:k + 1] * y_hi
        acc_lo = acc_lo + tw[:, k:k + 1] * y_lo
    x_hi = x1_ref[:, :half] + g2_ref[:, :half] * acc_hi
    x_lo = x1_ref[:, half:] + g2_ref[:, half:] * acc_lo
    ms = (jnp.sum(x_hi * x_hi, axis=-1, keepdims=True) + jnp.sum(x_lo * x_lo, axis=-1, keepdims=True)) / d
    inv = lax.rsqrt(ms + EPS)
    o_ref[:, :half] = x_hi * inv * fw_ref[:, :half]
    o_ref[:, half:] = x_lo * inv * fw_ref[:, half:]


def _final(x1, y4, tw, g2, fw, rows_per_mod, tm):
    m, d = x1.shape
    return pl.pallas_call(
        _final_kernel,
        grid=(m // tm,),
        in_specs=[pl.BlockSpec((tm, d), lambda i: (i, 0))]
        + [pl.BlockSpec((None, tm, d // 2), functools.partial(lambda k, i: (k, i, 0), k)) for k in range(TOP_K)]
        + [pl.BlockSpec((tm, LANES), lambda i: (i, 0)),
           pl.BlockSpec((None, 1, d), lambda i: ((i * tm) // rows_per_mod, 0, 0)),
           pl.BlockSpec((1, d), lambda i: (0, 0))],
        out_specs=pl.BlockSpec((tm, d), lambda i: (i, 0)),
        out_shape=jax.ShapeDtypeStruct((m, d), F32),
        compiler_params=_cparams("arbitrary"),
        name="final",
    )(x1, y4, y4, y4, y4, tw, g2, fw.reshape(1, d))


def _dispatch_plan(e, rank, counts, tm):
    n_assign = e.size
    padded = (counts + tm - 1) // tm * tm
    pad_end = jnp.cumsum(padded)
    pad_start = pad_end - padded
    dest = (pad_start[e] + rank).astype(jnp.int32)
    n_tiles = n_assign // tm + N_EXPERTS
    tile_start = jnp.arange(n_tiles, dtype=jnp.int32) * tm
    tile_expert = jnp.minimum(jnp.sum(pad_end[None, :] <= tile_start[:, None], axis=1), N_EXPERTS - 1).astype(jnp.int32)
    n_used = (pad_end[-1] // tm).astype(jnp.int32).reshape(1)
    tile_valid = jnp.clip(pad_start[tile_expert] + counts[tile_expert] - tile_start, 0, tm).astype(jnp.int32)
    return dest, tile_expert, n_used, tile_valid


def kernel(x, c, ctx, c_ctx, w_mod, b_mod, norm1_w, norm2_w, w_in, conv_w, conv_b, dt_bias, a_log, d_skip,
           ssd_norm_w, w_ssd_out, w_four_out, w_o, w_router, b_router, w_gate_up, b_gate_up, w_down, b_down,
           final_norm_w):
    bsz, n, d = x.shape
    n_ctx = ctx.shape[1]
    depth = w_mod.shape[0]
    assert depth == 1, "a stacked model would also need the context stream's residual update"
    d_inner = N_HEADS * HEAD_DIM
    off_dt = d_inner + 2 * BC_WIDTH
    m = bsz * n
    x2d = x.reshape(m, d)
    tm = min(512, n)

    for layer in range(depth):
        rows = -(-(bsz + 1) // 8) * 8
        c_rows = jnp.zeros((rows, d), F32).at[:bsz].set(c).at[bsz].set(c_ctx)
        mod = _modulation(c_rows, w_mod[layer], b_mod[layer])
        mods = [mod[:bsz, i * d:(i + 1) * d].reshape(bsz, 1, d) for i in range(N_MOD)]
        sh1, sc1, g1, sh2, sc2, g2 = mods
        sh1_c = mod[bsz:bsz + 1, :d].reshape(1, 1, d)
        sc1_c = mod[bsz:bsz + 1, d:2 * d].reshape(1, 1, d)

        wl = w_in[layer]
        w_main = jnp.concatenate([wl[:, :off_dt], wl[:, off_dt + 2 * N_HEADS:]], axis=1).astype(BF16)
        w_dt = jnp.zeros((d, 2 * LANES), F32)
        w_dt = w_dt.at[:, :N_HEADS].set(wl[:, off_dt:off_dt + N_HEADS])
        w_dt = w_dt.at[:, LANES:LANES + N_HEADS].set(wl[:, off_dt + N_HEADS:off_dt + 2 * N_HEADS]).astype(BF16)
        pad = jnp.zeros((2, 1, LANES - N_HEADS), F32)
        dt_bias2 = jnp.concatenate([dt_bias[layer].reshape(2, 1, N_HEADS), pad], axis=-1)
        a_log2 = jnp.concatenate([a_log[layer].reshape(2, 1, N_HEADS), pad], axis=-1)

        w_ctx = w_main[:, :off_dt]
        pc, dtc = _in_proj(ctx.reshape(bsz * n_ctx, d), norm1_w[layer], sc1_c, sh1_c, w_ctx, w_dt,
                           bsz * n_ctx, min(512, n_ctx))
        xbc_c = _conv_silu(pc.reshape(bsz, n_ctx, off_dt), conv_w[layer][:, :off_dt], conv_b[layer][:off_dt], off_dt)
        h0 = jnp.zeros((bsz, 2, N_GROUPS, D_STATE, HEADS_PER_GROUP * HEAD_DIM), F32)
        h_ctx = _ssd(xbc_c, dtc.reshape(bsz, n_ctx, 2 * LANES), dt_bias2, a_log2, h0, with_y=False)

        p, dtl = _in_proj(x2d, norm1_w[layer], sc1, sh1, w_main, w_dt, n, tm)
        n_conv = off_dt + 2 * BC_WIDTH
        xbc = _conv_silu(p.reshape(bsz, n, -1), conv_w[layer], conv_b[layer], n_conv)
        y2 = _ssd(xbc, dtl.reshape(bsz, n, 2 * LANES), dt_bias2, a_log2, h_ctx, with_y=True)
        ff = _fnet_mix(p, (n_conv + d_inner) // d, bsz, n)

        dsk = jnp.repeat(d_skip[layer].astype(F32), HEAD_DIM).reshape(1, d_inner)
        wr = jnp.zeros((d, LANES), F32).at[:, :N_EXPERTS].set(w_router[layer])
        br = jnp.full((1, LANES), NEG_BIG, F32).at[0, :N_EXPERTS].set(b_router[layer])
        x1, h2, ti, tw, rk, cnt = _tail(
            y2.reshape(2, m, d_inner), xbc.reshape(m, n_conv), p, ff, x2d, g1, sc2, sh2, dsk,
            ssd_norm_w[layer].reshape(1, d_inner), w_ssd_out[layer].astype(BF16), w_four_out[layer].astype(BF16),
            w_o[layer].astype(BF16), norm2_w[layer].reshape(1, d), wr, br, n, tm)

        counts = cnt[0, :N_EXPERTS].astype(jnp.int32)
        dest, tile_expert, n_used, tile_valid = _dispatch_plan(ti[:, :TOP_K], rk[:, :TOP_K], counts, MOE_TILE)
        dest_km = dest.T.reshape(-1)
        xs = _scatter_rows(h2, dest_km, tile_expert.shape[0] * MOE_TILE)
        ys = _experts(xs, tile_expert, n_used, tile_valid, w_gate_up[layer],
                      b_gate_up[layer].reshape(N_EXPERTS, 1, -1), w_down[layer],
                      b_down[layer].reshape(N_EXPERTS, 1, -1))
        y4 = _gather_rows(ys, dest_km).reshape(TOP_K, m, d // 2)
        x2d = _final(x1, y4, tw, g2, final_norm_w, n, tm)
    return x2d.reshape(bsz, n, d)
```

```python
import functools
import math

import numpy as np
import jax
import jax.numpy as jnp
from jax import lax
from jax.experimental import pallas as pl
from jax.experimental.pallas import tpu as pltpu
from jax.experimental.pallas import tpu_sc as plsc

F32 = jnp.float32
BF16 = jnp.bfloat16

EPS = 1e-6
GRID_W = 64
N_MOD = 6
F_GROUPS = 8
F_GROUP_DIM = 128
HEAD_DIM = 64
N_HEADS = 32
N_GROUPS = 4
HEADS_PER_GROUP = N_HEADS // N_GROUPS
D_STATE = 128
BC_WIDTH = N_GROUPS * D_STATE
CONV_K = 5
CHUNK = 128
N_EXPERTS = 32
TOP_K = 4
SWIGLU_LIMIT = 7.0
SWIGLU_ALPHA = 1.702

LANES = 128
VMEM_LIMIT_BYTES = 56 * 1024 * 1024
NEG_BIG = -1e30
MOE_TILE = 512
TAIL_ROWS = 256
SC_GATHER_WINDOW = 64


def _cparams(*sem):
    return pltpu.CompilerParams(dimension_semantics=sem, vmem_limit_bytes=VMEM_LIMIT_BYTES)


def _sigmoid(v):
    return 1.0 / (1.0 + jnp.exp(-v))


def _silu(v):
    return v * _sigmoid(v)


def _softplus(v):
    return jnp.maximum(v, 0.0) + jnp.log(1.0 + jnp.exp(-jnp.abs(v)))


def _pack_bf16_pair(v):
    w = v.shape[1] // 2
    bits = lax.bitcast_convert_type(v.astype(BF16).astype(F32), jnp.uint32)
    return bits[:, :w] | (bits[:, w:] >> 16)


def _unpack_bf16_pair(p):
    hi = lax.bitcast_convert_type(p & jnp.uint32(0xFFFF0000), F32)
    lo = lax.bitcast_convert_type(p << 16, F32)
    return hi, lo


def _mod_kernel(c_ref, w_ref, b_ref, o_ref):
    s = _silu(c_ref[...]).astype(BF16)
    o_ref[...] = jnp.dot(s, w_ref[...].astype(BF16), preferred_element_type=F32) + b_ref[...]


def _modulation(c_rows, w_mod, b_mod):
    rows, d = c_rows.shape
    n_out = w_mod.shape[1]
    tn = 1024
    return pl.pallas_call(
        _mod_kernel,
        grid=(n_out // tn,),
        in_specs=[pl.BlockSpec((rows, d), lambda j: (0, 0)),
                  pl.BlockSpec((d, tn), lambda j: (0, j)),
                  pl.BlockSpec((1, tn), lambda j: (0, j))],
        out_specs=pl.BlockSpec((rows, tn), lambda j: (0, j)),
        out_shape=jax.ShapeDtypeStruct((rows, n_out), F32),
        compiler_params=_cparams("arbitrary"),
        name="modulation",
    )(c_rows, w_mod, b_mod.reshape(1, n_out))


_INPROJ_TN = 1024


def _inproj_kernel(x_ref, nw_ref, sc_ref, sh_ref, w_ref, wdt_ref, p_ref, dt_ref):
    x = x_ref[...]
    inv = lax.rsqrt(jnp.mean(x * x, axis=-1, keepdims=True) + EPS)
    h = (x * inv * nw_ref[...]) * (1.0 + sc_ref[...]) + sh_ref[...]
    hb = h.astype(BF16)
    dt_ref[...] = jnp.dot(hb, wdt_ref[...], preferred_element_type=F32)
    for j in range(w_ref.shape[1] // _INPROJ_TN):
        cols = slice(j * _INPROJ_TN, (j + 1) * _INPROJ_TN)
        p_ref[:, cols] = jnp.dot(hb, w_ref[:, cols], preferred_element_type=F32).astype(p_ref.dtype)


def _in_proj(x2d, norm_w, sc, sh, w_main, w_dt, rows_per_mod, tm):
    m, d = x2d.shape
    n = w_main.shape[1]
    ndt = w_dt.shape[1]
    resident = dict(pipeline_mode=pl.Buffered(1))
    return pl.pallas_call(
        _inproj_kernel,
        grid=(m // tm,),
        in_specs=[pl.BlockSpec((tm, d), lambda i: (i, 0)),
                  pl.BlockSpec((1, d), lambda i: (0, 0), **resident),
                  pl.BlockSpec((None, 1, d), lambda i: ((i * tm) // rows_per_mod, 0, 0)),
                  pl.BlockSpec((None, 1, d), lambda i: ((i * tm) // rows_per_mod, 0, 0)),
                  pl.BlockSpec((d, n), lambda i: (0, 0), **resident),
                  pl.BlockSpec((d, ndt), lambda i: (0, 0), **resident)],
        out_specs=[pl.BlockSpec((tm, n), lambda i: (i, 0)),
                   pl.BlockSpec((tm, ndt), lambda i: (i, 0))],
        out_shape=[jax.ShapeDtypeStruct((m, n), BF16),
                   jax.ShapeDtypeStruct((m, ndt), F32)],
        compiler_params=_cparams("arbitrary"),
        name="in_proj",
    )(x2d, norm_w.reshape(1, d), sc, sh, w_main, w_dt)


_CONV_HALO = 16


_CONV_ROWS = 64


def _conv_shift_matrix(rc):
    win = rc + 2 * _CONV_HALO
    s = np.zeros((rc, CONV_K * win), np.float32)
    for k in range(CONV_K):
        for l in range(rc):
            s[l, k * win + _CONV_HALO + l + k - CONV_K // 2] = 1.0
    return jnp.asarray(s, BF16)


def _conv_kernel(p_ref, w_ref, b_ref, s_ref, o_ref, *, n, rc):
    tc = o_ref.shape[-1]
    w = w_ref[...].astype(BF16)
    bias = b_ref[...]
    smat = s_ref[...]
    zeros = jnp.zeros((_CONV_HALO, tc), BF16)
    for r0 in range(0, n, rc):
        top = zeros if r0 == 0 else p_ref[r0 - _CONV_HALO:r0, :]
        bot = zeros if r0 + rc >= n else p_ref[r0 + rc:r0 + rc + _CONV_HALO, :]
        window = jnp.concatenate([top, p_ref[r0:r0 + rc, :], bot], axis=0)
        taps = jnp.concatenate([window * w[k:k + 1, :] for k in range(CONV_K)], axis=0)
        acc = jnp.dot(smat, taps, preferred_element_type=F32) + bias
        o_ref[r0:r0 + rc, :] = _silu(acc).astype(o_ref.dtype)


def _conv_silu(p3d, conv_w, conv_b, n_ch):
    bsz, n, _ = p3d.shape
    tc = 512
    rc = min(_CONV_ROWS, n)
    smat = _conv_shift_matrix(rc)
    return pl.pallas_call(
        functools.partial(_conv_kernel, n=n, rc=rc),
        grid=(bsz, n_ch // tc),
        in_specs=[pl.BlockSpec((None, n, tc), lambda b, j: (b, 0, j)),
                  pl.BlockSpec((CONV_K, tc), lambda b, j: (0, j)),
                  pl.BlockSpec((1, tc), lambda b, j: (0, j)),
                  pl.BlockSpec(smat.shape, lambda b, j: (0, 0))],
        out_specs=pl.BlockSpec((None, n, tc), lambda b, j: (b, 0, j)),
        out_shape=jax.ShapeDtypeStruct((bsz, n, n_ch), BF16),
        compiler_params=_cparams("arbitrary", "arbitrary"),
        name="conv_silu",
    )(p3d, conv_w, conv_b.reshape(1, -1), smat)


LOG2E = 1.4426950408889634
SSD_SUB = 2


def _head_expand_matrix():
    e = np.zeros((LANES, N_HEADS * HEAD_DIM), np.float32)
    for h in range(N_HEADS):
        e[h, h * HEAD_DIM:(h + 1) * HEAD_DIM] = 1.0
    return jnp.asarray(e, BF16)


def _ssd_chunk_terms(dt_raw, dt_bias, a_log, expand, tri, fwd):
    q = CHUNK
    dt = _softplus(dt_raw + dt_bias)
    da = dt * (-jnp.exp(a_log))
    hi = da.astype(BF16)
    r1 = da - hi.astype(F32)
    mid = r1.astype(BF16)
    lo = (r1 - mid.astype(F32)).astype(BF16)
    ones_tri = jnp.where(tri, 1.0, 0.0).astype(BF16)
    cs3 = jnp.dot(ones_tri, jnp.concatenate([hi, mid, lo], axis=1), preferred_element_type=F32)
    cs = cs3[:, :LANES] + cs3[:, LANES:2 * LANES] + cs3[:, 2 * LANES:]
    tot = jnp.where(fwd, cs[q - 1:q, :], cs[0:1, :])
    dte = dt * jnp.exp(tot - cs)
    dec = jnp.exp(tot)
    dec_hi = dec.astype(BF16)
    dec_lo = (dec - dec_hi.astype(F32)).astype(BF16)
    dec2 = jnp.concatenate([dec_hi, dec_lo, jnp.zeros((6, LANES), BF16)], axis=0)
    stacked = jnp.concatenate([jnp.exp(cs).astype(BF16), dte.astype(BF16), dec2], axis=0)
    spread = jnp.dot(stacked, expand, preferred_element_type=F32)
    ecs_e = spread[:q]
    dte_e = spread[q:2 * q].astype(BF16)
    dec_e = jnp.broadcast_to(spread[2 * q:2 * q + 1] + spread[2 * q + 1:2 * q + 2], (8, spread.shape[1]))
    cs2 = cs * LOG2E
    return cs2, cs2.T, dt.T, ecs_e, dte_e, dec_e


def _ssd_kernel(*refs, with_y, nc):
    if with_y:
        (x_ref, b_ref, c_ref, dt_ref, dtn_ref, dtb_ref, alog_ref, exp_ref, h0_ref, y_ref,
         s_scr, cs_scr, dee_scr, dce_scr, ecs_scr) = refs
    else:
        (x_ref, b_ref, dt_ref, dtn_ref, dtb_ref, alog_ref, exp_ref, h0_ref, hout_ref,
         s_scr, cs_scr, dee_scr, dce_scr) = refs
    q = CHUNK
    gw = HEADS_PER_GROUP * HEAD_DIM
    d = pl.program_id(1)
    c = pl.program_id(2)
    row = lax.broadcasted_iota(jnp.int32, (q, q), 0)
    col = lax.broadcasted_iota(jnp.int32, (q, q), 1)
    fwd = d == 0
    tri = jnp.where(fwd, row - col, col - row) >= 0
    first_half = col < HEAD_DIM

    def store_terms(slot, terms):
        cs2, cs2_t, dt_t, ecs_e, dte_e, dec_e = terms
        cs_scr[slot, 0] = cs2
        cs_scr[slot, 1] = cs2_t
        cs_scr[slot, 2] = dt_t
        dee_scr[slot] = dte_e
        dce_scr[slot] = dec_e
        if with_y:
            ecs_scr[slot] = ecs_e

    def terms_of(ref, sub):
        rows = pl.ds(pl.multiple_of(sub * q, q), q)
        return _ssd_chunk_terms(ref[rows, :], dtb_ref[...], alog_ref[...], exp_ref[...], tri, fwd)

    def sub_of(j):
        return jnp.where(fwd, j, SSD_SUB - 1 - j)

    @pl.when(c == 0)
    def _():
        s_scr[...] = h0_ref[...]
        store_terms(0, terms_of(dt_ref, sub_of(0)))

    for j in range(SSD_SUB):
        slot = j % 2
        rows = pl.ds(pl.multiple_of(sub_of(j) * q, q), q)
        cs2 = cs_scr[slot, 0]
        cs2_t = cs_scr[slot, 1]
        dt_t = cs_scr[slot, 2]
        if j + 1 < SSD_SUB:
            store_terms(1 - slot, terms_of(dt_ref, sub_of(j + 1)))
        else:
            store_terms(1 - slot, terms_of(dtn_ref, sub_of(0)))

        for g in range(N_GROUPS):
            cols = slice(g * gw, (g + 1) * gw)
            bg = b_ref[rows, g * D_STATE:(g + 1) * D_STATE]
            xg = x_ref[rows, cols]
            s_old = s_scr[g]
            if with_y:
                cg = c_ref[rows, g * D_STATE:(g + 1) * D_STATE]
                cb = lax.dot_general(cg, bg, (((1,), (1,)), ((), ())), preferred_element_type=F32)
                y_off = jnp.dot(cg, s_old.astype(BF16), preferred_element_type=F32)
                for k in range(HEADS_PER_GROUP // 2):
                    pair = g * (HEADS_PER_GROUP // 2) + k
                    lanes = slice(pair * LANES, (pair + 1) * LANES)
                    xp = xg[:, k * LANES:(k + 1) * LANES]
                    xz = jnp.zeros_like(xp)
                    lhs = []
                    for hh in (2 * pair, 2 * pair + 1):
                        colb = jnp.broadcast_to(cs2[:, hh:hh + 1], (q, q))
                        rowb = cs2_t[hh:hh + 1, :]
                        decay = jnp.exp2(jnp.where(tri, colb - rowb, -jnp.inf))
                        lhs.append((cb * decay * dt_t[hh:hh + 1, :]).astype(BF16))
                    y_diag = jnp.dot(jnp.concatenate(lhs, axis=1),
                                     jnp.concatenate([jnp.where(first_half, xp, xz), jnp.where(first_half, xz, xp)],
                                                     axis=0),
                                     preferred_element_type=F32)
                    y = y_diag + ecs_scr[slot, :, lanes] * y_off[:, k * LANES:(k + 1) * LANES]
                    y_ref[rows, lanes] = y.astype(y_ref.dtype)
            xdte = xg * dee_scr[slot, :, cols]
            upd = lax.dot_general(bg, xdte, (((0,), (0,)), ((), ())), preferred_element_type=F32)
            s_scr[g] = s_old * dce_scr[slot, 0:1, cols] + upd

    if not with_y:
        @pl.when(c == nc - 1)
        def _():
            hout_ref[...] = s_scr[...]


def _ssd(xbc, dt_raw, dt_bias2, a_log2, h0, with_y):
    bsz, n, _ = xbc.shape
    blk = SSD_SUB * CHUNK
    nc = n // blk
    d_inner = N_HEADS * HEAD_DIM
    gw = HEADS_PER_GROUP * HEAD_DIM
    x_blk = d_inner // BC_WIDTH

    def ceff(dd, cc):
        return cc + dd * (nc - 1 - 2 * cc)

    def cnext(dd, cc):
        return ceff(dd, jnp.minimum(cc + 1, nc - 1))

    in_specs = [pl.BlockSpec((None, blk, d_inner), lambda b, dd, cc: (b, ceff(dd, cc), 0)),
                pl.BlockSpec((None, blk, BC_WIDTH), lambda b, dd, cc: (b, ceff(dd, cc), x_blk + dd))]
    args = [xbc, xbc]
    if with_y:
        in_specs.append(pl.BlockSpec((None, blk, BC_WIDTH), lambda b, dd, cc: (b, ceff(dd, cc), x_blk + 2 + dd)))
        args.append(xbc)
    in_specs += [pl.BlockSpec((None, blk, LANES), lambda b, dd, cc: (b, ceff(dd, cc), dd)),
                 pl.BlockSpec((None, blk, LANES), lambda b, dd, cc: (b, cnext(dd, cc), dd)),
                 pl.BlockSpec((None, 1, LANES), lambda b, dd, cc: (dd, 0, 0)),
                 pl.BlockSpec((None, 1, LANES), lambda b, dd, cc: (dd, 0, 0)),
                 pl.BlockSpec((LANES, d_inner), lambda b, dd, cc: (0, 0)),
                 pl.BlockSpec((None, None, N_GROUPS, D_STATE, gw), lambda b, dd, cc: (b, dd, 0, 0, 0))]
    args += [dt_raw, dt_raw, dt_bias2, a_log2, _head_expand_matrix(), h0]
    scratch = [pltpu.VMEM((N_GROUPS, D_STATE, gw), F32),
               pltpu.VMEM((2, 3, CHUNK, LANES), F32),
               pltpu.VMEM((2, CHUNK, d_inner), BF16),
               pltpu.VMEM((2, 8, d_inner), F32)]
    if with_y:
        out_specs = pl.BlockSpec((None, None, blk, d_inner), lambda b, dd, cc: (dd, b, ceff(dd, cc), 0))
        out_shape = jax.ShapeDtypeStruct((2, bsz, n, d_inner), BF16)
        scratch += [pltpu.VMEM((2, CHUNK, d_inner), F32)]
    else:
        out_specs = pl.BlockSpec((None, None, N_GROUPS, D_STATE, gw), lambda b, dd, cc: (b, dd, 0, 0, 0))
        out_shape = jax.ShapeDtypeStruct((bsz, 2, N_GROUPS, D_STATE, gw), F32)
    return pl.pallas_call(
        functools.partial(_ssd_kernel, with_y=with_y, nc=nc),
        grid=(bsz, 2, nc),
        in_specs=in_specs,
        out_specs=out_specs,
        out_shape=out_shape,
        scratch_shapes=scratch,
        compiler_params=_cparams("arbitrary", "arbitrary", "arbitrary"),
        name="ssd_scan" if with_y else "ssd_ctx_state",
    )(*args)


def _dft_mats(n):
    k = np.arange(n)
    ang = 2.0 * np.pi * ((k[:, None] * k[None, :]) % n) / n
    return np.cos(ang), np.sin(ang)


def _fnet_chan_kernel(u_ref, m_ref, o_ref):
    m = m_ref[...]
    for g in range(F_GROUPS):
        ug = u_ref[:, g * F_GROUP_DIM:(g + 1) * F_GROUP_DIM]
        pq = jnp.dot(ug, m, preferred_element_type=F32)
        o_ref[0, :, g * F_GROUP_DIM:(g + 1) * F_GROUP_DIM] = pq[:, :F_GROUP_DIM].astype(o_ref.dtype)
        o_ref[1, :, g * F_GROUP_DIM:(g + 1) * F_GROUP_DIM] = pq[:, F_GROUP_DIM:].astype(o_ref.dtype)


def _stage_lane_blocks(src_ref, scr, rows):
    width = src_ref.shape[-1]
    for t in range(2):
        v = src_ref[t].reshape(rows, width).astype(F32)
        for lb in range(width // LANES):
            scr[t, lb] = v[:, lb * LANES:(lb + 1) * LANES]


def _strided_rows(scr, t, start, count, stride, width):
    return jnp.concatenate([scr[t, lb, pl.ds(start, count, stride=stride), :] for lb in range(width // LANES)],
                           axis=1)


def _fnet_stage1_kernel(z_ref, m1_ref, twc_ref, tws_ref, o_ref, zs_scr, *, tcs):
    r = z_ref.shape[1]
    width = z_ref.shape[-1]
    _stage_lane_blocks(z_ref, zs_scr, r * tcs)
    m1 = m1_ref[...]
    for j in range(tcs):
        zz = jnp.concatenate([_strided_rows(zs_scr, 0, j, r, tcs, width),
                              _strided_rows(zs_scr, 1, j, r, tcs, width)], axis=0).astype(BF16)
        u = jnp.dot(m1, zz, preferred_element_type=F32)
        ur = u[:r]
        ui = u[r:]
        tc = twc_ref[j]
        ts = tws_ref[j]
        o_ref[0, j] = (ur * tc + ui * ts).astype(o_ref.dtype)
        o_ref[1, j] = (ui * tc - ur * ts).astype(o_ref.dtype)


def _fnet_stage2_kernel(u_ref, m2_ref, o_ref, us_scr, os_scr, *, tbs):
    c = u_ref.shape[1]
    width = u_ref.shape[-1]
    _stage_lane_blocks(u_ref, us_scr, c * tbs)
    m2 = m2_ref[...]
    for j in range(tbs):
        uu = jnp.concatenate([_strided_rows(us_scr, 0, j, c, tbs, width),
                              _strided_rows(us_scr, 1, j, c, tbs, width)], axis=0).astype(BF16)
        x = jnp.dot(m2, uu, preferred_element_type=F32)
        for lb in range(width // LANES):
            os_scr[lb, pl.ds(j, c, stride=tbs), :] = x[:, lb * LANES:(lb + 1) * LANES]
    out = jnp.concatenate([os_scr[lb] for lb in range(width // LANES)], axis=1)
    o_ref[...] = out.reshape(c, tbs, width).astype(o_ref.dtype)


def _fnet_mix(p2d, col_blk, bsz, n):
    m = p2d.shape[0]
    width = F_GROUPS * F_GROUP_DIM
    cgrid = GRID_W
    rgrid = n // cgrid
    scale = 1.0 / math.sqrt(n * F_GROUP_DIM)

    cc, sc = _dft_mats(F_GROUP_DIM)
    chan = jnp.asarray(np.concatenate([cc, sc], axis=1) * scale, BF16)
    tm = 512
    pq = pl.pallas_call(
        _fnet_chan_kernel,
        grid=(m // tm,),
        in_specs=[pl.BlockSpec((tm, width), lambda i: (i, col_blk)),
                  pl.BlockSpec((F_GROUP_DIM, 2 * F_GROUP_DIM), lambda i: (0, 0))],
        out_specs=pl.BlockSpec((2, tm, width), lambda i: (0, i, 0)),
        out_shape=jax.ShapeDtypeStruct((2, m, width), BF16),
        compiler_params=_cparams("arbitrary"),
        name="fnet_chan",
    )(p2d, chan)

    cr, sr = _dft_mats(rgrid)
    m1 = jnp.asarray(np.block([[cr, -sr], [-sr, -cr]]), BF16)
    bb = np.arange(rgrid)[None, :]
    ci = np.arange(cgrid)[:, None]
    ang = 2.0 * np.pi * ((ci * bb) % n) / n
    twc = jnp.asarray(np.cos(ang)[:, :, None], F32)
    tws = jnp.asarray(np.sin(ang)[:, :, None], F32)
    tcs = 16
    z5 = pq.reshape(2, bsz, rgrid, cgrid, width)
    u5 = pl.pallas_call(
        functools.partial(_fnet_stage1_kernel, tcs=tcs),
        grid=(bsz, cgrid // tcs),
        in_specs=[pl.BlockSpec((2, None, rgrid, tcs, width), lambda b, j: (0, b, 0, j, 0)),
                  pl.BlockSpec((2 * rgrid, 2 * rgrid), lambda b, j: (0, 0)),
                  pl.BlockSpec((tcs, rgrid, 1), lambda b, j: (j, 0, 0)),
                  pl.BlockSpec((tcs, rgrid, 1), lambda b, j: (j, 0, 0))],
        out_specs=pl.BlockSpec((2, None, tcs, rgrid, width), lambda b, j: (0, b, j, 0, 0)),
        out_shape=jax.ShapeDtypeStruct((2, bsz, cgrid, rgrid, width), BF16),
        scratch_shapes=[pltpu.VMEM((2, width // LANES, rgrid * tcs, LANES), F32)],
        compiler_params=_cparams("arbitrary", "arbitrary"),
        name="fnet_stage1",
    )(z5, m1, twc, tws)

    cc2, sc2 = _dft_mats(cgrid)
    m2 = jnp.asarray(np.concatenate([cc2, sc2], axis=1), BF16)
    tbs = min(16, rgrid)
    out = pl.pallas_call(
        functools.partial(_fnet_stage2_kernel, tbs=tbs),
        grid=(bsz, rgrid // tbs),
        in_specs=[pl.BlockSpec((2, None, cgrid, tbs, width), lambda b, j: (0, b, 0, j, 0)),
                  pl.BlockSpec((cgrid, 2 * cgrid), lambda b, j: (0, 0))],
        out_specs=pl.BlockSpec((None, cgrid, tbs, width), lambda b, j: (b, 0, j, 0)),
        out_shape=jax.ShapeDtypeStruct((bsz, cgrid, rgrid, width), BF16),
        scratch_shapes=[pltpu.VMEM((2, width // LANES, cgrid * tbs, LANES), F32),
                        pltpu.VMEM((width // LANES, cgrid * tbs, LANES), F32)],
        compiler_params=_cparams("arbitrary", "arbitrary"),
        name="fnet_stage2",
    )(u5, m2)
    return out.reshape(m, width)


def _tail_kernel(yf_ref, yb_ref, xs_ref, z_ref, ff_ref, gf_ref, gs_ref, x_ref, g1_ref, sc2_ref, sh2_ref,
                 dsk_ref, snw_ref, wso_ref, wfo_ref, wo_ref, n2w_ref, wr_ref, br_ref,
                 x1_ref, h2_ref, ti_ref, tw_ref, rk_ref, cnt_ref, carry_scr):
    y = (yf_ref[0] + yb_ref[0]).astype(F32) + dsk_ref[...] * xs_ref[...].astype(F32)
    g = y * _silu(z_ref[...]).astype(F32)
    inv = lax.rsqrt(jnp.mean(g * g, axis=-1, keepdims=True) + EPS)
    gn = (g * inv * snw_ref[...]).astype(BF16)
    y_ssd = jnp.dot(gn, wso_ref[...], preferred_element_type=F32)
    y_four = jnp.dot(ff_ref[...], wfo_ref[...], preferred_element_type=F32)
    t = _sigmoid(gf_ref[...]) * y_four.astype(BF16) + _sigmoid(gs_ref[...]) * y_ssd.astype(BF16)
    mix = jnp.dot(t, wo_ref[...], preferred_element_type=F32)
    x1 = x_ref[...] + g1_ref[...] * mix
    x1_ref[...] = x1
    inv2 = lax.rsqrt(jnp.mean(x1 * x1, axis=-1, keepdims=True) + EPS)
    h2 = (x1 * inv2 * n2w_ref[...]) * (1.0 + sc2_ref[...]) + sh2_ref[...]
    h2_ref[...] = _pack_bf16_pair(h2)
    h_hi = h2.astype(BF16)
    h_lo = (h2 - h_hi.astype(F32)).astype(BF16)
    wr = wr_ref[...]
    w_hi = wr.astype(BF16)
    w_lo = (wr - w_hi.astype(F32)).astype(BF16)
    logits = (jnp.dot(h_hi, w_hi, preferred_element_type=F32)
              + jnp.dot(h_hi, w_lo, preferred_element_type=F32)
              + jnp.dot(h_lo, w_hi, preferred_element_type=F32)) + br_ref[...]
    tm = logits.shape[0]
    lane = lax.broadcasted_iota(jnp.int32, (tm, LANES), 1)
    vals, idxs = [], []
    cur = logits
    for _ in range(TOP_K):
        mx = jnp.max(cur, axis=-1, keepdims=True)
        ix = jnp.min(jnp.where(cur == mx, lane, LANES), axis=-1, keepdims=True)
        vals.append(mx)
        idxs.append(ix)
        cur = jnp.where(lane == ix, NEG_BIG * 2.0, cur)
    es = [jnp.exp(v - vals[0]) for v in vals]
    den = es[0] + es[1] + es[2] + es[3]
    ti = jnp.zeros((tm, LANES), jnp.int32)
    tw = jnp.zeros((tm, LANES), F32)
    for k in range(TOP_K):
        ti = jnp.where(lane == k, idxs[k], ti)
        tw = jnp.where(lane == k, es[k] / den, tw)
    ti_ref[...] = ti
    tw_ref[...] = tw

    @pl.when(pl.program_id(0) == 0)
    def _():
        carry_scr[...] = jnp.zeros(carry_scr.shape, F32)

    onehots = [jnp.where(lane == ix, 1.0, 0.0) for ix in idxs]
    cnt = onehots[0] + onehots[1] + onehots[2] + onehots[3]
    r_i = lax.broadcasted_iota(jnp.int32, (tm, tm), 0)
    c_i = lax.broadcasted_iota(jnp.int32, (tm, tm), 1)
    earlier = jnp.where(r_i > c_i, 1.0, 0.0).astype(BF16)
    before = jnp.dot(earlier, cnt.astype(BF16), preferred_element_type=F32) + carry_scr[0:1, :]
    rk = jnp.zeros((tm, LANES), jnp.int32)
    for k in range(TOP_K):
        rank_k = jnp.sum(onehots[k] * before, axis=-1, keepdims=True)
        rk = jnp.where(lane == k, rank_k.astype(jnp.int32), rk)
    rk_ref[...] = rk
    total = carry_scr[...] + jnp.sum(cnt, axis=0, keepdims=True)
    carry_scr[...] = total
    cnt_ref[...] = total


def _tail(y2, xbc2d, p2d, ff, x2d, g1, sc2, sh2, dsk, snw, wso, wfo, wo, n2w, wr, br, rows_per_mod, tm):
    m, d = x2d.shape
    di = y2.shape[-1]
    zb = (2 * di) // di
    fb = (3 * di) // d
    row = lambda i: (i, 0)
    modrow = lambda i: ((i * tm) // rows_per_mod, 0, 0)
    const = lambda i: (0, 0)
    in_specs = [pl.BlockSpec((1, tm, di), lambda i: (0, i, 0)),
                pl.BlockSpec((1, tm, di), lambda i: (1, i, 0)),
                pl.BlockSpec((tm, di), row),
                pl.BlockSpec((tm, di), lambda i: (i, zb)),
                pl.BlockSpec((tm, d), row),
                pl.BlockSpec((tm, d), lambda i: (i, fb + 1)),
                pl.BlockSpec((tm, d), lambda i: (i, fb + 2)),
                pl.BlockSpec((tm, d), row),
                pl.BlockSpec((None, 1, d), modrow),
                pl.BlockSpec((None, 1, d), modrow),
                pl.BlockSpec((None, 1, d), modrow),
                pl.BlockSpec((1, di), const),
                pl.BlockSpec((1, di), const),
                pl.BlockSpec((di, d), const),
                pl.BlockSpec((d, d), const),
                pl.BlockSpec((d, d), const),
                pl.BlockSpec((1, d), const),
                pl.BlockSpec((d, LANES), const),
                pl.BlockSpec((1, LANES), const)]
    out_specs = [pl.BlockSpec((tm, d), row), pl.BlockSpec((tm, d // 2), row),
                 pl.BlockSpec((tm, LANES), row), pl.BlockSpec((tm, LANES), row), pl.BlockSpec((tm, LANES), row),
                 pl.BlockSpec((8, LANES), const)]
    out_shape = [jax.ShapeDtypeStruct((m, d), F32), jax.ShapeDtypeStruct((m, d // 2), jnp.uint32),
                 jax.ShapeDtypeStruct((m, LANES), jnp.int32), jax.ShapeDtypeStruct((m, LANES), F32),
                 jax.ShapeDtypeStruct((m, LANES), jnp.int32), jax.ShapeDtypeStruct((8, LANES), F32)]
    return pl.pallas_call(
        _tail_kernel,
        grid=(m // tm,),
        in_specs=in_specs,
        out_specs=out_specs,
        out_shape=out_shape,
        scratch_shapes=[pltpu.VMEM((8, LANES), F32)],
        compiler_params=_cparams("arbitrary"),
        name="tail",
    )(y2, y2, xbc2d, p2d, ff, p2d, p2d, x2d, g1, sc2, sh2, dsk, snw, wso, wfo, wo, n2w, wr, br)


def _gather_rows(src, idx):
    n_out = idx.shape[0]
    width = src.shape[1]
    win = SC_GATHER_WINDOW
    info = plsc.get_sparse_core_info()
    n_cores, n_workers = info.num_cores, info.num_cores * info.num_subcores
    per_worker = n_out // n_workers
    assert per_worker * n_workers == n_out and per_worker % (2 * win) == 0
    mesh = plsc.VectorSubcoreMesh(core_axis_name="core", subcore_axis_name="subcore")

    @functools.partial(
        pl.kernel, out_type=jax.ShapeDtypeStruct((n_out, width), src.dtype), mesh=mesh,
        scratch_types=[pltpu.VMEM((win,), jnp.int32), pltpu.VMEM((win,), jnp.int32),
                       pltpu.VMEM((win, width), src.dtype), pltpu.VMEM((win, width), src.dtype),
                       pltpu.SemaphoreType.DMA, pltpu.SemaphoreType.DMA,
                       pltpu.SemaphoreType.DMA, pltpu.SemaphoreType.DMA],
        name="gather_rows")
    def gather(src_hbm, idx_hbm, out_hbm, idx_a, idx_b, rows_a, rows_b, gsem_a, gsem_b, ssem_a, ssem_b):
        worker = lax.axis_index("subcore") * n_cores + lax.axis_index("core")
        base = worker * per_worker

        @pl.loop(0, per_worker, step=2 * win)
        def _(off):
            pltpu.sync_copy(idx_hbm.at[pl.ds(base + off, win)], idx_a)
            ga = pltpu.async_copy(src_hbm.at[idx_a], rows_a, gsem_a)
            pltpu.sync_copy(idx_hbm.at[pl.ds(base + off + win, win)], idx_b)
            gb = pltpu.async_copy(src_hbm.at[idx_b], rows_b, gsem_b)
            ga.wait()
            sa = pltpu.async_copy(rows_a, out_hbm.at[pl.ds(base + off, win)], ssem_a)
            gb.wait()
            sb = pltpu.async_copy(rows_b, out_hbm.at[pl.ds(base + off + win, win)], ssem_b)
            sa.wait()
            sb.wait()

    return gather(src, idx)


def _scatter_rows(src, dest, n_rows):
    m, width = src.shape
    win = SC_GATHER_WINDOW
    info = plsc.get_sparse_core_info()
    n_cores, n_workers = info.num_cores, info.num_cores * info.num_subcores
    per_worker = m // n_workers
    assert per_worker * n_workers == m and per_worker % win == 0
    mesh = plsc.VectorSubcoreMesh(core_axis_name="core", subcore_axis_name="subcore")

    @functools.partial(
        pl.kernel, out_type=jax.ShapeDtypeStruct((n_rows, width), src.dtype), mesh=mesh,
        scratch_types=[pltpu.VMEM((win, width), src.dtype)]
        + [pltpu.VMEM((win,), jnp.int32)] * TOP_K + [pltpu.SemaphoreType.DMA] * TOP_K,
        name="scatter_rows")
    def scatter(src_hbm, dest_hbm, out_hbm, rows_v, *rest):
        idx_v, sems = rest[:TOP_K], rest[TOP_K:]
        worker = lax.axis_index("subcore") * n_cores + lax.axis_index("core")
        base = worker * per_worker

        @pl.loop(0, per_worker, step=win)
        def _(off):
            t0 = base + off
            pltpu.sync_copy(src_hbm.at[pl.ds(t0, win)], rows_v)
            for k in range(TOP_K):
                pltpu.sync_copy(dest_hbm.at[pl.ds(k * m + t0, win)], idx_v[k])
            copies = [pltpu.async_copy(rows_v, out_hbm.at[idx_v[k]], sems[k]) for k in range(TOP_K)]
            for cp in copies:
                cp.wait()

    return scatter(src, dest)


def _expert_kernel(te_ref, nu_ref, tv_ref, x_ref, wgu_ref, bgu_ref, wd_ref, bd_ref, o_ref, wgu_scr, wd_scr):
    i = pl.program_id(0)
    used = i < nu_ref[0]
    new_expert = (i == 0) | (te_ref[i] != te_ref[jnp.maximum(i - 1, 0)])

    @pl.when(used & new_expert)
    def _():
        wgu_scr[...] = wgu_ref[...].astype(BF16)
        wd_scr[...] = wd_ref[...].astype(BF16)

    @pl.when(used)
    def _():
        dff = wd_ref.shape[0]
        half = wgu_ref.shape[0] // 2
        rows = lax.broadcasted_iota(jnp.int32, x_ref.shape, 0)
        xa, xb = _unpack_bf16_pair(jnp.where(rows < tv_ref[i], x_ref[...], jnp.uint32(0)))
        gu = (jnp.dot(xa.astype(BF16), wgu_scr[:half, :], preferred_element_type=F32)
              + jnp.dot(xb.astype(BF16), wgu_scr[half:, :], preferred_element_type=F32)) + bgu_ref[...]
        gate = jnp.minimum(gu[:, :dff], SWIGLU_LIMIT)
        up = jnp.clip(gu[:, dff:], -SWIGLU_LIMIT, SWIGLU_LIMIT)
        act = (up + 1.0) * gate * _sigmoid(SWIGLU_ALPHA * gate)
        y = jnp.dot(act.astype(BF16), wd_scr[...], preferred_element_type=F32) + bd_ref[...]
        o_ref[...] = _pack_bf16_pair(y)

    @pl.when(jnp.logical_not(used))
    def _():
        o_ref[...] = jnp.zeros(o_ref.shape, o_ref.dtype)


def _experts(xs, tile_expert, n_used, tile_valid, wgu, bgu, wd, bd):
    rows, dh = xs.shape
    d = 2 * dh
    tm = MOE_TILE
    n_tiles = rows // tm
    dff2 = wgu.shape[-1]

    def tile(i, te, nu, tv):
        return (jnp.minimum(i, nu[0] - 1), 0)

    def wsel(i, te, nu, tv):
        return (te[jnp.minimum(i, nu[0] - 1)], 0, 0)

    grid_spec = pltpu.PrefetchScalarGridSpec(
        num_scalar_prefetch=3,
        grid=(n_tiles,),
        in_specs=[pl.BlockSpec((tm, dh), tile),
                  pl.BlockSpec((None, d, dff2), wsel),
                  pl.BlockSpec((None, 1, dff2), wsel),
                  pl.BlockSpec((None, dff2 // 2, d), wsel),
                  pl.BlockSpec((None, 1, d), wsel)],
        out_specs=pl.BlockSpec((tm, dh), lambda i, te, nu, tv: (i, 0)),
        scratch_shapes=[pltpu.VMEM((d, dff2), BF16), pltpu.VMEM((dff2 // 2, d), BF16)],
    )
    return pl.pallas_call(
        _expert_kernel,
        grid_spec=grid_spec,
        out_shape=jax.ShapeDtypeStruct((rows, dh), jnp.uint32),
        compiler_params=_cparams("arbitrary"),
        name="experts",
    )(tile_expert, n_used, tile_valid, xs, wgu, bgu, wd, bd)


def _final_kernel(x1_ref, ya_ref, yb_ref, yc_ref, yd_ref, tw_ref, g2_ref, fw_ref, o_ref):
    d = x1_ref.shape[-1]
    half = d // 2
    tw = tw_ref[...]
    acc_hi = jnp.zeros((x1_ref.shape[0], half), F32)
    acc_lo = jnp.zeros((x1_ref.shape[0], half), F32)
    for k, y_ref in enumerate((ya_ref, yb_ref, yc_ref, yd_ref)):
        y_hi, y_lo = _unpack_bf16_pair(y_ref[...])
        acc_hi = acc_hi + tw[:, k:k + 1] * y_hi
        acc_lo = acc_lo + tw[:, k:k + 1] * y_lo
    x_hi = x1_ref[:, :half] + g2_ref[:, :half] * acc_hi
    x_lo = x1_ref[:, half:] + g2_ref[:, half:] * acc_lo
    ms = (jnp.sum(x_hi * x_hi, axis=-1, keepdims=True) + jnp.sum(x_lo * x_lo, axis=-1, keepdims=True)) / d
    inv = lax.rsqrt(ms + EPS)
    o_ref[:, :half] = x_hi * inv * fw_ref[:, :half]
    o_ref[:, half:] = x_lo * inv * fw_ref[:, half:]


def _final(x1, y4, tw, g2, fw, rows_per_mod, tm):
    m, d = x1.shape
    return pl.pallas_call(
        _final_kernel,
        grid=(m // tm,),
        in_specs=[pl.BlockSpec((tm, d), lambda i: (i, 0))]
        + [pl.BlockSpec((None, tm, d // 2), functools.partial(lambda k, i: (k, i, 0), k)) for k in range(TOP_K)]
        + [pl.BlockSpec((tm, LANES), lambda i: (i, 0)),
           pl.BlockSpec((None, 1, d), lambda i: ((i * tm) // rows_per_mod, 0, 0)),
           pl.BlockSpec((1, d), lambda i: (0, 0))],
        out_specs=pl.BlockSpec((tm, d), lambda i: (i, 0)),
        out_shape=jax.ShapeDtypeStruct((m, d), F32),
        compiler_params=_cparams("arbitrary"),
        name="final",
    )(x1, y4, y4, y4, y4, tw, g2, fw.reshape(1, d))


def _dispatch_plan(e, rank, counts, tm):
    n_assign = e.size
    padded = (counts + tm - 1) // tm * tm
    pad_end = jnp.cumsum(padded)
    pad_start = pad_end - padded
    dest = (pad_start[e] + rank).astype(jnp.int32)
    n_tiles = n_assign // tm + N_EXPERTS
    tile_start = jnp.arange(n_tiles, dtype=jnp.int32) * tm
    tile_expert = jnp.minimum(jnp.sum(pad_end[None, :] <= tile_start[:, None], axis=1), N_EXPERTS - 1).astype(jnp.int32)
    n_used = (pad_end[-1] // tm).astype(jnp.int32).reshape(1)
    tile_valid = jnp.clip(pad_start[tile_expert] + counts[tile_expert] - tile_start, 0, tm).astype(jnp.int32)
    return dest, tile_expert, n_used, tile_valid


def kernel(x, c, ctx, c_ctx, w_mod, b_mod, norm1_w, norm2_w, w_in, conv_w, conv_b, dt_bias, a_log, d_skip,
           ssd_norm_w, w_ssd_out, w_four_out, w_o, w_router, b_router, w_gate_up, b_gate_up, w_down, b_down,
           final_norm_w):
    bsz, n, d = x.shape
    n_ctx = ctx.shape[1]
    depth = w_mod.shape[0]
    assert depth == 1, "a stacked model would also need the context stream's residual update"
    d_inner = N_HEADS * HEAD_DIM
    off_dt = d_inner + 2 * BC_WIDTH
    m = bsz * n
    x2d = x.reshape(m, d)
    tm = min(512, n)

    for layer in range(depth):
        rows = -(-(bsz + 1) // 8) * 8
        c_rows = jnp.zeros((rows, d), F32).at[:bsz].set(c).at[bsz].set(c_ctx)
        mod = _modulation(c_rows, w_mod[layer], b_mod[layer])
        mods = [mod[:bsz, i * d:(i + 1) * d].reshape(bsz, 1, d) for i in range(N_MOD)]
        sh1, sc1, g1, sh2, sc2, g2 = mods
        sh1_c = mod[bsz:bsz + 1, :d].reshape(1, 1, d)
        sc1_c = mod[bsz:bsz + 1, d:2 * d].reshape(1, 1, d)

        wl = w_in[layer]
        w_main = jnp.concatenate([wl[:, :off_dt], wl[:, off_dt + 2 * N_HEADS:]], axis=1).astype(BF16)
        w_dt = jnp.zeros((d, 2 * LANES), F32)
        w_dt = w_dt.at[:, :N_HEADS].set(wl[:, off_dt:off_dt + N_HEADS])
        w_dt = w_dt.at[:, LANES:LANES + N_HEADS].set(wl[:, off_dt + N_HEADS:off_dt + 2 * N_HEADS]).astype(BF16)
        pad = jnp.zeros((2, 1, LANES - N_HEADS), F32)
        dt_bias2 = jnp.concatenate([dt_bias[layer].reshape(2, 1, N_HEADS), pad], axis=-1)
        a_log2 = jnp.concatenate([a_log[layer].reshape(2, 1, N_HEADS), pad], axis=-1)

        w_ctx = w_main[:, :off_dt]
        pc, dtc = _in_proj(ctx.reshape(bsz * n_ctx, d), norm1_w[layer], sc1_c, sh1_c, w_ctx, w_dt,
                           bsz * n_ctx, min(512, n_ctx))
        xbc_c = _conv_silu(pc.reshape(bsz, n_ctx, off_dt), conv_w[layer][:, :off_dt], conv_b[layer][:off_dt], off_dt)
        h0 = jnp.zeros((bsz, 2, N_GROUPS, D_STATE, HEADS_PER_GROUP * HEAD_DIM), F32)
        h_ctx = _ssd(xbc_c, dtc.reshape(bsz, n_ctx, 2 * LANES), dt_bias2, a_log2, h0, with_y=False)

        p, dtl = _in_proj(x2d, norm1_w[layer], sc1, sh1, w_main, w_dt, n, tm)
        n_conv = off_dt + 2 * BC_WIDTH
        xbc = _conv_silu(p.reshape(bsz, n, -1), conv_w[layer], conv_b[layer], n_conv)
        y2 = _ssd(xbc, dtl.reshape(bsz, n, 2 * LANES), dt_bias2, a_log2, h_ctx, with_y=True)
        ff = _fnet_mix(p, (n_conv + d_inner) // d, bsz, n)

        dsk = jnp.repeat(d_skip[layer].astype(F32), HEAD_DIM).reshape(1, d_inner)
        wr = jnp.zeros((d, LANES), F32).at[:, :N_EXPERTS].set(w_router[layer])
        br = jnp.full((1, LANES), NEG_BIG, F32).at[0, :N_EXPERTS].set(b_router[layer])
        x1, h2, ti, tw, rk, cnt = _tail(
            y2.reshape(2, m, d_inner), xbc.reshape(m, n_conv), p, ff, x2d, g1, sc2, sh2, dsk,
            ssd_norm_w[layer].reshape(1, d_inner), w_ssd_out[layer].astype(BF16), w_four_out[layer].astype(BF16),
            w_o[layer].astype(BF16), norm2_w[layer].reshape(1, d), wr, br, n, min(TAIL_ROWS, n))

        counts = cnt[0, :N_EXPERTS].astype(jnp.int32)
        dest, tile_expert, n_used, tile_valid = _dispatch_plan(ti[:, :TOP_K], rk[:, :TOP_K], counts, MOE_TILE)
        dest_km = dest.T.reshape(-1)
        xs = _scatter_rows(h2, dest_km, tile_expert.shape[0] * MOE_TILE)
        ys = _experts(xs, tile_expert, n_used, tile_valid, w_gate_up[layer],
                      b_gate_up[layer].reshape(N_EXPERTS, 1, -1), w_down[layer],
                      b_down[layer].reshape(N_EXPERTS, 1, -1))
        y4 = _gather_rows(ys, dest_km).reshape(TOP_K, m, d // 2)
        x2d = _final(x1, y4, tw, g2, final_norm_w, n, tm)
    return x2d.reshape(bsz, n, d)
```

```python
import functools
import math

import numpy as np
import jax
import jax.numpy as jnp
from jax import lax
from jax.experimental import pallas as pl
from jax.experimental.pallas import tpu as pltpu
from jax.experimental.pallas import tpu_sc as plsc

F32 = jnp.float32
BF16 = jnp.bfloat16

EPS = 1e-6
GRID_W = 64
N_MOD = 6
F_GROUPS = 8
F_GROUP_DIM = 128
HEAD_DIM = 64
N_HEADS = 32
N_GROUPS = 4
HEADS_PER_GROUP = N_HEADS // N_GROUPS
D_STATE = 128
BC_WIDTH = N_GROUPS * D_STATE
CONV_K = 5
CHUNK = 128
N_EXPERTS = 32
TOP_K = 4
SWIGLU_LIMIT = 7.0
SWIGLU_ALPHA = 1.702

LANES = 128
VMEM_LIMIT_BYTES = 56 * 1024 * 1024
NEG_BIG = -1e30
MOE_TILE = 512
TAIL_ROWS = 512
SC_GATHER_WINDOW = 64


def _cparams(*sem):
    return pltpu.CompilerParams(dimension_semantics=sem, vmem_limit_bytes=VMEM_LIMIT_BYTES)


def _sigmoid(v):
    return 1.0 / (1.0 + jnp.exp(-v))


def _silu(v):
    return v * _sigmoid(v)


def _softplus(v):
    return jnp.maximum(v, 0.0) + jnp.log(1.0 + jnp.exp(-jnp.abs(v)))


def _pack_bf16_pair(v):
    w = v.shape[1] // 2
    bits = lax.bitcast_convert_type(v.astype(BF16).astype(F32), jnp.uint32)
    return bits[:, :w] | (bits[:, w:] >> 16)


def _unpack_bf16_pair(p):
    hi = lax.bitcast_convert_type(p & jnp.uint32(0xFFFF0000), F32)
    lo = lax.bitcast_convert_type(p << 16, F32)
    return hi, lo


def _mod_kernel(c_ref, w_ref, b_ref, o_ref):
    s = _silu(c_ref[...]).astype(BF16)
    o_ref[...] = jnp.dot(s, w_ref[...].astype(BF16), preferred_element_type=F32) + b_ref[...]


def _modulation(c_rows, w_mod, b_mod):
    rows, d = c_rows.shape
    n_out = w_mod.shape[1]
    tn = 1024
    return pl.pallas_call(
        _mod_kernel,
        grid=(n_out // tn,),
        in_specs=[pl.BlockSpec((rows, d), lambda j: (0, 0)),
                  pl.BlockSpec((d, tn), lambda j: (0, j)),
                  pl.BlockSpec((1, tn), lambda j: (0, j))],
        out_specs=pl.BlockSpec((rows, tn), lambda j: (0, j)),
        out_shape=jax.ShapeDtypeStruct((rows, n_out), F32),
        compiler_params=_cparams("arbitrary"),
        name="modulation",
    )(c_rows, w_mod, b_mod.reshape(1, n_out))


_INPROJ_TN = 1024


def _inproj_kernel(x_ref, nw_ref, sc_ref, sh_ref, w_ref, wdt_ref, p_ref, dt_ref):
    x = x_ref[...]
    inv = lax.rsqrt(jnp.mean(x * x, axis=-1, keepdims=True) + EPS)
    h = (x * inv * nw_ref[...]) * (1.0 + sc_ref[...]) + sh_ref[...]
    hb = h.astype(BF16)
    dt_ref[...] = jnp.dot(hb, wdt_ref[...], preferred_element_type=F32)
    for j in range(w_ref.shape[1] // _INPROJ_TN):
        cols = slice(j * _INPROJ_TN, (j + 1) * _INPROJ_TN)
        p_ref[:, cols] = jnp.dot(hb, w_ref[:, cols], preferred_element_type=F32).astype(p_ref.dtype)


def _in_proj(x2d, norm_w, sc, sh, w_main, w_dt, rows_per_mod, tm):
    m, d = x2d.shape
    n = w_main.shape[1]
    ndt = w_dt.shape[1]
    resident = dict(pipeline_mode=pl.Buffered(1))
    return pl.pallas_call(
        _inproj_kernel,
        grid=(m // tm,),
        in_specs=[pl.BlockSpec((tm, d), lambda i: (i, 0)),
                  pl.BlockSpec((1, d), lambda i: (0, 0), **resident),
                  pl.BlockSpec((None, 1, d), lambda i: ((i * tm) // rows_per_mod, 0, 0)),
                  pl.BlockSpec((None, 1, d), lambda i: ((i * tm) // rows_per_mod, 0, 0)),
                  pl.BlockSpec((d, n), lambda i: (0, 0), **resident),
                  pl.BlockSpec((d, ndt), lambda i: (0, 0), **resident)],
        out_specs=[pl.BlockSpec((tm, n), lambda i: (i, 0)),
                   pl.BlockSpec((tm, ndt), lambda i: (i, 0))],
        out_shape=[jax.ShapeDtypeStruct((m, n), BF16),
                   jax.ShapeDtypeStruct((m, ndt), F32)],
        compiler_params=_cparams("arbitrary"),
        name="in_proj",
    )(x2d, norm_w.reshape(1, d), sc, sh, w_main, w_dt)


_CONV_HALO = 16


_CONV_ROWS = 64


def _conv_shift_matrix(rc):
    win = rc + 2 * _CONV_HALO
    s = np.zeros((rc, CONV_K * win), np.float32)
    for k in range(CONV_K):
        for l in range(rc):
            s[l, k * win + _CONV_HALO + l + k - CONV_K // 2] = 1.0
    return jnp.asarray(s, BF16)


def _conv_kernel(p_ref, w_ref, b_ref, s_ref, o_ref, *, n, rc):
    tc = o_ref.shape[-1]
    w = w_ref[...].astype(BF16)
    bias = b_ref[...]
    smat = s_ref[...]
    zeros = jnp.zeros((_CONV_HALO, tc), BF16)
    for r0 in range(0, n, rc):
        top = zeros if r0 == 0 else p_ref[r0 - _CONV_HALO:r0, :]
        bot = zeros if r0 + rc >= n else p_ref[r0 + rc:r0 + rc + _CONV_HALO, :]
        window = jnp.concatenate([top, p_ref[r0:r0 + rc, :], bot], axis=0)
        taps = jnp.concatenate([window * w[k:k + 1, :] for k in range(CONV_K)], axis=0)
        acc = jnp.dot(smat, taps, preferred_element_type=F32) + bias
        o_ref[r0:r0 + rc, :] = _silu(acc).astype(o_ref.dtype)


def _conv_silu(p3d, conv_w, conv_b, n_ch):
    bsz, n, _ = p3d.shape
    tc = 512
    rc = min(_CONV_ROWS, n)
    smat = _conv_shift_matrix(rc)
    return pl.pallas_call(
        functools.partial(_conv_kernel, n=n, rc=rc),
        grid=(bsz, n_ch // tc),
        in_specs=[pl.BlockSpec((None, n, tc), lambda b, j: (b, 0, j)),
                  pl.BlockSpec((CONV_K, tc), lambda b, j: (0, j)),
                  pl.BlockSpec((1, tc), lambda b, j: (0, j)),
                  pl.BlockSpec(smat.shape, lambda b, j: (0, 0))],
        out_specs=pl.BlockSpec((None, n, tc), lambda b, j: (b, 0, j)),
        out_shape=jax.ShapeDtypeStruct((bsz, n, n_ch), BF16),
        compiler_params=_cparams("arbitrary", "arbitrary"),
        name="conv_silu",
    )(p3d, conv_w, conv_b.reshape(1, -1), smat)


LOG2E = 1.4426950408889634
SSD_SUB = 4


def _head_expand_matrix():
    e = np.zeros((LANES, N_HEADS * HEAD_DIM), np.float32)
    for h in range(N_HEADS):
        e[h, h * HEAD_DIM:(h + 1) * HEAD_DIM] = 1.0
    return jnp.asarray(e, BF16)


def _ssd_chunk_terms(dt_raw, dt_bias, a_log, expand, tri, fwd):
    q = CHUNK
    dt = _softplus(dt_raw + dt_bias)
    da = dt * (-jnp.exp(a_log))
    hi = da.astype(BF16)
    r1 = da - hi.astype(F32)
    mid = r1.astype(BF16)
    lo = (r1 - mid.astype(F32)).astype(BF16)
    ones_tri = jnp.where(tri, 1.0, 0.0).astype(BF16)
    cs3 = jnp.dot(ones_tri, jnp.concatenate([hi, mid, lo], axis=1), preferred_element_type=F32)
    cs = cs3[:, :LANES] + cs3[:, LANES:2 * LANES] + cs3[:, 2 * LANES:]
    tot = jnp.where(fwd, cs[q - 1:q, :], cs[0:1, :])
    dte = dt * jnp.exp(tot - cs)
    dec = jnp.exp(tot)
    dec_hi = dec.astype(BF16)
    dec_lo = (dec - dec_hi.astype(F32)).astype(BF16)
    dec2 = jnp.concatenate([dec_hi, dec_lo, jnp.zeros((6, LANES), BF16)], axis=0)
    stacked = jnp.concatenate([jnp.exp(cs).astype(BF16), dte.astype(BF16), dec2], axis=0)
    spread = jnp.dot(stacked, expand, preferred_element_type=F32)
    ecs_e = spread[:q]
    dte_e = spread[q:2 * q].astype(BF16)
    dec_e = jnp.broadcast_to(spread[2 * q:2 * q + 1] + spread[2 * q + 1:2 * q + 2], (8, spread.shape[1]))
    cs2 = cs * LOG2E
    return cs2, cs2.T, dt.T, ecs_e, dte_e, dec_e


def _ssd_kernel(*refs, with_y, nc, n_sub):
    if with_y:
        (x_ref, b_ref, c_ref, dt_ref, dtn_ref, dtb_ref, alog_ref, exp_ref, h0_ref, y_ref,
         s_scr, cs_scr, dee_scr, dce_scr, ecs_scr) = refs
    else:
        (x_ref, b_ref, dt_ref, dtn_ref, dtb_ref, alog_ref, exp_ref, h0_ref, hout_ref,
         s_scr, cs_scr, dee_scr, dce_scr) = refs
    q = CHUNK
    gw = HEADS_PER_GROUP * HEAD_DIM
    d = pl.program_id(1)
    c = pl.program_id(2)
    row = lax.broadcasted_iota(jnp.int32, (q, q), 0)
    col = lax.broadcasted_iota(jnp.int32, (q, q), 1)
    fwd = d == 0
    tri = jnp.where(fwd, row - col, col - row) >= 0
    first_half = col < HEAD_DIM

    def store_terms(slot, terms):
        cs2, cs2_t, dt_t, ecs_e, dte_e, dec_e = terms
        cs_scr[slot, 0] = cs2
        cs_scr[slot, 1] = cs2_t
        cs_scr[slot, 2] = dt_t
        dee_scr[slot] = dte_e
        dce_scr[slot] = dec_e
        if with_y:
            ecs_scr[slot] = ecs_e

    def terms_of(ref, sub):
        rows = pl.ds(pl.multiple_of(sub * q, q), q)
        return _ssd_chunk_terms(ref[rows, :], dtb_ref[...], alog_ref[...], exp_ref[...], tri, fwd)

    def sub_of(j):
        return jnp.where(fwd, j, n_sub - 1 - j)

    @pl.when(c == 0)
    def _():
        s_scr[...] = h0_ref[...]
        store_terms(0, terms_of(dt_ref, sub_of(0)))

    for j in range(n_sub):
        slot = j % 2
        rows = pl.ds(pl.multiple_of(sub_of(j) * q, q), q)
        cs2 = cs_scr[slot, 0]
        cs2_t = cs_scr[slot, 1]
        dt_t = cs_scr[slot, 2]
        if j + 1 < n_sub:
            store_terms(1 - slot, terms_of(dt_ref, sub_of(j + 1)))
        else:
            store_terms(1 - slot, terms_of(dtn_ref, sub_of(0)))

        for g in range(N_GROUPS):
            cols = slice(g * gw, (g + 1) * gw)
            bg = b_ref[rows, g * D_STATE:(g + 1) * D_STATE]
            xg = x_ref[rows, cols]
            s_old = s_scr[g]
            if with_y:
                cg = c_ref[rows, g * D_STATE:(g + 1) * D_STATE]
                cb = lax.dot_general(cg, bg, (((1,), (1,)), ((), ())), preferred_element_type=F32)
                y_off = jnp.dot(cg, s_old.astype(BF16), preferred_element_type=F32)
                for k in range(HEADS_PER_GROUP // 2):
                    pair = g * (HEADS_PER_GROUP // 2) + k
                    lanes = slice(pair * LANES, (pair + 1) * LANES)
                    xp = xg[:, k * LANES:(k + 1) * LANES]
                    xz = jnp.zeros_like(xp)
                    lhs = []
                    for hh in (2 * pair, 2 * pair + 1):
                        colb = jnp.broadcast_to(cs2[:, hh:hh + 1], (q, q))
                        rowb = cs2_t[hh:hh + 1, :]
                        decay = jnp.exp2(jnp.where(tri, colb - rowb, -jnp.inf))
                        lhs.append((cb * decay * dt_t[hh:hh + 1, :]).astype(BF16))
                    y_diag = jnp.dot(jnp.concatenate(lhs, axis=1),
                                     jnp.concatenate([jnp.where(first_half, xp, xz), jnp.where(first_half, xz, xp)],
                                                     axis=0),
                                     preferred_element_type=F32)
                    y = y_diag + ecs_scr[slot, :, lanes] * y_off[:, k * LANES:(k + 1) * LANES]
                    y_ref[rows, lanes] = y.astype(y_ref.dtype)
            xdte = xg * dee_scr[slot, :, cols]
            upd = lax.dot_general(bg, xdte, (((0,), (0,)), ((), ())), preferred_element_type=F32)
            s_scr[g] = s_old * dce_scr[slot, 0:1, cols] + upd

    if not with_y:
        @pl.when(c == nc - 1)
        def _():
            hout_ref[...] = s_scr[...]


def _ssd(xbc, dt_raw, dt_bias2, a_log2, h0, with_y):
    bsz, n, _ = xbc.shape
    n_sub = min(SSD_SUB, n // CHUNK)
    assert n_sub % 2 == 0 and n % (n_sub * CHUNK) == 0
    blk = n_sub * CHUNK
    nc = n // blk
    d_inner = N_HEADS * HEAD_DIM
    gw = HEADS_PER_GROUP * HEAD_DIM
    x_blk = d_inner // BC_WIDTH

    def ceff(dd, cc):
        return cc + dd * (nc - 1 - 2 * cc)

    def cnext(dd, cc):
        return ceff(dd, jnp.minimum(cc + 1, nc - 1))

    in_specs = [pl.BlockSpec((None, blk, d_inner), lambda b, dd, cc: (b, ceff(dd, cc), 0)),
                pl.BlockSpec((None, blk, BC_WIDTH), lambda b, dd, cc: (b, ceff(dd, cc), x_blk + dd))]
    args = [xbc, xbc]
    if with_y:
        in_specs.append(pl.BlockSpec((None, blk, BC_WIDTH), lambda b, dd, cc: (b, ceff(dd, cc), x_blk + 2 + dd)))
        args.append(xbc)
    in_specs += [pl.BlockSpec((None, blk, LANES), lambda b, dd, cc: (b, ceff(dd, cc), dd)),
                 pl.BlockSpec((None, blk, LANES), lambda b, dd, cc: (b, cnext(dd, cc), dd)),
                 pl.BlockSpec((None, 1, LANES), lambda b, dd, cc: (dd, 0, 0)),
                 pl.BlockSpec((None, 1, LANES), lambda b, dd, cc: (dd, 0, 0)),
                 pl.BlockSpec((LANES, d_inner), lambda b, dd, cc: (0, 0)),
                 pl.BlockSpec((None, None, N_GROUPS, D_STATE, gw), lambda b, dd, cc: (b, dd, 0, 0, 0))]
    args += [dt_raw, dt_raw, dt_bias2, a_log2, _head_expand_matrix(), h0]
    scratch = [pltpu.VMEM((N_GROUPS, D_STATE, gw), F32),
               pltpu.VMEM((2, 3, CHUNK, LANES), F32),
               pltpu.VMEM((2, CHUNK, d_inner), BF16),
               pltpu.VMEM((2, 8, d_inner), F32)]
    if with_y:
        out_specs = pl.BlockSpec((None, None, blk, d_inner), lambda b, dd, cc: (dd, b, ceff(dd, cc), 0))
        out_shape = jax.ShapeDtypeStruct((2, bsz, n, d_inner), BF16)
        scratch += [pltpu.VMEM((2, CHUNK, d_inner), F32)]
    else:
        out_specs = pl.BlockSpec((None, None, N_GROUPS, D_STATE, gw), lambda b, dd, cc: (b, dd, 0, 0, 0))
        out_shape = jax.ShapeDtypeStruct((bsz, 2, N_GROUPS, D_STATE, gw), F32)
    return pl.pallas_call(
        functools.partial(_ssd_kernel, with_y=with_y, nc=nc, n_sub=n_sub),
        grid=(bsz, 2, nc),
        in_specs=in_specs,
        out_specs=out_specs,
        out_shape=out_shape,
        scratch_shapes=scratch,
        compiler_params=_cparams("arbitrary", "arbitrary", "arbitrary"),
        name="ssd_scan" if with_y else "ssd_ctx_state",
    )(*args)


def _dft_mats(n):
    k = np.arange(n)
    ang = 2.0 * np.pi * ((k[:, None] * k[None, :]) % n) / n
    return np.cos(ang), np.sin(ang)


def _fnet_chan_kernel(u_ref, m_ref, o_ref):
    m = m_ref[...]
    for g in range(F_GROUPS):
        ug = u_ref[:, g * F_GROUP_DIM:(g + 1) * F_GROUP_DIM]
        pq = jnp.dot(ug, m, preferred_element_type=F32)
        o_ref[0, :, g * F_GROUP_DIM:(g + 1) * F_GROUP_DIM] = pq[:, :F_GROUP_DIM].astype(o_ref.dtype)
        o_ref[1, :, g * F_GROUP_DIM:(g + 1) * F_GROUP_DIM] = pq[:, F_GROUP_DIM:].astype(o_ref.dtype)


def _stage_lane_blocks(src_ref, scr, rows):
    width = src_ref.shape[-1]
    for t in range(2):
        v = src_ref[t].reshape(rows, width).astype(F32)
        for lb in range(width // LANES):
            scr[t, lb] = v[:, lb * LANES:(lb + 1) * LANES]


def _strided_rows(scr, t, start, count, stride, width):
    return jnp.concatenate([scr[t, lb, pl.ds(start, count, stride=stride), :] for lb in range(width // LANES)],
                           axis=1)


def _fnet_stage1_kernel(z_ref, m1_ref, twc_ref, tws_ref, o_ref, zs_scr, *, tcs):
    r = z_ref.shape[1]
    width = z_ref.shape[-1]
    _stage_lane_blocks(z_ref, zs_scr, r * tcs)
    m1 = m1_ref[...]
    for j in range(tcs):
        zz = jnp.concatenate([_strided_rows(zs_scr, 0, j, r, tcs, width),
                              _strided_rows(zs_scr, 1, j, r, tcs, width)], axis=0).astype(BF16)
        u = jnp.dot(m1, zz, preferred_element_type=F32)
        ur = u[:r]
        ui = u[r:]
        tc = twc_ref[j]
        ts = tws_ref[j]
        o_ref[0, j] = (ur * tc + ui * ts).astype(o_ref.dtype)
        o_ref[1, j] = (ui * tc - ur * ts).astype(o_ref.dtype)


def _fnet_stage2_kernel(u_ref, m2_ref, o_ref, us_scr, os_scr, *, tbs):
    c = u_ref.shape[1]
    width = u_ref.shape[-1]
    _stage_lane_blocks(u_ref, us_scr, c * tbs)
    m2 = m2_ref[...]
    for j in range(tbs):
        uu = jnp.concatenate([_strided_rows(us_scr, 0, j, c, tbs, width),
                              _strided_rows(us_scr, 1, j, c, tbs, width)], axis=0).astype(BF16)
        x = jnp.dot(m2, uu, preferred_element_type=F32)
        for lb in range(width // LANES):
            os_scr[lb, pl.ds(j, c, stride=tbs), :] = x[:, lb * LANES:(lb + 1) * LANES]
    out = jnp.concatenate([os_scr[lb] for lb in range(width // LANES)], axis=1)
    o_ref[...] = out.reshape(c, tbs, width).astype(o_ref.dtype)


def _fnet_mix(p2d, col_blk, bsz, n):
    m = p2d.shape[0]
    width = F_GROUPS * F_GROUP_DIM
    cgrid = GRID_W
    rgrid = n // cgrid
    scale = 1.0 / math.sqrt(n * F_GROUP_DIM)

    cc, sc = _dft_mats(F_GROUP_DIM)
    chan = jnp.asarray(np.concatenate([cc, sc], axis=1) * scale, BF16)
    tm = 512
    pq = pl.pallas_call(
        _fnet_chan_kernel,
        grid=(m // tm,),
        in_specs=[pl.BlockSpec((tm, width), lambda i: (i, col_blk)),
                  pl.BlockSpec((F_GROUP_DIM, 2 * F_GROUP_DIM), lambda i: (0, 0))],
        out_specs=pl.BlockSpec((2, tm, width), lambda i: (0, i, 0)),
        out_shape=jax.ShapeDtypeStruct((2, m, width), BF16),
        compiler_params=_cparams("arbitrary"),
        name="fnet_chan",
    )(p2d, chan)

    cr, sr = _dft_mats(rgrid)
    m1 = jnp.asarray(np.block([[cr, -sr], [-sr, -cr]]), BF16)
    bb = np.arange(rgrid)[None, :]
    ci = np.arange(cgrid)[:, None]
    ang = 2.0 * np.pi * ((ci * bb) % n) / n
    twc = jnp.asarray(np.cos(ang)[:, :, None], F32)
    tws = jnp.asarray(np.sin(ang)[:, :, None], F32)
    tcs = 16
    z5 = pq.reshape(2, bsz, rgrid, cgrid, width)
    u5 = pl.pallas_call(
        functools.partial(_fnet_stage1_kernel, tcs=tcs),
        grid=(bsz, cgrid // tcs),
        in_specs=[pl.BlockSpec((2, None, rgrid, tcs, width), lambda b, j: (0, b, 0, j, 0)),
                  pl.BlockSpec((2 * rgrid, 2 * rgrid), lambda b, j: (0, 0)),
                  pl.BlockSpec((tcs, rgrid, 1), lambda b, j: (j, 0, 0)),
                  pl.BlockSpec((tcs, rgrid, 1), lambda b, j: (j, 0, 0))],
        out_specs=pl.BlockSpec((2, None, tcs, rgrid, width), lambda b, j: (0, b, j, 0, 0)),
        out_shape=jax.ShapeDtypeStruct((2, bsz, cgrid, rgrid, width), BF16),
        scratch_shapes=[pltpu.VMEM((2, width // LANES, rgrid * tcs, LANES), F32)],
        compiler_params=_cparams("arbitrary", "arbitrary"),
        name="fnet_stage1",
    )(z5, m1, twc, tws)

    cc2, sc2 = _dft_mats(cgrid)
    m2 = jnp.asarray(np.concatenate([cc2, sc2], axis=1), BF16)
    tbs = min(16, rgrid)
    out = pl.pallas_call(
        functools.partial(_fnet_stage2_kernel, tbs=tbs),
        grid=(bsz, rgrid // tbs),
        in_specs=[pl.BlockSpec((2, None, cgrid, tbs, width), lambda b, j: (0, b, 0, j, 0)),
                  pl.BlockSpec((cgrid, 2 * cgrid), lambda b, j: (0, 0))],
        out_specs=pl.BlockSpec((None, cgrid, tbs, width), lambda b, j: (b, 0, j, 0)),
        out_shape=jax.ShapeDtypeStruct((bsz, cgrid, rgrid, width), BF16),
        scratch_shapes=[pltpu.VMEM((2, width // LANES, cgrid * tbs, LANES), F32),
                        pltpu.VMEM((width // LANES, cgrid * tbs, LANES), F32)],
        compiler_params=_cparams("arbitrary", "arbitrary"),
        name="fnet_stage2",
    )(u5, m2)
    return out.reshape(m, width)


def _tail_kernel(yf_ref, yb_ref, xs_ref, z_ref, ff_ref, gf_ref, gs_ref, x_ref, g1_ref, sc2_ref, sh2_ref,
                 dsk_ref, snw_ref, wso_ref, wfo_ref, wo_ref, n2w_ref, wr_ref, br_ref,
                 x1_ref, h2_ref, ti_ref, tw_ref, rk_ref, cnt_ref, carry_scr):
    y = (yf_ref[0] + yb_ref[0]).astype(F32) + dsk_ref[...] * xs_ref[...].astype(F32)
    g = y * _silu(z_ref[...]).astype(F32)
    inv = lax.rsqrt(jnp.mean(g * g, axis=-1, keepdims=True) + EPS)
    gn = (g * inv * snw_ref[...]).astype(BF16)
    y_ssd = jnp.dot(gn, wso_ref[...], preferred_element_type=F32)
    y_four = jnp.dot(ff_ref[...], wfo_ref[...], preferred_element_type=F32)
    t = _sigmoid(gf_ref[...]) * y_four.astype(BF16) + _sigmoid(gs_ref[...]) * y_ssd.astype(BF16)
    mix = jnp.dot(t, wo_ref[...], preferred_element_type=F32)
    x1 = x_ref[...] + g1_ref[...] * mix
    x1_ref[...] = x1
    inv2 = lax.rsqrt(jnp.mean(x1 * x1, axis=-1, keepdims=True) + EPS)
    h2 = (x1 * inv2 * n2w_ref[...]) * (1.0 + sc2_ref[...]) + sh2_ref[...]
    h2_ref[...] = _pack_bf16_pair(h2)
    h_hi = h2.astype(BF16)
    h_lo = (h2 - h_hi.astype(F32)).astype(BF16)
    wr = wr_ref[...]
    w_hi = wr.astype(BF16)
    w_lo = (wr - w_hi.astype(F32)).astype(BF16)
    logits = (jnp.dot(h_hi, w_hi, preferred_element_type=F32)
              + jnp.dot(h_hi, w_lo, preferred_element_type=F32)
              + jnp.dot(h_lo, w_hi, preferred_element_type=F32)) + br_ref[...]
    tm = logits.shape[0]
    lane = lax.broadcasted_iota(jnp.int32, (tm, LANES), 1)
    vals, idxs = [], []
    cur = logits
    for _ in range(TOP_K):
        mx = jnp.max(cur, axis=-1, keepdims=True)
        ix = jnp.min(jnp.where(cur == mx, lane, LANES), axis=-1, keepdims=True)
        vals.append(mx)
        idxs.append(ix)
        cur = jnp.where(lane == ix, NEG_BIG * 2.0, cur)
    es = [jnp.exp(v - vals[0]) for v in vals]
    den = es[0] + es[1] + es[2] + es[3]
    ti = jnp.zeros((tm, LANES), jnp.int32)
    tw = jnp.zeros((tm, LANES), F32)
    for k in range(TOP_K):
        ti = jnp.where(lane == k, idxs[k], ti)
        tw = jnp.where(lane == k, es[k] / den, tw)
    ti_ref[...] = ti
    tw_ref[...] = tw

    @pl.when(pl.program_id(0) == 0)
    def _():
        carry_scr[...] = jnp.zeros(carry_scr.shape, F32)

    onehots = [jnp.where(lane == ix, 1.0, 0.0) for ix in idxs]
    cnt = onehots[0] + onehots[1] + onehots[2] + onehots[3]
    r_i = lax.broadcasted_iota(jnp.int32, (tm, tm), 0)
    c_i = lax.broadcasted_iota(jnp.int32, (tm, tm), 1)
    earlier = jnp.where(r_i > c_i, 1.0, 0.0).astype(BF16)
    before = jnp.dot(earlier, cnt.astype(BF16), preferred_element_type=F32) + carry_scr[0:1, :]
    rk = jnp.zeros((tm, LANES), jnp.int32)
    for k in range(TOP_K):
        rank_k = jnp.sum(onehots[k] * before, axis=-1, keepdims=True)
        rk = jnp.where(lane == k, rank_k.astype(jnp.int32), rk)
    rk_ref[...] = rk
    total = carry_scr[...] + jnp.sum(cnt, axis=0, keepdims=True)
    carry_scr[...] = total
    cnt_ref[...] = total


def _tail(y2, xbc2d, p2d, ff, x2d, g1, sc2, sh2, dsk, snw, wso, wfo, wo, n2w, wr, br, rows_per_mod, tm):
    m, d = x2d.shape
    di = y2.shape[-1]
    zb = (2 * di) // di
    fb = (3 * di) // d
    row = lambda i: (i, 0)
    modrow = lambda i: ((i * tm) // rows_per_mod, 0, 0)
    const = lambda i: (0, 0)
    in_specs = [pl.BlockSpec((1, tm, di), lambda i: (0, i, 0)),
                pl.BlockSpec((1, tm, di), lambda i: (1, i, 0)),
                pl.BlockSpec((tm, di), row),
                pl.BlockSpec((tm, di), lambda i: (i, zb)),
                pl.BlockSpec((tm, d), row),
                pl.BlockSpec((tm, d), lambda i: (i, fb + 1)),
                pl.BlockSpec((tm, d), lambda i: (i, fb + 2)),
                pl.BlockSpec((tm, d), row),
                pl.BlockSpec((None, 1, d), modrow),
                pl.BlockSpec((None, 1, d), modrow),
                pl.BlockSpec((None, 1, d), modrow),
                pl.BlockSpec((1, di), const),
                pl.BlockSpec((1, di), const),
                pl.BlockSpec((di, d), const),
                pl.BlockSpec((d, d), const),
                pl.BlockSpec((d, d), const),
                pl.BlockSpec((1, d), const),
                pl.BlockSpec((d, LANES), const),
                pl.BlockSpec((1, LANES), const)]
    out_specs = [pl.BlockSpec((tm, d), row), pl.BlockSpec((tm, d // 2), row),
                 pl.BlockSpec((tm, LANES), row), pl.BlockSpec((tm, LANES), row), pl.BlockSpec((tm, LANES), row),
                 pl.BlockSpec((8, LANES), const)]
    out_shape = [jax.ShapeDtypeStruct((m, d), F32), jax.ShapeDtypeStruct((m, d // 2), jnp.uint32),
                 jax.ShapeDtypeStruct((m, LANES), jnp.int32), jax.ShapeDtypeStruct((m, LANES), F32),
                 jax.ShapeDtypeStruct((m, LANES), jnp.int32), jax.ShapeDtypeStruct((8, LANES), F32)]
    return pl.pallas_call(
        _tail_kernel,
        grid=(m // tm,),
        in_specs=in_specs,
        out_specs=out_specs,
        out_shape=out_shape,
        scratch_shapes=[pltpu.VMEM((8, LANES), F32)],
        compiler_params=_cparams("arbitrary"),
        name="tail",
    )(y2, y2, xbc2d, p2d, ff, p2d, p2d, x2d, g1, sc2, sh2, dsk, snw, wso, wfo, wo, n2w, wr, br)


def _gather_rows(src, idx):
    n_out = idx.shape[0]
    width = src.shape[1]
    win = SC_GATHER_WINDOW
    info = plsc.get_sparse_core_info()
    n_cores, n_workers = info.num_cores, info.num_cores * info.num_subcores
    per_worker = n_out // n_workers
    assert per_worker * n_workers == n_out and per_worker % (2 * win) == 0
    mesh = plsc.VectorSubcoreMesh(core_axis_name="core", subcore_axis_name="subcore")

    @functools.partial(
        pl.kernel, out_type=jax.ShapeDtypeStruct((n_out, width), src.dtype), mesh=mesh,
        scratch_types=[pltpu.VMEM((win,), jnp.int32), pltpu.VMEM((win,), jnp.int32),
                       pltpu.VMEM((win, width), src.dtype), pltpu.VMEM((win, width), src.dtype),
                       pltpu.SemaphoreType.DMA, pltpu.SemaphoreType.DMA,
                       pltpu.SemaphoreType.DMA, pltpu.SemaphoreType.DMA],
        name="gather_rows")
    def gather(src_hbm, idx_hbm, out_hbm, idx_a, idx_b, rows_a, rows_b, gsem_a, gsem_b, ssem_a, ssem_b):
        worker = lax.axis_index("subcore") * n_cores + lax.axis_index("core")
        base = worker * per_worker

        @pl.loop(0, per_worker, step=2 * win)
        def _(off):
            pltpu.sync_copy(idx_hbm.at[pl.ds(base + off, win)], idx_a)
            ga = pltpu.async_copy(src_hbm.at[idx_a], rows_a, gsem_a)
            pltpu.sync_copy(idx_hbm.at[pl.ds(base + off + win, win)], idx_b)
            gb = pltpu.async_copy(src_hbm.at[idx_b], rows_b, gsem_b)
            ga.wait()
            sa = pltpu.async_copy(rows_a, out_hbm.at[pl.ds(base + off, win)], ssem_a)
            gb.wait()
            sb = pltpu.async_copy(rows_b, out_hbm.at[pl.ds(base + off + win, win)], ssem_b)
            sa.wait()
            sb.wait()

    return gather(src, idx)


def _scatter_rows(src, dest, n_rows):
    m, width = src.shape
    win = SC_GATHER_WINDOW
    info = plsc.get_sparse_core_info()
    n_cores, n_workers = info.num_cores, info.num_cores * info.num_subcores
    per_worker = m // n_workers
    assert per_worker * n_workers == m and per_worker % win == 0
    mesh = plsc.VectorSubcoreMesh(core_axis_name="core", subcore_axis_name="subcore")

    @functools.partial(
        pl.kernel, out_type=jax.ShapeDtypeStruct((n_rows, width), src.dtype), mesh=mesh,
        scratch_types=[pltpu.VMEM((win, width), src.dtype)]
        + [pltpu.VMEM((win,), jnp.int32)] * TOP_K + [pltpu.SemaphoreType.DMA] * TOP_K,
        name="scatter_rows")
    def scatter(src_hbm, dest_hbm, out_hbm, rows_v, *rest):
        idx_v, sems = rest[:TOP_K], rest[TOP_K:]
        worker = lax.axis_index("subcore") * n_cores + lax.axis_index("core")
        base = worker * per_worker

        @pl.loop(0, per_worker, step=win)
        def _(off):
            t0 = base + off
            pltpu.sync_copy(src_hbm.at[pl.ds(t0, win)], rows_v)
            for k in range(TOP_K):
                pltpu.sync_copy(dest_hbm.at[pl.ds(k * m + t0, win)], idx_v[k])
            copies = [pltpu.async_copy(rows_v, out_hbm.at[idx_v[k]], sems[k]) for k in range(TOP_K)]
            for cp in copies:
                cp.wait()

    return scatter(src, dest)


def _expert_kernel(te_ref, nu_ref, tv_ref, x_ref, wgu_ref, bgu_ref, wd_ref, bd_ref, o_ref, wgu_scr, wd_scr):
    i = pl.program_id(0)
    used = i < nu_ref[0]
    new_expert = (i == 0) | (te_ref[i] != te_ref[jnp.maximum(i - 1, 0)])

    @pl.when(used & new_expert)
    def _():
        wgu_scr[...] = wgu_ref[...].astype(BF16)
        wd_scr[...] = wd_ref[...].astype(BF16)

    @pl.when(used)
    def _():
        dff = wd_ref.shape[0]
        half = wgu_ref.shape[0] // 2
        rows = lax.broadcasted_iota(jnp.int32, x_ref.shape, 0)
        xa, xb = _unpack_bf16_pair(jnp.where(rows < tv_ref[i], x_ref[...], jnp.uint32(0)))
        gu = (jnp.dot(xa.astype(BF16), wgu_scr[:half, :], preferred_element_type=F32)
              + jnp.dot(xb.astype(BF16), wgu_scr[half:, :], preferred_element_type=F32)) + bgu_ref[...]
        gate = jnp.minimum(gu[:, :dff], SWIGLU_LIMIT)
        up = jnp.clip(gu[:, dff:], -SWIGLU_LIMIT, SWIGLU_LIMIT)
        act = (up + 1.0) * gate * _sigmoid(SWIGLU_ALPHA * gate)
        y = jnp.dot(act.astype(BF16), wd_scr[...], preferred_element_type=F32) + bd_ref[...]
        o_ref[...] = _pack_bf16_pair(y)

    @pl.when(jnp.logical_not(used))
    def _():
        o_ref[...] = jnp.zeros(o_ref.shape, o_ref.dtype)


def _experts(xs, tile_expert, n_used, tile_valid, wgu, bgu, wd, bd):
    rows, dh = xs.shape
    d = 2 * dh
    tm = MOE_TILE
    n_tiles = rows // tm
    dff2 = wgu.shape[-1]

    def tile(i, te, nu, tv):
        return (jnp.minimum(i, nu[0] - 1), 0)

    def wsel(i, te, nu, tv):
        return (te[jnp.minimum(i, nu[0] - 1)], 0, 0)

    grid_spec = pltpu.PrefetchScalarGridSpec(
        num_scalar_prefetch=3,
        grid=(n_tiles,),
        in_specs=[pl.BlockSpec((tm, dh), tile),
                  pl.BlockSpec((None, d, dff2), wsel),
                  pl.BlockSpec((None, 1, dff2), wsel),
                  pl.BlockSpec((None, dff2 // 2, d), wsel),
                  pl.BlockSpec((None, 1, d), wsel)],
        out_specs=pl.BlockSpec((tm, dh), lambda i, te, nu, tv: (i, 0)),
        scratch_shapes=[pltpu.VMEM((d, dff2), BF16), pltpu.VMEM((dff2 // 2, d), BF16)],
    )
    return pl.pallas_call(
        _expert_kernel,
        grid_spec=grid_spec,
        out_shape=jax.ShapeDtypeStruct((rows, dh), jnp.uint32),
        compiler_params=_cparams("arbitrary"),
        name="experts",
    )(tile_expert, n_used, tile_valid, xs, wgu, bgu, wd, bd)


def _final_kernel(x1_ref, ya_ref, yb_ref, yc_ref, yd_ref, tw_ref, g2_ref, fw_ref, o_ref):
    d = x1_ref.shape[-1]
    half = d // 2
    tw = tw_ref[...]
    acc_hi = jnp.zeros((x1_ref.shape[0], half), F32)
    acc_lo = jnp.zeros((x1_ref.shape[0], half), F32)
    for k, y_ref in enumerate((ya_ref, yb_ref, yc_ref, yd_ref)):
        y_hi, y_lo = _unpack_bf16_pair(y_ref[...])
        acc_hi = acc_hi + tw[:, k:k + 1] * y_hi
        acc_lo = acc_lo + tw[:, k:k + 1] * y_lo
    x_hi = x1_ref[:, :half] + g2_ref[:, :half] * acc_hi
    x_lo = x1_ref[:, half:] + g2_ref[:, half:] * acc_lo
    ms = (jnp.sum(x_hi * x_hi, axis=-1, keepdims=True) + jnp.sum(x_lo * x_lo, axis=-1, keepdims=True)) / d
    inv = lax.rsqrt(ms + EPS)
    o_ref[:, :half] = x_hi * inv * fw_ref[:, :half]
    o_ref[:, half:] = x_lo * inv * fw_ref[:, half:]


def _final(x1, y4, tw, g2, fw, rows_per_mod, tm):
    m, d = x1.shape
    return pl.pallas_call(
        _final_kernel,
        grid=(m // tm,),
        in_specs=[pl.BlockSpec((tm, d), lambda i: (i, 0))]
        + [pl.BlockSpec((None, tm, d // 2), functools.partial(lambda k, i: (k, i, 0), k)) for k in range(TOP_K)]
        + [pl.BlockSpec((tm, LANES), lambda i: (i, 0)),
           pl.BlockSpec((None, 1, d), lambda i: ((i * tm) // rows_per_mod, 0, 0)),
           pl.BlockSpec((1, d), lambda i: (0, 0))],
        out_specs=pl.BlockSpec((tm, d), lambda i: (i, 0)),
        out_shape=jax.ShapeDtypeStruct((m, d), F32),
        compiler_params=_cparams("arbitrary"),
        name="final",
    )(x1, y4, y4, y4, y4, tw, g2, fw.reshape(1, d))


def _dispatch_plan(e, rank, counts, tm):
    n_assign = e.size
    padded = (counts + tm - 1) // tm * tm
    pad_end = jnp.cumsum(padded)
    pad_start = pad_end - padded
    dest = (pad_start[e] + rank).astype(jnp.int32)
    n_tiles = n_assign // tm + N_EXPERTS
    tile_start = jnp.arange(n_tiles, dtype=jnp.int32) * tm
    tile_expert = jnp.minimum(jnp.sum(pad_end[None, :] <= tile_start[:, None], axis=1), N_EXPERTS - 1).astype(jnp.int32)
    n_used = (pad_end[-1] // tm).astype(jnp.int32).reshape(1)
    tile_valid = jnp.clip(pad_start[tile_expert] + counts[tile_expert] - tile_start, 0, tm).astype(jnp.int32)
    return dest, tile_expert, n_used, tile_valid


def kernel(x, c, ctx, c_ctx, w_mod, b_mod, norm1_w, norm2_w, w_in, conv_w, conv_b, dt_bias, a_log, d_skip,
           ssd_norm_w, w_ssd_out, w_four_out, w_o, w_router, b_router, w_gate_up, b_gate_up, w_down, b_down,
           final_norm_w):
    bsz, n, d = x.shape
    n_ctx = ctx.shape[1]
    depth = w_mod.shape[0]
    assert depth == 1, "a stacked model would also need the context stream's residual update"
    d_inner = N_HEADS * HEAD_DIM
    off_dt = d_inner + 2 * BC_WIDTH
    m = bsz * n
    x2d = x.reshape(m, d)
    tm = min(512, n)

    for layer in range(depth):
        rows = -(-(bsz + 1) // 8) * 8
        c_rows = jnp.zeros((rows, d), F32).at[:bsz].set(c).at[bsz].set(c_ctx)
        mod = _modulation(c_rows, w_mod[layer], b_mod[layer])
        mods = [mod[:bsz, i * d:(i + 1) * d].reshape(bsz, 1, d) for i in range(N_MOD)]
        sh1, sc1, g1, sh2, sc2, g2 = mods
        sh1_c = mod[bsz:bsz + 1, :d].reshape(1, 1, d)
        sc1_c = mod[bsz:bsz + 1, d:2 * d].reshape(1, 1, d)

        wl = w_in[layer]
        w_main = jnp.concatenate([wl[:, :off_dt], wl[:, off_dt + 2 * N_HEADS:]], axis=1).astype(BF16)
        w_dt = jnp.zeros((d, 2 * LANES), F32)
        w_dt = w_dt.at[:, :N_HEADS].set(wl[:, off_dt:off_dt + N_HEADS])
        w_dt = w_dt.at[:, LANES:LANES + N_HEADS].set(wl[:, off_dt + N_HEADS:off_dt + 2 * N_HEADS]).astype(BF16)
        pad = jnp.zeros((2, 1, LANES - N_HEADS), F32)
        dt_bias2 = jnp.concatenate([dt_bias[layer].reshape(2, 1, N_HEADS), pad], axis=-1)
        a_log2 = jnp.concatenate([a_log[layer].reshape(2, 1, N_HEADS), pad], axis=-1)

        w_ctx = w_main[:, :off_dt]
        pc, dtc = _in_proj(ctx.reshape(bsz * n_ctx, d), norm1_w[layer], sc1_c, sh1_c, w_ctx, w_dt,
                           bsz * n_ctx, min(512, n_ctx))
        xbc_c = _conv_silu(pc.reshape(bsz, n_ctx, off_dt), conv_w[layer][:, :off_dt], conv_b[layer][:off_dt], off_dt)
        h0 = jnp.zeros((bsz, 2, N_GROUPS, D_STATE, HEADS_PER_GROUP * HEAD_DIM), F32)
        h_ctx = _ssd(xbc_c, dtc.reshape(bsz, n_ctx, 2 * LANES), dt_bias2, a_log2, h0, with_y=False)

        p, dtl = _in_proj(x2d, norm1_w[layer], sc1, sh1, w_main, w_dt, n, tm)
        n_conv = off_dt + 2 * BC_WIDTH
        xbc = _conv_silu(p.reshape(bsz, n, -1), conv_w[layer], conv_b[layer], n_conv)
        y2 = _ssd(xbc, dtl.reshape(bsz, n, 2 * LANES), dt_bias2, a_log2, h_ctx, with_y=True)
        ff = _fnet_mix(p, (n_conv + d_inner) // d, bsz, n)

        dsk = jnp.repeat(d_skip[layer].astype(F32), HEAD_DIM).reshape(1, d_inner)
        wr = jnp.zeros((d, LANES), F32).at[:, :N_EXPERTS].set(w_router[layer])
        br = jnp.full((1, LANES), NEG_BIG, F32).at[0, :N_EXPERTS].set(b_router[layer])
        x1, h2, ti, tw, rk, cnt = _tail(
            y2.reshape(2, m, d_inner), xbc.reshape(m, n_conv), p, ff, x2d, g1, sc2, sh2, dsk,
            ssd_norm_w[layer].reshape(1, d_inner), w_ssd_out[layer].astype(BF16), w_four_out[layer].astype(BF16),
            w_o[layer].astype(BF16), norm2_w[layer].reshape(1, d), wr, br, n, min(TAIL_ROWS, n))

        counts = cnt[0, :N_EXPERTS].astype(jnp.int32)
        dest, tile_expert, n_used, tile_valid = _dispatch_plan(ti[:, :TOP_K], rk[:, :TOP_K], counts, MOE_TILE)
        dest_km = dest.T.reshape(-1)
        xs = _scatter_rows(h2, dest_km, tile_expert.shape[0] * MOE_TILE)
        ys = _experts(xs, tile_expert, n_used, tile_valid, w_gate_up[layer],
                      b_gate_up[layer].reshape(N_EXPERTS, 1, -1), w_down[layer],
                      b_down[layer].reshape(N_EXPERTS, 1, -1))
        y4 = _gather_rows(ys, dest_km).reshape(TOP_K, m, d // 2)
        x2d = _final(x1, y4, tw, g2, final_norm_w, n, tm)
    return x2d.reshape(bsz, n, d)
```

```python
import functools
import math

import numpy as np
import jax
import jax.numpy as jnp
from jax import lax
from jax.experimental import pallas as pl
from jax.experimental.pallas import tpu as pltpu
from jax.experimental.pallas import tpu_sc as plsc

F32 = jnp.float32
BF16 = jnp.bfloat16

EPS = 1e-6
GRID_W = 64
N_MOD = 6
F_GROUPS = 8
F_GROUP_DIM = 128
HEAD_DIM = 64
N_HEADS = 32
N_GROUPS = 4
HEADS_PER_GROUP = N_HEADS // N_GROUPS
D_STATE = 128
BC_WIDTH = N_GROUPS * D_STATE
CONV_K = 5
CHUNK = 128
N_EXPERTS = 32
TOP_K = 4
SWIGLU_LIMIT = 7.0
SWIGLU_ALPHA = 1.702

LANES = 128
VMEM_LIMIT_BYTES = 56 * 1024 * 1024
NEG_BIG = -1e30
MOE_TILE = 512
TAIL_ROWS = 512
SC_GATHER_WINDOW = 64


def _cparams(*sem):
    return pltpu.CompilerParams(dimension_semantics=sem, vmem_limit_bytes=VMEM_LIMIT_BYTES)


def _sigmoid(v):
    return 1.0 / (1.0 + jnp.exp(-v))


def _silu(v):
    return v * _sigmoid(v)


def _softplus(v):
    return jnp.maximum(v, 0.0) + jnp.log(1.0 + jnp.exp(-jnp.abs(v)))


def _pack_bf16_pair(v):
    w = v.shape[1] // 2
    bits = lax.bitcast_convert_type(v.astype(BF16).astype(F32), jnp.uint32)
    return bits[:, :w] | (bits[:, w:] >> 16)


def _unpack_bf16_pair(p):
    hi = lax.bitcast_convert_type(p & jnp.uint32(0xFFFF0000), F32)
    lo = lax.bitcast_convert_type(p << 16, F32)
    return hi, lo


def _mod_kernel(c_ref, w_ref, b_ref, o_ref):
    s = _silu(c_ref[...]).astype(BF16)
    o_ref[...] = jnp.dot(s, w_ref[...].astype(BF16), preferred_element_type=F32) + b_ref[...]


def _modulation(c_rows, w_mod, b_mod):
    rows, d = c_rows.shape
    n_out = w_mod.shape[1]
    tn = 1024
    return pl.pallas_call(
        _mod_kernel,
        grid=(n_out // tn,),
        in_specs=[pl.BlockSpec((rows, d), lambda j: (0, 0)),
                  pl.BlockSpec((d, tn), lambda j: (0, j)),
                  pl.BlockSpec((1, tn), lambda j: (0, j))],
        out_specs=pl.BlockSpec((rows, tn), lambda j: (0, j)),
        out_shape=jax.ShapeDtypeStruct((rows, n_out), F32),
        compiler_params=_cparams("arbitrary"),
        name="modulation",
    )(c_rows, w_mod, b_mod.reshape(1, n_out))


_INPROJ_TN = 1024


def _inproj_kernel(x_ref, nw_ref, sc_ref, sh_ref, w_ref, wdt_ref, p_ref, dt_ref):
    x = x_ref[...]
    inv = lax.rsqrt(jnp.mean(x * x, axis=-1, keepdims=True) + EPS)
    h = (x * inv * nw_ref[...]) * (1.0 + sc_ref[...]) + sh_ref[...]
    hb = h.astype(BF16)
    dt_ref[...] = jnp.dot(hb, wdt_ref[...], preferred_element_type=F32)
    for j in range(w_ref.shape[1] // _INPROJ_TN):
        cols = slice(j * _INPROJ_TN, (j + 1) * _INPROJ_TN)
        p_ref[:, cols] = jnp.dot(hb, w_ref[:, cols], preferred_element_type=F32).astype(p_ref.dtype)


def _in_proj(x2d, norm_w, sc, sh, w_main, w_dt, rows_per_mod, tm):
    m, d = x2d.shape
    n = w_main.shape[1]
    ndt = w_dt.shape[1]
    resident = dict(pipeline_mode=pl.Buffered(1))
    return pl.pallas_call(
        _inproj_kernel,
        grid=(m // tm,),
        in_specs=[pl.BlockSpec((tm, d), lambda i: (i, 0)),
                  pl.BlockSpec((1, d), lambda i: (0, 0), **resident),
                  pl.BlockSpec((None, 1, d), lambda i: ((i * tm) // rows_per_mod, 0, 0)),
                  pl.BlockSpec((None, 1, d), lambda i: ((i * tm) // rows_per_mod, 0, 0)),
                  pl.BlockSpec((d, n), lambda i: (0, 0), **resident),
                  pl.BlockSpec((d, ndt), lambda i: (0, 0), **resident)],
        out_specs=[pl.BlockSpec((tm, n), lambda i: (i, 0)),
                   pl.BlockSpec((tm, ndt), lambda i: (i, 0))],
        out_shape=[jax.ShapeDtypeStruct((m, n), BF16),
                   jax.ShapeDtypeStruct((m, ndt), F32)],
        compiler_params=_cparams("arbitrary"),
        name="in_proj",
    )(x2d, norm_w.reshape(1, d), sc, sh, w_main, w_dt)


_CONV_HALO = 16


_CONV_ROWS = 64


def _conv_shift_matrix(rc):
    win = rc + 2 * _CONV_HALO
    s = np.zeros((rc, CONV_K * win), np.float32)
    for k in range(CONV_K):
        for l in range(rc):
            s[l, k * win + _CONV_HALO + l + k - CONV_K // 2] = 1.0
    return jnp.asarray(s, BF16)


def _conv_kernel(p_ref, w_ref, b_ref, s_ref, o_ref, *, n, rc):
    tc = o_ref.shape[-1]
    w = w_ref[...].astype(BF16)
    bias = b_ref[...]
    smat = s_ref[...]
    zeros = jnp.zeros((_CONV_HALO, tc), BF16)
    for r0 in range(0, n, rc):
        top = zeros if r0 == 0 else p_ref[r0 - _CONV_HALO:r0, :]
        bot = zeros if r0 + rc >= n else p_ref[r0 + rc:r0 + rc + _CONV_HALO, :]
        window = jnp.concatenate([top, p_ref[r0:r0 + rc, :], bot], axis=0)
        taps = jnp.concatenate([window * w[k:k + 1, :] for k in range(CONV_K)], axis=0)
        acc = jnp.dot(smat, taps, preferred_element_type=F32) + bias
        o_ref[r0:r0 + rc, :] = _silu(acc).astype(o_ref.dtype)


def _conv_silu(p3d, conv_w, conv_b, n_ch):
    bsz, n, _ = p3d.shape
    tc = 512
    rc = min(_CONV_ROWS, n)
    smat = _conv_shift_matrix(rc)
    return pl.pallas_call(
        functools.partial(_conv_kernel, n=n, rc=rc),
        grid=(bsz, n_ch // tc),
        in_specs=[pl.BlockSpec((None, n, tc), lambda b, j: (b, 0, j)),
                  pl.BlockSpec((CONV_K, tc), lambda b, j: (0, j)),
                  pl.BlockSpec((1, tc), lambda b, j: (0, j)),
                  pl.BlockSpec(smat.shape, lambda b, j: (0, 0))],
        out_specs=pl.BlockSpec((None, n, tc), lambda b, j: (b, 0, j)),
        out_shape=jax.ShapeDtypeStruct((bsz, n, n_ch), BF16),
        compiler_params=_cparams("arbitrary", "arbitrary"),
        name="conv_silu",
    )(p3d, conv_w, conv_b.reshape(1, -1), smat)


LOG2E = 1.4426950408889634
SSD_SUB = 8


def _head_expand_matrix():
    e = np.zeros((LANES, N_HEADS * HEAD_DIM), np.float32)
    for h in range(N_HEADS):
        e[h, h * HEAD_DIM:(h + 1) * HEAD_DIM] = 1.0
    return jnp.asarray(e, BF16)


def _ssd_chunk_terms(dt_raw, dt_bias, a_log, expand, tri, fwd):
    q = CHUNK
    dt = _softplus(dt_raw + dt_bias)
    da = dt * (-jnp.exp(a_log))
    hi = da.astype(BF16)
    r1 = da - hi.astype(F32)
    mid = r1.astype(BF16)
    lo = (r1 - mid.astype(F32)).astype(BF16)
    ones_tri = jnp.where(tri, 1.0, 0.0).astype(BF16)
    cs3 = jnp.dot(ones_tri, jnp.concatenate([hi, mid, lo], axis=1), preferred_element_type=F32)
    cs = cs3[:, :LANES] + cs3[:, LANES:2 * LANES] + cs3[:, 2 * LANES:]
    tot = jnp.where(fwd, cs[q - 1:q, :], cs[0:1, :])
    dte = dt * jnp.exp(tot - cs)
    dec = jnp.exp(tot)
    dec_hi = dec.astype(BF16)
    dec_lo = (dec - dec_hi.astype(F32)).astype(BF16)
    dec2 = jnp.concatenate([dec_hi, dec_lo, jnp.zeros((6, LANES), BF16)], axis=0)
    stacked = jnp.concatenate([jnp.exp(cs).astype(BF16), dte.astype(BF16), dec2], axis=0)
    spread = jnp.dot(stacked, expand, preferred_element_type=F32)
    ecs_e = spread[:q]
    dte_e = spread[q:2 * q].astype(BF16)
    dec_e = jnp.broadcast_to(spread[2 * q:2 * q + 1] + spread[2 * q + 1:2 * q + 2], (8, spread.shape[1]))
    cs2 = cs * LOG2E
    return cs2, cs2.T, dt.T, ecs_e, dte_e, dec_e


def _ssd_kernel(*refs, with_y, nc, n_sub):
    if with_y:
        (x_ref, b_ref, c_ref, dt_ref, dtn_ref, dtb_ref, alog_ref, exp_ref, h0_ref, y_ref,
         s_scr, cs_scr, dee_scr, dce_scr, ecs_scr) = refs
    else:
        (x_ref, b_ref, dt_ref, dtn_ref, dtb_ref, alog_ref, exp_ref, h0_ref, hout_ref,
         s_scr, cs_scr, dee_scr, dce_scr) = refs
    q = CHUNK
    gw = HEADS_PER_GROUP * HEAD_DIM
    d = pl.program_id(1)
    c = pl.program_id(2)
    row = lax.broadcasted_iota(jnp.int32, (q, q), 0)
    col = lax.broadcasted_iota(jnp.int32, (q, q), 1)
    fwd = d == 0
    tri = jnp.where(fwd, row - col, col - row) >= 0
    first_half = col < HEAD_DIM

    def store_terms(slot, terms):
        cs2, cs2_t, dt_t, ecs_e, dte_e, dec_e = terms
        cs_scr[slot, 0] = cs2
        cs_scr[slot, 1] = cs2_t
        cs_scr[slot, 2] = dt_t
        dee_scr[slot] = dte_e
        dce_scr[slot] = dec_e
        if with_y:
            ecs_scr[slot] = ecs_e

    def terms_of(ref, sub):
        rows = pl.ds(pl.multiple_of(sub * q, q), q)
        return _ssd_chunk_terms(ref[rows, :], dtb_ref[...], alog_ref[...], exp_ref[...], tri, fwd)

    def sub_of(j):
        return jnp.where(fwd, j, n_sub - 1 - j)

    @pl.when(c == 0)
    def _():
        s_scr[...] = h0_ref[...]
        store_terms(0, terms_of(dt_ref, sub_of(0)))

    for j in range(n_sub):
        slot = j % 2
        rows = pl.ds(pl.multiple_of(sub_of(j) * q, q), q)
        cs2 = cs_scr[slot, 0]
        cs2_t = cs_scr[slot, 1]
        dt_t = cs_scr[slot, 2]
        if j + 1 < n_sub:
            store_terms(1 - slot, terms_of(dt_ref, sub_of(j + 1)))
        else:
            store_terms(1 - slot, terms_of(dtn_ref, sub_of(0)))

        for g in range(N_GROUPS):
            cols = slice(g * gw, (g + 1) * gw)
            bg = b_ref[rows, g * D_STATE:(g + 1) * D_STATE]
            xg = x_ref[rows, cols]
            s_old = s_scr[g]
            if with_y:
                cg = c_ref[rows, g * D_STATE:(g + 1) * D_STATE]
                cb = lax.dot_general(cg, bg, (((1,), (1,)), ((), ())), preferred_element_type=F32)
                y_off = jnp.dot(cg, s_old.astype(BF16), preferred_element_type=F32)
                for k in range(HEADS_PER_GROUP // 2):
                    pair = g * (HEADS_PER_GROUP // 2) + k
                    lanes = slice(pair * LANES, (pair + 1) * LANES)
                    xp = xg[:, k * LANES:(k + 1) * LANES]
                    xz = jnp.zeros_like(xp)
                    lhs = []
                    for hh in (2 * pair, 2 * pair + 1):
                        colb = jnp.broadcast_to(cs2[:, hh:hh + 1], (q, q))
                        rowb = cs2_t[hh:hh + 1, :]
                        decay = jnp.exp2(jnp.where(tri, colb - rowb, -jnp.inf))
                        lhs.append((cb * decay * dt_t[hh:hh + 1, :]).astype(BF16))
                    y_diag = jnp.dot(jnp.concatenate(lhs, axis=1),
                                     jnp.concatenate([jnp.where(first_half, xp, xz), jnp.where(first_half, xz, xp)],
                                                     axis=0),
                                     preferred_element_type=F32)
                    y = y_diag + ecs_scr[slot, :, lanes] * y_off[:, k * LANES:(k + 1) * LANES]
                    y_ref[rows, lanes] = y.astype(y_ref.dtype)
            xdte = xg * dee_scr[slot, :, cols]
            upd = lax.dot_general(bg, xdte, (((0,), (0,)), ((), ())), preferred_element_type=F32)
            s_scr[g] = s_old * dce_scr[slot, 0:1, cols] + upd

    if not with_y:
        @pl.when(c == nc - 1)
        def _():
            hout_ref[...] = s_scr[...]


def _ssd(xbc, dt_raw, dt_bias2, a_log2, h0, with_y):
    bsz, n, _ = xbc.shape
    n_sub = min(SSD_SUB, n // CHUNK)
    assert n_sub % 2 == 0 and n % (n_sub * CHUNK) == 0
    blk = n_sub * CHUNK
    nc = n // blk
    d_inner = N_HEADS * HEAD_DIM
    gw = HEADS_PER_GROUP * HEAD_DIM
    x_blk = d_inner // BC_WIDTH

    def ceff(dd, cc):
        return cc + dd * (nc - 1 - 2 * cc)

    def cnext(dd, cc):
        return ceff(dd, jnp.minimum(cc + 1, nc - 1))

    in_specs = [pl.BlockSpec((None, blk, d_inner), lambda b, dd, cc: (b, ceff(dd, cc), 0)),
                pl.BlockSpec((None, blk, BC_WIDTH), lambda b, dd, cc: (b, ceff(dd, cc), x_blk + dd))]
    args = [xbc, xbc]
    if with_y:
        in_specs.append(pl.BlockSpec((None, blk, BC_WIDTH), lambda b, dd, cc: (b, ceff(dd, cc), x_blk + 2 + dd)))
        args.append(xbc)
    in_specs += [pl.BlockSpec((None, blk, LANES), lambda b, dd, cc: (b, ceff(dd, cc), dd)),
                 pl.BlockSpec((None, blk, LANES), lambda b, dd, cc: (b, cnext(dd, cc), dd)),
                 pl.BlockSpec((None, 1, LANES), lambda b, dd, cc: (dd, 0, 0)),
                 pl.BlockSpec((None, 1, LANES), lambda b, dd, cc: (dd, 0, 0)),
                 pl.BlockSpec((LANES, d_inner), lambda b, dd, cc: (0, 0)),
                 pl.BlockSpec((None, None, N_GROUPS, D_STATE, gw), lambda b, dd, cc: (b, dd, 0, 0, 0))]
    args += [dt_raw, dt_raw, dt_bias2, a_log2, _head_expand_matrix(), h0]
    scratch = [pltpu.VMEM((N_GROUPS, D_STATE, gw), F32),
               pltpu.VMEM((2, 3, CHUNK, LANES), F32),
               pltpu.VMEM((2, CHUNK, d_inner), BF16),
               pltpu.VMEM((2, 8, d_inner), F32)]
    if with_y:
        out_specs = pl.BlockSpec((None, None, blk, d_inner), lambda b, dd, cc: (dd, b, ceff(dd, cc), 0))
        out_shape = jax.ShapeDtypeStruct((2, bsz, n, d_inner), BF16)
        scratch += [pltpu.VMEM((2, CHUNK, d_inner), F32)]
    else:
        out_specs = pl.BlockSpec((None, None, N_GROUPS, D_STATE, gw), lambda b, dd, cc: (b, dd, 0, 0, 0))
        out_shape = jax.ShapeDtypeStruct((bsz, 2, N_GROUPS, D_STATE, gw), F32)
    return pl.pallas_call(
        functools.partial(_ssd_kernel, with_y=with_y, nc=nc, n_sub=n_sub),
        grid=(bsz, 2, nc),
        in_specs=in_specs,
        out_specs=out_specs,
        out_shape=out_shape,
        scratch_shapes=scratch,
        compiler_params=_cparams("arbitrary", "arbitrary", "arbitrary"),
        name="ssd_scan" if with_y else "ssd_ctx_state",
    )(*args)


def _dft_mats(n):
    k = np.arange(n)
    ang = 2.0 * np.pi * ((k[:, None] * k[None, :]) % n) / n
    return np.cos(ang), np.sin(ang)


def _stage_lane_blocks(src_ref, scr, rows):
    width = src_ref.shape[-1]
    for t in range(2):
        v = src_ref[t].reshape(rows, width).astype(F32)
        for lb in range(width // LANES):
            scr[t, lb] = v[:, lb * LANES:(lb + 1) * LANES]


def _strided_rows(scr, t, start, count, stride, width):
    return jnp.concatenate([scr[t, lb, pl.ds(start, count, stride=stride), :] for lb in range(width // LANES)],
                           axis=1)


def _fnet_stage1_kernel(u_ref, chan_ref, m1_ref, twc_ref, tws_ref, o_ref, zs_scr, *, tcs):
    r = u_ref.shape[0]
    width = u_ref.shape[-1]
    u2 = u_ref[...].reshape(r * tcs, width)
    chan = chan_ref[...]
    for g in range(F_GROUPS):
        pq = jnp.dot(u2[:, g * F_GROUP_DIM:(g + 1) * F_GROUP_DIM], chan, preferred_element_type=F32)
        zs_scr[0, g] = pq[:, :F_GROUP_DIM]
        zs_scr[1, g] = pq[:, F_GROUP_DIM:]
    m1 = m1_ref[...]
    for j in range(tcs):
        zz = jnp.concatenate([_strided_rows(zs_scr, 0, j, r, tcs, width),
                              _strided_rows(zs_scr, 1, j, r, tcs, width)], axis=0).astype(BF16)
        u = jnp.dot(m1, zz, preferred_element_type=F32)
        ur = u[:r]
        ui = u[r:]
        tc = twc_ref[j]
        ts = tws_ref[j]
        o_ref[0, j] = (ur * tc + ui * ts).astype(o_ref.dtype)
        o_ref[1, j] = (ui * tc - ur * ts).astype(o_ref.dtype)


def _fnet_stage2_kernel(u_ref, m2_ref, o_ref, us_scr, os_scr, *, tbs):
    c = u_ref.shape[1]
    width = u_ref.shape[-1]
    _stage_lane_blocks(u_ref, us_scr, c * tbs)
    m2 = m2_ref[...]
    for j in range(tbs):
        uu = jnp.concatenate([_strided_rows(us_scr, 0, j, c, tbs, width),
                              _strided_rows(us_scr, 1, j, c, tbs, width)], axis=0).astype(BF16)
        x = jnp.dot(m2, uu, preferred_element_type=F32)
        for lb in range(width // LANES):
            os_scr[lb, pl.ds(j, c, stride=tbs), :] = x[:, lb * LANES:(lb + 1) * LANES]
    out = jnp.concatenate([os_scr[lb] for lb in range(width // LANES)], axis=1)
    o_ref[...] = out.reshape(c, tbs, width).astype(o_ref.dtype)


def _fnet_mix(p2d, col_blk, bsz, n):
    m = p2d.shape[0]
    width = F_GROUPS * F_GROUP_DIM
    cgrid = GRID_W
    rgrid = n // cgrid
    scale = 1.0 / math.sqrt(n * F_GROUP_DIM)

    cc, sc = _dft_mats(F_GROUP_DIM)
    chan = jnp.asarray(np.concatenate([cc, sc], axis=1) * scale, BF16)

    cr, sr = _dft_mats(rgrid)
    m1 = jnp.asarray(np.block([[cr, -sr], [-sr, -cr]]), BF16)
    bb = np.arange(rgrid)[None, :]
    ci = np.arange(cgrid)[:, None]
    ang = 2.0 * np.pi * ((ci * bb) % n) / n
    twc = jnp.asarray(np.cos(ang)[:, :, None], F32)
    tws = jnp.asarray(np.sin(ang)[:, :, None], F32)
    tcs = 16
    p4 = p2d.reshape(bsz, rgrid, cgrid, p2d.shape[-1])
    u5 = pl.pallas_call(
        functools.partial(_fnet_stage1_kernel, tcs=tcs),
        grid=(bsz, cgrid // tcs),
        in_specs=[pl.BlockSpec((None, rgrid, tcs, width), lambda b, j: (b, 0, j, col_blk)),
                  pl.BlockSpec((F_GROUP_DIM, 2 * F_GROUP_DIM), lambda b, j: (0, 0)),
                  pl.BlockSpec((2 * rgrid, 2 * rgrid), lambda b, j: (0, 0)),
                  pl.BlockSpec((tcs, rgrid, 1), lambda b, j: (j, 0, 0)),
                  pl.BlockSpec((tcs, rgrid, 1), lambda b, j: (j, 0, 0))],
        out_specs=pl.BlockSpec((2, None, tcs, rgrid, width), lambda b, j: (0, b, j, 0, 0)),
        out_shape=jax.ShapeDtypeStruct((2, bsz, cgrid, rgrid, width), BF16),
        scratch_shapes=[pltpu.VMEM((2, width // LANES, rgrid * tcs, LANES), F32)],
        compiler_params=_cparams("arbitrary", "arbitrary"),
        name="fnet_stage1",
    )(p4, chan, m1, twc, tws)

    cc2, sc2 = _dft_mats(cgrid)
    m2 = jnp.asarray(np.concatenate([cc2, sc2], axis=1), BF16)
    tbs = min(16, rgrid)
    out = pl.pallas_call(
        functools.partial(_fnet_stage2_kernel, tbs=tbs),
        grid=(bsz, rgrid // tbs),
        in_specs=[pl.BlockSpec((2, None, cgrid, tbs, width), lambda b, j: (0, b, 0, j, 0)),
                  pl.BlockSpec((cgrid, 2 * cgrid), lambda b, j: (0, 0))],
        out_specs=pl.BlockSpec((None, cgrid, tbs, width), lambda b, j: (b, 0, j, 0)),
        out_shape=jax.ShapeDtypeStruct((bsz, cgrid, rgrid, width), BF16),
        scratch_shapes=[pltpu.VMEM((2, width // LANES, cgrid * tbs, LANES), F32),
                        pltpu.VMEM((width // LANES, cgrid * tbs, LANES), F32)],
        compiler_params=_cparams("arbitrary", "arbitrary"),
        name="fnet_stage2",
    )(u5, m2)
    return out.reshape(m, width)


def _tail_kernel(yf_ref, yb_ref, xs_ref, z_ref, ff_ref, gf_ref, gs_ref, x_ref, g1_ref, sc2_ref, sh2_ref,
                 dsk_ref, snw_ref, wso_ref, wfo_ref, wo_ref, n2w_ref, wr_ref, br_ref,
                 x1_ref, h2_ref, ti_ref, tw_ref, rk_ref, cnt_ref, carry_scr):
    y = (yf_ref[0] + yb_ref[0]).astype(F32) + dsk_ref[...] * xs_ref[...].astype(F32)
    g = y * _silu(z_ref[...]).astype(F32)
    inv = lax.rsqrt(jnp.mean(g * g, axis=-1, keepdims=True) + EPS)
    gn = (g * inv * snw_ref[...]).astype(BF16)
    y_ssd = jnp.dot(gn, wso_ref[...], preferred_element_type=F32)
    y_four = jnp.dot(ff_ref[...], wfo_ref[...], preferred_element_type=F32)
    t = _sigmoid(gf_ref[...]) * y_four.astype(BF16) + _sigmoid(gs_ref[...]) * y_ssd.astype(BF16)
    mix = jnp.dot(t, wo_ref[...], preferred_element_type=F32)
    x1 = x_ref[...] + g1_ref[...] * mix
    x1_ref[...] = x1
    inv2 = lax.rsqrt(jnp.mean(x1 * x1, axis=-1, keepdims=True) + EPS)
    h2 = (x1 * inv2 * n2w_ref[...]) * (1.0 + sc2_ref[...]) + sh2_ref[...]
    h2_ref[...] = _pack_bf16_pair(h2)
    h_hi = h2.astype(BF16)
    h_lo = (h2 - h_hi.astype(F32)).astype(BF16)
    wr = wr_ref[...]
    w_hi = wr.astype(BF16)
    w_lo = (wr - w_hi.astype(F32)).astype(BF16)
    logits = (jnp.dot(h_hi, w_hi, preferred_element_type=F32)
              + jnp.dot(h_hi, w_lo, preferred_element_type=F32)
              + jnp.dot(h_lo, w_hi, preferred_element_type=F32)) + br_ref[...]
    tm = logits.shape[0]
    lane = lax.broadcasted_iota(jnp.int32, (tm, LANES), 1)
    vals, idxs = [], []
    cur = logits
    for _ in range(TOP_K):
        mx = jnp.max(cur, axis=-1, keepdims=True)
        ix = jnp.min(jnp.where(cur == mx, lane, LANES), axis=-1, keepdims=True)
        vals.append(mx)
        idxs.append(ix)
        cur = jnp.where(lane == ix, NEG_BIG * 2.0, cur)
    es = [jnp.exp(v - vals[0]) for v in vals]
    den = es[0] + es[1] + es[2] + es[3]
    ti = jnp.zeros((tm, LANES), jnp.int32)
    tw = jnp.zeros((tm, LANES), F32)
    for k in range(TOP_K):
        ti = jnp.where(lane == k, idxs[k], ti)
        tw = jnp.where(lane == k, es[k] / den, tw)
    ti_ref[...] = ti
    tw_ref[...] = tw

    @pl.when(pl.program_id(0) == 0)
    def _():
        carry_scr[...] = jnp.zeros(carry_scr.shape, F32)

    onehots = [jnp.where(lane == ix, 1.0, 0.0) for ix in idxs]
    cnt = onehots[0] + onehots[1] + onehots[2] + onehots[3]
    r_i = lax.broadcasted_iota(jnp.int32, (tm, tm), 0)
    c_i = lax.broadcasted_iota(jnp.int32, (tm, tm), 1)
    earlier = jnp.where(r_i > c_i, 1.0, 0.0).astype(BF16)
    before = jnp.dot(earlier, cnt.astype(BF16), preferred_element_type=F32) + carry_scr[0:1, :]
    rk = jnp.zeros((tm, LANES), jnp.int32)
    for k in range(TOP_K):
        rank_k = jnp.sum(onehots[k] * before, axis=-1, keepdims=True)
        rk = jnp.where(lane == k, rank_k.astype(jnp.int32), rk)
    rk_ref[...] = rk
    total = carry_scr[...] + jnp.sum(cnt, axis=0, keepdims=True)
    carry_scr[...] = total
    cnt_ref[...] = total


def _tail(y2, xbc2d, p2d, ff, x2d, g1, sc2, sh2, dsk, snw, wso, wfo, wo, n2w, wr, br, rows_per_mod, tm):
    m, d = x2d.shape
    di = y2.shape[-1]
    zb = (2 * di) // di
    fb = (3 * di) // d
    row = lambda i: (i, 0)
    modrow = lambda i: ((i * tm) // rows_per_mod, 0, 0)
    const = lambda i: (0, 0)
    in_specs = [pl.BlockSpec((1, tm, di), lambda i: (0, i, 0)),
                pl.BlockSpec((1, tm, di), lambda i: (1, i, 0)),
                pl.BlockSpec((tm, di), row),
                pl.BlockSpec((tm, di), lambda i: (i, zb)),
                pl.BlockSpec((tm, d), row),
                pl.BlockSpec((tm, d), lambda i: (i, fb + 1)),
                pl.BlockSpec((tm, d), lambda i: (i, fb + 2)),
                pl.BlockSpec((tm, d), row),
                pl.BlockSpec((None, 1, d), modrow),
                pl.BlockSpec((None, 1, d), modrow),
                pl.BlockSpec((None, 1, d), modrow),
                pl.BlockSpec((1, di), const),
                pl.BlockSpec((1, di), const),
                pl.BlockSpec((di, d), const),
                pl.BlockSpec((d, d), const),
                pl.BlockSpec((d, d), const),
                pl.BlockSpec((1, d), const),
                pl.BlockSpec((d, LANES), const),
                pl.BlockSpec((1, LANES), const)]
    out_specs = [pl.BlockSpec((tm, d), row), pl.BlockSpec((tm, d // 2), row),
                 pl.BlockSpec((tm, LANES), row), pl.BlockSpec((tm, LANES), row), pl.BlockSpec((tm, LANES), row),
                 pl.BlockSpec((8, LANES), const)]
    out_shape = [jax.ShapeDtypeStruct((m, d), F32), jax.ShapeDtypeStruct((m, d // 2), jnp.uint32),
                 jax.ShapeDtypeStruct((m, LANES), jnp.int32), jax.ShapeDtypeStruct((m, LANES), F32),
                 jax.ShapeDtypeStruct((m, LANES), jnp.int32), jax.ShapeDtypeStruct((8, LANES), F32)]
    return pl.pallas_call(
        _tail_kernel,
        grid=(m // tm,),
        in_specs=in_specs,
        out_specs=out_specs,
        out_shape=out_shape,
        scratch_shapes=[pltpu.VMEM((8, LANES), F32)],
        compiler_params=_cparams("arbitrary"),
        name="tail",
    )(y2, y2, xbc2d, p2d, ff, p2d, p2d, x2d, g1, sc2, sh2, dsk, snw, wso, wfo, wo, n2w, wr, br)


def _gather_rows(src, idx):
    n_out = idx.shape[0]
    width = src.shape[1]
    win = SC_GATHER_WINDOW
    info = plsc.get_sparse_core_info()
    n_cores, n_workers = info.num_cores, info.num_cores * info.num_subcores
    per_worker = n_out // n_workers
    assert per_worker * n_workers == n_out and per_worker % (2 * win) == 0
    mesh = plsc.VectorSubcoreMesh(core_axis_name="core", subcore_axis_name="subcore")

    @functools.partial(
        pl.kernel, out_type=jax.ShapeDtypeStruct((n_out, width), src.dtype), mesh=mesh,
        scratch_types=[pltpu.VMEM((win,), jnp.int32), pltpu.VMEM((win,), jnp.int32),
                       pltpu.VMEM((win, width), src.dtype), pltpu.VMEM((win, width), src.dtype),
                       pltpu.SemaphoreType.DMA, pltpu.SemaphoreType.DMA,
                       pltpu.SemaphoreType.DMA, pltpu.SemaphoreType.DMA],
        name="gather_rows")
    def gather(src_hbm, idx_hbm, out_hbm, idx_a, idx_b, rows_a, rows_b, gsem_a, gsem_b, ssem_a, ssem_b):
        worker = lax.axis_index("subcore") * n_cores + lax.axis_index("core")
        base = worker * per_worker

        @pl.loop(0, per_worker, step=2 * win)
        def _(off):
            pltpu.sync_copy(idx_hbm.at[pl.ds(base + off, win)], idx_a)
            ga = pltpu.async_copy(src_hbm.at[idx_a], rows_a, gsem_a)
            pltpu.sync_copy(idx_hbm.at[pl.ds(base + off + win, win)], idx_b)
            gb = pltpu.async_copy(src_hbm.at[idx_b], rows_b, gsem_b)
            ga.wait()
            sa = pltpu.async_copy(rows_a, out_hbm.at[pl.ds(base + off, win)], ssem_a)
            gb.wait()
            sb = pltpu.async_copy(rows_b, out_hbm.at[pl.ds(base + off + win, win)], ssem_b)
            sa.wait()
            sb.wait()

    return gather(src, idx)


def _scatter_rows(src, dest, n_rows):
    m, width = src.shape
    win = SC_GATHER_WINDOW
    info = plsc.get_sparse_core_info()
    n_cores, n_workers = info.num_cores, info.num_cores * info.num_subcores
    per_worker = m // n_workers
    assert per_worker * n_workers == m and per_worker % win == 0
    mesh = plsc.VectorSubcoreMesh(core_axis_name="core", subcore_axis_name="subcore")

    @functools.partial(
        pl.kernel, out_type=jax.ShapeDtypeStruct((n_rows, width), src.dtype), mesh=mesh,
        scratch_types=[pltpu.VMEM((win, width), src.dtype)]
        + [pltpu.VMEM((win,), jnp.int32)] * TOP_K + [pltpu.SemaphoreType.DMA] * TOP_K,
        name="scatter_rows")
    def scatter(src_hbm, dest_hbm, out_hbm, rows_v, *rest):
        idx_v, sems = rest[:TOP_K], rest[TOP_K:]
        worker = lax.axis_index("subcore") * n_cores + lax.axis_index("core")
        base = worker * per_worker

        @pl.loop(0, per_worker, step=win)
        def _(off):
            t0 = base + off
            pltpu.sync_copy(src_hbm.at[pl.ds(t0, win)], rows_v)
            for k in range(TOP_K):
                pltpu.sync_copy(dest_hbm.at[pl.ds(k * m + t0, win)], idx_v[k])
            copies = [pltpu.async_copy(rows_v, out_hbm.at[idx_v[k]], sems[k]) for k in range(TOP_K)]
            for cp in copies:
                cp.wait()

    return scatter(src, dest)


def _expert_kernel(te_ref, nu_ref, tv_ref, x_ref, wgu_ref, bgu_ref, wd_ref, bd_ref, o_ref, wgu_scr, wd_scr):
    i = pl.program_id(0)
    used = i < nu_ref[0]
    new_expert = (i == 0) | (te_ref[i] != te_ref[jnp.maximum(i - 1, 0)])

    @pl.when(used & new_expert)
    def _():
        wgu_scr[...] = wgu_ref[...].astype(BF16)
        wd_scr[...] = wd_ref[...].astype(BF16)

    @pl.when(used)
    def _():
        dff = wd_ref.shape[0]
        half = wgu_ref.shape[0] // 2
        rows = lax.broadcasted_iota(jnp.int32, x_ref.shape, 0)
        xa, xb = _unpack_bf16_pair(jnp.where(rows < tv_ref[i], x_ref[...], jnp.uint32(0)))
        gu = (jnp.dot(xa.astype(BF16), wgu_scr[:half, :], preferred_element_type=F32)
              + jnp.dot(xb.astype(BF16), wgu_scr[half:, :], preferred_element_type=F32)) + bgu_ref[...]
        gate = jnp.minimum(gu[:, :dff], SWIGLU_LIMIT)
        up = jnp.clip(gu[:, dff:], -SWIGLU_LIMIT, SWIGLU_LIMIT)
        act = (up + 1.0) * gate * _sigmoid(SWIGLU_ALPHA * gate)
        y = jnp.dot(act.astype(BF16), wd_scr[...], preferred_element_type=F32) + bd_ref[...]
        o_ref[...] = _pack_bf16_pair(y)

    @pl.when(jnp.logical_not(used))
    def _():
        o_ref[...] = jnp.zeros(o_ref.shape, o_ref.dtype)


def _experts(xs, tile_expert, n_used, tile_valid, wgu, bgu, wd, bd):
    rows, dh = xs.shape
    d = 2 * dh
    tm = MOE_TILE
    n_tiles = rows // tm
    dff2 = wgu.shape[-1]

    def tile(i, te, nu, tv):
        return (jnp.minimum(i, nu[0] - 1), 0)

    def wsel(i, te, nu, tv):
        return (te[jnp.minimum(i, nu[0] - 1)], 0, 0)

    grid_spec = pltpu.PrefetchScalarGridSpec(
        num_scalar_prefetch=3,
        grid=(n_tiles,),
        in_specs=[pl.BlockSpec((tm, dh), tile),
                  pl.BlockSpec((None, d, dff2), wsel),
                  pl.BlockSpec((None, 1, dff2), wsel),
                  pl.BlockSpec((None, dff2 // 2, d), wsel),
                  pl.BlockSpec((None, 1, d), wsel)],
        out_specs=pl.BlockSpec((tm, dh), lambda i, te, nu, tv: (i, 0)),
        scratch_shapes=[pltpu.VMEM((d, dff2), BF16), pltpu.VMEM((dff2 // 2, d), BF16)],
    )
    return pl.pallas_call(
        _expert_kernel,
        grid_spec=grid_spec,
        out_shape=jax.ShapeDtypeStruct((rows, dh), jnp.uint32),
        compiler_params=_cparams("arbitrary"),
        name="experts",
    )(tile_expert, n_used, tile_valid, xs, wgu, bgu, wd, bd)


def _final_kernel(x1_ref, ya_ref, yb_ref, yc_ref, yd_ref, tw_ref, g2_ref, fw_ref, o_ref):
    d = x1_ref.shape[-1]
    half = d // 2
    tw = tw_ref[...]
    acc_hi = jnp.zeros((x1_ref.shape[0], half), F32)
    acc_lo = jnp.zeros((x1_ref.shape[0], half), F32)
    for k, y_ref in enumerate((ya_ref, yb_ref, yc_ref, yd_ref)):
        y_hi, y_lo = _unpack_bf16_pair(y_ref[...])
        acc_hi = acc_hi + tw[:, k:k + 1] * y_hi
        acc_lo = acc_lo + tw[:, k:k + 1] * y_lo
    x_hi = x1_ref[:, :half] + g2_ref[:, :half] * acc_hi
    x_lo = x1_ref[:, half:] + g2_ref[:, half:] * acc_lo
    ms = (jnp.sum(x_hi * x_hi, axis=-1, keepdims=True) + jnp.sum(x_lo * x_lo, axis=-1, keepdims=True)) / d
    inv = lax.rsqrt(ms + EPS)
    o_ref[:, :half] = x_hi * inv * fw_ref[:, :half]
    o_ref[:, half:] = x_lo * inv * fw_ref[:, half:]


def _final(x1, y4, tw, g2, fw, rows_per_mod, tm):
    m, d = x1.shape
    return pl.pallas_call(
        _final_kernel,
        grid=(m // tm,),
        in_specs=[pl.BlockSpec((tm, d), lambda i: (i, 0))]
        + [pl.BlockSpec((None, tm, d // 2), functools.partial(lambda k, i: (k, i, 0), k)) for k in range(TOP_K)]
        + [pl.BlockSpec((tm, LANES), lambda i: (i, 0)),
           pl.BlockSpec((None, 1, d), lambda i: ((i * tm) // rows_per_mod, 0, 0)),
           pl.BlockSpec((1, d), lambda i: (0, 0))],
        out_specs=pl.BlockSpec((tm, d), lambda i: (i, 0)),
        out_shape=jax.ShapeDtypeStruct((m, d), F32),
        compiler_params=_cparams("arbitrary"),
        name="final",
    )(x1, y4, y4, y4, y4, tw, g2, fw.reshape(1, d))


def _dispatch_plan(e, rank, counts, tm):
    n_assign = e.size
    padded = (counts + tm - 1) // tm * tm
    pad_end = jnp.cumsum(padded)
    pad_start = pad_end - padded
    dest = (pad_start[e] + rank).astype(jnp.int32)
    n_tiles = n_assign // tm + N_EXPERTS
    tile_start = jnp.arange(n_tiles, dtype=jnp.int32) * tm
    tile_expert = jnp.minimum(jnp.sum(pad_end[None, :] <= tile_start[:, None], axis=1), N_EXPERTS - 1).astype(jnp.int32)
    n_used = (pad_end[-1] // tm).astype(jnp.int32).reshape(1)
    tile_valid = jnp.clip(pad_start[tile_expert] + counts[tile_expert] - tile_start, 0, tm).astype(jnp.int32)
    return dest, tile_expert, n_used, tile_valid


def kernel(x, c, ctx, c_ctx, w_mod, b_mod, norm1_w, norm2_w, w_in, conv_w, conv_b, dt_bias, a_log, d_skip,
           ssd_norm_w, w_ssd_out, w_four_out, w_o, w_router, b_router, w_gate_up, b_gate_up, w_down, b_down,
           final_norm_w):
    bsz, n, d = x.shape
    n_ctx = ctx.shape[1]
    depth = w_mod.shape[0]
    assert depth == 1, "a stacked model would also need the context stream's residual update"
    d_inner = N_HEADS * HEAD_DIM
    off_dt = d_inner + 2 * BC_WIDTH
    m = bsz * n
    x2d = x.reshape(m, d)
    tm = min(512, n)

    for layer in range(depth):
        rows = -(-(bsz + 1) // 8) * 8
        c_rows = jnp.zeros((rows, d), F32).at[:bsz].set(c).at[bsz].set(c_ctx)
        mod = _modulation(c_rows, w_mod[layer], b_mod[layer])
        mods = [mod[:bsz, i * d:(i + 1) * d].reshape(bsz, 1, d) for i in range(N_MOD)]
        sh1, sc1, g1, sh2, sc2, g2 = mods
        sh1_c = mod[bsz:bsz + 1, :d].reshape(1, 1, d)
        sc1_c = mod[bsz:bsz + 1, d:2 * d].reshape(1, 1, d)

        wl = w_in[layer]
        w_main = jnp.concatenate([wl[:, :off_dt], wl[:, off_dt + 2 * N_HEADS:]], axis=1).astype(BF16)
        w_dt = jnp.zeros((d, 2 * LANES), F32)
        w_dt = w_dt.at[:, :N_HEADS].set(wl[:, off_dt:off_dt + N_HEADS])
        w_dt = w_dt.at[:, LANES:LANES + N_HEADS].set(wl[:, off_dt + N_HEADS:off_dt + 2 * N_HEADS]).astype(BF16)
        pad = jnp.zeros((2, 1, LANES - N_HEADS), F32)
        dt_bias2 = jnp.concatenate([dt_bias[layer].reshape(2, 1, N_HEADS), pad], axis=-1)
        a_log2 = jnp.concatenate([a_log[layer].reshape(2, 1, N_HEADS), pad], axis=-1)

        w_ctx = w_main[:, :off_dt]
        pc, dtc = _in_proj(ctx.reshape(bsz * n_ctx, d), norm1_w[layer], sc1_c, sh1_c, w_ctx, w_dt,
                           bsz * n_ctx, min(512, n_ctx))
        xbc_c = _conv_silu(pc.reshape(bsz, n_ctx, off_dt), conv_w[layer][:, :off_dt], conv_b[layer][:off_dt], off_dt)
        h0 = jnp.zeros((bsz, 2, N_GROUPS, D_STATE, HEADS_PER_GROUP * HEAD_DIM), F32)
        h_ctx = _ssd(xbc_c, dtc.reshape(bsz, n_ctx, 2 * LANES), dt_bias2, a_log2, h0, with_y=False)

        p, dtl = _in_proj(x2d, norm1_w[layer], sc1, sh1, w_main, w_dt, n, tm)
        n_conv = off_dt + 2 * BC_WIDTH
        xbc = _conv_silu(p.reshape(bsz, n, -1), conv_w[layer], conv_b[layer], n_conv)
        y2 = _ssd(xbc, dtl.reshape(bsz, n, 2 * LANES), dt_bias2, a_log2, h_ctx, with_y=True)
        ff = _fnet_mix(p, (n_conv + d_inner) // d, bsz, n)

        dsk = jnp.repeat(d_skip[layer].astype(F32), HEAD_DIM).reshape(1, d_inner)
        wr = jnp.zeros((d, LANES), F32).at[:, :N_EXPERTS].set(w_router[layer])
        br = jnp.full((1, LANES), NEG_BIG, F32).at[0, :N_EXPERTS].set(b_router[layer])
        x1, h2, ti, tw, rk, cnt = _tail(
            y2.reshape(2, m, d_inner), xbc.reshape(m, n_conv), p, ff, x2d, g1, sc2, sh2, dsk,
            ssd_norm_w[layer].reshape(1, d_inner), w_ssd_out[layer].astype(BF16), w_four_out[layer].astype(BF16),
            w_o[layer].astype(BF16), norm2_w[layer].reshape(1, d), wr, br, n, min(TAIL_ROWS, n))

        counts = cnt[0, :N_EXPERTS].astype(jnp.int32)
        dest, tile_expert, n_used, tile_valid = _dispatch_plan(ti[:, :TOP_K], rk[:, :TOP_K], counts, MOE_TILE)
        dest_km = dest.T.reshape(-1)
        xs = _scatter_rows(h2, dest_km, tile_expert.shape[0] * MOE_TILE)
        ys = _experts(xs, tile_expert, n_used, tile_valid, w_gate_up[layer],
                      b_gate_up[layer].reshape(N_EXPERTS, 1, -1), w_down[layer],
                      b_down[layer].reshape(N_EXPERTS, 1, -1))
        y4 = _gather_rows(ys, dest_km).reshape(TOP_K, m, d // 2)
        x2d = _final(x1, y4, tw, g2, final_norm_w, n, tm)
    return x2d.reshape(bsz, n, d)
```

```python
import functools
import math

import numpy as np
import jax
import jax.numpy as jnp
from jax import lax
from jax.experimental import pallas as pl
from jax.experimental.pallas import tpu as pltpu
from jax.experimental.pallas import tpu_sc as plsc

F32 = jnp.float32
BF16 = jnp.bfloat16

EPS = 1e-6
GRID_W = 64
N_MOD = 6
F_GROUPS = 8
F_GROUP_DIM = 128
HEAD_DIM = 64
N_HEADS = 32
N_GROUPS = 4
HEADS_PER_GROUP = N_HEADS // N_GROUPS
D_STATE = 128
BC_WIDTH = N_GROUPS * D_STATE
CONV_K = 5
CHUNK = 128
N_EXPERTS = 32
TOP_K = 4
SWIGLU_LIMIT = 7.0
SWIGLU_ALPHA = 1.702

LANES = 128
VMEM_LIMIT_BYTES = 56 * 1024 * 1024
NEG_BIG = -1e30
MOE_TILE = 512
TAIL_ROWS = 512
COMBINE_PARTS = 4
SC_GATHER_WINDOW = 64


def _cparams(*sem):
    return pltpu.CompilerParams(dimension_semantics=sem, vmem_limit_bytes=VMEM_LIMIT_BYTES)


def _sigmoid(v):
    return 1.0 / (1.0 + jnp.exp(-v))


def _silu(v):
    return v * _sigmoid(v)


def _softplus(v):
    return jnp.maximum(v, 0.0) + jnp.log(1.0 + jnp.exp(-jnp.abs(v)))


def _pack_bf16_pair(v):
    w = v.shape[1] // 2
    bits = lax.bitcast_convert_type(v.astype(BF16).astype(F32), jnp.uint32)
    return bits[:, :w] | (bits[:, w:] >> 16)


def _unpack_bf16_pair(p):
    hi = lax.bitcast_convert_type(p & jnp.uint32(0xFFFF0000), F32)
    lo = lax.bitcast_convert_type(p << 16, F32)
    return hi, lo


def _mod_kernel(c_ref, w_ref, b_ref, o_ref):
    s = _silu(c_ref[...]).astype(BF16)
    o_ref[...] = jnp.dot(s, w_ref[...].astype(BF16), preferred_element_type=F32) + b_ref[...]


def _modulation(c_rows, w_mod, b_mod):
    rows, d = c_rows.shape
    n_out = w_mod.shape[1]
    tn = 1024
    return pl.pallas_call(
        _mod_kernel,
        grid=(n_out // tn,),
        in_specs=[pl.BlockSpec((rows, d), lambda j: (0, 0)),
                  pl.BlockSpec((d, tn), lambda j: (0, j)),
                  pl.BlockSpec((1, tn), lambda j: (0, j))],
        out_specs=pl.BlockSpec((rows, tn), lambda j: (0, j)),
        out_shape=jax.ShapeDtypeStruct((rows, n_out), F32),
        compiler_params=_cparams("arbitrary"),
        name="modulation",
    )(c_rows, w_mod, b_mod.reshape(1, n_out))


_INPROJ_TN = 1024


def _inproj_kernel(x_ref, nw_ref, sc_ref, sh_ref, w_ref, wdt_ref, p_ref, dt_ref):
    x = x_ref[...]
    inv = lax.rsqrt(jnp.mean(x * x, axis=-1, keepdims=True) + EPS)
    h = (x * inv * nw_ref[...]) * (1.0 + sc_ref[...]) + sh_ref[...]
    hb = h.astype(BF16)
    dt_ref[...] = jnp.dot(hb, wdt_ref[...], preferred_element_type=F32)
    for j in range(w_ref.shape[1] // _INPROJ_TN):
        cols = slice(j * _INPROJ_TN, (j + 1) * _INPROJ_TN)
        p_ref[:, cols] = jnp.dot(hb, w_ref[:, cols], preferred_element_type=F32).astype(p_ref.dtype)


def _in_proj(x2d, norm_w, sc, sh, w_main, w_dt, rows_per_mod, tm):
    m, d = x2d.shape
    n = w_main.shape[1]
    ndt = w_dt.shape[1]
    resident = dict(pipeline_mode=pl.Buffered(1))
    return pl.pallas_call(
        _inproj_kernel,
        grid=(m // tm,),
        in_specs=[pl.BlockSpec((tm, d), lambda i: (i, 0)),
                  pl.BlockSpec((1, d), lambda i: (0, 0), **resident),
                  pl.BlockSpec((None, 1, d), lambda i: ((i * tm) // rows_per_mod, 0, 0)),
                  pl.BlockSpec((None, 1, d), lambda i: ((i * tm) // rows_per_mod, 0, 0)),
                  pl.BlockSpec((d, n), lambda i: (0, 0), **resident),
                  pl.BlockSpec((d, ndt), lambda i: (0, 0), **resident)],
        out_specs=[pl.BlockSpec((tm, n), lambda i: (i, 0)),
                   pl.BlockSpec((tm, ndt), lambda i: (i, 0))],
        out_shape=[jax.ShapeDtypeStruct((m, n), BF16),
                   jax.ShapeDtypeStruct((m, ndt), F32)],
        compiler_params=_cparams("arbitrary"),
        name="in_proj",
    )(x2d, norm_w.reshape(1, d), sc, sh, w_main, w_dt)


_CONV_HALO = 16


_CONV_ROWS = 64


def _conv_shift_matrix(rc):
    win = rc + 2 * _CONV_HALO
    s = np.zeros((rc, CONV_K * win), np.float32)
    for k in range(CONV_K):
        for l in range(rc):
            s[l, k * win + _CONV_HALO + l + k - CONV_K // 2] = 1.0
    return jnp.asarray(s, BF16)


def _conv_kernel(p_ref, w_ref, b_ref, s_ref, o_ref, *, n, rc):
    tc = o_ref.shape[-1]
    w = w_ref[...].astype(BF16)
    bias = b_ref[...]
    smat = s_ref[...]
    zeros = jnp.zeros((_CONV_HALO, tc), BF16)
    for r0 in range(0, n, rc):
        top = zeros if r0 == 0 else p_ref[r0 - _CONV_HALO:r0, :]
        bot = zeros if r0 + rc >= n else p_ref[r0 + rc:r0 + rc + _CONV_HALO, :]
        window = jnp.concatenate([top, p_ref[r0:r0 + rc, :], bot], axis=0)
        taps = jnp.concatenate([window * w[k:k + 1, :] for k in range(CONV_K)], axis=0)
        acc = jnp.dot(smat, taps, preferred_element_type=F32) + bias
        o_ref[r0:r0 + rc, :] = _silu(acc).astype(o_ref.dtype)


def _conv_silu(p3d, conv_w, conv_b, n_ch):
    bsz, n, _ = p3d.shape
    tc = 512
    rc = min(_CONV_ROWS, n)
    smat = _conv_shift_matrix(rc)
    return pl.pallas_call(
        functools.partial(_conv_kernel, n=n, rc=rc),
        grid=(bsz, n_ch // tc),
        in_specs=[pl.BlockSpec((None, n, tc), lambda b, j: (b, 0, j)),
                  pl.BlockSpec((CONV_K, tc), lambda b, j: (0, j)),
                  pl.BlockSpec((1, tc), lambda b, j: (0, j)),
                  pl.BlockSpec(smat.shape, lambda b, j: (0, 0))],
        out_specs=pl.BlockSpec((None, n, tc), lambda b, j: (b, 0, j)),
        out_shape=jax.ShapeDtypeStruct((bsz, n, n_ch), BF16),
        compiler_params=_cparams("arbitrary", "arbitrary"),
        name="conv_silu",
    )(p3d, conv_w, conv_b.reshape(1, -1), smat)


LOG2E = 1.4426950408889634
SSD_SUB = 8


def _head_expand_matrix():
    e = np.zeros((LANES, N_HEADS * HEAD_DIM), np.float32)
    for h in range(N_HEADS):
        e[h, h * HEAD_DIM:(h + 1) * HEAD_DIM] = 1.0
    return jnp.asarray(e, BF16)


def _ssd_chunk_terms(dt_raw, dt_bias, a_log, expand, tri, fwd):
    q = CHUNK
    dt = _softplus(dt_raw + dt_bias)
    da = dt * (-jnp.exp(a_log))
    hi = da.astype(BF16)
    r1 = da - hi.astype(F32)
    mid = r1.astype(BF16)
    lo = (r1 - mid.astype(F32)).astype(BF16)
    ones_tri = jnp.where(tri, 1.0, 0.0).astype(BF16)
    cs3 = jnp.dot(ones_tri, jnp.concatenate([hi, mid, lo], axis=1), preferred_element_type=F32)
    cs = cs3[:, :LANES] + cs3[:, LANES:2 * LANES] + cs3[:, 2 * LANES:]
    tot = jnp.where(fwd, cs[q - 1:q, :], cs[0:1, :])
    dte = dt * jnp.exp(tot - cs)
    dec = jnp.exp(tot)
    dec_hi = dec.astype(BF16)
    dec_lo = (dec - dec_hi.astype(F32)).astype(BF16)
    dec2 = jnp.concatenate([dec_hi, dec_lo, jnp.zeros((6, LANES), BF16)], axis=0)
    stacked = jnp.concatenate([jnp.exp(cs).astype(BF16), dte.astype(BF16), dec2], axis=0)
    spread = jnp.dot(stacked, expand, preferred_element_type=F32)
    ecs_e = spread[:q]
    dte_e = spread[q:2 * q].astype(BF16)
    dec_e = jnp.broadcast_to(spread[2 * q:2 * q + 1] + spread[2 * q + 1:2 * q + 2], (8, spread.shape[1]))
    cs2 = cs * LOG2E
    return cs2, cs2.T, dt.T, ecs_e, dte_e, dec_e


def _ssd_kernel(*refs, with_y, nc, n_sub):
    if with_y:
        (x_ref, b_ref, c_ref, dt_ref, dtn_ref, dtb_ref, alog_ref, exp_ref, h0_ref, y_ref,
         s_scr, cs_scr, dee_scr, dce_scr, ecs_scr) = refs
    else:
        (x_ref, b_ref, dt_ref, dtn_ref, dtb_ref, alog_ref, exp_ref, h0_ref, hout_ref,
         s_scr, cs_scr, dee_scr, dce_scr) = refs
    q = CHUNK
    gw = HEADS_PER_GROUP * HEAD_DIM
    d = pl.program_id(1)
    c = pl.program_id(2)
    row = lax.broadcasted_iota(jnp.int32, (q, q), 0)
    col = lax.broadcasted_iota(jnp.int32, (q, q), 1)
    fwd = d == 0
    tri = jnp.where(fwd, row - col, col - row) >= 0
    first_half = col < HEAD_DIM

    def store_terms(slot, terms):
        cs2, cs2_t, dt_t, ecs_e, dte_e, dec_e = terms
        cs_scr[slot, 0] = cs2
        cs_scr[slot, 1] = cs2_t
        cs_scr[slot, 2] = dt_t
        dee_scr[slot] = dte_e
        dce_scr[slot] = dec_e
        if with_y:
            ecs_scr[slot] = ecs_e

    def terms_of(ref, sub):
        rows = pl.ds(pl.multiple_of(sub * q, q), q)
        return _ssd_chunk_terms(ref[rows, :], dtb_ref[...], alog_ref[...], exp_ref[...], tri, fwd)

    def sub_of(j):
        return jnp.where(fwd, j, n_sub - 1 - j)

    @pl.when(c == 0)
    def _():
        s_scr[...] = h0_ref[...]
        store_terms(0, terms_of(dt_ref, sub_of(0)))

    for j in range(n_sub):
        slot = j % 2
        rows = pl.ds(pl.multiple_of(sub_of(j) * q, q), q)
        cs2 = cs_scr[slot, 0]
        cs2_t = cs_scr[slot, 1]
        dt_t = cs_scr[slot, 2]
        if j + 1 < n_sub:
            store_terms(1 - slot, terms_of(dt_ref, sub_of(j + 1)))
        else:
            store_terms(1 - slot, terms_of(dtn_ref, sub_of(0)))

        for g in range(N_GROUPS):
            cols = slice(g * gw, (g + 1) * gw)
            bg = b_ref[rows, g * D_STATE:(g + 1) * D_STATE]
            xg = x_ref[rows, cols]
            s_old = s_scr[g]
            if with_y:
                cg = c_ref[rows, g * D_STATE:(g + 1) * D_STATE]
                cb = lax.dot_general(cg, bg, (((1,), (1,)), ((), ())), preferred_element_type=F32)
                y_off = jnp.dot(cg, s_old.astype(BF16), preferred_element_type=F32)
                for k in range(HEADS_PER_GROUP // 2):
                    pair = g * (HEADS_PER_GROUP // 2) + k
                    lanes = slice(pair * LANES, (pair + 1) * LANES)
                    xp = xg[:, k * LANES:(k + 1) * LANES]
                    xz = jnp.zeros_like(xp)
                    lhs = []
                    for hh in (2 * pair, 2 * pair + 1):
                        colb = jnp.broadcast_to(cs2[:, hh:hh + 1], (q, q))
                        rowb = cs2_t[hh:hh + 1, :]
                        decay = jnp.exp2(jnp.where(tri, colb - rowb, -jnp.inf))
                        lhs.append((cb * decay * dt_t[hh:hh + 1, :]).astype(BF16))
                    y_diag = jnp.dot(jnp.concatenate(lhs, axis=1),
                                     jnp.concatenate([jnp.where(first_half, xp, xz), jnp.where(first_half, xz, xp)],
                                                     axis=0),
                                     preferred_element_type=F32)
                    y = y_diag + ecs_scr[slot, :, lanes] * y_off[:, k * LANES:(k + 1) * LANES]
                    y_ref[rows, lanes] = y.astype(y_ref.dtype)
            xdte = xg * dee_scr[slot, :, cols]
            upd = lax.dot_general(bg, xdte, (((0,), (0,)), ((), ())), preferred_element_type=F32)
            s_scr[g] = s_old * dce_scr[slot, 0:1, cols] + upd

    if not with_y:
        @pl.when(c == nc - 1)
        def _():
            hout_ref[...] = s_scr[...]


def _ssd(xbc, dt_raw, dt_bias2, a_log2, h0, with_y):
    bsz, n, _ = xbc.shape
    n_sub = min(SSD_SUB, n // CHUNK)
    assert n_sub % 2 == 0 and n % (n_sub * CHUNK) == 0
    blk = n_sub * CHUNK
    nc = n // blk
    d_inner = N_HEADS * HEAD_DIM
    gw = HEADS_PER_GROUP * HEAD_DIM
    x_blk = d_inner // BC_WIDTH

    def ceff(dd, cc):
        return cc + dd * (nc - 1 - 2 * cc)

    def cnext(dd, cc):
        return ceff(dd, jnp.minimum(cc + 1, nc - 1))

    in_specs = [pl.BlockSpec((None, blk, d_inner), lambda b, dd, cc: (b, ceff(dd, cc), 0)),
                pl.BlockSpec((None, blk, BC_WIDTH), lambda b, dd, cc: (b, ceff(dd, cc), x_blk + dd))]
    args = [xbc, xbc]
    if with_y:
        in_specs.append(pl.BlockSpec((None, blk, BC_WIDTH), lambda b, dd, cc: (b, ceff(dd, cc), x_blk + 2 + dd)))
        args.append(xbc)
    in_specs += [pl.BlockSpec((None, blk, LANES), lambda b, dd, cc: (b, ceff(dd, cc), dd)),
                 pl.BlockSpec((None, blk, LANES), lambda b, dd, cc: (b, cnext(dd, cc), dd)),
                 pl.BlockSpec((None, 1, LANES), lambda b, dd, cc: (dd, 0, 0)),
                 pl.BlockSpec((None, 1, LANES), lambda b, dd, cc: (dd, 0, 0)),
                 pl.BlockSpec((LANES, d_inner), lambda b, dd, cc: (0, 0)),
                 pl.BlockSpec((None, None, N_GROUPS, D_STATE, gw), lambda b, dd, cc: (b, dd, 0, 0, 0))]
    args += [dt_raw, dt_raw, dt_bias2, a_log2, _head_expand_matrix(), h0]
    scratch = [pltpu.VMEM((N_GROUPS, D_STATE, gw), F32),
               pltpu.VMEM((2, 3, CHUNK, LANES), F32),
               pltpu.VMEM((2, CHUNK, d_inner), BF16),
               pltpu.VMEM((2, 8, d_inner), F32)]
    if with_y:
        out_specs = pl.BlockSpec((None, None, blk, d_inner), lambda b, dd, cc: (dd, b, ceff(dd, cc), 0))
        out_shape = jax.ShapeDtypeStruct((2, bsz, n, d_inner), BF16)
        scratch += [pltpu.VMEM((2, CHUNK, d_inner), F32)]
    else:
        out_specs = pl.BlockSpec((None, None, N_GROUPS, D_STATE, gw), lambda b, dd, cc: (b, dd, 0, 0, 0))
        out_shape = jax.ShapeDtypeStruct((bsz, 2, N_GROUPS, D_STATE, gw), F32)
    return pl.pallas_call(
        functools.partial(_ssd_kernel, with_y=with_y, nc=nc, n_sub=n_sub),
        grid=(bsz, 2, nc),
        in_specs=in_specs,
        out_specs=out_specs,
        out_shape=out_shape,
        scratch_shapes=scratch,
        compiler_params=_cparams("arbitrary", "arbitrary", "arbitrary"),
        name="ssd_scan" if with_y else "ssd_ctx_state",
    )(*args)


def _dft_mats(n):
    k = np.arange(n)
    ang = 2.0 * np.pi * ((k[:, None] * k[None, :]) % n) / n
    return np.cos(ang), np.sin(ang)


def _stage_lane_blocks(src_ref, scr, rows):
    width = src_ref.shape[-1]
    for t in range(2):
        v = src_ref[t].reshape(rows, width).astype(F32)
        for lb in range(width // LANES):
            scr[t, lb] = v[:, lb * LANES:(lb + 1) * LANES]


def _strided_rows(scr, t, start, count, stride, width):
    return jnp.concatenate([scr[t, lb, pl.ds(start, count, stride=stride), :] for lb in range(width // LANES)],
                           axis=1)


def _fnet_stage1_kernel(u_ref, chan_ref, m1_ref, twc_ref, tws_ref, o_ref, zs_scr, *, tcs):
    r = u_ref.shape[0]
    width = u_ref.shape[-1]
    u2 = u_ref[...].reshape(r * tcs, width)
    chan = chan_ref[...]
    for g in range(F_GROUPS):
        pq = jnp.dot(u2[:, g * F_GROUP_DIM:(g + 1) * F_GROUP_DIM], chan, preferred_element_type=F32)
        zs_scr[0, g] = pq[:, :F_GROUP_DIM]
        zs_scr[1, g] = pq[:, F_GROUP_DIM:]
    m1 = m1_ref[...]
    for j in range(tcs):
        zz = jnp.concatenate([_strided_rows(zs_scr, 0, j, r, tcs, width),
                              _strided_rows(zs_scr, 1, j, r, tcs, width)], axis=0).astype(BF16)
        u = jnp.dot(m1, zz, preferred_element_type=F32)
        ur = u[:r]
        ui = u[r:]
        tc = twc_ref[j]
        ts = tws_ref[j]
        o_ref[0, j] = (ur * tc + ui * ts).astype(o_ref.dtype)
        o_ref[1, j] = (ui * tc - ur * ts).astype(o_ref.dtype)


def _fnet_stage2_kernel(u_ref, m2_ref, o_ref, us_scr, os_scr, *, tbs):
    c = u_ref.shape[1]
    width = u_ref.shape[-1]
    _stage_lane_blocks(u_ref, us_scr, c * tbs)
    m2 = m2_ref[...]
    for j in range(tbs):
        uu = jnp.concatenate([_strided_rows(us_scr, 0, j, c, tbs, width),
                              _strided_rows(us_scr, 1, j, c, tbs, width)], axis=0).astype(BF16)
        x = jnp.dot(m2, uu, preferred_element_type=F32)
        for lb in range(width // LANES):
            os_scr[lb, pl.ds(j, c, stride=tbs), :] = x[:, lb * LANES:(lb + 1) * LANES]
    out = jnp.concatenate([os_scr[lb] for lb in range(width // LANES)], axis=1)
    o_ref[...] = out.reshape(c, tbs, width).astype(o_ref.dtype)


def _fnet_mix(p2d, col_blk, bsz, n):
    m = p2d.shape[0]
    width = F_GROUPS * F_GROUP_DIM
    cgrid = GRID_W
    rgrid = n // cgrid
    scale = 1.0 / math.sqrt(n * F_GROUP_DIM)

    cc, sc = _dft_mats(F_GROUP_DIM)
    chan = jnp.asarray(np.concatenate([cc, sc], axis=1) * scale, BF16)

    cr, sr = _dft_mats(rgrid)
    m1 = jnp.asarray(np.block([[cr, -sr], [-sr, -cr]]), BF16)
    bb = np.arange(rgrid)[None, :]
    ci = np.arange(cgrid)[:, None]
    ang = 2.0 * np.pi * ((ci * bb) % n) / n
    twc = jnp.asarray(np.cos(ang)[:, :, None], F32)
    tws = jnp.asarray(np.sin(ang)[:, :, None], F32)
    tcs = 16
    p4 = p2d.reshape(bsz, rgrid, cgrid, p2d.shape[-1])
    u5 = pl.pallas_call(
        functools.partial(_fnet_stage1_kernel, tcs=tcs),
        grid=(bsz, cgrid // tcs),
        in_specs=[pl.BlockSpec((None, rgrid, tcs, width), lambda b, j: (b, 0, j, col_blk)),
                  pl.BlockSpec((F_GROUP_DIM, 2 * F_GROUP_DIM), lambda b, j: (0, 0)),
                  pl.BlockSpec((2 * rgrid, 2 * rgrid), lambda b, j: (0, 0)),
                  pl.BlockSpec((tcs, rgrid, 1), lambda b, j: (j, 0, 0)),
                  pl.BlockSpec((tcs, rgrid, 1), lambda b, j: (j, 0, 0))],
        out_specs=pl.BlockSpec((2, None, tcs, rgrid, width), lambda b, j: (0, b, j, 0, 0)),
        out_shape=jax.ShapeDtypeStruct((2, bsz, cgrid, rgrid, width), BF16),
        scratch_shapes=[pltpu.VMEM((2, width // LANES, rgrid * tcs, LANES), F32)],
        compiler_params=_cparams("arbitrary", "arbitrary"),
        name="fnet_stage1",
    )(p4, chan, m1, twc, tws)

    cc2, sc2 = _dft_mats(cgrid)
    m2 = jnp.asarray(np.concatenate([cc2, sc2], axis=1), BF16)
    tbs = min(16, rgrid)
    out = pl.pallas_call(
        functools.partial(_fnet_stage2_kernel, tbs=tbs),
        grid=(bsz, rgrid // tbs),
        in_specs=[pl.BlockSpec((2, None, cgrid, tbs, width), lambda b, j: (0, b, 0, j, 0)),
                  pl.BlockSpec((cgrid, 2 * cgrid), lambda b, j: (0, 0))],
        out_specs=pl.BlockSpec((None, cgrid, tbs, width), lambda b, j: (b, 0, j, 0)),
        out_shape=jax.ShapeDtypeStruct((bsz, cgrid, rgrid, width), BF16),
        scratch_shapes=[pltpu.VMEM((2, width // LANES, cgrid * tbs, LANES), F32),
                        pltpu.VMEM((width // LANES, cgrid * tbs, LANES), F32)],
        compiler_params=_cparams("arbitrary", "arbitrary"),
        name="fnet_stage2",
    )(u5, m2)
    return out.reshape(m, width)


def _tail_kernel(yf_ref, yb_ref, xs_ref, z_ref, ff_ref, gf_ref, gs_ref, x_ref, g1_ref, sc2_ref, sh2_ref,
                 dsk_ref, snw_ref, wso_ref, wfo_ref, wo_ref, n2w_ref, wr_ref, br_ref,
                 x1_ref, h2_ref, ti_ref, tw_ref, rk_ref, cnt_ref, carry_scr):
    y = (yf_ref[0] + yb_ref[0]).astype(F32) + dsk_ref[...] * xs_ref[...].astype(F32)
    g = y * _silu(z_ref[...]).astype(F32)
    inv = lax.rsqrt(jnp.mean(g * g, axis=-1, keepdims=True) + EPS)
    gn = (g * inv * snw_ref[...]).astype(BF16)
    y_ssd = jnp.dot(gn, wso_ref[...], preferred_element_type=F32)
    y_four = jnp.dot(ff_ref[...], wfo_ref[...], preferred_element_type=F32)
    t = _sigmoid(gf_ref[...]) * y_four.astype(BF16) + _sigmoid(gs_ref[...]) * y_ssd.astype(BF16)
    mix = jnp.dot(t, wo_ref[...], preferred_element_type=F32)
    x1 = x_ref[...] + g1_ref[...] * mix
    x1_ref[...] = x1
    inv2 = lax.rsqrt(jnp.mean(x1 * x1, axis=-1, keepdims=True) + EPS)
    h2 = (x1 * inv2 * n2w_ref[...]) * (1.0 + sc2_ref[...]) + sh2_ref[...]
    h2_ref[...] = _pack_bf16_pair(h2)
    h_hi = h2.astype(BF16)
    h_lo = (h2 - h_hi.astype(F32)).astype(BF16)
    wr = wr_ref[...]
    w_hi = wr.astype(BF16)
    w_lo = (wr - w_hi.astype(F32)).astype(BF16)
    logits = (jnp.dot(h_hi, w_hi, preferred_element_type=F32)
              + jnp.dot(h_hi, w_lo, preferred_element_type=F32)
              + jnp.dot(h_lo, w_hi, preferred_element_type=F32)) + br_ref[...]
    tm = logits.shape[0]
    lane = lax.broadcasted_iota(jnp.int32, (tm, LANES), 1)
    vals, idxs = [], []
    cur = logits
    for _ in range(TOP_K):
        mx = jnp.max(cur, axis=-1, keepdims=True)
        ix = jnp.min(jnp.where(cur == mx, lane, LANES), axis=-1, keepdims=True)
        vals.append(mx)
        idxs.append(ix)
        cur = jnp.where(lane == ix, NEG_BIG * 2.0, cur)
    es = [jnp.exp(v - vals[0]) for v in vals]
    den = es[0] + es[1] + es[2] + es[3]
    ti = jnp.zeros((tm, LANES), jnp.int32)
    tw = jnp.zeros((tm, LANES), F32)
    for k in range(TOP_K):
        ti = jnp.where(lane == k, idxs[k], ti)
        tw = jnp.where(lane == k, es[k] / den, tw)
    ti_ref[...] = ti
    tw_ref[...] = tw

    @pl.when(pl.program_id(0) == 0)
    def _():
        carry_scr[...] = jnp.zeros(carry_scr.shape, F32)

    onehots = [jnp.where(lane == ix, 1.0, 0.0) for ix in idxs]
    cnt = onehots[0] + onehots[1] + onehots[2] + onehots[3]
    r_i = lax.broadcasted_iota(jnp.int32, (tm, tm), 0)
    c_i = lax.broadcasted_iota(jnp.int32, (tm, tm), 1)
    earlier = jnp.where(r_i > c_i, 1.0, 0.0).astype(BF16)
    before = jnp.dot(earlier, cnt.astype(BF16), preferred_element_type=F32) + carry_scr[0:1, :]
    rk = jnp.zeros((tm, LANES), jnp.int32)
    for k in range(TOP_K):
        rank_k = jnp.sum(onehots[k] * before, axis=-1, keepdims=True)
        rk = jnp.where(lane == k, rank_k.astype(jnp.int32), rk)
    rk_ref[...] = rk
    total = carry_scr[...] + jnp.sum(cnt, axis=0, keepdims=True)
    carry_scr[...] = total
    cnt_ref[...] = total


def _tail(y2, xbc2d, p2d, ff, x2d, g1, sc2, sh2, dsk, snw, wso, wfo, wo, n2w, wr, br, rows_per_mod, tm):
    m, d = x2d.shape
    di = y2.shape[-1]
    zb = (2 * di) // di
    fb = (3 * di) // d
    row = lambda i: (i, 0)
    modrow = lambda i: ((i * tm) // rows_per_mod, 0, 0)
    const = lambda i: (0, 0)
    in_specs = [pl.BlockSpec((1, tm, di), lambda i: (0, i, 0)),
                pl.BlockSpec((1, tm, di), lambda i: (1, i, 0)),
                pl.BlockSpec((tm, di), row),
                pl.BlockSpec((tm, di), lambda i: (i, zb)),
                pl.BlockSpec((tm, d), row),
                pl.BlockSpec((tm, d), lambda i: (i, fb + 1)),
                pl.BlockSpec((tm, d), lambda i: (i, fb + 2)),
                pl.BlockSpec((tm, d), row),
                pl.BlockSpec((None, 1, d), modrow),
                pl.BlockSpec((None, 1, d), modrow),
                pl.BlockSpec((None, 1, d), modrow),
                pl.BlockSpec((1, di), const),
                pl.BlockSpec((1, di), const),
                pl.BlockSpec((di, d), const),
                pl.BlockSpec((d, d), const),
                pl.BlockSpec((d, d), const),
                pl.BlockSpec((1, d), const),
                pl.BlockSpec((d, LANES), const),
                pl.BlockSpec((1, LANES), const)]
    out_specs = [pl.BlockSpec((tm, d), row), pl.BlockSpec((tm, d // 2), row),
                 pl.BlockSpec((tm, LANES), row), pl.BlockSpec((tm, LANES), row), pl.BlockSpec((tm, LANES), row),
                 pl.BlockSpec((8, LANES), const)]
    out_shape = [jax.ShapeDtypeStruct((m, d), F32), jax.ShapeDtypeStruct((m, d // 2), jnp.uint32),
                 jax.ShapeDtypeStruct((m, LANES), jnp.int32), jax.ShapeDtypeStruct((m, LANES), F32),
                 jax.ShapeDtypeStruct((m, LANES), jnp.int32), jax.ShapeDtypeStruct((8, LANES), F32)]
    return pl.pallas_call(
        _tail_kernel,
        grid=(m // tm,),
        in_specs=in_specs,
        out_specs=out_specs,
        out_shape=out_shape,
        scratch_shapes=[pltpu.VMEM((8, LANES), F32)],
        compiler_params=_cparams("arbitrary"),
        name="tail",
    )(y2, y2, xbc2d, p2d, ff, p2d, p2d, x2d, g1, sc2, sh2, dsk, snw, wso, wfo, wo, n2w, wr, br)


def _gather_rows(src, idx):
    n_out = idx.shape[0]
    width = src.shape[1]
    win = SC_GATHER_WINDOW
    info = plsc.get_sparse_core_info()
    n_cores, n_workers = info.num_cores, info.num_cores * info.num_subcores
    per_worker = n_out // n_workers
    assert per_worker * n_workers == n_out and per_worker % (2 * win) == 0
    mesh = plsc.VectorSubcoreMesh(core_axis_name="core", subcore_axis_name="subcore")

    @functools.partial(
        pl.kernel, out_type=jax.ShapeDtypeStruct((n_out, width), src.dtype), mesh=mesh,
        scratch_types=[pltpu.VMEM((win,), jnp.int32), pltpu.VMEM((win,), jnp.int32),
                       pltpu.VMEM((win, width), src.dtype), pltpu.VMEM((win, width), src.dtype),
                       pltpu.SemaphoreType.DMA, pltpu.SemaphoreType.DMA,
                       pltpu.SemaphoreType.DMA, pltpu.SemaphoreType.DMA],
        name="gather_rows")
    def gather(src_hbm, idx_hbm, out_hbm, idx_a, idx_b, rows_a, rows_b, gsem_a, gsem_b, ssem_a, ssem_b):
        worker = lax.axis_index("subcore") * n_cores + lax.axis_index("core")
        base = worker * per_worker

        @pl.loop(0, per_worker, step=2 * win)
        def _(off):
            pltpu.sync_copy(idx_hbm.at[pl.ds(base + off, win)], idx_a)
            ga = pltpu.async_copy(src_hbm.at[idx_a], rows_a, gsem_a)
            pltpu.sync_copy(idx_hbm.at[pl.ds(base + off + win, win)], idx_b)
            gb = pltpu.async_copy(src_hbm.at[idx_b], rows_b, gsem_b)
            ga.wait()
            sa = pltpu.async_copy(rows_a, out_hbm.at[pl.ds(base + off, win)], ssem_a)
            gb.wait()
            sb = pltpu.async_copy(rows_b, out_hbm.at[pl.ds(base + off + win, win)], ssem_b)
            sa.wait()
            sb.wait()

    return gather(src, idx)


def _scatter_rows(src, dest, n_rows):
    m, width = src.shape
    win = SC_GATHER_WINDOW
    info = plsc.get_sparse_core_info()
    n_cores, n_workers = info.num_cores, info.num_cores * info.num_subcores
    per_worker = m // n_workers
    assert per_worker * n_workers == m and per_worker % win == 0
    mesh = plsc.VectorSubcoreMesh(core_axis_name="core", subcore_axis_name="subcore")

    @functools.partial(
        pl.kernel, out_type=jax.ShapeDtypeStruct((n_rows, width), src.dtype), mesh=mesh,
        scratch_types=[pltpu.VMEM((win, width), src.dtype)]
        + [pltpu.VMEM((win,), jnp.int32)] * TOP_K + [pltpu.SemaphoreType.DMA] * TOP_K,
        name="scatter_rows")
    def scatter(src_hbm, dest_hbm, out_hbm, rows_v, *rest):
        idx_v, sems = rest[:TOP_K], rest[TOP_K:]
        worker = lax.axis_index("subcore") * n_cores + lax.axis_index("core")
        base = worker * per_worker

        @pl.loop(0, per_worker, step=win)
        def _(off):
            t0 = base + off
            pltpu.sync_copy(src_hbm.at[pl.ds(t0, win)], rows_v)
            for k in range(TOP_K):
                pltpu.sync_copy(dest_hbm.at[pl.ds(k * m + t0, win)], idx_v[k])
            copies = [pltpu.async_copy(rows_v, out_hbm.at[idx_v[k]], sems[k]) for k in range(TOP_K)]
            for cp in copies:
                cp.wait()

    return scatter(src, dest)


def _expert_kernel(te_ref, nu_ref, tv_ref, x_ref, wgu_ref, bgu_ref, wd_ref, bd_ref, o_ref, wgu_scr, wd_scr):
    i = pl.program_id(0)
    used = i < nu_ref[0]
    new_expert = (i == 0) | (te_ref[i] != te_ref[jnp.maximum(i - 1, 0)])

    @pl.when(used & new_expert)
    def _():
        wgu_scr[...] = wgu_ref[...].astype(BF16)
        wd_scr[...] = wd_ref[...].astype(BF16)

    @pl.when(used)
    def _():
        dff = wd_ref.shape[0]
        half = wgu_ref.shape[0] // 2
        rows = lax.broadcasted_iota(jnp.int32, x_ref.shape, 0)
        xa, xb = _unpack_bf16_pair(jnp.where(rows < tv_ref[i], x_ref[...], jnp.uint32(0)))
        gu = (jnp.dot(xa.astype(BF16), wgu_scr[:half, :], preferred_element_type=F32)
              + jnp.dot(xb.astype(BF16), wgu_scr[half:, :], preferred_element_type=F32)) + bgu_ref[...]
        gate = jnp.minimum(gu[:, :dff], SWIGLU_LIMIT)
        up = jnp.clip(gu[:, dff:], -SWIGLU_LIMIT, SWIGLU_LIMIT)
        act = (up + 1.0) * gate * _sigmoid(SWIGLU_ALPHA * gate)
        y = jnp.dot(act.astype(BF16), wd_scr[...], preferred_element_type=F32) + bd_ref[...]
        o_ref[...] = _pack_bf16_pair(y)

    @pl.when(jnp.logical_not(used))
    def _():
        o_ref[...] = jnp.zeros(o_ref.shape, o_ref.dtype)


def _experts(xs, tile_expert, n_used, tile_valid, wgu, bgu, wd, bd):
    rows, dh = xs.shape
    d = 2 * dh
    tm = MOE_TILE
    n_tiles = rows // tm
    dff2 = wgu.shape[-1]

    def tile(i, te, nu, tv):
        return (jnp.minimum(i, nu[0] - 1), 0)

    def wsel(i, te, nu, tv):
        return (te[jnp.minimum(i, nu[0] - 1)], 0, 0)

    grid_spec = pltpu.PrefetchScalarGridSpec(
        num_scalar_prefetch=3,
        grid=(n_tiles,),
        in_specs=[pl.BlockSpec((tm, dh), tile),
                  pl.BlockSpec((None, d, dff2), wsel),
                  pl.BlockSpec((None, 1, dff2), wsel),
                  pl.BlockSpec((None, dff2 // 2, d), wsel),
                  pl.BlockSpec((None, 1, d), wsel)],
        out_specs=pl.BlockSpec((tm, dh), lambda i, te, nu, tv: (i, 0)),
        scratch_shapes=[pltpu.VMEM((d, dff2), BF16), pltpu.VMEM((dff2 // 2, d), BF16)],
    )
    return pl.pallas_call(
        _expert_kernel,
        grid_spec=grid_spec,
        out_shape=jax.ShapeDtypeStruct((rows, dh), jnp.uint32),
        compiler_params=_cparams("arbitrary"),
        name="experts",
    )(tile_expert, n_used, tile_valid, xs, wgu, bgu, wd, bd)


def _final_kernel(x1_ref, ya_ref, yb_ref, yc_ref, yd_ref, tw_ref, g2_ref, fw_ref, *rest):
    o_ref = rest[-1]
    d = x1_ref.shape[-1]
    half = d // 2
    tw = tw_ref[...]
    acc_hi = jnp.zeros((x1_ref.shape[0], half), F32)
    acc_lo = jnp.zeros((x1_ref.shape[0], half), F32)
    for k, y_ref in enumerate((ya_ref, yb_ref, yc_ref, yd_ref)):
        y_hi, y_lo = _unpack_bf16_pair(y_ref[...])
        acc_hi = acc_hi + tw[:, k:k + 1] * y_hi
        acc_lo = acc_lo + tw[:, k:k + 1] * y_lo
    x_hi = x1_ref[:, :half] + g2_ref[:, :half] * acc_hi
    x_lo = x1_ref[:, half:] + g2_ref[:, half:] * acc_lo
    ms = (jnp.sum(x_hi * x_hi, axis=-1, keepdims=True) + jnp.sum(x_lo * x_lo, axis=-1, keepdims=True)) / d
    inv = lax.rsqrt(ms + EPS)
    o_ref[:, :half] = x_hi * inv * fw_ref[:, :half]
    o_ref[:, half:] = x_lo * inv * fw_ref[:, half:]


def _final(x1, y4, tw, g2, fw, rows_per_mod, tm, part, n_parts, out_prev):
    m, d = x1.shape
    steps = m // n_parts // tm
    off = part * steps
    in_specs = ([pl.BlockSpec((tm, d), lambda i: (i + off, 0))]
                + [pl.BlockSpec((None, tm, d // 2), functools.partial(lambda k, i: (k, i, 0), k)) for k in range(TOP_K)]
                + [pl.BlockSpec((tm, LANES), lambda i: (i + off, 0)),
                   pl.BlockSpec((None, 1, d), lambda i: (((i + off) * tm) // rows_per_mod, 0, 0)),
                   pl.BlockSpec((1, d), lambda i: (0, 0))])
    args = [x1, y4, y4, y4, y4, tw, g2, fw.reshape(1, d)]
    aliases = {}
    if out_prev is not None:
        in_specs.append(pl.BlockSpec(memory_space=pl.ANY))
        aliases = {len(args): 0}
        args.append(out_prev)
    return pl.pallas_call(
        _final_kernel,
        grid=(steps,),
        in_specs=in_specs,
        out_specs=pl.BlockSpec((tm, d), lambda i: (i + off, 0)),
        out_shape=jax.ShapeDtypeStruct((m, d), F32),
        input_output_aliases=aliases,
        compiler_params=_cparams("arbitrary"),
        name="final",
    )(*args)


def _dispatch_plan(e, rank, counts, tm):
    n_assign = e.size
    padded = (counts + tm - 1) // tm * tm
    pad_end = jnp.cumsum(padded)
    pad_start = pad_end - padded
    dest = (pad_start[e] + rank).astype(jnp.int32)
    n_tiles = n_assign // tm + N_EXPERTS
    tile_start = jnp.arange(n_tiles, dtype=jnp.int32) * tm
    tile_expert = jnp.minimum(jnp.sum(pad_end[None, :] <= tile_start[:, None], axis=1), N_EXPERTS - 1).astype(jnp.int32)
    n_used = (pad_end[-1] // tm).astype(jnp.int32).reshape(1)
    tile_valid = jnp.clip(pad_start[tile_expert] + counts[tile_expert] - tile_start, 0, tm).astype(jnp.int32)
    return dest, tile_expert, n_used, tile_valid


def kernel(x, c, ctx, c_ctx, w_mod, b_mod, norm1_w, norm2_w, w_in, conv_w, conv_b, dt_bias, a_log, d_skip,
           ssd_norm_w, w_ssd_out, w_four_out, w_o, w_router, b_router, w_gate_up, b_gate_up, w_down, b_down,
           final_norm_w):
    bsz, n, d = x.shape
    n_ctx = ctx.shape[1]
    depth = w_mod.shape[0]
    assert depth == 1, "a stacked model would also need the context stream's residual update"
    d_inner = N_HEADS * HEAD_DIM
    off_dt = d_inner + 2 * BC_WIDTH
    m = bsz * n
    x2d = x.reshape(m, d)
    tm = min(512, n)

    for layer in range(depth):
        rows = -(-(bsz + 1) // 8) * 8
        c_rows = jnp.zeros((rows, d), F32).at[:bsz].set(c).at[bsz].set(c_ctx)
        mod = _modulation(c_rows, w_mod[layer], b_mod[layer])
        mods = [mod[:bsz, i * d:(i + 1) * d].reshape(bsz, 1, d) for i in range(N_MOD)]
        sh1, sc1, g1, sh2, sc2, g2 = mods
        sh1_c = mod[bsz:bsz + 1, :d].reshape(1, 1, d)
        sc1_c = mod[bsz:bsz + 1, d:2 * d].reshape(1, 1, d)

        wl = w_in[layer]
        w_main = jnp.concatenate([wl[:, :off_dt], wl[:, off_dt + 2 * N_HEADS:]], axis=1).astype(BF16)
        w_dt = jnp.zeros((d, 2 * LANES), F32)
        w_dt = w_dt.at[:, :N_HEADS].set(wl[:, off_dt:off_dt + N_HEADS])
        w_dt = w_dt.at[:, LANES:LANES + N_HEADS].set(wl[:, off_dt + N_HEADS:off_dt + 2 * N_HEADS]).astype(BF16)
        pad = jnp.zeros((2, 1, LANES - N_HEADS), F32)
        dt_bias2 = jnp.concatenate([dt_bias[layer].reshape(2, 1, N_HEADS), pad], axis=-1)
        a_log2 = jnp.concatenate([a_log[layer].reshape(2, 1, N_HEADS), pad], axis=-1)

        w_ctx = w_main[:, :off_dt]
        pc, dtc = _in_proj(ctx.reshape(bsz * n_ctx, d), norm1_w[layer], sc1_c, sh1_c, w_ctx, w_dt,
                           bsz * n_ctx, min(512, n_ctx))
        xbc_c = _conv_silu(pc.reshape(bsz, n_ctx, off_dt), conv_w[layer][:, :off_dt], conv_b[layer][:off_dt], off_dt)
        h0 = jnp.zeros((bsz, 2, N_GROUPS, D_STATE, HEADS_PER_GROUP * HEAD_DIM), F32)
        h_ctx = _ssd(xbc_c, dtc.reshape(bsz, n_ctx, 2 * LANES), dt_bias2, a_log2, h0, with_y=False)

        p, dtl = _in_proj(x2d, norm1_w[layer], sc1, sh1, w_main, w_dt, n, tm)
        n_conv = off_dt + 2 * BC_WIDTH
        xbc = _conv_silu(p.reshape(bsz, n, -1), conv_w[layer], conv_b[layer], n_conv)
        y2 = _ssd(xbc, dtl.reshape(bsz, n, 2 * LANES), dt_bias2, a_log2, h_ctx, with_y=True)
        ff = _fnet_mix(p, (n_conv + d_inner) // d, bsz, n)

        dsk = jnp.repeat(d_skip[layer].astype(F32), HEAD_DIM).reshape(1, d_inner)
        wr = jnp.zeros((d, LANES), F32).at[:, :N_EXPERTS].set(w_router[layer])
        br = jnp.full((1, LANES), NEG_BIG, F32).at[0, :N_EXPERTS].set(b_router[layer])
        x1, h2, ti, tw, rk, cnt = _tail(
            y2.reshape(2, m, d_inner), xbc.reshape(m, n_conv), p, ff, x2d, g1, sc2, sh2, dsk,
            ssd_norm_w[layer].reshape(1, d_inner), w_ssd_out[layer].astype(BF16), w_four_out[layer].astype(BF16),
            w_o[layer].astype(BF16), norm2_w[layer].reshape(1, d), wr, br, n, min(TAIL_ROWS, n))

        counts = cnt[0, :N_EXPERTS].astype(jnp.int32)
        dest, tile_expert, n_used, tile_valid = _dispatch_plan(ti[:, :TOP_K], rk[:, :TOP_K], counts, MOE_TILE)
        dest_km = dest.T.reshape(-1)
        xs = _scatter_rows(h2, dest_km, tile_expert.shape[0] * MOE_TILE)
        ys = _experts(xs, tile_expert, n_used, tile_valid, w_gate_up[layer],
                      b_gate_up[layer].reshape(N_EXPERTS, 1, -1), w_down[layer],
                      b_down[layer].reshape(N_EXPERTS, 1, -1))
        mp = m // COMBINE_PARTS
        dest_parts = dest.T.reshape(TOP_K, COMBINE_PARTS, mp)
        x2d = None
        for part in range(COMBINE_PARTS):
            y4 = _gather_rows(ys, dest_parts[:, part].reshape(-1)).reshape(TOP_K, mp, d // 2)
            x2d = _final(x1, y4, tw, g2, final_norm_w, n, min(tm, mp), part, COMBINE_PARTS, x2d)
    return x2d.reshape(bsz, n, d)
```

```python
import functools
import math

import numpy as np
import jax
import jax.numpy as jnp
from jax import lax
from jax.experimental import pallas as pl
from jax.experimental.pallas import tpu as pltpu
from jax.experimental.pallas import tpu_sc as plsc

F32 = jnp.float32
BF16 = jnp.bfloat16

EPS = 1e-6
GRID_W = 64
N_MOD = 6
F_GROUPS = 8
F_GROUP_DIM = 128
HEAD_DIM = 64
N_HEADS = 32
N_GROUPS = 4
HEADS_PER_GROUP = N_HEADS // N_GROUPS
D_STATE = 128
BC_WIDTH = N_GROUPS * D_STATE
CONV_K = 5
CHUNK = 128
N_EXPERTS = 32
TOP_K = 4
SWIGLU_LIMIT = 7.0
SWIGLU_ALPHA = 1.702

LANES = 128
VMEM_LIMIT_BYTES = 56 * 1024 * 1024
NEG_BIG = -1e30
MOE_TILE = 512
TAIL_ROWS = 512
TAIL_SPLIT = 1
COMBINE_PARTS = 1
SC_GATHER_WINDOW = 64


def _cparams(*sem):
    return pltpu.CompilerParams(dimension_semantics=sem, vmem_limit_bytes=VMEM_LIMIT_BYTES)


def _sigmoid(v):
    return 1.0 / (1.0 + jnp.exp(-v))


def _silu(v):
    return v * _sigmoid(v)


def _softplus(v):
    return jnp.maximum(v, 0.0) + jnp.log(1.0 + jnp.exp(-jnp.abs(v)))


def _pack_bf16_pair(v):
    w = v.shape[1] // 2
    bits = lax.bitcast_convert_type(v.astype(BF16).astype(F32), jnp.uint32)
    return bits[:, :w] | (bits[:, w:] >> 16)


def _unpack_bf16_pair(p):
    hi = lax.bitcast_convert_type(p & jnp.uint32(0xFFFF0000), F32)
    lo = lax.bitcast_convert_type(p << 16, F32)
    return hi, lo


def _mod_kernel(c_ref, w_ref, b_ref, o_ref):
    s = _silu(c_ref[...]).astype(BF16)
    o_ref[...] = jnp.dot(s, w_ref[...].astype(BF16), preferred_element_type=F32) + b_ref[...]


def _modulation(c_rows, w_mod, b_mod):
    rows, d = c_rows.shape
    n_out = w_mod.shape[1]
    tn = 1024
    return pl.pallas_call(
        _mod_kernel,
        grid=(n_out // tn,),
        in_specs=[pl.BlockSpec((rows, d), lambda j: (0, 0)),
                  pl.BlockSpec((d, tn), lambda j: (0, j)),
                  pl.BlockSpec((1, tn), lambda j: (0, j))],
        out_specs=pl.BlockSpec((rows, tn), lambda j: (0, j)),
        out_shape=jax.ShapeDtypeStruct((rows, n_out), F32),
        compiler_params=_cparams("arbitrary"),
        name="modulation",
    )(c_rows, w_mod, b_mod.reshape(1, n_out))


_INPROJ_TN = 1024
_CONV_HALO = 16


def _inproj_conv_kernel(*refs, n_conv, n_rest, seq, tm):
    xp_ref, x_ref, xn_ref, nw_ref, sc_ref, sh_ref, w_ref, wdt_ref, cw_ref, cb_ref = refs[:10]
    if n_rest:
        xbc_ref, p_ref, dt_ref = refs[10:]
    else:
        xbc_ref, dt_ref = refs[10:]
    i = pl.program_id(0)
    tot = tm + 2 * _CONV_HALO

    def norm_mod(v):
        inv = lax.rsqrt(jnp.mean(v * v, axis=-1, keepdims=True) + EPS)
        return ((v * inv * nw_ref[...]) * (1.0 + sc_ref[...]) + sh_ref[...]).astype(BF16)

    hb = norm_mod(x_ref[...])
    h_ext = jnp.concatenate([norm_mod(xp_ref[...]), hb, norm_mod(xn_ref[...])], axis=0)
    dt_ref[...] = jnp.dot(hb, wdt_ref[...], preferred_element_type=F32)

    first = (i * tm) % seq == 0
    last = ((i + 1) * tm) % seq == 0
    lo = jnp.where(first, _CONV_HALO, 0)
    hi = jnp.where(last, _CONV_HALO + tm, tot)
    rowid = lax.broadcasted_iota(jnp.int32, (tot, _INPROJ_TN), 0)
    keep = (rowid >= lo) & (rowid < hi)
    for j in range(n_conv // _INPROJ_TN):
        cols = slice(j * _INPROJ_TN, (j + 1) * _INPROJ_TN)
        pe = jnp.where(keep, jnp.dot(h_ext, w_ref[:, cols], preferred_element_type=F32), 0.0)
        acc = jnp.broadcast_to(cb_ref[:, cols], (tm, _INPROJ_TN))
        for k in range(CONV_K):
            shift = (CONV_K // 2 - k) % tot
            rolled = pe if shift == 0 else pltpu.roll(pe, shift, axis=0)
            acc = acc + cw_ref[k:k + 1, cols] * rolled[_CONV_HALO:_CONV_HALO + tm, :]
        xbc_ref[:, cols] = _silu(acc).astype(xbc_ref.dtype)
    for j in range(n_rest // _INPROJ_TN):
        wcols = slice(n_conv + j * _INPROJ_TN, n_conv + (j + 1) * _INPROJ_TN)
        p_ref[:, j * _INPROJ_TN:(j + 1) * _INPROJ_TN] = jnp.dot(
            hb, w_ref[:, wcols], preferred_element_type=F32).astype(p_ref.dtype)


def _in_proj_conv(x2d, norm_w, sc, sh, w_main, w_dt, conv_w, conv_b, n_conv, seq, rows_per_mod, tm):
    m, d = x2d.shape
    n_rest = w_main.shape[1] - n_conv
    ndt = w_dt.shape[1]
    hb = tm // _CONV_HALO
    n_hblk = m // _CONV_HALO
    resident = dict(pipeline_mode=pl.Buffered(1))
    modrow = lambda i: ((i * tm) // rows_per_mod, 0, 0)
    out_specs = [pl.BlockSpec((tm, n_conv), lambda i: (i, 0))]
    out_shape = [jax.ShapeDtypeStruct((m, n_conv), BF16)]
    if n_rest:
        out_specs.append(pl.BlockSpec((tm, n_rest), lambda i: (i, 0)))
        out_shape.append(jax.ShapeDtypeStruct((m, n_rest), BF16))
    out_specs.append(pl.BlockSpec((tm, ndt), lambda i: (i, 0)))
    out_shape.append(jax.ShapeDtypeStruct((m, ndt), F32))
    outs = pl.pallas_call(
        functools.partial(_inproj_conv_kernel, n_conv=n_conv, n_rest=n_rest, seq=seq, tm=tm),
        grid=(m // tm,),
        in_specs=[pl.BlockSpec((_CONV_HALO, d), lambda i: (jnp.maximum(i * hb - 1, 0), 0)),
                  pl.BlockSpec((tm, d), lambda i: (i, 0)),
                  pl.BlockSpec((_CONV_HALO, d), lambda i: (jnp.minimum((i + 1) * hb, n_hblk - 1), 0)),
                  pl.BlockSpec((1, d), lambda i: (0, 0), **resident),
                  pl.BlockSpec((None, 1, d), modrow),
                  pl.BlockSpec((None, 1, d), modrow),
                  pl.BlockSpec((d, n_conv + n_rest), lambda i: (0, 0), **resident),
                  pl.BlockSpec((d, ndt), lambda i: (0, 0), **resident),
                  pl.BlockSpec((CONV_K, n_conv), lambda i: (0, 0), **resident),
                  pl.BlockSpec((1, n_conv), lambda i: (0, 0), **resident)],
        out_specs=out_specs,
        out_shape=out_shape,
        compiler_params=_cparams("arbitrary"),
        name="in_proj_conv",
    )(x2d, x2d, x2d, norm_w.reshape(1, d), sc, sh, w_main, w_dt, conv_w, conv_b.reshape(1, -1))
    if n_rest:
        return outs[0], outs[1], outs[2]
    return outs[0], None, outs[1]


LOG2E = 1.4426950408889634
SSD_SUB = 8


def _head_expand_matrix():
    e = np.zeros((LANES, N_HEADS * HEAD_DIM), np.float32)
    for h in range(N_HEADS):
        e[h, h * HEAD_DIM:(h + 1) * HEAD_DIM] = 1.0
    return jnp.asarray(e, BF16)


def _ssd_chunk_terms(dt_raw, dt_bias, a_log, expand, tri, fwd):
    q = CHUNK
    dt = _softplus(dt_raw + dt_bias)
    da = dt * (-jnp.exp(a_log))
    hi = da.astype(BF16)
    r1 = da - hi.astype(F32)
    mid = r1.astype(BF16)
    lo = (r1 - mid.astype(F32)).astype(BF16)
    ones_tri = jnp.where(tri, 1.0, 0.0).astype(BF16)
    cs3 = jnp.dot(ones_tri, jnp.concatenate([hi, mid, lo], axis=1), preferred_element_type=F32)
    cs = cs3[:, :LANES] + cs3[:, LANES:2 * LANES] + cs3[:, 2 * LANES:]
    tot = jnp.where(fwd, cs[q - 1:q, :], cs[0:1, :])
    dte = dt * jnp.exp(tot - cs)
    dec = jnp.exp(tot)
    dec_hi = dec.astype(BF16)
    dec_lo = (dec - dec_hi.astype(F32)).astype(BF16)
    dec2 = jnp.concatenate([dec_hi, dec_lo, jnp.zeros((6, LANES), BF16)], axis=0)
    stacked = jnp.concatenate([jnp.exp(cs).astype(BF16), dte.astype(BF16), dec2], axis=0)
    spread = jnp.dot(stacked, expand, preferred_element_type=F32)
    ecs_e = spread[:q]
    dte_e = spread[q:2 * q].astype(BF16)
    dec_e = jnp.broadcast_to(spread[2 * q:2 * q + 1] + spread[2 * q + 1:2 * q + 2], (8, spread.shape[1]))
    cs2 = cs * LOG2E
    return cs2, cs2.T, dt.T, ecs_e, dte_e, dec_e


def _ssd_kernel(*refs, with_y, nc, n_sub):
    if with_y:
        (x_ref, b_ref, c_ref, dt_ref, dtn_ref, dtb_ref, alog_ref, exp_ref, h0_ref, y_ref,
         s_scr, cs_scr, dee_scr, dce_scr, ecs_scr) = refs
    else:
        (x_ref, b_ref, dt_ref, dtn_ref, dtb_ref, alog_ref, exp_ref, h0_ref, hout_ref,
         s_scr, cs_scr, dee_scr, dce_scr) = refs
    q = CHUNK
    gw = HEADS_PER_GROUP * HEAD_DIM
    d = pl.program_id(1)
    c = pl.program_id(2)
    row = lax.broadcasted_iota(jnp.int32, (q, q), 0)
    col = lax.broadcasted_iota(jnp.int32, (q, q), 1)
    fwd = d == 0
    tri = jnp.where(fwd, row - col, col - row) >= 0
    first_half = col < HEAD_DIM

    def store_terms(slot, terms):
        cs2, cs2_t, dt_t, ecs_e, dte_e, dec_e = terms
        cs_scr[slot, 0] = cs2
        cs_scr[slot, 1] = cs2_t
        cs_scr[slot, 2] = dt_t
        dee_scr[slot] = dte_e
        dce_scr[slot] = dec_e
        if with_y:
            ecs_scr[slot] = ecs_e

    def terms_of(ref, sub):
        rows = pl.ds(pl.multiple_of(sub * q, q), q)
        return _ssd_chunk_terms(ref[rows, :], dtb_ref[...], alog_ref[...], exp_ref[...], tri, fwd)

    def sub_of(j):
        return jnp.where(fwd, j, n_sub - 1 - j)

    @pl.when(c == 0)
    def _():
        s_scr[...] = h0_ref[...]
        store_terms(0, terms_of(dt_ref, sub_of(0)))

    for j in range(n_sub):
        slot = j % 2
        rows = pl.ds(pl.multiple_of(sub_of(j) * q, q), q)
        cs2 = cs_scr[slot, 0]
        cs2_t = cs_scr[slot, 1]
        dt_t = cs_scr[slot, 2]
        if j + 1 < n_sub:
            store_terms(1 - slot, terms_of(dt_ref, sub_of(j + 1)))
        else:
            store_terms(1 - slot, terms_of(dtn_ref, sub_of(0)))

        for g in range(N_GROUPS):
            cols = slice(g * gw, (g + 1) * gw)
            bg = b_ref[rows, g * D_STATE:(g + 1) * D_STATE]
            xg = x_ref[rows, cols]
            s_old = s_scr[g]
            if with_y:
                cg = c_ref[rows, g * D_STATE:(g + 1) * D_STATE]
                cb = lax.dot_general(cg, bg, (((1,), (1,)), ((), ())), preferred_element_type=F32)
                y_off = jnp.dot(cg, s_old.astype(BF16), preferred_element_type=F32)
                for k in range(HEADS_PER_GROUP // 2):
                    pair = g * (HEADS_PER_GROUP // 2) + k
                    lanes = slice(pair * LANES, (pair + 1) * LANES)
                    xp = xg[:, k * LANES:(k + 1) * LANES]
                    xz = jnp.zeros_like(xp)
                    lhs = []
                    for hh in (2 * pair, 2 * pair + 1):
                        colb = jnp.broadcast_to(cs2[:, hh:hh + 1], (q, q))
                        rowb = cs2_t[hh:hh + 1, :]
                        decay = jnp.exp2(jnp.where(tri, colb - rowb, -jnp.inf))
                        lhs.append((cb * decay * dt_t[hh:hh + 1, :]).astype(BF16))
                    y_diag = jnp.dot(jnp.concatenate(lhs, axis=1),
                                     jnp.concatenate([jnp.where(first_half, xp, xz), jnp.where(first_half, xz, xp)],
                                                     axis=0),
                                     preferred_element_type=F32)
                    y = y_diag + ecs_scr[slot, :, lanes] * y_off[:, k * LANES:(k + 1) * LANES]
                    y_ref[rows, lanes] = y.astype(y_ref.dtype)
            xdte = xg * dee_scr[slot, :, cols]
            upd = lax.dot_general(bg, xdte, (((0,), (0,)), ((), ())), preferred_element_type=F32)
            s_scr[g] = s_old * dce_scr[slot, 0:1, cols] + upd

    if not with_y:
        @pl.when(c == nc - 1)
        def _():
            hout_ref[...] = s_scr[...]


def _ssd(xbc, dt_raw, dt_bias2, a_log2, h0, with_y):
    bsz, n, _ = xbc.shape
    n_sub = min(SSD_SUB, n // CHUNK)
    assert n_sub % 2 == 0 and n % (n_sub * CHUNK) == 0
    blk = n_sub * CHUNK
    nc = n // blk
    d_inner = N_HEADS * HEAD_DIM
    gw = HEADS_PER_GROUP * HEAD_DIM
    x_blk = d_inner // BC_WIDTH

    def ceff(dd, cc):
        return cc + dd * (nc - 1 - 2 * cc)

    def cnext(dd, cc):
        return ceff(dd, jnp.minimum(cc + 1, nc - 1))

    in_specs = [pl.BlockSpec((None, blk, d_inner), lambda b, dd, cc: (b, ceff(dd, cc), 0)),
                pl.BlockSpec((None, blk, BC_WIDTH), lambda b, dd, cc: (b, ceff(dd, cc), x_blk + dd))]
    args = [xbc, xbc]
    if with_y:
        in_specs.append(pl.BlockSpec((None, blk, BC_WIDTH), lambda b, dd, cc: (b, ceff(dd, cc), x_blk + 2 + dd)))
        args.append(xbc)
    in_specs += [pl.BlockSpec((None, blk, LANES), lambda b, dd, cc: (b, ceff(dd, cc), dd)),
                 pl.BlockSpec((None, blk, LANES), lambda b, dd, cc: (b, cnext(dd, cc), dd)),
                 pl.BlockSpec((None, 1, LANES), lambda b, dd, cc: (dd, 0, 0)),
                 pl.BlockSpec((None, 1, LANES), lambda b, dd, cc: (dd, 0, 0)),
                 pl.BlockSpec((LANES, d_inner), lambda b, dd, cc: (0, 0)),
                 pl.BlockSpec((None, None, N_GROUPS, D_STATE, gw), lambda b, dd, cc: (b, dd, 0, 0, 0))]
    args += [dt_raw, dt_raw, dt_bias2, a_log2, _head_expand_matrix(), h0]
    scratch = [pltpu.VMEM((N_GROUPS, D_STATE, gw), F32),
               pltpu.VMEM((2, 3, CHUNK, LANES), F32),
               pltpu.VMEM((2, CHUNK, d_inner), BF16),
               pltpu.VMEM((2, 8, d_inner), F32)]
    if with_y:
        out_specs = pl.BlockSpec((None, None, blk, d_inner), lambda b, dd, cc: (dd, b, ceff(dd, cc), 0))
        out_shape = jax.ShapeDtypeStruct((2, bsz, n, d_inner), BF16)
        scratch += [pltpu.VMEM((2, CHUNK, d_inner), F32)]
    else:
        out_specs = pl.BlockSpec((None, None, N_GROUPS, D_STATE, gw), lambda b, dd, cc: (b, dd, 0, 0, 0))
        out_shape = jax.ShapeDtypeStruct((bsz, 2, N_GROUPS, D_STATE, gw), F32)
    return pl.pallas_call(
        functools.partial(_ssd_kernel, with_y=with_y, nc=nc, n_sub=n_sub),
        grid=(bsz, 2, nc),
        in_specs=in_specs,
        out_specs=out_specs,
        out_shape=out_shape,
        scratch_shapes=scratch,
        compiler_params=_cparams("arbitrary", "arbitrary", "arbitrary"),
        name="ssd_scan" if with_y else "ssd_ctx_state",
    )(*args)


def _dft_mats(n):
    k = np.arange(n)
    ang = 2.0 * np.pi * ((k[:, None] * k[None, :]) % n) / n
    return np.cos(ang), np.sin(ang)


def _stage_lane_blocks(src_ref, scr, rows):
    width = src_ref.shape[-1]
    for t in range(2):
        v = src_ref[t].reshape(rows, width).astype(F32)
        for lb in range(width // LANES):
            scr[t, lb] = v[:, lb * LANES:(lb + 1) * LANES]


def _strided_rows(scr, t, start, count, stride, width):
    return jnp.concatenate([scr[t, lb, pl.ds(start, count, stride=stride), :] for lb in range(width // LANES)],
                           axis=1)


def _fnet_stage1_kernel(u_ref, chan_ref, m1_ref, twc_ref, tws_ref, o_ref, zs_scr, *, tcs):
    r = u_ref.shape[0]
    width = u_ref.shape[-1]
    u2 = u_ref[...].reshape(r * tcs, width)
    chan = chan_ref[...]
    for g in range(F_GROUPS):
        pq = jnp.dot(u2[:, g * F_GROUP_DIM:(g + 1) * F_GROUP_DIM], chan, preferred_element_type=F32)
        zs_scr[0, g] = pq[:, :F_GROUP_DIM]
        zs_scr[1, g] = pq[:, F_GROUP_DIM:]
    m1 = m1_ref[...]
    for j in range(tcs):
        zz = jnp.concatenate([_strided_rows(zs_scr, 0, j, r, tcs, width),
                              _strided_rows(zs_scr, 1, j, r, tcs, width)], axis=0).astype(BF16)
        u = jnp.dot(m1, zz, preferred_element_type=F32)
        ur = u[:r]
        ui = u[r:]
        tc = twc_ref[j]
        ts = tws_ref[j]
        o_ref[0, j] = (ur * tc + ui * ts).astype(o_ref.dtype)
        o_ref[1, j] = (ui * tc - ur * ts).astype(o_ref.dtype)


def _fnet_stage2_kernel(u_ref, m2_ref, o_ref, us_scr, os_scr, *, tbs):
    c = u_ref.shape[1]
    width = u_ref.shape[-1]
    _stage_lane_blocks(u_ref, us_scr, c * tbs)
    m2 = m2_ref[...]
    for j in range(tbs):
        uu = jnp.concatenate([_strided_rows(us_scr, 0, j, c, tbs, width),
                              _strided_rows(us_scr, 1, j, c, tbs, width)], axis=0).astype(BF16)
        x = jnp.dot(m2, uu, preferred_element_type=F32)
        for lb in range(width // LANES):
            os_scr[lb, pl.ds(j, c, stride=tbs), :] = x[:, lb * LANES:(lb + 1) * LANES]
    out = jnp.concatenate([os_scr[lb] for lb in range(width // LANES)], axis=1)
    o_ref[...] = out.reshape(c, tbs, width).astype(o_ref.dtype)


def _fnet_mix(p2d, col_blk, bsz, n):
    m = p2d.shape[0]
    width = F_GROUPS * F_GROUP_DIM
    cgrid = GRID_W
    rgrid = n // cgrid
    scale = 1.0 / math.sqrt(n * F_GROUP_DIM)

    cc, sc = _dft_mats(F_GROUP_DIM)
    chan = jnp.asarray(np.concatenate([cc, sc], axis=1) * scale, BF16)

    cr, sr = _dft_mats(rgrid)
    m1 = jnp.asarray(np.block([[cr, -sr], [-sr, -cr]]), BF16)
    bb = np.arange(rgrid)[None, :]
    ci = np.arange(cgrid)[:, None]
    ang = 2.0 * np.pi * ((ci * bb) % n) / n
    twc = jnp.asarray(np.cos(ang)[:, :, None], F32)
    tws = jnp.asarray(np.sin(ang)[:, :, None], F32)
    tcs = 16
    p4 = p2d.reshape(bsz, rgrid, cgrid, p2d.shape[-1])
    u5 = pl.pallas_call(
        functools.partial(_fnet_stage1_kernel, tcs=tcs),
        grid=(bsz, cgrid // tcs),
        in_specs=[pl.BlockSpec((None, rgrid, tcs, width), lambda b, j: (b, 0, j, col_blk)),
                  pl.BlockSpec((F_GROUP_DIM, 2 * F_GROUP_DIM), lambda b, j: (0, 0)),
                  pl.BlockSpec((2 * rgrid, 2 * rgrid), lambda b, j: (0, 0)),
                  pl.BlockSpec((tcs, rgrid, 1), lambda b, j: (j, 0, 0)),
                  pl.BlockSpec((tcs, rgrid, 1), lambda b, j: (j, 0, 0))],
        out_specs=pl.BlockSpec((2, None, tcs, rgrid, width), lambda b, j: (0, b, j, 0, 0)),
        out_shape=jax.ShapeDtypeStruct((2, bsz, cgrid, rgrid, width), BF16),
        scratch_shapes=[pltpu.VMEM((2, width // LANES, rgrid * tcs, LANES), F32)],
        compiler_params=_cparams("arbitrary", "arbitrary"),
        name="fnet_stage1",
    )(p4, chan, m1, twc, tws)

    cc2, sc2 = _dft_mats(cgrid)
    m2 = jnp.asarray(np.concatenate([cc2, sc2], axis=1), BF16)
    tbs = min(16, rgrid)
    out = pl.pallas_call(
        functools.partial(_fnet_stage2_kernel, tbs=tbs),
        grid=(bsz, rgrid // tbs),
        in_specs=[pl.BlockSpec((2, None, cgrid, tbs, width), lambda b, j: (0, b, 0, j, 0)),
                  pl.BlockSpec((cgrid, 2 * cgrid), lambda b, j: (0, 0))],
        out_specs=pl.BlockSpec((None, cgrid, tbs, width), lambda b, j: (b, 0, j, 0)),
        out_shape=jax.ShapeDtypeStruct((bsz, cgrid, rgrid, width), BF16),
        scratch_shapes=[pltpu.VMEM((2, width // LANES, cgrid * tbs, LANES), F32),
                        pltpu.VMEM((width // LANES, cgrid * tbs, LANES), F32)],
        compiler_params=_cparams("arbitrary", "arbitrary"),
        name="fnet_stage2",
    )(u5, m2)
    return out.reshape(m, width)


def _tail_kernel(yf_ref, yb_ref, xs_ref, z_ref, ff_ref, gf_ref, gs_ref, x_ref, g1_ref, sc2_ref, sh2_ref,
                 dsk_ref, snw_ref, wso_ref, wfo_ref, wo_ref, n2w_ref, wr_ref, br_ref,
                 x1_ref, h2_ref, ti_ref, tw_ref, rk_ref, cnt_ref, carry_scr):
    @pl.when(pl.program_id(0) == 0)
    def _():
        carry_scr[...] = jnp.zeros(carry_scr.shape, F32)

    wr = wr_ref[...]
    w_hi = wr.astype(BF16)
    w_lo = (wr - w_hi.astype(F32)).astype(BF16)
    tm = x_ref.shape[0] // TAIL_SPLIT
    lane = lax.broadcasted_iota(jnp.int32, (tm, LANES), 1)
    r_i = lax.broadcasted_iota(jnp.int32, (tm, tm), 0)
    c_i = lax.broadcasted_iota(jnp.int32, (tm, tm), 1)
    earlier = jnp.where(r_i > c_i, 1.0, 0.0).astype(BF16)

    for part in range(TAIL_SPLIT):
        rows = slice(part * tm, (part + 1) * tm)
        y = (yf_ref[0, rows, :] + yb_ref[0, rows, :]).astype(F32) + dsk_ref[...] * xs_ref[rows, :].astype(F32)
        g = y * _silu(z_ref[rows, :]).astype(F32)
        inv = lax.rsqrt(jnp.mean(g * g, axis=-1, keepdims=True) + EPS)
        gn = (g * inv * snw_ref[...]).astype(BF16)
        y_ssd = jnp.dot(gn, wso_ref[...], preferred_element_type=F32)
        y_four = jnp.dot(ff_ref[rows, :], wfo_ref[...], preferred_element_type=F32)
        t = _sigmoid(gf_ref[rows, :]) * y_four.astype(BF16) + _sigmoid(gs_ref[rows, :]) * y_ssd.astype(BF16)
        mix = jnp.dot(t, wo_ref[...], preferred_element_type=F32)
        x1 = x_ref[rows, :] + g1_ref[...] * mix
        x1_ref[rows, :] = x1
        inv2 = lax.rsqrt(jnp.mean(x1 * x1, axis=-1, keepdims=True) + EPS)
        h2 = (x1 * inv2 * n2w_ref[...]) * (1.0 + sc2_ref[...]) + sh2_ref[...]
        h2_ref[rows, :] = _pack_bf16_pair(h2)
        h_hi = h2.astype(BF16)
        h_lo = (h2 - h_hi.astype(F32)).astype(BF16)
        logits = (jnp.dot(h_hi, w_hi, preferred_element_type=F32)
                  + jnp.dot(h_hi, w_lo, preferred_element_type=F32)
                  + jnp.dot(h_lo, w_hi, preferred_element_type=F32)) + br_ref[...]
        vals, idxs = [], []
        cur = logits
        for _ in range(TOP_K):
            mx = jnp.max(cur, axis=-1, keepdims=True)
            ix = jnp.min(jnp.where(cur == mx, lane, LANES), axis=-1, keepdims=True)
            vals.append(mx)
            idxs.append(ix)
            cur = jnp.where(lane == ix, NEG_BIG * 2.0, cur)
        es = [jnp.exp(v - vals[0]) for v in vals]
        den = es[0] + es[1] + es[2] + es[3]
        ti = jnp.zeros((tm, LANES), jnp.int32)
        tw = jnp.zeros((tm, LANES), F32)
        for k in range(TOP_K):
            ti = jnp.where(lane == k, idxs[k], ti)
            tw = jnp.where(lane == k, es[k] / den, tw)
        ti_ref[rows, :] = ti
        tw_ref[rows, :] = tw

        onehots = [jnp.where(lane == ix, 1.0, 0.0) for ix in idxs]
        cnt = onehots[0] + onehots[1] + onehots[2] + onehots[3]
        before = jnp.dot(earlier, cnt.astype(BF16), preferred_element_type=F32) + carry_scr[0:1, :]
        rk = jnp.zeros((tm, LANES), jnp.int32)
        for k in range(TOP_K):
            rank_k = jnp.sum(onehots[k] * before, axis=-1, keepdims=True)
            rk = jnp.where(lane == k, rank_k.astype(jnp.int32), rk)
        rk_ref[rows, :] = rk
        carry_scr[...] = carry_scr[...] + jnp.sum(cnt, axis=0, keepdims=True)
    cnt_ref[...] = carry_scr[...]


def _tail(y2, xbc2d, p2d, ff, x2d, g1, sc2, sh2, dsk, snw, wso, wfo, wo, n2w, wr, br, rows_per_mod, tm):
    m, d = x2d.shape
    di = y2.shape[-1]
    zb = 0
    fb = di // d
    row = lambda i: (i, 0)
    modrow = lambda i: ((i * tm) // rows_per_mod, 0, 0)
    const = lambda i: (0, 0)
    in_specs = [pl.BlockSpec((1, tm, di), lambda i: (0, i, 0)),
                pl.BlockSpec((1, tm, di), lambda i: (1, i, 0)),
                pl.BlockSpec((tm, di), row),
                pl.BlockSpec((tm, di), lambda i: (i, zb)),
                pl.BlockSpec((tm, d), row),
                pl.BlockSpec((tm, d), lambda i: (i, fb + 1)),
                pl.BlockSpec((tm, d), lambda i: (i, fb + 2)),
                pl.BlockSpec((tm, d), row),
                pl.BlockSpec((None, 1, d), modrow),
                pl.BlockSpec((None, 1, d), modrow),
                pl.BlockSpec((None, 1, d), modrow),
                pl.BlockSpec((1, di), const),
                pl.BlockSpec((1, di), const),
                pl.BlockSpec((di, d), const),
                pl.BlockSpec((d, d), const),
                pl.BlockSpec((d, d), const),
                pl.BlockSpec((1, d), const),
                pl.BlockSpec((d, LANES), const),
                pl.BlockSpec((1, LANES), const)]
    out_specs = [pl.BlockSpec((tm, d), row), pl.BlockSpec((tm, d // 2), row),
                 pl.BlockSpec((tm, LANES), row), pl.BlockSpec((tm, LANES), row), pl.BlockSpec((tm, LANES), row),
                 pl.BlockSpec((8, LANES), const)]
    out_shape = [jax.ShapeDtypeStruct((m, d), F32), jax.ShapeDtypeStruct((m, d // 2), jnp.uint32),
                 jax.ShapeDtypeStruct((m, LANES), jnp.int32), jax.ShapeDtypeStruct((m, LANES), F32),
                 jax.ShapeDtypeStruct((m, LANES), jnp.int32), jax.ShapeDtypeStruct((8, LANES), F32)]
    return pl.pallas_call(
        _tail_kernel,
        grid=(m // tm,),
        in_specs=in_specs,
        out_specs=out_specs,
        out_shape=out_shape,
        scratch_shapes=[pltpu.VMEM((8, LANES), F32)],
        compiler_params=_cparams("arbitrary"),
        name="tail",
    )(y2, y2, xbc2d, p2d, ff, p2d, p2d, x2d, g1, sc2, sh2, dsk, snw, wso, wfo, wo, n2w, wr, br)


def _gather_rows(src, idx):
    n_out = idx.shape[0]
    width = src.shape[1]
    win = SC_GATHER_WINDOW
    info = plsc.get_sparse_core_info()
    n_cores, n_workers = info.num_cores, info.num_cores * info.num_subcores
    per_worker = n_out // n_workers
    assert per_worker * n_workers == n_out and per_worker % (2 * win) == 0
    mesh = plsc.VectorSubcoreMesh(core_axis_name="core", subcore_axis_name="subcore")

    @functools.partial(
        pl.kernel, out_type=jax.ShapeDtypeStruct((n_out, width), src.dtype), mesh=mesh,
        scratch_types=[pltpu.VMEM((win,), jnp.int32), pltpu.VMEM((win,), jnp.int32),
                       pltpu.VMEM((win, width), src.dtype), pltpu.VMEM((win, width), src.dtype),
                       pltpu.SemaphoreType.DMA, pltpu.SemaphoreType.DMA,
                       pltpu.SemaphoreType.DMA, pltpu.SemaphoreType.DMA],
        name="gather_rows")
    def gather(src_hbm, idx_hbm, out_hbm, idx_a, idx_b, rows_a, rows_b, gsem_a, gsem_b, ssem_a, ssem_b):
        worker = lax.axis_index("subcore") * n_cores + lax.axis_index("core")
        base = worker * per_worker

        @pl.loop(0, per_worker, step=2 * win)
        def _(off):
            pltpu.sync_copy(idx_hbm.at[pl.ds(base + off, win)], idx_a)
            ga = pltpu.async_copy(src_hbm.at[idx_a], rows_a, gsem_a)
            pltpu.sync_copy(idx_hbm.at[pl.ds(base + off + win, win)], idx_b)
            gb = pltpu.async_copy(src_hbm.at[idx_b], rows_b, gsem_b)
            ga.wait()
            sa = pltpu.async_copy(rows_a, out_hbm.at[pl.ds(base + off, win)], ssem_a)
            gb.wait()
            sb = pltpu.async_copy(rows_b, out_hbm.at[pl.ds(base + off + win, win)], ssem_b)
            sa.wait()
            sb.wait()

    return gather(src, idx)


def _scatter_rows(src, dest, n_rows):
    m, width = src.shape
    win = SC_GATHER_WINDOW
    info = plsc.get_sparse_core_info()
    n_cores, n_workers = info.num_cores, info.num_cores * info.num_subcores
    per_worker = m // n_workers
    assert per_worker * n_workers == m and per_worker % win == 0
    mesh = plsc.VectorSubcoreMesh(core_axis_name="core", subcore_axis_name="subcore")

    @functools.partial(
        pl.kernel, out_type=jax.ShapeDtypeStruct((n_rows, width), src.dtype), mesh=mesh,
        scratch_types=[pltpu.VMEM((win, width), src.dtype)]
        + [pltpu.VMEM((win,), jnp.int32)] * TOP_K + [pltpu.SemaphoreType.DMA] * TOP_K,
        name="scatter_rows")
    def scatter(src_hbm, dest_hbm, out_hbm, rows_v, *rest):
        idx_v, sems = rest[:TOP_K], rest[TOP_K:]
        worker = lax.axis_index("subcore") * n_cores + lax.axis_index("core")
        base = worker * per_worker

        @pl.loop(0, per_worker, step=win)
        def _(off):
            t0 = base + off
            pltpu.sync_copy(src_hbm.at[pl.ds(t0, win)], rows_v)
            for k in range(TOP_K):
                pltpu.sync_copy(dest_hbm.at[pl.ds(k * m + t0, win)], idx_v[k])
            copies = [pltpu.async_copy(rows_v, out_hbm.at[idx_v[k]], sems[k]) for k in range(TOP_K)]
            for cp in copies:
                cp.wait()

    return scatter(src, dest)


def _expert_kernel(te_ref, nu_ref, tv_ref, x_ref, wgu_ref, bgu_ref, wd_ref, bd_ref, o_ref, wgu_scr, wd_scr):
    i = pl.program_id(0)
    used = i < nu_ref[0]
    new_expert = (i == 0) | (te_ref[i] != te_ref[jnp.maximum(i - 1, 0)])

    @pl.when(used & new_expert)
    def _():
        wgu_scr[...] = wgu_ref[...].astype(BF16)
        wd_scr[...] = wd_ref[...].astype(BF16)

    @pl.when(used)
    def _():
        dff = wd_ref.shape[0]
        half = wgu_ref.shape[0] // 2
        rows = lax.broadcasted_iota(jnp.int32, x_ref.shape, 0)
        xa, xb = _unpack_bf16_pair(jnp.where(rows < tv_ref[i], x_ref[...], jnp.uint32(0)))
        gu = (jnp.dot(xa.astype(BF16), wgu_scr[:half, :], preferred_element_type=F32)
              + jnp.dot(xb.astype(BF16), wgu_scr[half:, :], preferred_element_type=F32)) + bgu_ref[...]
        gate = jnp.minimum(gu[:, :dff], SWIGLU_LIMIT)
        up = jnp.clip(gu[:, dff:], -SWIGLU_LIMIT, SWIGLU_LIMIT)
        act = (up + 1.0) * gate * _sigmoid(SWIGLU_ALPHA * gate)
        y = jnp.dot(act.astype(BF16), wd_scr[...], preferred_element_type=F32) + bd_ref[...]
        o_ref[...] = _pack_bf16_pair(y)

    @pl.when(jnp.logical_not(used))
    def _():
        o_ref[...] = jnp.zeros(o_ref.shape, o_ref.dtype)


def _experts(xs, tile_expert, n_used, tile_valid, wgu, bgu, wd, bd):
    rows, dh = xs.shape
    d = 2 * dh
    tm = MOE_TILE
    n_tiles = rows // tm
    dff2 = wgu.shape[-1]

    def tile(i, te, nu, tv):
        return (jnp.minimum(i, nu[0] - 1), 0)

    def wsel(i, te, nu, tv):
        return (te[jnp.minimum(i, nu[0] - 1)], 0, 0)

    grid_spec = pltpu.PrefetchScalarGridSpec(
        num_scalar_prefetch=3,
        grid=(n_tiles,),
        in_specs=[pl.BlockSpec((tm, dh), tile),
                  pl.BlockSpec((None, d, dff2), wsel),
                  pl.BlockSpec((None, 1, dff2), wsel),
                  pl.BlockSpec((None, dff2 // 2, d), wsel),
                  pl.BlockSpec((None, 1, d), wsel)],
        out_specs=pl.BlockSpec((tm, dh), lambda i, te, nu, tv: (i, 0)),
        scratch_shapes=[pltpu.VMEM((d, dff2), BF16), pltpu.VMEM((dff2 // 2, d), BF16)],
    )
    return pl.pallas_call(
        _expert_kernel,
        grid_spec=grid_spec,
        out_shape=jax.ShapeDtypeStruct((rows, dh), jnp.uint32),
        compiler_params=_cparams("arbitrary"),
        name="experts",
    )(tile_expert, n_used, tile_valid, xs, wgu, bgu, wd, bd)


def _final_kernel(x1_ref, ya_ref, yb_ref, yc_ref, yd_ref, tw_ref, g2_ref, fw_ref, *rest):
    o_ref = rest[-1]
    d = x1_ref.shape[-1]
    half = d // 2
    tw = tw_ref[...]
    acc_hi = jnp.zeros((x1_ref.shape[0], half), F32)
    acc_lo = jnp.zeros((x1_ref.shape[0], half), F32)
    for k, y_ref in enumerate((ya_ref, yb_ref, yc_ref, yd_ref)):
        y_hi, y_lo = _unpack_bf16_pair(y_ref[...])
        acc_hi = acc_hi + tw[:, k:k + 1] * y_hi
        acc_lo = acc_lo + tw[:, k:k + 1] * y_lo
    x_hi = x1_ref[:, :half] + g2_ref[:, :half] * acc_hi
    x_lo = x1_ref[:, half:] + g2_ref[:, half:] * acc_lo
    ms = (jnp.sum(x_hi * x_hi, axis=-1, keepdims=True) + jnp.sum(x_lo * x_lo, axis=-1, keepdims=True)) / d
    inv = lax.rsqrt(ms + EPS)
    o_ref[:, :half] = x_hi * inv * fw_ref[:, :half]
    o_ref[:, half:] = x_lo * inv * fw_ref[:, half:]


def _final(x1, y4, tw, g2, fw, rows_per_mod, tm, part, n_parts, out_prev):
    m, d = x1.shape
    steps = m // n_parts // tm
    off = part * steps
    in_specs = ([pl.BlockSpec((tm, d), lambda i: (i + off, 0))]
                + [pl.BlockSpec((None, tm, d // 2), functools.partial(lambda k, i: (k, i, 0), k)) for k in range(TOP_K)]
                + [pl.BlockSpec((tm, LANES), lambda i: (i + off, 0)),
                   pl.BlockSpec((None, 1, d), lambda i: (((i + off) * tm) // rows_per_mod, 0, 0)),
                   pl.BlockSpec((1, d), lambda i: (0, 0))])
    args = [x1, y4, y4, y4, y4, tw, g2, fw.reshape(1, d)]
    aliases = {}
    if out_prev is not None:
        in_specs.append(pl.BlockSpec(memory_space=pl.ANY))
        aliases = {len(args): 0}
        args.append(out_prev)
    return pl.pallas_call(
        _final_kernel,
        grid=(steps,),
        in_specs=in_specs,
        out_specs=pl.BlockSpec((tm, d), lambda i: (i + off, 0)),
        out_shape=jax.ShapeDtypeStruct((m, d), F32),
        input_output_aliases=aliases,
        compiler_params=_cparams("arbitrary"),
        name="final",
    )(*args)


def _dispatch_plan(e, rank, counts, tm):
    n_assign = e.size
    padded = (counts + tm - 1) // tm * tm
    pad_end = jnp.cumsum(padded)
    pad_start = pad_end - padded
    dest = (pad_start[e] + rank).astype(jnp.int32)
    n_tiles = n_assign // tm + N_EXPERTS
    tile_start = jnp.arange(n_tiles, dtype=jnp.int32) * tm
    tile_expert = jnp.minimum(jnp.sum(pad_end[None, :] <= tile_start[:, None], axis=1), N_EXPERTS - 1).astype(jnp.int32)
    n_used = (pad_end[-1] // tm).astype(jnp.int32).reshape(1)
    tile_valid = jnp.clip(pad_start[tile_expert] + counts[tile_expert] - tile_start, 0, tm).astype(jnp.int32)
    return dest, tile_expert, n_used, tile_valid


def kernel(x, c, ctx, c_ctx, w_mod, b_mod, norm1_w, norm2_w, w_in, conv_w, conv_b, dt_bias, a_log, d_skip,
           ssd_norm_w, w_ssd_out, w_four_out, w_o, w_router, b_router, w_gate_up, b_gate_up, w_down, b_down,
           final_norm_w):
    bsz, n, d = x.shape
    n_ctx = ctx.shape[1]
    depth = w_mod.shape[0]
    assert depth == 1, "a stacked model would also need the context stream's residual update"
    d_inner = N_HEADS * HEAD_DIM
    off_dt = d_inner + 2 * BC_WIDTH
    m = bsz * n
    x2d = x.reshape(m, d)
    tm = min(512, n)

    for layer in range(depth):
        rows = -(-(bsz + 1) // 8) * 8
        c_rows = jnp.zeros((rows, d), F32).at[:bsz].set(c).at[bsz].set(c_ctx)
        mod = _modulation(c_rows, w_mod[layer], b_mod[layer])
        mods = [mod[:bsz, i * d:(i + 1) * d].reshape(bsz, 1, d) for i in range(N_MOD)]
        sh1, sc1, g1, sh2, sc2, g2 = mods
        sh1_c = mod[bsz:bsz + 1, :d].reshape(1, 1, d)
        sc1_c = mod[bsz:bsz + 1, d:2 * d].reshape(1, 1, d)

        wl = w_in[layer]
        w_main = jnp.concatenate([wl[:, :off_dt], wl[:, off_dt + 2 * N_HEADS:]], axis=1).astype(BF16)
        w_dt = jnp.zeros((d, 2 * LANES), F32)
        w_dt = w_dt.at[:, :N_HEADS].set(wl[:, off_dt:off_dt + N_HEADS])
        w_dt = w_dt.at[:, LANES:LANES + N_HEADS].set(wl[:, off_dt + N_HEADS:off_dt + 2 * N_HEADS]).astype(BF16)
        pad = jnp.zeros((2, 1, LANES - N_HEADS), F32)
        dt_bias2 = jnp.concatenate([dt_bias[layer].reshape(2, 1, N_HEADS), pad], axis=-1)
        a_log2 = jnp.concatenate([a_log[layer].reshape(2, 1, N_HEADS), pad], axis=-1)

        n_conv = off_dt + 2 * BC_WIDTH
        w_ctx = w_main[:, :off_dt]
        xbc_c, _, dtc = _in_proj_conv(ctx.reshape(bsz * n_ctx, d), norm1_w[layer], sc1_c, sh1_c, w_ctx, w_dt,
                                      conv_w[layer][:, :off_dt], conv_b[layer][:off_dt], off_dt, n_ctx,
                                      bsz * n_ctx, min(512, n_ctx))
        h0 = jnp.zeros((bsz, 2, N_GROUPS, D_STATE, HEADS_PER_GROUP * HEAD_DIM), F32)
        h_ctx = _ssd(xbc_c.reshape(bsz, n_ctx, off_dt), dtc.reshape(bsz, n_ctx, 2 * LANES), dt_bias2, a_log2, h0,
                     with_y=False)

        xbc, p, dtl = _in_proj_conv(x2d, norm1_w[layer], sc1, sh1, w_main, w_dt, conv_w[layer], conv_b[layer],
                                    n_conv, n, n, tm)
        xbc = xbc.reshape(bsz, n, n_conv)
        y2 = _ssd(xbc, dtl.reshape(bsz, n, 2 * LANES), dt_bias2, a_log2, h_ctx, with_y=True)
        ff = _fnet_mix(p, d_inner // d, bsz, n)

        dsk = jnp.repeat(d_skip[layer].astype(F32), HEAD_DIM).reshape(1, d_inner)
        wr = jnp.zeros((d, LANES), F32).at[:, :N_EXPERTS].set(w_router[layer])
        br = jnp.full((1, LANES), NEG_BIG, F32).at[0, :N_EXPERTS].set(b_router[layer])
        x1, h2, ti, tw, rk, cnt = _tail(
            y2.reshape(2, m, d_inner), xbc.reshape(m, n_conv), p, ff, x2d, g1, sc2, sh2, dsk,
            ssd_norm_w[layer].reshape(1, d_inner), w_ssd_out[layer].astype(BF16), w_four_out[layer].astype(BF16),
            w_o[layer].astype(BF16), norm2_w[layer].reshape(1, d), wr, br, n, min(TAIL_ROWS, n))

        counts = cnt[0, :N_EXPERTS].astype(jnp.int32)
        dest, tile_expert, n_used, tile_valid = _dispatch_plan(ti[:, :TOP_K], rk[:, :TOP_K], counts, MOE_TILE)
        dest_km = dest.T.reshape(-1)
        xs = _scatter_rows(h2, dest_km, tile_expert.shape[0] * MOE_TILE)
        ys = _experts(xs, tile_expert, n_used, tile_valid, w_gate_up[layer],
                      b_gate_up[layer].reshape(N_EXPERTS, 1, -1), w_down[layer],
                      b_down[layer].reshape(N_EXPERTS, 1, -1))
        mp = m // COMBINE_PARTS
        dest_parts = dest.T.reshape(TOP_K, COMBINE_PARTS, mp)
        x2d = None
        for part in range(COMBINE_PARTS):
            y4 = _gather_rows(ys, dest_parts[:, part].reshape(-1)).reshape(TOP_K, mp, d // 2)
            x2d = _final(x1, y4, tw, g2, final_norm_w, n, min(tm, mp), part, COMBINE_PARTS, x2d)
    return x2d.reshape(bsz, n, d)
```

```python
import functools
import math

import numpy as np
import jax
import jax.numpy as jnp
from jax import lax
from jax.experimental import pallas as pl
from jax.experimental.pallas import tpu as pltpu
from jax.experimental.pallas import tpu_sc as plsc

F32 = jnp.float32
BF16 = jnp.bfloat16

EPS = 1e-6
GRID_W = 64
N_MOD = 6
F_GROUPS = 8
F_GROUP_DIM = 128
HEAD_DIM = 64
N_HEADS = 32
N_GROUPS = 4
HEADS_PER_GROUP = N_HEADS // N_GROUPS
D_STATE = 128
BC_WIDTH = N_GROUPS * D_STATE
CONV_K = 5
CHUNK = 128
N_EXPERTS = 32
TOP_K = 4
SWIGLU_LIMIT = 7.0
SWIGLU_ALPHA = 1.702

LANES = 128
VMEM_LIMIT_BYTES = 56 * 1024 * 1024
NEG_BIG = -1e30
MOE_TILE = 512
TAIL_ROWS = 512
SC_GATHER_WINDOW = 64


def _cparams(*sem):
    return pltpu.CompilerParams(dimension_semantics=sem, vmem_limit_bytes=VMEM_LIMIT_BYTES)


def _sigmoid(v):
    return 1.0 / (1.0 + jnp.exp(-v))


def _silu(v):
    return v * _sigmoid(v)


def _softplus(v):
    return jnp.maximum(v, 0.0) + jnp.log(1.0 + jnp.exp(-jnp.abs(v)))


def _pack_bf16_pair(v):
    w = v.shape[1] // 2
    bits = lax.bitcast_convert_type(v.astype(BF16).astype(F32), jnp.uint32)
    return bits[:, :w] | (bits[:, w:] >> 16)


def _unpack_bf16_pair(p):
    hi = lax.bitcast_convert_type(p & jnp.uint32(0xFFFF0000), F32)
    lo = lax.bitcast_convert_type(p << 16, F32)
    return hi, lo


def _mod_kernel(c_ref, w_ref, b_ref, o_ref):
    s = _silu(c_ref[...]).astype(BF16)
    o_ref[...] = jnp.dot(s, w_ref[...].astype(BF16), preferred_element_type=F32) + b_ref[...]


def _modulation(c_rows, w_mod, b_mod):
    rows, d = c_rows.shape
    n_out = w_mod.shape[1]
    tn = 1024
    return pl.pallas_call(
        _mod_kernel,
        grid=(n_out // tn,),
        in_specs=[pl.BlockSpec((rows, d), lambda j: (0, 0)),
                  pl.BlockSpec((d, tn), lambda j: (0, j)),
                  pl.BlockSpec((1, tn), lambda j: (0, j))],
        out_specs=pl.BlockSpec((rows, tn), lambda j: (0, j)),
        out_shape=jax.ShapeDtypeStruct((rows, n_out), F32),
        compiler_params=_cparams("arbitrary"),
        name="modulation",
    )(c_rows, w_mod, b_mod.reshape(1, n_out))


_INPROJ_TN = 1024


def _inproj_kernel(x_ref, nw_ref, sc_ref, sh_ref, w_ref, wdt_ref, p_ref, dt_ref):
    x = x_ref[...]
    inv = lax.rsqrt(jnp.mean(x * x, axis=-1, keepdims=True) + EPS)
    h = (x * inv * nw_ref[...]) * (1.0 + sc_ref[...]) + sh_ref[...]
    hb = h.astype(BF16)
    dt_ref[...] = jnp.dot(hb, wdt_ref[...], preferred_element_type=F32)
    for j in range(w_ref.shape[1] // _INPROJ_TN):
        cols = slice(j * _INPROJ_TN, (j + 1) * _INPROJ_TN)
        p_ref[:, cols] = jnp.dot(hb, w_ref[:, cols], preferred_element_type=F32).astype(p_ref.dtype)


def _in_proj(x2d, norm_w, sc, sh, w_main, w_dt, rows_per_mod, tm):
    m, d = x2d.shape
    n = w_main.shape[1]
    ndt = w_dt.shape[1]
    resident = dict(pipeline_mode=pl.Buffered(1))
    return pl.pallas_call(
        _inproj_kernel,
        grid=(m // tm,),
        in_specs=[pl.BlockSpec((tm, d), lambda i: (i, 0)),
                  pl.BlockSpec((1, d), lambda i: (0, 0), **resident),
                  pl.BlockSpec((None, 1, d), lambda i: ((i * tm) // rows_per_mod, 0, 0)),
                  pl.BlockSpec((None, 1, d), lambda i: ((i * tm) // rows_per_mod, 0, 0)),
                  pl.BlockSpec((d, n), lambda i: (0, 0), **resident),
                  pl.BlockSpec((d, ndt), lambda i: (0, 0), **resident)],
        out_specs=[pl.BlockSpec((tm, n), lambda i: (i, 0)),
                   pl.BlockSpec((tm, ndt), lambda i: (i, 0))],
        out_shape=[jax.ShapeDtypeStruct((m, n), BF16),
                   jax.ShapeDtypeStruct((m, ndt), F32)],
        compiler_params=_cparams("arbitrary"),
        name="in_proj",
    )(x2d, norm_w.reshape(1, d), sc, sh, w_main, w_dt)


_CONV_HALO = 16


_CONV_ROWS = 64


def _conv_shift_matrix(rc):
    win = rc + 2 * _CONV_HALO
    s = np.zeros((rc, CONV_K * win), np.float32)
    for k in range(CONV_K):
        for l in range(rc):
            s[l, k * win + _CONV_HALO + l + k - CONV_K // 2] = 1.0
    return jnp.asarray(s, BF16)


def _conv_kernel(p_ref, w_ref, b_ref, s_ref, o_ref, *, n, rc):
    tc = o_ref.shape[-1]
    w = w_ref[...].astype(BF16)
    bias = b_ref[...]
    smat = s_ref[...]
    zeros = jnp.zeros((_CONV_HALO, tc), BF16)
    for r0 in range(0, n, rc):
        top = zeros if r0 == 0 else p_ref[r0 - _CONV_HALO:r0, :]
        bot = zeros if r0 + rc >= n else p_ref[r0 + rc:r0 + rc + _CONV_HALO, :]
        window = jnp.concatenate([top, p_ref[r0:r0 + rc, :], bot], axis=0)
        taps = jnp.concatenate([window * w[k:k + 1, :] for k in range(CONV_K)], axis=0)
        acc = jnp.dot(smat, taps, preferred_element_type=F32) + bias
        o_ref[r0:r0 + rc, :] = _silu(acc).astype(o_ref.dtype)


def _conv_silu(p3d, conv_w, conv_b, n_ch):
    bsz, n, _ = p3d.shape
    tc = 512
    rc = min(_CONV_ROWS, n)
    smat = _conv_shift_matrix(rc)
    return pl.pallas_call(
        functools.partial(_conv_kernel, n=n, rc=rc),
        grid=(bsz, n_ch // tc),
        in_specs=[pl.BlockSpec((None, n, tc), lambda b, j: (b, 0, j)),
                  pl.BlockSpec((CONV_K, tc), lambda b, j: (0, j)),
                  pl.BlockSpec((1, tc), lambda b, j: (0, j)),
                  pl.BlockSpec(smat.shape, lambda b, j: (0, 0))],
        out_specs=pl.BlockSpec((None, n, tc), lambda b, j: (b, 0, j)),
        out_shape=jax.ShapeDtypeStruct((bsz, n, n_ch), BF16),
        compiler_params=_cparams("arbitrary", "arbitrary"),
        name="conv_silu",
    )(p3d, conv_w, conv_b.reshape(1, -1), smat)


LOG2E = 1.4426950408889634
SSD_SUB = 8


def _head_expand_matrix():
    e = np.zeros((LANES, N_HEADS * HEAD_DIM), np.float32)
    for h in range(N_HEADS):
        e[h, h * HEAD_DIM:(h + 1) * HEAD_DIM] = 1.0
    return jnp.asarray(e, BF16)


def _ssd_chunk_terms(dt_raw, dt_bias, a_log, expand, tri, fwd):
    q = CHUNK
    dt = _softplus(dt_raw + dt_bias)
    da = dt * (-jnp.exp(a_log))
    hi = da.astype(BF16)
    r1 = da - hi.astype(F32)
    mid = r1.astype(BF16)
    lo = (r1 - mid.astype(F32)).astype(BF16)
    ones_tri = jnp.where(tri, 1.0, 0.0).astype(BF16)
    cs3 = jnp.dot(ones_tri, jnp.concatenate([hi, mid, lo], axis=1), preferred_element_type=F32)
    cs = cs3[:, :LANES] + cs3[:, LANES:2 * LANES] + cs3[:, 2 * LANES:]
    tot = jnp.where(fwd, cs[q - 1:q, :], cs[0:1, :])
    dte = dt * jnp.exp(tot - cs)
    dec = jnp.exp(tot)
    dec_hi = dec.astype(BF16)
    dec_lo = (dec - dec_hi.astype(F32)).astype(BF16)
    dec2 = jnp.concatenate([dec_hi, dec_lo, jnp.zeros((6, LANES), BF16)], axis=0)
    stacked = jnp.concatenate([jnp.exp(cs).astype(BF16), dte.astype(BF16), dec2], axis=0)
    spread = jnp.dot(stacked, expand, preferred_element_type=F32)
    ecs_e = spread[:q]
    dte_e = spread[q:2 * q].astype(BF16)
    dec_e = jnp.broadcast_to(spread[2 * q:2 * q + 1] + spread[2 * q + 1:2 * q + 2], (8, spread.shape[1]))
    cs2 = cs * LOG2E
    return cs2, cs2.T, dt.T, ecs_e, dte_e, dec_e


def _ssd_kernel(*refs, with_y, nc, n_sub):
    if with_y:
        (x_ref, b_ref, c_ref, dt_ref, dtn_ref, dtb_ref, alog_ref, exp_ref, h0_ref, y_ref,
         s_scr, cs_scr, dee_scr, dce_scr, ecs_scr) = refs
    else:
        (x_ref, b_ref, dt_ref, dtn_ref, dtb_ref, alog_ref, exp_ref, h0_ref, hout_ref,
         s_scr, cs_scr, dee_scr, dce_scr) = refs
    q = CHUNK
    gw = HEADS_PER_GROUP * HEAD_DIM
    d = pl.program_id(1)
    c = pl.program_id(2)
    row = lax.broadcasted_iota(jnp.int32, (q, q), 0)
    col = lax.broadcasted_iota(jnp.int32, (q, q), 1)
    fwd = d == 0
    tri = jnp.where(fwd, row - col, col - row) >= 0
    first_half = col < HEAD_DIM

    def store_terms(slot, terms):
        cs2, cs2_t, dt_t, ecs_e, dte_e, dec_e = terms
        cs_scr[slot, 0] = cs2
        cs_scr[slot, 1] = cs2_t
        cs_scr[slot, 2] = dt_t
        dee_scr[slot] = dte_e
        dce_scr[slot] = dec_e
        if with_y:
            ecs_scr[slot] = ecs_e

    def terms_of(ref, sub):
        rows = pl.ds(pl.multiple_of(sub * q, q), q)
        return _ssd_chunk_terms(ref[rows, :], dtb_ref[...], alog_ref[...], exp_ref[...], tri, fwd)

    def sub_of(j):
        return jnp.where(fwd, j, n_sub - 1 - j)

    @pl.when(c == 0)
    def _():
        s_scr[...] = h0_ref[...]
        store_terms(0, terms_of(dt_ref, sub_of(0)))

    for j in range(n_sub):
        slot = j % 2
        rows = pl.ds(pl.multiple_of(sub_of(j) * q, q), q)
        cs2 = cs_scr[slot, 0]
        cs2_t = cs_scr[slot, 1]
        dt_t = cs_scr[slot, 2]
        if j + 1 < n_sub:
            store_terms(1 - slot, terms_of(dt_ref, sub_of(j + 1)))
        else:
            store_terms(1 - slot, terms_of(dtn_ref, sub_of(0)))

        for g in range(N_GROUPS):
            cols = slice(g * gw, (g + 1) * gw)
            bg = b_ref[rows, g * D_STATE:(g + 1) * D_STATE]
            xg = x_ref[rows, cols]
            s_old = s_scr[g]
            if with_y:
                cg = c_ref[rows, g * D_STATE:(g + 1) * D_STATE]
                cb = lax.dot_general(cg, bg, (((1,), (1,)), ((), ())), preferred_element_type=F32)
                y_off = jnp.dot(cg, s_old.astype(BF16), preferred_element_type=F32)
                for k in range(HEADS_PER_GROUP // 2):
                    pair = g * (HEADS_PER_GROUP // 2) + k
                    lanes = slice(pair * LANES, (pair + 1) * LANES)
                    xp = xg[:, k * LANES:(k + 1) * LANES]
                    xz = jnp.zeros_like(xp)
                    lhs = []
                    for hh in (2 * pair, 2 * pair + 1):
                        colb = jnp.broadcast_to(cs2[:, hh:hh + 1], (q, q))
                        rowb = cs2_t[hh:hh + 1, :]
                        decay = jnp.exp2(jnp.where(tri, colb - rowb, -jnp.inf))
                        lhs.append((cb * decay * dt_t[hh:hh + 1, :]).astype(BF16))
                    y_diag = jnp.dot(jnp.concatenate(lhs, axis=1),
                                     jnp.concatenate([jnp.where(first_half, xp, xz), jnp.where(first_half, xz, xp)],
                                                     axis=0),
                                     preferred_element_type=F32)
                    y = y_diag + ecs_scr[slot, :, lanes] * y_off[:, k * LANES:(k + 1) * LANES]
                    y_ref[rows, lanes] = y.astype(y_ref.dtype)
            xdte = xg * dee_scr[slot, :, cols]
            upd = lax.dot_general(bg, xdte, (((0,), (0,)), ((), ())), preferred_element_type=F32)
            s_scr[g] = s_old * dce_scr[slot, 0:1, cols] + upd

    if not with_y:
        @pl.when(c == nc - 1)
        def _():
            hout_ref[...] = s_scr[...]


def _ssd(xbc, dt_raw, dt_bias2, a_log2, h0, with_y):
    bsz, n, _ = xbc.shape
    n_sub = min(SSD_SUB, n // CHUNK)
    assert n_sub % 2 == 0 and n % (n_sub * CHUNK) == 0
    blk = n_sub * CHUNK
    nc = n // blk
    d_inner = N_HEADS * HEAD_DIM
    gw = HEADS_PER_GROUP * HEAD_DIM
    x_blk = d_inner // BC_WIDTH

    def ceff(dd, cc):
        return cc + dd * (nc - 1 - 2 * cc)

    def cnext(dd, cc):
        return ceff(dd, jnp.minimum(cc + 1, nc - 1))

    in_specs = [pl.BlockSpec((None, blk, d_inner), lambda b, dd, cc: (b, ceff(dd, cc), 0)),
                pl.BlockSpec((None, blk, BC_WIDTH), lambda b, dd, cc: (b, ceff(dd, cc), x_blk + dd))]
    args = [xbc, xbc]
    if with_y:
        in_specs.append(pl.BlockSpec((None, blk, BC_WIDTH), lambda b, dd, cc: (b, ceff(dd, cc), x_blk + 2 + dd)))
        args.append(xbc)
    in_specs += [pl.BlockSpec((None, blk, LANES), lambda b, dd, cc: (b, ceff(dd, cc), dd)),
                 pl.BlockSpec((None, blk, LANES), lambda b, dd, cc: (b, cnext(dd, cc), dd)),
                 pl.BlockSpec((None, 1, LANES), lambda b, dd, cc: (dd, 0, 0)),
                 pl.BlockSpec((None, 1, LANES), lambda b, dd, cc: (dd, 0, 0)),
                 pl.BlockSpec((LANES, d_inner), lambda b, dd, cc: (0, 0)),
                 pl.BlockSpec((None, None, N_GROUPS, D_STATE, gw), lambda b, dd, cc: (b, dd, 0, 0, 0))]
    args += [dt_raw, dt_raw, dt_bias2, a_log2, _head_expand_matrix(), h0]
    scratch = [pltpu.VMEM((N_GROUPS, D_STATE, gw), F32),
               pltpu.VMEM((2, 3, CHUNK, LANES), F32),
               pltpu.VMEM((2, CHUNK, d_inner), BF16),
               pltpu.VMEM((2, 8, d_inner), F32)]
    if with_y:
        out_specs = pl.BlockSpec((None, None, blk, d_inner), lambda b, dd, cc: (dd, b, ceff(dd, cc), 0))
        out_shape = jax.ShapeDtypeStruct((2, bsz, n, d_inner), BF16)
        scratch += [pltpu.VMEM((2, CHUNK, d_inner), F32)]
    else:
        out_specs = pl.BlockSpec((None, None, N_GROUPS, D_STATE, gw), lambda b, dd, cc: (b, dd, 0, 0, 0))
        out_shape = jax.ShapeDtypeStruct((bsz, 2, N_GROUPS, D_STATE, gw), F32)
    return pl.pallas_call(
        functools.partial(_ssd_kernel, with_y=with_y, nc=nc, n_sub=n_sub),
        grid=(bsz, 2, nc),
        in_specs=in_specs,
        out_specs=out_specs,
        out_shape=out_shape,
        scratch_shapes=scratch,
        compiler_params=_cparams("arbitrary", "arbitrary", "arbitrary"),
        name="ssd_scan" if with_y else "ssd_ctx_state",
    )(*args)


def _dft_mats(n):
    k = np.arange(n)
    ang = 2.0 * np.pi * ((k[:, None] * k[None, :]) % n) / n
    return np.cos(ang), np.sin(ang)


def _stage_lane_blocks(src_ref, scr, rows):
    width = src_ref.shape[-1]
    for t in range(2):
        v = src_ref[t].reshape(rows, width).astype(F32)
        for lb in range(width // LANES):
            scr[t, lb] = v[:, lb * LANES:(lb + 1) * LANES]


def _strided_rows(scr, t, start, count, stride, width):
    return jnp.concatenate([scr[t, lb, pl.ds(start, count, stride=stride), :] for lb in range(width // LANES)],
                           axis=1)


def _fnet_stage1_kernel(u_ref, chan_ref, m1_ref, twc_ref, tws_ref, o_ref, zs_scr, *, tcs):
    r = u_ref.shape[0]
    width = u_ref.shape[-1]
    u2 = u_ref[...].reshape(r * tcs, width)
    chan = chan_ref[...]
    for g in range(F_GROUPS):
        pq = jnp.dot(u2[:, g * F_GROUP_DIM:(g + 1) * F_GROUP_DIM], chan, preferred_element_type=F32)
        zs_scr[0, g] = pq[:, :F_GROUP_DIM]
        zs_scr[1, g] = pq[:, F_GROUP_DIM:]
    m1 = m1_ref[...]
    for j in range(tcs):
        zz = jnp.concatenate([_strided_rows(zs_scr, 0, j, r, tcs, width),
                              _strided_rows(zs_scr, 1, j, r, tcs, width)], axis=0).astype(BF16)
        u = jnp.dot(m1, zz, preferred_element_type=F32)
        ur = u[:r]
        ui = u[r:]
        tc = twc_ref[j]
        ts = tws_ref[j]
        o_ref[0, j] = (ur * tc + ui * ts).astype(o_ref.dtype)
        o_ref[1, j] = (ui * tc - ur * ts).astype(o_ref.dtype)


def _fnet_stage2_kernel(u_ref, m2_ref, o_ref, us_scr, os_scr, *, tbs):
    c = u_ref.shape[1]
    width = u_ref.shape[-1]
    _stage_lane_blocks(u_ref, us_scr, c * tbs)
    m2 = m2_ref[...]
    for j in range(tbs):
        uu = jnp.concatenate([_strided_rows(us_scr, 0, j, c, tbs, width),
                              _strided_rows(us_scr, 1, j, c, tbs, width)], axis=0).astype(BF16)
        x = jnp.dot(m2, uu, preferred_element_type=F32)
        for lb in range(width // LANES):
            os_scr[lb, pl.ds(j, c, stride=tbs), :] = x[:, lb * LANES:(lb + 1) * LANES]
    out = jnp.concatenate([os_scr[lb] for lb in range(width // LANES)], axis=1)
    o_ref[...] = out.reshape(c, tbs, width).astype(o_ref.dtype)


def _fnet_mix(p2d, col_blk, bsz, n):
    m = p2d.shape[0]
    width = F_GROUPS * F_GROUP_DIM
    cgrid = GRID_W
    rgrid = n // cgrid
    scale = 1.0 / math.sqrt(n * F_GROUP_DIM)

    cc, sc = _dft_mats(F_GROUP_DIM)
    chan = jnp.asarray(np.concatenate([cc, sc], axis=1) * scale, BF16)

    cr, sr = _dft_mats(rgrid)
    m1 = jnp.asarray(np.block([[cr, -sr], [-sr, -cr]]), BF16)
    bb = np.arange(rgrid)[None, :]
    ci = np.arange(cgrid)[:, None]
    ang = 2.0 * np.pi * ((ci * bb) % n) / n
    twc = jnp.asarray(np.cos(ang)[:, :, None], F32)
    tws = jnp.asarray(np.sin(ang)[:, :, None], F32)
    tcs = 16
    p4 = p2d.reshape(bsz, rgrid, cgrid, p2d.shape[-1])
    u5 = pl.pallas_call(
        functools.partial(_fnet_stage1_kernel, tcs=tcs),
        grid=(bsz, cgrid // tcs),
        in_specs=[pl.BlockSpec((None, rgrid, tcs, width), lambda b, j: (b, 0, j, col_blk)),
                  pl.BlockSpec((F_GROUP_DIM, 2 * F_GROUP_DIM), lambda b, j: (0, 0)),
                  pl.BlockSpec((2 * rgrid, 2 * rgrid), lambda b, j: (0, 0)),
                  pl.BlockSpec((tcs, rgrid, 1), lambda b, j: (j, 0, 0)),
                  pl.BlockSpec((tcs, rgrid, 1), lambda b, j: (j, 0, 0))],
        out_specs=pl.BlockSpec((2, None, tcs, rgrid, width), lambda b, j: (0, b, j, 0, 0)),
        out_shape=jax.ShapeDtypeStruct((2, bsz, cgrid, rgrid, width), BF16),
        scratch_shapes=[pltpu.VMEM((2, width // LANES, rgrid * tcs, LANES), F32)],
        compiler_params=_cparams("arbitrary", "arbitrary"),
        name="fnet_stage1",
    )(p4, chan, m1, twc, tws)

    cc2, sc2 = _dft_mats(cgrid)
    m2 = jnp.asarray(np.concatenate([cc2, sc2], axis=1), BF16)
    tbs = min(16, rgrid)
    out = pl.pallas_call(
        functools.partial(_fnet_stage2_kernel, tbs=tbs),
        grid=(bsz, rgrid // tbs),
        in_specs=[pl.BlockSpec((2, None, cgrid, tbs, width), lambda b, j: (0, b, 0, j, 0)),
                  pl.BlockSpec((cgrid, 2 * cgrid), lambda b, j: (0, 0))],
        out_specs=pl.BlockSpec((None, cgrid, tbs, width), lambda b, j: (b, 0, j, 0)),
        out_shape=jax.ShapeDtypeStruct((bsz, cgrid, rgrid, width), BF16),
        scratch_shapes=[pltpu.VMEM((2, width // LANES, cgrid * tbs, LANES), F32),
                        pltpu.VMEM((width // LANES, cgrid * tbs, LANES), F32)],
        compiler_params=_cparams("arbitrary", "arbitrary"),
        name="fnet_stage2",
    )(u5, m2)
    return out.reshape(m, width)


def _tail_kernel(yf_ref, yb_ref, xs_ref, z_ref, ff_ref, gf_ref, gs_ref, x_ref, g1_ref, sc2_ref, sh2_ref,
                 dsk_ref, snw_ref, wso_ref, wfo_ref, wo_ref, n2w_ref, wr_ref, br_ref,
                 x1_ref, h2_ref, ti_ref, tw_ref, rk_ref, cnt_ref, carry_scr):
    y = (yf_ref[0] + yb_ref[0]).astype(F32) + dsk_ref[...] * xs_ref[...].astype(F32)
    g = y * _silu(z_ref[...]).astype(F32)
    inv = lax.rsqrt(jnp.mean(g * g, axis=-1, keepdims=True) + EPS)
    gn = (g * inv * snw_ref[...]).astype(BF16)
    y_ssd = jnp.dot(gn, wso_ref[...], preferred_element_type=F32)
    y_four = jnp.dot(ff_ref[...], wfo_ref[...], preferred_element_type=F32)
    t = _sigmoid(gf_ref[...]) * y_four.astype(BF16) + _sigmoid(gs_ref[...]) * y_ssd.astype(BF16)
    mix = jnp.dot(t, wo_ref[...], preferred_element_type=F32)
    x1 = x_ref[...] + g1_ref[...] * mix
    x1_ref[...] = x1
    inv2 = lax.rsqrt(jnp.mean(x1 * x1, axis=-1, keepdims=True) + EPS)
    h2 = (x1 * inv2 * n2w_ref[...]) * (1.0 + sc2_ref[...]) + sh2_ref[...]
    h2_ref[...] = _pack_bf16_pair(h2)
    h_hi = h2.astype(BF16)
    h_lo = (h2 - h_hi.astype(F32)).astype(BF16)
    wr = wr_ref[...]
    w_hi = wr.astype(BF16)
    w_lo = (wr - w_hi.astype(F32)).astype(BF16)
    logits = (jnp.dot(h_hi, w_hi, preferred_element_type=F32)
              + jnp.dot(h_hi, w_lo, preferred_element_type=F32)
              + jnp.dot(h_lo, w_hi, preferred_element_type=F32)) + br_ref[...]
    tm = logits.shape[0]
    lane = lax.broadcasted_iota(jnp.int32, (tm, LANES), 1)
    vals, idxs = [], []
    cur = logits
    for _ in range(TOP_K):
        mx = jnp.max(cur, axis=-1, keepdims=True)
        ix = jnp.min(jnp.where(cur == mx, lane, LANES), axis=-1, keepdims=True)
        vals.append(mx)
        idxs.append(ix)
        cur = jnp.where(lane == ix, NEG_BIG * 2.0, cur)
    es = [jnp.exp(v - vals[0]) for v in vals]
    den = es[0] + es[1] + es[2] + es[3]
    ti = jnp.zeros((tm, LANES), jnp.int32)
    tw = jnp.zeros((tm, LANES), F32)
    for k in range(TOP_K):
        ti = jnp.where(lane == k, idxs[k], ti)
        tw = jnp.where(lane == k, es[k] / den, tw)
    ti_ref[...] = ti
    tw_ref[...] = tw

    @pl.when(pl.program_id(0) == 0)
    def _():
        carry_scr[...] = jnp.zeros(carry_scr.shape, F32)

    onehots = [jnp.where(lane == ix, 1.0, 0.0) for ix in idxs]
    cnt = onehots[0] + onehots[1] + onehots[2] + onehots[3]
    r_i = lax.broadcasted_iota(jnp.int32, (tm, tm), 0)
    c_i = lax.broadcasted_iota(jnp.int32, (tm, tm), 1)
    earlier = jnp.where(r_i > c_i, 1.0, 0.0).astype(BF16)
    before = jnp.dot(earlier, cnt.astype(BF16), preferred_element_type=F32) + carry_scr[0:1, :]
    rk = jnp.zeros((tm, LANES), jnp.int32)
    for k in range(TOP_K):
        rank_k = jnp.sum(onehots[k] * before, axis=-1, keepdims=True)
        rk = jnp.where(lane == k, rank_k.astype(jnp.int32), rk)
    rk_ref[...] = rk
    total = carry_scr[...] + jnp.sum(cnt, axis=0, keepdims=True)
    carry_scr[...] = total
    cnt_ref[...] = total


def _tail(y2, xbc2d, p2d, ff, x2d, g1, sc2, sh2, dsk, snw, wso, wfo, wo, n2w, wr, br, rows_per_mod, tm):
    m, d = x2d.shape
    di = y2.shape[-1]
    zb = (2 * di) // di
    fb = (3 * di) // d
    row = lambda i: (i, 0)
    modrow = lambda i: ((i * tm) // rows_per_mod, 0, 0)
    const = lambda i: (0, 0)
    in_specs = [pl.BlockSpec((1, tm, di), lambda i: (0, i, 0)),
                pl.BlockSpec((1, tm, di), lambda i: (1, i, 0)),
                pl.BlockSpec((tm, di), row),
                pl.BlockSpec((tm, di), lambda i: (i, zb)),
                pl.BlockSpec((tm, d), row),
                pl.BlockSpec((tm, d), lambda i: (i, fb + 1)),
                pl.BlockSpec((tm, d), lambda i: (i, fb + 2)),
                pl.BlockSpec((tm, d), row),
                pl.BlockSpec((None, 1, d), modrow),
                pl.BlockSpec((None, 1, d), modrow),
                pl.BlockSpec((None, 1, d), modrow),
                pl.BlockSpec((1, di), const),
                pl.BlockSpec((1, di), const),
                pl.BlockSpec((di, d), const),
                pl.BlockSpec((d, d), const),
                pl.BlockSpec((d, d), const),
                pl.BlockSpec((1, d), const),
                pl.BlockSpec((d, LANES), const),
                pl.BlockSpec((1, LANES), const)]
    out_specs = [pl.BlockSpec((tm, d), row), pl.BlockSpec((tm, d // 2), row),
                 pl.BlockSpec((tm, LANES), row), pl.BlockSpec((tm, LANES), row), pl.BlockSpec((tm, LANES), row),
                 pl.BlockSpec((8, LANES), const)]
    out_shape = [jax.ShapeDtypeStruct((m, d), F32), jax.ShapeDtypeStruct((m, d // 2), jnp.uint32),
                 jax.ShapeDtypeStruct((m, LANES), jnp.int32), jax.ShapeDtypeStruct((m, LANES), F32),
                 jax.ShapeDtypeStruct((m, LANES), jnp.int32), jax.ShapeDtypeStruct((8, LANES), F32)]
    return pl.pallas_call(
        _tail_kernel,
        grid=(m // tm,),
        in_specs=in_specs,
        out_specs=out_specs,
        out_shape=out_shape,
        scratch_shapes=[pltpu.VMEM((8, LANES), F32)],
        compiler_params=_cparams("arbitrary"),
        name="tail",
    )(y2, y2, xbc2d, p2d, ff, p2d, p2d, x2d, g1, sc2, sh2, dsk, snw, wso, wfo, wo, n2w, wr, br)


def _gather_rows(src, idx):
    n_out = idx.shape[0]
    width = src.shape[1]
    win = SC_GATHER_WINDOW
    info = plsc.get_sparse_core_info()
    n_cores, n_workers = info.num_cores, info.num_cores * info.num_subcores
    per_worker = n_out // n_workers
    assert per_worker * n_workers == n_out and per_worker % (2 * win) == 0
    mesh = plsc.VectorSubcoreMesh(core_axis_name="core", subcore_axis_name="subcore")

    @functools.partial(
        pl.kernel, out_type=jax.ShapeDtypeStruct((n_out, width), src.dtype), mesh=mesh,
        scratch_types=[pltpu.VMEM((win,), jnp.int32), pltpu.VMEM((win,), jnp.int32),
                       pltpu.VMEM((win, width), src.dtype), pltpu.VMEM((win, width), src.dtype),
                       pltpu.SemaphoreType.DMA, pltpu.SemaphoreType.DMA,
                       pltpu.SemaphoreType.DMA, pltpu.SemaphoreType.DMA],
        name="gather_rows")
    def gather(src_hbm, idx_hbm, out_hbm, idx_a, idx_b, rows_a, rows_b, gsem_a, gsem_b, ssem_a, ssem_b):
        worker = lax.axis_index("subcore") * n_cores + lax.axis_index("core")
        base = worker * per_worker

        @pl.loop(0, per_worker, step=2 * win)
        def _(off):
            pltpu.sync_copy(idx_hbm.at[pl.ds(base + off, win)], idx_a)
            ga = pltpu.async_copy(src_hbm.at[idx_a], rows_a, gsem_a)
            pltpu.sync_copy(idx_hbm.at[pl.ds(base + off + win, win)], idx_b)
            gb = pltpu.async_copy(src_hbm.at[idx_b], rows_b, gsem_b)
            ga.wait()
            sa = pltpu.async_copy(rows_a, out_hbm.at[pl.ds(base + off, win)], ssem_a)
            gb.wait()
            sb = pltpu.async_copy(rows_b, out_hbm.at[pl.ds(base + off + win, win)], ssem_b)
            sa.wait()
            sb.wait()

    return gather(src, idx)


def _scatter_rows(src, dest, n_rows):
    m, width = src.shape
    win = SC_GATHER_WINDOW
    info = plsc.get_sparse_core_info()
    n_cores, n_workers = info.num_cores, info.num_cores * info.num_subcores
    per_worker = m // n_workers
    assert per_worker * n_workers == m and per_worker % win == 0
    mesh = plsc.VectorSubcoreMesh(core_axis_name="core", subcore_axis_name="subcore")

    @functools.partial(
        pl.kernel, out_type=jax.ShapeDtypeStruct((n_rows, width), src.dtype), mesh=mesh,
        scratch_types=[pltpu.VMEM((win, width), src.dtype)]
        + [pltpu.VMEM((win,), jnp.int32)] * TOP_K + [pltpu.SemaphoreType.DMA] * TOP_K,
        name="scatter_rows")
    def scatter(src_hbm, dest_hbm, out_hbm, rows_v, *rest):
        idx_v, sems = rest[:TOP_K], rest[TOP_K:]
        worker = lax.axis_index("subcore") * n_cores + lax.axis_index("core")
        base = worker * per_worker

        @pl.loop(0, per_worker, step=win)
        def _(off):
            t0 = base + off
            pltpu.sync_copy(src_hbm.at[pl.ds(t0, win)], rows_v)
            for k in range(TOP_K):
                pltpu.sync_copy(dest_hbm.at[pl.ds(k * m + t0, win)], idx_v[k])
            copies = [pltpu.async_copy(rows_v, out_hbm.at[idx_v[k]], sems[k]) for k in range(TOP_K)]
            for cp in copies:
                cp.wait()

    return scatter(src, dest)


def _expert_kernel(te_ref, nu_ref, tv_ref, x_ref, wgu_ref, bgu_ref, wd_ref, bd_ref, o_ref, wgu_scr, wd_scr):
    i = pl.program_id(0)
    used = i < nu_ref[0]
    new_expert = (i == 0) | (te_ref[i] != te_ref[jnp.maximum(i - 1, 0)])

    @pl.when(used & new_expert)
    def _():
        wgu_scr[...] = wgu_ref[...].astype(BF16)
        wd_scr[...] = wd_ref[...].astype(BF16)

    @pl.when(used)
    def _():
        dff = wd_ref.shape[0]
        half = wgu_ref.shape[0] // 2
        rows = lax.broadcasted_iota(jnp.int32, x_ref.shape, 0)
        xa, xb = _unpack_bf16_pair(jnp.where(rows < tv_ref[i], x_ref[...], jnp.uint32(0)))
        gu = (jnp.dot(xa.astype(BF16), wgu_scr[:half, :], preferred_element_type=F32)
              + jnp.dot(xb.astype(BF16), wgu_scr[half:, :], preferred_element_type=F32)) + bgu_ref[...]
        gate = jnp.minimum(gu[:, :dff], SWIGLU_LIMIT)
        up = jnp.clip(gu[:, dff:], -SWIGLU_LIMIT, SWIGLU_LIMIT)
        act = (up + 1.0) * gate * _sigmoid(SWIGLU_ALPHA * gate)
        y = jnp.dot(act.astype(BF16), wd_scr[...], preferred_element_type=F32) + bd_ref[...]
        o_ref[...] = _pack_bf16_pair(y)

    @pl.when(jnp.logical_not(used))
    def _():
        o_ref[...] = jnp.zeros(o_ref.shape, o_ref.dtype)


def _experts(xs, tile_expert, n_used, tile_valid, wgu, bgu, wd, bd):
    rows, dh = xs.shape
    d = 2 * dh
    tm = MOE_TILE
    n_tiles = rows // tm
    dff2 = wgu.shape[-1]

    def tile(i, te, nu, tv):
        return (jnp.minimum(i, nu[0] - 1), 0)

    def wsel(i, te, nu, tv):
        return (te[jnp.minimum(i, nu[0] - 1)], 0, 0)

    grid_spec = pltpu.PrefetchScalarGridSpec(
        num_scalar_prefetch=3,
        grid=(n_tiles,),
        in_specs=[pl.BlockSpec((tm, dh), tile),
                  pl.BlockSpec((None, d, dff2), wsel),
                  pl.BlockSpec((None, 1, dff2), wsel),
                  pl.BlockSpec((None, dff2 // 2, d), wsel),
                  pl.BlockSpec((None, 1, d), wsel)],
        out_specs=pl.BlockSpec((tm, dh), lambda i, te, nu, tv: (i, 0)),
        scratch_shapes=[pltpu.VMEM((d, dff2), BF16), pltpu.VMEM((dff2 // 2, d), BF16)],
    )
    return pl.pallas_call(
        _expert_kernel,
        grid_spec=grid_spec,
        out_shape=jax.ShapeDtypeStruct((rows, dh), jnp.uint32),
        compiler_params=_cparams("arbitrary"),
        name="experts",
    )(tile_expert, n_used, tile_valid, xs, wgu, bgu, wd, bd)


def _final_kernel(x1_ref, ya_ref, yb_ref, yc_ref, yd_ref, tw_ref, g2_ref, fw_ref, o_ref):
    d = x1_ref.shape[-1]
    half = d // 2
    tw = tw_ref[...]
    acc_hi = jnp.zeros((x1_ref.shape[0], half), F32)
    acc_lo = jnp.zeros((x1_ref.shape[0], half), F32)
    for k, y_ref in enumerate((ya_ref, yb_ref, yc_ref, yd_ref)):
        y_hi, y_lo = _unpack_bf16_pair(y_ref[...])
        acc_hi = acc_hi + tw[:, k:k + 1] * y_hi
        acc_lo = acc_lo + tw[:, k:k + 1] * y_lo
    x_hi = x1_ref[:, :half] + g2_ref[:, :half] * acc_hi
    x_lo = x1_ref[:, half:] + g2_ref[:, half:] * acc_lo
    ms = (jnp.sum(x_hi * x_hi, axis=-1, keepdims=True) + jnp.sum(x_lo * x_lo, axis=-1, keepdims=True)) / d
    inv = lax.rsqrt(ms + EPS)
    o_ref[:, :half] = x_hi * inv * fw_ref[:, :half]
    o_ref[:, half:] = x_lo * inv * fw_ref[:, half:]


def _final(x1, y4, tw, g2, fw, rows_per_mod, tm):
    m, d = x1.shape
    return pl.pallas_call(
        _final_kernel,
        grid=(m // tm,),
        in_specs=[pl.BlockSpec((tm, d), lambda i: (i, 0))]
        + [pl.BlockSpec((None, tm, d // 2), functools.partial(lambda k, i: (k, i, 0), k)) for k in range(TOP_K)]
        + [pl.BlockSpec((tm, LANES), lambda i: (i, 0)),
           pl.BlockSpec((None, 1, d), lambda i: ((i * tm) // rows_per_mod, 0, 0)),
           pl.BlockSpec((1, d), lambda i: (0, 0))],
        out_specs=pl.BlockSpec((tm, d), lambda i: (i, 0)),
        out_shape=jax.ShapeDtypeStruct((m, d), F32),
        compiler_params=_cparams("arbitrary"),
        name="final",
    )(x1, y4, y4, y4, y4, tw, g2, fw.reshape(1, d))


def _dispatch_plan(e, rank, counts, tm):
    n_assign = e.size
    padded = (counts + tm - 1) // tm * tm
    pad_end = jnp.cumsum(padded)
    pad_start = pad_end - padded
    dest = (pad_start[e] + rank).astype(jnp.int32)
    n_tiles = n_assign // tm + N_EXPERTS
    tile_start = jnp.arange(n_tiles, dtype=jnp.int32) * tm
    tile_expert = jnp.minimum(jnp.sum(pad_end[None, :] <= tile_start[:, None], axis=1), N_EXPERTS - 1).astype(jnp.int32)
    n_used = (pad_end[-1] // tm).astype(jnp.int32).reshape(1)
    tile_valid = jnp.clip(pad_start[tile_expert] + counts[tile_expert] - tile_start, 0, tm).astype(jnp.int32)
    return dest, tile_expert, n_used, tile_valid


def kernel(x, c, ctx, c_ctx, w_mod, b_mod, norm1_w, norm2_w, w_in, conv_w, conv_b, dt_bias, a_log, d_skip,
           ssd_norm_w, w_ssd_out, w_four_out, w_o, w_router, b_router, w_gate_up, b_gate_up, w_down, b_down,
           final_norm_w):
    bsz, n, d = x.shape
    n_ctx = ctx.shape[1]
    depth = w_mod.shape[0]
    assert depth == 1, "a stacked model would also need the context stream's residual update"
    d_inner = N_HEADS * HEAD_DIM
    off_dt = d_inner + 2 * BC_WIDTH
    m = bsz * n
    x2d = x.reshape(m, d)
    tm = min(512, n)

    for layer in range(depth):
        rows = -(-(bsz + 1) // 8) * 8
        c_rows = jnp.zeros((rows, d), F32).at[:bsz].set(c).at[bsz].set(c_ctx)
        mod = _modulation(c_rows, w_mod[layer], b_mod[layer])
        mods = [mod[:bsz, i * d:(i + 1) * d].reshape(bsz, 1, d) for i in range(N_MOD)]
        sh1, sc1, g1, sh2, sc2, g2 = mods
        sh1_c = mod[bsz:bsz + 1, :d].reshape(1, 1, d)
        sc1_c = mod[bsz:bsz + 1, d:2 * d].reshape(1, 1, d)

        wl = w_in[layer]
        w_main = jnp.concatenate([wl[:, :off_dt], wl[:, off_dt + 2 * N_HEADS:]], axis=1).astype(BF16)
        w_dt = jnp.zeros((d, 2 * LANES), F32)
        w_dt = w_dt.at[:, :N_HEADS].set(wl[:, off_dt:off_dt + N_HEADS])
        w_dt = w_dt.at[:, LANES:LANES + N_HEADS].set(wl[:, off_dt + N_HEADS:off_dt + 2 * N_HEADS]).astype(BF16)
        pad = jnp.zeros((2, 1, LANES - N_HEADS), F32)
        dt_bias2 = jnp.concatenate([dt_bias[layer].reshape(2, 1, N_HEADS), pad], axis=-1)
        a_log2 = jnp.concatenate([a_log[layer].reshape(2, 1, N_HEADS), pad], axis=-1)

        w_ctx = w_main[:, :off_dt]
        pc, dtc = _in_proj(ctx.reshape(bsz * n_ctx, d), norm1_w[layer], sc1_c, sh1_c, w_ctx, w_dt,
                           bsz * n_ctx, min(512, n_ctx))
        xbc_c = _conv_silu(pc.reshape(bsz, n_ctx, off_dt), conv_w[layer][:, :off_dt], conv_b[layer][:off_dt], off_dt)
        h0 = jnp.zeros((bsz, 2, N_GROUPS, D_STATE, HEADS_PER_GROUP * HEAD_DIM), F32)
        h_ctx = _ssd(xbc_c, dtc.reshape(bsz, n_ctx, 2 * LANES), dt_bias2, a_log2, h0, with_y=False)

        p, dtl = _in_proj(x2d, norm1_w[layer], sc1, sh1, w_main, w_dt, n, tm)
        n_conv = off_dt + 2 * BC_WIDTH
        xbc = _conv_silu(p.reshape(bsz, n, -1), conv_w[layer], conv_b[layer], n_conv)
        y2 = _ssd(xbc, dtl.reshape(bsz, n, 2 * LANES), dt_bias2, a_log2, h_ctx, with_y=True)
        ff = _fnet_mix(p, (n_conv + d_inner) // d, bsz, n)

        dsk = jnp.repeat(d_skip[layer].astype(F32), HEAD_DIM).reshape(1, d_inner)
        wr = jnp.zeros((d, LANES), F32).at[:, :N_EXPERTS].set(w_router[layer])
        br = jnp.full((1, LANES), NEG_BIG, F32).at[0, :N_EXPERTS].set(b_router[layer])
        x1, h2, ti, tw, rk, cnt = _tail(
            y2.reshape(2, m, d_inner), xbc.reshape(m, n_conv), p, ff, x2d, g1, sc2, sh2, dsk,
            ssd_norm_w[layer].reshape(1, d_inner), w_ssd_out[layer].astype(BF16), w_four_out[layer].astype(BF16),
            w_o[layer].astype(BF16), norm2_w[layer].reshape(1, d), wr, br, n, min(TAIL_ROWS, n))

        counts = cnt[0, :N_EXPERTS].astype(jnp.int32)
        dest, tile_expert, n_used, tile_valid = _dispatch_plan(ti[:, :TOP_K], rk[:, :TOP_K], counts, MOE_TILE)
        dest_km = dest.T.reshape(-1)
        xs = _scatter_rows(h2, dest_km, tile_expert.shape[0] * MOE_TILE)
        ys = _experts(xs, tile_expert, n_used, tile_valid, w_gate_up[layer].astype(BF16),
                      b_gate_up[layer].reshape(N_EXPERTS, 1, -1), w_down[layer].astype(BF16),
                      b_down[layer].reshape(N_EXPERTS, 1, -1))
        y4 = _gather_rows(ys, dest_km).reshape(TOP_K, m, d // 2)
        x2d = _final(x1, y4, tw, g2, final_norm_w, n, tm)
    return x2d.reshape(bsz, n, d)
```

```python
import functools
import math

import numpy as np
import jax
import jax.numpy as jnp
from jax import lax
from jax.experimental import pallas as pl
from jax.experimental.pallas import tpu as pltpu
from jax.experimental.pallas import tpu_sc as plsc

F32 = jnp.float32
BF16 = jnp.bfloat16

EPS = 1e-6
GRID_W = 64
N_MOD = 6
F_GROUPS = 8
F_GROUP_DIM = 128
HEAD_DIM = 64
N_HEADS = 32
N_GROUPS = 4
HEADS_PER_GROUP = N_HEADS // N_GROUPS
D_STATE = 128
BC_WIDTH = N_GROUPS * D_STATE
CONV_K = 5
CHUNK = 128
N_EXPERTS = 32
TOP_K = 4
SWIGLU_LIMIT = 7.0
SWIGLU_ALPHA = 1.702

LANES = 128
VMEM_LIMIT_BYTES = 56 * 1024 * 1024
NEG_BIG = -1e30
MOE_TILE = 512
TAIL_ROWS = 512
SC_GATHER_WINDOW = 64


def _cparams(*sem):
    return pltpu.CompilerParams(dimension_semantics=sem, vmem_limit_bytes=VMEM_LIMIT_BYTES)


def _sigmoid(v):
    return 1.0 / (1.0 + jnp.exp(-v))


def _silu(v):
    return v * _sigmoid(v)


def _softplus(v):
    return jnp.maximum(v, 0.0) + jnp.log(1.0 + jnp.exp(-jnp.abs(v)))


def _pack_bf16_pair(v):
    w = v.shape[1] // 2
    bits = lax.bitcast_convert_type(v.astype(BF16).astype(F32), jnp.uint32)
    return bits[:, :w] | (bits[:, w:] >> 16)


def _unpack_bf16_pair(p):
    hi = lax.bitcast_convert_type(p & jnp.uint32(0xFFFF0000), F32)
    lo = lax.bitcast_convert_type(p << 16, F32)
    return hi, lo


def _mod_kernel(c_ref, w_ref, b_ref, o_ref):
    s = _silu(c_ref[...]).astype(BF16)
    o_ref[...] = jnp.dot(s, w_ref[...].astype(BF16), preferred_element_type=F32) + b_ref[...]


def _modulation(c_rows, w_mod, b_mod):
    rows, d = c_rows.shape
    n_out = w_mod.shape[1]
    tn = 1024
    return pl.pallas_call(
        _mod_kernel,
        grid=(n_out // tn,),
        in_specs=[pl.BlockSpec((rows, d), lambda j: (0, 0)),
                  pl.BlockSpec((d, tn), lambda j: (0, j)),
                  pl.BlockSpec((1, tn), lambda j: (0, j))],
        out_specs=pl.BlockSpec((rows, tn), lambda j: (0, j)),
        out_shape=jax.ShapeDtypeStruct((rows, n_out), F32),
        compiler_params=_cparams("arbitrary"),
        name="modulation",
    )(c_rows, w_mod, b_mod.reshape(1, n_out))


_INPROJ_TN = 1024


def _inproj_kernel(x_ref, nw_ref, sc_ref, sh_ref, w_ref, wdt_ref, p_ref, dt_ref):
    x = x_ref[...]
    inv = lax.rsqrt(jnp.mean(x * x, axis=-1, keepdims=True) + EPS)
    h = (x * inv * nw_ref[...]) * (1.0 + sc_ref[...]) + sh_ref[...]
    hb = h.astype(BF16)
    dt_ref[...] = jnp.dot(hb, wdt_ref[...], preferred_element_type=F32)
    for j in range(w_ref.shape[1] // _INPROJ_TN):
        cols = slice(j * _INPROJ_TN, (j + 1) * _INPROJ_TN)
        p_ref[:, cols] = jnp.dot(hb, w_ref[:, cols], preferred_element_type=F32).astype(p_ref.dtype)


def _in_proj(x2d, norm_w, sc, sh, w_main, w_dt, rows_per_mod, tm):
    m, d = x2d.shape
    n = w_main.shape[1]
    ndt = w_dt.shape[1]
    resident = dict(pipeline_mode=pl.Buffered(1))
    return pl.pallas_call(
        _inproj_kernel,
        grid=(m // tm,),
        in_specs=[pl.BlockSpec((tm, d), lambda i: (i, 0)),
                  pl.BlockSpec((1, d), lambda i: (0, 0), **resident),
                  pl.BlockSpec((None, 1, d), lambda i: ((i * tm) // rows_per_mod, 0, 0)),
                  pl.BlockSpec((None, 1, d), lambda i: ((i * tm) // rows_per_mod, 0, 0)),
                  pl.BlockSpec((d, n), lambda i: (0, 0), **resident),
                  pl.BlockSpec((d, ndt), lambda i: (0, 0), **resident)],
        out_specs=[pl.BlockSpec((tm, n), lambda i: (i, 0)),
                   pl.BlockSpec((tm, ndt), lambda i: (i, 0))],
        out_shape=[jax.ShapeDtypeStruct((m, n), BF16),
                   jax.ShapeDtypeStruct((m, ndt), F32)],
        compiler_params=_cparams("arbitrary"),
        name="in_proj",
    )(x2d, norm_w.reshape(1, d), sc, sh, w_main, w_dt)


_CONV_HALO = 16


_CONV_ROWS = 64


def _conv_shift_matrix(rc):
    win = rc + 2 * _CONV_HALO
    s = np.zeros((rc, CONV_K * win), np.float32)
    for k in range(CONV_K):
        for l in range(rc):
            s[l, k * win + _CONV_HALO + l + k - CONV_K // 2] = 1.0
    return jnp.asarray(s, BF16)


def _conv_kernel(p_ref, w_ref, b_ref, s_ref, o_ref, *, n, rc):
    tc = o_ref.shape[-1]
    w = w_ref[...].astype(BF16)
    bias = b_ref[...]
    smat = s_ref[...]
    zeros = jnp.zeros((_CONV_HALO, tc), BF16)
    for r0 in range(0, n, rc):
        top = zeros if r0 == 0 else p_ref[r0 - _CONV_HALO:r0, :]
        bot = zeros if r0 + rc >= n else p_ref[r0 + rc:r0 + rc + _CONV_HALO, :]
        window = jnp.concatenate([top, p_ref[r0:r0 + rc, :], bot], axis=0)
        taps = jnp.concatenate([window * w[k:k + 1, :] for k in range(CONV_K)], axis=0)
        acc = jnp.dot(smat, taps, preferred_element_type=F32) + bias
        o_ref[r0:r0 + rc, :] = _silu(acc).astype(o_ref.dtype)


def _conv_silu(p3d, conv_w, conv_b, n_ch):
    bsz, n, _ = p3d.shape
    tc = 512
    rc = min(_CONV_ROWS, n)
    smat = _conv_shift_matrix(rc)
    return pl.pallas_call(
        functools.partial(_conv_kernel, n=n, rc=rc),
        grid=(bsz, n_ch // tc),
        in_specs=[pl.BlockSpec((None, n, tc), lambda b, j: (b, 0, j)),
                  pl.BlockSpec((CONV_K, tc), lambda b, j: (0, j)),
                  pl.BlockSpec((1, tc), lambda b, j: (0, j)),
                  pl.BlockSpec(smat.shape, lambda b, j: (0, 0))],
        out_specs=pl.BlockSpec((None, n, tc), lambda b, j: (b, 0, j)),
        out_shape=jax.ShapeDtypeStruct((bsz, n, n_ch), BF16),
        compiler_params=_cparams("arbitrary", "arbitrary"),
        name="conv_silu",
    )(p3d, conv_w, conv_b.reshape(1, -1), smat)


LOG2E = 1.4426950408889634
SSD_SUB = 8


def _head_expand_matrix():
    e = np.zeros((LANES, N_HEADS * HEAD_DIM), np.float32)
    for h in range(N_HEADS):
        e[h, h * HEAD_DIM:(h + 1) * HEAD_DIM] = 1.0
    return jnp.asarray(e, BF16)


def _ssd_chunk_terms(dt_raw, dt_bias, a_log, expand, tri, fwd):
    q = CHUNK
    dt = _softplus(dt_raw + dt_bias)
    da = dt * (-jnp.exp(a_log))
    hi = da.astype(BF16)
    r1 = da - hi.astype(F32)
    mid = r1.astype(BF16)
    lo = (r1 - mid.astype(F32)).astype(BF16)
    ones_tri = jnp.where(tri, 1.0, 0.0).astype(BF16)
    cs3 = jnp.dot(ones_tri, jnp.concatenate([hi, mid, lo], axis=1), preferred_element_type=F32)
    cs = cs3[:, :LANES] + cs3[:, LANES:2 * LANES] + cs3[:, 2 * LANES:]
    tot = jnp.where(fwd, cs[q - 1:q, :], cs[0:1, :])
    dte = dt * jnp.exp(tot - cs)
    dec = jnp.exp(tot)
    dec_hi = dec.astype(BF16)
    dec_lo = (dec - dec_hi.astype(F32)).astype(BF16)
    dec2 = jnp.concatenate([dec_hi, dec_lo, jnp.zeros((6, LANES), BF16)], axis=0)
    stacked = jnp.concatenate([jnp.exp(cs).astype(BF16), dte.astype(BF16), dec2], axis=0)
    spread = jnp.dot(stacked, expand, preferred_element_type=F32)
    ecs_e = spread[:q]
    dte_e = spread[q:2 * q].astype(BF16)
    dec_e = jnp.broadcast_to(spread[2 * q:2 * q + 1] + spread[2 * q + 1:2 * q + 2], (8, spread.shape[1]))
    cs2 = cs * LOG2E
    return cs2, cs2.T, dt.T, ecs_e, dte_e, dec_e


def _ssd_kernel(*refs, with_y, nc, n_sub):
    if with_y:
        (x_ref, b_ref, c_ref, dt_ref, dtn_ref, dtb_ref, alog_ref, exp_ref, h0_ref, y_ref,
         s_scr, cs_scr, dee_scr, dce_scr, ecs_scr) = refs
    else:
        (x_ref, b_ref, dt_ref, dtn_ref, dtb_ref, alog_ref, exp_ref, h0_ref, hout_ref,
         s_scr, cs_scr, dee_scr, dce_scr) = refs
    q = CHUNK
    gw = HEADS_PER_GROUP * HEAD_DIM
    d = pl.program_id(1)
    c = pl.program_id(2)
    row = lax.broadcasted_iota(jnp.int32, (q, q), 0)
    col = lax.broadcasted_iota(jnp.int32, (q, q), 1)
    fwd = d == 0
    tri = jnp.where(fwd, row - col, col - row) >= 0
    first_half = col < HEAD_DIM

    def store_terms(slot, terms):
        cs2, cs2_t, dt_t, ecs_e, dte_e, dec_e = terms
        cs_scr[slot, 0] = cs2
        cs_scr[slot, 1] = cs2_t
        cs_scr[slot, 2] = dt_t
        dee_scr[slot] = dte_e
        dce_scr[slot] = dec_e
        if with_y:
            ecs_scr[slot] = ecs_e

    def terms_of(ref, sub):
        rows = pl.ds(pl.multiple_of(sub * q, q), q)
        return _ssd_chunk_terms(ref[rows, :], dtb_ref[...], alog_ref[...], exp_ref[...], tri, fwd)

    def sub_of(j):
        return jnp.where(fwd, j, n_sub - 1 - j)

    @pl.when(c == 0)
    def _():
        s_scr[...] = h0_ref[...]
        store_terms(0, terms_of(dt_ref, sub_of(0)))

    for j in range(n_sub):
        slot = j % 2
        rows = pl.ds(pl.multiple_of(sub_of(j) * q, q), q)
        cs2 = cs_scr[slot, 0]
        cs2_t = cs_scr[slot, 1]
        dt_t = cs_scr[slot, 2]
        if j + 1 < n_sub:
            store_terms(1 - slot, terms_of(dt_ref, sub_of(j + 1)))
        else:
            store_terms(1 - slot, terms_of(dtn_ref, sub_of(0)))

        for g in range(N_GROUPS):
            cols = slice(g * gw, (g + 1) * gw)
            bg = b_ref[rows, g * D_STATE:(g + 1) * D_STATE]
            xg = x_ref[rows, cols]
            s_old = s_scr[g]
            if with_y:
                cg = c_ref[rows, g * D_STATE:(g + 1) * D_STATE]
                cb = lax.dot_general(cg, bg, (((1,), (1,)), ((), ())), preferred_element_type=F32)
                y_off = jnp.dot(cg, s_old.astype(BF16), preferred_element_type=F32)
                for k in range(HEADS_PER_GROUP // 2):
                    pair = g * (HEADS_PER_GROUP // 2) + k
                    lanes = slice(pair * LANES, (pair + 1) * LANES)
                    xp = xg[:, k * LANES:(k + 1) * LANES]
                    xz = jnp.zeros_like(xp)
                    lhs = []
                    for hh in (2 * pair, 2 * pair + 1):
                        colb = jnp.broadcast_to(cs2[:, hh:hh + 1], (q, q))
                        rowb = cs2_t[hh:hh + 1, :]
                        decay = jnp.exp2(jnp.where(tri, colb - rowb, -jnp.inf))
                        lhs.append((cb * decay * dt_t[hh:hh + 1, :]).astype(BF16))
                    y_diag = jnp.dot(jnp.concatenate(lhs, axis=1),
                                     jnp.concatenate([jnp.where(first_half, xp, xz), jnp.where(first_half, xz, xp)],
                                                     axis=0),
                                     preferred_element_type=F32)
                    y = y_diag + ecs_scr[slot, :, lanes] * y_off[:, k * LANES:(k + 1) * LANES]
                    y_ref[rows, lanes] = y.astype(y_ref.dtype)
            xdte = xg * dee_scr[slot, :, cols]
            upd = lax.dot_general(bg, xdte, (((0,), (0,)), ((), ())), preferred_element_type=F32)
            s_scr[g] = s_old * dce_scr[slot, 0:1, cols] + upd

    if not with_y:
        @pl.when(c == nc - 1)
        def _():
            hout_ref[...] = s_scr[...]


def _ssd(xbc, dt_raw, dt_bias2, a_log2, h0, with_y):
    bsz, n, _ = xbc.shape
    n_sub = min(SSD_SUB, n // CHUNK)
    assert n_sub % 2 == 0 and n % (n_sub * CHUNK) == 0
    blk = n_sub * CHUNK
    nc = n // blk
    d_inner = N_HEADS * HEAD_DIM
    gw = HEADS_PER_GROUP * HEAD_DIM
    x_blk = d_inner // BC_WIDTH

    def ceff(dd, cc):
        return cc + dd * (nc - 1 - 2 * cc)

    def cnext(dd, cc):
        return ceff(dd, jnp.minimum(cc + 1, nc - 1))

    in_specs = [pl.BlockSpec((None, blk, d_inner), lambda b, dd, cc: (b, ceff(dd, cc), 0)),
                pl.BlockSpec((None, blk, BC_WIDTH), lambda b, dd, cc: (b, ceff(dd, cc), x_blk + dd))]
    args = [xbc, xbc]
    if with_y:
        in_specs.append(pl.BlockSpec((None, blk, BC_WIDTH), lambda b, dd, cc: (b, ceff(dd, cc), x_blk + 2 + dd)))
        args.append(xbc)
    in_specs += [pl.BlockSpec((None, blk, LANES), lambda b, dd, cc: (b, ceff(dd, cc), dd)),
                 pl.BlockSpec((None, blk, LANES), lambda b, dd, cc: (b, cnext(dd, cc), dd)),
                 pl.BlockSpec((None, 1, LANES), lambda b, dd, cc: (dd, 0, 0)),
                 pl.BlockSpec((None, 1, LANES), lambda b, dd, cc: (dd, 0, 0)),
                 pl.BlockSpec((LANES, d_inner), lambda b, dd, cc: (0, 0)),
                 pl.BlockSpec((None, None, N_GROUPS, D_STATE, gw), lambda b, dd, cc: (b, dd, 0, 0, 0))]
    args += [dt_raw, dt_raw, dt_bias2, a_log2, _head_expand_matrix(), h0]
    scratch = [pltpu.VMEM((N_GROUPS, D_STATE, gw), F32),
               pltpu.VMEM((2, 3, CHUNK, LANES), F32),
               pltpu.VMEM((2, CHUNK, d_inner), BF16),
               pltpu.VMEM((2, 8, d_inner), F32)]
    if with_y:
        out_specs = pl.BlockSpec((None, None, blk, d_inner), lambda b, dd, cc: (dd, b, ceff(dd, cc), 0))
        out_shape = jax.ShapeDtypeStruct((2, bsz, n, d_inner), BF16)
        scratch += [pltpu.VMEM((2, CHUNK, d_inner), F32)]
    else:
        out_specs = pl.BlockSpec((None, None, N_GROUPS, D_STATE, gw), lambda b, dd, cc: (b, dd, 0, 0, 0))
        out_shape = jax.ShapeDtypeStruct((bsz, 2, N_GROUPS, D_STATE, gw), F32)
    return pl.pallas_call(
        functools.partial(_ssd_kernel, with_y=with_y, nc=nc, n_sub=n_sub),
        grid=(bsz, 2, nc),
        in_specs=in_specs,
        out_specs=out_specs,
        out_shape=out_shape,
        scratch_shapes=scratch,
        compiler_params=_cparams("arbitrary", "arbitrary", "arbitrary"),
        name="ssd_scan" if with_y else "ssd_ctx_state",
    )(*args)


def _dft_mats(n):
    k = np.arange(n)
    ang = 2.0 * np.pi * ((k[:, None] * k[None, :]) % n) / n
    return np.cos(ang), np.sin(ang)


def _strided_rows(scr, t, start, count, stride, width):
    return jnp.concatenate([scr[t, lb, pl.ds(start, count, stride=stride), :] for lb in range(width // LANES)],
                           axis=1)


def _fnet_stage1_kernel(u_ref, chan_ref, m1_ref, twc_ref, tws_ref, o_ref, zs_scr, *, tcs):
    r = u_ref.shape[0]
    width = u_ref.shape[-1]
    u2 = u_ref[...].reshape(r * tcs, width)
    chan = chan_ref[...]
    for g in range(F_GROUPS):
        pq = jnp.dot(u2[:, g * F_GROUP_DIM:(g + 1) * F_GROUP_DIM], chan, preferred_element_type=F32)
        zs_scr[0, g] = pq[:, :F_GROUP_DIM]
        zs_scr[1, g] = pq[:, F_GROUP_DIM:]
    m1 = m1_ref[...]
    for j in range(tcs):
        zz = jnp.concatenate([_strided_rows(zs_scr, 0, j, r, tcs, width),
                              _strided_rows(zs_scr, 1, j, r, tcs, width)], axis=0).astype(BF16)
        u = jnp.dot(m1, zz, preferred_element_type=F32)
        ur = u[:r]
        ui = u[r:]
        tc = twc_ref[j]
        ts = tws_ref[j]
        o_ref[0, j] = (ur * tc + ui * ts).astype(o_ref.dtype)
        o_ref[1, j] = (ui * tc - ur * ts).astype(o_ref.dtype)


def _fnet_stage2_kernel(u_ref, k2_ref, o_ref):
    c, tbs, width = o_ref.shape
    uu = jnp.concatenate([u_ref[0].reshape(c * tbs, width), u_ref[1].reshape(c * tbs, width)], axis=0)
    out = jnp.dot(k2_ref[...], uu, preferred_element_type=F32)
    o_ref[...] = out.reshape(c, tbs, width).astype(o_ref.dtype)


def _fnet_mix(p2d, col_blk, bsz, n):
    m = p2d.shape[0]
    width = F_GROUPS * F_GROUP_DIM
    cgrid = GRID_W
    rgrid = n // cgrid
    scale = 1.0 / math.sqrt(n * F_GROUP_DIM)

    cc, sc = _dft_mats(F_GROUP_DIM)
    chan = jnp.asarray(np.concatenate([cc, sc], axis=1) * scale, BF16)

    cr, sr = _dft_mats(rgrid)
    m1 = jnp.asarray(np.block([[cr, -sr], [-sr, -cr]]), BF16)
    bb = np.arange(rgrid)[None, :]
    ci = np.arange(cgrid)[:, None]
    ang = 2.0 * np.pi * ((ci * bb) % n) / n
    twc = jnp.asarray(np.cos(ang)[:, :, None], F32)
    tws = jnp.asarray(np.sin(ang)[:, :, None], F32)
    tcs = 16
    p4 = p2d.reshape(bsz, rgrid, cgrid, p2d.shape[-1])
    u5 = pl.pallas_call(
        functools.partial(_fnet_stage1_kernel, tcs=tcs),
        grid=(bsz, cgrid // tcs),
        in_specs=[pl.BlockSpec((None, rgrid, tcs, width), lambda b, j: (b, 0, j, col_blk)),
                  pl.BlockSpec((F_GROUP_DIM, 2 * F_GROUP_DIM), lambda b, j: (0, 0)),
                  pl.BlockSpec((2 * rgrid, 2 * rgrid), lambda b, j: (0, 0)),
                  pl.BlockSpec((tcs, rgrid, 1), lambda b, j: (j, 0, 0)),
                  pl.BlockSpec((tcs, rgrid, 1), lambda b, j: (j, 0, 0))],
        out_specs=pl.BlockSpec((2, None, tcs, rgrid, width), lambda b, j: (0, b, j, 0, 0)),
        out_shape=jax.ShapeDtypeStruct((2, bsz, cgrid, rgrid, width), BF16),
        scratch_shapes=[pltpu.VMEM((2, width // LANES, rgrid * tcs, LANES), F32)],
        compiler_params=_cparams("arbitrary", "arbitrary"),
        name="fnet_stage1",
    )(p4, chan, m1, twc, tws)

    cc2, sc2 = _dft_mats(cgrid)
    tbs = min(16, rgrid)
    eye = np.eye(tbs)
    k2 = jnp.asarray(np.concatenate([np.kron(cc2, eye), np.kron(sc2, eye)], axis=1), BF16)
    out = pl.pallas_call(
        _fnet_stage2_kernel,
        grid=(bsz, rgrid // tbs),
        in_specs=[pl.BlockSpec((2, None, cgrid, tbs, width), lambda b, j: (0, b, 0, j, 0)),
                  pl.BlockSpec(k2.shape, lambda b, j: (0, 0))],
        out_specs=pl.BlockSpec((None, cgrid, tbs, width), lambda b, j: (b, 0, j, 0)),
        out_shape=jax.ShapeDtypeStruct((bsz, cgrid, rgrid, width), BF16),
        compiler_params=_cparams("arbitrary", "arbitrary"),
        name="fnet_stage2",
    )(u5, k2)
    return out.reshape(m, width)


def _tail_kernel(yf_ref, yb_ref, xs_ref, z_ref, ff_ref, gf_ref, gs_ref, x_ref, g1_ref, sc2_ref, sh2_ref,
                 dsk_ref, snw_ref, wso_ref, wfo_ref, wo_ref, n2w_ref, wr_ref, br_ref,
                 x1_ref, h2_ref, ti_ref, tw_ref, rk_ref, cnt_ref, carry_scr):
    y = (yf_ref[0] + yb_ref[0]).astype(F32) + dsk_ref[...] * xs_ref[...].astype(F32)
    g = y * _silu(z_ref[...]).astype(F32)
    inv = lax.rsqrt(jnp.mean(g * g, axis=-1, keepdims=True) + EPS)
    gn = (g * inv * snw_ref[...]).astype(BF16)
    y_ssd = jnp.dot(gn, wso_ref[...], preferred_element_type=F32)
    y_four = jnp.dot(ff_ref[...], wfo_ref[...], preferred_element_type=F32)
    t = _sigmoid(gf_ref[...]) * y_four.astype(BF16) + _sigmoid(gs_ref[...]) * y_ssd.astype(BF16)
    mix = jnp.dot(t, wo_ref[...], preferred_element_type=F32)
    x1 = x_ref[...] + g1_ref[...] * mix
    x1_ref[...] = x1
    inv2 = lax.rsqrt(jnp.mean(x1 * x1, axis=-1, keepdims=True) + EPS)
    h2 = (x1 * inv2 * n2w_ref[...]) * (1.0 + sc2_ref[...]) + sh2_ref[...]
    h2_ref[...] = _pack_bf16_pair(h2)
    h_hi = h2.astype(BF16)
    h_lo = (h2 - h_hi.astype(F32)).astype(BF16)
    wr = wr_ref[...]
    w_hi = wr.astype(BF16)
    w_lo = (wr - w_hi.astype(F32)).astype(BF16)
    logits = (jnp.dot(h_hi, w_hi, preferred_element_type=F32)
              + jnp.dot(h_hi, w_lo, preferred_element_type=F32)
              + jnp.dot(h_lo, w_hi, preferred_element_type=F32)) + br_ref[...]
    tm = logits.shape[0]
    lane = lax.broadcasted_iota(jnp.int32, (tm, LANES), 1)
    vals, idxs = [], []
    cur = logits
    for _ in range(TOP_K):
        mx = jnp.max(cur, axis=-1, keepdims=True)
        ix = jnp.min(jnp.where(cur == mx, lane, LANES), axis=-1, keepdims=True)
        vals.append(mx)
        idxs.append(ix)
        cur = jnp.where(lane == ix, NEG_BIG * 2.0, cur)
    es = [jnp.exp(v - vals[0]) for v in vals]
    den = es[0] + es[1] + es[2] + es[3]
    ti = jnp.zeros((tm, LANES), jnp.int32)
    tw = jnp.zeros((tm, LANES), F32)
    for k in range(TOP_K):
        ti = jnp.where(lane == k, idxs[k], ti)
        tw = jnp.where(lane == k, es[k] / den, tw)
    ti_ref[...] = ti
    tw_ref[...] = tw

    @pl.when(pl.program_id(0) == 0)
    def _():
        carry_scr[...] = jnp.zeros(carry_scr.shape, F32)

    onehots = [jnp.where(lane == ix, 1.0, 0.0) for ix in idxs]
    cnt = onehots[0] + onehots[1] + onehots[2] + onehots[3]
    r_i = lax.broadcasted_iota(jnp.int32, (tm, tm), 0)
    c_i = lax.broadcasted_iota(jnp.int32, (tm, tm), 1)
    earlier = jnp.where(r_i > c_i, 1.0, 0.0).astype(BF16)
    before = jnp.dot(earlier, cnt.astype(BF16), preferred_element_type=F32) + carry_scr[0:1, :]
    rk = jnp.zeros((tm, LANES), jnp.int32)
    for k in range(TOP_K):
        rank_k = jnp.sum(onehots[k] * before, axis=-1, keepdims=True)
        rk = jnp.where(lane == k, rank_k.astype(jnp.int32), rk)
    rk_ref[...] = rk
    total = carry_scr[...] + jnp.sum(cnt, axis=0, keepdims=True)
    carry_scr[...] = total
    cnt_ref[...] = total


def _tail(y2, xbc2d, p2d, ff, x2d, g1, sc2, sh2, dsk, snw, wso, wfo, wo, n2w, wr, br, rows_per_mod, tm):
    m, d = x2d.shape
    di = y2.shape[-1]
    zb = (2 * di) // di
    fb = (3 * di) // d
    row = lambda i: (i, 0)
    modrow = lambda i: ((i * tm) // rows_per_mod, 0, 0)
    const = lambda i: (0, 0)
    in_specs = [pl.BlockSpec((1, tm, di), lambda i: (0, i, 0)),
                pl.BlockSpec((1, tm, di), lambda i: (1, i, 0)),
                pl.BlockSpec((tm, di), row),
                pl.BlockSpec((tm, di), lambda i: (i, zb)),
                pl.BlockSpec((tm, d), row),
                pl.BlockSpec((tm, d), lambda i: (i, fb + 1)),
                pl.BlockSpec((tm, d), lambda i: (i, fb + 2)),
                pl.BlockSpec((tm, d), row),
                pl.BlockSpec((None, 1, d), modrow),
                pl.BlockSpec((None, 1, d), modrow),
                pl.BlockSpec((None, 1, d), modrow),
                pl.BlockSpec((1, di), const),
                pl.BlockSpec((1, di), const),
                pl.BlockSpec((di, d), const),
                pl.BlockSpec((d, d), const),
                pl.BlockSpec((d, d), const),
                pl.BlockSpec((1, d), const),
                pl.BlockSpec((d, LANES), const),
                pl.BlockSpec((1, LANES), const)]
    out_specs = [pl.BlockSpec((tm, d), row), pl.BlockSpec((tm, d // 2), row),
                 pl.BlockSpec((tm, LANES), row), pl.BlockSpec((tm, LANES), row), pl.BlockSpec((tm, LANES), row),
                 pl.BlockSpec((8, LANES), const)]
    out_shape = [jax.ShapeDtypeStruct((m, d), F32), jax.ShapeDtypeStruct((m, d // 2), jnp.uint32),
                 jax.ShapeDtypeStruct((m, LANES), jnp.int32), jax.ShapeDtypeStruct((m, LANES), F32),
                 jax.ShapeDtypeStruct((m, LANES), jnp.int32), jax.ShapeDtypeStruct((8, LANES), F32)]
    return pl.pallas_call(
        _tail_kernel,
        grid=(m // tm,),
        in_specs=in_specs,
        out_specs=out_specs,
        out_shape=out_shape,
        scratch_shapes=[pltpu.VMEM((8, LANES), F32)],
        compiler_params=_cparams("arbitrary"),
        name="tail",
    )(y2, y2, xbc2d, p2d, ff, p2d, p2d, x2d, g1, sc2, sh2, dsk, snw, wso, wfo, wo, n2w, wr, br)


def _gather_rows(src, idx):
    n_out = idx.shape[0]
    width = src.shape[1]
    win = SC_GATHER_WINDOW
    info = plsc.get_sparse_core_info()
    n_cores, n_workers = info.num_cores, info.num_cores * info.num_subcores
    per_worker = n_out // n_workers
    assert per_worker * n_workers == n_out and per_worker % (2 * win) == 0
    mesh = plsc.VectorSubcoreMesh(core_axis_name="core", subcore_axis_name="subcore")

    @functools.partial(
        pl.kernel, out_type=jax.ShapeDtypeStruct((n_out, width), src.dtype), mesh=mesh,
        scratch_types=[pltpu.VMEM((win,), jnp.int32), pltpu.VMEM((win,), jnp.int32),
                       pltpu.VMEM((win, width), src.dtype), pltpu.VMEM((win, width), src.dtype),
                       pltpu.SemaphoreType.DMA, pltpu.SemaphoreType.DMA,
                       pltpu.SemaphoreType.DMA, pltpu.SemaphoreType.DMA],
        name="gather_rows")
    def gather(src_hbm, idx_hbm, out_hbm, idx_a, idx_b, rows_a, rows_b, gsem_a, gsem_b, ssem_a, ssem_b):
        worker = lax.axis_index("subcore") * n_cores + lax.axis_index("core")
        base = worker * per_worker

        @pl.loop(0, per_worker, step=2 * win)
        def _(off):
            pltpu.sync_copy(idx_hbm.at[pl.ds(base + off, win)], idx_a)
            ga = pltpu.async_copy(src_hbm.at[idx_a], rows_a, gsem_a)
            pltpu.sync_copy(idx_hbm.at[pl.ds(base + off + win, win)], idx_b)
            gb = pltpu.async_copy(src_hbm.at[idx_b], rows_b, gsem_b)
            ga.wait()
            sa = pltpu.async_copy(rows_a, out_hbm.at[pl.ds(base + off, win)], ssem_a)
            gb.wait()
            sb = pltpu.async_copy(rows_b, out_hbm.at[pl.ds(base + off + win, win)], ssem_b)
            sa.wait()
            sb.wait()

    return gather(src, idx)


def _scatter_rows(src, dest, n_rows):
    m, width = src.shape
    win = SC_GATHER_WINDOW
    info = plsc.get_sparse_core_info()
    n_cores, n_workers = info.num_cores, info.num_cores * info.num_subcores
    per_worker = m // n_workers
    assert per_worker * n_workers == m and per_worker % win == 0
    mesh = plsc.VectorSubcoreMesh(core_axis_name="core", subcore_axis_name="subcore")

    @functools.partial(
        pl.kernel, out_type=jax.ShapeDtypeStruct((n_rows, width), src.dtype), mesh=mesh,
        scratch_types=[pltpu.VMEM((win, width), src.dtype)]
        + [pltpu.VMEM((win,), jnp.int32)] * TOP_K + [pltpu.SemaphoreType.DMA] * TOP_K,
        name="scatter_rows")
    def scatter(src_hbm, dest_hbm, out_hbm, rows_v, *rest):
        idx_v, sems = rest[:TOP_K], rest[TOP_K:]
        worker = lax.axis_index("subcore") * n_cores + lax.axis_index("core")
        base = worker * per_worker

        @pl.loop(0, per_worker, step=win)
        def _(off):
            t0 = base + off
            pltpu.sync_copy(src_hbm.at[pl.ds(t0, win)], rows_v)
            for k in range(TOP_K):
                pltpu.sync_copy(dest_hbm.at[pl.ds(k * m + t0, win)], idx_v[k])
            copies = [pltpu.async_copy(rows_v, out_hbm.at[idx_v[k]], sems[k]) for k in range(TOP_K)]
            for cp in copies:
                cp.wait()

    return scatter(src, dest)


def _expert_kernel(te_ref, nu_ref, tv_ref, x_ref, wgu_ref, bgu_ref, wd_ref, bd_ref, o_ref, wgu_scr, wd_scr):
    i = pl.program_id(0)
    used = i < nu_ref[0]
    new_expert = (i == 0) | (te_ref[i] != te_ref[jnp.maximum(i - 1, 0)])

    @pl.when(used & new_expert)
    def _():
        wgu_scr[...] = wgu_ref[...].astype(BF16)
        wd_scr[...] = wd_ref[...].astype(BF16)

    @pl.when(used)
    def _():
        dff = wd_ref.shape[0]
        half = wgu_ref.shape[0] // 2
        rows = lax.broadcasted_iota(jnp.int32, x_ref.shape, 0)
        xa, xb = _unpack_bf16_pair(jnp.where(rows < tv_ref[i], x_ref[...], jnp.uint32(0)))
        gu = (jnp.dot(xa.astype(BF16), wgu_scr[:half, :], preferred_element_type=F32)
              + jnp.dot(xb.astype(BF16), wgu_scr[half:, :], preferred_element_type=F32)) + bgu_ref[...]
        gate = jnp.minimum(gu[:, :dff], SWIGLU_LIMIT)
        up = jnp.clip(gu[:, dff:], -SWIGLU_LIMIT, SWIGLU_LIMIT)
        act = (up + 1.0) * gate * _sigmoid(SWIGLU_ALPHA * gate)
        y = jnp.dot(act.astype(BF16), wd_scr[...], preferred_element_type=F32) + bd_ref[...]
        o_ref[...] = _pack_bf16_pair(y)

    @pl.when(jnp.logical_not(used))
    def _():
        o_ref[...] = jnp.zeros(o_ref.shape, o_ref.dtype)


def _experts(xs, tile_expert, n_used, tile_valid, wgu, bgu, wd, bd):
    rows, dh = xs.shape
    d = 2 * dh
    tm = MOE_TILE
    n_tiles = rows // tm
    dff2 = wgu.shape[-1]

    def tile(i, te, nu, tv):
        return (jnp.minimum(i, nu[0] - 1), 0)

    def wsel(i, te, nu, tv):
        return (te[jnp.minimum(i, nu[0] - 1)], 0, 0)

    grid_spec = pltpu.PrefetchScalarGridSpec(
        num_scalar_prefetch=3,
        grid=(n_tiles,),
        in_specs=[pl.BlockSpec((tm, dh), tile),
                  pl.BlockSpec((None, d, dff2), wsel),
                  pl.BlockSpec((None, 1, dff2), wsel),
                  pl.BlockSpec((None, dff2 // 2, d), wsel),
                  pl.BlockSpec((None, 1, d), wsel)],
        out_specs=pl.BlockSpec((tm, dh), lambda i, te, nu, tv: (i, 0)),
        scratch_shapes=[pltpu.VMEM((d, dff2), BF16), pltpu.VMEM((dff2 // 2, d), BF16)],
    )
    return pl.pallas_call(
        _expert_kernel,
        grid_spec=grid_spec,
        out_shape=jax.ShapeDtypeStruct((rows, dh), jnp.uint32),
        compiler_params=_cparams("arbitrary"),
        name="experts",
    )(tile_expert, n_used, tile_valid, xs, wgu, bgu, wd, bd)


def _final_kernel(x1_ref, ya_ref, yb_ref, yc_ref, yd_ref, tw_ref, g2_ref, fw_ref, o_ref):
    d = x1_ref.shape[-1]
    half = d // 2
    tw = tw_ref[...]
    acc_hi = jnp.zeros((x1_ref.shape[0], half), F32)
    acc_lo = jnp.zeros((x1_ref.shape[0], half), F32)
    for k, y_ref in enumerate((ya_ref, yb_ref, yc_ref, yd_ref)):
        y_hi, y_lo = _unpack_bf16_pair(y_ref[...])
        acc_hi = acc_hi + tw[:, k:k + 1] * y_hi
        acc_lo = acc_lo + tw[:, k:k + 1] * y_lo
    x_hi = x1_ref[:, :half] + g2_ref[:, :half] * acc_hi
    x_lo = x1_ref[:, half:] + g2_ref[:, half:] * acc_lo
    ms = (jnp.sum(x_hi * x_hi, axis=-1, keepdims=True) + jnp.sum(x_lo * x_lo, axis=-1, keepdims=True)) / d
    inv = lax.rsqrt(ms + EPS)
    o_ref[:, :half] = x_hi * inv * fw_ref[:, :half]
    o_ref[:, half:] = x_lo * inv * fw_ref[:, half:]


def _final(x1, y4, tw, g2, fw, rows_per_mod, tm):
    m, d = x1.shape
    return pl.pallas_call(
        _final_kernel,
        grid=(m // tm,),
        in_specs=[pl.BlockSpec((tm, d), lambda i: (i, 0))]
        + [pl.BlockSpec((None, tm, d // 2), functools.partial(lambda k, i: (k, i, 0), k)) for k in range(TOP_K)]
        + [pl.BlockSpec((tm, LANES), lambda i: (i, 0)),
           pl.BlockSpec((None, 1, d), lambda i: ((i * tm) // rows_per_mod, 0, 0)),
           pl.BlockSpec((1, d), lambda i: (0, 0))],
        out_specs=pl.BlockSpec((tm, d), lambda i: (i, 0)),
        out_shape=jax.ShapeDtypeStruct((m, d), F32),
        compiler_params=_cparams("arbitrary"),
        name="final",
    )(x1, y4, y4, y4, y4, tw, g2, fw.reshape(1, d))


def _dispatch_plan(e, rank, counts, tm):
    n_assign = e.size
    padded = (counts + tm - 1) // tm * tm
    pad_end = jnp.cumsum(padded)
    pad_start = pad_end - padded
    dest = (pad_start[e] + rank).astype(jnp.int32)
    n_tiles = n_assign // tm + N_EXPERTS
    tile_start = jnp.arange(n_tiles, dtype=jnp.int32) * tm
    tile_expert = jnp.minimum(jnp.sum(pad_end[None, :] <= tile_start[:, None], axis=1), N_EXPERTS - 1).astype(jnp.int32)
    n_used = (pad_end[-1] // tm).astype(jnp.int32).reshape(1)
    tile_valid = jnp.clip(pad_start[tile_expert] + counts[tile_expert] - tile_start, 0, tm).astype(jnp.int32)
    return dest, tile_expert, n_used, tile_valid


def kernel(x, c, ctx, c_ctx, w_mod, b_mod, norm1_w, norm2_w, w_in, conv_w, conv_b, dt_bias, a_log, d_skip,
           ssd_norm_w, w_ssd_out, w_four_out, w_o, w_router, b_router, w_gate_up, b_gate_up, w_down, b_down,
           final_norm_w):
    bsz, n, d = x.shape
    n_ctx = ctx.shape[1]
    depth = w_mod.shape[0]
    assert depth == 1, "a stacked model would also need the context stream's residual update"
    d_inner = N_HEADS * HEAD_DIM
    off_dt = d_inner + 2 * BC_WIDTH
    m = bsz * n
    x2d = x.reshape(m, d)
    tm = min(512, n)

    for layer in range(depth):
        rows = -(-(bsz + 1) // 8) * 8
        c_rows = jnp.zeros((rows, d), F32).at[:bsz].set(c).at[bsz].set(c_ctx)
        mod = _modulation(c_rows, w_mod[layer], b_mod[layer])
        mods = [mod[:bsz, i * d:(i + 1) * d].reshape(bsz, 1, d) for i in range(N_MOD)]
        sh1, sc1, g1, sh2, sc2, g2 = mods
        sh1_c = mod[bsz:bsz + 1, :d].reshape(1, 1, d)
        sc1_c = mod[bsz:bsz + 1, d:2 * d].reshape(1, 1, d)

        wl = w_in[layer]
        w_main = jnp.concatenate([wl[:, :off_dt], wl[:, off_dt + 2 * N_HEADS:]], axis=1).astype(BF16)
        w_dt = jnp.zeros((d, 2 * LANES), F32)
        w_dt = w_dt.at[:, :N_HEADS].set(wl[:, off_dt:off_dt + N_HEADS])
        w_dt = w_dt.at[:, LANES:LANES + N_HEADS].set(wl[:, off_dt + N_HEADS:off_dt + 2 * N_HEADS]).astype(BF16)
        pad = jnp.zeros((2, 1, LANES - N_HEADS), F32)
        dt_bias2 = jnp.concatenate([dt_bias[layer].reshape(2, 1, N_HEADS), pad], axis=-1)
        a_log2 = jnp.concatenate([a_log[layer].reshape(2, 1, N_HEADS), pad], axis=-1)

        w_ctx = w_main[:, :off_dt]
        pc, dtc = _in_proj(ctx.reshape(bsz * n_ctx, d), norm1_w[layer], sc1_c, sh1_c, w_ctx, w_dt,
                           bsz * n_ctx, min(512, n_ctx))
        xbc_c = _conv_silu(pc.reshape(bsz, n_ctx, off_dt), conv_w[layer][:, :off_dt], conv_b[layer][:off_dt], off_dt)
        h0 = jnp.zeros((bsz, 2, N_GROUPS, D_STATE, HEADS_PER_GROUP * HEAD_DIM), F32)
        h_ctx = _ssd(xbc_c, dtc.reshape(bsz, n_ctx, 2 * LANES), dt_bias2, a_log2, h0, with_y=False)

        p, dtl = _in_proj(x2d, norm1_w[layer], sc1, sh1, w_main, w_dt, n, tm)
        n_conv = off_dt + 2 * BC_WIDTH
        xbc = _conv_silu(p.reshape(bsz, n, -1), conv_w[layer], conv_b[layer], n_conv)
        y2 = _ssd(xbc, dtl.reshape(bsz, n, 2 * LANES), dt_bias2, a_log2, h_ctx, with_y=True)
        ff = _fnet_mix(p, (n_conv + d_inner) // d, bsz, n)

        dsk = jnp.repeat(d_skip[layer].astype(F32), HEAD_DIM).reshape(1, d_inner)
        wr = jnp.zeros((d, LANES), F32).at[:, :N_EXPERTS].set(w_router[layer])
        br = jnp.full((1, LANES), NEG_BIG, F32).at[0, :N_EXPERTS].set(b_router[layer])
        x1, h2, ti, tw, rk, cnt = _tail(
            y2.reshape(2, m, d_inner), xbc.reshape(m, n_conv), p, ff, x2d, g1, sc2, sh2, dsk,
            ssd_norm_w[layer].reshape(1, d_inner), w_ssd_out[layer].astype(BF16), w_four_out[layer].astype(BF16),
            w_o[layer].astype(BF16), norm2_w[layer].reshape(1, d), wr, br, n, min(TAIL_ROWS, n))

        counts = cnt[0, :N_EXPERTS].astype(jnp.int32)
        dest, tile_expert, n_used, tile_valid = _dispatch_plan(ti[:, :TOP_K], rk[:, :TOP_K], counts, MOE_TILE)
        dest_km = dest.T.reshape(-1)
        xs = _scatter_rows(h2, dest_km, tile_expert.shape[0] * MOE_TILE)
        ys = _experts(xs, tile_expert, n_used, tile_valid, w_gate_up[layer],
                      b_gate_up[layer].reshape(N_EXPERTS, 1, -1), w_down[layer],
                      b_down[layer].reshape(N_EXPERTS, 1, -1))
        y4 = _gather_rows(ys, dest_km).reshape(TOP_K, m, d // 2)
        x2d = _final(x1, y4, tw, g2, final_norm_w, n, tm)
    return x2d.reshape(bsz, n, d)
```

```python
import functools
import math

import numpy as np
import jax
import jax.numpy as jnp
from jax import lax
from jax.experimental import pallas as pl
from jax.experimental.pallas import tpu as pltpu
from jax.experimental.pallas import tpu_sc as plsc

F32 = jnp.float32
BF16 = jnp.bfloat16

EPS = 1e-6
GRID_W = 64
N_MOD = 6
F_GROUPS = 8
F_GROUP_DIM = 128
HEAD_DIM = 64
N_HEADS = 32
N_GROUPS = 4
HEADS_PER_GROUP = N_HEADS // N_GROUPS
D_STATE = 128
BC_WIDTH = N_GROUPS * D_STATE
CONV_K = 5
CHUNK = 128
N_EXPERTS = 32
TOP_K = 4
SWIGLU_LIMIT = 7.0
SWIGLU_ALPHA = 1.702

LANES = 128
VMEM_LIMIT_BYTES = 56 * 1024 * 1024
NEG_BIG = -1e30
MOE_TILE = 512
TAIL_ROWS = 512
SC_GATHER_WINDOW = 64


def _cparams(*sem):
    return pltpu.CompilerParams(dimension_semantics=sem, vmem_limit_bytes=VMEM_LIMIT_BYTES)


def _sigmoid(v):
    return 1.0 / (1.0 + jnp.exp(-v))


def _silu(v):
    return v * _sigmoid(v)


def _softplus(v):
    return jnp.maximum(v, 0.0) + jnp.log(1.0 + jnp.exp(-jnp.abs(v)))


def _pack_bf16_pair(v):
    w = v.shape[1] // 2
    bits = lax.bitcast_convert_type(v.astype(BF16).astype(F32), jnp.uint32)
    return bits[:, :w] | (bits[:, w:] >> 16)


def _unpack_bf16_pair(p):
    hi = lax.bitcast_convert_type(p & jnp.uint32(0xFFFF0000), F32)
    lo = lax.bitcast_convert_type(p << 16, F32)
    return hi, lo


def _mod_kernel(c_ref, w_ref, b_ref, o_ref):
    s = _silu(c_ref[...]).astype(BF16)
    o_ref[...] = jnp.dot(s, w_ref[...].astype(BF16), preferred_element_type=F32) + b_ref[...]


def _modulation(c_rows, w_mod, b_mod):
    rows, d = c_rows.shape
    n_out = w_mod.shape[1]
    tn = 1024
    return pl.pallas_call(
        _mod_kernel,
        grid=(n_out // tn,),
        in_specs=[pl.BlockSpec((rows, d), lambda j: (0, 0)),
                  pl.BlockSpec((d, tn), lambda j: (0, j)),
                  pl.BlockSpec((1, tn), lambda j: (0, j))],
        out_specs=pl.BlockSpec((rows, tn), lambda j: (0, j)),
        out_shape=jax.ShapeDtypeStruct((rows, n_out), F32),
        compiler_params=_cparams("arbitrary"),
        name="modulation",
    )(c_rows, w_mod, b_mod.reshape(1, n_out))


_INPROJ_TN = 1024


def _inproj_kernel(x_ref, nw_ref, sc_ref, sh_ref, w_ref, wdt_ref, p_ref, dt_ref):
    x = x_ref[...]
    inv = lax.rsqrt(jnp.mean(x * x, axis=-1, keepdims=True) + EPS)
    h = (x * inv * nw_ref[...]) * (1.0 + sc_ref[...]) + sh_ref[...]
    hb = h.astype(BF16)
    dt_ref[...] = jnp.dot(hb, wdt_ref[...], preferred_element_type=F32)
    for j in range(w_ref.shape[1] // _INPROJ_TN):
        cols = slice(j * _INPROJ_TN, (j + 1) * _INPROJ_TN)
        p_ref[:, cols] = jnp.dot(hb, w_ref[:, cols], preferred_element_type=F32).astype(p_ref.dtype)


def _in_proj(x2d, norm_w, sc, sh, w_main, w_dt, rows_per_mod, tm):
    m, d = x2d.shape
    n = w_main.shape[1]
    ndt = w_dt.shape[1]
    resident = dict(pipeline_mode=pl.Buffered(1))
    return pl.pallas_call(
        _inproj_kernel,
        grid=(m // tm,),
        in_specs=[pl.BlockSpec((tm, d), lambda i: (i, 0)),
                  pl.BlockSpec((1, d), lambda i: (0, 0), **resident),
                  pl.BlockSpec((None, 1, d), lambda i: ((i * tm) // rows_per_mod, 0, 0)),
                  pl.BlockSpec((None, 1, d), lambda i: ((i * tm) // rows_per_mod, 0, 0)),
                  pl.BlockSpec((d, n), lambda i: (0, 0), **resident),
                  pl.BlockSpec((d, ndt), lambda i: (0, 0), **resident)],
        out_specs=[pl.BlockSpec((tm, n), lambda i: (i, 0)),
                   pl.BlockSpec((tm, ndt), lambda i: (i, 0))],
        out_shape=[jax.ShapeDtypeStruct((m, n), BF16),
                   jax.ShapeDtypeStruct((m, ndt), F32)],
        compiler_params=_cparams("arbitrary"),
        name="in_proj",
    )(x2d, norm_w.reshape(1, d), sc, sh, w_main, w_dt)


_CONV_HALO = 16


_CONV_ROWS = 64


def _conv_shift_matrix(rc):
    win = rc + 2 * _CONV_HALO
    s = np.zeros((rc, CONV_K * win), np.float32)
    for k in range(CONV_K):
        for l in range(rc):
            s[l, k * win + _CONV_HALO + l + k - CONV_K // 2] = 1.0
    return jnp.asarray(s, BF16)


def _conv_kernel(p_ref, w_ref, b_ref, s_ref, o_ref, *, n, rc):
    tc = o_ref.shape[-1]
    w = w_ref[...].astype(BF16)
    bias = b_ref[...]
    smat = s_ref[...]
    zeros = jnp.zeros((_CONV_HALO, tc), BF16)
    for r0 in range(0, n, rc):
        top = zeros if r0 == 0 else p_ref[r0 - _CONV_HALO:r0, :]
        bot = zeros if r0 + rc >= n else p_ref[r0 + rc:r0 + rc + _CONV_HALO, :]
        window = jnp.concatenate([top, p_ref[r0:r0 + rc, :], bot], axis=0)
        taps = jnp.concatenate([window * w[k:k + 1, :] for k in range(CONV_K)], axis=0)
        acc = jnp.dot(smat, taps, preferred_element_type=F32) + bias
        o_ref[r0:r0 + rc, :] = _silu(acc).astype(o_ref.dtype)


def _conv_silu(p3d, conv_w, conv_b, n_ch):
    bsz, n, _ = p3d.shape
    tc = 512
    rc = min(_CONV_ROWS, n)
    smat = _conv_shift_matrix(rc)
    return pl.pallas_call(
        functools.partial(_conv_kernel, n=n, rc=rc),
        grid=(bsz, n_ch // tc),
        in_specs=[pl.BlockSpec((None, n, tc), lambda b, j: (b, 0, j)),
                  pl.BlockSpec((CONV_K, tc), lambda b, j: (0, j)),
                  pl.BlockSpec((1, tc), lambda b, j: (0, j)),
                  pl.BlockSpec(smat.shape, lambda b, j: (0, 0))],
        out_specs=pl.BlockSpec((None, n, tc), lambda b, j: (b, 0, j)),
        out_shape=jax.ShapeDtypeStruct((bsz, n, n_ch), BF16),
        compiler_params=_cparams("arbitrary", "arbitrary"),
        name="conv_silu",
    )(p3d, conv_w, conv_b.reshape(1, -1), smat)


LOG2E = 1.4426950408889634
SSD_SUB = 8


def _head_expand_matrix():
    e = np.zeros((LANES, N_HEADS * HEAD_DIM), np.float32)
    for h in range(N_HEADS):
        e[h, h * HEAD_DIM:(h + 1) * HEAD_DIM] = 1.0
    return jnp.asarray(e, BF16)


def _ssd_chunk_terms(dt_raw, dt_bias, a_log, expand, tri, fwd):
    q = CHUNK
    dt = _softplus(dt_raw + dt_bias)
    da = dt * (-jnp.exp(a_log))
    hi = da.astype(BF16)
    r1 = da - hi.astype(F32)
    mid = r1.astype(BF16)
    lo = (r1 - mid.astype(F32)).astype(BF16)
    ones_tri = jnp.where(tri, 1.0, 0.0).astype(BF16)
    cs3 = jnp.dot(ones_tri, jnp.concatenate([hi, mid, lo], axis=1), preferred_element_type=F32)
    cs = cs3[:, :LANES] + cs3[:, LANES:2 * LANES] + cs3[:, 2 * LANES:]
    tot = jnp.where(fwd, cs[q - 1:q, :], cs[0:1, :])
    dte = dt * jnp.exp(tot - cs)
    dec = jnp.exp(tot)
    dec_hi = dec.astype(BF16)
    dec_lo = (dec - dec_hi.astype(F32)).astype(BF16)
    dec2 = jnp.concatenate([dec_hi, dec_lo, jnp.zeros((6, LANES), BF16)], axis=0)
    stacked = jnp.concatenate([jnp.exp(cs).astype(BF16), dte.astype(BF16), dec2], axis=0)
    spread = jnp.dot(stacked, expand, preferred_element_type=F32)
    ecs_e = spread[:q]
    dte_e = spread[q:2 * q].astype(BF16)
    dec_e = jnp.broadcast_to(spread[2 * q:2 * q + 1] + spread[2 * q + 1:2 * q + 2], (8, spread.shape[1]))
    cs2 = cs * LOG2E
    return cs2, cs2.T, dt.T, ecs_e, dte_e, dec_e


def _ssd_kernel(*refs, with_y, nc, n_sub):
    if with_y:
        (x_ref, b_ref, c_ref, dt_ref, dtn_ref, dtb_ref, alog_ref, exp_ref, h0_ref, y_ref,
         s_scr, cs_scr, dee_scr, dce_scr, ecs_scr) = refs
    else:
        (x_ref, b_ref, dt_ref, dtn_ref, dtb_ref, alog_ref, exp_ref, h0_ref, hout_ref,
         s_scr, cs_scr, dee_scr, dce_scr) = refs
    q = CHUNK
    gw = HEADS_PER_GROUP * HEAD_DIM
    d = pl.program_id(1)
    c = pl.program_id(2)
    row = lax.broadcasted_iota(jnp.int32, (q, q), 0)
    col = lax.broadcasted_iota(jnp.int32, (q, q), 1)
    fwd = d == 0
    tri = jnp.where(fwd, row - col, col - row) >= 0
    first_half = col < HEAD_DIM

    def store_terms(slot, terms):
        cs2, cs2_t, dt_t, ecs_e, dte_e, dec_e = terms
        cs_scr[slot, 0] = cs2
        cs_scr[slot, 1] = cs2_t
        cs_scr[slot, 2] = dt_t
        dee_scr[slot] = dte_e
        dce_scr[slot] = dec_e
        if with_y:
            ecs_scr[slot] = ecs_e

    def terms_of(ref, sub):
        rows = pl.ds(pl.multiple_of(sub * q, q), q)
        return _ssd_chunk_terms(ref[rows, :], dtb_ref[...], alog_ref[...], exp_ref[...], tri, fwd)

    def sub_of(j):
        return jnp.where(fwd, j, n_sub - 1 - j)

    @pl.when(c == 0)
    def _():
        s_scr[...] = h0_ref[...]
        store_terms(0, terms_of(dt_ref, sub_of(0)))

    for j in range(n_sub):
        slot = j % 2
        rows = pl.ds(pl.multiple_of(sub_of(j) * q, q), q)
        cs2 = cs_scr[slot, 0]
        cs2_t = cs_scr[slot, 1]
        dt_t = cs_scr[slot, 2]
        if j + 1 < n_sub:
            store_terms(1 - slot, terms_of(dt_ref, sub_of(j + 1)))
        else:
            store_terms(1 - slot, terms_of(dtn_ref, sub_of(0)))

        for g in range(N_GROUPS):
            cols = slice(g * gw, (g + 1) * gw)
            bg = b_ref[rows, g * D_STATE:(g + 1) * D_STATE]
            xg = x_ref[rows, cols]
            s_old = s_scr[g]
            if with_y:
                cg = c_ref[rows, g * D_STATE:(g + 1) * D_STATE]
                cb = lax.dot_general(cg, bg, (((1,), (1,)), ((), ())), preferred_element_type=F32)
                y_off = jnp.dot(cg, s_old.astype(BF16), preferred_element_type=F32)
                for k in range(HEADS_PER_GROUP // 2):
                    pair = g * (HEADS_PER_GROUP // 2) + k
                    lanes = slice(pair * LANES, (pair + 1) * LANES)
                    xp = xg[:, k * LANES:(k + 1) * LANES]
                    xz = jnp.zeros_like(xp)
                    lhs = []
                    for hh in (2 * pair, 2 * pair + 1):
                        colb = jnp.broadcast_to(cs2[:, hh:hh + 1], (q, q))
                        rowb = cs2_t[hh:hh + 1, :]
                        decay = jnp.exp2(jnp.where(tri, colb - rowb, -jnp.inf))
                        lhs.append((cb * decay * dt_t[hh:hh + 1, :]).astype(BF16))
                    y_diag = jnp.dot(jnp.concatenate(lhs, axis=1),
                                     jnp.concatenate([jnp.where(first_half, xp, xz), jnp.where(first_half, xz, xp)],
                                                     axis=0),
                                     preferred_element_type=F32)
                    y = y_diag + ecs_scr[slot, :, lanes] * y_off[:, k * LANES:(k + 1) * LANES]
                    y_ref[rows, lanes] = y.astype(y_ref.dtype)
            xdte = xg * dee_scr[slot, :, cols]
            upd = lax.dot_general(bg, xdte, (((0,), (0,)), ((), ())), preferred_element_type=F32)
            s_scr[g] = s_old * dce_scr[slot, 0:1, cols] + upd

    if not with_y:
        @pl.when(c == nc - 1)
        def _():
            hout_ref[...] = s_scr[...]


def _ssd(xbc, dt_raw, dt_bias2, a_log2, h0, with_y):
    bsz, n, _ = xbc.shape
    n_sub = min(SSD_SUB, n // CHUNK)
    assert n_sub % 2 == 0 and n % (n_sub * CHUNK) == 0
    blk = n_sub * CHUNK
    nc = n // blk
    d_inner = N_HEADS * HEAD_DIM
    gw = HEADS_PER_GROUP * HEAD_DIM
    x_blk = d_inner // BC_WIDTH

    def ceff(dd, cc):
        return cc + dd * (nc - 1 - 2 * cc)

    def cnext(dd, cc):
        return ceff(dd, jnp.minimum(cc + 1, nc - 1))

    in_specs = [pl.BlockSpec((None, blk, d_inner), lambda b, dd, cc: (b, ceff(dd, cc), 0)),
                pl.BlockSpec((None, blk, BC_WIDTH), lambda b, dd, cc: (b, ceff(dd, cc), x_blk + dd))]
    args = [xbc, xbc]
    if with_y:
        in_specs.append(pl.BlockSpec((None, blk, BC_WIDTH), lambda b, dd, cc: (b, ceff(dd, cc), x_blk + 2 + dd)))
        args.append(xbc)
    in_specs += [pl.BlockSpec((None, blk, LANES), lambda b, dd, cc: (b, ceff(dd, cc), dd)),
                 pl.BlockSpec((None, blk, LANES), lambda b, dd, cc: (b, cnext(dd, cc), dd)),
                 pl.BlockSpec((None, 1, LANES), lambda b, dd, cc: (dd, 0, 0)),
                 pl.BlockSpec((None, 1, LANES), lambda b, dd, cc: (dd, 0, 0)),
                 pl.BlockSpec((LANES, d_inner), lambda b, dd, cc: (0, 0)),
                 pl.BlockSpec((None, None, N_GROUPS, D_STATE, gw), lambda b, dd, cc: (b, dd, 0, 0, 0))]
    args += [dt_raw, dt_raw, dt_bias2, a_log2, _head_expand_matrix(), h0]
    scratch = [pltpu.VMEM((N_GROUPS, D_STATE, gw), F32),
               pltpu.VMEM((2, 3, CHUNK, LANES), F32),
               pltpu.VMEM((2, CHUNK, d_inner), BF16),
               pltpu.VMEM((2, 8, d_inner), F32)]
    if with_y:
        out_specs = pl.BlockSpec((None, None, blk, d_inner), lambda b, dd, cc: (dd, b, ceff(dd, cc), 0))
        out_shape = jax.ShapeDtypeStruct((2, bsz, n, d_inner), BF16)
        scratch += [pltpu.VMEM((2, CHUNK, d_inner), F32)]
    else:
        out_specs = pl.BlockSpec((None, None, N_GROUPS, D_STATE, gw), lambda b, dd, cc: (b, dd, 0, 0, 0))
        out_shape = jax.ShapeDtypeStruct((bsz, 2, N_GROUPS, D_STATE, gw), F32)
    return pl.pallas_call(
        functools.partial(_ssd_kernel, with_y=with_y, nc=nc, n_sub=n_sub),
        grid=(bsz, 2, nc),
        in_specs=in_specs,
        out_specs=out_specs,
        out_shape=out_shape,
        scratch_shapes=scratch,
        compiler_params=_cparams("arbitrary", "arbitrary", "arbitrary"),
        name="ssd_scan" if with_y else "ssd_ctx_state",
    )(*args)


def _dft_mats(n):
    k = np.arange(n)
    ang = 2.0 * np.pi * ((k[:, None] * k[None, :]) % n) / n
    return np.cos(ang), np.sin(ang)


def _strided_rows(scr, t, start, count, stride, width):
    return jnp.concatenate([scr[t, lb, pl.ds(start, count, stride=stride), :] for lb in range(width // LANES)],
                           axis=1)


def _fnet_stage1_kernel(u_ref, chan_ref, m1_ref, twc_ref, tws_ref, o_ref, zs_scr, *, tcs):
    r = u_ref.shape[0]
    width = u_ref.shape[-1]
    u2 = u_ref[...].reshape(r * tcs, width)
    chan = chan_ref[...]
    for g in range(F_GROUPS):
        pq = jnp.dot(u2[:, g * F_GROUP_DIM:(g + 1) * F_GROUP_DIM], chan, preferred_element_type=F32)
        zs_scr[0, g] = pq[:, :F_GROUP_DIM]
        zs_scr[1, g] = pq[:, F_GROUP_DIM:]
    m1 = m1_ref[...]
    for j in range(tcs):
        zz = jnp.concatenate([_strided_rows(zs_scr, 0, j, r, tcs, width),
                              _strided_rows(zs_scr, 1, j, r, tcs, width)], axis=0).astype(BF16)
        u = jnp.dot(m1, zz, preferred_element_type=F32)
        ur = u[:r]
        ui = u[r:]
        tc = twc_ref[j]
        ts = tws_ref[j]
        o_ref[0, j] = (ur * tc + ui * ts).astype(o_ref.dtype)
        o_ref[1, j] = (ui * tc - ur * ts).astype(o_ref.dtype)


def _fnet_stage2_kernel(u_ref, k2_ref, o_ref):
    c, tbs, width = o_ref.shape
    uu = jnp.concatenate([u_ref[0].reshape(c * tbs, width), u_ref[1].reshape(c * tbs, width)], axis=0)
    out = jnp.dot(k2_ref[...], uu, preferred_element_type=F32)
    o_ref[...] = out.reshape(c, tbs, width).astype(o_ref.dtype)


def _fnet_mix(p2d, col_blk, bsz, n):
    m = p2d.shape[0]
    width = F_GROUPS * F_GROUP_DIM
    cgrid = GRID_W
    rgrid = n // cgrid
    scale = 1.0 / math.sqrt(n * F_GROUP_DIM)

    cc, sc = _dft_mats(F_GROUP_DIM)
    chan = jnp.asarray(np.concatenate([cc, sc], axis=1) * scale, BF16)

    cr, sr = _dft_mats(rgrid)
    m1 = jnp.asarray(np.block([[cr, -sr], [-sr, -cr]]), BF16)
    bb = np.arange(rgrid)[None, :]
    ci = np.arange(cgrid)[:, None]
    ang = 2.0 * np.pi * ((ci * bb) % n) / n
    twc = jnp.asarray(np.cos(ang)[:, :, None], F32)
    tws = jnp.asarray(np.sin(ang)[:, :, None], F32)
    tcs = 16
    p4 = p2d.reshape(bsz, rgrid, cgrid, p2d.shape[-1])
    u5 = pl.pallas_call(
        functools.partial(_fnet_stage1_kernel, tcs=tcs),
        grid=(bsz, cgrid // tcs),
        in_specs=[pl.BlockSpec((None, rgrid, tcs, width), lambda b, j: (b, 0, j, col_blk)),
                  pl.BlockSpec((F_GROUP_DIM, 2 * F_GROUP_DIM), lambda b, j: (0, 0)),
                  pl.BlockSpec((2 * rgrid, 2 * rgrid), lambda b, j: (0, 0)),
                  pl.BlockSpec((tcs, rgrid, 1), lambda b, j: (j, 0, 0)),
                  pl.BlockSpec((tcs, rgrid, 1), lambda b, j: (j, 0, 0))],
        out_specs=pl.BlockSpec((2, None, tcs, rgrid, width), lambda b, j: (0, b, j, 0, 0)),
        out_shape=jax.ShapeDtypeStruct((2, bsz, cgrid, rgrid, width), BF16),
        scratch_shapes=[pltpu.VMEM((2, width // LANES, rgrid * tcs, LANES), F32)],
        compiler_params=_cparams("arbitrary", "arbitrary"),
        name="fnet_stage1",
    )(p4, chan, m1, twc, tws)

    cc2, sc2 = _dft_mats(cgrid)
    tbs = min(16, rgrid)
    eye = np.eye(tbs)
    k2 = jnp.asarray(np.concatenate([np.kron(cc2, eye), np.kron(sc2, eye)], axis=1), BF16)
    out = pl.pallas_call(
        _fnet_stage2_kernel,
        grid=(bsz, rgrid // tbs),
        in_specs=[pl.BlockSpec((2, None, cgrid, tbs, width), lambda b, j: (0, b, 0, j, 0)),
                  pl.BlockSpec(k2.shape, lambda b, j: (0, 0))],
        out_specs=pl.BlockSpec((None, cgrid, tbs, width), lambda b, j: (b, 0, j, 0)),
        out_shape=jax.ShapeDtypeStruct((bsz, cgrid, rgrid, width), BF16),
        compiler_params=_cparams("arbitrary", "arbitrary"),
        name="fnet_stage2",
    )(u5, k2)
    return out.reshape(m, width)


def _tail_kernel(yf_ref, yb_ref, xs_ref, z_ref, ff_ref, gf_ref, gs_ref, x_ref, g1_ref, sc2_ref, sh2_ref,
                 dsk_ref, snw_ref, wso_ref, wfo_ref, wo_ref, n2w_ref, wr_ref, br_ref,
                 x1_ref, h2_ref, ti_ref, tw_ref, rk_ref, cnt_ref, carry_scr):
    y = (yf_ref[0] + yb_ref[0]).astype(F32) + dsk_ref[...] * xs_ref[...].astype(F32)
    g = y * _silu(z_ref[...]).astype(F32)
    inv = lax.rsqrt(jnp.mean(g * g, axis=-1, keepdims=True) + EPS)
    gn = (g * inv * snw_ref[...]).astype(BF16)
    y_ssd = jnp.dot(gn, wso_ref[...], preferred_element_type=F32)
    y_four = jnp.dot(ff_ref[...], wfo_ref[...], preferred_element_type=F32)
    t = _sigmoid(gf_ref[...]) * y_four.astype(BF16) + _sigmoid(gs_ref[...]) * y_ssd.astype(BF16)
    mix = jnp.dot(t, wo_ref[...], preferred_element_type=F32)
    x1 = x_ref[...] + g1_ref[...] * mix
    x1_ref[...] = x1
    inv2 = lax.rsqrt(jnp.mean(x1 * x1, axis=-1, keepdims=True) + EPS)
    h2 = (x1 * inv2 * n2w_ref[...]) * (1.0 + sc2_ref[...]) + sh2_ref[...]
    h2_ref[...] = _pack_bf16_pair(h2)
    h_hi = h2.astype(BF16)
    h_lo = (h2 - h_hi.astype(F32)).astype(BF16)
    wr = wr_ref[...]
    w_hi = wr.astype(BF16)
    w_lo = (wr - w_hi.astype(F32)).astype(BF16)
    logits = (jnp.dot(h_hi, w_hi, preferred_element_type=F32)
              + jnp.dot(h_hi, w_lo, preferred_element_type=F32)
              + jnp.dot(h_lo, w_hi, preferred_element_type=F32)) + br_ref[...]
    tm = logits.shape[0]
    lane = lax.broadcasted_iota(jnp.int32, (tm, LANES), 1)
    vals, idxs = [], []
    cur = logits
    for _ in range(TOP_K):
        mx = jnp.max(cur, axis=-1, keepdims=True)
        ix = jnp.min(jnp.where(cur == mx, lane, LANES), axis=-1, keepdims=True)
        vals.append(mx)
        idxs.append(ix)
        cur = jnp.where(lane == ix, NEG_BIG * 2.0, cur)
    es = [jnp.exp(v - vals[0]) for v in vals]
    den = es[0] + es[1] + es[2] + es[3]
    ti = jnp.zeros((tm, LANES), jnp.int32)
    tw = jnp.zeros((tm, LANES), F32)
    for k in range(TOP_K):
        ti = jnp.where(lane == k, idxs[k], ti)
        tw = jnp.where(lane == k, es[k] / den, tw)
    ti_ref[...] = ti.T[:8, :]
    tw_ref[...] = tw

    @pl.when(pl.program_id(0) == 0)
    def _():
        carry_scr[...] = jnp.zeros(carry_scr.shape, F32)

    onehots = [jnp.where(lane == ix, 1.0, 0.0) for ix in idxs]
    cnt = onehots[0] + onehots[1] + onehots[2] + onehots[3]
    r_i = lax.broadcasted_iota(jnp.int32, (tm, tm), 0)
    c_i = lax.broadcasted_iota(jnp.int32, (tm, tm), 1)
    earlier = jnp.where(r_i > c_i, 1.0, 0.0).astype(BF16)
    before = jnp.dot(earlier, cnt.astype(BF16), preferred_element_type=F32) + carry_scr[0:1, :]
    rk = jnp.zeros((tm, LANES), jnp.int32)
    for k in range(TOP_K):
        rank_k = jnp.sum(onehots[k] * before, axis=-1, keepdims=True)
        rk = jnp.where(lane == k, rank_k.astype(jnp.int32), rk)
    rk_ref[...] = rk.T[:8, :]
    total = carry_scr[...] + jnp.sum(cnt, axis=0, keepdims=True)
    carry_scr[...] = total
    cnt_ref[...] = total


def _tail(y2, xbc2d, p2d, ff, x2d, g1, sc2, sh2, dsk, snw, wso, wfo, wo, n2w, wr, br, rows_per_mod, tm):
    m, d = x2d.shape
    di = y2.shape[-1]
    zb = (2 * di) // di
    fb = (3 * di) // d
    row = lambda i: (i, 0)
    modrow = lambda i: ((i * tm) // rows_per_mod, 0, 0)
    const = lambda i: (0, 0)
    in_specs = [pl.BlockSpec((1, tm, di), lambda i: (0, i, 0)),
                pl.BlockSpec((1, tm, di), lambda i: (1, i, 0)),
                pl.BlockSpec((tm, di), row),
                pl.BlockSpec((tm, di), lambda i: (i, zb)),
                pl.BlockSpec((tm, d), row),
                pl.BlockSpec((tm, d), lambda i: (i, fb + 1)),
                pl.BlockSpec((tm, d), lambda i: (i, fb + 2)),
                pl.BlockSpec((tm, d), row),
                pl.BlockSpec((None, 1, d), modrow),
                pl.BlockSpec((None, 1, d), modrow),
                pl.BlockSpec((None, 1, d), modrow),
                pl.BlockSpec((1, di), const),
                pl.BlockSpec((1, di), const),
                pl.BlockSpec((di, d), const),
                pl.BlockSpec((d, d), const),
                pl.BlockSpec((d, d), const),
                pl.BlockSpec((1, d), const),
                pl.BlockSpec((d, LANES), const),
                pl.BlockSpec((1, LANES), const)]
    tcol = lambda i: (0, i)
    out_specs = [pl.BlockSpec((tm, d), row), pl.BlockSpec((tm, d // 2), row),
                 pl.BlockSpec((8, tm), tcol), pl.BlockSpec((tm, LANES), row), pl.BlockSpec((8, tm), tcol),
                 pl.BlockSpec((8, LANES), const)]
    out_shape = [jax.ShapeDtypeStruct((m, d), F32), jax.ShapeDtypeStruct((m, d // 2), jnp.uint32),
                 jax.ShapeDtypeStruct((8, m), jnp.int32), jax.ShapeDtypeStruct((m, LANES), F32),
                 jax.ShapeDtypeStruct((8, m), jnp.int32), jax.ShapeDtypeStruct((8, LANES), F32)]
    return pl.pallas_call(
        _tail_kernel,
        grid=(m // tm,),
        in_specs=in_specs,
        out_specs=out_specs,
        out_shape=out_shape,
        scratch_shapes=[pltpu.VMEM((8, LANES), F32)],
        compiler_params=_cparams("arbitrary"),
        name="tail",
    )(y2, y2, xbc2d, p2d, ff, p2d, p2d, x2d, g1, sc2, sh2, dsk, snw, wso, wfo, wo, n2w, wr, br)


def _gather_rows(src, idx):
    n_out = idx.shape[0]
    width = src.shape[1]
    win = SC_GATHER_WINDOW
    info = plsc.get_sparse_core_info()
    n_cores, n_workers = info.num_cores, info.num_cores * info.num_subcores
    per_worker = n_out // n_workers
    assert per_worker * n_workers == n_out and per_worker % (2 * win) == 0
    mesh = plsc.VectorSubcoreMesh(core_axis_name="core", subcore_axis_name="subcore")

    @functools.partial(
        pl.kernel, out_type=jax.ShapeDtypeStruct((n_out, width), src.dtype), mesh=mesh,
        scratch_types=[pltpu.VMEM((win,), jnp.int32), pltpu.VMEM((win,), jnp.int32),
                       pltpu.VMEM((win, width), src.dtype), pltpu.VMEM((win, width), src.dtype),
                       pltpu.SemaphoreType.DMA, pltpu.SemaphoreType.DMA,
                       pltpu.SemaphoreType.DMA, pltpu.SemaphoreType.DMA],
        name="gather_rows")
    def gather(src_hbm, idx_hbm, out_hbm, idx_a, idx_b, rows_a, rows_b, gsem_a, gsem_b, ssem_a, ssem_b):
        worker = lax.axis_index("subcore") * n_cores + lax.axis_index("core")
        base = worker * per_worker

        @pl.loop(0, per_worker, step=2 * win)
        def _(off):
            pltpu.sync_copy(idx_hbm.at[pl.ds(base + off, win)], idx_a)
            ga = pltpu.async_copy(src_hbm.at[idx_a], rows_a, gsem_a)
            pltpu.sync_copy(idx_hbm.at[pl.ds(base + off + win, win)], idx_b)
            gb = pltpu.async_copy(src_hbm.at[idx_b], rows_b, gsem_b)
            ga.wait()
            sa = pltpu.async_copy(rows_a, out_hbm.at[pl.ds(base + off, win)], ssem_a)
            gb.wait()
            sb = pltpu.async_copy(rows_b, out_hbm.at[pl.ds(base + off + win, win)], ssem_b)
            sa.wait()
            sb.wait()

    return gather(src, idx)


def _scatter_rows(src, dest, n_rows):
    m, width = src.shape
    win = SC_GATHER_WINDOW
    info = plsc.get_sparse_core_info()
    n_cores, n_workers = info.num_cores, info.num_cores * info.num_subcores
    per_worker = m // n_workers
    assert per_worker * n_workers == m and per_worker % win == 0
    mesh = plsc.VectorSubcoreMesh(core_axis_name="core", subcore_axis_name="subcore")

    @functools.partial(
        pl.kernel, out_type=jax.ShapeDtypeStruct((n_rows, width), src.dtype), mesh=mesh,
        scratch_types=[pltpu.VMEM((win, width), src.dtype)]
        + [pltpu.VMEM((win,), jnp.int32)] * TOP_K + [pltpu.SemaphoreType.DMA] * (TOP_K + 1),
        name="scatter_rows")
    def scatter(src_hbm, dest_hbm, out_hbm, rows_v, *rest):
        idx_v, sems, row_sem = rest[:TOP_K], rest[TOP_K:2 * TOP_K], rest[2 * TOP_K]
        worker = lax.axis_index("subcore") * n_cores + lax.axis_index("core")
        base = worker * per_worker

        @pl.loop(0, per_worker, step=win)
        def _(off):
            t0 = base + off
            loads = [pltpu.async_copy(src_hbm.at[pl.ds(t0, win)], rows_v, row_sem)]
            loads += [pltpu.async_copy(dest_hbm.at[pl.ds(k * m + t0, win)], idx_v[k], sems[k]) for k in range(TOP_K)]
            for ld in loads:
                ld.wait()
            copies = [pltpu.async_copy(rows_v, out_hbm.at[idx_v[k]], sems[k]) for k in range(TOP_K)]
            for cp in copies:
                cp.wait()

    return scatter(src, dest)


def _expert_kernel(te_ref, nu_ref, tv_ref, x_ref, wgu_ref, bgu_ref, wd_ref, bd_ref, o_ref, wgu_scr, wd_scr):
    i = pl.program_id(0)
    used = i < nu_ref[0]
    new_expert = (i == 0) | (te_ref[i] != te_ref[jnp.maximum(i - 1, 0)])

    @pl.when(used & new_expert)
    def _():
        wgu_scr[...] = wgu_ref[...].astype(BF16)
        wd_scr[...] = wd_ref[...].astype(BF16)

    @pl.when(used)
    def _():
        dff = wd_ref.shape[0]
        half = wgu_ref.shape[0] // 2
        rows = lax.broadcasted_iota(jnp.int32, x_ref.shape, 0)
        xa, xb = _unpack_bf16_pair(jnp.where(rows < tv_ref[i], x_ref[...], jnp.uint32(0)))
        gu = (jnp.dot(xa.astype(BF16), wgu_scr[:half, :], preferred_element_type=F32)
              + jnp.dot(xb.astype(BF16), wgu_scr[half:, :], preferred_element_type=F32)) + bgu_ref[...]
        gate = jnp.minimum(gu[:, :dff], SWIGLU_LIMIT)
        up = jnp.clip(gu[:, dff:], -SWIGLU_LIMIT, SWIGLU_LIMIT)
        act = (up + 1.0) * gate * _sigmoid(SWIGLU_ALPHA * gate)
        y = jnp.dot(act.astype(BF16), wd_scr[...], preferred_element_type=F32) + bd_ref[...]
        o_ref[...] = _pack_bf16_pair(y)

    @pl.when(jnp.logical_not(used))
    def _():
        o_ref[...] = jnp.zeros(o_ref.shape, o_ref.dtype)


def _experts(xs, tile_expert, n_used, tile_valid, wgu, bgu, wd, bd):
    rows, dh = xs.shape
    d = 2 * dh
    tm = MOE_TILE
    n_tiles = rows // tm
    dff2 = wgu.shape[-1]

    def tile(i, te, nu, tv):
        return (jnp.minimum(i, nu[0] - 1), 0)

    def wsel(i, te, nu, tv):
        return (te[jnp.minimum(i, nu[0] - 1)], 0, 0)

    grid_spec = pltpu.PrefetchScalarGridSpec(
        num_scalar_prefetch=3,
        grid=(n_tiles,),
        in_specs=[pl.BlockSpec((tm, dh), tile),
                  pl.BlockSpec((None, d, dff2), wsel),
                  pl.BlockSpec((None, 1, dff2), wsel),
                  pl.BlockSpec((None, dff2 // 2, d), wsel),
                  pl.BlockSpec((None, 1, d), wsel)],
        out_specs=pl.BlockSpec((tm, dh), lambda i, te, nu, tv: (i, 0)),
        scratch_shapes=[pltpu.VMEM((d, dff2), BF16), pltpu.VMEM((dff2 // 2, d), BF16)],
    )
    return pl.pallas_call(
        _expert_kernel,
        grid_spec=grid_spec,
        out_shape=jax.ShapeDtypeStruct((rows, dh), jnp.uint32),
        compiler_params=_cparams("arbitrary"),
        name="experts",
    )(tile_expert, n_used, tile_valid, xs, wgu, bgu, wd, bd)


def _final_kernel(x1_ref, ya_ref, yb_ref, yc_ref, yd_ref, tw_ref, g2_ref, fw_ref, o_ref):
    d = x1_ref.shape[-1]
    half = d // 2
    tw = tw_ref[...]
    acc_hi = jnp.zeros((x1_ref.shape[0], half), F32)
    acc_lo = jnp.zeros((x1_ref.shape[0], half), F32)
    for k, y_ref in enumerate((ya_ref, yb_ref, yc_ref, yd_ref)):
        y_hi, y_lo = _unpack_bf16_pair(y_ref[...])
        acc_hi = acc_hi + tw[:, k:k + 1] * y_hi
        acc_lo = acc_lo + tw[:, k:k + 1] * y_lo
    x_hi = x1_ref[:, :half] + g2_ref[:, :half] * acc_hi
    x_lo = x1_ref[:, half:] + g2_ref[:, half:] * acc_lo
    ms = (jnp.sum(x_hi * x_hi, axis=-1, keepdims=True) + jnp.sum(x_lo * x_lo, axis=-1, keepdims=True)) / d
    inv = lax.rsqrt(ms + EPS)
    o_ref[:, :half] = x_hi * inv * fw_ref[:, :half]
    o_ref[:, half:] = x_lo * inv * fw_ref[:, half:]


def _final(x1, y4, tw, g2, fw, rows_per_mod, tm):
    m, d = x1.shape
    return pl.pallas_call(
        _final_kernel,
        grid=(m // tm,),
        in_specs=[pl.BlockSpec((tm, d), lambda i: (i, 0))]
        + [pl.BlockSpec((None, tm, d // 2), functools.partial(lambda k, i: (k, i, 0), k)) for k in range(TOP_K)]
        + [pl.BlockSpec((tm, LANES), lambda i: (i, 0)),
           pl.BlockSpec((None, 1, d), lambda i: ((i * tm) // rows_per_mod, 0, 0)),
           pl.BlockSpec((1, d), lambda i: (0, 0))],
        out_specs=pl.BlockSpec((tm, d), lambda i: (i, 0)),
        out_shape=jax.ShapeDtypeStruct((m, d), F32),
        compiler_params=_cparams("arbitrary"),
        name="final",
    )(x1, y4, y4, y4, y4, tw, g2, fw.reshape(1, d))


def _dispatch_plan(e, rank, counts, tm):
    n_assign = e.size
    padded = (counts + tm - 1) // tm * tm
    pad_end = jnp.cumsum(padded)
    pad_start = pad_end - padded
    dest = (pad_start[e] + rank).astype(jnp.int32)
    n_tiles = n_assign // tm + N_EXPERTS
    tile_start = jnp.arange(n_tiles, dtype=jnp.int32) * tm
    tile_expert = jnp.minimum(jnp.sum(pad_end[None, :] <= tile_start[:, None], axis=1), N_EXPERTS - 1).astype(jnp.int32)
    n_used = (pad_end[-1] // tm).astype(jnp.int32).reshape(1)
    tile_valid = jnp.clip(pad_start[tile_expert] + counts[tile_expert] - tile_start, 0, tm).astype(jnp.int32)
    return dest, tile_expert, n_used, tile_valid


def kernel(x, c, ctx, c_ctx, w_mod, b_mod, norm1_w, norm2_w, w_in, conv_w, conv_b, dt_bias, a_log, d_skip,
           ssd_norm_w, w_ssd_out, w_four_out, w_o, w_router, b_router, w_gate_up, b_gate_up, w_down, b_down,
           final_norm_w):
    bsz, n, d = x.shape
    n_ctx = ctx.shape[1]
    depth = w_mod.shape[0]
    assert depth == 1, "a stacked model would also need the context stream's residual update"
    d_inner = N_HEADS * HEAD_DIM
    off_dt = d_inner + 2 * BC_WIDTH
    m = bsz * n
    x2d = x.reshape(m, d)
    tm = min(512, n)

    for layer in range(depth):
        rows = -(-(bsz + 1) // 8) * 8
        c_rows = jnp.zeros((rows, d), F32).at[:bsz].set(c).at[bsz].set(c_ctx)
        mod = _modulation(c_rows, w_mod[layer], b_mod[layer])
        mods = [mod[:bsz, i * d:(i + 1) * d].reshape(bsz, 1, d) for i in range(N_MOD)]
        sh1, sc1, g1, sh2, sc2, g2 = mods
        sh1_c = mod[bsz:bsz + 1, :d].reshape(1, 1, d)
        sc1_c = mod[bsz:bsz + 1, d:2 * d].reshape(1, 1, d)

        wl = w_in[layer]
        w_main = jnp.concatenate([wl[:, :off_dt], wl[:, off_dt + 2 * N_HEADS:]], axis=1).astype(BF16)
        w_dt = jnp.zeros((d, 2 * LANES), F32)
        w_dt = w_dt.at[:, :N_HEADS].set(wl[:, off_dt:off_dt + N_HEADS])
        w_dt = w_dt.at[:, LANES:LANES + N_HEADS].set(wl[:, off_dt + N_HEADS:off_dt + 2 * N_HEADS]).astype(BF16)
        pad = jnp.zeros((2, 1, LANES - N_HEADS), F32)
        dt_bias2 = jnp.concatenate([dt_bias[layer].reshape(2, 1, N_HEADS), pad], axis=-1)
        a_log2 = jnp.concatenate([a_log[layer].reshape(2, 1, N_HEADS), pad], axis=-1)

        w_ctx = w_main[:, :off_dt]
        pc, dtc = _in_proj(ctx.reshape(bsz * n_ctx, d), norm1_w[layer], sc1_c, sh1_c, w_ctx, w_dt,
                           bsz * n_ctx, min(512, n_ctx))
        xbc_c = _conv_silu(pc.reshape(bsz, n_ctx, off_dt), conv_w[layer][:, :off_dt], conv_b[layer][:off_dt], off_dt)
        h0 = jnp.zeros((bsz, 2, N_GROUPS, D_STATE, HEADS_PER_GROUP * HEAD_DIM), F32)
        h_ctx = _ssd(xbc_c, dtc.reshape(bsz, n_ctx, 2 * LANES), dt_bias2, a_log2, h0, with_y=False)

        p, dtl = _in_proj(x2d, norm1_w[layer], sc1, sh1, w_main, w_dt, n, tm)
        n_conv = off_dt + 2 * BC_WIDTH
        xbc = _conv_silu(p.reshape(bsz, n, -1), conv_w[layer], conv_b[layer], n_conv)
        y2 = _ssd(xbc, dtl.reshape(bsz, n, 2 * LANES), dt_bias2, a_log2, h_ctx, with_y=True)
        ff = _fnet_mix(p, (n_conv + d_inner) // d, bsz, n)

        dsk = jnp.repeat(d_skip[layer].astype(F32), HEAD_DIM).reshape(1, d_inner)
        wr = jnp.zeros((d, LANES), F32).at[:, :N_EXPERTS].set(w_router[layer])
        br = jnp.full((1, LANES), NEG_BIG, F32).at[0, :N_EXPERTS].set(b_router[layer])
        x1, h2, ti, tw, rk, cnt = _tail(
            y2.reshape(2, m, d_inner), xbc.reshape(m, n_conv), p, ff, x2d, g1, sc2, sh2, dsk,
            ssd_norm_w[layer].reshape(1, d_inner), w_ssd_out[layer].astype(BF16), w_four_out[layer].astype(BF16),
            w_o[layer].astype(BF16), norm2_w[layer].reshape(1, d), wr, br, n, min(TAIL_ROWS, n))

        counts = cnt[0, :N_EXPERTS].astype(jnp.int32)
        dest, tile_expert, n_used, tile_valid = _dispatch_plan(ti[:TOP_K], rk[:TOP_K], counts, MOE_TILE)
        dest_km = dest.reshape(-1)
        xs = _scatter_rows(h2, dest_km, tile_expert.shape[0] * MOE_TILE)
        ys = _experts(xs, tile_expert, n_used, tile_valid, w_gate_up[layer],
                      b_gate_up[layer].reshape(N_EXPERTS, 1, -1), w_down[layer],
                      b_down[layer].reshape(N_EXPERTS, 1, -1))
        y4 = _gather_rows(ys, dest_km).reshape(TOP_K, m, d // 2)
        x2d = _final(x1, y4, tw, g2, final_norm_w, n, tm)
    return x2d.reshape(bsz, n, d)
```

```python
import functools
import math

import numpy as np
import jax
import jax.numpy as jnp
from jax import lax
from jax.experimental import pallas as pl
from jax.experimental.pallas import tpu as pltpu
from jax.experimental.pallas import tpu_sc as plsc

F32 = jnp.float32
BF16 = jnp.bfloat16

EPS = 1e-6
GRID_W = 64
N_MOD = 6
F_GROUPS = 8
F_GROUP_DIM = 128
HEAD_DIM = 64
N_HEADS = 32
N_GROUPS = 4
HEADS_PER_GROUP = N_HEADS // N_GROUPS
D_STATE = 128
BC_WIDTH = N_GROUPS * D_STATE
CONV_K = 5
CHUNK = 128
N_EXPERTS = 32
TOP_K = 4
SWIGLU_LIMIT = 7.0
SWIGLU_ALPHA = 1.702

LANES = 128
VMEM_LIMIT_BYTES = 56 * 1024 * 1024
NEG_BIG = -1e30
MOE_TILE = 512
TAIL_ROWS = 512
SC_GATHER_WINDOW = 64


def _cparams(*sem):
    return pltpu.CompilerParams(dimension_semantics=sem, vmem_limit_bytes=VMEM_LIMIT_BYTES)


def _sigmoid(v):
    return 1.0 / (1.0 + jnp.exp(-v))


def _silu(v):
    return v * _sigmoid(v)


def _softplus(v):
    return jnp.maximum(v, 0.0) + jnp.log(1.0 + jnp.exp(-jnp.abs(v)))


def _pack_bf16_pair(v):
    w = v.shape[1] // 2
    bits = lax.bitcast_convert_type(v.astype(BF16).astype(F32), jnp.uint32)
    return bits[:, :w] | (bits[:, w:] >> 16)


def _unpack_bf16_pair(p):
    hi = lax.bitcast_convert_type(p & jnp.uint32(0xFFFF0000), F32)
    lo = lax.bitcast_convert_type(p << 16, F32)
    return hi, lo


def _mod_kernel(c_ref, w_ref, b_ref, o_ref):
    s = _silu(c_ref[...]).astype(BF16)
    o_ref[...] = jnp.dot(s, w_ref[...].astype(BF16), preferred_element_type=F32) + b_ref[...]


def _modulation(c_rows, w_mod, b_mod):
    rows, d = c_rows.shape
    n_out = w_mod.shape[1]
    tn = 1024
    return pl.pallas_call(
        _mod_kernel,
        grid=(n_out // tn,),
        in_specs=[pl.BlockSpec((rows, d), lambda j: (0, 0)),
                  pl.BlockSpec((d, tn), lambda j: (0, j)),
                  pl.BlockSpec((1, tn), lambda j: (0, j))],
        out_specs=pl.BlockSpec((rows, tn), lambda j: (0, j)),
        out_shape=jax.ShapeDtypeStruct((rows, n_out), F32),
        compiler_params=_cparams("arbitrary"),
        name="modulation",
    )(c_rows, w_mod, b_mod.reshape(1, n_out))


_INPROJ_TN = 1024


def _inproj_kernel(x_ref, nw_ref, sc_ref, sh_ref, w_ref, wdt_ref, p_ref, dt_ref):
    x = x_ref[...]
    inv = lax.rsqrt(jnp.mean(x * x, axis=-1, keepdims=True) + EPS)
    h = (x * inv * nw_ref[...]) * (1.0 + sc_ref[...]) + sh_ref[...]
    hb = h.astype(BF16)
    dt_ref[...] = jnp.dot(hb, wdt_ref[...], preferred_element_type=F32)
    for j in range(w_ref.shape[1] // _INPROJ_TN):
        cols = slice(j * _INPROJ_TN, (j + 1) * _INPROJ_TN)
        p_ref[:, cols] = jnp.dot(hb, w_ref[:, cols], preferred_element_type=F32).astype(p_ref.dtype)


def _in_proj(x2d, norm_w, sc, sh, w_main, w_dt, rows_per_mod, tm):
    m, d = x2d.shape
    n = w_main.shape[1]
    ndt = w_dt.shape[1]
    resident = dict(pipeline_mode=pl.Buffered(1))
    return pl.pallas_call(
        _inproj_kernel,
        grid=(m // tm,),
        in_specs=[pl.BlockSpec((tm, d), lambda i: (i, 0)),
                  pl.BlockSpec((1, d), lambda i: (0, 0), **resident),
                  pl.BlockSpec((None, 1, d), lambda i: ((i * tm) // rows_per_mod, 0, 0)),
                  pl.BlockSpec((None, 1, d), lambda i: ((i * tm) // rows_per_mod, 0, 0)),
                  pl.BlockSpec((d, n), lambda i: (0, 0), **resident),
                  pl.BlockSpec((d, ndt), lambda i: (0, 0), **resident)],
        out_specs=[pl.BlockSpec((tm, n), lambda i: (i, 0)),
                   pl.BlockSpec((tm, ndt), lambda i: (i, 0))],
        out_shape=[jax.ShapeDtypeStruct((m, n), BF16),
                   jax.ShapeDtypeStruct((m, ndt), F32)],
        compiler_params=_cparams("arbitrary"),
        name="in_proj",
    )(x2d, norm_w.reshape(1, d), sc, sh, w_main, w_dt)


_CONV_HALO = 16


_CONV_ROWS = 64


def _conv_shift_matrix(rc):
    win = rc + 2 * _CONV_HALO
    s = np.zeros((rc, CONV_K * win), np.float32)
    for k in range(CONV_K):
        for l in range(rc):
            s[l, k * win + _CONV_HALO + l + k - CONV_K // 2] = 1.0
    return jnp.asarray(s, BF16)


def _conv_kernel(p_ref, w_ref, b_ref, s_ref, o_ref, *, n, rc):
    tc = o_ref.shape[-1]
    w = w_ref[...].astype(BF16)
    bias = b_ref[...]
    smat = s_ref[...]
    zeros = jnp.zeros((_CONV_HALO, tc), BF16)
    for r0 in range(0, n, rc):
        top = zeros if r0 == 0 else p_ref[r0 - _CONV_HALO:r0, :]
        bot = zeros if r0 + rc >= n else p_ref[r0 + rc:r0 + rc + _CONV_HALO, :]
        window = jnp.concatenate([top, p_ref[r0:r0 + rc, :], bot], axis=0)
        taps = jnp.concatenate([window * w[k:k + 1, :] for k in range(CONV_K)], axis=0)
        acc = jnp.dot(smat, taps, preferred_element_type=F32) + bias
        o_ref[r0:r0 + rc, :] = _silu(acc).astype(o_ref.dtype)


def _conv_silu(p3d, conv_w, conv_b, n_ch):
    bsz, n, _ = p3d.shape
    tc = 512
    rc = min(_CONV_ROWS, n)
    smat = _conv_shift_matrix(rc)
    return pl.pallas_call(
        functools.partial(_conv_kernel, n=n, rc=rc),
        grid=(bsz, n_ch // tc),
        in_specs=[pl.BlockSpec((None, n, tc), lambda b, j: (b, 0, j)),
                  pl.BlockSpec((CONV_K, tc), lambda b, j: (0, j)),
                  pl.BlockSpec((1, tc), lambda b, j: (0, j)),
                  pl.BlockSpec(smat.shape, lambda b, j: (0, 0))],
        out_specs=pl.BlockSpec((None, n, tc), lambda b, j: (b, 0, j)),
        out_shape=jax.ShapeDtypeStruct((bsz, n, n_ch), BF16),
        compiler_params=_cparams("arbitrary", "arbitrary"),
        name="conv_silu",
    )(p3d, conv_w, conv_b.reshape(1, -1), smat)


LOG2E = 1.4426950408889634
SSD_SUB = 8


def _head_expand_matrix():
    e = np.zeros((LANES, N_HEADS * HEAD_DIM), np.float32)
    for h in range(N_HEADS):
        e[h, h * HEAD_DIM:(h + 1) * HEAD_DIM] = 1.0
    return jnp.asarray(e, BF16)


def _ssd_chunk_terms(dt_raw, dt_bias, a_log, expand, tri, fwd):
    q = CHUNK
    dt = _softplus(dt_raw + dt_bias)
    da = dt * (-jnp.exp(a_log))
    hi = da.astype(BF16)
    r1 = da - hi.astype(F32)
    mid = r1.astype(BF16)
    lo = (r1 - mid.astype(F32)).astype(BF16)
    ones_tri = jnp.where(tri, 1.0, 0.0).astype(BF16)
    cs3 = jnp.dot(ones_tri, jnp.concatenate([hi, mid, lo], axis=1), preferred_element_type=F32)
    cs = cs3[:, :LANES] + cs3[:, LANES:2 * LANES] + cs3[:, 2 * LANES:]
    tot = jnp.where(fwd, cs[q - 1:q, :], cs[0:1, :])
    dte = dt * jnp.exp(tot - cs)
    dec = jnp.exp(tot)
    dec_hi = dec.astype(BF16)
    dec_lo = (dec - dec_hi.astype(F32)).astype(BF16)
    dec2 = jnp.concatenate([dec_hi, dec_lo, jnp.zeros((6, LANES), BF16)], axis=0)
    stacked = jnp.concatenate([jnp.exp(cs).astype(BF16), dte.astype(BF16), dec2], axis=0)
    spread = jnp.dot(stacked, expand, preferred_element_type=F32)
    ecs_e = spread[:q]
    dte_e = spread[q:2 * q].astype(BF16)
    dec_e = jnp.broadcast_to(spread[2 * q:2 * q + 1] + spread[2 * q + 1:2 * q + 2], (8, spread.shape[1]))
    cs2 = cs * LOG2E
    return cs2, cs2.T, dt.T, ecs_e, dte_e, dec_e


def _ssd_kernel(*refs, with_y, nc, n_sub):
    if with_y:
        (x_ref, b_ref, c_ref, dt_ref, dtn_ref, dtb_ref, alog_ref, exp_ref, h0_ref, y_ref,
         s_scr, cs_scr, dee_scr, dce_scr, ecs_scr) = refs
    else:
        (x_ref, b_ref, dt_ref, dtn_ref, dtb_ref, alog_ref, exp_ref, h0_ref, hout_ref,
         s_scr, cs_scr, dee_scr, dce_scr) = refs
    q = CHUNK
    gw = HEADS_PER_GROUP * HEAD_DIM
    d = pl.program_id(1)
    c = pl.program_id(2)
    row = lax.broadcasted_iota(jnp.int32, (q, q), 0)
    col = lax.broadcasted_iota(jnp.int32, (q, q), 1)
    fwd = d == 0
    tri = jnp.where(fwd, row - col, col - row) >= 0
    first_half = col < HEAD_DIM

    def store_terms(slot, terms):
        cs2, cs2_t, dt_t, ecs_e, dte_e, dec_e = terms
        cs_scr[slot, 0] = cs2
        cs_scr[slot, 1] = cs2_t
        cs_scr[slot, 2] = dt_t
        dee_scr[slot] = dte_e
        dce_scr[slot] = dec_e
        if with_y:
            ecs_scr[slot] = ecs_e

    def terms_of(ref, sub):
        rows = pl.ds(pl.multiple_of(sub * q, q), q)
        return _ssd_chunk_terms(ref[rows, :], dtb_ref[...], alog_ref[...], exp_ref[...], tri, fwd)

    def sub_of(j):
        return jnp.where(fwd, j, n_sub - 1 - j)

    @pl.when(c == 0)
    def _():
        s_scr[...] = h0_ref[...]
        store_terms(0, terms_of(dt_ref, sub_of(0)))

    for j in range(n_sub):
        slot = j % 2
        rows = pl.ds(pl.multiple_of(sub_of(j) * q, q), q)
        cs2 = cs_scr[slot, 0]
        cs2_t = cs_scr[slot, 1]
        dt_t = cs_scr[slot, 2]
        if j + 1 < n_sub:
            store_terms(1 - slot, terms_of(dt_ref, sub_of(j + 1)))
        else:
            store_terms(1 - slot, terms_of(dtn_ref, sub_of(0)))

        for g in range(N_GROUPS):
            cols = slice(g * gw, (g + 1) * gw)
            bg = b_ref[rows, g * D_STATE:(g + 1) * D_STATE]
            xg = x_ref[rows, cols]
            s_old = s_scr[g]
            if with_y:
                cg = c_ref[rows, g * D_STATE:(g + 1) * D_STATE]
                cb = lax.dot_general(cg, bg, (((1,), (1,)), ((), ())), preferred_element_type=F32)
                y_off = jnp.dot(cg, s_old.astype(BF16), preferred_element_type=F32)
                for k in range(HEADS_PER_GROUP // 2):
                    pair = g * (HEADS_PER_GROUP // 2) + k
                    lanes = slice(pair * LANES, (pair + 1) * LANES)
                    xp = xg[:, k * LANES:(k + 1) * LANES]
                    xz = jnp.zeros_like(xp)
                    lhs = []
                    for hh in (2 * pair, 2 * pair + 1):
                        colb = jnp.broadcast_to(cs2[:, hh:hh + 1], (q, q))
                        rowb = cs2_t[hh:hh + 1, :]
                        decay = jnp.exp2(jnp.where(tri, colb - rowb, -jnp.inf))
                        lhs.append((cb * decay * dt_t[hh:hh + 1, :]).astype(BF16))
                    y_diag = jnp.dot(jnp.concatenate(lhs, axis=1),
                                     jnp.concatenate([jnp.where(first_half, xp, xz), jnp.where(first_half, xz, xp)],
                                                     axis=0),
                                     preferred_element_type=F32)
                    y = y_diag + ecs_scr[slot, :, lanes] * y_off[:, k * LANES:(k + 1) * LANES]
                    y_ref[rows, lanes] = y.astype(y_ref.dtype)
            xdte = xg * dee_scr[slot, :, cols]
            upd = lax.dot_general(bg, xdte, (((0,), (0,)), ((), ())), preferred_element_type=F32)
            s_scr[g] = s_old * dce_scr[slot, 0:1, cols] + upd

    if not with_y:
        @pl.when(c == nc - 1)
        def _():
            hout_ref[...] = s_scr[...]


def _ssd(xbc, dt_raw, dt_bias2, a_log2, h0, with_y):
    bsz, n, _ = xbc.shape
    n_sub = min(SSD_SUB, n // CHUNK)
    assert n_sub % 2 == 0 and n % (n_sub * CHUNK) == 0
    blk = n_sub * CHUNK
    nc = n // blk
    d_inner = N_HEADS * HEAD_DIM
    gw = HEADS_PER_GROUP * HEAD_DIM
    x_blk = d_inner // BC_WIDTH

    def ceff(dd, cc):
        return cc + dd * (nc - 1 - 2 * cc)

    def cnext(dd, cc):
        return ceff(dd, jnp.minimum(cc + 1, nc - 1))

    in_specs = [pl.BlockSpec((None, blk, d_inner), lambda b, dd, cc: (b, ceff(dd, cc), 0)),
                pl.BlockSpec((None, blk, BC_WIDTH), lambda b, dd, cc: (b, ceff(dd, cc), x_blk + dd))]
    args = [xbc, xbc]
    if with_y:
        in_specs.append(pl.BlockSpec((None, blk, BC_WIDTH), lambda b, dd, cc: (b, ceff(dd, cc), x_blk + 2 + dd)))
        args.append(xbc)
    in_specs += [pl.BlockSpec((None, blk, LANES), lambda b, dd, cc: (b, ceff(dd, cc), dd)),
                 pl.BlockSpec((None, blk, LANES), lambda b, dd, cc: (b, cnext(dd, cc), dd)),
                 pl.BlockSpec((None, 1, LANES), lambda b, dd, cc: (dd, 0, 0)),
                 pl.BlockSpec((None, 1, LANES), lambda b, dd, cc: (dd, 0, 0)),
                 pl.BlockSpec((LANES, d_inner), lambda b, dd, cc: (0, 0)),
                 pl.BlockSpec((None, None, N_GROUPS, D_STATE, gw), lambda b, dd, cc: (b, dd, 0, 0, 0))]
    args += [dt_raw, dt_raw, dt_bias2, a_log2, _head_expand_matrix(), h0]
    scratch = [pltpu.VMEM((N_GROUPS, D_STATE, gw), F32),
               pltpu.VMEM((2, 3, CHUNK, LANES), F32),
               pltpu.VMEM((2, CHUNK, d_inner), BF16),
               pltpu.VMEM((2, 8, d_inner), F32)]
    if with_y:
        out_specs = pl.BlockSpec((None, None, blk, d_inner), lambda b, dd, cc: (dd, b, ceff(dd, cc), 0))
        out_shape = jax.ShapeDtypeStruct((2, bsz, n, d_inner), BF16)
        scratch += [pltpu.VMEM((2, CHUNK, d_inner), F32)]
    else:
        out_specs = pl.BlockSpec((None, None, N_GROUPS, D_STATE, gw), lambda b, dd, cc: (b, dd, 0, 0, 0))
        out_shape = jax.ShapeDtypeStruct((bsz, 2, N_GROUPS, D_STATE, gw), F32)
    return pl.pallas_call(
        functools.partial(_ssd_kernel, with_y=with_y, nc=nc, n_sub=n_sub),
        grid=(bsz, 2, nc),
        in_specs=in_specs,
        out_specs=out_specs,
        out_shape=out_shape,
        scratch_shapes=scratch,
        compiler_params=_cparams("arbitrary", "arbitrary", "arbitrary"),
        name="ssd_scan" if with_y else "ssd_ctx_state",
    )(*args)


def _dft_mats(n):
    k = np.arange(n)
    ang = 2.0 * np.pi * ((k[:, None] * k[None, :]) % n) / n
    return np.cos(ang), np.sin(ang)


def _strided_rows(scr, t, start, count, stride, width):
    return jnp.concatenate([scr[t, lb, pl.ds(start, count, stride=stride), :] for lb in range(width // LANES)],
                           axis=1)


def _fnet_stage1_kernel(u_ref, chan_ref, m1_ref, twc_ref, tws_ref, o_ref, zs_scr, *, tcs):
    r = u_ref.shape[0]
    width = u_ref.shape[-1]
    u2 = u_ref[...].reshape(r * tcs, width)
    chan = chan_ref[...]
    for g in range(F_GROUPS):
        pq = jnp.dot(u2[:, g * F_GROUP_DIM:(g + 1) * F_GROUP_DIM], chan, preferred_element_type=F32)
        zs_scr[0, g] = pq[:, :F_GROUP_DIM]
        zs_scr[1, g] = pq[:, F_GROUP_DIM:]
    m1 = m1_ref[...]
    for j in range(tcs):
        zz = jnp.concatenate([_strided_rows(zs_scr, 0, j, r, tcs, width),
                              _strided_rows(zs_scr, 1, j, r, tcs, width)], axis=0).astype(BF16)
        u = jnp.dot(m1, zz, preferred_element_type=F32)
        ur = u[:r]
        ui = u[r:]
        tc = twc_ref[j]
        ts = tws_ref[j]
        o_ref[0, j] = (ur * tc + ui * ts).astype(o_ref.dtype)
        o_ref[1, j] = (ui * tc - ur * ts).astype(o_ref.dtype)


def _fnet_stage2_kernel(u_ref, k2_ref, o_ref):
    c, tbs, width = o_ref.shape
    uu = jnp.concatenate([u_ref[0].reshape(c * tbs, width), u_ref[1].reshape(c * tbs, width)], axis=0)
    out = jnp.dot(k2_ref[...], uu, preferred_element_type=F32)
    o_ref[...] = out.reshape(c, tbs, width).astype(o_ref.dtype)


def _fnet_mix(p2d, col_blk, bsz, n):
    m = p2d.shape[0]
    width = F_GROUPS * F_GROUP_DIM
    cgrid = GRID_W
    rgrid = n // cgrid
    scale = 1.0 / math.sqrt(n * F_GROUP_DIM)

    cc, sc = _dft_mats(F_GROUP_DIM)
    chan = jnp.asarray(np.concatenate([cc, sc], axis=1) * scale, BF16)

    cr, sr = _dft_mats(rgrid)
    m1 = jnp.asarray(np.block([[cr, -sr], [-sr, -cr]]), BF16)
    bb = np.arange(rgrid)[None, :]
    ci = np.arange(cgrid)[:, None]
    ang = 2.0 * np.pi * ((ci * bb) % n) / n
    twc = jnp.asarray(np.cos(ang)[:, :, None], F32)
    tws = jnp.asarray(np.sin(ang)[:, :, None], F32)
    tcs = 16
    p4 = p2d.reshape(bsz, rgrid, cgrid, p2d.shape[-1])
    u5 = pl.pallas_call(
        functools.partial(_fnet_stage1_kernel, tcs=tcs),
        grid=(bsz, cgrid // tcs),
        in_specs=[pl.BlockSpec((None, rgrid, tcs, width), lambda b, j: (b, 0, j, col_blk)),
                  pl.BlockSpec((F_GROUP_DIM, 2 * F_GROUP_DIM), lambda b, j: (0, 0)),
                  pl.BlockSpec((2 * rgrid, 2 * rgrid), lambda b, j: (0, 0)),
                  pl.BlockSpec((tcs, rgrid, 1), lambda b, j: (j, 0, 0)),
                  pl.BlockSpec((tcs, rgrid, 1), lambda b, j: (j, 0, 0))],
        out_specs=pl.BlockSpec((2, None, tcs, rgrid, width), lambda b, j: (0, b, j, 0, 0)),
        out_shape=jax.ShapeDtypeStruct((2, bsz, cgrid, rgrid, width), BF16),
        scratch_shapes=[pltpu.VMEM((2, width // LANES, rgrid * tcs, LANES), F32)],
        compiler_params=_cparams("arbitrary", "arbitrary"),
        name="fnet_stage1",
    )(p4, chan, m1, twc, tws)

    cc2, sc2 = _dft_mats(cgrid)
    tbs = min(16, rgrid)
    eye = np.eye(tbs)
    k2 = jnp.asarray(np.concatenate([np.kron(cc2, eye), np.kron(sc2, eye)], axis=1), BF16)
    out = pl.pallas_call(
        _fnet_stage2_kernel,
        grid=(bsz, rgrid // tbs),
        in_specs=[pl.BlockSpec((2, None, cgrid, tbs, width), lambda b, j: (0, b, 0, j, 0)),
                  pl.BlockSpec(k2.shape, lambda b, j: (0, 0))],
        out_specs=pl.BlockSpec((None, cgrid, tbs, width), lambda b, j: (b, 0, j, 0)),
        out_shape=jax.ShapeDtypeStruct((bsz, cgrid, rgrid, width), BF16),
        compiler_params=_cparams("arbitrary", "arbitrary"),
        name="fnet_stage2",
    )(u5, k2)
    return out.reshape(m, width)


def _tail_kernel(yf_ref, yb_ref, xs_ref, z_ref, ff_ref, gf_ref, gs_ref, x_ref, g1_ref, sc2_ref, sh2_ref,
                 dsk_ref, snw_ref, wso_ref, wfo_ref, wo_ref, n2w_ref, wr_ref, br_ref,
                 x1_ref, h2_ref, ti_ref, tw_ref, rk_ref, cnt_ref, carry_scr):
    y = (yf_ref[0] + yb_ref[0]).astype(F32) + dsk_ref[...] * xs_ref[...].astype(F32)
    g = y * _silu(z_ref[...]).astype(F32)
    inv = lax.rsqrt(jnp.mean(g * g, axis=-1, keepdims=True) + EPS)
    gn = (g * inv * snw_ref[...]).astype(BF16)
    y_ssd = jnp.dot(gn, wso_ref[...], preferred_element_type=F32)
    y_four = jnp.dot(ff_ref[...], wfo_ref[...], preferred_element_type=F32)
    t = _sigmoid(gf_ref[...]) * y_four.astype(BF16) + _sigmoid(gs_ref[...]) * y_ssd.astype(BF16)
    mix = jnp.dot(t, wo_ref[...], preferred_element_type=F32)
    x1 = x_ref[...] + g1_ref[...] * mix
    x1_ref[...] = x1
    inv2 = lax.rsqrt(jnp.mean(x1 * x1, axis=-1, keepdims=True) + EPS)
    h2 = (x1 * inv2 * n2w_ref[...]) * (1.0 + sc2_ref[...]) + sh2_ref[...]
    h2_ref[...] = _pack_bf16_pair(h2)
    h_hi = h2.astype(BF16)
    h_lo = (h2 - h_hi.astype(F32)).astype(BF16)
    wr = wr_ref[...]
    w_hi = wr.astype(BF16)
    w_lo = (wr - w_hi.astype(F32)).astype(BF16)
    logits = (jnp.dot(h_hi, w_hi, preferred_element_type=F32)
              + jnp.dot(h_hi, w_lo, preferred_element_type=F32)
              + jnp.dot(h_lo, w_hi, preferred_element_type=F32)) + br_ref[...]
    tm = logits.shape[0]
    lane = lax.broadcasted_iota(jnp.int32, (tm, LANES), 1)
    vals, idxs = [], []
    cur = logits
    for _ in range(TOP_K):
        mx = jnp.max(cur, axis=-1, keepdims=True)
        ix = jnp.min(jnp.where(cur == mx, lane, LANES), axis=-1, keepdims=True)
        vals.append(mx)
        idxs.append(ix)
        cur = jnp.where(lane == ix, NEG_BIG * 2.0, cur)
    es = [jnp.exp(v - vals[0]) for v in vals]
    den = es[0] + es[1] + es[2] + es[3]
    ti = jnp.zeros((tm, LANES), jnp.int32)
    tw = jnp.zeros((tm, LANES), F32)
    for k in range(TOP_K):
        ti = jnp.where(lane == k, idxs[k], ti)
        tw = jnp.where(lane == k, es[k] / den, tw)
    ti_ref[...] = ti.T[:8, :]
    tw_ref[...] = tw

    @pl.when(pl.program_id(0) == 0)
    def _():
        carry_scr[...] = jnp.zeros(carry_scr.shape, F32)

    onehots = [jnp.where(lane == ix, 1.0, 0.0) for ix in idxs]
    cnt = onehots[0] + onehots[1] + onehots[2] + onehots[3]
    r_i = lax.broadcasted_iota(jnp.int32, (tm, tm), 0)
    c_i = lax.broadcasted_iota(jnp.int32, (tm, tm), 1)
    earlier = jnp.where(r_i > c_i, 1.0, 0.0).astype(BF16)
    before = jnp.dot(earlier, cnt.astype(BF16), preferred_element_type=F32) + carry_scr[0:1, :]
    rk = jnp.zeros((tm, LANES), jnp.int32)
    for k in range(TOP_K):
        rank_k = jnp.sum(onehots[k] * before, axis=-1, keepdims=True)
        rk = jnp.where(lane == k, rank_k.astype(jnp.int32), rk)
    rk_ref[...] = rk.T[:8, :]
    total = carry_scr[...] + jnp.sum(cnt, axis=0, keepdims=True)
    carry_scr[...] = total
    cnt_ref[...] = total


def _tail(y2, xbc2d, p2d, ff, x2d, g1, sc2, sh2, dsk, snw, wso, wfo, wo, n2w, wr, br, rows_per_mod, tm):
    m, d = x2d.shape
    di = y2.shape[-1]
    zb = (2 * di) // di
    fb = (3 * di) // d
    row = lambda i: (i, 0)
    modrow = lambda i: ((i * tm) // rows_per_mod, 0, 0)
    const = lambda i: (0, 0)
    in_specs = [pl.BlockSpec((1, tm, di), lambda i: (0, i, 0)),
                pl.BlockSpec((1, tm, di), lambda i: (1, i, 0)),
                pl.BlockSpec((tm, di), row),
                pl.BlockSpec((tm, di), lambda i: (i, zb)),
                pl.BlockSpec((tm, d), row),
                pl.BlockSpec((tm, d), lambda i: (i, fb + 1)),
                pl.BlockSpec((tm, d), lambda i: (i, fb + 2)),
                pl.BlockSpec((tm, d), row),
                pl.BlockSpec((None, 1, d), modrow),
                pl.BlockSpec((None, 1, d), modrow),
                pl.BlockSpec((None, 1, d), modrow),
                pl.BlockSpec((1, di), const),
                pl.BlockSpec((1, di), const),
                pl.BlockSpec((di, d), const),
                pl.BlockSpec((d, d), const),
                pl.BlockSpec((d, d), const),
                pl.BlockSpec((1, d), const),
                pl.BlockSpec((d, LANES), const),
                pl.BlockSpec((1, LANES), const)]
    tcol = lambda i: (0, i)
    out_specs = [pl.BlockSpec((tm, d), row), pl.BlockSpec((tm, d // 2), row),
                 pl.BlockSpec((8, tm), tcol), pl.BlockSpec((tm, LANES), row), pl.BlockSpec((8, tm), tcol),
                 pl.BlockSpec((8, LANES), const)]
    out_shape = [jax.ShapeDtypeStruct((m, d), F32), jax.ShapeDtypeStruct((m, d // 2), jnp.uint32),
                 jax.ShapeDtypeStruct((8, m), jnp.int32), jax.ShapeDtypeStruct((m, LANES), F32),
                 jax.ShapeDtypeStruct((8, m), jnp.int32), jax.ShapeDtypeStruct((8, LANES), F32)]
    return pl.pallas_call(
        _tail_kernel,
        grid=(m // tm,),
        in_specs=in_specs,
        out_specs=out_specs,
        out_shape=out_shape,
        scratch_shapes=[pltpu.VMEM((8, LANES), F32)],
        compiler_params=_cparams("arbitrary"),
        name="tail",
    )(y2, y2, xbc2d, p2d, ff, p2d, p2d, x2d, g1, sc2, sh2, dsk, snw, wso, wfo, wo, n2w, wr, br)


def _gather_rows(src, idx):
    n_out = idx.shape[0]
    width = src.shape[1]
    win = SC_GATHER_WINDOW
    info = plsc.get_sparse_core_info()
    n_cores, n_workers = info.num_cores, info.num_cores * info.num_subcores
    per_worker = n_out // n_workers
    assert per_worker * n_workers == n_out and per_worker % (2 * win) == 0
    mesh = plsc.VectorSubcoreMesh(core_axis_name="core", subcore_axis_name="subcore")

    @functools.partial(
        pl.kernel, out_type=jax.ShapeDtypeStruct((n_out, width), src.dtype), mesh=mesh,
        scratch_types=[pltpu.VMEM((win,), jnp.int32), pltpu.VMEM((win,), jnp.int32),
                       pltpu.VMEM((win, width), src.dtype), pltpu.VMEM((win, width), src.dtype),
                       pltpu.SemaphoreType.DMA, pltpu.SemaphoreType.DMA,
                       pltpu.SemaphoreType.DMA, pltpu.SemaphoreType.DMA],
        name="gather_rows")
    def gather(src_hbm, idx_hbm, out_hbm, idx_a, idx_b, rows_a, rows_b, gsem_a, gsem_b, ssem_a, ssem_b):
        worker = lax.axis_index("subcore") * n_cores + lax.axis_index("core")
        base = worker * per_worker

        @pl.loop(0, per_worker, step=2 * win)
        def _(off):
            pltpu.sync_copy(idx_hbm.at[pl.ds(base + off, win)], idx_a)
            ga = pltpu.async_copy(src_hbm.at[idx_a], rows_a, gsem_a)
            pltpu.sync_copy(idx_hbm.at[pl.ds(base + off + win, win)], idx_b)
            gb = pltpu.async_copy(src_hbm.at[idx_b], rows_b, gsem_b)
            ga.wait()
            sa = pltpu.async_copy(rows_a, out_hbm.at[pl.ds(base + off, win)], ssem_a)
            gb.wait()
            sb = pltpu.async_copy(rows_b, out_hbm.at[pl.ds(base + off + win, win)], ssem_b)
            sa.wait()
            sb.wait()

    return gather(src, idx)


def _scatter_rows(src, dest, n_rows):
    m, width = src.shape
    win = SC_GATHER_WINDOW
    info = plsc.get_sparse_core_info()
    n_cores, n_workers = info.num_cores, info.num_cores * info.num_subcores
    per_worker = m // n_workers
    assert per_worker * n_workers == m and per_worker % win == 0
    mesh = plsc.VectorSubcoreMesh(core_axis_name="core", subcore_axis_name="subcore")

    @functools.partial(
        pl.kernel, out_type=jax.ShapeDtypeStruct((n_rows, width), src.dtype), mesh=mesh,
        scratch_types=[pltpu.VMEM((win, width), src.dtype)]
        + [pltpu.VMEM((win,), jnp.int32)] * TOP_K + [pltpu.SemaphoreType.DMA] * (TOP_K + 1),
        name="scatter_rows")
    def scatter(src_hbm, dest_hbm, out_hbm, rows_v, *rest):
        idx_v, sems, row_sem = rest[:TOP_K], rest[TOP_K:2 * TOP_K], rest[2 * TOP_K]
        worker = lax.axis_index("subcore") * n_cores + lax.axis_index("core")
        base = worker * per_worker

        @pl.loop(0, per_worker, step=win)
        def _(off):
            t0 = base + off
            loads = [pltpu.async_copy(src_hbm.at[pl.ds(t0, win)], rows_v, row_sem)]
            loads += [pltpu.async_copy(dest_hbm.at[pl.ds(k * m + t0, win)], idx_v[k], sems[k]) for k in range(TOP_K)]
            for ld in loads:
                ld.wait()
            copies = [pltpu.async_copy(rows_v, out_hbm.at[idx_v[k]], sems[k]) for k in range(TOP_K)]
            for cp in copies:
                cp.wait()

    return scatter(src, dest)


def _expert_kernel(te_ref, nu_ref, tv_ref, x_ref, wgu_ref, bgu_ref, wd_ref, bd_ref, o_ref, wgu_scr, wd_scr):
    i = pl.program_id(0)
    used = i < nu_ref[0]
    new_expert = (i == 0) | (te_ref[i] != te_ref[jnp.maximum(i - 1, 0)])

    @pl.when(used & new_expert)
    def _():
        wgu_scr[...] = wgu_ref[...].astype(BF16)
        wd_scr[...] = wd_ref[...].astype(BF16)

    @pl.when(used)
    def _():
        dff = wd_ref.shape[0]
        half = wgu_ref.shape[0] // 2
        rows = lax.broadcasted_iota(jnp.int32, x_ref.shape, 0)
        xa, xb = _unpack_bf16_pair(jnp.where(rows < tv_ref[i], x_ref[...], jnp.uint32(0)))
        gu = (jnp.dot(xa.astype(BF16), wgu_scr[:half, :], preferred_element_type=F32)
              + jnp.dot(xb.astype(BF16), wgu_scr[half:, :], preferred_element_type=F32)) + bgu_ref[...]
        gate = jnp.minimum(gu[:, :dff], SWIGLU_LIMIT)
        up = jnp.clip(gu[:, dff:], -SWIGLU_LIMIT, SWIGLU_LIMIT)
        act = (up + 1.0) * gate * _sigmoid(SWIGLU_ALPHA * gate)
        y = jnp.dot(act.astype(BF16), wd_scr[...], preferred_element_type=F32) + bd_ref[...]
        o_ref[...] = _pack_bf16_pair(y)

    @pl.when(jnp.logical_not(used))
    def _():
        o_ref[...] = jnp.zeros(o_ref.shape, o_ref.dtype)


def _experts(xs, tile_expert, n_used, tile_valid, wgu, bgu, wd, bd):
    rows, dh = xs.shape
    d = 2 * dh
    tm = MOE_TILE
    n_tiles = rows // tm
    dff2 = wgu.shape[-1]

    def tile(i, te, nu, tv):
        return (jnp.minimum(i, nu[0] - 1), 0)

    def wsel(i, te, nu, tv):
        return (te[jnp.minimum(i, nu[0] - 1)], 0, 0)

    grid_spec = pltpu.PrefetchScalarGridSpec(
        num_scalar_prefetch=3,
        grid=(n_tiles,),
        in_specs=[pl.BlockSpec((tm, dh), tile),
                  pl.BlockSpec((None, d, dff2), wsel),
                  pl.BlockSpec((None, 1, dff2), wsel),
                  pl.BlockSpec((None, dff2 // 2, d), wsel),
                  pl.BlockSpec((None, 1, d), wsel)],
        out_specs=pl.BlockSpec((tm, dh), lambda i, te, nu, tv: (i, 0)),
        scratch_shapes=[pltpu.VMEM((d, dff2), BF16), pltpu.VMEM((dff2 // 2, d), BF16)],
    )
    return pl.pallas_call(
        _expert_kernel,
        grid_spec=grid_spec,
        out_shape=jax.ShapeDtypeStruct((rows, dh), jnp.uint32),
        compiler_params=_cparams("arbitrary"),
        name="experts",
    )(tile_expert, n_used, tile_valid, xs, wgu, bgu, wd, bd)


def _final_kernel(x1_ref, ya_ref, yb_ref, yc_ref, yd_ref, tw_ref, g2_ref, fw_ref, o_ref):
    d = x1_ref.shape[-1]
    half = d // 2
    tw = tw_ref[...]
    acc_hi = jnp.zeros((x1_ref.shape[0], half), F32)
    acc_lo = jnp.zeros((x1_ref.shape[0], half), F32)
    for k, y_ref in enumerate((ya_ref, yb_ref, yc_ref, yd_ref)):
        y_hi, y_lo = _unpack_bf16_pair(y_ref[...])
        acc_hi = acc_hi + tw[:, k:k + 1] * y_hi
        acc_lo = acc_lo + tw[:, k:k + 1] * y_lo
    x_hi = x1_ref[:, :half] + g2_ref[:, :half] * acc_hi
    x_lo = x1_ref[:, half:] + g2_ref[:, half:] * acc_lo
    ms = (jnp.sum(x_hi * x_hi, axis=-1, keepdims=True) + jnp.sum(x_lo * x_lo, axis=-1, keepdims=True)) / d
    inv = lax.rsqrt(ms + EPS)
    o_ref[:, :half] = x_hi * inv * fw_ref[:, :half]
    o_ref[:, half:] = x_lo * inv * fw_ref[:, half:]


def _final(x1, y4, tw, g2, fw, rows_per_mod, tm):
    m, d = x1.shape
    return pl.pallas_call(
        _final_kernel,
        grid=(m // tm,),
        in_specs=[pl.BlockSpec((tm, d), lambda i: (i, 0))]
        + [pl.BlockSpec((None, tm, d // 2), functools.partial(lambda k, i: (k, i, 0), k)) for k in range(TOP_K)]
        + [pl.BlockSpec((tm, LANES), lambda i: (i, 0)),
           pl.BlockSpec((None, 1, d), lambda i: ((i * tm) // rows_per_mod, 0, 0)),
           pl.BlockSpec((1, d), lambda i: (0, 0))],
        out_specs=pl.BlockSpec((tm, d), lambda i: (i, 0)),
        out_shape=jax.ShapeDtypeStruct((m, d), F32),
        compiler_params=_cparams("arbitrary"),
        name="final",
    )(x1, y4, y4, y4, y4, tw, g2, fw.reshape(1, d))


def _dispatch_plan(e, rank, counts, tm):
    n_assign = e.size
    padded = (counts + tm - 1) // tm * tm
    pad_end = jnp.cumsum(padded)
    pad_start = pad_end - padded
    dest = rank.astype(jnp.int32)
    for j in range(N_EXPERTS - 1):
        dest = dest + jnp.where(e > j, padded[j], 0).astype(jnp.int32)
    n_tiles = n_assign // tm + N_EXPERTS
    tile_start = jnp.arange(n_tiles, dtype=jnp.int32) * tm
    tile_expert = jnp.minimum(jnp.sum(pad_end[None, :] <= tile_start[:, None], axis=1), N_EXPERTS - 1).astype(jnp.int32)
    n_used = (pad_end[-1] // tm).astype(jnp.int32).reshape(1)
    tile_valid = jnp.clip(pad_start[tile_expert] + counts[tile_expert] - tile_start, 0, tm).astype(jnp.int32)
    return dest, tile_expert, n_used, tile_valid


def kernel(x, c, ctx, c_ctx, w_mod, b_mod, norm1_w, norm2_w, w_in, conv_w, conv_b, dt_bias, a_log, d_skip,
           ssd_norm_w, w_ssd_out, w_four_out, w_o, w_router, b_router, w_gate_up, b_gate_up, w_down, b_down,
           final_norm_w):
    bsz, n, d = x.shape
    n_ctx = ctx.shape[1]
    depth = w_mod.shape[0]
    assert depth == 1, "a stacked model would also need the context stream's residual update"
    d_inner = N_HEADS * HEAD_DIM
    off_dt = d_inner + 2 * BC_WIDTH
    m = bsz * n
    x2d = x.reshape(m, d)
    tm = min(512, n)

    for layer in range(depth):
        rows = -(-(bsz + 1) // 8) * 8
        c_rows = jnp.zeros((rows, d), F32).at[:bsz].set(c).at[bsz].set(c_ctx)
        mod = _modulation(c_rows, w_mod[layer], b_mod[layer])
        mods = [mod[:bsz, i * d:(i + 1) * d].reshape(bsz, 1, d) for i in range(N_MOD)]
        sh1, sc1, g1, sh2, sc2, g2 = mods
        sh1_c = mod[bsz:bsz + 1, :d].reshape(1, 1, d)
        sc1_c = mod[bsz:bsz + 1, d:2 * d].reshape(1, 1, d)

        wl = w_in[layer]
        w_main = jnp.concatenate([wl[:, :off_dt], wl[:, off_dt + 2 * N_HEADS:]], axis=1).astype(BF16)
        w_dt = jnp.zeros((d, 2 * LANES), F32)
        w_dt = w_dt.at[:, :N_HEADS].set(wl[:, off_dt:off_dt + N_HEADS])
        w_dt = w_dt.at[:, LANES:LANES + N_HEADS].set(wl[:, off_dt + N_HEADS:off_dt + 2 * N_HEADS]).astype(BF16)
        pad = jnp.zeros((2, 1, LANES - N_HEADS), F32)
        dt_bias2 = jnp.concatenate([dt_bias[layer].reshape(2, 1, N_HEADS), pad], axis=-1)
        a_log2 = jnp.concatenate([a_log[layer].reshape(2, 1, N_HEADS), pad], axis=-1)

        w_ctx = w_main[:, :off_dt]
        pc, dtc = _in_proj(ctx.reshape(bsz * n_ctx, d), norm1_w[layer], sc1_c, sh1_c, w_ctx, w_dt,
                           bsz * n_ctx, min(512, n_ctx))
        xbc_c = _conv_silu(pc.reshape(bsz, n_ctx, off_dt), conv_w[layer][:, :off_dt], conv_b[layer][:off_dt], off_dt)
        h0 = jnp.zeros((bsz, 2, N_GROUPS, D_STATE, HEADS_PER_GROUP * HEAD_DIM), F32)
        h_ctx = _ssd(xbc_c, dtc.reshape(bsz, n_ctx, 2 * LANES), dt_bias2, a_log2, h0, with_y=False)

        p, dtl = _in_proj(x2d, norm1_w[layer], sc1, sh1, w_main, w_dt, n, tm)
        n_conv = off_dt + 2 * BC_WIDTH
        xbc = _conv_silu(p.reshape(bsz, n, -1), conv_w[layer], conv_b[layer], n_conv)
        y2 = _ssd(xbc, dtl.reshape(bsz, n, 2 * LANES), dt_bias2, a_log2, h_ctx, with_y=True)
        ff = _fnet_mix(p, (n_conv + d_inner) // d, bsz, n)

        dsk = jnp.repeat(d_skip[layer].astype(F32), HEAD_DIM).reshape(1, d_inner)
        wr = jnp.zeros((d, LANES), F32).at[:, :N_EXPERTS].set(w_router[layer])
        br = jnp.full((1, LANES), NEG_BIG, F32).at[0, :N_EXPERTS].set(b_router[layer])
        x1, h2, ti, tw, rk, cnt = _tail(
            y2.reshape(2, m, d_inner), xbc.reshape(m, n_conv), p, ff, x2d, g1, sc2, sh2, dsk,
            ssd_norm_w[layer].reshape(1, d_inner), w_ssd_out[layer].astype(BF16), w_four_out[layer].astype(BF16),
            w_o[layer].astype(BF16), norm2_w[layer].reshape(1, d), wr, br, n, min(TAIL_ROWS, n))

        counts = cnt[0, :N_EXPERTS].astype(jnp.int32)
        dest, tile_expert, n_used, tile_valid = _dispatch_plan(ti[:TOP_K], rk[:TOP_K], counts, MOE_TILE)
        dest_km = dest.reshape(-1)
        xs = _scatter_rows(h2, dest_km, tile_expert.shape[0] * MOE_TILE)
        ys = _experts(xs, tile_expert, n_used, tile_valid, w_gate_up[layer],
                      b_gate_up[layer].reshape(N_EXPERTS, 1, -1), w_down[layer],
                      b_down[layer].reshape(N_EXPERTS, 1, -1))
        y4 = _gather_rows(ys, dest_km).reshape(TOP_K, m, d // 2)
        x2d = _final(x1, y4, tw, g2, final_norm_w, n, tm)
    return x2d.reshape(bsz, n, d)
```

```python
import functools
import math

import numpy as np
import jax
import jax.numpy as jnp
from jax import lax
from jax.experimental import pallas as pl
from jax.experimental.pallas import tpu as pltpu
from jax.experimental.pallas import tpu_sc as plsc

F32 = jnp.float32
BF16 = jnp.bfloat16

EPS = 1e-6
GRID_W = 64
N_MOD = 6
F_GROUPS = 8
F_GROUP_DIM = 128
HEAD_DIM = 64
N_HEADS = 32
N_GROUPS = 4
HEADS_PER_GROUP = N_HEADS // N_GROUPS
D_STATE = 128
BC_WIDTH = N_GROUPS * D_STATE
CONV_K = 5
CHUNK = 128
N_EXPERTS = 32
TOP_K = 4
SWIGLU_LIMIT = 7.0
SWIGLU_ALPHA = 1.702

LANES = 128
VMEM_LIMIT_BYTES = 56 * 1024 * 1024
NEG_BIG = -1e30
MOE_TILE = 512
TAIL_ROWS = 512
SC_GATHER_WINDOW = 64


def _cparams(*sem):
    return pltpu.CompilerParams(dimension_semantics=sem, vmem_limit_bytes=VMEM_LIMIT_BYTES)


def _sigmoid(v):
    return 1.0 / (1.0 + jnp.exp(-v))


def _silu(v):
    return v * _sigmoid(v)


def _softplus(v):
    return jnp.maximum(v, 0.0) + jnp.log(1.0 + jnp.exp(-jnp.abs(v)))


def _pack_bf16_pair(v):
    w = v.shape[1] // 2
    bits = lax.bitcast_convert_type(v.astype(BF16).astype(F32), jnp.uint32)
    return bits[:, :w] | (bits[:, w:] >> 16)


def _unpack_bf16_pair(p):
    hi = lax.bitcast_convert_type(p & jnp.uint32(0xFFFF0000), F32)
    lo = lax.bitcast_convert_type(p << 16, F32)
    return hi, lo


def _mod_kernel(c_ref, w_ref, b_ref, o_ref):
    s = _silu(c_ref[...]).astype(BF16)
    o_ref[...] = jnp.dot(s, w_ref[...].astype(BF16), preferred_element_type=F32) + b_ref[...]


def _modulation(c_rows, w_mod, b_mod):
    rows, d = c_rows.shape
    n_out = w_mod.shape[1]
    tn = 1024
    return pl.pallas_call(
        _mod_kernel,
        grid=(n_out // tn,),
        in_specs=[pl.BlockSpec((rows, d), lambda j: (0, 0)),
                  pl.BlockSpec((d, tn), lambda j: (0, j)),
                  pl.BlockSpec((1, tn), lambda j: (0, j))],
        out_specs=pl.BlockSpec((rows, tn), lambda j: (0, j)),
        out_shape=jax.ShapeDtypeStruct((rows, n_out), F32),
        compiler_params=_cparams("arbitrary"),
        name="modulation",
    )(c_rows, w_mod, b_mod.reshape(1, n_out))


_INPROJ_TN = 1024


def _inproj_kernel(x_ref, nw_ref, sc_ref, sh_ref, w_ref, wdt_ref, p_ref, dt_ref):
    x = x_ref[...]
    inv = lax.rsqrt(jnp.mean(x * x, axis=-1, keepdims=True) + EPS)
    h = (x * inv * nw_ref[...]) * (1.0 + sc_ref[...]) + sh_ref[...]
    hb = h.astype(BF16)
    dt_ref[...] = jnp.dot(hb, wdt_ref[...], preferred_element_type=F32)
    for j in range(w_ref.shape[1] // _INPROJ_TN):
        cols = slice(j * _INPROJ_TN, (j + 1) * _INPROJ_TN)
        p_ref[:, cols] = jnp.dot(hb, w_ref[:, cols], preferred_element_type=F32).astype(p_ref.dtype)


def _in_proj(x2d, norm_w, sc, sh, w_main, w_dt, rows_per_mod, tm):
    m, d = x2d.shape
    n = w_main.shape[1]
    ndt = w_dt.shape[1]
    resident = dict(pipeline_mode=pl.Buffered(1))
    return pl.pallas_call(
        _inproj_kernel,
        grid=(m // tm,),
        in_specs=[pl.BlockSpec((tm, d), lambda i: (i, 0)),
                  pl.BlockSpec((1, d), lambda i: (0, 0), **resident),
                  pl.BlockSpec((None, 1, d), lambda i: ((i * tm) // rows_per_mod, 0, 0)),
                  pl.BlockSpec((None, 1, d), lambda i: ((i * tm) // rows_per_mod, 0, 0)),
                  pl.BlockSpec((d, n), lambda i: (0, 0), **resident),
                  pl.BlockSpec((d, ndt), lambda i: (0, 0), **resident)],
        out_specs=[pl.BlockSpec((tm, n), lambda i: (i, 0)),
                   pl.BlockSpec((tm, ndt), lambda i: (i, 0))],
        out_shape=[jax.ShapeDtypeStruct((m, n), BF16),
                   jax.ShapeDtypeStruct((m, ndt), F32)],
        compiler_params=_cparams("arbitrary"),
        name="in_proj",
    )(x2d, norm_w.reshape(1, d), sc, sh, w_main, w_dt)


_CONV_HALO = 16


_CONV_ROWS = 64


def _conv_shift_matrix(rc):
    win = rc + 2 * _CONV_HALO
    s = np.zeros((rc, CONV_K * win), np.float32)
    for k in range(CONV_K):
        for l in range(rc):
            s[l, k * win + _CONV_HALO + l + k - CONV_K // 2] = 1.0
    return jnp.asarray(s, BF16)


def _conv_kernel(p_ref, w_ref, b_ref, s_ref, o_ref, *, n, rc):
    tc = o_ref.shape[-1]
    w = w_ref[...].astype(BF16)
    bias = b_ref[...]
    smat = s_ref[...]
    zeros = jnp.zeros((_CONV_HALO, tc), BF16)
    for r0 in range(0, n, rc):
        top = zeros if r0 == 0 else p_ref[r0 - _CONV_HALO:r0, :]
        bot = zeros if r0 + rc >= n else p_ref[r0 + rc:r0 + rc + _CONV_HALO, :]
        window = jnp.concatenate([top, p_ref[r0:r0 + rc, :], bot], axis=0)
        taps = jnp.concatenate([window * w[k:k + 1, :] for k in range(CONV_K)], axis=0)
        acc = jnp.dot(smat, taps, preferred_element_type=F32) + bias
        o_ref[r0:r0 + rc, :] = _silu(acc).astype(o_ref.dtype)


def _conv_silu(p3d, conv_w, conv_b, n_ch):
    bsz, n, _ = p3d.shape
    tc = 512
    rc = min(_CONV_ROWS, n)
    smat = _conv_shift_matrix(rc)
    return pl.pallas_call(
        functools.partial(_conv_kernel, n=n, rc=rc),
        grid=(bsz, n_ch // tc),
        in_specs=[pl.BlockSpec((None, n, tc), lambda b, j: (b, 0, j)),
                  pl.BlockSpec((CONV_K, tc), lambda b, j: (0, j)),
                  pl.BlockSpec((1, tc), lambda b, j: (0, j)),
                  pl.BlockSpec(smat.shape, lambda b, j: (0, 0))],
        out_specs=pl.BlockSpec((None, n, tc), lambda b, j: (b, 0, j)),
        out_shape=jax.ShapeDtypeStruct((bsz, n, n_ch), BF16),
        compiler_params=_cparams("arbitrary", "arbitrary"),
        name="conv_silu",
    )(p3d, conv_w, conv_b.reshape(1, -1), smat)


LOG2E = 1.4426950408889634
SSD_SUB = 16


def _head_expand_matrix():
    e = np.zeros((LANES, N_HEADS * HEAD_DIM), np.float32)
    for h in range(N_HEADS):
        e[h, h * HEAD_DIM:(h + 1) * HEAD_DIM] = 1.0
    return jnp.asarray(e, BF16)


def _ssd_chunk_terms(dt_raw, dt_bias, a_log, expand, tri, fwd):
    q = CHUNK
    dt = _softplus(dt_raw + dt_bias)
    da = dt * (-jnp.exp(a_log))
    hi = da.astype(BF16)
    r1 = da - hi.astype(F32)
    mid = r1.astype(BF16)
    lo = (r1 - mid.astype(F32)).astype(BF16)
    ones_tri = jnp.where(tri, 1.0, 0.0).astype(BF16)
    cs3 = jnp.dot(ones_tri, jnp.concatenate([hi, mid, lo], axis=1), preferred_element_type=F32)
    cs = cs3[:, :LANES] + cs3[:, LANES:2 * LANES] + cs3[:, 2 * LANES:]
    tot = jnp.where(fwd, cs[q - 1:q, :], cs[0:1, :])
    dte = dt * jnp.exp(tot - cs)
    dec = jnp.exp(tot)
    dec_hi = dec.astype(BF16)
    dec_lo = (dec - dec_hi.astype(F32)).astype(BF16)
    dec2 = jnp.concatenate([dec_hi, dec_lo, jnp.zeros((6, LANES), BF16)], axis=0)
    stacked = jnp.concatenate([jnp.exp(cs).astype(BF16), dte.astype(BF16), dec2], axis=0)
    spread = jnp.dot(stacked, expand, preferred_element_type=F32)
    ecs_e = spread[:q]
    dte_e = spread[q:2 * q].astype(BF16)
    dec_e = jnp.broadcast_to(spread[2 * q:2 * q + 1] + spread[2 * q + 1:2 * q + 2], (8, spread.shape[1]))
    cs2 = cs * LOG2E
    return cs2, cs2.T, dt.T, ecs_e, dte_e, dec_e


def _ssd_kernel(*refs, with_y, nc, n_sub):
    if with_y:
        (x_ref, b_ref, c_ref, dt_ref, dtn_ref, dtb_ref, alog_ref, exp_ref, h0_ref, y_ref,
         s_scr, cs_scr, dee_scr, dce_scr, ecs_scr) = refs
    else:
        (x_ref, b_ref, dt_ref, dtn_ref, dtb_ref, alog_ref, exp_ref, h0_ref, hout_ref,
         s_scr, cs_scr, dee_scr, dce_scr) = refs
    q = CHUNK
    gw = HEADS_PER_GROUP * HEAD_DIM
    d = pl.program_id(1)
    c = pl.program_id(2)
    row = lax.broadcasted_iota(jnp.int32, (q, q), 0)
    col = lax.broadcasted_iota(jnp.int32, (q, q), 1)
    fwd = d == 0
    tri = jnp.where(fwd, row - col, col - row) >= 0
    first_half = col < HEAD_DIM

    def store_terms(slot, terms):
        cs2, cs2_t, dt_t, ecs_e, dte_e, dec_e = terms
        cs_scr[slot, 0] = cs2
        cs_scr[slot, 1] = cs2_t
        cs_scr[slot, 2] = dt_t
        dee_scr[slot] = dte_e
        dce_scr[slot] = dec_e
        if with_y:
            ecs_scr[slot] = ecs_e

    def terms_of(ref, sub):
        rows = pl.ds(pl.multiple_of(sub * q, q), q)
        return _ssd_chunk_terms(ref[rows, :], dtb_ref[...], alog_ref[...], exp_ref[...], tri, fwd)

    def sub_of(j):
        return jnp.where(fwd, j, n_sub - 1 - j)

    @pl.when(c == 0)
    def _():
        s_scr[...] = h0_ref[...]
        store_terms(0, terms_of(dt_ref, sub_of(0)))

    for j in range(n_sub):
        slot = j % 2
        rows = pl.ds(pl.multiple_of(sub_of(j) * q, q), q)
        cs2 = cs_scr[slot, 0]
        cs2_t = cs_scr[slot, 1]
        dt_t = cs_scr[slot, 2]
        if j + 1 < n_sub:
            store_terms(1 - slot, terms_of(dt_ref, sub_of(j + 1)))
        else:
            store_terms(1 - slot, terms_of(dtn_ref, sub_of(0)))

        for g in range(N_GROUPS):
            cols = slice(g * gw, (g + 1) * gw)
            bg = b_ref[rows, g * D_STATE:(g + 1) * D_STATE]
            xg = x_ref[rows, cols]
            s_old = s_scr[g]
            if with_y:
                cg = c_ref[rows, g * D_STATE:(g + 1) * D_STATE]
                cb = lax.dot_general(cg, bg, (((1,), (1,)), ((), ())), preferred_element_type=F32)
                y_off = jnp.dot(cg, s_old.astype(BF16), preferred_element_type=F32)
                for k in range(HEADS_PER_GROUP // 2):
                    pair = g * (HEADS_PER_GROUP // 2) + k
                    lanes = slice(pair * LANES, (pair + 1) * LANES)
                    xp = xg[:, k * LANES:(k + 1) * LANES]
                    xz = jnp.zeros_like(xp)
                    lhs = []
                    for hh in (2 * pair, 2 * pair + 1):
                        colb = jnp.broadcast_to(cs2[:, hh:hh + 1], (q, q))
                        rowb = cs2_t[hh:hh + 1, :]
                        decay = jnp.exp2(jnp.where(tri, colb - rowb, -jnp.inf))
                        lhs.append((cb * decay * dt_t[hh:hh + 1, :]).astype(BF16))
                    y_diag = jnp.dot(jnp.concatenate(lhs, axis=1),
                                     jnp.concatenate([jnp.where(first_half, xp, xz), jnp.where(first_half, xz, xp)],
                                                     axis=0),
                                     preferred_element_type=F32)
                    y = y_diag + ecs_scr[slot, :, lanes] * y_off[:, k * LANES:(k + 1) * LANES]
                    y_ref[rows, lanes] = y.astype(y_ref.dtype)
            xdte = xg * dee_scr[slot, :, cols]
            upd = lax.dot_general(bg, xdte, (((0,), (0,)), ((), ())), preferred_element_type=F32)
            s_scr[g] = s_old * dce_scr[slot, 0:1, cols] + upd

    if not with_y:
        @pl.when(c == nc - 1)
        def _():
            hout_ref[...] = s_scr[...]


def _ssd(xbc, dt_raw, dt_bias2, a_log2, h0, with_y):
    bsz, n, _ = xbc.shape
    n_sub = min(SSD_SUB, n // CHUNK)
    assert n_sub % 2 == 0 and n % (n_sub * CHUNK) == 0
    blk = n_sub * CHUNK
    nc = n // blk
    d_inner = N_HEADS * HEAD_DIM
    gw = HEADS_PER_GROUP * HEAD_DIM
    x_blk = d_inner // BC_WIDTH

    def ceff(dd, cc):
        return cc + dd * (nc - 1 - 2 * cc)

    def cnext(dd, cc):
        return ceff(dd, jnp.minimum(cc + 1, nc - 1))

    in_specs = [pl.BlockSpec((None, blk, d_inner), lambda b, dd, cc: (b, ceff(dd, cc), 0)),
                pl.BlockSpec((None, blk, BC_WIDTH), lambda b, dd, cc: (b, ceff(dd, cc), x_blk + dd))]
    args = [xbc, xbc]
    if with_y:
        in_specs.append(pl.BlockSpec((None, blk, BC_WIDTH), lambda b, dd, cc: (b, ceff(dd, cc), x_blk + 2 + dd)))
        args.append(xbc)
    in_specs += [pl.BlockSpec((None, blk, LANES), lambda b, dd, cc: (b, ceff(dd, cc), dd)),
                 pl.BlockSpec((None, blk, LANES), lambda b, dd, cc: (b, cnext(dd, cc), dd)),
                 pl.BlockSpec((None, 1, LANES), lambda b, dd, cc: (dd, 0, 0)),
                 pl.BlockSpec((None, 1, LANES), lambda b, dd, cc: (dd, 0, 0)),
                 pl.BlockSpec((LANES, d_inner), lambda b, dd, cc: (0, 0)),
                 pl.BlockSpec((None, None, N_GROUPS, D_STATE, gw), lambda b, dd, cc: (b, dd, 0, 0, 0))]
    args += [dt_raw, dt_raw, dt_bias2, a_log2, _head_expand_matrix(), h0]
    scratch = [pltpu.VMEM((N_GROUPS, D_STATE, gw), F32),
               pltpu.VMEM((2, 3, CHUNK, LANES), F32),
               pltpu.VMEM((2, CHUNK, d_inner), BF16),
               pltpu.VMEM((2, 8, d_inner), F32)]
    if with_y:
        out_specs = pl.BlockSpec((None, None, blk, d_inner), lambda b, dd, cc: (dd, b, ceff(dd, cc), 0))
        out_shape = jax.ShapeDtypeStruct((2, bsz, n, d_inner), BF16)
        scratch += [pltpu.VMEM((2, CHUNK, d_inner), F32)]
    else:
        out_specs = pl.BlockSpec((None, None, N_GROUPS, D_STATE, gw), lambda b, dd, cc: (b, dd, 0, 0, 0))
        out_shape = jax.ShapeDtypeStruct((bsz, 2, N_GROUPS, D_STATE, gw), F32)
    return pl.pallas_call(
        functools.partial(_ssd_kernel, with_y=with_y, nc=nc, n_sub=n_sub),
        grid=(bsz, 2, nc),
        in_specs=in_specs,
        out_specs=out_specs,
        out_shape=out_shape,
        scratch_shapes=scratch,
        compiler_params=_cparams("arbitrary", "arbitrary", "arbitrary"),
        name="ssd_scan" if with_y else "ssd_ctx_state",
    )(*args)


def _dft_mats(n):
    k = np.arange(n)
    ang = 2.0 * np.pi * ((k[:, None] * k[None, :]) % n) / n
    return np.cos(ang), np.sin(ang)


def _strided_rows(scr, t, start, count, stride, width):
    return jnp.concatenate([scr[t, lb, pl.ds(start, count, stride=stride), :] for lb in range(width // LANES)],
                           axis=1)


def _fnet_stage1_kernel(u_ref, chan_ref, m1_ref, twc_ref, tws_ref, o_ref, zs_scr, *, tcs):
    r = u_ref.shape[0]
    width = u_ref.shape[-1]
    u2 = u_ref[...].reshape(r * tcs, width)
    chan = chan_ref[...]
    for g in range(F_GROUPS):
        pq = jnp.dot(u2[:, g * F_GROUP_DIM:(g + 1) * F_GROUP_DIM], chan, preferred_element_type=F32)
        zs_scr[0, g] = pq[:, :F_GROUP_DIM]
        zs_scr[1, g] = pq[:, F_GROUP_DIM:]
    m1 = m1_ref[...]
    for j in range(tcs):
        zz = jnp.concatenate([_strided_rows(zs_scr, 0, j, r, tcs, width),
                              _strided_rows(zs_scr, 1, j, r, tcs, width)], axis=0).astype(BF16)
        u = jnp.dot(m1, zz, preferred_element_type=F32)
        ur = u[:r]
        ui = u[r:]
        tc = twc_ref[j]
        ts = tws_ref[j]
        o_ref[0, j] = (ur * tc + ui * ts).astype(o_ref.dtype)
        o_ref[1, j] = (ui * tc - ur * ts).astype(o_ref.dtype)


def _fnet_stage2_kernel(u_ref, k2_ref, o_ref):
    c, tbs, width = o_ref.shape
    uu = jnp.concatenate([u_ref[0].reshape(c * tbs, width), u_ref[1].reshape(c * tbs, width)], axis=0)
    out = jnp.dot(k2_ref[...], uu, preferred_element_type=F32)
    o_ref[...] = out.reshape(c, tbs, width).astype(o_ref.dtype)


def _fnet_mix(p2d, col_blk, bsz, n):
    m = p2d.shape[0]
    width = F_GROUPS * F_GROUP_DIM
    cgrid = GRID_W
    rgrid = n // cgrid
    scale = 1.0 / math.sqrt(n * F_GROUP_DIM)

    cc, sc = _dft_mats(F_GROUP_DIM)
    chan = jnp.asarray(np.concatenate([cc, sc], axis=1) * scale, BF16)

    cr, sr = _dft_mats(rgrid)
    m1 = jnp.asarray(np.block([[cr, -sr], [-sr, -cr]]), BF16)
    bb = np.arange(rgrid)[None, :]
    ci = np.arange(cgrid)[:, None]
    ang = 2.0 * np.pi * ((ci * bb) % n) / n
    twc = jnp.asarray(np.cos(ang)[:, :, None], F32)
    tws = jnp.asarray(np.sin(ang)[:, :, None], F32)
    tcs = 16
    p4 = p2d.reshape(bsz, rgrid, cgrid, p2d.shape[-1])
    u5 = pl.pallas_call(
        functools.partial(_fnet_stage1_kernel, tcs=tcs),
        grid=(bsz, cgrid // tcs),
        in_specs=[pl.BlockSpec((None, rgrid, tcs, width), lambda b, j: (b, 0, j, col_blk)),
                  pl.BlockSpec((F_GROUP_DIM, 2 * F_GROUP_DIM), lambda b, j: (0, 0)),
                  pl.BlockSpec((2 * rgrid, 2 * rgrid), lambda b, j: (0, 0)),
                  pl.BlockSpec((tcs, rgrid, 1), lambda b, j: (j, 0, 0)),
                  pl.BlockSpec((tcs, rgrid, 1), lambda b, j: (j, 0, 0))],
        out_specs=pl.BlockSpec((2, None, tcs, rgrid, width), lambda b, j: (0, b, j, 0, 0)),
        out_shape=jax.ShapeDtypeStruct((2, bsz, cgrid, rgrid, width), BF16),
        scratch_shapes=[pltpu.VMEM((2, width // LANES, rgrid * tcs, LANES), F32)],
        compiler_params=_cparams("arbitrary", "arbitrary"),
        name="fnet_stage1",
    )(p4, chan, m1, twc, tws)

    cc2, sc2 = _dft_mats(cgrid)
    tbs = min(16, rgrid)
    eye = np.eye(tbs)
    k2 = jnp.asarray(np.concatenate([np.kron(cc2, eye), np.kron(sc2, eye)], axis=1), BF16)
    out = pl.pallas_call(
        _fnet_stage2_kernel,
        grid=(bsz, rgrid // tbs),
        in_specs=[pl.BlockSpec((2, None, cgrid, tbs, width), lambda b, j: (0, b, 0, j, 0)),
                  pl.BlockSpec(k2.shape, lambda b, j: (0, 0))],
        out_specs=pl.BlockSpec((None, cgrid, tbs, width), lambda b, j: (b, 0, j, 0)),
        out_shape=jax.ShapeDtypeStruct((bsz, cgrid, rgrid, width), BF16),
        compiler_params=_cparams("arbitrary", "arbitrary"),
        name="fnet_stage2",
    )(u5, k2)
    return out.reshape(m, width)


def _tail_kernel(yf_ref, yb_ref, xs_ref, z_ref, ff_ref, gf_ref, gs_ref, x_ref, g1_ref, sc2_ref, sh2_ref,
                 dsk_ref, snw_ref, wso_ref, wfo_ref, wo_ref, n2w_ref, wr_ref, br_ref,
                 x1_ref, h2_ref, ti_ref, tw_ref, rk_ref, cnt_ref, carry_scr):
    y_four = jnp.dot(ff_ref[...], wfo_ref[...], preferred_element_type=F32)
    y = (yf_ref[0] + yb_ref[0]).astype(F32) + dsk_ref[...] * xs_ref[...].astype(F32)
    g = y * _silu(z_ref[...]).astype(F32)
    inv = lax.rsqrt(jnp.mean(g * g, axis=-1, keepdims=True) + EPS)
    gn = (g * inv * snw_ref[...]).astype(BF16)
    y_ssd = jnp.dot(gn, wso_ref[...], preferred_element_type=F32)
    t = _sigmoid(gf_ref[...]) * y_four.astype(BF16) + _sigmoid(gs_ref[...]) * y_ssd.astype(BF16)
    mix = jnp.dot(t, wo_ref[...], preferred_element_type=F32)
    x1 = x_ref[...] + g1_ref[...] * mix
    x1_ref[...] = x1
    inv2 = lax.rsqrt(jnp.mean(x1 * x1, axis=-1, keepdims=True) + EPS)
    h2 = (x1 * inv2 * n2w_ref[...]) * (1.0 + sc2_ref[...]) + sh2_ref[...]
    h2_ref[...] = _pack_bf16_pair(h2)
    h_hi = h2.astype(BF16)
    h_lo = (h2 - h_hi.astype(F32)).astype(BF16)
    wr = wr_ref[...]
    w_hi = wr.astype(BF16)
    w_lo = (wr - w_hi.astype(F32)).astype(BF16)
    logits = (jnp.dot(h_hi, w_hi, preferred_element_type=F32)
              + jnp.dot(h_hi, w_lo, preferred_element_type=F32)
              + jnp.dot(h_lo, w_hi, preferred_element_type=F32)) + br_ref[...]
    tm = logits.shape[0]
    lane = lax.broadcasted_iota(jnp.int32, (tm, LANES), 1)
    vals, idxs = [], []
    cur = logits
    for _ in range(TOP_K):
        mx = jnp.max(cur, axis=-1, keepdims=True)
        ix = jnp.min(jnp.where(cur == mx, lane, LANES), axis=-1, keepdims=True)
        vals.append(mx)
        idxs.append(ix)
        cur = jnp.where(lane == ix, NEG_BIG * 2.0, cur)
    es = [jnp.exp(v - vals[0]) for v in vals]
    den = es[0] + es[1] + es[2] + es[3]
    ti = jnp.zeros((tm, LANES), jnp.int32)
    tw = jnp.zeros((tm, LANES), F32)
    for k in range(TOP_K):
        ti = jnp.where(lane == k, idxs[k], ti)
        tw = jnp.where(lane == k, es[k] / den, tw)
    ti_ref[...] = ti.T[:8, :]
    tw_ref[...] = tw

    @pl.when(pl.program_id(0) == 0)
    def _():
        carry_scr[...] = jnp.zeros(carry_scr.shape, F32)

    onehots = [jnp.where(lane == ix, 1.0, 0.0) for ix in idxs]
    cnt = onehots[0] + onehots[1] + onehots[2] + onehots[3]
    r_i = lax.broadcasted_iota(jnp.int32, (tm, tm), 0)
    c_i = lax.broadcasted_iota(jnp.int32, (tm, tm), 1)
    earlier = jnp.where(r_i > c_i, 1.0, 0.0).astype(BF16)
    before = jnp.dot(earlier, cnt.astype(BF16), preferred_element_type=F32) + carry_scr[0:1, :]
    rk = jnp.zeros((tm, LANES), jnp.int32)
    for k in range(TOP_K):
        rank_k = jnp.sum(onehots[k] * before, axis=-1, keepdims=True)
        rk = jnp.where(lane == k, rank_k.astype(jnp.int32), rk)
    rk_ref[...] = rk.T[:8, :]
    total = carry_scr[...] + jnp.sum(cnt, axis=0, keepdims=True)
    carry_scr[...] = total
    cnt_ref[...] = total


def _tail(y2, xbc2d, p2d, ff, x2d, g1, sc2, sh2, dsk, snw, wso, wfo, wo, n2w, wr, br, rows_per_mod, tm):
    m, d = x2d.shape
    di = y2.shape[-1]
    zb = (2 * di) // di
    fb = (3 * di) // d
    row = lambda i: (i, 0)
    modrow = lambda i: ((i * tm) // rows_per_mod, 0, 0)
    const = lambda i: (0, 0)
    in_specs = [pl.BlockSpec((1, tm, di), lambda i: (0, i, 0)),
                pl.BlockSpec((1, tm, di), lambda i: (1, i, 0)),
                pl.BlockSpec((tm, di), row),
                pl.BlockSpec((tm, di), lambda i: (i, zb)),
                pl.BlockSpec((tm, d), row),
                pl.BlockSpec((tm, d), lambda i: (i, fb + 1)),
                pl.BlockSpec((tm, d), lambda i: (i, fb + 2)),
                pl.BlockSpec((tm, d), row),
                pl.BlockSpec((None, 1, d), modrow),
                pl.BlockSpec((None, 1, d), modrow),
                pl.BlockSpec((None, 1, d), modrow),
                pl.BlockSpec((1, di), const),
                pl.BlockSpec((1, di), const),
                pl.BlockSpec((di, d), const),
                pl.BlockSpec((d, d), const),
                pl.BlockSpec((d, d), const),
                pl.BlockSpec((1, d), const),
                pl.BlockSpec((d, LANES), const),
                pl.BlockSpec((1, LANES), const)]
    tcol = lambda i: (0, i)
    out_specs = [pl.BlockSpec((tm, d), row), pl.BlockSpec((tm, d // 2), row),
                 pl.BlockSpec((8, tm), tcol), pl.BlockSpec((tm, LANES), row), pl.BlockSpec((8, tm), tcol),
                 pl.BlockSpec((8, LANES), const)]
    out_shape = [jax.ShapeDtypeStruct((m, d), F32), jax.ShapeDtypeStruct((m, d // 2), jnp.uint32),
                 jax.ShapeDtypeStruct((8, m), jnp.int32), jax.ShapeDtypeStruct((m, LANES), F32),
                 jax.ShapeDtypeStruct((8, m), jnp.int32), jax.ShapeDtypeStruct((8, LANES), F32)]
    return pl.pallas_call(
        _tail_kernel,
        grid=(m // tm,),
        in_specs=in_specs,
        out_specs=out_specs,
        out_shape=out_shape,
        scratch_shapes=[pltpu.VMEM((8, LANES), F32)],
        compiler_params=_cparams("arbitrary"),
        name="tail",
    )(y2, y2, xbc2d, p2d, ff, p2d, p2d, x2d, g1, sc2, sh2, dsk, snw, wso, wfo, wo, n2w, wr, br)


def _gather_rows(src, idx):
    n_out = idx.shape[0]
    width = src.shape[1]
    win = SC_GATHER_WINDOW
    info = plsc.get_sparse_core_info()
    n_cores, n_workers = info.num_cores, info.num_cores * info.num_subcores
    per_worker = n_out // n_workers
    assert per_worker * n_workers == n_out and per_worker % (2 * win) == 0
    mesh = plsc.VectorSubcoreMesh(core_axis_name="core", subcore_axis_name="subcore")

    @functools.partial(
        pl.kernel, out_type=jax.ShapeDtypeStruct((n_out, width), src.dtype), mesh=mesh,
        scratch_types=[pltpu.VMEM((win,), jnp.int32), pltpu.VMEM((win,), jnp.int32),
                       pltpu.VMEM((win, width), src.dtype), pltpu.VMEM((win, width), src.dtype),
                       pltpu.SemaphoreType.DMA, pltpu.SemaphoreType.DMA,
                       pltpu.SemaphoreType.DMA, pltpu.SemaphoreType.DMA],
        name="gather_rows")
    def gather(src_hbm, idx_hbm, out_hbm, idx_a, idx_b, rows_a, rows_b, gsem_a, gsem_b, ssem_a, ssem_b):
        worker = lax.axis_index("subcore") * n_cores + lax.axis_index("core")
        base = worker * per_worker

        @pl.loop(0, per_worker, step=2 * win)
        def _(off):
            pltpu.sync_copy(idx_hbm.at[pl.ds(base + off, win)], idx_a)
            ga = pltpu.async_copy(src_hbm.at[idx_a], rows_a, gsem_a)
            pltpu.sync_copy(idx_hbm.at[pl.ds(base + off + win, win)], idx_b)
            gb = pltpu.async_copy(src_hbm.at[idx_b], rows_b, gsem_b)
            ga.wait()
            sa = pltpu.async_copy(rows_a, out_hbm.at[pl.ds(base + off, win)], ssem_a)
            gb.wait()
            sb = pltpu.async_copy(rows_b, out_hbm.at[pl.ds(base + off + win, win)], ssem_b)
            sa.wait()
            sb.wait()

    return gather(src, idx)


def _scatter_rows(src, dest, n_rows):
    m, width = src.shape
    win = SC_GATHER_WINDOW
    info = plsc.get_sparse_core_info()
    n_cores, n_workers = info.num_cores, info.num_cores * info.num_subcores
    per_worker = m // n_workers
    assert per_worker * n_workers == m and per_worker % win == 0
    mesh = plsc.VectorSubcoreMesh(core_axis_name="core", subcore_axis_name="subcore")

    @functools.partial(
        pl.kernel, out_type=jax.ShapeDtypeStruct((n_rows, width), src.dtype), mesh=mesh,
        scratch_types=[pltpu.VMEM((win, width), src.dtype)]
        + [pltpu.VMEM((win,), jnp.int32)] * TOP_K + [pltpu.SemaphoreType.DMA] * (TOP_K + 1),
        name="scatter_rows")
    def scatter(src_hbm, dest_hbm, out_hbm, rows_v, *rest):
        idx_v, sems, row_sem = rest[:TOP_K], rest[TOP_K:2 * TOP_K], rest[2 * TOP_K]
        worker = lax.axis_index("subcore") * n_cores + lax.axis_index("core")
        base = worker * per_worker

        @pl.loop(0, per_worker, step=win)
        def _(off):
            t0 = base + off
            loads = [pltpu.async_copy(src_hbm.at[pl.ds(t0, win)], rows_v, row_sem)]
            loads += [pltpu.async_copy(dest_hbm.at[pl.ds(k * m + t0, win)], idx_v[k], sems[k]) for k in range(TOP_K)]
            for ld in loads:
                ld.wait()
            copies = [pltpu.async_copy(rows_v, out_hbm.at[idx_v[k]], sems[k]) for k in range(TOP_K)]
            for cp in copies:
                cp.wait()

    return scatter(src, dest)


def _expert_kernel(te_ref, nu_ref, tv_ref, x_ref, wgu_ref, bgu_ref, wd_ref, bd_ref, o_ref, wgu_scr, wd_scr):
    i = pl.program_id(0)
    used = i < nu_ref[0]
    new_expert = (i == 0) | (te_ref[i] != te_ref[jnp.maximum(i - 1, 0)])

    @pl.when(used & new_expert)
    def _():
        wgu_scr[...] = wgu_ref[...].astype(BF16)
        wd_scr[...] = wd_ref[...].astype(BF16)

    @pl.when(used)
    def _():
        dff = wd_ref.shape[0]
        half = wgu_ref.shape[0] // 2
        rows = lax.broadcasted_iota(jnp.int32, x_ref.shape, 0)
        xa, xb = _unpack_bf16_pair(jnp.where(rows < tv_ref[i], x_ref[...], jnp.uint32(0)))
        gu = (jnp.dot(xa.astype(BF16), wgu_scr[:half, :], preferred_element_type=F32)
              + jnp.dot(xb.astype(BF16), wgu_scr[half:, :], preferred_element_type=F32)) + bgu_ref[...]
        gate = jnp.minimum(gu[:, :dff], SWIGLU_LIMIT)
        up = jnp.clip(gu[:, dff:], -SWIGLU_LIMIT, SWIGLU_LIMIT)
        act = (up + 1.0) * gate * _sigmoid(SWIGLU_ALPHA * gate)
        y = jnp.dot(act.astype(BF16), wd_scr[...], preferred_element_type=F32) + bd_ref[...]
        o_ref[...] = _pack_bf16_pair(y)

    @pl.when(jnp.logical_not(used))
    def _():
        o_ref[...] = jnp.zeros(o_ref.shape, o_ref.dtype)


def _experts(xs, tile_expert, n_used, tile_valid, wgu, bgu, wd, bd):
    rows, dh = xs.shape
    d = 2 * dh
    tm = MOE_TILE
    n_tiles = rows // tm
    dff2 = wgu.shape[-1]

    def tile(i, te, nu, tv):
        return (jnp.minimum(i, nu[0] - 1), 0)

    def wsel(i, te, nu, tv):
        return (te[jnp.minimum(i, nu[0] - 1)], 0, 0)

    grid_spec = pltpu.PrefetchScalarGridSpec(
        num_scalar_prefetch=3,
        grid=(n_tiles,),
        in_specs=[pl.BlockSpec((tm, dh), tile),
                  pl.BlockSpec((None, d, dff2), wsel),
                  pl.BlockSpec((None, 1, dff2), wsel),
                  pl.BlockSpec((None, dff2 // 2, d), wsel),
                  pl.BlockSpec((None, 1, d), wsel)],
        out_specs=pl.BlockSpec((tm, dh), lambda i, te, nu, tv: (i, 0)),
        scratch_shapes=[pltpu.VMEM((d, dff2), BF16), pltpu.VMEM((dff2 // 2, d), BF16)],
    )
    return pl.pallas_call(
        _expert_kernel,
        grid_spec=grid_spec,
        out_shape=jax.ShapeDtypeStruct((rows, dh), jnp.uint32),
        compiler_params=_cparams("arbitrary"),
        name="experts",
    )(tile_expert, n_used, tile_valid, xs, wgu, bgu, wd, bd)


def _final_kernel(x1_ref, ya_ref, yb_ref, yc_ref, yd_ref, tw_ref, g2_ref, fw_ref, o_ref):
    d = x1_ref.shape[-1]
    half = d // 2
    tw = tw_ref[...]
    acc_hi = jnp.zeros((x1_ref.shape[0], half), F32)
    acc_lo = jnp.zeros((x1_ref.shape[0], half), F32)
    for k, y_ref in enumerate((ya_ref, yb_ref, yc_ref, yd_ref)):
        y_hi, y_lo = _unpack_bf16_pair(y_ref[...])
        acc_hi = acc_hi + tw[:, k:k + 1] * y_hi
        acc_lo = acc_lo + tw[:, k:k + 1] * y_lo
    x_hi = x1_ref[:, :half] + g2_ref[:, :half] * acc_hi
    x_lo = x1_ref[:, half:] + g2_ref[:, half:] * acc_lo
    ms = (jnp.sum(x_hi * x_hi, axis=-1, keepdims=True) + jnp.sum(x_lo * x_lo, axis=-1, keepdims=True)) / d
    inv = lax.rsqrt(ms + EPS)
    o_ref[:, :half] = x_hi * inv * fw_ref[:, :half]
    o_ref[:, half:] = x_lo * inv * fw_ref[:, half:]


def _final(x1, y4, tw, g2, fw, rows_per_mod, tm):
    m, d = x1.shape
    return pl.pallas_call(
        _final_kernel,
        grid=(m // tm,),
        in_specs=[pl.BlockSpec((tm, d), lambda i: (i, 0))]
        + [pl.BlockSpec((None, tm, d // 2), functools.partial(lambda k, i: (k, i, 0), k)) for k in range(TOP_K)]
        + [pl.BlockSpec((tm, LANES), lambda i: (i, 0)),
           pl.BlockSpec((None, 1, d), lambda i: ((i * tm) // rows_per_mod, 0, 0)),
           pl.BlockSpec((1, d), lambda i: (0, 0))],
        out_specs=pl.BlockSpec((tm, d), lambda i: (i, 0)),
        out_shape=jax.ShapeDtypeStruct((m, d), F32),
        compiler_params=_cparams("arbitrary"),
        name="final",
    )(x1, y4, y4, y4, y4, tw, g2, fw.reshape(1, d))


def _dispatch_plan(e, rank, counts, tm):
    n_assign = e.size
    padded = (counts + tm - 1) // tm * tm
    pad_end = jnp.cumsum(padded)
    pad_start = pad_end - padded
    dest = rank.astype(jnp.int32)
    for j in range(N_EXPERTS - 1):
        dest = dest + jnp.where(e > j, padded[j], 0).astype(jnp.int32)
    n_tiles = n_assign // tm + N_EXPERTS
    tile_start = jnp.arange(n_tiles, dtype=jnp.int32) * tm
    tile_expert = jnp.minimum(jnp.sum(pad_end[None, :] <= tile_start[:, None], axis=1), N_EXPERTS - 1).astype(jnp.int32)
    n_used = (pad_end[-1] // tm).astype(jnp.int32).reshape(1)
    tile_valid = jnp.clip(pad_start[tile_expert] + counts[tile_expert] - tile_start, 0, tm).astype(jnp.int32)
    return dest, tile_expert, n_used, tile_valid


def kernel(x, c, ctx, c_ctx, w_mod, b_mod, norm1_w, norm2_w, w_in, conv_w, conv_b, dt_bias, a_log, d_skip,
           ssd_norm_w, w_ssd_out, w_four_out, w_o, w_router, b_router, w_gate_up, b_gate_up, w_down, b_down,
           final_norm_w):
    bsz, n, d = x.shape
    n_ctx = ctx.shape[1]
    depth = w_mod.shape[0]
    assert depth == 1, "a stacked model would also need the context stream's residual update"
    d_inner = N_HEADS * HEAD_DIM
    off_dt = d_inner + 2 * BC_WIDTH
    m = bsz * n
    x2d = x.reshape(m, d)
    tm = min(512, n)

    for layer in range(depth):
        rows = -(-(bsz + 1) // 8) * 8
        c_rows = jnp.zeros((rows, d), F32).at[:bsz].set(c).at[bsz].set(c_ctx)
        mod = _modulation(c_rows, w_mod[layer], b_mod[layer])
        mods = [mod[:bsz, i * d:(i + 1) * d].reshape(bsz, 1, d) for i in range(N_MOD)]
        sh1, sc1, g1, sh2, sc2, g2 = mods
        sh1_c = mod[bsz:bsz + 1, :d].reshape(1, 1, d)
        sc1_c = mod[bsz:bsz + 1, d:2 * d].reshape(1, 1, d)

        wl = w_in[layer]
        w_main = jnp.concatenate([wl[:, :off_dt], wl[:, off_dt + 2 * N_HEADS:]], axis=1).astype(BF16)
        w_dt = jnp.zeros((d, 2 * LANES), F32)
        w_dt = w_dt.at[:, :N_HEADS].set(wl[:, off_dt:off_dt + N_HEADS])
        w_dt = w_dt.at[:, LANES:LANES + N_HEADS].set(wl[:, off_dt + N_HEADS:off_dt + 2 * N_HEADS]).astype(BF16)
        pad = jnp.zeros((2, 1, LANES - N_HEADS), F32)
        dt_bias2 = jnp.concatenate([dt_bias[layer].reshape(2, 1, N_HEADS), pad], axis=-1)
        a_log2 = jnp.concatenate([a_log[layer].reshape(2, 1, N_HEADS), pad], axis=-1)

        w_ctx = w_main[:, :off_dt]
        pc, dtc = _in_proj(ctx.reshape(bsz * n_ctx, d), norm1_w[layer], sc1_c, sh1_c, w_ctx, w_dt,
                           bsz * n_ctx, min(512, n_ctx))
        xbc_c = _conv_silu(pc.reshape(bsz, n_ctx, off_dt), conv_w[layer][:, :off_dt], conv_b[layer][:off_dt], off_dt)
        h0 = jnp.zeros((bsz, 2, N_GROUPS, D_STATE, HEADS_PER_GROUP * HEAD_DIM), F32)
        h_ctx = _ssd(xbc_c, dtc.reshape(bsz, n_ctx, 2 * LANES), dt_bias2, a_log2, h0, with_y=False)

        p, dtl = _in_proj(x2d, norm1_w[layer], sc1, sh1, w_main, w_dt, n, tm)
        n_conv = off_dt + 2 * BC_WIDTH
        xbc = _conv_silu(p.reshape(bsz, n, -1), conv_w[layer], conv_b[layer], n_conv)
        y2 = _ssd(xbc, dtl.reshape(bsz, n, 2 * LANES), dt_bias2, a_log2, h_ctx, with_y=True)
        ff = _fnet_mix(p, (n_conv + d_inner) // d, bsz, n)

        dsk = jnp.repeat(d_skip[layer].astype(F32), HEAD_DIM).reshape(1, d_inner)
        wr = jnp.zeros((d, LANES), F32).at[:, :N_EXPERTS].set(w_router[layer])
        br = jnp.full((1, LANES), NEG_BIG, F32).at[0, :N_EXPERTS].set(b_router[layer])
        x1, h2, ti, tw, rk, cnt = _tail(
            y2.reshape(2, m, d_inner), xbc.reshape(m, n_conv), p, ff, x2d, g1, sc2, sh2, dsk,
            ssd_norm_w[layer].reshape(1, d_inner), w_ssd_out[layer].astype(BF16), w_four_out[layer].astype(BF16),
            w_o[layer].astype(BF16), norm2_w[layer].reshape(1, d), wr, br, n, min(TAIL_ROWS, n))

        counts = cnt[0, :N_EXPERTS].astype(jnp.int32)
        dest, tile_expert, n_used, tile_valid = _dispatch_plan(ti[:TOP_K], rk[:TOP_K], counts, MOE_TILE)
        dest_km = dest.reshape(-1)
        xs = _scatter_rows(h2, dest_km, tile_expert.shape[0] * MOE_TILE)
        ys = _experts(xs, tile_expert, n_used, tile_valid, w_gate_up[layer],
                      b_gate_up[layer].reshape(N_EXPERTS, 1, -1), w_down[layer],
                      b_down[layer].reshape(N_EXPERTS, 1, -1))
        y4 = _gather_rows(ys, dest_km).reshape(TOP_K, m, d // 2)
        x2d = _final(x1, y4, tw, g2, final_norm_w, n, tm)
    return x2d.reshape(bsz, n, d)
```

```python
import functools
import math

import numpy as np
import jax
import jax.numpy as jnp
from jax import lax
from jax.experimental import pallas as pl
from jax.experimental.pallas import tpu as pltpu
from jax.experimental.pallas import tpu_sc as plsc

F32 = jnp.float32
BF16 = jnp.bfloat16

EPS = 1e-6
GRID_W = 64
N_MOD = 6
F_GROUPS = 8
F_GROUP_DIM = 128
HEAD_DIM = 64
N_HEADS = 32
N_GROUPS = 4
HEADS_PER_GROUP = N_HEADS // N_GROUPS
D_STATE = 128
BC_WIDTH = N_GROUPS * D_STATE
CONV_K = 5
CHUNK = 128
N_EXPERTS = 32
TOP_K = 4
SWIGLU_LIMIT = 7.0
SWIGLU_ALPHA = 1.702

LANES = 128
VMEM_LIMIT_BYTES = 56 * 1024 * 1024
NEG_BIG = -1e30
MOE_TILE = 512
TAIL_ROWS = 512
SC_GATHER_WINDOW = 64


def _cparams(*sem):
    return pltpu.CompilerParams(dimension_semantics=sem, vmem_limit_bytes=VMEM_LIMIT_BYTES)


def _sigmoid(v):
    return 1.0 / (1.0 + jnp.exp(-v))


def _silu(v):
    return v * _sigmoid(v)


def _softplus(v):
    return jnp.maximum(v, 0.0) + jnp.log(1.0 + jnp.exp(-jnp.abs(v)))


def _pack_bf16_pair(v):
    w = v.shape[1] // 2
    bits = lax.bitcast_convert_type(v.astype(BF16).astype(F32), jnp.uint32)
    return bits[:, :w] | (bits[:, w:] >> 16)


def _unpack_bf16_pair(p):
    hi = lax.bitcast_convert_type(p & jnp.uint32(0xFFFF0000), F32)
    lo = lax.bitcast_convert_type(p << 16, F32)
    return hi, lo


def _mod_kernel(c_ref, w_ref, b_ref, o_ref):
    s = _silu(c_ref[...]).astype(BF16)
    o_ref[...] = jnp.dot(s, w_ref[...].astype(BF16), preferred_element_type=F32) + b_ref[...]


def _modulation(c_rows, w_mod, b_mod):
    rows, d = c_rows.shape
    n_out = w_mod.shape[1]
    tn = 1024
    return pl.pallas_call(
        _mod_kernel,
        grid=(n_out // tn,),
        in_specs=[pl.BlockSpec((rows, d), lambda j: (0, 0)),
                  pl.BlockSpec((d, tn), lambda j: (0, j)),
                  pl.BlockSpec((1, tn), lambda j: (0, j))],
        out_specs=pl.BlockSpec((rows, tn), lambda j: (0, j)),
        out_shape=jax.ShapeDtypeStruct((rows, n_out), F32),
        compiler_params=_cparams("arbitrary"),
        name="modulation",
    )(c_rows, w_mod, b_mod.reshape(1, n_out))


_INPROJ_TN = 1024


def _inproj_kernel(x_ref, nw_ref, sc_ref, sh_ref, w_ref, wdt_ref, p_ref, dt_ref):
    x = x_ref[...]
    inv = lax.rsqrt(jnp.mean(x * x, axis=-1, keepdims=True) + EPS)
    h = (x * inv * nw_ref[...]) * (1.0 + sc_ref[...]) + sh_ref[...]
    hb = h.astype(BF16)
    dt_ref[...] = jnp.dot(hb, wdt_ref[...], preferred_element_type=F32)
    for j in range(w_ref.shape[1] // _INPROJ_TN):
        cols = slice(j * _INPROJ_TN, (j + 1) * _INPROJ_TN)
        p_ref[:, cols] = jnp.dot(hb, w_ref[:, cols], preferred_element_type=F32).astype(p_ref.dtype)


def _in_proj(x2d, norm_w, sc, sh, w_main, w_dt, rows_per_mod, tm):
    m, d = x2d.shape
    n = w_main.shape[1]
    ndt = w_dt.shape[1]
    resident = dict(pipeline_mode=pl.Buffered(1))
    return pl.pallas_call(
        _inproj_kernel,
        grid=(m // tm,),
        in_specs=[pl.BlockSpec((tm, d), lambda i: (i, 0)),
                  pl.BlockSpec((1, d), lambda i: (0, 0), **resident),
                  pl.BlockSpec((None, 1, d), lambda i: ((i * tm) // rows_per_mod, 0, 0)),
                  pl.BlockSpec((None, 1, d), lambda i: ((i * tm) // rows_per_mod, 0, 0)),
                  pl.BlockSpec((d, n), lambda i: (0, 0), **resident),
                  pl.BlockSpec((d, ndt), lambda i: (0, 0), **resident)],
        out_specs=[pl.BlockSpec((tm, n), lambda i: (i, 0)),
                   pl.BlockSpec((tm, ndt), lambda i: (i, 0))],
        out_shape=[jax.ShapeDtypeStruct((m, n), BF16),
                   jax.ShapeDtypeStruct((m, ndt), F32)],
        compiler_params=_cparams("arbitrary"),
        name="in_proj",
    )(x2d, norm_w.reshape(1, d), sc, sh, w_main, w_dt)


_CONV_HALO = 16


_CONV_ROWS = 64


def _conv_shift_matrix(rc):
    win = rc + 2 * _CONV_HALO
    s = np.zeros((rc, CONV_K * win), np.float32)
    for k in range(CONV_K):
        for l in range(rc):
            s[l, k * win + _CONV_HALO + l + k - CONV_K // 2] = 1.0
    return jnp.asarray(s, BF16)


def _conv_kernel(p_ref, w_ref, b_ref, s_ref, o_ref, *, n, rc):
    tc = o_ref.shape[-1]
    w = w_ref[...].astype(BF16)
    bias = b_ref[...]
    smat = s_ref[...]
    zeros = jnp.zeros((_CONV_HALO, tc), BF16)
    for r0 in range(0, n, rc):
        top = zeros if r0 == 0 else p_ref[r0 - _CONV_HALO:r0, :]
        bot = zeros if r0 + rc >= n else p_ref[r0 + rc:r0 + rc + _CONV_HALO, :]
        window = jnp.concatenate([top, p_ref[r0:r0 + rc, :], bot], axis=0)
        taps = jnp.concatenate([window * w[k:k + 1, :] for k in range(CONV_K)], axis=0)
        acc = jnp.dot(smat, taps, preferred_element_type=F32) + bias
        o_ref[r0:r0 + rc, :] = _silu(acc).astype(o_ref.dtype)


def _conv_silu(p3d, conv_w, conv_b, n_ch):
    bsz, n, _ = p3d.shape
    tc = 512
    rc = min(_CONV_ROWS, n)
    smat = _conv_shift_matrix(rc)
    return pl.pallas_call(
        functools.partial(_conv_kernel, n=n, rc=rc),
        grid=(bsz, n_ch // tc),
        in_specs=[pl.BlockSpec((None, n, tc), lambda b, j: (b, 0, j)),
                  pl.BlockSpec((CONV_K, tc), lambda b, j: (0, j)),
                  pl.BlockSpec((1, tc), lambda b, j: (0, j)),
                  pl.BlockSpec(smat.shape, lambda b, j: (0, 0))],
        out_specs=pl.BlockSpec((None, n, tc), lambda b, j: (b, 0, j)),
        out_shape=jax.ShapeDtypeStruct((bsz, n, n_ch), BF16),
        compiler_params=_cparams("arbitrary", "arbitrary"),
        name="conv_silu",
    )(p3d, conv_w, conv_b.reshape(1, -1), smat)


LOG2E = 1.4426950408889634
SSD_SUB = 8


def _head_expand_matrix():
    e = np.zeros((LANES, N_HEADS * HEAD_DIM), np.float32)
    for h in range(N_HEADS):
        e[h, h * HEAD_DIM:(h + 1) * HEAD_DIM] = 1.0
    return jnp.asarray(e, BF16)


def _ssd_chunk_terms(dt_raw, dt_bias, a_log, expand, tri, fwd):
    q = CHUNK
    dt = _softplus(dt_raw + dt_bias)
    da = dt * (-jnp.exp(a_log))
    hi = da.astype(BF16)
    r1 = da - hi.astype(F32)
    mid = r1.astype(BF16)
    lo = (r1 - mid.astype(F32)).astype(BF16)
    ones_tri = jnp.where(tri, 1.0, 0.0).astype(BF16)
    cs3 = jnp.dot(ones_tri, jnp.concatenate([hi, mid, lo], axis=1), preferred_element_type=F32)
    cs = cs3[:, :LANES] + cs3[:, LANES:2 * LANES] + cs3[:, 2 * LANES:]
    tot = jnp.where(fwd, cs[q - 1:q, :], cs[0:1, :])
    dte = dt * jnp.exp(tot - cs)
    dec = jnp.exp(tot)
    dec_hi = dec.astype(BF16)
    dec_lo = (dec - dec_hi.astype(F32)).astype(BF16)
    dec2 = jnp.concatenate([dec_hi, dec_lo, jnp.zeros((6, LANES), BF16)], axis=0)
    stacked = jnp.concatenate([jnp.exp(cs).astype(BF16), dte.astype(BF16), dec2], axis=0)
    spread = jnp.dot(stacked, expand, preferred_element_type=F32)
    ecs_e = spread[:q]
    dte_e = spread[q:2 * q].astype(BF16)
    dec_e = jnp.broadcast_to(spread[2 * q:2 * q + 1] + spread[2 * q + 1:2 * q + 2], (8, spread.shape[1]))
    cs2 = cs * LOG2E
    return cs2, cs2.T, dt.T, ecs_e, dte_e, dec_e


def _ssd_kernel(*refs, with_y, nc, n_sub):
    if with_y:
        (x_ref, b_ref, c_ref, dt_ref, dtn_ref, dtb_ref, alog_ref, exp_ref, h0_ref, y_ref,
         s_scr, cs_scr, dee_scr, dce_scr, ecs_scr) = refs
    else:
        (x_ref, b_ref, dt_ref, dtn_ref, dtb_ref, alog_ref, exp_ref, h0_ref, hout_ref,
         s_scr, cs_scr, dee_scr, dce_scr) = refs
    q = CHUNK
    gw = HEADS_PER_GROUP * HEAD_DIM
    d = pl.program_id(1)
    c = pl.program_id(2)
    row = lax.broadcasted_iota(jnp.int32, (q, q), 0)
    col = lax.broadcasted_iota(jnp.int32, (q, q), 1)
    fwd = d == 0
    tri = jnp.where(fwd, row - col, col - row) >= 0
    first_half = col < HEAD_DIM

    def store_terms(slot, terms):
        cs2, cs2_t, dt_t, ecs_e, dte_e, dec_e = terms
        cs_scr[slot, 0] = cs2
        cs_scr[slot, 1] = cs2_t
        cs_scr[slot, 2] = dt_t
        dee_scr[slot] = dte_e
        dce_scr[slot] = dec_e
        if with_y:
            ecs_scr[slot] = ecs_e

    def terms_of(ref, sub):
        rows = pl.ds(pl.multiple_of(sub * q, q), q)
        return _ssd_chunk_terms(ref[rows, :], dtb_ref[...], alog_ref[...], exp_ref[...], tri, fwd)

    def sub_of(j):
        return jnp.where(fwd, j, n_sub - 1 - j)

    @pl.when(c == 0)
    def _():
        s_scr[...] = h0_ref[...]
        store_terms(0, terms_of(dt_ref, sub_of(0)))

    for j in range(n_sub):
        slot = j % 2
        rows = pl.ds(pl.multiple_of(sub_of(j) * q, q), q)
        cs2 = cs_scr[slot, 0]
        cs2_t = cs_scr[slot, 1]
        dt_t = cs_scr[slot, 2]
        if j + 1 < n_sub:
            store_terms(1 - slot, terms_of(dt_ref, sub_of(j + 1)))
        else:
            store_terms(1 - slot, terms_of(dtn_ref, sub_of(0)))

        for g in range(N_GROUPS):
            cols = slice(g * gw, (g + 1) * gw)
            bg = b_ref[rows, g * D_STATE:(g + 1) * D_STATE]
            xg = x_ref[rows, cols]
            s_old = s_scr[g]
            if with_y:
                cg = c_ref[rows, g * D_STATE:(g + 1) * D_STATE]
                cb = lax.dot_general(cg, bg, (((1,), (1,)), ((), ())), preferred_element_type=F32)
                y_off = jnp.dot(cg, s_old.astype(BF16), preferred_element_type=F32)
                for k in range(HEADS_PER_GROUP // 2):
                    pair = g * (HEADS_PER_GROUP // 2) + k
                    lanes = slice(pair * LANES, (pair + 1) * LANES)
                    xp = xg[:, k * LANES:(k + 1) * LANES]
                    xz = jnp.zeros_like(xp)
                    lhs = []
                    for hh in (2 * pair, 2 * pair + 1):
                        colb = jnp.broadcast_to(cs2[:, hh:hh + 1], (q, q))
                        rowb = cs2_t[hh:hh + 1, :]
                        decay = jnp.exp2(jnp.where(tri, colb - rowb, -jnp.inf))
                        lhs.append((cb * decay * dt_t[hh:hh + 1, :]).astype(BF16))
                    y_diag = jnp.dot(jnp.concatenate(lhs, axis=1),
                                     jnp.concatenate([jnp.where(first_half, xp, xz), jnp.where(first_half, xz, xp)],
                                                     axis=0),
                                     preferred_element_type=F32)
                    y = y_diag + ecs_scr[slot, :, lanes] * y_off[:, k * LANES:(k + 1) * LANES]
                    y_ref[rows, lanes] = y.astype(y_ref.dtype)
            xdte = xg * dee_scr[slot, :, cols]
            upd = lax.dot_general(bg, xdte, (((0,), (0,)), ((), ())), preferred_element_type=F32)
            s_scr[g] = s_old * dce_scr[slot, 0:1, cols] + upd

    if not with_y:
        @pl.when(c == nc - 1)
        def _():
            hout_ref[...] = s_scr[...]


def _ssd(xbc, dt_raw, dt_bias2, a_log2, h0, with_y):
    bsz, n, _ = xbc.shape
    n_sub = min(SSD_SUB, n // CHUNK)
    assert n_sub % 2 == 0 and n % (n_sub * CHUNK) == 0
    blk = n_sub * CHUNK
    nc = n // blk
    d_inner = N_HEADS * HEAD_DIM
    gw = HEADS_PER_GROUP * HEAD_DIM
    x_blk = d_inner // BC_WIDTH

    def ceff(dd, cc):
        return cc + dd * (nc - 1 - 2 * cc)

    def cnext(dd, cc):
        return ceff(dd, jnp.minimum(cc + 1, nc - 1))

    in_specs = [pl.BlockSpec((None, blk, d_inner), lambda b, dd, cc: (b, ceff(dd, cc), 0)),
                pl.BlockSpec((None, blk, BC_WIDTH), lambda b, dd, cc: (b, ceff(dd, cc), x_blk + dd))]
    args = [xbc, xbc]
    if with_y:
        in_specs.append(pl.BlockSpec((None, blk, BC_WIDTH), lambda b, dd, cc: (b, ceff(dd, cc), x_blk + 2 + dd)))
        args.append(xbc)
    in_specs += [pl.BlockSpec((None, blk, LANES), lambda b, dd, cc: (b, ceff(dd, cc), dd)),
                 pl.BlockSpec((None, blk, LANES), lambda b, dd, cc: (b, cnext(dd, cc), dd)),
                 pl.BlockSpec((None, 1, LANES), lambda b, dd, cc: (dd, 0, 0)),
                 pl.BlockSpec((None, 1, LANES), lambda b, dd, cc: (dd, 0, 0)),
                 pl.BlockSpec((LANES, d_inner), lambda b, dd, cc: (0, 0)),
                 pl.BlockSpec((None, None, N_GROUPS, D_STATE, gw), lambda b, dd, cc: (b, dd, 0, 0, 0))]
    args += [dt_raw, dt_raw, dt_bias2, a_log2, _head_expand_matrix(), h0]
    scratch = [pltpu.VMEM((N_GROUPS, D_STATE, gw), F32),
               pltpu.VMEM((2, 3, CHUNK, LANES), F32),
               pltpu.VMEM((2, CHUNK, d_inner), BF16),
               pltpu.VMEM((2, 8, d_inner), F32)]
    if with_y:
        out_specs = pl.BlockSpec((None, None, blk, d_inner), lambda b, dd, cc: (dd, b, ceff(dd, cc), 0))
        out_shape = jax.ShapeDtypeStruct((2, bsz, n, d_inner), BF16)
        scratch += [pltpu.VMEM((2, CHUNK, d_inner), F32)]
    else:
        out_specs = pl.BlockSpec((None, None, N_GROUPS, D_STATE, gw), lambda b, dd, cc: (b, dd, 0, 0, 0))
        out_shape = jax.ShapeDtypeStruct((bsz, 2, N_GROUPS, D_STATE, gw), F32)
    return pl.pallas_call(
        functools.partial(_ssd_kernel, with_y=with_y, nc=nc, n_sub=n_sub),
        grid=(bsz, 2, nc),
        in_specs=in_specs,
        out_specs=out_specs,
        out_shape=out_shape,
        scratch_shapes=scratch,
        compiler_params=_cparams("arbitrary", "arbitrary", "arbitrary"),
        name="ssd_scan" if with_y else "ssd_ctx_state",
    )(*args)


def _dft_mats(n):
    k = np.arange(n)
    ang = 2.0 * np.pi * ((k[:, None] * k[None, :]) % n) / n
    return np.cos(ang), np.sin(ang)


def _strided_rows(scr, t, start, count, stride, width):
    return jnp.concatenate([scr[t, lb, pl.ds(start, count, stride=stride), :] for lb in range(width // LANES)],
                           axis=1)


def _fnet_stage1_kernel(u_ref, chan_ref, m1_ref, twc_ref, tws_ref, o_ref, zs_scr, *, tcs):
    r = u_ref.shape[0]
    width = u_ref.shape[-1]
    u2 = u_ref[...].reshape(r * tcs, width)
    chan = chan_ref[...]
    for g in range(F_GROUPS):
        pq = jnp.dot(u2[:, g * F_GROUP_DIM:(g + 1) * F_GROUP_DIM], chan, preferred_element_type=F32)
        zs_scr[0, g] = pq[:, :F_GROUP_DIM]
        zs_scr[1, g] = pq[:, F_GROUP_DIM:]
    m1 = m1_ref[...]
    for j in range(tcs):
        zz = jnp.concatenate([_strided_rows(zs_scr, 0, j, r, tcs, width),
                              _strided_rows(zs_scr, 1, j, r, tcs, width)], axis=0).astype(BF16)
        u = jnp.dot(m1, zz, preferred_element_type=F32)
        ur = u[:r]
        ui = u[r:]
        tc = twc_ref[j]
        ts = tws_ref[j]
        o_ref[0, j] = (ur * tc + ui * ts).astype(o_ref.dtype)
        o_ref[1, j] = (ui * tc - ur * ts).astype(o_ref.dtype)


def _fnet_stage2_kernel(u_ref, k2_ref, o_ref):
    c, tbs, width = o_ref.shape
    uu = jnp.concatenate([u_ref[0].reshape(c * tbs, width), u_ref[1].reshape(c * tbs, width)], axis=0)
    out = jnp.dot(k2_ref[...], uu, preferred_element_type=F32)
    o_ref[...] = out.reshape(c, tbs, width).astype(o_ref.dtype)


def _fnet_mix(p2d, col_blk, bsz, n):
    m = p2d.shape[0]
    width = F_GROUPS * F_GROUP_DIM
    cgrid = GRID_W
    rgrid = n // cgrid
    scale = 1.0 / math.sqrt(n * F_GROUP_DIM)

    cc, sc = _dft_mats(F_GROUP_DIM)
    chan = jnp.asarray(np.concatenate([cc, sc], axis=1) * scale, BF16)

    cr, sr = _dft_mats(rgrid)
    m1 = jnp.asarray(np.block([[cr, -sr], [-sr, -cr]]), BF16)
    bb = np.arange(rgrid)[None, :]
    ci = np.arange(cgrid)[:, None]
    ang = 2.0 * np.pi * ((ci * bb) % n) / n
    twc = jnp.asarray(np.cos(ang)[:, :, None], F32)
    tws = jnp.asarray(np.sin(ang)[:, :, None], F32)
    tcs = 16
    p4 = p2d.reshape(bsz, rgrid, cgrid, p2d.shape[-1])
    u5 = pl.pallas_call(
        functools.partial(_fnet_stage1_kernel, tcs=tcs),
        grid=(bsz, cgrid // tcs),
        in_specs=[pl.BlockSpec((None, rgrid, tcs, width), lambda b, j: (b, 0, j, col_blk)),
                  pl.BlockSpec((F_GROUP_DIM, 2 * F_GROUP_DIM), lambda b, j: (0, 0)),
                  pl.BlockSpec((2 * rgrid, 2 * rgrid), lambda b, j: (0, 0)),
                  pl.BlockSpec((tcs, rgrid, 1), lambda b, j: (j, 0, 0)),
                  pl.BlockSpec((tcs, rgrid, 1), lambda b, j: (j, 0, 0))],
        out_specs=pl.BlockSpec((2, None, tcs, rgrid, width), lambda b, j: (0, b, j, 0, 0)),
        out_shape=jax.ShapeDtypeStruct((2, bsz, cgrid, rgrid, width), BF16),
        scratch_shapes=[pltpu.VMEM((2, width // LANES, rgrid * tcs, LANES), F32)],
        compiler_params=_cparams("arbitrary", "arbitrary"),
        name="fnet_stage1",
    )(p4, chan, m1, twc, tws)

    cc2, sc2 = _dft_mats(cgrid)
    tbs = min(16, rgrid)
    eye = np.eye(tbs)
    k2 = jnp.asarray(np.concatenate([np.kron(cc2, eye), np.kron(sc2, eye)], axis=1), BF16)
    out = pl.pallas_call(
        _fnet_stage2_kernel,
        grid=(bsz, rgrid // tbs),
        in_specs=[pl.BlockSpec((2, None, cgrid, tbs, width), lambda b, j: (0, b, 0, j, 0)),
                  pl.BlockSpec(k2.shape, lambda b, j: (0, 0))],
        out_specs=pl.BlockSpec((None, cgrid, tbs, width), lambda b, j: (b, 0, j, 0)),
        out_shape=jax.ShapeDtypeStruct((bsz, cgrid, rgrid, width), BF16),
        compiler_params=_cparams("arbitrary", "arbitrary"),
        name="fnet_stage2",
    )(u5, k2)
    return out.reshape(m, width)


def _tail_kernel(yf_ref, yb_ref, xs_ref, z_ref, ff_ref, gf_ref, gs_ref, x_ref, g1_ref, sc2_ref, sh2_ref,
                 dsk_ref, snw_ref, wso_ref, wfo_ref, wo_ref, n2w_ref, wr_ref, br_ref,
                 x1_ref, h2_ref, ti_ref, tw_ref, rk_ref, cnt_ref, carry_scr, h2_scr):
    i = pl.program_id(0)

    @pl.when(i == 0)
    def _():
        carry_scr[...] = jnp.zeros(carry_scr.shape, F32)
        h2_scr[...] = jnp.zeros(h2_scr.shape, F32)

    h2p = h2_scr[...]
    h_hi = h2p.astype(BF16)
    h_lo = (h2p - h_hi.astype(F32)).astype(BF16)
    wr = wr_ref[...]
    w_hi = wr.astype(BF16)
    w_lo = (wr - w_hi.astype(F32)).astype(BF16)
    logits = (jnp.dot(h_hi, w_hi, preferred_element_type=F32)
              + jnp.dot(h_hi, w_lo, preferred_element_type=F32)
              + jnp.dot(h_lo, w_hi, preferred_element_type=F32)) + br_ref[...]
    tm = logits.shape[0]
    lane = lax.broadcasted_iota(jnp.int32, (tm, LANES), 1)
    vals, idxs = [], []
    cur = logits
    for _ in range(TOP_K):
        mx = jnp.max(cur, axis=-1, keepdims=True)
        ix = jnp.min(jnp.where(cur == mx, lane, LANES), axis=-1, keepdims=True)
        vals.append(mx)
        idxs.append(ix)
        cur = jnp.where(lane == ix, NEG_BIG * 2.0, cur)
    es = [jnp.exp(v - vals[0]) for v in vals]
    den = es[0] + es[1] + es[2] + es[3]
    ti = jnp.zeros((tm, LANES), jnp.int32)
    tw = jnp.zeros((tm, LANES), F32)
    for k in range(TOP_K):
        ti = jnp.where(lane == k, idxs[k], ti)
        tw = jnp.where(lane == k, es[k] / den, tw)
    ti_ref[...] = ti.T[:8, :]
    tw_ref[...] = tw

    onehots = [jnp.where(lane == ix, 1.0, 0.0) for ix in idxs]
    cnt = onehots[0] + onehots[1] + onehots[2] + onehots[3]
    r_i = lax.broadcasted_iota(jnp.int32, (tm, tm), 0)
    c_i = lax.broadcasted_iota(jnp.int32, (tm, tm), 1)
    earlier = jnp.where(r_i > c_i, 1.0, 0.0).astype(BF16)
    before = jnp.dot(earlier, cnt.astype(BF16), preferred_element_type=F32) + carry_scr[0:1, :]
    rk = jnp.zeros((tm, LANES), jnp.int32)
    for k in range(TOP_K):
        rank_k = jnp.sum(onehots[k] * before, axis=-1, keepdims=True)
        rk = jnp.where(lane == k, rank_k.astype(jnp.int32), rk)
    rk_ref[...] = rk.T[:8, :]
    real = jnp.where(i > 0, 1.0, 0.0)
    total = carry_scr[...] + real * jnp.sum(cnt, axis=0, keepdims=True)
    carry_scr[...] = total
    cnt_ref[...] = total

    y = (yf_ref[0] + yb_ref[0]).astype(F32) + dsk_ref[...] * xs_ref[...].astype(F32)
    g = y * _silu(z_ref[...]).astype(F32)
    inv = lax.rsqrt(jnp.mean(g * g, axis=-1, keepdims=True) + EPS)
    gn = (g * inv * snw_ref[...]).astype(BF16)
    y_ssd = jnp.dot(gn, wso_ref[...], preferred_element_type=F32)
    y_four = jnp.dot(ff_ref[...], wfo_ref[...], preferred_element_type=F32)
    t = _sigmoid(gf_ref[...]) * y_four.astype(BF16) + _sigmoid(gs_ref[...]) * y_ssd.astype(BF16)
    mix = jnp.dot(t, wo_ref[...], preferred_element_type=F32)
    x1 = x_ref[...] + g1_ref[...] * mix
    x1_ref[...] = x1
    inv2 = lax.rsqrt(jnp.mean(x1 * x1, axis=-1, keepdims=True) + EPS)
    h2 = (x1 * inv2 * n2w_ref[...]) * (1.0 + sc2_ref[...]) + sh2_ref[...]
    h2_ref[...] = _pack_bf16_pair(h2)
    h2_scr[...] = h2


def _tail(y2, xbc2d, p2d, ff, x2d, g1, sc2, sh2, dsk, snw, wso, wfo, wo, n2w, wr, br, rows_per_mod, tm):
    m, d = x2d.shape
    di = y2.shape[-1]
    zb = (2 * di) // di
    fb = (3 * di) // d
    steps = m // tm
    cur = lambda i: jnp.minimum(i, steps - 1)
    prev = lambda i: jnp.maximum(i - 1, 0)
    row = lambda i: (cur(i), 0)
    modrow = lambda i: ((cur(i) * tm) // rows_per_mod, 0, 0)
    const = lambda i: (0, 0)
    in_specs = [pl.BlockSpec((1, tm, di), lambda i: (0, cur(i), 0)),
                pl.BlockSpec((1, tm, di), lambda i: (1, cur(i), 0)),
                pl.BlockSpec((tm, di), row),
                pl.BlockSpec((tm, di), lambda i: (cur(i), zb)),
                pl.BlockSpec((tm, d), row),
                pl.BlockSpec((tm, d), lambda i: (cur(i), fb + 1)),
                pl.BlockSpec((tm, d), lambda i: (cur(i), fb + 2)),
                pl.BlockSpec((tm, d), row),
                pl.BlockSpec((None, 1, d), modrow),
                pl.BlockSpec((None, 1, d), modrow),
                pl.BlockSpec((None, 1, d), modrow),
                pl.BlockSpec((1, di), const),
                pl.BlockSpec((1, di), const),
                pl.BlockSpec((di, d), const),
                pl.BlockSpec((d, d), const),
                pl.BlockSpec((d, d), const),
                pl.BlockSpec((1, d), const),
                pl.BlockSpec((d, LANES), const),
                pl.BlockSpec((1, LANES), const)]
    out_specs = [pl.BlockSpec((tm, d), row), pl.BlockSpec((tm, d // 2), row),
                 pl.BlockSpec((8, tm), lambda i: (0, prev(i))), pl.BlockSpec((tm, LANES), lambda i: (prev(i), 0)),
                 pl.BlockSpec((8, tm), lambda i: (0, prev(i))),
                 pl.BlockSpec((8, LANES), const)]
    out_shape = [jax.ShapeDtypeStruct((m, d), F32), jax.ShapeDtypeStruct((m, d // 2), jnp.uint32),
                 jax.ShapeDtypeStruct((8, m), jnp.int32), jax.ShapeDtypeStruct((m, LANES), F32),
                 jax.ShapeDtypeStruct((8, m), jnp.int32), jax.ShapeDtypeStruct((8, LANES), F32)]
    return pl.pallas_call(
        _tail_kernel,
        grid=(steps + 1,),
        in_specs=in_specs,
        out_specs=out_specs,
        out_shape=out_shape,
        scratch_shapes=[pltpu.VMEM((8, LANES), F32), pltpu.VMEM((tm, d), F32)],
        compiler_params=_cparams("arbitrary"),
        name="tail",
    )(y2, y2, xbc2d, p2d, ff, p2d, p2d, x2d, g1, sc2, sh2, dsk, snw, wso, wfo, wo, n2w, wr, br)


def _gather_rows(src, idx):
    n_out = idx.shape[0]
    width = src.shape[1]
    win = SC_GATHER_WINDOW
    info = plsc.get_sparse_core_info()
    n_cores, n_workers = info.num_cores, info.num_cores * info.num_subcores
    per_worker = n_out // n_workers
    assert per_worker * n_workers == n_out and per_worker % (2 * win) == 0
    mesh = plsc.VectorSubcoreMesh(core_axis_name="core", subcore_axis_name="subcore")

    @functools.partial(
        pl.kernel, out_type=jax.ShapeDtypeStruct((n_out, width), src.dtype), mesh=mesh,
        scratch_types=[pltpu.VMEM((win,), jnp.int32), pltpu.VMEM((win,), jnp.int32),
                       pltpu.VMEM((win, width), src.dtype), pltpu.VMEM((win, width), src.dtype),
                       pltpu.SemaphoreType.DMA, pltpu.SemaphoreType.DMA,
                       pltpu.SemaphoreType.DMA, pltpu.SemaphoreType.DMA],
        name="gather_rows")
    def gather(src_hbm, idx_hbm, out_hbm, idx_a, idx_b, rows_a, rows_b, gsem_a, gsem_b, ssem_a, ssem_b):
        worker = lax.axis_index("subcore") * n_cores + lax.axis_index("core")
        base = worker * per_worker

        @pl.loop(0, per_worker, step=2 * win)
        def _(off):
            pltpu.sync_copy(idx_hbm.at[pl.ds(base + off, win)], idx_a)
            ga = pltpu.async_copy(src_hbm.at[idx_a], rows_a, gsem_a)
            pltpu.sync_copy(idx_hbm.at[pl.ds(base + off + win, win)], idx_b)
            gb = pltpu.async_copy(src_hbm.at[idx_b], rows_b, gsem_b)
            ga.wait()
            sa = pltpu.async_copy(rows_a, out_hbm.at[pl.ds(base + off, win)], ssem_a)
            gb.wait()
            sb = pltpu.async_copy(rows_b, out_hbm.at[pl.ds(base + off + win, win)], ssem_b)
            sa.wait()
            sb.wait()

    return gather(src, idx)


def _scatter_rows(src, dest, n_rows):
    m, width = src.shape
    win = SC_GATHER_WINDOW
    info = plsc.get_sparse_core_info()
    n_cores, n_workers = info.num_cores, info.num_cores * info.num_subcores
    per_worker = m // n_workers
    assert per_worker * n_workers == m and per_worker % win == 0
    mesh = plsc.VectorSubcoreMesh(core_axis_name="core", subcore_axis_name="subcore")

    @functools.partial(
        pl.kernel, out_type=jax.ShapeDtypeStruct((n_rows, width), src.dtype), mesh=mesh,
        scratch_types=[pltpu.VMEM((win, width), src.dtype)]
        + [pltpu.VMEM((win,), jnp.int32)] * TOP_K + [pltpu.SemaphoreType.DMA] * (TOP_K + 1),
        name="scatter_rows")
    def scatter(src_hbm, dest_hbm, out_hbm, rows_v, *rest):
        idx_v, sems, row_sem = rest[:TOP_K], rest[TOP_K:2 * TOP_K], rest[2 * TOP_K]
        worker = lax.axis_index("subcore") * n_cores + lax.axis_index("core")
        base = worker * per_worker

        @pl.loop(0, per_worker, step=win)
        def _(off):
            t0 = base + off
            loads = [pltpu.async_copy(src_hbm.at[pl.ds(t0, win)], rows_v, row_sem)]
            loads += [pltpu.async_copy(dest_hbm.at[pl.ds(k * m + t0, win)], idx_v[k], sems[k]) for k in range(TOP_K)]
            for ld in loads:
                ld.wait()
            copies = [pltpu.async_copy(rows_v, out_hbm.at[idx_v[k]], sems[k]) for k in range(TOP_K)]
            for cp in copies:
                cp.wait()

    return scatter(src, dest)


def _expert_kernel(te_ref, nu_ref, tv_ref, x_ref, wgu_ref, bgu_ref, wd_ref, bd_ref, o_ref, wgu_scr, wd_scr):
    i = pl.program_id(0)
    used = i < nu_ref[0]
    new_expert = (i == 0) | (te_ref[i] != te_ref[jnp.maximum(i - 1, 0)])

    @pl.when(used & new_expert)
    def _():
        wgu_scr[...] = wgu_ref[...].astype(BF16)
        wd_scr[...] = wd_ref[...].astype(BF16)

    @pl.when(used)
    def _():
        dff = wd_ref.shape[0]
        half = wgu_ref.shape[0] // 2
        rows = lax.broadcasted_iota(jnp.int32, x_ref.shape, 0)
        xa, xb = _unpack_bf16_pair(jnp.where(rows < tv_ref[i], x_ref[...], jnp.uint32(0)))
        gu = (jnp.dot(xa.astype(BF16), wgu_scr[:half, :], preferred_element_type=F32)
              + jnp.dot(xb.astype(BF16), wgu_scr[half:, :], preferred_element_type=F32)) + bgu_ref[...]
        gate = jnp.minimum(gu[:, :dff], SWIGLU_LIMIT)
        up = jnp.clip(gu[:, dff:], -SWIGLU_LIMIT, SWIGLU_LIMIT)
        act = (up + 1.0) * gate * _sigmoid(SWIGLU_ALPHA * gate)
        y = jnp.dot(act.astype(BF16), wd_scr[...], preferred_element_type=F32) + bd_ref[...]
        o_ref[...] = _pack_bf16_pair(y)

    @pl.when(jnp.logical_not(used))
    def _():
        o_ref[...] = jnp.zeros(o_ref.shape, o_ref.dtype)


def _experts(xs, tile_expert, n_used, tile_valid, wgu, bgu, wd, bd):
    rows, dh = xs.shape
    d = 2 * dh
    tm = MOE_TILE
    n_tiles = rows // tm
    dff2 = wgu.shape[-1]

    def tile(i, te, nu, tv):
        return (jnp.minimum(i, nu[0] - 1), 0)

    def wsel(i, te, nu, tv):
        return (te[jnp.minimum(i, nu[0] - 1)], 0, 0)

    grid_spec = pltpu.PrefetchScalarGridSpec(
        num_scalar_prefetch=3,
        grid=(n_tiles,),
        in_specs=[pl.BlockSpec((tm, dh), tile),
                  pl.BlockSpec((None, d, dff2), wsel),
                  pl.BlockSpec((None, 1, dff2), wsel),
                  pl.BlockSpec((None, dff2 // 2, d), wsel),
                  pl.BlockSpec((None, 1, d), wsel)],
        out_specs=pl.BlockSpec((tm, dh), lambda i, te, nu, tv: (i, 0)),
        scratch_shapes=[pltpu.VMEM((d, dff2), BF16), pltpu.VMEM((dff2 // 2, d), BF16)],
    )
    return pl.pallas_call(
        _expert_kernel,
        grid_spec=grid_spec,
        out_shape=jax.ShapeDtypeStruct((rows, dh), jnp.uint32),
        compiler_params=_cparams("arbitrary"),
        name="experts",
    )(tile_expert, n_used, tile_valid, xs, wgu, bgu, wd, bd)


def _final_kernel(x1_ref, ya_ref, yb_ref, yc_ref, yd_ref, tw_ref, g2_ref, fw_ref, o_ref):
    d = x1_ref.shape[-1]
    half = d // 2
    tw = tw_ref[...]
    acc_hi = jnp.zeros((x1_ref.shape[0], half), F32)
    acc_lo = jnp.zeros((x1_ref.shape[0], half), F32)
    for k, y_ref in enumerate((ya_ref, yb_ref, yc_ref, yd_ref)):
        y_hi, y_lo = _unpack_bf16_pair(y_ref[...])
        acc_hi = acc_hi + tw[:, k:k + 1] * y_hi
        acc_lo = acc_lo + tw[:, k:k + 1] * y_lo
    x_hi = x1_ref[:, :half] + g2_ref[:, :half] * acc_hi
    x_lo = x1_ref[:, half:] + g2_ref[:, half:] * acc_lo
    ms = (jnp.sum(x_hi * x_hi, axis=-1, keepdims=True) + jnp.sum(x_lo * x_lo, axis=-1, keepdims=True)) / d
    inv = lax.rsqrt(ms + EPS)
    o_ref[:, :half] = x_hi * inv * fw_ref[:, :half]
    o_ref[:, half:] = x_lo * inv * fw_ref[:, half:]


def _final(x1, y4, tw, g2, fw, rows_per_mod, tm):
    m, d = x1.shape
    return pl.pallas_call(
        _final_kernel,
        grid=(m // tm,),
        in_specs=[pl.BlockSpec((tm, d), lambda i: (i, 0))]
        + [pl.BlockSpec((None, tm, d // 2), functools.partial(lambda k, i: (k, i, 0), k)) for k in range(TOP_K)]
        + [pl.BlockSpec((tm, LANES), lambda i: (i, 0)),
           pl.BlockSpec((None, 1, d), lambda i: ((i * tm) // rows_per_mod, 0, 0)),
           pl.BlockSpec((1, d), lambda i: (0, 0))],
        out_specs=pl.BlockSpec((tm, d), lambda i: (i, 0)),
        out_shape=jax.ShapeDtypeStruct((m, d), F32),
        compiler_params=_cparams("arbitrary"),
        name="final",
    )(x1, y4, y4, y4, y4, tw, g2, fw.reshape(1, d))


def _dispatch_plan(e, rank, counts, tm):
    n_assign = e.size
    padded = (counts + tm - 1) // tm * tm
    pad_end = jnp.cumsum(padded)
    pad_start = pad_end - padded
    dest = rank.astype(jnp.int32)
    for j in range(N_EXPERTS - 1):
        dest = dest + jnp.where(e > j, padded[j], 0).astype(jnp.int32)
    n_tiles = n_assign // tm + N_EXPERTS
    tile_start = jnp.arange(n_tiles, dtype=jnp.int32) * tm
    tile_expert = jnp.minimum(jnp.sum(pad_end[None, :] <= tile_start[:, None], axis=1), N_EXPERTS - 1).astype(jnp.int32)
    n_used = (pad_end[-1] // tm).astype(jnp.int32).reshape(1)
    tile_valid = jnp.clip(pad_start[tile_expert] + counts[tile_expert] - tile_start, 0, tm).astype(jnp.int32)
    return dest, tile_expert, n_used, tile_valid


def kernel(x, c, ctx, c_ctx, w_mod, b_mod, norm1_w, norm2_w, w_in, conv_w, conv_b, dt_bias, a_log, d_skip,
           ssd_norm_w, w_ssd_out, w_four_out, w_o, w_router, b_router, w_gate_up, b_gate_up, w_down, b_down,
           final_norm_w):
    bsz, n, d = x.shape
    n_ctx = ctx.shape[1]
    depth = w_mod.shape[0]
    assert depth == 1, "a stacked model would also need the context stream's residual update"
    d_inner = N_HEADS * HEAD_DIM
    off_dt = d_inner + 2 * BC_WIDTH
    m = bsz * n
    x2d = x.reshape(m, d)
    tm = min(512, n)

    for layer in range(depth):
        rows = -(-(bsz + 1) // 8) * 8
        c_rows = jnp.zeros((rows, d), F32).at[:bsz].set(c).at[bsz].set(c_ctx)
        mod = _modulation(c_rows, w_mod[layer], b_mod[layer])
        mods = [mod[:bsz, i * d:(i + 1) * d].reshape(bsz, 1, d) for i in range(N_MOD)]
        sh1, sc1, g1, sh2, sc2, g2 = mods
        sh1_c = mod[bsz:bsz + 1, :d].reshape(1, 1, d)
        sc1_c = mod[bsz:bsz + 1, d:2 * d].reshape(1, 1, d)

        wl = w_in[layer]
        w_main = jnp.concatenate([wl[:, :off_dt], wl[:, off_dt + 2 * N_HEADS:]], axis=1).astype(BF16)
        w_dt = jnp.zeros((d, 2 * LANES), F32)
        w_dt = w_dt.at[:, :N_HEADS].set(wl[:, off_dt:off_dt + N_HEADS])
        w_dt = w_dt.at[:, LANES:LANES + N_HEADS].set(wl[:, off_dt + N_HEADS:off_dt + 2 * N_HEADS]).astype(BF16)
        pad = jnp.zeros((2, 1, LANES - N_HEADS), F32)
        dt_bias2 = jnp.concatenate([dt_bias[layer].reshape(2, 1, N_HEADS), pad], axis=-1)
        a_log2 = jnp.concatenate([a_log[layer].reshape(2, 1, N_HEADS), pad], axis=-1)

        w_ctx = w_main[:, :off_dt]
        pc, dtc = _in_proj(ctx.reshape(bsz * n_ctx, d), norm1_w[layer], sc1_c, sh1_c, w_ctx, w_dt,
                           bsz * n_ctx, min(512, n_ctx))
        xbc_c = _conv_silu(pc.reshape(bsz, n_ctx, off_dt), conv_w[layer][:, :off_dt], conv_b[layer][:off_dt], off_dt)
        h0 = jnp.zeros((bsz, 2, N_GROUPS, D_STATE, HEADS_PER_GROUP * HEAD_DIM), F32)
        h_ctx = _ssd(xbc_c, dtc.reshape(bsz, n_ctx, 2 * LANES), dt_bias2, a_log2, h0, with_y=False)

        p, dtl = _in_proj(x2d, norm1_w[layer], sc1, sh1, w_main, w_dt, n, tm)
        n_conv = off_dt + 2 * BC_WIDTH
        xbc = _conv_silu(p.reshape(bsz, n, -1), conv_w[layer], conv_b[layer], n_conv)
        y2 = _ssd(xbc, dtl.reshape(bsz, n, 2 * LANES), dt_bias2, a_log2, h_ctx, with_y=True)
        ff = _fnet_mix(p, (n_conv + d_inner) // d, bsz, n)

        dsk = jnp.repeat(d_skip[layer].astype(F32), HEAD_DIM).reshape(1, d_inner)
        wr = jnp.zeros((d, LANES), F32).at[:, :N_EXPERTS].set(w_router[layer])
        br = jnp.full((1, LANES), NEG_BIG, F32).at[0, :N_EXPERTS].set(b_router[layer])
        x1, h2, ti, tw, rk, cnt = _tail(
            y2.reshape(2, m, d_inner), xbc.reshape(m, n_conv), p, ff, x2d, g1, sc2, sh2, dsk,
            ssd_norm_w[layer].reshape(1, d_inner), w_ssd_out[layer].astype(BF16), w_four_out[layer].astype(BF16),
            w_o[layer].astype(BF16), norm2_w[layer].reshape(1, d), wr, br, n, min(TAIL_ROWS, n))

        counts = cnt[0, :N_EXPERTS].astype(jnp.int32)
        dest, tile_expert, n_used, tile_valid = _dispatch_plan(ti[:TOP_K], rk[:TOP_K], counts, MOE_TILE)
        dest_km = dest.reshape(-1)
        xs = _scatter_rows(h2, dest_km, tile_expert.shape[0] * MOE_TILE)
        ys = _experts(xs, tile_expert, n_used, tile_valid, w_gate_up[layer],
                      b_gate_up[layer].reshape(N_EXPERTS, 1, -1), w_down[layer],
                      b_down[layer].reshape(N_EXPERTS, 1, -1))
        y4 = _gather_rows(ys, dest_km).reshape(TOP_K, m, d // 2)
        x2d = _final(x1, y4, tw, g2, final_norm_w, n, tm)
    return x2d.reshape(bsz, n, d)
```

```python
import functools
import math

import numpy as np
import jax
import jax.numpy as jnp
from jax import lax
from jax.experimental import pallas as pl
from jax.experimental.pallas import tpu as pltpu
from jax.experimental.pallas import tpu_sc as plsc

F32 = jnp.float32
BF16 = jnp.bfloat16

EPS = 1e-6
GRID_W = 64
N_MOD = 6
F_GROUPS = 8
F_GROUP_DIM = 128
HEAD_DIM = 64
N_HEADS = 32
N_GROUPS = 4
HEADS_PER_GROUP = N_HEADS // N_GROUPS
D_STATE = 128
BC_WIDTH = N_GROUPS * D_STATE
CONV_K = 5
CHUNK = 128
N_EXPERTS = 32
TOP_K = 4
SWIGLU_LIMIT = 7.0
SWIGLU_ALPHA = 1.702

LANES = 128
VMEM_LIMIT_BYTES = 56 * 1024 * 1024
NEG_BIG = -1e30
MOE_TILE = 512
TAIL_ROWS = 512
SC_GATHER_WINDOW = 64


def _cparams(*sem):
    return pltpu.CompilerParams(dimension_semantics=sem, vmem_limit_bytes=VMEM_LIMIT_BYTES)


def _sigmoid(v):
    return 1.0 / (1.0 + jnp.exp(-v))


def _silu(v):
    return v * _sigmoid(v)


def _softplus(v):
    return jnp.maximum(v, 0.0) + jnp.log(1.0 + jnp.exp(-jnp.abs(v)))


def _pack_bf16_pair(v):
    w = v.shape[1] // 2
    bits = lax.bitcast_convert_type(v.astype(BF16).astype(F32), jnp.uint32)
    return bits[:, :w] | (bits[:, w:] >> 16)


def _unpack_bf16_pair(p):
    hi = lax.bitcast_convert_type(p & jnp.uint32(0xFFFF0000), F32)
    lo = lax.bitcast_convert_type(p << 16, F32)
    return hi, lo


def _mod_kernel(c_ref, w_ref, b_ref, o_ref):
    s = _silu(c_ref[...]).astype(BF16)
    o_ref[...] = jnp.dot(s, w_ref[...].astype(BF16), preferred_element_type=F32) + b_ref[...]


def _modulation(c_rows, w_mod, b_mod):
    rows, d = c_rows.shape
    n_out = w_mod.shape[1]
    tn = 1024
    return pl.pallas_call(
        _mod_kernel,
        grid=(n_out // tn,),
        in_specs=[pl.BlockSpec((rows, d), lambda j: (0, 0)),
                  pl.BlockSpec((d, tn), lambda j: (0, j)),
                  pl.BlockSpec((1, tn), lambda j: (0, j))],
        out_specs=pl.BlockSpec((rows, tn), lambda j: (0, j)),
        out_shape=jax.ShapeDtypeStruct((rows, n_out), F32),
        compiler_params=_cparams("arbitrary"),
        name="modulation",
    )(c_rows, w_mod, b_mod.reshape(1, n_out))


_INPROJ_TN = 1024


def _inproj_kernel(x_ref, nw_ref, sc_ref, sh_ref, w_ref, wdt_ref, p_ref, dt_ref):
    x = x_ref[...]
    inv = lax.rsqrt(jnp.mean(x * x, axis=-1, keepdims=True) + EPS)
    h = (x * inv * nw_ref[...]) * (1.0 + sc_ref[...]) + sh_ref[...]
    hb = h.astype(BF16)
    dt_ref[...] = jnp.dot(hb, wdt_ref[...], preferred_element_type=F32)
    for j in range(w_ref.shape[1] // _INPROJ_TN):
        cols = slice(j * _INPROJ_TN, (j + 1) * _INPROJ_TN)
        p_ref[:, cols] = jnp.dot(hb, w_ref[:, cols], preferred_element_type=F32).astype(p_ref.dtype)


def _in_proj(x2d, norm_w, sc, sh, w_main, w_dt, rows_per_mod, tm):
    m, d = x2d.shape
    n = w_main.shape[1]
    ndt = w_dt.shape[1]
    resident = dict(pipeline_mode=pl.Buffered(1))
    return pl.pallas_call(
        _inproj_kernel,
        grid=(m // tm,),
        in_specs=[pl.BlockSpec((tm, d), lambda i: (i, 0)),
                  pl.BlockSpec((1, d), lambda i: (0, 0), **resident),
                  pl.BlockSpec((None, 1, d), lambda i: ((i * tm) // rows_per_mod, 0, 0)),
                  pl.BlockSpec((None, 1, d), lambda i: ((i * tm) // rows_per_mod, 0, 0)),
                  pl.BlockSpec((d, n), lambda i: (0, 0), **resident),
                  pl.BlockSpec((d, ndt), lambda i: (0, 0), **resident)],
        out_specs=[pl.BlockSpec((tm, n), lambda i: (i, 0)),
                   pl.BlockSpec((tm, ndt), lambda i: (i, 0))],
        out_shape=[jax.ShapeDtypeStruct((m, n), BF16),
                   jax.ShapeDtypeStruct((m, ndt), F32)],
        compiler_params=_cparams("arbitrary"),
        name="in_proj",
    )(x2d, norm_w.reshape(1, d), sc, sh, w_main, w_dt)


_CONV_HALO = 16


_CONV_ROWS = 64


def _conv_shift_matrix(rc):
    win = rc + 2 * _CONV_HALO
    s = np.zeros((rc, CONV_K * win), np.float32)
    for k in range(CONV_K):
        for l in range(rc):
            s[l, k * win + _CONV_HALO + l + k - CONV_K // 2] = 1.0
    return jnp.asarray(s, BF16)


def _conv_kernel(p_ref, w_ref, b_ref, s_ref, o_ref, *, n, rc):
    tc = o_ref.shape[-1]
    w = w_ref[...].astype(BF16)
    bias = b_ref[...]
    smat = s_ref[...]
    zeros = jnp.zeros((_CONV_HALO, tc), BF16)
    for r0 in range(0, n, rc):
        top = zeros if r0 == 0 else p_ref[r0 - _CONV_HALO:r0, :]
        bot = zeros if r0 + rc >= n else p_ref[r0 + rc:r0 + rc + _CONV_HALO, :]
        window = jnp.concatenate([top, p_ref[r0:r0 + rc, :], bot], axis=0)
        taps = jnp.concatenate([window * w[k:k + 1, :] for k in range(CONV_K)], axis=0)
        acc = jnp.dot(smat, taps, preferred_element_type=F32) + bias
        o_ref[r0:r0 + rc, :] = _silu(acc).astype(o_ref.dtype)


def _conv_silu(p3d, conv_w, conv_b, n_ch):
    bsz, n, _ = p3d.shape
    tc = 512
    rc = min(_CONV_ROWS, n)
    smat = _conv_shift_matrix(rc)
    return pl.pallas_call(
        functools.partial(_conv_kernel, n=n, rc=rc),
        grid=(bsz, n_ch // tc),
        in_specs=[pl.BlockSpec((None, n, tc), lambda b, j: (b, 0, j)),
                  pl.BlockSpec((CONV_K, tc), lambda b, j: (0, j)),
                  pl.BlockSpec((1, tc), lambda b, j: (0, j)),
                  pl.BlockSpec(smat.shape, lambda b, j: (0, 0))],
        out_specs=pl.BlockSpec((None, n, tc), lambda b, j: (b, 0, j)),
        out_shape=jax.ShapeDtypeStruct((bsz, n, n_ch), BF16),
        compiler_params=_cparams("arbitrary", "arbitrary"),
        name="conv_silu",
    )(p3d, conv_w, conv_b.reshape(1, -1), smat)


LOG2E = 1.4426950408889634
SSD_SUB = 8


def _head_expand_matrix():
    e = np.zeros((LANES, N_HEADS * HEAD_DIM), np.float32)
    for h in range(N_HEADS):
        e[h, h * HEAD_DIM:(h + 1) * HEAD_DIM] = 1.0
    return jnp.asarray(e, BF16)


def _ssd_chunk_terms(dt_raw, dt_bias, a_log, expand, tri, fwd):
    q = CHUNK
    dt = _softplus(dt_raw + dt_bias)
    da = dt * (-jnp.exp(a_log))
    hi = da.astype(BF16)
    r1 = da - hi.astype(F32)
    mid = r1.astype(BF16)
    lo = (r1 - mid.astype(F32)).astype(BF16)
    ones_tri = jnp.where(tri, 1.0, 0.0).astype(BF16)
    cs3 = jnp.dot(ones_tri, jnp.concatenate([hi, mid, lo], axis=1), preferred_element_type=F32)
    cs = cs3[:, :LANES] + cs3[:, LANES:2 * LANES] + cs3[:, 2 * LANES:]
    tot = jnp.where(fwd, cs[q - 1:q, :], cs[0:1, :])
    dte = dt * jnp.exp(tot - cs)
    dec = jnp.exp(tot)
    dec_hi = dec.astype(BF16)
    dec_lo = (dec - dec_hi.astype(F32)).astype(BF16)
    dec2 = jnp.concatenate([dec_hi, dec_lo, jnp.zeros((6, LANES), BF16)], axis=0)
    stacked = jnp.concatenate([jnp.exp(cs).astype(BF16), dte.astype(BF16), dec2], axis=0)
    spread = jnp.dot(stacked, expand, preferred_element_type=F32)
    ecs_e = spread[:q]
    dte_e = spread[q:2 * q].astype(BF16)
    dec_e = jnp.broadcast_to(spread[2 * q:2 * q + 1] + spread[2 * q + 1:2 * q + 2], (8, spread.shape[1]))
    cs2 = cs * LOG2E
    return cs2, cs2.T, dt.T, ecs_e, dte_e, dec_e


def _ssd_kernel(*refs, with_y, nc, n_sub):
    if with_y:
        (x_ref, b_ref, c_ref, dt_ref, dtn_ref, dtb_ref, alog_ref, exp_ref, h0_ref, y_ref,
         s_scr, cs_scr, dee_scr, dce_scr, ecs_scr) = refs
    else:
        (x_ref, b_ref, dt_ref, dtn_ref, dtb_ref, alog_ref, exp_ref, h0_ref, hout_ref,
         s_scr, cs_scr, dee_scr, dce_scr) = refs
    q = CHUNK
    gw = HEADS_PER_GROUP * HEAD_DIM
    d = pl.program_id(1)
    c = pl.program_id(2)
    row = lax.broadcasted_iota(jnp.int32, (q, q), 0)
    col = lax.broadcasted_iota(jnp.int32, (q, q), 1)
    fwd = d == 0
    tri = jnp.where(fwd, row - col, col - row) >= 0
    first_half = col < HEAD_DIM

    def store_terms(slot, terms):
        cs2, cs2_t, dt_t, ecs_e, dte_e, dec_e = terms
        cs_scr[slot, 0] = cs2
        cs_scr[slot, 1] = cs2_t
        cs_scr[slot, 2] = dt_t
        dee_scr[slot] = dte_e
        dce_scr[slot] = dec_e
        if with_y:
            ecs_scr[slot] = ecs_e

    def terms_of(ref, sub):
        rows = pl.ds(pl.multiple_of(sub * q, q), q)
        return _ssd_chunk_terms(ref[rows, :], dtb_ref[...], alog_ref[...], exp_ref[...], tri, fwd)

    def sub_of(j):
        return jnp.where(fwd, j, n_sub - 1 - j)

    @pl.when(c == 0)
    def _():
        s_scr[...] = h0_ref[...]
        store_terms(0, terms_of(dt_ref, sub_of(0)))

    for j in range(n_sub):
        slot = j % 2
        rows = pl.ds(pl.multiple_of(sub_of(j) * q, q), q)
        cs2 = cs_scr[slot, 0]
        cs2_t = cs_scr[slot, 1]
        dt_t = cs_scr[slot, 2]
        if j + 1 < n_sub:
            store_terms(1 - slot, terms_of(dt_ref, sub_of(j + 1)))
        else:
            store_terms(1 - slot, terms_of(dtn_ref, sub_of(0)))

        for g in range(N_GROUPS):
            cols = slice(g * gw, (g + 1) * gw)
            bg = b_ref[rows, g * D_STATE:(g + 1) * D_STATE]
            xg = x_ref[rows, cols]
            s_old = s_scr[g]
            if with_y:
                cg = c_ref[rows, g * D_STATE:(g + 1) * D_STATE]
                cb = lax.dot_general(cg, bg, (((1,), (1,)), ((), ())), preferred_element_type=F32)
                y_off = jnp.dot(cg, s_old.astype(BF16), preferred_element_type=F32)
                for k in range(HEADS_PER_GROUP // 2):
                    pair = g * (HEADS_PER_GROUP // 2) + k
                    lanes = slice(pair * LANES, (pair + 1) * LANES)
                    xp = xg[:, k * LANES:(k + 1) * LANES]
                    xz = jnp.zeros_like(xp)
                    lhs = []
                    for hh in (2 * pair, 2 * pair + 1):
                        colb = jnp.broadcast_to(cs2[:, hh:hh + 1], (q, q))
                        rowb = cs2_t[hh:hh + 1, :]
                        decay = jnp.exp2(jnp.where(tri, colb - rowb, -jnp.inf))
                        lhs.append((cb * decay * dt_t[hh:hh + 1, :]).astype(BF16))
                    y_diag = jnp.dot(jnp.concatenate(lhs, axis=1),
                                     jnp.concatenate([jnp.where(first_half, xp, xz), jnp.where(first_half, xz, xp)],
                                                     axis=0),
                                     preferred_element_type=F32)
                    y = y_diag + ecs_scr[slot, :, lanes] * y_off[:, k * LANES:(k + 1) * LANES]
                    y_ref[rows, lanes] = y.astype(y_ref.dtype)
            xdte = xg * dee_scr[slot, :, cols]
            upd = lax.dot_general(bg, xdte, (((0,), (0,)), ((), ())), preferred_element_type=F32)
            s_scr[g] = s_old * dce_scr[slot, 0:1, cols] + upd

    if not with_y:
        @pl.when(c == nc - 1)
        def _():
            hout_ref[...] = s_scr[...]


def _ssd(xbc, dt_raw, dt_bias2, a_log2, h0, with_y):
    bsz, n, _ = xbc.shape
    n_sub = min(SSD_SUB, n // CHUNK)
    assert n_sub % 2 == 0 and n % (n_sub * CHUNK) == 0
    blk = n_sub * CHUNK
    nc = n // blk
    d_inner = N_HEADS * HEAD_DIM
    gw = HEADS_PER_GROUP * HEAD_DIM
    x_blk = d_inner // BC_WIDTH

    def ceff(dd, cc):
        return cc + dd * (nc - 1 - 2 * cc)

    def cnext(dd, cc):
        return ceff(dd, jnp.minimum(cc + 1, nc - 1))

    in_specs = [pl.BlockSpec((None, blk, d_inner), lambda b, dd, cc: (b, ceff(dd, cc), 0)),
                pl.BlockSpec((None, blk, BC_WIDTH), lambda b, dd, cc: (b, ceff(dd, cc), x_blk + dd))]
    args = [xbc, xbc]
    if with_y:
        in_specs.append(pl.BlockSpec((None, blk, BC_WIDTH), lambda b, dd, cc: (b, ceff(dd, cc), x_blk + 2 + dd)))
        args.append(xbc)
    in_specs += [pl.BlockSpec((None, blk, LANES), lambda b, dd, cc: (b, ceff(dd, cc), dd)),
                 pl.BlockSpec((None, blk, LANES), lambda b, dd, cc: (b, cnext(dd, cc), dd)),
                 pl.BlockSpec((None, 1, LANES), lambda b, dd, cc: (dd, 0, 0)),
                 pl.BlockSpec((None, 1, LANES), lambda b, dd, cc: (dd, 0, 0)),
                 pl.BlockSpec((LANES, d_inner), lambda b, dd, cc: (0, 0)),
                 pl.BlockSpec((None, None, N_GROUPS, D_STATE, gw), lambda b, dd, cc: (b, dd, 0, 0, 0))]
    args += [dt_raw, dt_raw, dt_bias2, a_log2, _head_expand_matrix(), h0]
    scratch = [pltpu.VMEM((N_GROUPS, D_STATE, gw), F32),
               pltpu.VMEM((2, 3, CHUNK, LANES), F32),
               pltpu.VMEM((2, CHUNK, d_inner), BF16),
               pltpu.VMEM((2, 8, d_inner), F32)]
    if with_y:
        out_specs = pl.BlockSpec((None, None, blk, d_inner), lambda b, dd, cc: (dd, b, ceff(dd, cc), 0))
        out_shape = jax.ShapeDtypeStruct((2, bsz, n, d_inner), BF16)
        scratch += [pltpu.VMEM((2, CHUNK, d_inner), F32)]
    else:
        out_specs = pl.BlockSpec((None, None, N_GROUPS, D_STATE, gw), lambda b, dd, cc: (b, dd, 0, 0, 0))
        out_shape = jax.ShapeDtypeStruct((bsz, 2, N_GROUPS, D_STATE, gw), F32)
    return pl.pallas_call(
        functools.partial(_ssd_kernel, with_y=with_y, nc=nc, n_sub=n_sub),
        grid=(bsz, 2, nc),
        in_specs=in_specs,
        out_specs=out_specs,
        out_shape=out_shape,
        scratch_shapes=scratch,
        compiler_params=_cparams("arbitrary", "arbitrary", "arbitrary"),
        name="ssd_scan" if with_y else "ssd_ctx_state",
    )(*args)


def _dft_mats(n):
    k = np.arange(n)
    ang = 2.0 * np.pi * ((k[:, None] * k[None, :]) % n) / n
    return np.cos(ang), np.sin(ang)


def _strided_rows(scr, t, start, count, stride, width):
    return jnp.concatenate([scr[t, lb, pl.ds(start, count, stride=stride), :] for lb in range(width // LANES)],
                           axis=1)


def _fnet_stage1_kernel(u_ref, chan_ref, m1_ref, twc_ref, tws_ref, o_ref, zs_scr, *, tcs):
    r = u_ref.shape[0]
    width = u_ref.shape[-1]
    u2 = u_ref[...].reshape(r * tcs, width)
    chan = chan_ref[...]
    for g in range(F_GROUPS):
        pq = jnp.dot(u2[:, g * F_GROUP_DIM:(g + 1) * F_GROUP_DIM], chan, preferred_element_type=F32)
        zs_scr[0, g] = pq[:, :F_GROUP_DIM]
        zs_scr[1, g] = pq[:, F_GROUP_DIM:]
    m1 = m1_ref[...]
    for j in range(tcs):
        zz = jnp.concatenate([_strided_rows(zs_scr, 0, j, r, tcs, width),
                              _strided_rows(zs_scr, 1, j, r, tcs, width)], axis=0).astype(BF16)
        u = jnp.dot(m1, zz, preferred_element_type=F32)
        ur = u[:r]
        ui = u[r:]
        tc = twc_ref[j]
        ts = tws_ref[j]
        o_ref[0, j] = (ur * tc + ui * ts).astype(o_ref.dtype)
        o_ref[1, j] = (ui * tc - ur * ts).astype(o_ref.dtype)


def _fnet_stage2_kernel(u_ref, k2_ref, o_ref):
    c, tbs, width = o_ref.shape
    uu = jnp.concatenate([u_ref[0].reshape(c * tbs, width), u_ref[1].reshape(c * tbs, width)], axis=0)
    out = jnp.dot(k2_ref[...], uu, preferred_element_type=F32)
    o_ref[...] = out.reshape(c, tbs, width).astype(o_ref.dtype)


def _fnet_mix(p2d, col_blk, bsz, n):
    m = p2d.shape[0]
    width = F_GROUPS * F_GROUP_DIM
    cgrid = GRID_W
    rgrid = n // cgrid
    scale = 1.0 / math.sqrt(n * F_GROUP_DIM)

    cc, sc = _dft_mats(F_GROUP_DIM)
    chan = jnp.asarray(np.concatenate([cc, sc], axis=1) * scale, BF16)

    cr, sr = _dft_mats(rgrid)
    m1 = jnp.asarray(np.block([[cr, -sr], [-sr, -cr]]), BF16)
    bb = np.arange(rgrid)[None, :]
    ci = np.arange(cgrid)[:, None]
    ang = 2.0 * np.pi * ((ci * bb) % n) / n
    twc = jnp.asarray(np.cos(ang)[:, :, None], F32)
    tws = jnp.asarray(np.sin(ang)[:, :, None], F32)
    tcs = 16
    p4 = p2d.reshape(bsz, rgrid, cgrid, p2d.shape[-1])
    u5 = pl.pallas_call(
        functools.partial(_fnet_stage1_kernel, tcs=tcs),
        grid=(bsz, cgrid // tcs),
        in_specs=[pl.BlockSpec((None, rgrid, tcs, width), lambda b, j: (b, 0, j, col_blk)),
                  pl.BlockSpec((F_GROUP_DIM, 2 * F_GROUP_DIM), lambda b, j: (0, 0)),
                  pl.BlockSpec((2 * rgrid, 2 * rgrid), lambda b, j: (0, 0)),
                  pl.BlockSpec((tcs, rgrid, 1), lambda b, j: (j, 0, 0)),
                  pl.BlockSpec((tcs, rgrid, 1), lambda b, j: (j, 0, 0))],
        out_specs=pl.BlockSpec((2, None, tcs, rgrid, width), lambda b, j: (0, b, j, 0, 0)),
        out_shape=jax.ShapeDtypeStruct((2, bsz, cgrid, rgrid, width), BF16),
        scratch_shapes=[pltpu.VMEM((2, width // LANES, rgrid * tcs, LANES), F32)],
        compiler_params=_cparams("arbitrary", "arbitrary"),
        name="fnet_stage1",
    )(p4, chan, m1, twc, tws)

    cc2, sc2 = _dft_mats(cgrid)
    tbs = min(16, rgrid)
    eye = np.eye(tbs)
    k2 = jnp.asarray(np.concatenate([np.kron(cc2, eye), np.kron(sc2, eye)], axis=1), BF16)
    out = pl.pallas_call(
        _fnet_stage2_kernel,
        grid=(bsz, rgrid // tbs),
        in_specs=[pl.BlockSpec((2, None, cgrid, tbs, width), lambda b, j: (0, b, 0, j, 0)),
                  pl.BlockSpec(k2.shape, lambda b, j: (0, 0))],
        out_specs=pl.BlockSpec((None, cgrid, tbs, width), lambda b, j: (b, 0, j, 0)),
        out_shape=jax.ShapeDtypeStruct((bsz, cgrid, rgrid, width), BF16),
        compiler_params=_cparams("arbitrary", "arbitrary"),
        name="fnet_stage2",
    )(u5, k2)
    return out.reshape(m, width)


def _tail_kernel(yf_ref, yb_ref, xs_ref, z_ref, ff_ref, gf_ref, gs_ref, x_ref, g1_ref, sc2_ref, sh2_ref,
                 dsk_ref, snw_ref, wso_ref, wfo_ref, wo_ref, n2w_ref, wr_ref, br_ref,
                 x1_ref, h2_ref, ti_ref, tw_ref, rk_ref, cnt_ref, carry_scr):
    y = (yf_ref[0] + yb_ref[0]).astype(F32) + dsk_ref[...] * xs_ref[...].astype(F32)
    g = y * _silu(z_ref[...]).astype(F32)
    inv = lax.rsqrt(jnp.mean(g * g, axis=-1, keepdims=True) + EPS)
    gn = (g * inv * snw_ref[...]).astype(BF16)
    y_ssd = jnp.dot(gn, wso_ref[...], preferred_element_type=F32)
    y_four = jnp.dot(ff_ref[...], wfo_ref[...], preferred_element_type=F32)
    t = _sigmoid(gf_ref[...]) * y_four.astype(BF16) + _sigmoid(gs_ref[...]) * y_ssd.astype(BF16)
    mix = jnp.dot(t, wo_ref[...], preferred_element_type=F32)
    x1 = x_ref[...] + g1_ref[...] * mix
    x1_ref[...] = x1
    inv2 = lax.rsqrt(jnp.mean(x1 * x1, axis=-1, keepdims=True) + EPS)
    h2 = (x1 * inv2 * n2w_ref[...]) * (1.0 + sc2_ref[...]) + sh2_ref[...]
    h2_ref[...] = _pack_bf16_pair(h2)
    h_hi = h2.astype(BF16)
    h_lo = (h2 - h_hi.astype(F32)).astype(BF16)
    wr = wr_ref[...]
    w_hi = wr.astype(BF16)
    w_lo = (wr - w_hi.astype(F32)).astype(BF16)
    logits = (jnp.dot(h_hi, w_hi, preferred_element_type=F32)
              + jnp.dot(h_hi, w_lo, preferred_element_type=F32)
              + jnp.dot(h_lo, w_hi, preferred_element_type=F32)) + br_ref[...]
    tm = logits.shape[0]
    lane = lax.broadcasted_iota(jnp.int32, (tm, LANES), 1)
    vals, idxs = [], []
    cur = logits
    for _ in range(TOP_K):
        mx = jnp.max(cur, axis=-1, keepdims=True)
        ix = jnp.min(jnp.where(cur == mx, lane, LANES), axis=-1, keepdims=True)
        vals.append(mx)
        idxs.append(ix)
        cur = jnp.where(lane == ix, NEG_BIG * 2.0, cur)
    es = [jnp.exp(v - vals[0]) for v in vals]
    den = es[0] + es[1] + es[2] + es[3]
    ti = jnp.zeros((tm, LANES), jnp.int32)
    tw = jnp.zeros((tm, LANES), F32)
    for k in range(TOP_K):
        ti = jnp.where(lane == k, idxs[k], ti)
        tw = jnp.where(lane == k, es[k] / den, tw)
    ti_ref[...] = ti.T[:8, :]
    tw_ref[...] = tw

    @pl.when(pl.program_id(0) == 0)
    def _():
        carry_scr[...] = jnp.zeros(carry_scr.shape, F32)

    onehots = [jnp.where(lane == ix, 1.0, 0.0) for ix in idxs]
    cnt = onehots[0] + onehots[1] + onehots[2] + onehots[3]
    r_i = lax.broadcasted_iota(jnp.int32, (tm, tm), 0)
    c_i = lax.broadcasted_iota(jnp.int32, (tm, tm), 1)
    earlier = jnp.where(r_i > c_i, 1.0, 0.0).astype(BF16)
    before = jnp.dot(earlier, cnt.astype(BF16), preferred_element_type=F32) + carry_scr[0:1, :]
    rk = jnp.zeros((tm, LANES), jnp.int32)
    for k in range(TOP_K):
        rank_k = jnp.sum(onehots[k] * before, axis=-1, keepdims=True)
        rk = jnp.where(lane == k, rank_k.astype(jnp.int32), rk)
    rk_ref[...] = rk.T[:8, :]
    total = carry_scr[...] + jnp.sum(cnt, axis=0, keepdims=True)
    carry_scr[...] = total
    cnt_ref[...] = total


def _tail(y2, xbc2d, p2d, ff, x2d, g1, sc2, sh2, dsk, snw, wso, wfo, wo, n2w, wr, br, rows_per_mod, tm):
    m, d = x2d.shape
    di = y2.shape[-1]
    zb = (2 * di) // di
    fb = (3 * di) // d
    row = lambda i: (i, 0)
    modrow = lambda i: ((i * tm) // rows_per_mod, 0, 0)
    const = lambda i: (0, 0)
    in_specs = [pl.BlockSpec((1, tm, di), lambda i: (0, i, 0)),
                pl.BlockSpec((1, tm, di), lambda i: (1, i, 0)),
                pl.BlockSpec((tm, di), row),
                pl.BlockSpec((tm, di), lambda i: (i, zb)),
                pl.BlockSpec((tm, d), row),
                pl.BlockSpec((tm, d), lambda i: (i, fb + 1)),
                pl.BlockSpec((tm, d), lambda i: (i, fb + 2)),
                pl.BlockSpec((tm, d), row),
                pl.BlockSpec((None, 1, d), modrow),
                pl.BlockSpec((None, 1, d), modrow),
                pl.BlockSpec((None, 1, d), modrow),
                pl.BlockSpec((1, di), const),
                pl.BlockSpec((1, di), const),
                pl.BlockSpec((di, d), const),
                pl.BlockSpec((d, d), const),
                pl.BlockSpec((d, d), const),
                pl.BlockSpec((1, d), const),
                pl.BlockSpec((d, LANES), const),
                pl.BlockSpec((1, LANES), const)]
    tcol = lambda i: (0, i)
    out_specs = [pl.BlockSpec((tm, d), row), pl.BlockSpec((tm, d // 2), row),
                 pl.BlockSpec((8, tm), tcol), pl.BlockSpec((tm, LANES), row), pl.BlockSpec((8, tm), tcol),
                 pl.BlockSpec((8, LANES), const)]
    out_shape = [jax.ShapeDtypeStruct((m, d), F32), jax.ShapeDtypeStruct((m, d // 2), jnp.uint32),
                 jax.ShapeDtypeStruct((8, m), jnp.int32), jax.ShapeDtypeStruct((m, LANES), F32),
                 jax.ShapeDtypeStruct((8, m), jnp.int32), jax.ShapeDtypeStruct((8, LANES), F32)]
    return pl.pallas_call(
        _tail_kernel,
        grid=(m // tm,),
        in_specs=in_specs,
        out_specs=out_specs,
        out_shape=out_shape,
        scratch_shapes=[pltpu.VMEM((8, LANES), F32)],
        compiler_params=_cparams("arbitrary"),
        name="tail",
    )(y2, y2, xbc2d, p2d, ff, p2d, p2d, x2d, g1, sc2, sh2, dsk, snw, wso, wfo, wo, n2w, wr, br)


def _gather_rows(src, idx):
    n_out = idx.shape[0]
    width = src.shape[1]
    win = SC_GATHER_WINDOW
    info = plsc.get_sparse_core_info()
    n_cores, n_workers = info.num_cores, info.num_cores * info.num_subcores
    per_worker = n_out // n_workers
    assert per_worker * n_workers == n_out and per_worker % (2 * win) == 0
    mesh = plsc.VectorSubcoreMesh(core_axis_name="core", subcore_axis_name="subcore")

    @functools.partial(
        pl.kernel, out_type=jax.ShapeDtypeStruct((n_out, width), src.dtype), mesh=mesh,
        scratch_types=[pltpu.VMEM((win,), jnp.int32), pltpu.VMEM((win,), jnp.int32),
                       pltpu.VMEM((win, width), src.dtype), pltpu.VMEM((win, width), src.dtype),
                       pltpu.SemaphoreType.DMA, pltpu.SemaphoreType.DMA,
                       pltpu.SemaphoreType.DMA, pltpu.SemaphoreType.DMA],
        name="gather_rows")
    def gather(src_hbm, idx_hbm, out_hbm, idx_a, idx_b, rows_a, rows_b, gsem_a, gsem_b, ssem_a, ssem_b):
        worker = lax.axis_index("subcore") * n_cores + lax.axis_index("core")
        base = worker * per_worker

        @pl.loop(0, per_worker, step=2 * win)
        def _(off):
            pltpu.sync_copy(idx_hbm.at[pl.ds(base + off, win)], idx_a)
            ga = pltpu.async_copy(src_hbm.at[idx_a], rows_a, gsem_a)
            pltpu.sync_copy(idx_hbm.at[pl.ds(base + off + win, win)], idx_b)
            gb = pltpu.async_copy(src_hbm.at[idx_b], rows_b, gsem_b)
            ga.wait()
            sa = pltpu.async_copy(rows_a, out_hbm.at[pl.ds(base + off, win)], ssem_a)
            gb.wait()
            sb = pltpu.async_copy(rows_b, out_hbm.at[pl.ds(base + off + win, win)], ssem_b)
            sa.wait()
            sb.wait()

    return gather(src, idx)


def _scatter_rows(src, dest, n_rows):
    m, width = src.shape
    win = SC_GATHER_WINDOW
    info = plsc.get_sparse_core_info()
    n_cores, n_workers = info.num_cores, info.num_cores * info.num_subcores
    per_worker = m // n_workers
    assert per_worker * n_workers == m and per_worker % win == 0
    mesh = plsc.VectorSubcoreMesh(core_axis_name="core", subcore_axis_name="subcore")

    @functools.partial(
        pl.kernel, out_type=jax.ShapeDtypeStruct((n_rows, width), src.dtype), mesh=mesh,
        scratch_types=[pltpu.VMEM((win, width), src.dtype)]
        + [pltpu.VMEM((win,), jnp.int32)] * TOP_K + [pltpu.SemaphoreType.DMA] * (TOP_K + 1),
        name="scatter_rows")
    def scatter(src_hbm, dest_hbm, out_hbm, rows_v, *rest):
        idx_v, sems, row_sem = rest[:TOP_K], rest[TOP_K:2 * TOP_K], rest[2 * TOP_K]
        worker = lax.axis_index("subcore") * n_cores + lax.axis_index("core")
        base = worker * per_worker

        @pl.loop(0, per_worker, step=win)
        def _(off):
            t0 = base + off
            loads = [pltpu.async_copy(src_hbm.at[pl.ds(t0, win)], rows_v, row_sem)]
            loads += [pltpu.async_copy(dest_hbm.at[pl.ds(k * m + t0, win)], idx_v[k], sems[k]) for k in range(TOP_K)]
            for ld in loads:
                ld.wait()
            copies = [pltpu.async_copy(rows_v, out_hbm.at[idx_v[k]], sems[k]) for k in range(TOP_K)]
            for cp in copies:
                cp.wait()

    return scatter(src, dest)


def _expert_kernel(te_ref, nu_ref, tv_ref, x_ref, wgu_ref, bgu_ref, wd_ref, bd_ref, o_ref, wgu_scr, wd_scr):
    i = pl.program_id(0)
    used = i < nu_ref[0]
    new_expert = (i == 0) | (te_ref[i] != te_ref[jnp.maximum(i - 1, 0)])

    @pl.when(used & new_expert)
    def _():
        wgu_scr[...] = wgu_ref[...].astype(BF16)
        wd_scr[...] = wd_ref[...].astype(BF16)

    @pl.when(used)
    def _():
        dff = wd_ref.shape[0]
        rows = lax.broadcasted_iota(jnp.int32, x_ref.shape, 0)
        xa, xb = _unpack_bf16_pair(jnp.where(rows < tv_ref[i], x_ref[...], jnp.uint32(0)))
        x = jnp.concatenate([xa.astype(BF16), xb.astype(BF16)], axis=1)
        gu = jnp.dot(x, wgu_scr[...], preferred_element_type=F32) + bgu_ref[...]
        gate = jnp.minimum(gu[:, :dff], SWIGLU_LIMIT)
        up = jnp.clip(gu[:, dff:], -SWIGLU_LIMIT, SWIGLU_LIMIT)
        act = (up + 1.0) * gate * _sigmoid(SWIGLU_ALPHA * gate)
        y = jnp.dot(act.astype(BF16), wd_scr[...], preferred_element_type=F32) + bd_ref[...]
        o_ref[...] = _pack_bf16_pair(y)

    @pl.when(jnp.logical_not(used))
    def _():
        o_ref[...] = jnp.zeros(o_ref.shape, o_ref.dtype)


def _experts(xs, tile_expert, n_used, tile_valid, wgu, bgu, wd, bd):
    rows, dh = xs.shape
    d = 2 * dh
    tm = MOE_TILE
    n_tiles = rows // tm
    dff2 = wgu.shape[-1]

    def tile(i, te, nu, tv):
        return (jnp.minimum(i, nu[0] - 1), 0)

    def wsel(i, te, nu, tv):
        return (te[jnp.minimum(i, nu[0] - 1)], 0, 0)

    grid_spec = pltpu.PrefetchScalarGridSpec(
        num_scalar_prefetch=3,
        grid=(n_tiles,),
        in_specs=[pl.BlockSpec((tm, dh), tile),
                  pl.BlockSpec((None, d, dff2), wsel),
                  pl.BlockSpec((None, 1, dff2), wsel),
                  pl.BlockSpec((None, dff2 // 2, d), wsel),
                  pl.BlockSpec((None, 1, d), wsel)],
        out_specs=pl.BlockSpec((tm, dh), lambda i, te, nu, tv: (i, 0)),
        scratch_shapes=[pltpu.VMEM((d, dff2), BF16), pltpu.VMEM((dff2 // 2, d), BF16)],
    )
    return pl.pallas_call(
        _expert_kernel,
        grid_spec=grid_spec,
        out_shape=jax.ShapeDtypeStruct((rows, dh), jnp.uint32),
        compiler_params=_cparams("arbitrary"),
        name="experts",
    )(tile_expert, n_used, tile_valid, xs, wgu, bgu, wd, bd)


def _final_kernel(x1_ref, ya_ref, yb_ref, yc_ref, yd_ref, tw_ref, g2_ref, fw_ref, o_ref):
    d = x1_ref.shape[-1]
    half = d // 2
    tw = tw_ref[...]
    acc_hi = jnp.zeros((x1_ref.shape[0], half), F32)
    acc_lo = jnp.zeros((x1_ref.shape[0], half), F32)
    for k, y_ref in enumerate((ya_ref, yb_ref, yc_ref, yd_ref)):
        y_hi, y_lo = _unpack_bf16_pair(y_ref[...])
        acc_hi = acc_hi + tw[:, k:k + 1] * y_hi
        acc_lo = acc_lo + tw[:, k:k + 1] * y_lo
    x_hi = x1_ref[:, :half] + g2_ref[:, :half] * acc_hi
    x_lo = x1_ref[:, half:] + g2_ref[:, half:] * acc_lo
    ms = (jnp.sum(x_hi * x_hi, axis=-1, keepdims=True) + jnp.sum(x_lo * x_lo, axis=-1, keepdims=True)) / d
    inv = lax.rsqrt(ms + EPS)
    o_ref[:, :half] = x_hi * inv * fw_ref[:, :half]
    o_ref[:, half:] = x_lo * inv * fw_ref[:, half:]


def _final(x1, y4, tw, g2, fw, rows_per_mod, tm):
    m, d = x1.shape
    return pl.pallas_call(
        _final_kernel,
        grid=(m // tm,),
        in_specs=[pl.BlockSpec((tm, d), lambda i: (i, 0))]
        + [pl.BlockSpec((None, tm, d // 2), functools.partial(lambda k, i: (k, i, 0), k)) for k in range(TOP_K)]
        + [pl.BlockSpec((tm, LANES), lambda i: (i, 0)),
           pl.BlockSpec((None, 1, d), lambda i: ((i * tm) // rows_per_mod, 0, 0)),
           pl.BlockSpec((1, d), lambda i: (0, 0))],
        out_specs=pl.BlockSpec((tm, d), lambda i: (i, 0)),
        out_shape=jax.ShapeDtypeStruct((m, d), F32),
        compiler_params=_cparams("arbitrary"),
        name="final",
    )(x1, y4, y4, y4, y4, tw, g2, fw.reshape(1, d))


def _dispatch_plan(e, rank, counts, tm):
    n_assign = e.size
    padded = (counts + tm - 1) // tm * tm
    pad_end = jnp.cumsum(padded)
    pad_start = pad_end - padded
    dest = rank.astype(jnp.int32)
    for j in range(N_EXPERTS - 1):
        dest = dest + jnp.where(e > j, padded[j], 0).astype(jnp.int32)
    n_tiles = n_assign // tm + N_EXPERTS
    tile_start = jnp.arange(n_tiles, dtype=jnp.int32) * tm
    tile_expert = jnp.minimum(jnp.sum(pad_end[None, :] <= tile_start[:, None], axis=1), N_EXPERTS - 1).astype(jnp.int32)
    n_used = (pad_end[-1] // tm).astype(jnp.int32).reshape(1)
    tile_valid = jnp.clip(pad_start[tile_expert] + counts[tile_expert] - tile_start, 0, tm).astype(jnp.int32)
    return dest, tile_expert, n_used, tile_valid


def kernel(x, c, ctx, c_ctx, w_mod, b_mod, norm1_w, norm2_w, w_in, conv_w, conv_b, dt_bias, a_log, d_skip,
           ssd_norm_w, w_ssd_out, w_four_out, w_o, w_router, b_router, w_gate_up, b_gate_up, w_down, b_down,
           final_norm_w):
    bsz, n, d = x.shape
    n_ctx = ctx.shape[1]
    depth = w_mod.shape[0]
    assert depth == 1, "a stacked model would also need the context stream's residual update"
    d_inner = N_HEADS * HEAD_DIM
    off_dt = d_inner + 2 * BC_WIDTH
    m = bsz * n
    x2d = x.reshape(m, d)
    tm = min(512, n)

    for layer in range(depth):
        rows = -(-(bsz + 1) // 8) * 8
        c_rows = jnp.zeros((rows, d), F32).at[:bsz].set(c).at[bsz].set(c_ctx)
        mod = _modulation(c_rows, w_mod[layer], b_mod[layer])
        mods = [mod[:bsz, i * d:(i + 1) * d].reshape(bsz, 1, d) for i in range(N_MOD)]
        sh1, sc1, g1, sh2, sc2, g2 = mods
        sh1_c = mod[bsz:bsz + 1, :d].reshape(1, 1, d)
        sc1_c = mod[bsz:bsz + 1, d:2 * d].reshape(1, 1, d)

        wl = w_in[layer]
        w_main = jnp.concatenate([wl[:, :off_dt], wl[:, off_dt + 2 * N_HEADS:]], axis=1).astype(BF16)
        w_dt = jnp.zeros((d, 2 * LANES), F32)
        w_dt = w_dt.at[:, :N_HEADS].set(wl[:, off_dt:off_dt + N_HEADS])
        w_dt = w_dt.at[:, LANES:LANES + N_HEADS].set(wl[:, off_dt + N_HEADS:off_dt + 2 * N_HEADS]).astype(BF16)
        pad = jnp.zeros((2, 1, LANES - N_HEADS), F32)
        dt_bias2 = jnp.concatenate([dt_bias[layer].reshape(2, 1, N_HEADS), pad], axis=-1)
        a_log2 = jnp.concatenate([a_log[layer].reshape(2, 1, N_HEADS), pad], axis=-1)

        w_ctx = w_main[:, :off_dt]
        pc, dtc = _in_proj(ctx.reshape(bsz * n_ctx, d), norm1_w[layer], sc1_c, sh1_c, w_ctx, w_dt,
                           bsz * n_ctx, min(512, n_ctx))
        xbc_c = _conv_silu(pc.reshape(bsz, n_ctx, off_dt), conv_w[layer][:, :off_dt], conv_b[layer][:off_dt], off_dt)
        h0 = jnp.zeros((bsz, 2, N_GROUPS, D_STATE, HEADS_PER_GROUP * HEAD_DIM), F32)
        h_ctx = _ssd(xbc_c, dtc.reshape(bsz, n_ctx, 2 * LANES), dt_bias2, a_log2, h0, with_y=False)

        p, dtl = _in_proj(x2d, norm1_w[layer], sc1, sh1, w_main, w_dt, n, tm)
        n_conv = off_dt + 2 * BC_WIDTH
        xbc = _conv_silu(p.reshape(bsz, n, -1), conv_w[layer], conv_b[layer], n_conv)
        y2 = _ssd(xbc, dtl.reshape(bsz, n, 2 * LANES), dt_bias2, a_log2, h_ctx, with_y=True)
        ff = _fnet_mix(p, (n_conv + d_inner) // d, bsz, n)

        dsk = jnp.repeat(d_skip[layer].astype(F32), HEAD_DIM).reshape(1, d_inner)
        wr = jnp.zeros((d, LANES), F32).at[:, :N_EXPERTS].set(w_router[layer])
        br = jnp.full((1, LANES), NEG_BIG, F32).at[0, :N_EXPERTS].set(b_router[layer])
        x1, h2, ti, tw, rk, cnt = _tail(
            y2.reshape(2, m, d_inner), xbc.reshape(m, n_conv), p, ff, x2d, g1, sc2, sh2, dsk,
            ssd_norm_w[layer].reshape(1, d_inner), w_ssd_out[layer].astype(BF16), w_four_out[layer].astype(BF16),
            w_o[layer].astype(BF16), norm2_w[layer].reshape(1, d), wr, br, n, min(TAIL_ROWS, n))

        counts = cnt[0, :N_EXPERTS].astype(jnp.int32)
        dest, tile_expert, n_used, tile_valid = _dispatch_plan(ti[:TOP_K], rk[:TOP_K], counts, MOE_TILE)
        dest_km = dest.reshape(-1)
        xs = _scatter_rows(h2, dest_km, tile_expert.shape[0] * MOE_TILE)
        ys = _experts(xs, tile_expert, n_used, tile_valid, w_gate_up[layer],
                      b_gate_up[layer].reshape(N_EXPERTS, 1, -1), w_down[layer],
                      b_down[layer].reshape(N_EXPERTS, 1, -1))
        y4 = _gather_rows(ys, dest_km).reshape(TOP_K, m, d // 2)
        x2d = _final(x1, y4, tw, g2, final_norm_w, n, tm)
    return x2d.reshape(bsz, n, d)
```

```python
import functools
import math

import numpy as np
import jax
import jax.numpy as jnp
from jax import lax
from jax.experimental import pallas as pl
from jax.experimental.pallas import tpu as pltpu
from jax.experimental.pallas import tpu_sc as plsc

F32 = jnp.float32
BF16 = jnp.bfloat16

EPS = 1e-6
GRID_W = 64
N_MOD = 6
F_GROUPS = 8
F_GROUP_DIM = 128
HEAD_DIM = 64
N_HEADS = 32
N_GROUPS = 4
HEADS_PER_GROUP = N_HEADS // N_GROUPS
D_STATE = 128
BC_WIDTH = N_GROUPS * D_STATE
CONV_K = 5
CHUNK = 128
N_EXPERTS = 32
TOP_K = 4
SWIGLU_LIMIT = 7.0
SWIGLU_ALPHA = 1.702

LANES = 128
VMEM_LIMIT_BYTES = 56 * 1024 * 1024
NEG_BIG = -1e30
MOE_TILE = 512
TAIL_ROWS = 512
SC_GATHER_WINDOW = 64


def _cparams(*sem):
    return pltpu.CompilerParams(dimension_semantics=sem, vmem_limit_bytes=VMEM_LIMIT_BYTES)


def _sigmoid(v):
    return 1.0 / (1.0 + jnp.exp(-v))


def _silu(v):
    return v * _sigmoid(v)


def _softplus(v):
    return jnp.maximum(v, 0.0) + jnp.log(1.0 + jnp.exp(-jnp.abs(v)))


def _pack_bf16_pair(v):
    w = v.shape[1] // 2
    bits = lax.bitcast_convert_type(v.astype(BF16).astype(F32), jnp.uint32)
    return bits[:, :w] | (bits[:, w:] >> 16)


def _unpack_bf16_pair(p):
    hi = lax.bitcast_convert_type(p & jnp.uint32(0xFFFF0000), F32)
    lo = lax.bitcast_convert_type(p << 16, F32)
    return hi, lo


def _mod_kernel(c_ref, w_ref, b_ref, o_ref):
    s = _silu(c_ref[...]).astype(BF16)
    o_ref[...] = jnp.dot(s, w_ref[...].astype(BF16), preferred_element_type=F32) + b_ref[...]


def _modulation(c_rows, w_mod, b_mod):
    rows, d = c_rows.shape
    n_out = w_mod.shape[1]
    tn = 1024
    return pl.pallas_call(
        _mod_kernel,
        grid=(n_out // tn,),
        in_specs=[pl.BlockSpec((rows, d), lambda j: (0, 0)),
                  pl.BlockSpec((d, tn), lambda j: (0, j)),
                  pl.BlockSpec((1, tn), lambda j: (0, j))],
        out_specs=pl.BlockSpec((rows, tn), lambda j: (0, j)),
        out_shape=jax.ShapeDtypeStruct((rows, n_out), F32),
        compiler_params=_cparams("arbitrary"),
        name="modulation",
    )(c_rows, w_mod, b_mod.reshape(1, n_out))


_INPROJ_TN = 1024


def _inproj_kernel(x_ref, nw_ref, sc_ref, sh_ref, *rest):
    *w_refs, wdt_ref, p_ref, dt_ref = rest
    x = x_ref[...]
    inv = lax.rsqrt(jnp.mean(x * x, axis=-1, keepdims=True) + EPS)
    h = (x * inv * nw_ref[...]) * (1.0 + sc_ref[...]) + sh_ref[...]
    hb = h.astype(BF16)
    dt_ref[...] = jnp.dot(hb, wdt_ref[...], preferred_element_type=F32)
    out_col = 0
    for w_ref in w_refs:
        for j in range(w_ref.shape[1] // _INPROJ_TN):
            cols = slice(j * _INPROJ_TN, (j + 1) * _INPROJ_TN)
            p_ref[:, out_col:out_col + _INPROJ_TN] = jnp.dot(
                hb, w_ref[:, cols], preferred_element_type=F32).astype(p_ref.dtype)
            out_col += _INPROJ_TN


def _in_proj(x2d, norm_w, sc, sh, w_parts, w_dt, rows_per_mod, tm):
    m, d = x2d.shape
    n = sum(w.shape[1] for w in w_parts)
    ndt = w_dt.shape[1]
    resident = dict(pipeline_mode=pl.Buffered(1))
    return pl.pallas_call(
        _inproj_kernel,
        grid=(m // tm,),
        in_specs=[pl.BlockSpec((tm, d), lambda i: (i, 0)),
                  pl.BlockSpec((1, d), lambda i: (0, 0), **resident),
                  pl.BlockSpec((None, 1, d), lambda i: ((i * tm) // rows_per_mod, 0, 0)),
                  pl.BlockSpec((None, 1, d), lambda i: ((i * tm) // rows_per_mod, 0, 0))]
        + [pl.BlockSpec(w.shape, lambda i: (0, 0), **resident) for w in w_parts]
        + [pl.BlockSpec((d, ndt), lambda i: (0, 0), **resident)],
        out_specs=[pl.BlockSpec((tm, n), lambda i: (i, 0)),
                   pl.BlockSpec((tm, ndt), lambda i: (i, 0))],
        out_shape=[jax.ShapeDtypeStruct((m, n), BF16),
                   jax.ShapeDtypeStruct((m, ndt), F32)],
        compiler_params=_cparams("arbitrary"),
        name="in_proj",
    )(x2d, norm_w.reshape(1, d), sc, sh, *w_parts, w_dt)


_CONV_HALO = 16


_CONV_ROWS = 64


def _conv_shift_matrix(rc):
    win = rc + 2 * _CONV_HALO
    s = np.zeros((rc, CONV_K * win), np.float32)
    for k in range(CONV_K):
        for l in range(rc):
            s[l, k * win + _CONV_HALO + l + k - CONV_K // 2] = 1.0
    return jnp.asarray(s, BF16)


def _conv_kernel(p_ref, w_ref, b_ref, s_ref, o_ref, *, n, rc):
    tc = o_ref.shape[-1]
    w = w_ref[...].astype(BF16)
    bias = b_ref[...]
    smat = s_ref[...]
    zeros = jnp.zeros((_CONV_HALO, tc), BF16)
    for r0 in range(0, n, rc):
        top = zeros if r0 == 0 else p_ref[r0 - _CONV_HALO:r0, :]
        bot = zeros if r0 + rc >= n else p_ref[r0 + rc:r0 + rc + _CONV_HALO, :]
        window = jnp.concatenate([top, p_ref[r0:r0 + rc, :], bot], axis=0)
        taps = jnp.concatenate([window * w[k:k + 1, :] for k in range(CONV_K)], axis=0)
        acc = jnp.dot(smat, taps, preferred_element_type=F32) + bias
        o_ref[r0:r0 + rc, :] = _silu(acc).astype(o_ref.dtype)


def _conv_silu(p3d, conv_w, conv_b, n_ch):
    bsz, n, _ = p3d.shape
    tc = 512
    rc = min(_CONV_ROWS, n)
    smat = _conv_shift_matrix(rc)
    return pl.pallas_call(
        functools.partial(_conv_kernel, n=n, rc=rc),
        grid=(bsz, n_ch // tc),
        in_specs=[pl.BlockSpec((None, n, tc), lambda b, j: (b, 0, j)),
                  pl.BlockSpec((CONV_K, tc), lambda b, j: (0, j)),
                  pl.BlockSpec((1, tc), lambda b, j: (0, j)),
                  pl.BlockSpec(smat.shape, lambda b, j: (0, 0))],
        out_specs=pl.BlockSpec((None, n, tc), lambda b, j: (b, 0, j)),
        out_shape=jax.ShapeDtypeStruct((bsz, n, n_ch), BF16),
        compiler_params=_cparams("arbitrary", "arbitrary"),
        name="conv_silu",
    )(p3d, conv_w, conv_b.reshape(1, -1), smat)


LOG2E = 1.4426950408889634
SSD_SUB = 8


def _head_expand_matrix():
    e = np.zeros((LANES, N_HEADS * HEAD_DIM), np.float32)
    for h in range(N_HEADS):
        e[h, h * HEAD_DIM:(h + 1) * HEAD_DIM] = 1.0
    return jnp.asarray(e, BF16)


def _ssd_chunk_terms(dt_raw, dt_bias, a_log, expand, tri, fwd):
    q = CHUNK
    dt = _softplus(dt_raw + dt_bias)
    da = dt * (-jnp.exp(a_log))
    hi = da.astype(BF16)
    r1 = da - hi.astype(F32)
    mid = r1.astype(BF16)
    lo = (r1 - mid.astype(F32)).astype(BF16)
    ones_tri = jnp.where(tri, 1.0, 0.0).astype(BF16)
    cs3 = jnp.dot(ones_tri, jnp.concatenate([hi, mid, lo], axis=1), preferred_element_type=F32)
    cs = cs3[:, :LANES] + cs3[:, LANES:2 * LANES] + cs3[:, 2 * LANES:]
    tot = jnp.where(fwd, cs[q - 1:q, :], cs[0:1, :])
    dte = dt * jnp.exp(tot - cs)
    dec = jnp.exp(tot)
    dec_hi = dec.astype(BF16)
    dec_lo = (dec - dec_hi.astype(F32)).astype(BF16)
    dec2 = jnp.concatenate([dec_hi, dec_lo, jnp.zeros((6, LANES), BF16)], axis=0)
    stacked = jnp.concatenate([jnp.exp(cs).astype(BF16), dte.astype(BF16), dec2], axis=0)
    spread = jnp.dot(stacked, expand, preferred_element_type=F32)
    ecs_e = spread[:q]
    dte_e = spread[q:2 * q].astype(BF16)
    dec_e = jnp.broadcast_to(spread[2 * q:2 * q + 1] + spread[2 * q + 1:2 * q + 2], (8, spread.shape[1]))
    cs2 = cs * LOG2E
    return cs2, cs2.T, dt.T, ecs_e, dte_e, dec_e


def _ssd_kernel(*refs, with_y, nc, n_sub):
    if with_y:
        (x_ref, b_ref, c_ref, dt_ref, dtn_ref, dtb_ref, alog_ref, exp_ref, h0_ref, y_ref,
         s_scr, cs_scr, dee_scr, dce_scr, ecs_scr) = refs
    else:
        (x_ref, b_ref, dt_ref, dtn_ref, dtb_ref, alog_ref, exp_ref, h0_ref, hout_ref,
         s_scr, cs_scr, dee_scr, dce_scr) = refs
    q = CHUNK
    gw = HEADS_PER_GROUP * HEAD_DIM
    d = pl.program_id(1)
    c = pl.program_id(2)
    row = lax.broadcasted_iota(jnp.int32, (q, q), 0)
    col = lax.broadcasted_iota(jnp.int32, (q, q), 1)
    fwd = d == 0
    tri = jnp.where(fwd, row - col, col - row) >= 0
    first_half = col < HEAD_DIM

    def store_terms(slot, terms):
        cs2, cs2_t, dt_t, ecs_e, dte_e, dec_e = terms
        cs_scr[slot, 0] = cs2
        cs_scr[slot, 1] = cs2_t
        cs_scr[slot, 2] = dt_t
        dee_scr[slot] = dte_e
        dce_scr[slot] = dec_e
        if with_y:
            ecs_scr[slot] = ecs_e

    def terms_of(ref, sub):
        rows = pl.ds(pl.multiple_of(sub * q, q), q)
        return _ssd_chunk_terms(ref[rows, :], dtb_ref[...], alog_ref[...], exp_ref[...], tri, fwd)

    def sub_of(j):
        return jnp.where(fwd, j, n_sub - 1 - j)

    @pl.when(c == 0)
    def _():
        s_scr[...] = h0_ref[...]
        store_terms(0, terms_of(dt_ref, sub_of(0)))

    for j in range(n_sub):
        slot = j % 2
        rows = pl.ds(pl.multiple_of(sub_of(j) * q, q), q)
        cs2 = cs_scr[slot, 0]
        cs2_t = cs_scr[slot, 1]
        dt_t = cs_scr[slot, 2]
        if j + 1 < n_sub:
            store_terms(1 - slot, terms_of(dt_ref, sub_of(j + 1)))
        else:
            store_terms(1 - slot, terms_of(dtn_ref, sub_of(0)))

        for g in range(N_GROUPS):
            cols = slice(g * gw, (g + 1) * gw)
            bg = b_ref[rows, g * D_STATE:(g + 1) * D_STATE]
            xg = x_ref[rows, cols]
            s_old = s_scr[g]
            if with_y:
                cg = c_ref[rows, g * D_STATE:(g + 1) * D_STATE]
                cb = lax.dot_general(cg, bg, (((1,), (1,)), ((), ())), preferred_element_type=F32)
                y_off = jnp.dot(cg, s_old.astype(BF16), preferred_element_type=F32)
                for k in range(HEADS_PER_GROUP // 2):
                    pair = g * (HEADS_PER_GROUP // 2) + k
                    lanes = slice(pair * LANES, (pair + 1) * LANES)
                    xp = xg[:, k * LANES:(k + 1) * LANES]
                    xz = jnp.zeros_like(xp)
                    lhs = []
                    for hh in (2 * pair, 2 * pair + 1):
                        colb = jnp.broadcast_to(cs2[:, hh:hh + 1], (q, q))
                        rowb = cs2_t[hh:hh + 1, :]
                        decay = jnp.exp2(jnp.where(tri, colb - rowb, -jnp.inf))
                        lhs.append((cb * decay * dt_t[hh:hh + 1, :]).astype(BF16))
                    y_diag = jnp.dot(jnp.concatenate(lhs, axis=1),
                                     jnp.concatenate([jnp.where(first_half, xp, xz), jnp.where(first_half, xz, xp)],
                                                     axis=0),
                                     preferred_element_type=F32)
                    y = y_diag + ecs_scr[slot, :, lanes] * y_off[:, k * LANES:(k + 1) * LANES]
                    y_ref[rows, lanes] = y.astype(y_ref.dtype)
            xdte = xg * dee_scr[slot, :, cols]
            upd = lax.dot_general(bg, xdte, (((0,), (0,)), ((), ())), preferred_element_type=F32)
            s_scr[g] = s_old * dce_scr[slot, 0:1, cols] + upd

    if not with_y:
        @pl.when(c == nc - 1)
        def _():
            hout_ref[...] = s_scr[...]


def _ssd(xbc, dt_raw, dt_bias2, a_log2, h0, with_y):
    bsz, n, _ = xbc.shape
    n_sub = min(SSD_SUB, n // CHUNK)
    assert n_sub % 2 == 0 and n % (n_sub * CHUNK) == 0
    blk = n_sub * CHUNK
    nc = n // blk
    d_inner = N_HEADS * HEAD_DIM
    gw = HEADS_PER_GROUP * HEAD_DIM
    x_blk = d_inner // BC_WIDTH

    def ceff(dd, cc):
        return cc + dd * (nc - 1 - 2 * cc)

    def cnext(dd, cc):
        return ceff(dd, jnp.minimum(cc + 1, nc - 1))

    in_specs = [pl.BlockSpec((None, blk, d_inner), lambda b, dd, cc: (b, ceff(dd, cc), 0)),
                pl.BlockSpec((None, blk, BC_WIDTH), lambda b, dd, cc: (b, ceff(dd, cc), x_blk + dd))]
    args = [xbc, xbc]
    if with_y:
        in_specs.append(pl.BlockSpec((None, blk, BC_WIDTH), lambda b, dd, cc: (b, ceff(dd, cc), x_blk + 2 + dd)))
        args.append(xbc)
    in_specs += [pl.BlockSpec((None, blk, LANES), lambda b, dd, cc: (b, ceff(dd, cc), dd)),
                 pl.BlockSpec((None, blk, LANES), lambda b, dd, cc: (b, cnext(dd, cc), dd)),
                 pl.BlockSpec((None, 1, LANES), lambda b, dd, cc: (dd, 0, 0)),
                 pl.BlockSpec((None, 1, LANES), lambda b, dd, cc: (dd, 0, 0)),
                 pl.BlockSpec((LANES, d_inner), lambda b, dd, cc: (0, 0)),
                 pl.BlockSpec((None, None, N_GROUPS, D_STATE, gw), lambda b, dd, cc: (b, dd, 0, 0, 0))]
    args += [dt_raw, dt_raw, dt_bias2, a_log2, _head_expand_matrix(), h0]
    scratch = [pltpu.VMEM((N_GROUPS, D_STATE, gw), F32),
               pltpu.VMEM((2, 3, CHUNK, LANES), F32),
               pltpu.VMEM((2, CHUNK, d_inner), BF16),
               pltpu.VMEM((2, 8, d_inner), F32)]
    if with_y:
        out_specs = pl.BlockSpec((None, None, blk, d_inner), lambda b, dd, cc: (dd, b, ceff(dd, cc), 0))
        out_shape = jax.ShapeDtypeStruct((2, bsz, n, d_inner), BF16)
        scratch += [pltpu.VMEM((2, CHUNK, d_inner), F32)]
    else:
        out_specs = pl.BlockSpec((None, None, N_GROUPS, D_STATE, gw), lambda b, dd, cc: (b, dd, 0, 0, 0))
        out_shape = jax.ShapeDtypeStruct((bsz, 2, N_GROUPS, D_STATE, gw), F32)
    return pl.pallas_call(
        functools.partial(_ssd_kernel, with_y=with_y, nc=nc, n_sub=n_sub),
        grid=(bsz, 2, nc),
        in_specs=in_specs,
        out_specs=out_specs,
        out_shape=out_shape,
        scratch_shapes=scratch,
        compiler_params=_cparams("arbitrary", "arbitrary", "arbitrary"),
        name="ssd_scan" if with_y else "ssd_ctx_state",
    )(*args)


def _dft_mats(n):
    k = np.arange(n)
    ang = 2.0 * np.pi * ((k[:, None] * k[None, :]) % n) / n
    return np.cos(ang), np.sin(ang)


def _strided_rows(scr, t, start, count, stride, width):
    return jnp.concatenate([scr[t, lb, pl.ds(start, count, stride=stride), :] for lb in range(width // LANES)],
                           axis=1)


def _fnet_stage1_kernel(u_ref, chan_ref, m1_ref, twc_ref, tws_ref, o_ref, zs_scr, *, tcs):
    r = u_ref.shape[0]
    width = u_ref.shape[-1]
    u2 = u_ref[...].reshape(r * tcs, width)
    chan = chan_ref[...]
    for g in range(F_GROUPS):
        pq = jnp.dot(u2[:, g * F_GROUP_DIM:(g + 1) * F_GROUP_DIM], chan, preferred_element_type=F32)
        zs_scr[0, g] = pq[:, :F_GROUP_DIM]
        zs_scr[1, g] = pq[:, F_GROUP_DIM:]
    m1 = m1_ref[...]
    for j in range(tcs):
        zz = jnp.concatenate([_strided_rows(zs_scr, 0, j, r, tcs, width),
                              _strided_rows(zs_scr, 1, j, r, tcs, width)], axis=0).astype(BF16)
        u = jnp.dot(m1, zz, preferred_element_type=F32)
        ur = u[:r]
        ui = u[r:]
        tc = twc_ref[j]
        ts = tws_ref[j]
        o_ref[0, j] = (ur * tc + ui * ts).astype(o_ref.dtype)
        o_ref[1, j] = (ui * tc - ur * ts).astype(o_ref.dtype)


def _fnet_stage2_kernel(u_ref, k2_ref, o_ref):
    c, tbs, width = o_ref.shape
    uu = jnp.concatenate([u_ref[0].reshape(c * tbs, width), u_ref[1].reshape(c * tbs, width)], axis=0)
    out = jnp.dot(k2_ref[...], uu, preferred_element_type=F32)
    o_ref[...] = out.reshape(c, tbs, width).astype(o_ref.dtype)


def _fnet_mix(p2d, col_blk, bsz, n):
    m = p2d.shape[0]
    width = F_GROUPS * F_GROUP_DIM
    cgrid = GRID_W
    rgrid = n // cgrid
    scale = 1.0 / math.sqrt(n * F_GROUP_DIM)

    cc, sc = _dft_mats(F_GROUP_DIM)
    chan = jnp.asarray(np.concatenate([cc, sc], axis=1) * scale, BF16)

    cr, sr = _dft_mats(rgrid)
    m1 = jnp.asarray(np.block([[cr, -sr], [-sr, -cr]]), BF16)
    bb = np.arange(rgrid)[None, :]
    ci = np.arange(cgrid)[:, None]
    ang = 2.0 * np.pi * ((ci * bb) % n) / n
    twc = jnp.asarray(np.cos(ang)[:, :, None], F32)
    tws = jnp.asarray(np.sin(ang)[:, :, None], F32)
    tcs = 16
    p4 = p2d.reshape(bsz, rgrid, cgrid, p2d.shape[-1])
    u5 = pl.pallas_call(
        functools.partial(_fnet_stage1_kernel, tcs=tcs),
        grid=(bsz, cgrid // tcs),
        in_specs=[pl.BlockSpec((None, rgrid, tcs, width), lambda b, j: (b, 0, j, col_blk)),
                  pl.BlockSpec((F_GROUP_DIM, 2 * F_GROUP_DIM), lambda b, j: (0, 0)),
                  pl.BlockSpec((2 * rgrid, 2 * rgrid), lambda b, j: (0, 0)),
                  pl.BlockSpec((tcs, rgrid, 1), lambda b, j: (j, 0, 0)),
                  pl.BlockSpec((tcs, rgrid, 1), lambda b, j: (j, 0, 0))],
        out_specs=pl.BlockSpec((2, None, tcs, rgrid, width), lambda b, j: (0, b, j, 0, 0)),
        out_shape=jax.ShapeDtypeStruct((2, bsz, cgrid, rgrid, width), BF16),
        scratch_shapes=[pltpu.VMEM((2, width // LANES, rgrid * tcs, LANES), F32)],
        compiler_params=_cparams("arbitrary", "arbitrary"),
        name="fnet_stage1",
    )(p4, chan, m1, twc, tws)

    cc2, sc2 = _dft_mats(cgrid)
    tbs = min(16, rgrid)
    eye = np.eye(tbs)
    k2 = jnp.asarray(np.concatenate([np.kron(cc2, eye), np.kron(sc2, eye)], axis=1), BF16)
    out = pl.pallas_call(
        _fnet_stage2_kernel,
        grid=(bsz, rgrid // tbs),
        in_specs=[pl.BlockSpec((2, None, cgrid, tbs, width), lambda b, j: (0, b, 0, j, 0)),
                  pl.BlockSpec(k2.shape, lambda b, j: (0, 0))],
        out_specs=pl.BlockSpec((None, cgrid, tbs, width), lambda b, j: (b, 0, j, 0)),
        out_shape=jax.ShapeDtypeStruct((bsz, cgrid, rgrid, width), BF16),
        compiler_params=_cparams("arbitrary", "arbitrary"),
        name="fnet_stage2",
    )(u5, k2)
    return out.reshape(m, width)


def _tail_kernel(yf_ref, yb_ref, xs_ref, z_ref, ff_ref, gf_ref, gs_ref, x_ref, g1_ref, sc2_ref, sh2_ref,
                 dsk_ref, snw_ref, wso_ref, wfo_ref, wo_ref, n2w_ref, wr_ref, br_ref,
                 x1_ref, h2_ref, ti_ref, tw_ref, rk_ref, cnt_ref, carry_scr):
    y = (yf_ref[0] + yb_ref[0]).astype(F32) + dsk_ref[...] * xs_ref[...].astype(F32)
    g = y * _silu(z_ref[...]).astype(F32)
    inv = lax.rsqrt(jnp.mean(g * g, axis=-1, keepdims=True) + EPS)
    gn = (g * inv * snw_ref[...]).astype(BF16)
    y_ssd = jnp.dot(gn, wso_ref[...], preferred_element_type=F32)
    y_four = jnp.dot(ff_ref[...], wfo_ref[...], preferred_element_type=F32)
    t = _sigmoid(gf_ref[...]) * y_four.astype(BF16) + _sigmoid(gs_ref[...]) * y_ssd.astype(BF16)
    mix = jnp.dot(t, wo_ref[...], preferred_element_type=F32)
    x1 = x_ref[...] + g1_ref[...] * mix
    x1_ref[...] = x1
    inv2 = lax.rsqrt(jnp.mean(x1 * x1, axis=-1, keepdims=True) + EPS)
    h2 = (x1 * inv2 * n2w_ref[...]) * (1.0 + sc2_ref[...]) + sh2_ref[...]
    h2_ref[...] = _pack_bf16_pair(h2)
    h_hi = h2.astype(BF16)
    h_lo = (h2 - h_hi.astype(F32)).astype(BF16)
    wr = wr_ref[...]
    w_hi = wr.astype(BF16)
    w_lo = (wr - w_hi.astype(F32)).astype(BF16)
    logits = (jnp.dot(h_hi, w_hi, preferred_element_type=F32)
              + jnp.dot(h_hi, w_lo, preferred_element_type=F32)
              + jnp.dot(h_lo, w_hi, preferred_element_type=F32)) + br_ref[...]
    tm = logits.shape[0]
    lane = lax.broadcasted_iota(jnp.int32, (tm, LANES), 1)
    vals, idxs = [], []
    cur = logits
    for _ in range(TOP_K):
        mx = jnp.max(cur, axis=-1, keepdims=True)
        ix = jnp.min(jnp.where(cur == mx, lane, LANES), axis=-1, keepdims=True)
        vals.append(mx)
        idxs.append(ix)
        cur = jnp.where(lane == ix, NEG_BIG * 2.0, cur)
    es = [jnp.exp(v - vals[0]) for v in vals]
    den = es[0] + es[1] + es[2] + es[3]
    ti = jnp.zeros((tm, LANES), jnp.int32)
    tw = jnp.zeros((tm, LANES), F32)
    for k in range(TOP_K):
        ti = jnp.where(lane == k, idxs[k], ti)
        tw = jnp.where(lane == k, es[k] / den, tw)
    ti_ref[...] = ti.T[:8, :]
    tw_ref[...] = tw

    @pl.when(pl.program_id(0) == 0)
    def _():
        carry_scr[...] = jnp.zeros(carry_scr.shape, F32)

    onehots = [jnp.where(lane == ix, 1.0, 0.0) for ix in idxs]
    cnt = onehots[0] + onehots[1] + onehots[2] + onehots[3]
    r_i = lax.broadcasted_iota(jnp.int32, (tm, tm), 0)
    c_i = lax.broadcasted_iota(jnp.int32, (tm, tm), 1)
    earlier = jnp.where(r_i > c_i, 1.0, 0.0).astype(BF16)
    before = jnp.dot(earlier, cnt.astype(BF16), preferred_element_type=F32) + carry_scr[0:1, :]
    rk = jnp.zeros((tm, LANES), jnp.int32)
    for k in range(TOP_K):
        rank_k = jnp.sum(onehots[k] * before, axis=-1, keepdims=True)
        rk = jnp.where(lane == k, rank_k.astype(jnp.int32), rk)
    rk_ref[...] = rk.T[:8, :]
    total = carry_scr[...] + jnp.sum(cnt, axis=0, keepdims=True)
    carry_scr[...] = total
    cnt_ref[...] = total


def _tail(y2, xbc2d, p2d, ff, x2d, g1, sc2, sh2, dsk, snw, wso, wfo, wo, n2w, wr, br, rows_per_mod, tm):
    m, d = x2d.shape
    di = y2.shape[-1]
    zb = (2 * di) // di
    fb = (3 * di) // d
    row = lambda i: (i, 0)
    modrow = lambda i: ((i * tm) // rows_per_mod, 0, 0)
    const = lambda i: (0, 0)
    in_specs = [pl.BlockSpec((1, tm, di), lambda i: (0, i, 0)),
                pl.BlockSpec((1, tm, di), lambda i: (1, i, 0)),
                pl.BlockSpec((tm, di), row),
                pl.BlockSpec((tm, di), lambda i: (i, zb)),
                pl.BlockSpec((tm, d), row),
                pl.BlockSpec((tm, d), lambda i: (i, fb + 1)),
                pl.BlockSpec((tm, d), lambda i: (i, fb + 2)),
                pl.BlockSpec((tm, d), row),
                pl.BlockSpec((None, 1, d), modrow),
                pl.BlockSpec((None, 1, d), modrow),
                pl.BlockSpec((None, 1, d), modrow),
                pl.BlockSpec((1, di), const),
                pl.BlockSpec((1, di), const),
                pl.BlockSpec((di, d), const),
                pl.BlockSpec((d, d), const),
                pl.BlockSpec((d, d), const),
                pl.BlockSpec((1, d), const),
                pl.BlockSpec((d, LANES), const),
                pl.BlockSpec((1, LANES), const)]
    tcol = lambda i: (0, i)
    out_specs = [pl.BlockSpec((tm, d), row), pl.BlockSpec((tm, d // 2), row),
                 pl.BlockSpec((8, tm), tcol), pl.BlockSpec((tm, LANES), row), pl.BlockSpec((8, tm), tcol),
                 pl.BlockSpec((8, LANES), const)]
    out_shape = [jax.ShapeDtypeStruct((m, d), F32), jax.ShapeDtypeStruct((m, d // 2), jnp.uint32),
                 jax.ShapeDtypeStruct((8, m), jnp.int32), jax.ShapeDtypeStruct((m, LANES), F32),
                 jax.ShapeDtypeStruct((8, m), jnp.int32), jax.ShapeDtypeStruct((8, LANES), F32)]
    return pl.pallas_call(
        _tail_kernel,
        grid=(m // tm,),
        in_specs=in_specs,
        out_specs=out_specs,
        out_shape=out_shape,
        scratch_shapes=[pltpu.VMEM((8, LANES), F32)],
        compiler_params=_cparams("arbitrary"),
        name="tail",
    )(y2, y2, xbc2d, p2d, ff, p2d, p2d, x2d, g1, sc2, sh2, dsk, snw, wso, wfo, wo, n2w, wr, br)


def _gather_rows(src, idx):
    n_out = idx.shape[0]
    width = src.shape[1]
    win = SC_GATHER_WINDOW
    info = plsc.get_sparse_core_info()
    n_cores, n_workers = info.num_cores, info.num_cores * info.num_subcores
    per_worker = n_out // n_workers
    assert per_worker * n_workers == n_out and per_worker % (2 * win) == 0
    mesh = plsc.VectorSubcoreMesh(core_axis_name="core", subcore_axis_name="subcore")

    @functools.partial(
        pl.kernel, out_type=jax.ShapeDtypeStruct((n_out, width), src.dtype), mesh=mesh,
        scratch_types=[pltpu.VMEM((win,), jnp.int32), pltpu.VMEM((win,), jnp.int32),
                       pltpu.VMEM((win, width), src.dtype), pltpu.VMEM((win, width), src.dtype),
                       pltpu.SemaphoreType.DMA, pltpu.SemaphoreType.DMA,
                       pltpu.SemaphoreType.DMA, pltpu.SemaphoreType.DMA],
        name="gather_rows")
    def gather(src_hbm, idx_hbm, out_hbm, idx_a, idx_b, rows_a, rows_b, gsem_a, gsem_b, ssem_a, ssem_b):
        worker = lax.axis_index("subcore") * n_cores + lax.axis_index("core")
        base = worker * per_worker

        @pl.loop(0, per_worker, step=2 * win)
        def _(off):
            pltpu.sync_copy(idx_hbm.at[pl.ds(base + off, win)], idx_a)
            ga = pltpu.async_copy(src_hbm.at[idx_a], rows_a, gsem_a)
            pltpu.sync_copy(idx_hbm.at[pl.ds(base + off + win, win)], idx_b)
            gb = pltpu.async_copy(src_hbm.at[idx_b], rows_b, gsem_b)
            ga.wait()
            sa = pltpu.async_copy(rows_a, out_hbm.at[pl.ds(base + off, win)], ssem_a)
            gb.wait()
            sb = pltpu.async_copy(rows_b, out_hbm.at[pl.ds(base + off + win, win)], ssem_b)
            sa.wait()
            sb.wait()

    return gather(src, idx)


def _scatter_rows(src, dest, n_rows):
    m, width = src.shape
    win = SC_GATHER_WINDOW
    info = plsc.get_sparse_core_info()
    n_cores, n_workers = info.num_cores, info.num_cores * info.num_subcores
    per_worker = m // n_workers
    assert per_worker * n_workers == m and per_worker % win == 0
    mesh = plsc.VectorSubcoreMesh(core_axis_name="core", subcore_axis_name="subcore")

    @functools.partial(
        pl.kernel, out_type=jax.ShapeDtypeStruct((n_rows, width), src.dtype), mesh=mesh,
        scratch_types=[pltpu.VMEM((win, width), src.dtype)]
        + [pltpu.VMEM((win,), jnp.int32)] * TOP_K + [pltpu.SemaphoreType.DMA] * (TOP_K + 1),
        name="scatter_rows")
    def scatter(src_hbm, dest_hbm, out_hbm, rows_v, *rest):
        idx_v, sems, row_sem = rest[:TOP_K], rest[TOP_K:2 * TOP_K], rest[2 * TOP_K]
        worker = lax.axis_index("subcore") * n_cores + lax.axis_index("core")
        base = worker * per_worker

        @pl.loop(0, per_worker, step=win)
        def _(off):
            t0 = base + off
            loads = [pltpu.async_copy(src_hbm.at[pl.ds(t0, win)], rows_v, row_sem)]
            loads += [pltpu.async_copy(dest_hbm.at[pl.ds(k * m + t0, win)], idx_v[k], sems[k]) for k in range(TOP_K)]
            for ld in loads:
                ld.wait()
            copies = [pltpu.async_copy(rows_v, out_hbm.at[idx_v[k]], sems[k]) for k in range(TOP_K)]
            for cp in copies:
                cp.wait()

    return scatter(src, dest)


def _expert_kernel(te_ref, nu_ref, tv_ref, x_ref, wgu_ref, bgu_ref, wd_ref, bd_ref, o_ref, wgu_scr, wd_scr):
    i = pl.program_id(0)
    used = i < nu_ref[0]
    new_expert = (i == 0) | (te_ref[i] != te_ref[jnp.maximum(i - 1, 0)])

    @pl.when(used & new_expert)
    def _():
        wgu_scr[...] = wgu_ref[...].astype(BF16)
        wd_scr[...] = wd_ref[...].astype(BF16)

    @pl.when(used)
    def _():
        dff = wd_ref.shape[0]
        rows = lax.broadcasted_iota(jnp.int32, x_ref.shape, 0)
        xa, xb = _unpack_bf16_pair(jnp.where(rows < tv_ref[i], x_ref[...], jnp.uint32(0)))
        x = jnp.concatenate([xa.astype(BF16), xb.astype(BF16)], axis=1)
        gu = jnp.dot(x, wgu_scr[...], preferred_element_type=F32) + bgu_ref[...]
        gate = jnp.minimum(gu[:, :dff], SWIGLU_LIMIT)
        up = jnp.clip(gu[:, dff:], -SWIGLU_LIMIT, SWIGLU_LIMIT)
        act = (up + 1.0) * gate * _sigmoid(SWIGLU_ALPHA * gate)
        y = jnp.dot(act.astype(BF16), wd_scr[...], preferred_element_type=F32) + bd_ref[...]
        o_ref[...] = _pack_bf16_pair(y)

    @pl.when(jnp.logical_not(used))
    def _():
        o_ref[...] = jnp.zeros(o_ref.shape, o_ref.dtype)


def _experts(xs, tile_expert, n_used, tile_valid, wgu, bgu, wd, bd):
    rows, dh = xs.shape
    d = 2 * dh
    tm = MOE_TILE
    n_tiles = rows // tm
    dff2 = wgu.shape[-1]

    def tile(i, te, nu, tv):
        return (jnp.minimum(i, nu[0] - 1), 0)

    def wsel(i, te, nu, tv):
        return (te[jnp.minimum(i, nu[0] - 1)], 0, 0)

    grid_spec = pltpu.PrefetchScalarGridSpec(
        num_scalar_prefetch=3,
        grid=(n_tiles,),
        in_specs=[pl.BlockSpec((tm, dh), tile),
                  pl.BlockSpec((None, d, dff2), wsel),
                  pl.BlockSpec((None, 1, dff2), wsel),
                  pl.BlockSpec((None, dff2 // 2, d), wsel),
                  pl.BlockSpec((None, 1, d), wsel)],
        out_specs=pl.BlockSpec((tm, dh), lambda i, te, nu, tv: (i, 0)),
        scratch_shapes=[pltpu.VMEM((d, dff2), BF16), pltpu.VMEM((dff2 // 2, d), BF16)],
    )
    return pl.pallas_call(
        _expert_kernel,
        grid_spec=grid_spec,
        out_shape=jax.ShapeDtypeStruct((rows, dh), jnp.uint32),
        compiler_params=_cparams("arbitrary"),
        name="experts",
    )(tile_expert, n_used, tile_valid, xs, wgu, bgu, wd, bd)


def _final_kernel(x1_ref, ya_ref, yb_ref, yc_ref, yd_ref, tw_ref, g2_ref, fw_ref, o_ref):
    d = x1_ref.shape[-1]
    half = d // 2
    tw = tw_ref[...]
    acc_hi = jnp.zeros((x1_ref.shape[0], half), F32)
    acc_lo = jnp.zeros((x1_ref.shape[0], half), F32)
    for k, y_ref in enumerate((ya_ref, yb_ref, yc_ref, yd_ref)):
        y_hi, y_lo = _unpack_bf16_pair(y_ref[...])
        acc_hi = acc_hi + tw[:, k:k + 1] * y_hi
        acc_lo = acc_lo + tw[:, k:k + 1] * y_lo
    x_hi = x1_ref[:, :half] + g2_ref[:, :half] * acc_hi
    x_lo = x1_ref[:, half:] + g2_ref[:, half:] * acc_lo
    ms = (jnp.sum(x_hi * x_hi, axis=-1, keepdims=True) + jnp.sum(x_lo * x_lo, axis=-1, keepdims=True)) / d
    inv = lax.rsqrt(ms + EPS)
    o_ref[:, :half] = x_hi * inv * fw_ref[:, :half]
    o_ref[:, half:] = x_lo * inv * fw_ref[:, half:]


def _final(x1, y4, tw, g2, fw, rows_per_mod, tm):
    m, d = x1.shape
    return pl.pallas_call(
        _final_kernel,
        grid=(m // tm,),
        in_specs=[pl.BlockSpec((tm, d), lambda i: (i, 0))]
        + [pl.BlockSpec((None, tm, d // 2), functools.partial(lambda k, i: (k, i, 0), k)) for k in range(TOP_K)]
        + [pl.BlockSpec((tm, LANES), lambda i: (i, 0)),
           pl.BlockSpec((None, 1, d), lambda i: ((i * tm) // rows_per_mod, 0, 0)),
           pl.BlockSpec((1, d), lambda i: (0, 0))],
        out_specs=pl.BlockSpec((tm, d), lambda i: (i, 0)),
        out_shape=jax.ShapeDtypeStruct((m, d), F32),
        compiler_params=_cparams("arbitrary"),
        name="final",
    )(x1, y4, y4, y4, y4, tw, g2, fw.reshape(1, d))


def _dispatch_plan(e, rank, counts, tm):
    n_assign = e.size
    padded = (counts + tm - 1) // tm * tm
    pad_end = jnp.cumsum(padded)
    pad_start = pad_end - padded
    dest = rank.astype(jnp.int32)
    for j in range(N_EXPERTS - 1):
        dest = dest + jnp.where(e > j, padded[j], 0).astype(jnp.int32)
    n_tiles = n_assign // tm + N_EXPERTS
    tile_start = jnp.arange(n_tiles, dtype=jnp.int32) * tm
    tile_expert = jnp.minimum(jnp.sum(pad_end[None, :] <= tile_start[:, None], axis=1), N_EXPERTS - 1).astype(jnp.int32)
    n_used = (pad_end[-1] // tm).astype(jnp.int32).reshape(1)
    tile_valid = jnp.clip(pad_start[tile_expert] + counts[tile_expert] - tile_start, 0, tm).astype(jnp.int32)
    return dest, tile_expert, n_used, tile_valid


def kernel(x, c, ctx, c_ctx, w_mod, b_mod, norm1_w, norm2_w, w_in, conv_w, conv_b, dt_bias, a_log, d_skip,
           ssd_norm_w, w_ssd_out, w_four_out, w_o, w_router, b_router, w_gate_up, b_gate_up, w_down, b_down,
           final_norm_w):
    bsz, n, d = x.shape
    n_ctx = ctx.shape[1]
    depth = w_mod.shape[0]
    assert depth == 1, "a stacked model would also need the context stream's residual update"
    d_inner = N_HEADS * HEAD_DIM
    off_dt = d_inner + 2 * BC_WIDTH
    m = bsz * n
    x2d = x.reshape(m, d)
    tm = min(512, n)

    for layer in range(depth):
        rows = -(-(bsz + 1) // 8) * 8
        c_rows = jnp.zeros((rows, d), F32).at[:bsz].set(c).at[bsz].set(c_ctx)
        mod = _modulation(c_rows, w_mod[layer], b_mod[layer])
        mods = [mod[:bsz, i * d:(i + 1) * d].reshape(bsz, 1, d) for i in range(N_MOD)]
        sh1, sc1, g1, sh2, sc2, g2 = mods
        sh1_c = mod[bsz:bsz + 1, :d].reshape(1, 1, d)
        sc1_c = mod[bsz:bsz + 1, d:2 * d].reshape(1, 1, d)

        wl = w_in[layer]
        w_head = wl[:, :off_dt].astype(BF16)
        w_rest = wl[:, off_dt + 2 * N_HEADS:].astype(BF16)
        w_dt = jnp.zeros((d, 2 * LANES), F32)
        w_dt = w_dt.at[:, :N_HEADS].set(wl[:, off_dt:off_dt + N_HEADS])
        w_dt = w_dt.at[:, LANES:LANES + N_HEADS].set(wl[:, off_dt + N_HEADS:off_dt + 2 * N_HEADS]).astype(BF16)
        pad = jnp.zeros((2, 1, LANES - N_HEADS), F32)
        dt_bias2 = jnp.concatenate([dt_bias[layer].reshape(2, 1, N_HEADS), pad], axis=-1)
        a_log2 = jnp.concatenate([a_log[layer].reshape(2, 1, N_HEADS), pad], axis=-1)

        pc, dtc = _in_proj(ctx.reshape(bsz * n_ctx, d), norm1_w[layer], sc1_c, sh1_c, [w_head], w_dt,
                           bsz * n_ctx, min(512, n_ctx))
        xbc_c = _conv_silu(pc.reshape(bsz, n_ctx, off_dt), conv_w[layer][:, :off_dt], conv_b[layer][:off_dt], off_dt)
        h0 = jnp.zeros((bsz, 2, N_GROUPS, D_STATE, HEADS_PER_GROUP * HEAD_DIM), F32)
        h_ctx = _ssd(xbc_c, dtc.reshape(bsz, n_ctx, 2 * LANES), dt_bias2, a_log2, h0, with_y=False)

        p, dtl = _in_proj(x2d, norm1_w[layer], sc1, sh1, [w_head, w_rest], w_dt, n, tm)
        n_conv = off_dt + 2 * BC_WIDTH
        xbc = _conv_silu(p.reshape(bsz, n, -1), conv_w[layer], conv_b[layer], n_conv)
        y2 = _ssd(xbc, dtl.reshape(bsz, n, 2 * LANES), dt_bias2, a_log2, h_ctx, with_y=True)
        ff = _fnet_mix(p, (n_conv + d_inner) // d, bsz, n)

        dsk = jnp.repeat(d_skip[layer].astype(F32), HEAD_DIM).reshape(1, d_inner)
        wr = jnp.zeros((d, LANES), F32).at[:, :N_EXPERTS].set(w_router[layer])
        br = jnp.full((1, LANES), NEG_BIG, F32).at[0, :N_EXPERTS].set(b_router[layer])
        x1, h2, ti, tw, rk, cnt = _tail(
            y2.reshape(2, m, d_inner), xbc.reshape(m, n_conv), p, ff, x2d, g1, sc2, sh2, dsk,
            ssd_norm_w[layer].reshape(1, d_inner), w_ssd_out[layer].astype(BF16), w_four_out[layer].astype(BF16),
            w_o[layer].astype(BF16), norm2_w[layer].reshape(1, d), wr, br, n, min(TAIL_ROWS, n))

        counts = cnt[0, :N_EXPERTS].astype(jnp.int32)
        dest, tile_expert, n_used, tile_valid = _dispatch_plan(ti[:TOP_K], rk[:TOP_K], counts, MOE_TILE)
        dest_km = dest.reshape(-1)
        xs = _scatter_rows(h2, dest_km, tile_expert.shape[0] * MOE_TILE)
        ys = _experts(xs, tile_expert, n_used, tile_valid, w_gate_up[layer],
                      b_gate_up[layer].reshape(N_EXPERTS, 1, -1), w_down[layer],
                      b_down[layer].reshape(N_EXPERTS, 1, -1))
        y4 = _gather_rows(ys, dest_km).reshape(TOP_K, m, d // 2)
        x2d = _final(x1, y4, tw, g2, final_norm_w, n, tm)
    return x2d.reshape(bsz, n, d)
```

```python
import functools
import math

import numpy as np
import jax
import jax.numpy as jnp
from jax import lax
from jax.experimental import pallas as pl
from jax.experimental.pallas import tpu as pltpu
from jax.experimental.pallas import tpu_sc as plsc

F32 = jnp.float32
BF16 = jnp.bfloat16

EPS = 1e-6
GRID_W = 64
N_MOD = 6
F_GROUPS = 8
F_GROUP_DIM = 128
HEAD_DIM = 64
N_HEADS = 32
N_GROUPS = 4
HEADS_PER_GROUP = N_HEADS // N_GROUPS
D_STATE = 128
BC_WIDTH = N_GROUPS * D_STATE
CONV_K = 5
CHUNK = 128
N_EXPERTS = 32
TOP_K = 4
SWIGLU_LIMIT = 7.0
SWIGLU_ALPHA = 1.702

LANES = 128
VMEM_LIMIT_BYTES = 56 * 1024 * 1024
NEG_BIG = -1e30
MOE_TILE = 512
TAIL_ROWS = 512
SC_GATHER_WINDOW = 64


def _cparams(*sem):
    return pltpu.CompilerParams(dimension_semantics=sem, vmem_limit_bytes=VMEM_LIMIT_BYTES)


def _sigmoid(v):
    return 1.0 / (1.0 + jnp.exp(-v))


def _silu(v):
    return v * _sigmoid(v)


def _softplus(v):
    return jnp.maximum(v, 0.0) + jnp.log(1.0 + jnp.exp(-jnp.abs(v)))


def _pack_bf16_pair(v):
    w = v.shape[1] // 2
    bits = lax.bitcast_convert_type(v.astype(BF16).astype(F32), jnp.uint32)
    return bits[:, :w] | (bits[:, w:] >> 16)


def _unpack_bf16_pair(p):
    hi = lax.bitcast_convert_type(p & jnp.uint32(0xFFFF0000), F32)
    lo = lax.bitcast_convert_type(p << 16, F32)
    return hi, lo


def _mod_kernel(c_ref, w_ref, b_ref, o_ref):
    s = _silu(c_ref[...]).astype(BF16)
    o_ref[...] = jnp.dot(s, w_ref[...].astype(BF16), preferred_element_type=F32) + b_ref[...]


def _modulation(c_rows, w_mod, b_mod):
    rows, d = c_rows.shape
    n_out = w_mod.shape[1]
    tn = 1024
    return pl.pallas_call(
        _mod_kernel,
        grid=(n_out // tn,),
        in_specs=[pl.BlockSpec((rows, d), lambda j: (0, 0)),
                  pl.BlockSpec((d, tn), lambda j: (0, j)),
                  pl.BlockSpec((1, tn), lambda j: (0, j))],
        out_specs=pl.BlockSpec((rows, tn), lambda j: (0, j)),
        out_shape=jax.ShapeDtypeStruct((rows, n_out), F32),
        compiler_params=_cparams("arbitrary"),
        name="modulation",
    )(c_rows, w_mod, b_mod.reshape(1, n_out))


_INPROJ_TN = 1024


def _inproj_kernel(x_ref, nw_ref, sc_ref, sh_ref, *rest):
    *w_refs, wdt_ref, p_ref, dt_ref = rest
    x = x_ref[...]
    inv = lax.rsqrt(jnp.mean(x * x, axis=-1, keepdims=True) + EPS)
    h = (x * inv * nw_ref[...]) * (1.0 + sc_ref[...]) + sh_ref[...]
    hb = h.astype(BF16)
    dt_ref[...] = jnp.dot(hb, wdt_ref[...], preferred_element_type=F32)
    out_col = 0
    for w_ref in w_refs:
        for j in range(w_ref.shape[1] // _INPROJ_TN):
            cols = slice(j * _INPROJ_TN, (j + 1) * _INPROJ_TN)
            p_ref[:, out_col:out_col + _INPROJ_TN] = jnp.dot(
                hb, w_ref[:, cols], preferred_element_type=F32).astype(p_ref.dtype)
            out_col += _INPROJ_TN


def _in_proj(x2d, norm_w, sc, sh, w_parts, w_dt, rows_per_mod, tm):
    m, d = x2d.shape
    n = sum(w.shape[1] for w in w_parts)
    ndt = w_dt.shape[1]
    resident = dict(pipeline_mode=pl.Buffered(1))
    return pl.pallas_call(
        _inproj_kernel,
        grid=(m // tm,),
        in_specs=[pl.BlockSpec((tm, d), lambda i: (i, 0)),
                  pl.BlockSpec((1, d), lambda i: (0, 0), **resident),
                  pl.BlockSpec((None, 1, d), lambda i: ((i * tm) // rows_per_mod, 0, 0)),
                  pl.BlockSpec((None, 1, d), lambda i: ((i * tm) // rows_per_mod, 0, 0))]
        + [pl.BlockSpec(w.shape, lambda i: (0, 0), **resident) for w in w_parts]
        + [pl.BlockSpec((d, ndt), lambda i: (0, 0), **resident)],
        out_specs=[pl.BlockSpec((tm, n), lambda i: (i, 0)),
                   pl.BlockSpec((tm, ndt), lambda i: (i, 0))],
        out_shape=[jax.ShapeDtypeStruct((m, n), BF16),
                   jax.ShapeDtypeStruct((m, ndt), F32)],
        compiler_params=_cparams("arbitrary"),
        name="in_proj",
    )(x2d, norm_w.reshape(1, d), sc, sh, *w_parts, w_dt)


_CONV_HALO = 16


_CONV_ROWS = 64


def _conv_shift_matrix(rc):
    win = rc + 2 * _CONV_HALO
    s = np.zeros((rc, CONV_K * win), np.float32)
    for k in range(CONV_K):
        for l in range(rc):
            s[l, k * win + _CONV_HALO + l + k - CONV_K // 2] = 1.0
    return jnp.asarray(s, BF16)


def _conv_kernel(p_ref, w_ref, b_ref, s_ref, o_ref, *, n, rc):
    tc = o_ref.shape[-1]
    w = w_ref[...].astype(BF16)
    bias = b_ref[...]
    smat = s_ref[...]
    zeros = jnp.zeros((_CONV_HALO, tc), BF16)
    for r0 in range(0, n, rc):
        top = zeros if r0 == 0 else p_ref[r0 - _CONV_HALO:r0, :]
        bot = zeros if r0 + rc >= n else p_ref[r0 + rc:r0 + rc + _CONV_HALO, :]
        window = jnp.concatenate([top, p_ref[r0:r0 + rc, :], bot], axis=0)
        taps = jnp.concatenate([window * w[k:k + 1, :] for k in range(CONV_K)], axis=0)
        acc = jnp.dot(smat, taps, preferred_element_type=F32) + bias
        o_ref[r0:r0 + rc, :] = _silu(acc).astype(o_ref.dtype)


def _conv_silu(p3d, conv_w, conv_b, n_ch):
    bsz, n, _ = p3d.shape
    tc = 512
    rc = min(_CONV_ROWS, n)
    smat = _conv_shift_matrix(rc)
    return pl.pallas_call(
        functools.partial(_conv_kernel, n=n, rc=rc),
        grid=(bsz, n_ch // tc),
        in_specs=[pl.BlockSpec((None, n, tc), lambda b, j: (b, 0, j)),
                  pl.BlockSpec((CONV_K, tc), lambda b, j: (0, j)),
                  pl.BlockSpec((1, tc), lambda b, j: (0, j)),
                  pl.BlockSpec(smat.shape, lambda b, j: (0, 0))],
        out_specs=pl.BlockSpec((None, n, tc), lambda b, j: (b, 0, j)),
        out_shape=jax.ShapeDtypeStruct((bsz, n, n_ch), BF16),
        compiler_params=_cparams("arbitrary", "arbitrary"),
        name="conv_silu",
    )(p3d, conv_w, conv_b.reshape(1, -1), smat)


LOG2E = 1.4426950408889634
SSD_SUB = 8


def _head_expand_matrix():
    e = np.zeros((LANES, N_HEADS * HEAD_DIM), np.float32)
    for h in range(N_HEADS):
        e[h, h * HEAD_DIM:(h + 1) * HEAD_DIM] = 1.0
    return jnp.asarray(e, BF16)


def _ssd_chunk_terms(dt_raw, dt_bias, a_log, expand, tri, fwd):
    q = CHUNK
    dt = _softplus(dt_raw + dt_bias)
    da = dt * (-jnp.exp(a_log))
    hi = da.astype(BF16)
    r1 = da - hi.astype(F32)
    mid = r1.astype(BF16)
    lo = (r1 - mid.astype(F32)).astype(BF16)
    ones_tri = jnp.where(tri, 1.0, 0.0).astype(BF16)
    cs3 = jnp.dot(ones_tri, jnp.concatenate([hi, mid, lo], axis=1), preferred_element_type=F32)
    cs = cs3[:, :LANES] + cs3[:, LANES:2 * LANES] + cs3[:, 2 * LANES:]
    tot = jnp.where(fwd, cs[q - 1:q, :], cs[0:1, :])
    dte = dt * jnp.exp(tot - cs)
    dec = jnp.exp(tot)
    dec_hi = dec.astype(BF16)
    dec_lo = (dec - dec_hi.astype(F32)).astype(BF16)
    dec2 = jnp.concatenate([dec_hi, dec_lo, jnp.zeros((6, LANES), BF16)], axis=0)
    stacked = jnp.concatenate([jnp.exp(cs).astype(BF16), dte.astype(BF16), dec2], axis=0)
    spread = jnp.dot(stacked, expand, preferred_element_type=F32)
    ecs_e = spread[:q]
    dte_e = spread[q:2 * q].astype(BF16)
    dec_e = jnp.broadcast_to(spread[2 * q:2 * q + 1] + spread[2 * q + 1:2 * q + 2], (8, spread.shape[1]))
    cs2 = cs * LOG2E
    return cs2, cs2.T, dt.T, ecs_e, dte_e, dec_e


def _ssd_kernel(*refs, with_y, nc, n_sub):
    if with_y:
        (x_ref, b_ref, c_ref, dt_ref, dtn_ref, dtb_ref, alog_ref, exp_ref, h0_ref, y_ref,
         s_scr, cs_scr, dee_scr, dce_scr, ecs_scr) = refs
    else:
        (x_ref, b_ref, dt_ref, dtn_ref, dtb_ref, alog_ref, exp_ref, h0_ref, hout_ref,
         s_scr, cs_scr, dee_scr, dce_scr) = refs
    q = CHUNK
    gw = HEADS_PER_GROUP * HEAD_DIM
    d = pl.program_id(1)
    c = pl.program_id(2)
    row = lax.broadcasted_iota(jnp.int32, (q, q), 0)
    col = lax.broadcasted_iota(jnp.int32, (q, q), 1)
    fwd = d == 0
    tri = jnp.where(fwd, row - col, col - row) >= 0
    first_half = col < HEAD_DIM

    def store_terms(slot, terms):
        cs2, cs2_t, dt_t, ecs_e, dte_e, dec_e = terms
        cs_scr[slot, 0] = cs2
        cs_scr[slot, 1] = cs2_t
        cs_scr[slot, 2] = dt_t
        dee_scr[slot] = dte_e
        dce_scr[slot] = dec_e
        if with_y:
            ecs_scr[slot] = ecs_e

    def terms_of(ref, sub):
        rows = pl.ds(pl.multiple_of(sub * q, q), q)
        return _ssd_chunk_terms(ref[rows, :], dtb_ref[...], alog_ref[...], exp_ref[...], tri, fwd)

    def sub_of(j):
        return jnp.where(fwd, j, n_sub - 1 - j)

    @pl.when(c == 0)
    def _():
        s_scr[...] = h0_ref[...]
        store_terms(0, terms_of(dt_ref, sub_of(0)))

    for j in range(n_sub):
        slot = j % 2
        rows = pl.ds(pl.multiple_of(sub_of(j) * q, q), q)
        cs2 = cs_scr[slot, 0]
        cs2_t = cs_scr[slot, 1]
        dt_t = cs_scr[slot, 2]
        if j + 1 < n_sub:
            store_terms(1 - slot, terms_of(dt_ref, sub_of(j + 1)))
        else:
            store_terms(1 - slot, terms_of(dtn_ref, sub_of(0)))

        for g in range(N_GROUPS):
            cols = slice(g * gw, (g + 1) * gw)
            bg = b_ref[rows, g * D_STATE:(g + 1) * D_STATE]
            xg = x_ref[rows, cols]
            s_old = s_scr[g]
            if with_y:
                cg = c_ref[rows, g * D_STATE:(g + 1) * D_STATE]
                cb = lax.dot_general(cg, bg, (((1,), (1,)), ((), ())), preferred_element_type=F32)
                y_off = jnp.dot(cg, s_old.astype(BF16), preferred_element_type=F32)
                for k in range(HEADS_PER_GROUP // 2):
                    pair = g * (HEADS_PER_GROUP // 2) + k
                    lanes = slice(pair * LANES, (pair + 1) * LANES)
                    xp = xg[:, k * LANES:(k + 1) * LANES]
                    xz = jnp.zeros_like(xp)
                    lhs = []
                    for hh in (2 * pair, 2 * pair + 1):
                        colb = jnp.broadcast_to(cs2[:, hh:hh + 1], (q, q))
                        rowb = cs2_t[hh:hh + 1, :]
                        decay = jnp.exp2(jnp.where(tri, colb - rowb, -jnp.inf))
                        lhs.append((cb * decay * dt_t[hh:hh + 1, :]).astype(BF16))
                    y_diag = jnp.dot(jnp.concatenate(lhs, axis=1),
                                     jnp.concatenate([jnp.where(first_half, xp, xz), jnp.where(first_half, xz, xp)],
                                                     axis=0),
                                     preferred_element_type=F32)
                    y = y_diag + ecs_scr[slot, :, lanes] * y_off[:, k * LANES:(k + 1) * LANES]
                    y_ref[rows, lanes] = y.astype(y_ref.dtype)
            xdte = xg * dee_scr[slot, :, cols]
            upd = lax.dot_general(bg, xdte, (((0,), (0,)), ((), ())), preferred_element_type=F32)
            s_scr[g] = s_old * dce_scr[slot, 0:1, cols] + upd

    if not with_y:
        @pl.when(c == nc - 1)
        def _():
            hout_ref[...] = s_scr[...]


def _ssd(xbc, dt_raw, dt_bias2, a_log2, h0, with_y):
    bsz, n, _ = xbc.shape
    n_sub = min(SSD_SUB, n // CHUNK)
    assert n_sub % 2 == 0 and n % (n_sub * CHUNK) == 0
    blk = n_sub * CHUNK
    nc = n // blk
    d_inner = N_HEADS * HEAD_DIM
    gw = HEADS_PER_GROUP * HEAD_DIM
    x_blk = d_inner // BC_WIDTH

    def ceff(dd, cc):
        return cc + dd * (nc - 1 - 2 * cc)

    def cnext(dd, cc):
        return ceff(dd, jnp.minimum(cc + 1, nc - 1))

    in_specs = [pl.BlockSpec((None, blk, d_inner), lambda b, dd, cc: (b, ceff(dd, cc), 0)),
                pl.BlockSpec((None, blk, BC_WIDTH), lambda b, dd, cc: (b, ceff(dd, cc), x_blk + dd))]
    args = [xbc, xbc]
    if with_y:
        in_specs.append(pl.BlockSpec((None, blk, BC_WIDTH), lambda b, dd, cc: (b, ceff(dd, cc), x_blk + 2 + dd)))
        args.append(xbc)
    in_specs += [pl.BlockSpec((None, blk, LANES), lambda b, dd, cc: (b, ceff(dd, cc), dd)),
                 pl.BlockSpec((None, blk, LANES), lambda b, dd, cc: (b, cnext(dd, cc), dd)),
                 pl.BlockSpec((None, 1, LANES), lambda b, dd, cc: (dd, 0, 0)),
                 pl.BlockSpec((None, 1, LANES), lambda b, dd, cc: (dd, 0, 0)),
                 pl.BlockSpec((LANES, d_inner), lambda b, dd, cc: (0, 0)),
                 pl.BlockSpec((None, None, N_GROUPS, D_STATE, gw), lambda b, dd, cc: (b, dd, 0, 0, 0))]
    args += [dt_raw, dt_raw, dt_bias2, a_log2, _head_expand_matrix(), h0]
    scratch = [pltpu.VMEM((N_GROUPS, D_STATE, gw), F32),
               pltpu.VMEM((2, 3, CHUNK, LANES), F32),
               pltpu.VMEM((2, CHUNK, d_inner), BF16),
               pltpu.VMEM((2, 8, d_inner), F32)]
    if with_y:
        out_specs = pl.BlockSpec((None, None, blk, d_inner), lambda b, dd, cc: (dd, b, ceff(dd, cc), 0))
        out_shape = jax.ShapeDtypeStruct((2, bsz, n, d_inner), BF16)
        scratch += [pltpu.VMEM((2, CHUNK, d_inner), F32)]
    else:
        out_specs = pl.BlockSpec((None, None, N_GROUPS, D_STATE, gw), lambda b, dd, cc: (b, dd, 0, 0, 0))
        out_shape = jax.ShapeDtypeStruct((bsz, 2, N_GROUPS, D_STATE, gw), F32)
    return pl.pallas_call(
        functools.partial(_ssd_kernel, with_y=with_y, nc=nc, n_sub=n_sub),
        grid=(bsz, 2, nc),
        in_specs=in_specs,
        out_specs=out_specs,
        out_shape=out_shape,
        scratch_shapes=scratch,
        compiler_params=_cparams("arbitrary", "arbitrary", "arbitrary"),
        name="ssd_scan" if with_y else "ssd_ctx_state",
    )(*args)


def _dft_mats(n):
    k = np.arange(n)
    ang = 2.0 * np.pi * ((k[:, None] * k[None, :]) % n) / n
    return np.cos(ang), np.sin(ang)


def _strided_rows(scr, t, start, count, stride, width):
    return jnp.concatenate([scr[t, lb, pl.ds(start, count, stride=stride), :] for lb in range(width // LANES)],
                           axis=1)


def _fnet_stage1_kernel(u_ref, chan_ref, m1_ref, twc_ref, tws_ref, o_ref, zs_scr, *, tcs):
    r = u_ref.shape[0]
    width = u_ref.shape[-1]
    u2 = u_ref[...].reshape(r * tcs, width)
    chan = chan_ref[...]
    for g in range(F_GROUPS):
        pq = jnp.dot(u2[:, g * F_GROUP_DIM:(g + 1) * F_GROUP_DIM], chan, preferred_element_type=F32)
        zs_scr[0, g] = pq[:, :F_GROUP_DIM]
        zs_scr[1, g] = pq[:, F_GROUP_DIM:]
    m1 = m1_ref[...]
    for j in range(tcs):
        zz = jnp.concatenate([_strided_rows(zs_scr, 0, j, r, tcs, width),
                              _strided_rows(zs_scr, 1, j, r, tcs, width)], axis=0).astype(BF16)
        u = jnp.dot(m1, zz, preferred_element_type=F32)
        ur = u[:r]
        ui = u[r:]
        tc = twc_ref[j]
        ts = tws_ref[j]
        o_ref[0, j] = (ur * tc + ui * ts).astype(o_ref.dtype)
        o_ref[1, j] = (ui * tc - ur * ts).astype(o_ref.dtype)


def _fnet_stage2_kernel(u_ref, k2_ref, o_ref):
    c, tbs, width = o_ref.shape
    uu = jnp.concatenate([u_ref[0].reshape(c * tbs, width), u_ref[1].reshape(c * tbs, width)], axis=0)
    out = jnp.dot(k2_ref[...], uu, preferred_element_type=F32)
    o_ref[...] = out.reshape(c, tbs, width).astype(o_ref.dtype)


def _fnet_mix(p2d, col_blk, bsz, n):
    m = p2d.shape[0]
    width = F_GROUPS * F_GROUP_DIM
    cgrid = GRID_W
    rgrid = n // cgrid
    scale = 1.0 / math.sqrt(n * F_GROUP_DIM)

    cc, sc = _dft_mats(F_GROUP_DIM)
    chan = jnp.asarray(np.concatenate([cc, sc], axis=1) * scale, BF16)

    cr, sr = _dft_mats(rgrid)
    m1 = jnp.asarray(np.block([[cr, -sr], [-sr, -cr]]), BF16)
    bb = np.arange(rgrid)[None, :]
    ci = np.arange(cgrid)[:, None]
    ang = 2.0 * np.pi * ((ci * bb) % n) / n
    twc = jnp.asarray(np.cos(ang)[:, :, None], F32)
    tws = jnp.asarray(np.sin(ang)[:, :, None], F32)
    tcs = 16
    p4 = p2d.reshape(bsz, rgrid, cgrid, p2d.shape[-1])
    u5 = pl.pallas_call(
        functools.partial(_fnet_stage1_kernel, tcs=tcs),
        grid=(bsz, cgrid // tcs),
        in_specs=[pl.BlockSpec((None, rgrid, tcs, width), lambda b, j: (b, 0, j, col_blk)),
                  pl.BlockSpec((F_GROUP_DIM, 2 * F_GROUP_DIM), lambda b, j: (0, 0)),
                  pl.BlockSpec((2 * rgrid, 2 * rgrid), lambda b, j: (0, 0)),
                  pl.BlockSpec((tcs, rgrid, 1), lambda b, j: (j, 0, 0)),
                  pl.BlockSpec((tcs, rgrid, 1), lambda b, j: (j, 0, 0))],
        out_specs=pl.BlockSpec((2, None, tcs, rgrid, width), lambda b, j: (0, b, j, 0, 0)),
        out_shape=jax.ShapeDtypeStruct((2, bsz, cgrid, rgrid, width), BF16),
        scratch_shapes=[pltpu.VMEM((2, width // LANES, rgrid * tcs, LANES), F32)],
        compiler_params=_cparams("arbitrary", "arbitrary"),
        name="fnet_stage1",
    )(p4, chan, m1, twc, tws)

    cc2, sc2 = _dft_mats(cgrid)
    tbs = min(16, rgrid)
    eye = np.eye(tbs)
    k2 = jnp.asarray(np.concatenate([np.kron(cc2, eye), np.kron(sc2, eye)], axis=1), BF16)
    out = pl.pallas_call(
        _fnet_stage2_kernel,
        grid=(bsz, rgrid // tbs),
        in_specs=[pl.BlockSpec((2, None, cgrid, tbs, width), lambda b, j: (0, b, 0, j, 0)),
                  pl.BlockSpec(k2.shape, lambda b, j: (0, 0))],
        out_specs=pl.BlockSpec((None, cgrid, tbs, width), lambda b, j: (b, 0, j, 0)),
        out_shape=jax.ShapeDtypeStruct((bsz, cgrid, rgrid, width), BF16),
        compiler_params=_cparams("arbitrary", "arbitrary"),
        name="fnet_stage2",
    )(u5, k2)
    return out.reshape(m, width)


def _tail_kernel(yf_ref, yb_ref, xs_ref, z_ref, ff_ref, gf_ref, gs_ref, x_ref, g1_ref, sc2_ref, sh2_ref,
                 dsk_ref, snw_ref, wso_ref, wfo_ref, wo_ref, n2w_ref, wr_ref, br_ref,
                 x1_ref, h2_ref, ti_ref, tw_ref, rk_ref, cnt_ref, carry_scr):
    y = (yf_ref[0] + yb_ref[0]).astype(F32) + dsk_ref[...] * xs_ref[...].astype(F32)
    g = y * _silu(z_ref[...]).astype(F32)
    inv = lax.rsqrt(jnp.mean(g * g, axis=-1, keepdims=True) + EPS)
    gn = (g * inv * snw_ref[...]).astype(BF16)
    y_ssd = jnp.dot(gn, wso_ref[...], preferred_element_type=F32)
    y_four = jnp.dot(ff_ref[...], wfo_ref[...], preferred_element_type=F32)
    t = _sigmoid(gf_ref[...]) * y_four.astype(BF16) + _sigmoid(gs_ref[...]) * y_ssd.astype(BF16)
    mix = jnp.dot(t, wo_ref[...], preferred_element_type=F32)
    x1 = x_ref[...] + g1_ref[...] * mix
    x1_ref[...] = x1
    inv2 = lax.rsqrt(jnp.mean(x1 * x1, axis=-1, keepdims=True) + EPS)
    h2 = (x1 * inv2 * n2w_ref[...]) * (1.0 + sc2_ref[...]) + sh2_ref[...]
    h2_ref[...] = _pack_bf16_pair(h2)
    h_hi = h2.astype(BF16)
    h_lo = (h2 - h_hi.astype(F32)).astype(BF16)
    wr = wr_ref[...]
    w_hi = wr.astype(BF16)
    w_lo = (wr - w_hi.astype(F32)).astype(BF16)
    logits = (jnp.dot(h_hi, w_hi, preferred_element_type=F32)
              + jnp.dot(h_hi, w_lo, preferred_element_type=F32)
              + jnp.dot(h_lo, w_hi, preferred_element_type=F32)) + br_ref[...]
    tm = logits.shape[0]
    lane = lax.broadcasted_iota(jnp.int32, (tm, LANES), 1)
    vals, idxs = [], []
    cur = logits
    for _ in range(TOP_K):
        mx = jnp.max(cur, axis=-1, keepdims=True)
        ix = jnp.min(jnp.where(cur == mx, lane, LANES), axis=-1, keepdims=True)
        vals.append(mx)
        idxs.append(ix)
        cur = jnp.where(lane == ix, NEG_BIG * 2.0, cur)
    es = [jnp.exp(v - vals[0]) for v in vals]
    den = es[0] + es[1] + es[2] + es[3]
    ti = jnp.zeros((tm, LANES), jnp.int32)
    tw = jnp.zeros((tm, LANES), F32)
    for k in range(TOP_K):
        ti = jnp.where(lane == k, idxs[k], ti)
        tw = jnp.where(lane == k, es[k] / den, tw)
    ti_ref[...] = ti.T[:8, :]
    tw_ref[...] = tw

    @pl.when(pl.program_id(0) == 0)
    def _():
        carry_scr[...] = jnp.zeros(carry_scr.shape, F32)

    onehots = [jnp.where(lane == ix, 1.0, 0.0) for ix in idxs]
    cnt = onehots[0] + onehots[1] + onehots[2] + onehots[3]
    r_i = lax.broadcasted_iota(jnp.int32, (tm, tm), 0)
    c_i = lax.broadcasted_iota(jnp.int32, (tm, tm), 1)
    earlier = jnp.where(r_i > c_i, 1.0, 0.0).astype(BF16)
    before = jnp.dot(earlier, cnt.astype(BF16), preferred_element_type=F32) + carry_scr[0:1, :]
    rk = jnp.zeros((tm, LANES), jnp.int32)
    for k in range(TOP_K):
        rank_k = jnp.sum(onehots[k] * before, axis=-1, keepdims=True)
        rk = jnp.where(lane == k, rank_k.astype(jnp.int32), rk)
    rk_ref[...] = rk.T[:8, :]
    total = carry_scr[...] + jnp.sum(cnt, axis=0, keepdims=True)
    carry_scr[...] = total
    cnt_ref[...] = total


def _tail(y2, xbc2d, p2d, ff, x2d, g1, sc2, sh2, dsk, snw, wso, wfo, wo, n2w, wr, br, rows_per_mod, tm):
    m, d = x2d.shape
    di = y2.shape[-1]
    zb = (2 * di) // di
    fb = (3 * di) // d
    row = lambda i: (i, 0)
    modrow = lambda i: ((i * tm) // rows_per_mod, 0, 0)
    const = lambda i: (0, 0)
    in_specs = [pl.BlockSpec((1, tm, di), lambda i: (0, i, 0)),
                pl.BlockSpec((1, tm, di), lambda i: (1, i, 0)),
                pl.BlockSpec((tm, di), row),
                pl.BlockSpec((tm, di), lambda i: (i, zb)),
                pl.BlockSpec((tm, d), row),
                pl.BlockSpec((tm, d), lambda i: (i, fb + 1)),
                pl.BlockSpec((tm, d), lambda i: (i, fb + 2)),
                pl.BlockSpec((tm, d), row),
                pl.BlockSpec((None, 1, d), modrow),
                pl.BlockSpec((None, 1, d), modrow),
                pl.BlockSpec((None, 1, d), modrow),
                pl.BlockSpec((1, di), const),
                pl.BlockSpec((1, di), const),
                pl.BlockSpec((di, d), const),
                pl.BlockSpec((d, d), const),
                pl.BlockSpec((d, d), const),
                pl.BlockSpec((1, d), const),
                pl.BlockSpec((d, LANES), const),
                pl.BlockSpec((1, LANES), const)]
    tcol = lambda i: (0, i)
    out_specs = [pl.BlockSpec((tm, d), row), pl.BlockSpec((tm, d // 2), row),
                 pl.BlockSpec((8, tm), tcol), pl.BlockSpec((tm, LANES), row), pl.BlockSpec((8, tm), tcol),
                 pl.BlockSpec((8, LANES), const)]
    out_shape = [jax.ShapeDtypeStruct((m, d), F32), jax.ShapeDtypeStruct((m, d // 2), jnp.uint32),
                 jax.ShapeDtypeStruct((8, m), jnp.int32), jax.ShapeDtypeStruct((m, LANES), F32),
                 jax.ShapeDtypeStruct((8, m), jnp.int32), jax.ShapeDtypeStruct((8, LANES), F32)]
    return pl.pallas_call(
        _tail_kernel,
        grid=(m // tm,),
        in_specs=in_specs,
        out_specs=out_specs,
        out_shape=out_shape,
        scratch_shapes=[pltpu.VMEM((8, LANES), F32)],
        compiler_params=_cparams("arbitrary"),
        name="tail",
    )(y2, y2, xbc2d, p2d, ff, p2d, p2d, x2d, g1, sc2, sh2, dsk, snw, wso, wfo, wo, n2w, wr, br)


def _gather_rows(src, idx):
    n_out = idx.shape[0]
    width = src.shape[1]
    win = SC_GATHER_WINDOW
    info = plsc.get_sparse_core_info()
    n_cores, n_workers = info.num_cores, info.num_cores * info.num_subcores
    per_worker = n_out // n_workers
    assert per_worker * n_workers == n_out and per_worker % (2 * win) == 0
    mesh = plsc.VectorSubcoreMesh(core_axis_name="core", subcore_axis_name="subcore")

    @functools.partial(
        pl.kernel, out_type=jax.ShapeDtypeStruct((n_out, width), src.dtype), mesh=mesh,
        scratch_types=[pltpu.VMEM((win,), jnp.int32), pltpu.VMEM((win,), jnp.int32),
                       pltpu.VMEM((win, width), src.dtype), pltpu.VMEM((win, width), src.dtype),
                       pltpu.SemaphoreType.DMA, pltpu.SemaphoreType.DMA,
                       pltpu.SemaphoreType.DMA, pltpu.SemaphoreType.DMA],
        name="gather_rows")
    def gather(src_hbm, idx_hbm, out_hbm, idx_a, idx_b, rows_a, rows_b, gsem_a, gsem_b, ssem_a, ssem_b):
        worker = lax.axis_index("subcore") * n_cores + lax.axis_index("core")
        base = worker * per_worker

        @pl.loop(0, per_worker, step=2 * win)
        def _(off):
            pltpu.sync_copy(idx_hbm.at[pl.ds(base + off, win)], idx_a)
            ga = pltpu.async_copy(src_hbm.at[idx_a], rows_a, gsem_a)
            pltpu.sync_copy(idx_hbm.at[pl.ds(base + off + win, win)], idx_b)
            gb = pltpu.async_copy(src_hbm.at[idx_b], rows_b, gsem_b)
            ga.wait()
            sa = pltpu.async_copy(rows_a, out_hbm.at[pl.ds(base + off, win)], ssem_a)
            gb.wait()
            sb = pltpu.async_copy(rows_b, out_hbm.at[pl.ds(base + off + win, win)], ssem_b)
            sa.wait()
            sb.wait()

    return gather(src, idx)


def _scatter_rows(src, dest, n_rows):
    m, width = src.shape
    win = SC_GATHER_WINDOW
    info = plsc.get_sparse_core_info()
    n_cores, n_workers = info.num_cores, info.num_cores * info.num_subcores
    per_worker = m // n_workers
    assert per_worker * n_workers == m and per_worker % win == 0
    mesh = plsc.VectorSubcoreMesh(core_axis_name="core", subcore_axis_name="subcore")

    @functools.partial(
        pl.kernel, out_type=jax.ShapeDtypeStruct((n_rows, width), src.dtype), mesh=mesh,
        scratch_types=[pltpu.VMEM((win, width), src.dtype)]
        + [pltpu.VMEM((win,), jnp.int32)] * TOP_K + [pltpu.SemaphoreType.DMA] * (TOP_K + 1),
        name="scatter_rows")
    def scatter(src_hbm, dest_hbm, out_hbm, rows_v, *rest):
        idx_v, sems, row_sem = rest[:TOP_K], rest[TOP_K:2 * TOP_K], rest[2 * TOP_K]
        worker = lax.axis_index("subcore") * n_cores + lax.axis_index("core")
        base = worker * per_worker

        @pl.loop(0, per_worker, step=win)
        def _(off):
            t0 = base + off
            loads = [pltpu.async_copy(src_hbm.at[pl.ds(t0, win)], rows_v, row_sem)]
            loads += [pltpu.async_copy(dest_hbm.at[pl.ds(k * m + t0, win)], idx_v[k], sems[k]) for k in range(TOP_K)]
            for ld in loads:
                ld.wait()
            copies = [pltpu.async_copy(rows_v, out_hbm.at[idx_v[k]], sems[k]) for k in range(TOP_K)]
            for cp in copies:
                cp.wait()

    return scatter(src, dest)


def _expert_kernel(te_ref, nu_ref, tv_ref, x_ref, wgu_ref, bgu_ref, wd_ref, bd_ref, o_ref, wgu_scr, wd_scr):
    i = pl.program_id(0)
    used = i < nu_ref[0]
    new_expert = (i == 0) | (te_ref[i] != te_ref[jnp.maximum(i - 1, 0)])

    @pl.when(used & new_expert)
    def _():
        wgu_scr[...] = wgu_ref[...].astype(BF16)
        wd_scr[...] = wd_ref[...].astype(BF16)

    def mlp(n_rows):
        dff = wd_ref.shape[0]
        xin = x_ref[:n_rows, :]
        rows = lax.broadcasted_iota(jnp.int32, xin.shape, 0)
        xa, xb = _unpack_bf16_pair(jnp.where(rows < tv_ref[i], xin, jnp.uint32(0)))
        x = jnp.concatenate([xa.astype(BF16), xb.astype(BF16)], axis=1)
        gu = jnp.dot(x, wgu_scr[...], preferred_element_type=F32) + bgu_ref[...]
        gate = jnp.minimum(gu[:, :dff], SWIGLU_LIMIT)
        up = jnp.clip(gu[:, dff:], -SWIGLU_LIMIT, SWIGLU_LIMIT)
        act = (up + 1.0) * gate * _sigmoid(SWIGLU_ALPHA * gate)
        y = jnp.dot(act.astype(BF16), wd_scr[...], preferred_element_type=F32) + bd_ref[...]
        o_ref[:n_rows, :] = _pack_bf16_pair(y)
        if n_rows < o_ref.shape[0]:
            o_ref[n_rows:, :] = jnp.zeros((o_ref.shape[0] - n_rows, o_ref.shape[1]), o_ref.dtype)

    half_rows = x_ref.shape[0] // 2

    @pl.when(used & (tv_ref[i] > half_rows))
    def _():
        mlp(x_ref.shape[0])

    @pl.when(used & (tv_ref[i] <= half_rows))
    def _():
        mlp(half_rows)

    @pl.when(jnp.logical_not(used))
    def _():
        o_ref[...] = jnp.zeros(o_ref.shape, o_ref.dtype)


def _experts(xs, tile_expert, n_used, tile_valid, wgu, bgu, wd, bd):
    rows, dh = xs.shape
    d = 2 * dh
    tm = MOE_TILE
    n_tiles = rows // tm
    dff2 = wgu.shape[-1]

    def tile(i, te, nu, tv):
        return (jnp.minimum(i, nu[0] - 1), 0)

    def wsel(i, te, nu, tv):
        return (te[jnp.minimum(i, nu[0] - 1)], 0, 0)

    grid_spec = pltpu.PrefetchScalarGridSpec(
        num_scalar_prefetch=3,
        grid=(n_tiles,),
        in_specs=[pl.BlockSpec((tm, dh), tile),
                  pl.BlockSpec((None, d, dff2), wsel),
                  pl.BlockSpec((None, 1, dff2), wsel),
                  pl.BlockSpec((None, dff2 // 2, d), wsel),
                  pl.BlockSpec((None, 1, d), wsel)],
        out_specs=pl.BlockSpec((tm, dh), lambda i, te, nu, tv: (i, 0)),
        scratch_shapes=[pltpu.VMEM((d, dff2), BF16), pltpu.VMEM((dff2 // 2, d), BF16)],
    )
    return pl.pallas_call(
        _expert_kernel,
        grid_spec=grid_spec,
        out_shape=jax.ShapeDtypeStruct((rows, dh), jnp.uint32),
        compiler_params=_cparams("arbitrary"),
        name="experts",
    )(tile_expert, n_used, tile_valid, xs, wgu, bgu, wd, bd)


def _final_kernel(x1_ref, ya_ref, yb_ref, yc_ref, yd_ref, tw_ref, g2_ref, fw_ref, o_ref):
    d = x1_ref.shape[-1]
    half = d // 2
    tw = tw_ref[...]
    acc_hi = jnp.zeros((x1_ref.shape[0], half), F32)
    acc_lo = jnp.zeros((x1_ref.shape[0], half), F32)
    for k, y_ref in enumerate((ya_ref, yb_ref, yc_ref, yd_ref)):
        y_hi, y_lo = _unpack_bf16_pair(y_ref[...])
        acc_hi = acc_hi + tw[:, k:k + 1] * y_hi
        acc_lo = acc_lo + tw[:, k:k + 1] * y_lo
    x_hi = x1_ref[:, :half] + g2_ref[:, :half] * acc_hi
    x_lo = x1_ref[:, half:] + g2_ref[:, half:] * acc_lo
    ms = (jnp.sum(x_hi * x_hi, axis=-1, keepdims=True) + jnp.sum(x_lo * x_lo, axis=-1, keepdims=True)) / d
    inv = lax.rsqrt(ms + EPS)
    o_ref[:, :half] = x_hi * inv * fw_ref[:, :half]
    o_ref[:, half:] = x_lo * inv * fw_ref[:, half:]


def _final(x1, y4, tw, g2, fw, rows_per_mod, tm):
    m, d = x1.shape
    return pl.pallas_call(
        _final_kernel,
        grid=(m // tm,),
        in_specs=[pl.BlockSpec((tm, d), lambda i: (i, 0))]
        + [pl.BlockSpec((None, tm, d // 2), functools.partial(lambda k, i: (k, i, 0), k)) for k in range(TOP_K)]
        + [pl.BlockSpec((tm, LANES), lambda i: (i, 0)),
           pl.BlockSpec((None, 1, d), lambda i: ((i * tm) // rows_per_mod, 0, 0)),
           pl.BlockSpec((1, d), lambda i: (0, 0))],
        out_specs=pl.BlockSpec((tm, d), lambda i: (i, 0)),
        out_shape=jax.ShapeDtypeStruct((m, d), F32),
        compiler_params=_cparams("arbitrary"),
        name="final",
    )(x1, y4, y4, y4, y4, tw, g2, fw.reshape(1, d))


def _dispatch_plan(e, rank, counts, tm):
    n_assign = e.size
    padded = (counts + tm - 1) // tm * tm
    pad_end = jnp.cumsum(padded)
    pad_start = pad_end - padded
    dest = rank.astype(jnp.int32)
    for j in range(N_EXPERTS - 1):
        dest = dest + jnp.where(e > j, padded[j], 0).astype(jnp.int32)
    n_tiles = n_assign // tm + N_EXPERTS
    tile_start = jnp.arange(n_tiles, dtype=jnp.int32) * tm
    tile_expert = jnp.minimum(jnp.sum(pad_end[None, :] <= tile_start[:, None], axis=1), N_EXPERTS - 1).astype(jnp.int32)
    n_used = (pad_end[-1] // tm).astype(jnp.int32).reshape(1)
    tile_valid = jnp.clip(pad_start[tile_expert] + counts[tile_expert] - tile_start, 0, tm).astype(jnp.int32)
    return dest, tile_expert, n_used, tile_valid


def kernel(x, c, ctx, c_ctx, w_mod, b_mod, norm1_w, norm2_w, w_in, conv_w, conv_b, dt_bias, a_log, d_skip,
           ssd_norm_w, w_ssd_out, w_four_out, w_o, w_router, b_router, w_gate_up, b_gate_up, w_down, b_down,
           final_norm_w):
    bsz, n, d = x.shape
    n_ctx = ctx.shape[1]
    depth = w_mod.shape[0]
    assert depth == 1, "a stacked model would also need the context stream's residual update"
    d_inner = N_HEADS * HEAD_DIM
    off_dt = d_inner + 2 * BC_WIDTH
    m = bsz * n
    x2d = x.reshape(m, d)
    tm = min(512, n)

    for layer in range(depth):
        rows = -(-(bsz + 1) // 8) * 8
        c_rows = jnp.zeros((rows, d), F32).at[:bsz].set(c).at[bsz].set(c_ctx)
        mod = _modulation(c_rows, w_mod[layer], b_mod[layer])
        mods = [mod[:bsz, i * d:(i + 1) * d].reshape(bsz, 1, d) for i in range(N_MOD)]
        sh1, sc1, g1, sh2, sc2, g2 = mods
        sh1_c = mod[bsz:bsz + 1, :d].reshape(1, 1, d)
        sc1_c = mod[bsz:bsz + 1, d:2 * d].reshape(1, 1, d)

        wl = w_in[layer]
        w_head = wl[:, :off_dt].astype(BF16)
        w_rest = wl[:, off_dt + 2 * N_HEADS:].astype(BF16)
        w_dt = jnp.zeros((d, 2 * LANES), F32)
        w_dt = w_dt.at[:, :N_HEADS].set(wl[:, off_dt:off_dt + N_HEADS])
        w_dt = w_dt.at[:, LANES:LANES + N_HEADS].set(wl[:, off_dt + N_HEADS:off_dt + 2 * N_HEADS]).astype(BF16)
        pad = jnp.zeros((2, 1, LANES - N_HEADS), F32)
        dt_bias2 = jnp.concatenate([dt_bias[layer].reshape(2, 1, N_HEADS), pad], axis=-1)
        a_log2 = jnp.concatenate([a_log[layer].reshape(2, 1, N_HEADS), pad], axis=-1)

        pc, dtc = _in_proj(ctx.reshape(bsz * n_ctx, d), norm1_w[layer], sc1_c, sh1_c, [w_head], w_dt,
                           bsz * n_ctx, min(512, n_ctx))
        xbc_c = _conv_silu(pc.reshape(bsz, n_ctx, off_dt), conv_w[layer][:, :off_dt], conv_b[layer][:off_dt], off_dt)
        h0 = jnp.zeros((bsz, 2, N_GROUPS, D_STATE, HEADS_PER_GROUP * HEAD_DIM), F32)
        h_ctx = _ssd(xbc_c, dtc.reshape(bsz, n_ctx, 2 * LANES), dt_bias2, a_log2, h0, with_y=False)

        p, dtl = _in_proj(x2d, norm1_w[layer], sc1, sh1, [w_head, w_rest], w_dt, n, tm)
        n_conv = off_dt + 2 * BC_WIDTH
        xbc = _conv_silu(p.reshape(bsz, n, -1), conv_w[layer], conv_b[layer], n_conv)
        y2 = _ssd(xbc, dtl.reshape(bsz, n, 2 * LANES), dt_bias2, a_log2, h_ctx, with_y=True)
        ff = _fnet_mix(p, (n_conv + d_inner) // d, bsz, n)

        dsk = jnp.repeat(d_skip[layer].astype(F32), HEAD_DIM).reshape(1, d_inner)
        wr = jnp.zeros((d, LANES), F32).at[:, :N_EXPERTS].set(w_router[layer])
        br = jnp.full((1, LANES), NEG_BIG, F32).at[0, :N_EXPERTS].set(b_router[layer])
        x1, h2, ti, tw, rk, cnt = _tail(
            y2.reshape(2, m, d_inner), xbc.reshape(m, n_conv), p, ff, x2d, g1, sc2, sh2, dsk,
            ssd_norm_w[layer].reshape(1, d_inner), w_ssd_out[layer].astype(BF16), w_four_out[layer].astype(BF16),
            w_o[layer].astype(BF16), norm2_w[layer].reshape(1, d), wr, br, n, min(TAIL_ROWS, n))

        counts = cnt[0, :N_EXPERTS].astype(jnp.int32)
        dest, tile_expert, n_used, tile_valid = _dispatch_plan(ti[:TOP_K], rk[:TOP_K], counts, MOE_TILE)
        dest_km = dest.reshape(-1)
        xs = _scatter_rows(h2, dest_km, tile_expert.shape[0] * MOE_TILE)
        ys = _experts(xs, tile_expert, n_used, tile_valid, w_gate_up[layer],
                      b_gate_up[layer].reshape(N_EXPERTS, 1, -1), w_down[layer],
                      b_down[layer].reshape(N_EXPERTS, 1, -1))
        y4 = _gather_rows(ys, dest_km).reshape(TOP_K, m, d // 2)
        x2d = _final(x1, y4, tw, g2, final_norm_w, n, tm)
    return x2d.reshape(bsz, n, d)
```

```python
import functools
import math

import numpy as np
import jax
import jax.numpy as jnp
from jax import lax
from jax.experimental import pallas as pl
from jax.experimental.pallas import tpu as pltpu
from jax.experimental.pallas import tpu_sc as plsc

F32 = jnp.float32
BF16 = jnp.bfloat16

EPS = 1e-6
GRID_W = 64
N_MOD = 6
F_GROUPS = 8
F_GROUP_DIM = 128
HEAD_DIM = 64
N_HEADS = 32
N_GROUPS = 4
HEADS_PER_GROUP = N_HEADS // N_GROUPS
D_STATE = 128
BC_WIDTH = N_GROUPS * D_STATE
CONV_K = 5
CHUNK = 128
N_EXPERTS = 32
TOP_K = 4
SWIGLU_LIMIT = 7.0
SWIGLU_ALPHA = 1.702

LANES = 128
VMEM_LIMIT_BYTES = 56 * 1024 * 1024
NEG_BIG = -1e30
MOE_TILE = 512
TAIL_ROWS = 512
SC_GATHER_WINDOW = 64


def _cparams(*sem):
    return pltpu.CompilerParams(dimension_semantics=sem, vmem_limit_bytes=VMEM_LIMIT_BYTES)


def _sigmoid(v):
    return 1.0 / (1.0 + jnp.exp(-v))


def _silu(v):
    return v * _sigmoid(v)


def _softplus(v):
    return jnp.maximum(v, 0.0) + jnp.log(1.0 + jnp.exp(-jnp.abs(v)))


def _pack_bf16_pair(v):
    w = v.shape[1] // 2
    bits = lax.bitcast_convert_type(v.astype(BF16).astype(F32), jnp.uint32)
    return bits[:, :w] | (bits[:, w:] >> 16)


def _unpack_bf16_pair(p):
    hi = lax.bitcast_convert_type(p & jnp.uint32(0xFFFF0000), F32)
    lo = lax.bitcast_convert_type(p << 16, F32)
    return hi, lo


def _mod_kernel(c_ref, w_ref, b_ref, o_ref):
    s = _silu(c_ref[...]).astype(BF16)
    o_ref[...] = jnp.dot(s, w_ref[...].astype(BF16), preferred_element_type=F32) + b_ref[...]


def _modulation(c_rows, w_mod, b_mod):
    rows, d = c_rows.shape
    n_out = w_mod.shape[1]
    tn = 1024
    return pl.pallas_call(
        _mod_kernel,
        grid=(n_out // tn,),
        in_specs=[pl.BlockSpec((rows, d), lambda j: (0, 0)),
                  pl.BlockSpec((d, tn), lambda j: (0, j)),
                  pl.BlockSpec((1, tn), lambda j: (0, j))],
        out_specs=pl.BlockSpec((rows, tn), lambda j: (0, j)),
        out_shape=jax.ShapeDtypeStruct((rows, n_out), F32),
        compiler_params=_cparams("arbitrary"),
        name="modulation",
    )(c_rows, w_mod, b_mod.reshape(1, n_out))


_INPROJ_TN = 1024


def _inproj_kernel(x_ref, nw_ref, sc_ref, sh_ref, *rest):
    *w_refs, wdt_ref, p_ref, dt_ref = rest
    x = x_ref[...]
    inv = lax.rsqrt(jnp.mean(x * x, axis=-1, keepdims=True) + EPS)
    h = (x * inv * nw_ref[...]) * (1.0 + sc_ref[...]) + sh_ref[...]
    hb = h.astype(BF16)
    dt_ref[...] = jnp.dot(hb, wdt_ref[...], preferred_element_type=F32)
    out_col = 0
    for w_ref in w_refs:
        for j in range(w_ref.shape[1] // _INPROJ_TN):
            cols = slice(j * _INPROJ_TN, (j + 1) * _INPROJ_TN)
            p_ref[:, out_col:out_col + _INPROJ_TN] = jnp.dot(
                hb, w_ref[:, cols], preferred_element_type=F32).astype(p_ref.dtype)
            out_col += _INPROJ_TN


def _in_proj(x2d, norm_w, sc, sh, w_parts, w_dt, rows_per_mod, tm):
    m, d = x2d.shape
    n = sum(w.shape[1] for w in w_parts)
    ndt = w_dt.shape[1]
    resident = dict(pipeline_mode=pl.Buffered(1))
    return pl.pallas_call(
        _inproj_kernel,
        grid=(m // tm,),
        in_specs=[pl.BlockSpec((tm, d), lambda i: (i, 0)),
                  pl.BlockSpec((1, d), lambda i: (0, 0), **resident),
                  pl.BlockSpec((None, 1, d), lambda i: ((i * tm) // rows_per_mod, 0, 0)),
                  pl.BlockSpec((None, 1, d), lambda i: ((i * tm) // rows_per_mod, 0, 0))]
        + [pl.BlockSpec(w.shape, lambda i: (0, 0), **resident) for w in w_parts]
        + [pl.BlockSpec((d, ndt), lambda i: (0, 0), **resident)],
        out_specs=[pl.BlockSpec((tm, n), lambda i: (i, 0)),
                   pl.BlockSpec((tm, ndt), lambda i: (i, 0))],
        out_shape=[jax.ShapeDtypeStruct((m, n), BF16),
                   jax.ShapeDtypeStruct((m, ndt), F32)],
        compiler_params=_cparams("arbitrary"),
        name="in_proj",
    )(x2d, norm_w.reshape(1, d), sc, sh, *w_parts, w_dt)


_CONV_HALO = 16


_CONV_ROWS = 64


def _conv_shift_matrix(rc):
    win = rc + 2 * _CONV_HALO
    s = np.zeros((rc, CONV_K * win), np.float32)
    for k in range(CONV_K):
        for l in range(rc):
            s[l, k * win + _CONV_HALO + l + k - CONV_K // 2] = 1.0
    return jnp.asarray(s, BF16)


def _conv_kernel(p_ref, w_ref, b_ref, s_ref, o_ref, *, n, rc):
    tc = o_ref.shape[-1]
    w = w_ref[...].astype(BF16)
    bias = b_ref[...]
    smat = s_ref[...]
    zeros = jnp.zeros((_CONV_HALO, tc), BF16)
    for r0 in range(0, n, rc):
        top = zeros if r0 == 0 else p_ref[r0 - _CONV_HALO:r0, :]
        bot = zeros if r0 + rc >= n else p_ref[r0 + rc:r0 + rc + _CONV_HALO, :]
        window = jnp.concatenate([top, p_ref[r0:r0 + rc, :], bot], axis=0)
        taps = jnp.concatenate([window * w[k:k + 1, :] for k in range(CONV_K)], axis=0)
        acc = jnp.dot(smat, taps, preferred_element_type=F32) + bias
        o_ref[r0:r0 + rc, :] = _silu(acc).astype(o_ref.dtype)


def _conv_silu(p3d, conv_w, conv_b, n_ch):
    bsz, n, _ = p3d.shape
    tc = 512
    rc = min(_CONV_ROWS, n)
    smat = _conv_shift_matrix(rc)
    return pl.pallas_call(
        functools.partial(_conv_kernel, n=n, rc=rc),
        grid=(bsz, n_ch // tc),
        in_specs=[pl.BlockSpec((None, n, tc), lambda b, j: (b, 0, j)),
                  pl.BlockSpec((CONV_K, tc), lambda b, j: (0, j)),
                  pl.BlockSpec((1, tc), lambda b, j: (0, j)),
                  pl.BlockSpec(smat.shape, lambda b, j: (0, 0))],
        out_specs=pl.BlockSpec((None, n, tc), lambda b, j: (b, 0, j)),
        out_shape=jax.ShapeDtypeStruct((bsz, n, n_ch), BF16),
        compiler_params=_cparams("arbitrary", "arbitrary"),
        name="conv_silu",
    )(p3d, conv_w, conv_b.reshape(1, -1), smat)


LOG2E = 1.4426950408889634
SSD_SUB = 8


def _head_expand_matrix():
    e = np.zeros((LANES, N_HEADS * HEAD_DIM), np.float32)
    for h in range(N_HEADS):
        e[h, h * HEAD_DIM:(h + 1) * HEAD_DIM] = 1.0
    return jnp.asarray(e, BF16)


def _ssd_chunk_terms(dt_raw, dt_bias, a_log, expand, tri, fwd):
    q = CHUNK
    dt = _softplus(dt_raw + dt_bias)
    da = dt * (-jnp.exp(a_log))
    hi = da.astype(BF16)
    r1 = da - hi.astype(F32)
    mid = r1.astype(BF16)
    lo = (r1 - mid.astype(F32)).astype(BF16)
    ones_tri = jnp.where(tri, 1.0, 0.0).astype(BF16)
    cs3 = jnp.dot(ones_tri, jnp.concatenate([hi, mid, lo], axis=1), preferred_element_type=F32)
    cs = cs3[:, :LANES] + cs3[:, LANES:2 * LANES] + cs3[:, 2 * LANES:]
    tot = jnp.where(fwd, cs[q - 1:q, :], cs[0:1, :])
    dte = dt * jnp.exp(tot - cs)
    dec = jnp.exp(tot)
    dec_hi = dec.astype(BF16)
    dec_lo = (dec - dec_hi.astype(F32)).astype(BF16)
    dec2 = jnp.concatenate([dec_hi, dec_lo, jnp.zeros((6, LANES), BF16)], axis=0)
    stacked = jnp.concatenate([jnp.exp(cs).astype(BF16), dte.astype(BF16), dec2], axis=0)
    spread = jnp.dot(stacked, expand, preferred_element_type=F32)
    ecs_e = spread[:q]
    dte_e = spread[q:2 * q].astype(BF16)
    dec_e = jnp.broadcast_to(spread[2 * q:2 * q + 1] + spread[2 * q + 1:2 * q + 2], (8, spread.shape[1]))
    cs2 = cs * LOG2E
    return cs2, cs2.T, dt.T, ecs_e, dte_e, dec_e


def _ssd_kernel(*refs, with_y, nc, n_sub):
    if with_y:
        (x_ref, b_ref, c_ref, dt_ref, dtn_ref, dtb_ref, alog_ref, exp_ref, h0_ref, y_ref,
         s_scr, cs_scr, dee_scr, dce_scr, ecs_scr) = refs
    else:
        (x_ref, b_ref, dt_ref, dtn_ref, dtb_ref, alog_ref, exp_ref, h0_ref, hout_ref,
         s_scr, cs_scr, dee_scr, dce_scr) = refs
    q = CHUNK
    gw = HEADS_PER_GROUP * HEAD_DIM
    d = pl.program_id(1)
    c = pl.program_id(2)
    row = lax.broadcasted_iota(jnp.int32, (q, q), 0)
    col = lax.broadcasted_iota(jnp.int32, (q, q), 1)
    fwd = d == 0
    tri = jnp.where(fwd, row - col, col - row) >= 0
    first_half = col < HEAD_DIM

    def store_terms(slot, terms):
        cs2, cs2_t, dt_t, ecs_e, dte_e, dec_e = terms
        cs_scr[slot, 0] = cs2
        cs_scr[slot, 1] = cs2_t
        cs_scr[slot, 2] = dt_t
        dee_scr[slot] = dte_e
        dce_scr[slot] = dec_e
        if with_y:
            ecs_scr[slot] = ecs_e

    def terms_of(ref, sub):
        rows = pl.ds(pl.multiple_of(sub * q, q), q)
        return _ssd_chunk_terms(ref[rows, :], dtb_ref[...], alog_ref[...], exp_ref[...], tri, fwd)

    def sub_of(j):
        return jnp.where(fwd, j, n_sub - 1 - j)

    @pl.when(c == 0)
    def _():
        s_scr[...] = h0_ref[...]
        store_terms(0, terms_of(dt_ref, sub_of(0)))

    for j in range(n_sub):
        slot = j % 2
        rows = pl.ds(pl.multiple_of(sub_of(j) * q, q), q)
        cs2 = cs_scr[slot, 0]
        cs2_t = cs_scr[slot, 1]
        dt_t = cs_scr[slot, 2]
        if j + 1 < n_sub:
            store_terms(1 - slot, terms_of(dt_ref, sub_of(j + 1)))
        else:
            store_terms(1 - slot, terms_of(dtn_ref, sub_of(0)))

        for g in range(N_GROUPS):
            cols = slice(g * gw, (g + 1) * gw)
            bg = b_ref[rows, g * D_STATE:(g + 1) * D_STATE]
            xg = x_ref[rows, cols]
            s_old = s_scr[g]
            if with_y:
                cg = c_ref[rows, g * D_STATE:(g + 1) * D_STATE]
                cb = lax.dot_general(cg, bg, (((1,), (1,)), ((), ())), preferred_element_type=F32)
                y_off = jnp.dot(cg, s_old.astype(BF16), preferred_element_type=F32)
                for k in range(HEADS_PER_GROUP // 2):
                    pair = g * (HEADS_PER_GROUP // 2) + k
                    lanes = slice(pair * LANES, (pair + 1) * LANES)
                    xp = xg[:, k * LANES:(k + 1) * LANES]
                    xz = jnp.zeros_like(xp)
                    lhs = []
                    for hh in (2 * pair, 2 * pair + 1):
                        colb = jnp.broadcast_to(cs2[:, hh:hh + 1], (q, q))
                        rowb = cs2_t[hh:hh + 1, :]
                        decay = jnp.exp2(jnp.where(tri, colb - rowb, -jnp.inf))
                        lhs.append((cb * decay * dt_t[hh:hh + 1, :]).astype(BF16))
                    y_diag = jnp.dot(jnp.concatenate(lhs, axis=1),
                                     jnp.concatenate([jnp.where(first_half, xp, xz), jnp.where(first_half, xz, xp)],
                                                     axis=0),
                                     preferred_element_type=F32)
                    y = y_diag + ecs_scr[slot, :, lanes] * y_off[:, k * LANES:(k + 1) * LANES]
                    y_ref[rows, lanes] = y.astype(y_ref.dtype)
            xdte = xg * dee_scr[slot, :, cols]
            upd = lax.dot_general(bg, xdte, (((0,), (0,)), ((), ())), preferred_element_type=F32)
            s_scr[g] = s_old * dce_scr[slot, 0:1, cols] + upd

    if not with_y:
        @pl.when(c == nc - 1)
        def _():
            hout_ref[...] = s_scr[...]


def _ssd(xbc, dt_raw, dt_bias2, a_log2, h0, with_y):
    bsz, n, _ = xbc.shape
    n_sub = min(SSD_SUB, n // CHUNK)
    assert n_sub % 2 == 0 and n % (n_sub * CHUNK) == 0
    blk = n_sub * CHUNK
    nc = n // blk
    d_inner = N_HEADS * HEAD_DIM
    gw = HEADS_PER_GROUP * HEAD_DIM
    x_blk = d_inner // BC_WIDTH

    def ceff(dd, cc):
        return cc + dd * (nc - 1 - 2 * cc)

    def cnext(dd, cc):
        return ceff(dd, jnp.minimum(cc + 1, nc - 1))

    in_specs = [pl.BlockSpec((None, blk, d_inner), lambda b, dd, cc: (b, ceff(dd, cc), 0)),
                pl.BlockSpec((None, blk, BC_WIDTH), lambda b, dd, cc: (b, ceff(dd, cc), x_blk + dd))]
    args = [xbc, xbc]
    if with_y:
        in_specs.append(pl.BlockSpec((None, blk, BC_WIDTH), lambda b, dd, cc: (b, ceff(dd, cc), x_blk + 2 + dd)))
        args.append(xbc)
    in_specs += [pl.BlockSpec((None, blk, LANES), lambda b, dd, cc: (b, ceff(dd, cc), dd)),
                 pl.BlockSpec((None, blk, LANES), lambda b, dd, cc: (b, cnext(dd, cc), dd)),
                 pl.BlockSpec((None, 1, LANES), lambda b, dd, cc: (dd, 0, 0)),
                 pl.BlockSpec((None, 1, LANES), lambda b, dd, cc: (dd, 0, 0)),
                 pl.BlockSpec((LANES, d_inner), lambda b, dd, cc: (0, 0)),
                 pl.BlockSpec((None, None, N_GROUPS, D_STATE, gw), lambda b, dd, cc: (b, dd, 0, 0, 0))]
    args += [dt_raw, dt_raw, dt_bias2, a_log2, _head_expand_matrix(), h0]
    scratch = [pltpu.VMEM((N_GROUPS, D_STATE, gw), F32),
               pltpu.VMEM((2, 3, CHUNK, LANES), F32),
               pltpu.VMEM((2, CHUNK, d_inner), BF16),
               pltpu.VMEM((2, 8, d_inner), F32)]
    if with_y:
        out_specs = pl.BlockSpec((None, None, blk, d_inner), lambda b, dd, cc: (dd, b, ceff(dd, cc), 0))
        out_shape = jax.ShapeDtypeStruct((2, bsz, n, d_inner), BF16)
        scratch += [pltpu.VMEM((2, CHUNK, d_inner), F32)]
    else:
        out_specs = pl.BlockSpec((None, None, N_GROUPS, D_STATE, gw), lambda b, dd, cc: (b, dd, 0, 0, 0))
        out_shape = jax.ShapeDtypeStruct((bsz, 2, N_GROUPS, D_STATE, gw), F32)
    return pl.pallas_call(
        functools.partial(_ssd_kernel, with_y=with_y, nc=nc, n_sub=n_sub),
        grid=(bsz, 2, nc),
        in_specs=in_specs,
        out_specs=out_specs,
        out_shape=out_shape,
        scratch_shapes=scratch,
        compiler_params=_cparams("arbitrary", "arbitrary", "arbitrary"),
        name="ssd_scan" if with_y else "ssd_ctx_state",
    )(*args)


def _dft_mats(n):
    k = np.arange(n)
    ang = 2.0 * np.pi * ((k[:, None] * k[None, :]) % n) / n
    return np.cos(ang), np.sin(ang)


def _strided_rows(scr, t, start, count, stride, width):
    return jnp.concatenate([scr[t, lb, pl.ds(start, count, stride=stride), :] for lb in range(width // LANES)],
                           axis=1)


def _fnet_stage1_kernel(u_ref, chan_ref, m1_ref, twc_ref, tws_ref, o_ref, zs_scr, *, tcs):
    r = u_ref.shape[0]
    width = u_ref.shape[-1]
    u2 = u_ref[...].reshape(r * tcs, width)
    chan = chan_ref[...]
    for g in range(F_GROUPS):
        pq = jnp.dot(u2[:, g * F_GROUP_DIM:(g + 1) * F_GROUP_DIM], chan, preferred_element_type=F32)
        zs_scr[0, g] = pq[:, :F_GROUP_DIM]
        zs_scr[1, g] = pq[:, F_GROUP_DIM:]
    m1 = m1_ref[...]
    for j in range(tcs):
        zz = jnp.concatenate([_strided_rows(zs_scr, 0, j, r, tcs, width),
                              _strided_rows(zs_scr, 1, j, r, tcs, width)], axis=0).astype(BF16)
        u = jnp.dot(m1, zz, preferred_element_type=F32)
        ur = u[:r]
        ui = u[r:]
        tc = twc_ref[j]
        ts = tws_ref[j]
        o_ref[0, j] = (ur * tc + ui * ts).astype(o_ref.dtype)
        o_ref[1, j] = (ui * tc - ur * ts).astype(o_ref.dtype)


def _fnet_stage2_kernel(u_ref, k2_ref, o_ref):
    c, tbs, width = o_ref.shape
    uu = jnp.concatenate([u_ref[0].reshape(c * tbs, width), u_ref[1].reshape(c * tbs, width)], axis=0)
    out = jnp.dot(k2_ref[...], uu, preferred_element_type=F32)
    o_ref[...] = out.reshape(c, tbs, width).astype(o_ref.dtype)


def _fnet_mix(p2d, col_blk, bsz, n):
    m = p2d.shape[0]
    width = F_GROUPS * F_GROUP_DIM
    cgrid = GRID_W
    rgrid = n // cgrid
    scale = 1.0 / math.sqrt(n * F_GROUP_DIM)

    cc, sc = _dft_mats(F_GROUP_DIM)
    chan = jnp.asarray(np.concatenate([cc, sc], axis=1) * scale, BF16)

    cr, sr = _dft_mats(rgrid)
    m1 = jnp.asarray(np.block([[cr, -sr], [-sr, -cr]]), BF16)
    bb = np.arange(rgrid)[None, :]
    ci = np.arange(cgrid)[:, None]
    ang = 2.0 * np.pi * ((ci * bb) % n) / n
    twc = jnp.asarray(np.cos(ang)[:, :, None], F32)
    tws = jnp.asarray(np.sin(ang)[:, :, None], F32)
    tcs = 16
    p4 = p2d.reshape(bsz, rgrid, cgrid, p2d.shape[-1])
    u5 = pl.pallas_call(
        functools.partial(_fnet_stage1_kernel, tcs=tcs),
        grid=(bsz, cgrid // tcs),
        in_specs=[pl.BlockSpec((None, rgrid, tcs, width), lambda b, j: (b, 0, j, col_blk)),
                  pl.BlockSpec((F_GROUP_DIM, 2 * F_GROUP_DIM), lambda b, j: (0, 0)),
                  pl.BlockSpec((2 * rgrid, 2 * rgrid), lambda b, j: (0, 0)),
                  pl.BlockSpec((tcs, rgrid, 1), lambda b, j: (j, 0, 0)),
                  pl.BlockSpec((tcs, rgrid, 1), lambda b, j: (j, 0, 0))],
        out_specs=pl.BlockSpec((2, None, tcs, rgrid, width), lambda b, j: (0, b, j, 0, 0)),
        out_shape=jax.ShapeDtypeStruct((2, bsz, cgrid, rgrid, width), BF16),
        scratch_shapes=[pltpu.VMEM((2, width // LANES, rgrid * tcs, LANES), F32)],
        compiler_params=_cparams("arbitrary", "arbitrary"),
        name="fnet_stage1",
    )(p4, chan, m1, twc, tws)

    cc2, sc2 = _dft_mats(cgrid)
    tbs = min(16, rgrid)
    eye = np.eye(tbs)
    k2 = jnp.asarray(np.concatenate([np.kron(cc2, eye), np.kron(sc2, eye)], axis=1), BF16)
    out = pl.pallas_call(
        _fnet_stage2_kernel,
        grid=(bsz, rgrid // tbs),
        in_specs=[pl.BlockSpec((2, None, cgrid, tbs, width), lambda b, j: (0, b, 0, j, 0)),
                  pl.BlockSpec(k2.shape, lambda b, j: (0, 0))],
        out_specs=pl.BlockSpec((None, cgrid, tbs, width), lambda b, j: (b, 0, j, 0)),
        out_shape=jax.ShapeDtypeStruct((bsz, cgrid, rgrid, width), BF16),
        compiler_params=_cparams("arbitrary", "arbitrary"),
        name="fnet_stage2",
    )(u5, k2)
    return out.reshape(m, width)


def _tail_kernel(yf_ref, yb_ref, xs_ref, z_ref, ff_ref, gf_ref, gs_ref, x_ref, g1_ref, sc2_ref, sh2_ref,
                 dsk_ref, snw_ref, wso_ref, wfo_ref, wo_ref, n2w_ref, wr_ref, br_ref,
                 x1_ref, h2_ref, ti_ref, tw_ref, rk_ref, cnt_ref, carry_scr):
    y = (yf_ref[0] + yb_ref[0]).astype(F32) + dsk_ref[...] * xs_ref[...].astype(F32)
    g = y * _silu(z_ref[...]).astype(F32)
    inv = lax.rsqrt(jnp.mean(g * g, axis=-1, keepdims=True) + EPS)
    gn = (g * inv * snw_ref[...]).astype(BF16)
    y_ssd = jnp.dot(gn, wso_ref[...], preferred_element_type=F32)
    y_four = jnp.dot(ff_ref[...], wfo_ref[...], preferred_element_type=F32)
    t = _sigmoid(gf_ref[...]) * y_four.astype(BF16) + _sigmoid(gs_ref[...]) * y_ssd.astype(BF16)
    mix = jnp.dot(t, wo_ref[...], preferred_element_type=F32)
    x1 = x_ref[...] + g1_ref[...] * mix
    x1_ref[...] = x1
    inv2 = lax.rsqrt(jnp.mean(x1 * x1, axis=-1, keepdims=True) + EPS)
    h2 = (x1 * inv2 * n2w_ref[...]) * (1.0 + sc2_ref[...]) + sh2_ref[...]
    h2_ref[...] = _pack_bf16_pair(h2)
    h_hi = h2.astype(BF16)
    h_lo = (h2 - h_hi.astype(F32)).astype(BF16)
    wr = wr_ref[...]
    w_hi = wr.astype(BF16)
    w_lo = (wr - w_hi.astype(F32)).astype(BF16)
    logits = (jnp.dot(h_hi, w_hi, preferred_element_type=F32)
              + jnp.dot(h_hi, w_lo, preferred_element_type=F32)
              + jnp.dot(h_lo, w_hi, preferred_element_type=F32)) + br_ref[...]
    tm = logits.shape[0]
    lane = lax.broadcasted_iota(jnp.int32, (tm, LANES), 1)
    vals, idxs = [], []
    cur = logits
    for _ in range(TOP_K):
        mx = jnp.max(cur, axis=-1, keepdims=True)
        ix = jnp.min(jnp.where(cur == mx, lane, LANES), axis=-1, keepdims=True)
        vals.append(mx)
        idxs.append(ix)
        cur = jnp.where(lane == ix, NEG_BIG * 2.0, cur)
    es = [jnp.exp(v - vals[0]) for v in vals]
    den = es[0] + es[1] + es[2] + es[3]
    ti = jnp.zeros((tm, LANES), jnp.int32)
    tw = jnp.zeros((tm, LANES), F32)
    for k in range(TOP_K):
        ti = jnp.where(lane == k, idxs[k], ti)
        tw = jnp.where(lane == k, es[k] / den, tw)
    ti_ref[...] = ti.T[:8, :]
    tw_ref[...] = tw

    @pl.when(pl.program_id(0) == 0)
    def _():
        carry_scr[...] = jnp.zeros(carry_scr.shape, F32)

    onehots = [jnp.where(lane == ix, 1.0, 0.0) for ix in idxs]
    cnt = onehots[0] + onehots[1] + onehots[2] + onehots[3]
    r_i = lax.broadcasted_iota(jnp.int32, (tm, tm), 0)
    c_i = lax.broadcasted_iota(jnp.int32, (tm, tm), 1)
    earlier = jnp.where(r_i > c_i, 1.0, 0.0).astype(BF16)
    before = jnp.dot(earlier, cnt.astype(BF16), preferred_element_type=F32) + carry_scr[0:1, :]
    rk = jnp.zeros((tm, LANES), jnp.int32)
    for k in range(TOP_K):
        rank_k = jnp.sum(onehots[k] * before, axis=-1, keepdims=True)
        rk = jnp.where(lane == k, rank_k.astype(jnp.int32), rk)
    rk_ref[...] = rk.T[:8, :]
    total = carry_scr[...] + jnp.sum(cnt, axis=0, keepdims=True)
    carry_scr[...] = total
    cnt_ref[...] = total


def _tail(y2, xbc2d, p2d, ff, x2d, g1, sc2, sh2, dsk, snw, wso, wfo, wo, n2w, wr, br, rows_per_mod, tm):
    m, d = x2d.shape
    di = y2.shape[-1]
    zb = (2 * di) // di
    fb = (3 * di) // d
    row = lambda i: (i, 0)
    modrow = lambda i: ((i * tm) // rows_per_mod, 0, 0)
    const = lambda i: (0, 0)
    in_specs = [pl.BlockSpec((1, tm, di), lambda i: (0, i, 0)),
                pl.BlockSpec((1, tm, di), lambda i: (1, i, 0)),
                pl.BlockSpec((tm, di), row),
                pl.BlockSpec((tm, di), lambda i: (i, zb)),
                pl.BlockSpec((tm, d), row),
                pl.BlockSpec((tm, d), lambda i: (i, fb + 1)),
                pl.BlockSpec((tm, d), lambda i: (i, fb + 2)),
                pl.BlockSpec((tm, d), row),
                pl.BlockSpec((None, 1, d), modrow),
                pl.BlockSpec((None, 1, d), modrow),
                pl.BlockSpec((None, 1, d), modrow),
                pl.BlockSpec((1, di), const),
                pl.BlockSpec((1, di), const),
                pl.BlockSpec((di, d), const),
                pl.BlockSpec((d, d), const),
                pl.BlockSpec((d, d), const),
                pl.BlockSpec((1, d), const),
                pl.BlockSpec((d, LANES), const),
                pl.BlockSpec((1, LANES), const)]
    tcol = lambda i: (0, i)
    out_specs = [pl.BlockSpec((tm, d), row), pl.BlockSpec((tm, d // 2), row),
                 pl.BlockSpec((8, tm), tcol), pl.BlockSpec((tm, LANES), row), pl.BlockSpec((8, tm), tcol),
                 pl.BlockSpec((8, LANES), const)]
    out_shape = [jax.ShapeDtypeStruct((m, d), F32), jax.ShapeDtypeStruct((m, d // 2), jnp.uint32),
                 jax.ShapeDtypeStruct((8, m), jnp.int32), jax.ShapeDtypeStruct((m, LANES), F32),
                 jax.ShapeDtypeStruct((8, m), jnp.int32), jax.ShapeDtypeStruct((8, LANES), F32)]
    return pl.pallas_call(
        _tail_kernel,
        grid=(m // tm,),
        in_specs=in_specs,
        out_specs=out_specs,
        out_shape=out_shape,
        scratch_shapes=[pltpu.VMEM((8, LANES), F32)],
        compiler_params=_cparams("arbitrary"),
        name="tail",
    )(y2, y2, xbc2d, p2d, ff, p2d, p2d, x2d, g1, sc2, sh2, dsk, snw, wso, wfo, wo, n2w, wr, br)


def _gather_rows(src, idx):
    n_out = idx.shape[0]
    width = src.shape[1]
    win = SC_GATHER_WINDOW
    info = plsc.get_sparse_core_info()
    n_cores, n_workers = info.num_cores, info.num_cores * info.num_subcores
    per_worker = n_out // n_workers
    assert per_worker * n_workers == n_out and per_worker % (2 * win) == 0
    mesh = plsc.VectorSubcoreMesh(core_axis_name="core", subcore_axis_name="subcore")

    @functools.partial(
        pl.kernel, out_type=jax.ShapeDtypeStruct((n_out, width), src.dtype), mesh=mesh,
        scratch_types=[pltpu.VMEM((win,), jnp.int32), pltpu.VMEM((win,), jnp.int32),
                       pltpu.VMEM((win, width), src.dtype), pltpu.VMEM((win, width), src.dtype),
                       pltpu.SemaphoreType.DMA, pltpu.SemaphoreType.DMA,
                       pltpu.SemaphoreType.DMA, pltpu.SemaphoreType.DMA],
        name="gather_rows")
    def gather(src_hbm, idx_hbm, out_hbm, idx_a, idx_b, rows_a, rows_b, gsem_a, gsem_b, ssem_a, ssem_b):
        worker = lax.axis_index("subcore") * n_cores + lax.axis_index("core")
        base = worker * per_worker

        @pl.loop(0, per_worker, step=2 * win)
        def _(off):
            pltpu.sync_copy(idx_hbm.at[pl.ds(base + off, win)], idx_a)
            ga = pltpu.async_copy(src_hbm.at[idx_a], rows_a, gsem_a)
            pltpu.sync_copy(idx_hbm.at[pl.ds(base + off + win, win)], idx_b)
            gb = pltpu.async_copy(src_hbm.at[idx_b], rows_b, gsem_b)
            ga.wait()
            sa = pltpu.async_copy(rows_a, out_hbm.at[pl.ds(base + off, win)], ssem_a)
            gb.wait()
            sb = pltpu.async_copy(rows_b, out_hbm.at[pl.ds(base + off + win, win)], ssem_b)
            sa.wait()
            sb.wait()

    return gather(src, idx)


def _scatter_rows(src, dest, n_rows):
    m, width = src.shape
    win = SC_GATHER_WINDOW
    info = plsc.get_sparse_core_info()
    n_cores, n_workers = info.num_cores, info.num_cores * info.num_subcores
    per_worker = m // n_workers
    assert per_worker * n_workers == m and per_worker % win == 0
    mesh = plsc.VectorSubcoreMesh(core_axis_name="core", subcore_axis_name="subcore")

    @functools.partial(
        pl.kernel, out_type=jax.ShapeDtypeStruct((n_rows, width), src.dtype), mesh=mesh,
        scratch_types=[pltpu.VMEM((win, width), src.dtype)]
        + [pltpu.VMEM((win,), jnp.int32)] * TOP_K + [pltpu.SemaphoreType.DMA] * (TOP_K + 1),
        name="scatter_rows")
    def scatter(src_hbm, dest_hbm, out_hbm, rows_v, *rest):
        idx_v, sems, row_sem = rest[:TOP_K], rest[TOP_K:2 * TOP_K], rest[2 * TOP_K]
        worker = lax.axis_index("subcore") * n_cores + lax.axis_index("core")
        base = worker * per_worker

        @pl.loop(0, per_worker, step=win)
        def _(off):
            t0 = base + off
            loads = [pltpu.async_copy(src_hbm.at[pl.ds(t0, win)], rows_v, row_sem)]
            loads += [pltpu.async_copy(dest_hbm.at[pl.ds(k * m + t0, win)], idx_v[k], sems[k]) for k in range(TOP_K)]
            for ld in loads:
                ld.wait()
            copies = [pltpu.async_copy(rows_v, out_hbm.at[idx_v[k]], sems[k]) for k in range(TOP_K)]
            for cp in copies:
                cp.wait()

    return scatter(src, dest)


def _expert_kernel(te_ref, nu_ref, tv_ref, x_ref, wgu_ref, bgu_ref, wd_ref, bd_ref, o_ref, wgu_scr, wd_scr):
    i = pl.program_id(0)
    used = i < nu_ref[0]
    new_expert = (i == 0) | (te_ref[i] != te_ref[jnp.maximum(i - 1, 0)])

    @pl.when(used & new_expert)
    def _():
        wgu_scr[...] = wgu_ref[...].astype(BF16)
        wd_scr[...] = wd_ref[...].astype(BF16)

    @pl.when(used)
    def _():
        dff = wd_ref.shape[0]
        rows = lax.broadcasted_iota(jnp.int32, x_ref.shape, 0)
        xa, xb = _unpack_bf16_pair(jnp.where(rows < tv_ref[i], x_ref[...], jnp.uint32(0)))
        x = jnp.concatenate([xa.astype(BF16), xb.astype(BF16)], axis=1)
        gu = jnp.dot(x, wgu_scr[...], preferred_element_type=F32) + bgu_ref[...]
        gate = jnp.minimum(gu[:, :dff], SWIGLU_LIMIT)
        up = jnp.clip(gu[:, dff:], -SWIGLU_LIMIT, SWIGLU_LIMIT)
        act = (up + 1.0) * gate * _sigmoid(SWIGLU_ALPHA * gate)
        y = jnp.dot(act.astype(BF16), wd_scr[...], preferred_element_type=F32) + bd_ref[...]
        o_ref[...] = _pack_bf16_pair(y)

    @pl.when(jnp.logical_not(used))
    def _():
        o_ref[...] = jnp.zeros(o_ref.shape, o_ref.dtype)


def _experts(xs, tile_expert, n_used, tile_valid, wgu, bgu, wd, bd):
    rows, dh = xs.shape
    d = 2 * dh
    tm = MOE_TILE
    n_tiles = rows // tm
    dff2 = wgu.shape[-1]

    def tile(i, te, nu, tv):
        return (jnp.minimum(i, nu[0] - 1), 0)

    def wsel(i, te, nu, tv):
        return (te[jnp.minimum(i, nu[0] - 1)], 0, 0)

    grid_spec = pltpu.PrefetchScalarGridSpec(
        num_scalar_prefetch=3,
        grid=(n_tiles,),
        in_specs=[pl.BlockSpec((tm, dh), tile),
                  pl.BlockSpec((None, d, dff2), wsel),
                  pl.BlockSpec((None, 1, dff2), wsel),
                  pl.BlockSpec((None, dff2 // 2, d), wsel),
                  pl.BlockSpec((None, 1, d), wsel)],
        out_specs=pl.BlockSpec((tm, dh), lambda i, te, nu, tv: (i, 0)),
        scratch_shapes=[pltpu.VMEM((d, dff2), BF16), pltpu.VMEM((dff2 // 2, d), BF16)],
    )
    return pl.pallas_call(
        _expert_kernel,
        grid_spec=grid_spec,
        out_shape=jax.ShapeDtypeStruct((rows, dh), jnp.uint32),
        compiler_params=_cparams("arbitrary"),
        name="experts",
    )(tile_expert, n_used, tile_valid, xs, wgu, bgu, wd, bd)


def _final_kernel(x1_ref, ya_ref, yb_ref, yc_ref, yd_ref, tw_ref, g2_ref, fw_ref, o_ref):
    d = x1_ref.shape[-1]
    half = d // 2
    tw = tw_ref[...]
    acc_hi = jnp.zeros((x1_ref.shape[0], half), F32)
    acc_lo = jnp.zeros((x1_ref.shape[0], half), F32)
    for k, y_ref in enumerate((ya_ref, yb_ref, yc_ref, yd_ref)):
        y_hi, y_lo = _unpack_bf16_pair(y_ref[...])
        acc_hi = acc_hi + tw[:, k:k + 1] * y_hi
        acc_lo = acc_lo + tw[:, k:k + 1] * y_lo
    x_hi = x1_ref[:, :half] + g2_ref[:, :half] * acc_hi
    x_lo = x1_ref[:, half:] + g2_ref[:, half:] * acc_lo
    ms = (jnp.sum(x_hi * x_hi, axis=-1, keepdims=True) + jnp.sum(x_lo * x_lo, axis=-1, keepdims=True)) / d
    inv = lax.rsqrt(ms + EPS)
    o_ref[:, :half] = x_hi * inv * fw_ref[:, :half]
    o_ref[:, half:] = x_lo * inv * fw_ref[:, half:]


def _final(x1, y4, tw, g2, fw, rows_per_mod, tm):
    m, d = x1.shape
    return pl.pallas_call(
        _final_kernel,
        grid=(m // tm,),
        in_specs=[pl.BlockSpec((tm, d), lambda i: (i, 0))]
        + [pl.BlockSpec((None, tm, d // 2), functools.partial(lambda k, i: (k, i, 0), k)) for k in range(TOP_K)]
        + [pl.BlockSpec((tm, LANES), lambda i: (i, 0)),
           pl.BlockSpec((None, 1, d), lambda i: ((i * tm) // rows_per_mod, 0, 0)),
           pl.BlockSpec((1, d), lambda i: (0, 0))],
        out_specs=pl.BlockSpec((tm, d), lambda i: (i, 0)),
        out_shape=jax.ShapeDtypeStruct((m, d), F32),
        compiler_params=_cparams("arbitrary"),
        name="final",
    )(x1, y4, y4, y4, y4, tw, g2, fw.reshape(1, d))


def _dispatch_plan(e, rank, counts, tm):
    n_assign = e.size
    padded = (counts + tm - 1) // tm * tm
    pad_end = jnp.cumsum(padded)
    pad_start = pad_end - padded
    dest = rank.astype(jnp.int32)
    for j in range(N_EXPERTS - 1):
        dest = dest + jnp.where(e > j, padded[j], 0).astype(jnp.int32)
    n_tiles = n_assign // tm + N_EXPERTS
    tile_start = jnp.arange(n_tiles, dtype=jnp.int32) * tm
    tile_expert = jnp.minimum(jnp.sum(pad_end[None, :] <= tile_start[:, None], axis=1), N_EXPERTS - 1).astype(jnp.int32)
    n_used = (pad_end[-1] // tm).astype(jnp.int32).reshape(1)
    tile_valid = jnp.clip(pad_start[tile_expert] + counts[tile_expert] - tile_start, 0, tm).astype(jnp.int32)
    return dest, tile_expert, n_used, tile_valid


def kernel(x, c, ctx, c_ctx, w_mod, b_mod, norm1_w, norm2_w, w_in, conv_w, conv_b, dt_bias, a_log, d_skip,
           ssd_norm_w, w_ssd_out, w_four_out, w_o, w_router, b_router, w_gate_up, b_gate_up, w_down, b_down,
           final_norm_w):
    bsz, n, d = x.shape
    n_ctx = ctx.shape[1]
    depth = w_mod.shape[0]
    assert depth == 1, "a stacked model would also need the context stream's residual update"
    d_inner = N_HEADS * HEAD_DIM
    off_dt = d_inner + 2 * BC_WIDTH
    m = bsz * n
    x2d = x.reshape(m, d)
    tm = min(512, n)

    for layer in range(depth):
        rows = -(-(bsz + 1) // 8) * 8
        c_rows = jnp.zeros((rows, d), F32).at[:bsz].set(c).at[bsz].set(c_ctx)
        mod = _modulation(c_rows, w_mod[layer], b_mod[layer])
        mods = [mod[:bsz, i * d:(i + 1) * d].reshape(bsz, 1, d) for i in range(N_MOD)]
        sh1, sc1, g1, sh2, sc2, g2 = mods
        sh1_c = mod[bsz:bsz + 1, :d].reshape(1, 1, d)
        sc1_c = mod[bsz:bsz + 1, d:2 * d].reshape(1, 1, d)

        wl = w_in[layer]
        w_head = wl[:, :off_dt].astype(BF16)
        w_rest = wl[:, off_dt + 2 * N_HEADS:].astype(BF16)
        w_dt = jnp.zeros((d, 2 * LANES), F32)
        w_dt = w_dt.at[:, :N_HEADS].set(wl[:, off_dt:off_dt + N_HEADS])
        w_dt = w_dt.at[:, LANES:LANES + N_HEADS].set(wl[:, off_dt + N_HEADS:off_dt + 2 * N_HEADS]).astype(BF16)
        pad = jnp.zeros((2, 1, LANES - N_HEADS), F32)
        dt_bias2 = jnp.concatenate([dt_bias[layer].reshape(2, 1, N_HEADS), pad], axis=-1)
        a_log2 = jnp.concatenate([a_log[layer].reshape(2, 1, N_HEADS), pad], axis=-1)

        pc, dtc = _in_proj(ctx.reshape(bsz * n_ctx, d), norm1_w[layer], sc1_c, sh1_c, [w_head], w_dt,
                           bsz * n_ctx, min(512, n_ctx))
        xbc_c = _conv_silu(pc.reshape(bsz, n_ctx, off_dt), conv_w[layer][:, :off_dt], conv_b[layer][:off_dt], off_dt)
        h0 = jnp.zeros((bsz, 2, N_GROUPS, D_STATE, HEADS_PER_GROUP * HEAD_DIM), F32)
        h_ctx = _ssd(xbc_c, dtc.reshape(bsz, n_ctx, 2 * LANES), dt_bias2, a_log2, h0, with_y=False)

        p, dtl = _in_proj(x2d, norm1_w[layer], sc1, sh1, [w_head, w_rest], w_dt, n, tm)
        n_conv = off_dt + 2 * BC_WIDTH
        xbc = _conv_silu(p.reshape(bsz, n, -1), conv_w[layer], conv_b[layer], n_conv)
        y2 = _ssd(xbc, dtl.reshape(bsz, n, 2 * LANES), dt_bias2, a_log2, h_ctx, with_y=True)
        ff = _fnet_mix(p, (n_conv + d_inner) // d, bsz, n)

        dsk = jnp.repeat(d_skip[layer].astype(F32), HEAD_DIM).reshape(1, d_inner)
        wr = jnp.zeros((d, LANES), F32).at[:, :N_EXPERTS].set(w_router[layer])
        br = jnp.full((1, LANES), NEG_BIG, F32).at[0, :N_EXPERTS].set(b_router[layer])
        x1, h2, ti, tw, rk, cnt = _tail(
            y2.reshape(2, m, d_inner), xbc.reshape(m, n_conv), p, ff, x2d, g1, sc2, sh2, dsk,
            ssd_norm_w[layer].reshape(1, d_inner), w_ssd_out[layer].astype(BF16), w_four_out[layer].astype(BF16),
            w_o[layer].astype(BF16), norm2_w[layer].reshape(1, d), wr, br, n, min(TAIL_ROWS, n))

        counts = cnt[0, :N_EXPERTS].astype(jnp.int32)
        dest, tile_expert, n_used, tile_valid = _dispatch_plan(ti[:TOP_K], rk[:TOP_K], counts, MOE_TILE)
        dest_km = dest.reshape(-1)
        xs = _scatter_rows(h2, dest_km, tile_expert.shape[0] * MOE_TILE)
        ys = _experts(xs, tile_expert, n_used, tile_valid, w_gate_up[layer],
                      b_gate_up[layer].reshape(N_EXPERTS, 1, -1), w_down[layer],
                      b_down[layer].reshape(N_EXPERTS, 1, -1))
        y4 = _gather_rows(ys, dest_km).reshape(TOP_K, m, d // 2)
        x2d = _final(x1, y4, tw, g2, final_norm_w, n, tm)
    return x2d.reshape(bsz, n, d)
```

```python
import functools
import math

import numpy as np
import jax
import jax.numpy as jnp
from jax import lax
from jax.experimental import pallas as pl
from jax.experimental.pallas import tpu as pltpu
from jax.experimental.pallas import tpu_sc as plsc

F32 = jnp.float32
BF16 = jnp.bfloat16

EPS = 1e-6
GRID_W = 64
N_MOD = 6
F_GROUPS = 8
F_GROUP_DIM = 128
HEAD_DIM = 64
N_HEADS = 32
N_GROUPS = 4
HEADS_PER_GROUP = N_HEADS // N_GROUPS
D_STATE = 128
BC_WIDTH = N_GROUPS * D_STATE
CONV_K = 5
CHUNK = 128
N_EXPERTS = 32
TOP_K = 4
SWIGLU_LIMIT = 7.0
SWIGLU_ALPHA = 1.702

LANES = 128
VMEM_LIMIT_BYTES = 56 * 1024 * 1024
NEG_BIG = -1e30
MOE_TILE = 512
TAIL_ROWS = 512
SC_GATHER_WINDOW = 64


def _cparams(*sem):
    return pltpu.CompilerParams(dimension_semantics=sem, vmem_limit_bytes=VMEM_LIMIT_BYTES)


def _sigmoid(v):
    return 1.0 / (1.0 + jnp.exp(-v))


def _silu(v):
    return v * _sigmoid(v)


def _softplus(v):
    return jnp.maximum(v, 0.0) + jnp.log(1.0 + jnp.exp(-jnp.abs(v)))


def _pack_bf16_pair(v):
    w = v.shape[1] // 2
    bits = lax.bitcast_convert_type(v.astype(BF16).astype(F32), jnp.uint32)
    return bits[:, :w] | (bits[:, w:] >> 16)


def _unpack_bf16_pair(p):
    hi = lax.bitcast_convert_type(p & jnp.uint32(0xFFFF0000), F32)
    lo = lax.bitcast_convert_type(p << 16, F32)
    return hi, lo


def _mod_kernel(c_ref, w_ref, b_ref, o_ref):
    s = _silu(c_ref[...]).astype(BF16)
    o_ref[...] = jnp.dot(s, w_ref[...].astype(BF16), preferred_element_type=F32) + b_ref[...]


def _modulation(c_rows, w_mod, b_mod):
    rows, d = c_rows.shape
    n_out = w_mod.shape[1]
    tn = 1024
    return pl.pallas_call(
        _mod_kernel,
        grid=(n_out // tn,),
        in_specs=[pl.BlockSpec((rows, d), lambda j: (0, 0)),
                  pl.BlockSpec((d, tn), lambda j: (0, j)),
                  pl.BlockSpec((1, tn), lambda j: (0, j))],
        out_specs=pl.BlockSpec((rows, tn), lambda j: (0, j)),
        out_shape=jax.ShapeDtypeStruct((rows, n_out), F32),
        compiler_params=_cparams("arbitrary"),
        name="modulation",
    )(c_rows, w_mod, b_mod.reshape(1, n_out))


_INPROJ_TN = 1024


def _inproj_kernel(x_ref, nw_ref, sc_ref, sh_ref, *rest):
    *w_refs, wdt_ref, p_ref, dt_ref = rest
    x = x_ref[...]
    inv = lax.rsqrt(jnp.mean(x * x, axis=-1, keepdims=True) + EPS)
    h = (x * inv * nw_ref[...]) * (1.0 + sc_ref[...]) + sh_ref[...]
    hb = h.astype(BF16)
    dt_ref[...] = jnp.dot(hb, wdt_ref[...], preferred_element_type=F32)
    out_col = 0
    for w_ref in w_refs:
        for j in range(w_ref.shape[1] // _INPROJ_TN):
            cols = slice(j * _INPROJ_TN, (j + 1) * _INPROJ_TN)
            p_ref[:, out_col:out_col + _INPROJ_TN] = jnp.dot(
                hb, w_ref[:, cols], preferred_element_type=F32).astype(p_ref.dtype)
            out_col += _INPROJ_TN


def _in_proj(x2d, norm_w, sc, sh, w_parts, w_dt, rows_per_mod, tm):
    m, d = x2d.shape
    n = sum(w.shape[1] for w in w_parts)
    ndt = w_dt.shape[1]
    resident = dict(pipeline_mode=pl.Buffered(1))
    return pl.pallas_call(
        _inproj_kernel,
        grid=(m // tm,),
        in_specs=[pl.BlockSpec((tm, d), lambda i: (i, 0)),
                  pl.BlockSpec((1, d), lambda i: (0, 0), **resident),
                  pl.BlockSpec((None, 1, d), lambda i: ((i * tm) // rows_per_mod, 0, 0)),
                  pl.BlockSpec((None, 1, d), lambda i: ((i * tm) // rows_per_mod, 0, 0))]
        + [pl.BlockSpec(w.shape, lambda i: (0, 0), **resident) for w in w_parts]
        + [pl.BlockSpec((d, ndt), lambda i: (0, 0), **resident)],
        out_specs=[pl.BlockSpec((tm, n), lambda i: (i, 0)),
                   pl.BlockSpec((tm, ndt), lambda i: (i, 0))],
        out_shape=[jax.ShapeDtypeStruct((m, n), BF16),
                   jax.ShapeDtypeStruct((m, ndt), F32)],
        compiler_params=_cparams("arbitrary"),
        name="in_proj",
    )(x2d, norm_w.reshape(1, d), sc, sh, *w_parts, w_dt)


_CONV_HALO = 16


_CONV_ROWS = 64


def _conv_shift_matrix(rc):
    win = rc + 2 * _CONV_HALO
    s = np.zeros((rc, CONV_K * win), np.float32)
    for k in range(CONV_K):
        for l in range(rc):
            s[l, k * win + _CONV_HALO + l + k - CONV_K // 2] = 1.0
    return jnp.asarray(s, BF16)


def _conv_kernel(p_ref, w_ref, b_ref, s_ref, o_ref, *, n, rc):
    tc = o_ref.shape[-1]
    w = w_ref[...].astype(BF16)
    bias = b_ref[...]
    smat = s_ref[...]
    zeros = jnp.zeros((_CONV_HALO, tc), BF16)
    for r0 in range(0, n, rc):
        top = zeros if r0 == 0 else p_ref[r0 - _CONV_HALO:r0, :]
        bot = zeros if r0 + rc >= n else p_ref[r0 + rc:r0 + rc + _CONV_HALO, :]
        window = jnp.concatenate([top, p_ref[r0:r0 + rc, :], bot], axis=0)
        taps = jnp.concatenate([window * w[k:k + 1, :] for k in range(CONV_K)], axis=0)
        acc = jnp.dot(smat, taps, preferred_element_type=F32) + bias
        o_ref[r0:r0 + rc, :] = _silu(acc).astype(o_ref.dtype)


def _conv_silu(p3d, conv_w, conv_b, n_ch):
    bsz, n, _ = p3d.shape
    tc = 512
    rc = min(_CONV_ROWS, n)
    smat = _conv_shift_matrix(rc)
    return pl.pallas_call(
        functools.partial(_conv_kernel, n=n, rc=rc),
        grid=(bsz, n_ch // tc),
        in_specs=[pl.BlockSpec((None, n, tc), lambda b, j: (b, 0, j)),
                  pl.BlockSpec((CONV_K, tc), lambda b, j: (0, j)),
                  pl.BlockSpec((1, tc), lambda b, j: (0, j)),
                  pl.BlockSpec(smat.shape, lambda b, j: (0, 0))],
        out_specs=pl.BlockSpec((None, n, tc), lambda b, j: (b, 0, j)),
        out_shape=jax.ShapeDtypeStruct((bsz, n, n_ch), BF16),
        compiler_params=_cparams("arbitrary", "arbitrary"),
        name="conv_silu",
    )(p3d, conv_w, conv_b.reshape(1, -1), smat)


LOG2E = 1.4426950408889634
SSD_SUB = 8


def _head_expand_matrix():
    e = np.zeros((LANES, N_HEADS * HEAD_DIM), np.float32)
    for h in range(N_HEADS):
        e[h, h * HEAD_DIM:(h + 1) * HEAD_DIM] = 1.0
    return jnp.asarray(e, BF16)


def _ssd_chunk_terms(dt_raw, dt_bias, a_log, expand, tri, fwd):
    q = CHUNK
    dt = _softplus(dt_raw + dt_bias)
    da = dt * (-jnp.exp(a_log))
    hi = da.astype(BF16)
    r1 = da - hi.astype(F32)
    mid = r1.astype(BF16)
    lo = (r1 - mid.astype(F32)).astype(BF16)
    ones_tri = jnp.where(tri, 1.0, 0.0).astype(BF16)
    cs3 = jnp.dot(ones_tri, jnp.concatenate([hi, mid, lo], axis=1), preferred_element_type=F32)
    cs = cs3[:, :LANES] + cs3[:, LANES:2 * LANES] + cs3[:, 2 * LANES:]
    tot = jnp.where(fwd, cs[q - 1:q, :], cs[0:1, :])
    dte = dt * jnp.exp(tot - cs)
    dec = jnp.exp(tot)
    dec_hi = dec.astype(BF16)
    dec_lo = (dec - dec_hi.astype(F32)).astype(BF16)
    dec2 = jnp.concatenate([dec_hi, dec_lo, jnp.zeros((6, LANES), BF16)], axis=0)
    stacked = jnp.concatenate([jnp.exp(cs).astype(BF16), dte.astype(BF16), dec2], axis=0)
    spread = jnp.dot(stacked, expand, preferred_element_type=F32)
    ecs_e = spread[:q]
    dte_e = spread[q:2 * q].astype(BF16)
    dec_e = jnp.broadcast_to(spread[2 * q:2 * q + 1] + spread[2 * q + 1:2 * q + 2], (8, spread.shape[1]))
    cs2 = cs * LOG2E
    return cs2, cs2.T, dt.T, ecs_e, dte_e, dec_e


def _ssd_kernel(*refs, with_y, nc, n_sub):
    if with_y:
        (x_ref, b_ref, c_ref, dt_ref, dtn_ref, dtb_ref, alog_ref, exp_ref, h0_ref, y_ref,
         s_scr, cs_scr, dee_scr, dce_scr, ecs_scr) = refs
    else:
        (x_ref, b_ref, dt_ref, dtn_ref, dtb_ref, alog_ref, exp_ref, h0_ref, hout_ref,
         s_scr, cs_scr, dee_scr, dce_scr) = refs
    q = CHUNK
    gw = HEADS_PER_GROUP * HEAD_DIM
    d = pl.program_id(1)
    c = pl.program_id(2)
    row = lax.broadcasted_iota(jnp.int32, (q, q), 0)
    col = lax.broadcasted_iota(jnp.int32, (q, q), 1)
    fwd = d == 0
    tri = jnp.where(fwd, row - col, col - row) >= 0
    first_half = col < HEAD_DIM

    def store_terms(slot, terms):
        cs2, cs2_t, dt_t, ecs_e, dte_e, dec_e = terms
        cs_scr[slot, 0] = cs2
        cs_scr[slot, 1] = cs2_t
        cs_scr[slot, 2] = dt_t
        dee_scr[slot] = dte_e
        dce_scr[slot] = dec_e
        if with_y:
            ecs_scr[slot] = ecs_e

    def terms_of(ref, sub):
        rows = pl.ds(pl.multiple_of(sub * q, q), q)
        return _ssd_chunk_terms(ref[rows, :], dtb_ref[...], alog_ref[...], exp_ref[...], tri, fwd)

    def sub_of(j):
        return jnp.where(fwd, j, n_sub - 1 - j)

    @pl.when(c == 0)
    def _():
        s_scr[...] = h0_ref[...]
        store_terms(0, terms_of(dt_ref, sub_of(0)))

    for j in range(n_sub):
        slot = j % 2
        rows = pl.ds(pl.multiple_of(sub_of(j) * q, q), q)
        cs2 = cs_scr[slot, 0]
        cs2_t = cs_scr[slot, 1]
        dt_t = cs_scr[slot, 2]
        if j + 1 < n_sub:
            store_terms(1 - slot, terms_of(dt_ref, sub_of(j + 1)))
        else:
            store_terms(1 - slot, terms_of(dtn_ref, sub_of(0)))

        for g in range(N_GROUPS):
            cols = slice(g * gw, (g + 1) * gw)
            bg = b_ref[rows, g * D_STATE:(g + 1) * D_STATE]
            xg = x_ref[rows, cols]
            s_old = s_scr[g]
            if with_y:
                cg = c_ref[rows, g * D_STATE:(g + 1) * D_STATE]
                cb = lax.dot_general(cg, bg, (((1,), (1,)), ((), ())), preferred_element_type=F32)
                y_off = jnp.dot(cg, s_old.astype(BF16), preferred_element_type=F32)
                for k in range(HEADS_PER_GROUP // 2):
                    pair = g * (HEADS_PER_GROUP // 2) + k
                    lanes = slice(pair * LANES, (pair + 1) * LANES)
                    xp = xg[:, k * LANES:(k + 1) * LANES]
                    xz = jnp.zeros_like(xp)
                    lhs = []
                    for hh in (2 * pair, 2 * pair + 1):
                        colb = jnp.broadcast_to(cs2[:, hh:hh + 1], (q, q))
                        rowb = cs2_t[hh:hh + 1, :]
                        decay = jnp.exp2(jnp.where(tri, colb - rowb, -jnp.inf))
                        lhs.append((cb * decay * dt_t[hh:hh + 1, :]).astype(BF16))
                    y_diag = jnp.dot(jnp.concatenate(lhs, axis=1),
                                     jnp.concatenate([jnp.where(first_half, xp, xz), jnp.where(first_half, xz, xp)],
                                                     axis=0),
                                     preferred_element_type=F32)
                    y = y_diag + ecs_scr[slot, :, lanes] * y_off[:, k * LANES:(k + 1) * LANES]
                    y_ref[rows, lanes] = y.astype(y_ref.dtype)
            xdte = xg * dee_scr[slot, :, cols]
            upd = lax.dot_general(bg, xdte, (((0,), (0,)), ((), ())), preferred_element_type=F32)
            s_scr[g] = s_old * dce_scr[slot, 0:1, cols] + upd

    if not with_y:
        @pl.when(c == nc - 1)
        def _():
            hout_ref[...] = s_scr[...]


def _ssd(xbc, dt_raw, dt_bias2, a_log2, h0, with_y):
    bsz, n, _ = xbc.shape
    n_sub = min(SSD_SUB, n // CHUNK)
    assert n_sub % 2 == 0 and n % (n_sub * CHUNK) == 0
    blk = n_sub * CHUNK
    nc = n // blk
    d_inner = N_HEADS * HEAD_DIM
    gw = HEADS_PER_GROUP * HEAD_DIM
    x_blk = d_inner // BC_WIDTH

    def ceff(dd, cc):
        return cc + dd * (nc - 1 - 2 * cc)

    def cnext(dd, cc):
        return ceff(dd, jnp.minimum(cc + 1, nc - 1))

    in_specs = [pl.BlockSpec((None, blk, d_inner), lambda b, dd, cc: (b, ceff(dd, cc), 0)),
                pl.BlockSpec((None, blk, BC_WIDTH), lambda b, dd, cc: (b, ceff(dd, cc), x_blk + dd))]
    args = [xbc, xbc]
    if with_y:
        in_specs.append(pl.BlockSpec((None, blk, BC_WIDTH), lambda b, dd, cc: (b, ceff(dd, cc), x_blk + 2 + dd)))
        args.append(xbc)
    in_specs += [pl.BlockSpec((None, blk, LANES), lambda b, dd, cc: (b, ceff(dd, cc), dd)),
                 pl.BlockSpec((None, blk, LANES), lambda b, dd, cc: (b, cnext(dd, cc), dd)),
                 pl.BlockSpec((None, 1, LANES), lambda b, dd, cc: (dd, 0, 0)),
                 pl.BlockSpec((None, 1, LANES), lambda b, dd, cc: (dd, 0, 0)),
                 pl.BlockSpec((LANES, d_inner), lambda b, dd, cc: (0, 0)),
                 pl.BlockSpec((None, None, N_GROUPS, D_STATE, gw), lambda b, dd, cc: (b, dd, 0, 0, 0))]
    args += [dt_raw, dt_raw, dt_bias2, a_log2, _head_expand_matrix(), h0]
    scratch = [pltpu.VMEM((N_GROUPS, D_STATE, gw), F32),
               pltpu.VMEM((2, 3, CHUNK, LANES), F32),
               pltpu.VMEM((2, CHUNK, d_inner), BF16),
               pltpu.VMEM((2, 8, d_inner), F32)]
    if with_y:
        out_specs = pl.BlockSpec((None, None, blk, d_inner), lambda b, dd, cc: (dd, b, ceff(dd, cc), 0))
        out_shape = jax.ShapeDtypeStruct((2, bsz, n, d_inner), BF16)
        scratch += [pltpu.VMEM((2, CHUNK, d_inner), F32)]
    else:
        out_specs = pl.BlockSpec((None, None, N_GROUPS, D_STATE, gw), lambda b, dd, cc: (b, dd, 0, 0, 0))
        out_shape = jax.ShapeDtypeStruct((bsz, 2, N_GROUPS, D_STATE, gw), F32)
    return pl.pallas_call(
        functools.partial(_ssd_kernel, with_y=with_y, nc=nc, n_sub=n_sub),
        grid=(bsz, 2, nc),
        in_specs=in_specs,
        out_specs=out_specs,
        out_shape=out_shape,
        scratch_shapes=scratch,
        compiler_params=_cparams("arbitrary", "arbitrary", "arbitrary"),
        name="ssd_scan" if with_y else "ssd_ctx_state",
    )(*args)


def _dft_mats(n):
    k = np.arange(n)
    ang = 2.0 * np.pi * ((k[:, None] * k[None, :]) % n) / n
    return np.cos(ang), np.sin(ang)


def _strided_rows(scr, t, start, count, stride, width):
    return jnp.concatenate([scr[t, lb, pl.ds(start, count, stride=stride), :] for lb in range(width // LANES)],
                           axis=1)


def _fnet_stage1_kernel(u_ref, chan_ref, m1_ref, twc_ref, tws_ref, o_ref, zs_scr, *, tcs):
    r = u_ref.shape[0]
    width = u_ref.shape[-1]
    u2 = u_ref[...].reshape(r * tcs, width)
    chan = chan_ref[...]
    for g in range(F_GROUPS):
        pq = jnp.dot(u2[:, g * F_GROUP_DIM:(g + 1) * F_GROUP_DIM], chan, preferred_element_type=F32)
        zs_scr[0, g] = pq[:, :F_GROUP_DIM]
        zs_scr[1, g] = pq[:, F_GROUP_DIM:]
    m1 = m1_ref[...]
    for j in range(tcs):
        zz = jnp.concatenate([_strided_rows(zs_scr, 0, j, r, tcs, width),
                              _strided_rows(zs_scr, 1, j, r, tcs, width)], axis=0).astype(BF16)
        u = jnp.dot(m1, zz, preferred_element_type=F32)
        ur = u[:r]
        ui = u[r:]
        tc = twc_ref[j]
        ts = tws_ref[j]
        o_ref[0, j] = (ur * tc + ui * ts).astype(o_ref.dtype)
        o_ref[1, j] = (ui * tc - ur * ts).astype(o_ref.dtype)


def _fnet_stage2_kernel(u_ref, k2_ref, o_ref):
    c, tbs, width = o_ref.shape
    uu = jnp.concatenate([u_ref[0].reshape(c * tbs, width), u_ref[1].reshape(c * tbs, width)], axis=0)
    out = jnp.dot(k2_ref[...], uu, preferred_element_type=F32)
    o_ref[...] = out.reshape(c, tbs, width).astype(o_ref.dtype)


def _fnet_mix(p2d, col_blk, bsz, n):
    m = p2d.shape[0]
    width = F_GROUPS * F_GROUP_DIM
    cgrid = GRID_W
    rgrid = n // cgrid
    scale = 1.0 / math.sqrt(n * F_GROUP_DIM)

    cc, sc = _dft_mats(F_GROUP_DIM)
    chan = jnp.asarray(np.concatenate([cc, sc], axis=1) * scale, BF16)

    cr, sr = _dft_mats(rgrid)
    m1 = jnp.asarray(np.block([[cr, -sr], [-sr, -cr]]), BF16)
    bb = np.arange(rgrid)[None, :]
    ci = np.arange(cgrid)[:, None]
    ang = 2.0 * np.pi * ((ci * bb) % n) / n
    twc = jnp.asarray(np.cos(ang)[:, :, None], F32)
    tws = jnp.asarray(np.sin(ang)[:, :, None], F32)
    tcs = 16
    p4 = p2d.reshape(bsz, rgrid, cgrid, p2d.shape[-1])
    u5 = pl.pallas_call(
        functools.partial(_fnet_stage1_kernel, tcs=tcs),
        grid=(bsz, cgrid // tcs),
        in_specs=[pl.BlockSpec((None, rgrid, tcs, width), lambda b, j: (b, 0, j, col_blk)),
                  pl.BlockSpec((F_GROUP_DIM, 2 * F_GROUP_DIM), lambda b, j: (0, 0)),
                  pl.BlockSpec((2 * rgrid, 2 * rgrid), lambda b, j: (0, 0)),
                  pl.BlockSpec((tcs, rgrid, 1), lambda b, j: (j, 0, 0)),
                  pl.BlockSpec((tcs, rgrid, 1), lambda b, j: (j, 0, 0))],
        out_specs=pl.BlockSpec((2, None, tcs, rgrid, width), lambda b, j: (0, b, j, 0, 0)),
        out_shape=jax.ShapeDtypeStruct((2, bsz, cgrid, rgrid, width), BF16),
        scratch_shapes=[pltpu.VMEM((2, width // LANES, rgrid * tcs, LANES), F32)],
        compiler_params=_cparams("arbitrary", "arbitrary"),
        name="fnet_stage1",
    )(p4, chan, m1, twc, tws)

    cc2, sc2 = _dft_mats(cgrid)
    tbs = min(16, rgrid)
    eye = np.eye(tbs)
    k2 = jnp.asarray(np.concatenate([np.kron(cc2, eye), np.kron(sc2, eye)], axis=1), BF16)
    out = pl.pallas_call(
        _fnet_stage2_kernel,
        grid=(bsz, rgrid // tbs),
        in_specs=[pl.BlockSpec((2, None, cgrid, tbs, width), lambda b, j: (0, b, 0, j, 0)),
                  pl.BlockSpec(k2.shape, lambda b, j: (0, 0))],
        out_specs=pl.BlockSpec((None, cgrid, tbs, width), lambda b, j: (b, 0, j, 0)),
        out_shape=jax.ShapeDtypeStruct((bsz, cgrid, rgrid, width), BF16),
        compiler_params=_cparams("arbitrary", "arbitrary"),
        name="fnet_stage2",
    )(u5, k2)
    return out.reshape(m, width)


def _tail_kernel(yf_ref, yb_ref, xs_ref, z_ref, ff_ref, gf_ref, gs_ref, x_ref, g1_ref, sc2_ref, sh2_ref,
                 dsk_ref, snw_ref, wso_ref, wfo_ref, wo_ref, n2w_ref, wr_ref, br_ref,
                 x1_ref, h2_ref, ti_ref, tw_ref, cnt_ref, carry_scr):
    y = (yf_ref[0] + yb_ref[0]).astype(F32) + dsk_ref[...] * xs_ref[...].astype(F32)
    g = y * _silu(z_ref[...]).astype(F32)
    inv = lax.rsqrt(jnp.mean(g * g, axis=-1, keepdims=True) + EPS)
    gn = (g * inv * snw_ref[...]).astype(BF16)
    y_ssd = jnp.dot(gn, wso_ref[...], preferred_element_type=F32)
    y_four = jnp.dot(ff_ref[...], wfo_ref[...], preferred_element_type=F32)
    t = _sigmoid(gf_ref[...]) * y_four.astype(BF16) + _sigmoid(gs_ref[...]) * y_ssd.astype(BF16)
    mix = jnp.dot(t, wo_ref[...], preferred_element_type=F32)
    x1 = x_ref[...] + g1_ref[...] * mix
    x1_ref[...] = x1
    inv2 = lax.rsqrt(jnp.mean(x1 * x1, axis=-1, keepdims=True) + EPS)
    h2 = (x1 * inv2 * n2w_ref[...]) * (1.0 + sc2_ref[...]) + sh2_ref[...]
    h2_ref[...] = _pack_bf16_pair(h2)
    h_hi = h2.astype(BF16)
    h_lo = (h2 - h_hi.astype(F32)).astype(BF16)
    wr = wr_ref[...]
    w_hi = wr.astype(BF16)
    w_lo = (wr - w_hi.astype(F32)).astype(BF16)
    logits = (jnp.dot(h_hi, w_hi, preferred_element_type=F32)
              + jnp.dot(h_hi, w_lo, preferred_element_type=F32)
              + jnp.dot(h_lo, w_hi, preferred_element_type=F32)) + br_ref[...]
    tm = logits.shape[0]
    lane = lax.broadcasted_iota(jnp.int32, (tm, LANES), 1)
    vals, idxs = [], []
    cur = logits
    for _ in range(TOP_K):
        mx = jnp.max(cur, axis=-1, keepdims=True)
        ix = jnp.min(jnp.where(cur == mx, lane, LANES), axis=-1, keepdims=True)
        vals.append(mx)
        idxs.append(ix)
        cur = jnp.where(lane == ix, NEG_BIG * 2.0, cur)
    es = [jnp.exp(v - vals[0]) for v in vals]
    den = es[0] + es[1] + es[2] + es[3]
    ti = jnp.zeros((tm, LANES), jnp.int32)
    tw = jnp.zeros((tm, LANES), F32)
    for k in range(TOP_K):
        ti = jnp.where(lane == k, idxs[k], ti)
        tw = jnp.where(lane == k, es[k] / den, tw)
    tw_ref[...] = tw

    @pl.when(pl.program_id(0) == 0)
    def _():
        carry_scr[...] = jnp.zeros(carry_scr.shape, F32)

    onehots = [jnp.where(lane == ix, 1.0, 0.0) for ix in idxs]
    cnt = onehots[0] + onehots[1] + onehots[2] + onehots[3]
    r_i = lax.broadcasted_iota(jnp.int32, (tm, tm), 0)
    c_i = lax.broadcasted_iota(jnp.int32, (tm, tm), 1)
    earlier = jnp.where(r_i > c_i, 1.0, 0.0).astype(BF16)
    before = jnp.dot(earlier, cnt.astype(BF16), preferred_element_type=F32) + carry_scr[0:1, :]
    route = ti
    for k in range(TOP_K):
        rank_k = jnp.sum(onehots[k] * before, axis=-1, keepdims=True)
        route = jnp.where(lane == TOP_K + k, rank_k.astype(jnp.int32), route)
    ti_ref[...] = route.T[:2 * TOP_K, :]
    total = carry_scr[...] + jnp.sum(cnt, axis=0, keepdims=True)
    carry_scr[...] = total
    cnt_ref[...] = total


def _tail(y2, xbc2d, p2d, ff, x2d, g1, sc2, sh2, dsk, snw, wso, wfo, wo, n2w, wr, br, rows_per_mod, tm):
    m, d = x2d.shape
    di = y2.shape[-1]
    zb = (2 * di) // di
    fb = (3 * di) // d
    row = lambda i: (i, 0)
    modrow = lambda i: ((i * tm) // rows_per_mod, 0, 0)
    const = lambda i: (0, 0)
    in_specs = [pl.BlockSpec((1, tm, di), lambda i: (0, i, 0)),
                pl.BlockSpec((1, tm, di), lambda i: (1, i, 0)),
                pl.BlockSpec((tm, di), row),
                pl.BlockSpec((tm, di), lambda i: (i, zb)),
                pl.BlockSpec((tm, d), row),
                pl.BlockSpec((tm, d), lambda i: (i, fb + 1)),
                pl.BlockSpec((tm, d), lambda i: (i, fb + 2)),
                pl.BlockSpec((tm, d), row),
                pl.BlockSpec((None, 1, d), modrow),
                pl.BlockSpec((None, 1, d), modrow),
                pl.BlockSpec((None, 1, d), modrow),
                pl.BlockSpec((1, di), const),
                pl.BlockSpec((1, di), const),
                pl.BlockSpec((di, d), const),
                pl.BlockSpec((d, d), const),
                pl.BlockSpec((d, d), const),
                pl.BlockSpec((1, d), const),
                pl.BlockSpec((d, LANES), const),
                pl.BlockSpec((1, LANES), const)]
    tcol = lambda i: (0, i)
    out_specs = [pl.BlockSpec((tm, d), row), pl.BlockSpec((tm, d // 2), row),
                 pl.BlockSpec((2 * TOP_K, tm), tcol), pl.BlockSpec((tm, LANES), row),
                 pl.BlockSpec((8, LANES), const)]
    out_shape = [jax.ShapeDtypeStruct((m, d), F32), jax.ShapeDtypeStruct((m, d // 2), jnp.uint32),
                 jax.ShapeDtypeStruct((2 * TOP_K, m), jnp.int32), jax.ShapeDtypeStruct((m, LANES), F32),
                 jax.ShapeDtypeStruct((8, LANES), F32)]
    return pl.pallas_call(
        _tail_kernel,
        grid=(m // tm,),
        in_specs=in_specs,
        out_specs=out_specs,
        out_shape=out_shape,
        scratch_shapes=[pltpu.VMEM((8, LANES), F32)],
        compiler_params=_cparams("arbitrary"),
        name="tail",
    )(y2, y2, xbc2d, p2d, ff, p2d, p2d, x2d, g1, sc2, sh2, dsk, snw, wso, wfo, wo, n2w, wr, br)


def _gather_rows(src, idx):
    n_out = idx.shape[0]
    width = src.shape[1]
    win = SC_GATHER_WINDOW
    info = plsc.get_sparse_core_info()
    n_cores, n_workers = info.num_cores, info.num_cores * info.num_subcores
    per_worker = n_out // n_workers
    assert per_worker * n_workers == n_out and per_worker % (2 * win) == 0
    mesh = plsc.VectorSubcoreMesh(core_axis_name="core", subcore_axis_name="subcore")

    @functools.partial(
        pl.kernel, out_type=jax.ShapeDtypeStruct((n_out, width), src.dtype), mesh=mesh,
        scratch_types=[pltpu.VMEM((win,), jnp.int32), pltpu.VMEM((win,), jnp.int32),
                       pltpu.VMEM((win, width), src.dtype), pltpu.VMEM((win, width), src.dtype),
                       pltpu.SemaphoreType.DMA, pltpu.SemaphoreType.DMA,
                       pltpu.SemaphoreType.DMA, pltpu.SemaphoreType.DMA],
        name="gather_rows")
    def gather(src_hbm, idx_hbm, out_hbm, idx_a, idx_b, rows_a, rows_b, gsem_a, gsem_b, ssem_a, ssem_b):
        worker = lax.axis_index("subcore") * n_cores + lax.axis_index("core")
        base = worker * per_worker

        @pl.loop(0, per_worker, step=2 * win)
        def _(off):
            pltpu.sync_copy(idx_hbm.at[pl.ds(base + off, win)], idx_a)
            ga = pltpu.async_copy(src_hbm.at[idx_a], rows_a, gsem_a)
            pltpu.sync_copy(idx_hbm.at[pl.ds(base + off + win, win)], idx_b)
            gb = pltpu.async_copy(src_hbm.at[idx_b], rows_b, gsem_b)
            ga.wait()
            sa = pltpu.async_copy(rows_a, out_hbm.at[pl.ds(base + off, win)], ssem_a)
            gb.wait()
            sb = pltpu.async_copy(rows_b, out_hbm.at[pl.ds(base + off + win, win)], ssem_b)
            sa.wait()
            sb.wait()

    return gather(src, idx)


def _scatter_rows(src, dest, n_rows):
    m, width = src.shape
    win = SC_GATHER_WINDOW
    info = plsc.get_sparse_core_info()
    n_cores, n_workers = info.num_cores, info.num_cores * info.num_subcores
    per_worker = m // n_workers
    assert per_worker * n_workers == m and per_worker % win == 0
    mesh = plsc.VectorSubcoreMesh(core_axis_name="core", subcore_axis_name="subcore")

    @functools.partial(
        pl.kernel, out_type=jax.ShapeDtypeStruct((n_rows, width), src.dtype), mesh=mesh,
        scratch_types=[pltpu.VMEM((win, width), src.dtype)]
        + [pltpu.VMEM((win,), jnp.int32)] * TOP_K + [pltpu.SemaphoreType.DMA] * (TOP_K + 1),
        name="scatter_rows")
    def scatter(src_hbm, dest_hbm, out_hbm, rows_v, *rest):
        idx_v, sems, row_sem = rest[:TOP_K], rest[TOP_K:2 * TOP_K], rest[2 * TOP_K]
        worker = lax.axis_index("subcore") * n_cores + lax.axis_index("core")
        base = worker * per_worker

        @pl.loop(0, per_worker, step=win)
        def _(off):
            t0 = base + off
            loads = [pltpu.async_copy(src_hbm.at[pl.ds(t0, win)], rows_v, row_sem)]
            loads += [pltpu.async_copy(dest_hbm.at[pl.ds(k * m + t0, win)], idx_v[k], sems[k]) for k in range(TOP_K)]
            for ld in loads:
                ld.wait()
            copies = [pltpu.async_copy(rows_v, out_hbm.at[idx_v[k]], sems[k]) for k in range(TOP_K)]
            for cp in copies:
                cp.wait()

    return scatter(src, dest)


def _expert_kernel(te_ref, nu_ref, tv_ref, x_ref, wgu_ref, bgu_ref, wd_ref, bd_ref, o_ref, wgu_scr, wd_scr):
    i = pl.program_id(0)
    used = i < nu_ref[0]
    new_expert = (i == 0) | (te_ref[i] != te_ref[jnp.maximum(i - 1, 0)])

    @pl.when(used & new_expert)
    def _():
        wgu_scr[...] = wgu_ref[...].astype(BF16)
        wd_scr[...] = wd_ref[...].astype(BF16)

    @pl.when(used)
    def _():
        dff = wd_ref.shape[0]
        rows = lax.broadcasted_iota(jnp.int32, x_ref.shape, 0)
        xa, xb = _unpack_bf16_pair(jnp.where(rows < tv_ref[i], x_ref[...], jnp.uint32(0)))
        x = jnp.concatenate([xa.astype(BF16), xb.astype(BF16)], axis=1)
        gu = jnp.dot(x, wgu_scr[...], preferred_element_type=F32) + bgu_ref[...]
        gate = jnp.minimum(gu[:, :dff], SWIGLU_LIMIT)
        up = jnp.clip(gu[:, dff:], -SWIGLU_LIMIT, SWIGLU_LIMIT)
        act = (up + 1.0) * gate * _sigmoid(SWIGLU_ALPHA * gate)
        y = jnp.dot(act.astype(BF16), wd_scr[...], preferred_element_type=F32) + bd_ref[...]
        o_ref[...] = _pack_bf16_pair(y)

    @pl.when(jnp.logical_not(used))
    def _():
        o_ref[...] = jnp.zeros(o_ref.shape, o_ref.dtype)


def _experts(xs, tile_expert, n_used, tile_valid, wgu, bgu, wd, bd):
    rows, dh = xs.shape
    d = 2 * dh
    tm = MOE_TILE
    n_tiles = rows // tm
    dff2 = wgu.shape[-1]

    def tile(i, te, nu, tv):
        return (jnp.minimum(i, nu[0] - 1), 0)

    def wsel(i, te, nu, tv):
        return (te[jnp.minimum(i, nu[0] - 1)], 0, 0)

    grid_spec = pltpu.PrefetchScalarGridSpec(
        num_scalar_prefetch=3,
        grid=(n_tiles,),
        in_specs=[pl.BlockSpec((tm, dh), tile),
                  pl.BlockSpec((None, d, dff2), wsel),
                  pl.BlockSpec((None, 1, dff2), wsel),
                  pl.BlockSpec((None, dff2 // 2, d), wsel),
                  pl.BlockSpec((None, 1, d), wsel)],
        out_specs=pl.BlockSpec((tm, dh), lambda i, te, nu, tv: (i, 0)),
        scratch_shapes=[pltpu.VMEM((d, dff2), BF16), pltpu.VMEM((dff2 // 2, d), BF16)],
    )
    return pl.pallas_call(
        _expert_kernel,
        grid_spec=grid_spec,
        out_shape=jax.ShapeDtypeStruct((rows, dh), jnp.uint32),
        compiler_params=_cparams("arbitrary"),
        name="experts",
    )(tile_expert, n_used, tile_valid, xs, wgu, bgu, wd, bd)


def _final_kernel(x1_ref, ya_ref, yb_ref, yc_ref, yd_ref, tw_ref, g2_ref, fw_ref, o_ref):
    d = x1_ref.shape[-1]
    half = d // 2
    tw = tw_ref[...]
    acc_hi = jnp.zeros((x1_ref.shape[0], half), F32)
    acc_lo = jnp.zeros((x1_ref.shape[0], half), F32)
    for k, y_ref in enumerate((ya_ref, yb_ref, yc_ref, yd_ref)):
        y_hi, y_lo = _unpack_bf16_pair(y_ref[...])
        acc_hi = acc_hi + tw[:, k:k + 1] * y_hi
        acc_lo = acc_lo + tw[:, k:k + 1] * y_lo
    x_hi = x1_ref[:, :half] + g2_ref[:, :half] * acc_hi
    x_lo = x1_ref[:, half:] + g2_ref[:, half:] * acc_lo
    ms = (jnp.sum(x_hi * x_hi, axis=-1, keepdims=True) + jnp.sum(x_lo * x_lo, axis=-1, keepdims=True)) / d
    inv = lax.rsqrt(ms + EPS)
    o_ref[:, :half] = x_hi * inv * fw_ref[:, :half]
    o_ref[:, half:] = x_lo * inv * fw_ref[:, half:]


def _final(x1, y4, tw, g2, fw, rows_per_mod, tm):
    m, d = x1.shape
    return pl.pallas_call(
        _final_kernel,
        grid=(m // tm,),
        in_specs=[pl.BlockSpec((tm, d), lambda i: (i, 0))]
        + [pl.BlockSpec((None, tm, d // 2), functools.partial(lambda k, i: (k, i, 0), k)) for k in range(TOP_K)]
        + [pl.BlockSpec((tm, LANES), lambda i: (i, 0)),
           pl.BlockSpec((None, 1, d), lambda i: ((i * tm) // rows_per_mod, 0, 0)),
           pl.BlockSpec((1, d), lambda i: (0, 0))],
        out_specs=pl.BlockSpec((tm, d), lambda i: (i, 0)),
        out_shape=jax.ShapeDtypeStruct((m, d), F32),
        compiler_params=_cparams("arbitrary"),
        name="final",
    )(x1, y4, y4, y4, y4, tw, g2, fw.reshape(1, d))


def _dispatch_plan(e, rank, counts, tm):
    n_assign = e.size
    padded = (counts + tm - 1) // tm * tm
    pad_end = jnp.cumsum(padded)
    pad_start = pad_end - padded
    dest = rank.astype(jnp.int32)
    for j in range(N_EXPERTS - 1):
        dest = dest + jnp.where(e > j, padded[j], 0).astype(jnp.int32)
    n_tiles = n_assign // tm + N_EXPERTS
    tile_start = jnp.arange(n_tiles, dtype=jnp.int32) * tm
    tile_expert = jnp.minimum(jnp.sum(pad_end[None, :] <= tile_start[:, None], axis=1), N_EXPERTS - 1).astype(jnp.int32)
    n_used = (pad_end[-1] // tm).astype(jnp.int32).reshape(1)
    tile_valid = jnp.clip(pad_start[tile_expert] + counts[tile_expert] - tile_start, 0, tm).astype(jnp.int32)
    return dest, tile_expert, n_used, tile_valid


def kernel(x, c, ctx, c_ctx, w_mod, b_mod, norm1_w, norm2_w, w_in, conv_w, conv_b, dt_bias, a_log, d_skip,
           ssd_norm_w, w_ssd_out, w_four_out, w_o, w_router, b_router, w_gate_up, b_gate_up, w_down, b_down,
           final_norm_w):
    bsz, n, d = x.shape
    n_ctx = ctx.shape[1]
    depth = w_mod.shape[0]
    assert depth == 1, "a stacked model would also need the context stream's residual update"
    d_inner = N_HEADS * HEAD_DIM
    off_dt = d_inner + 2 * BC_WIDTH
    m = bsz * n
    x2d = x.reshape(m, d)
    tm = min(512, n)

    for layer in range(depth):
        rows = -(-(bsz + 1) // 8) * 8
        c_rows = jnp.zeros((rows, d), F32).at[:bsz].set(c).at[bsz].set(c_ctx)
        mod = _modulation(c_rows, w_mod[layer], b_mod[layer])
        mods = [mod[:bsz, i * d:(i + 1) * d].reshape(bsz, 1, d) for i in range(N_MOD)]
        sh1, sc1, g1, sh2, sc2, g2 = mods
        sh1_c = mod[bsz:bsz + 1, :d].reshape(1, 1, d)
        sc1_c = mod[bsz:bsz + 1, d:2 * d].reshape(1, 1, d)

        wl = w_in[layer]
        w_head = wl[:, :off_dt].astype(BF16)
        w_rest = wl[:, off_dt + 2 * N_HEADS:].astype(BF16)
        w_dt = jnp.zeros((d, 2 * LANES), F32)
        w_dt = w_dt.at[:, :N_HEADS].set(wl[:, off_dt:off_dt + N_HEADS])
        w_dt = w_dt.at[:, LANES:LANES + N_HEADS].set(wl[:, off_dt + N_HEADS:off_dt + 2 * N_HEADS]).astype(BF16)
        pad = jnp.zeros((2, 1, LANES - N_HEADS), F32)
        dt_bias2 = jnp.concatenate([dt_bias[layer].reshape(2, 1, N_HEADS), pad], axis=-1)
        a_log2 = jnp.concatenate([a_log[layer].reshape(2, 1, N_HEADS), pad], axis=-1)

        pc, dtc = _in_proj(ctx.reshape(bsz * n_ctx, d), norm1_w[layer], sc1_c, sh1_c, [w_head], w_dt,
                           bsz * n_ctx, min(512, n_ctx))
        xbc_c = _conv_silu(pc.reshape(bsz, n_ctx, off_dt), conv_w[layer][:, :off_dt], conv_b[layer][:off_dt], off_dt)
        h0 = jnp.zeros((bsz, 2, N_GROUPS, D_STATE, HEADS_PER_GROUP * HEAD_DIM), F32)
        h_ctx = _ssd(xbc_c, dtc.reshape(bsz, n_ctx, 2 * LANES), dt_bias2, a_log2, h0, with_y=False)

        p, dtl = _in_proj(x2d, norm1_w[layer], sc1, sh1, [w_head, w_rest], w_dt, n, tm)
        n_conv = off_dt + 2 * BC_WIDTH
        xbc = _conv_silu(p.reshape(bsz, n, -1), conv_w[layer], conv_b[layer], n_conv)
        y2 = _ssd(xbc, dtl.reshape(bsz, n, 2 * LANES), dt_bias2, a_log2, h_ctx, with_y=True)
        ff = _fnet_mix(p, (n_conv + d_inner) // d, bsz, n)

        dsk = jnp.repeat(d_skip[layer].astype(F32), HEAD_DIM).reshape(1, d_inner)
        wr = jnp.zeros((d, LANES), F32).at[:, :N_EXPERTS].set(w_router[layer])
        br = jnp.full((1, LANES), NEG_BIG, F32).at[0, :N_EXPERTS].set(b_router[layer])
        x1, h2, route, tw, cnt = _tail(
            y2.reshape(2, m, d_inner), xbc.reshape(m, n_conv), p, ff, x2d, g1, sc2, sh2, dsk,
            ssd_norm_w[layer].reshape(1, d_inner), w_ssd_out[layer].astype(BF16), w_four_out[layer].astype(BF16),
            w_o[layer].astype(BF16), norm2_w[layer].reshape(1, d), wr, br, n, min(TAIL_ROWS, n))

        counts = cnt[0, :N_EXPERTS].astype(jnp.int32)
        dest, tile_expert, n_used, tile_valid = _dispatch_plan(route[:TOP_K], route[TOP_K:], counts, MOE_TILE)
        dest_km = dest.reshape(-1)
        xs = _scatter_rows(h2, dest_km, tile_expert.shape[0] * MOE_TILE)
        ys = _experts(xs, tile_expert, n_used, tile_valid, w_gate_up[layer],
                      b_gate_up[layer].reshape(N_EXPERTS, 1, -1), w_down[layer],
                      b_down[layer].reshape(N_EXPERTS, 1, -1))
        y4 = _gather_rows(ys, dest_km).reshape(TOP_K, m, d // 2)
        x2d = _final(x1, y4, tw, g2, final_norm_w, n, tm)
    return x2d.reshape(bsz, n, d)
```
